```python
import math
import jax, jax.numpy as jnp
from jax import lax
import numpy as np

D_MODEL = 1024
BATCH = 4
SEQ = 8192
DEPTH = 2

GRID_W = 64
CTX_LEN = 256
EPS = 1e-6

F_GROUPS = 4
F_GDIM = 64
F_WIDTH = F_GROUPS * F_GDIM
DA_HEADS = 6
DA_DIM = 32
DA_VDIM = 2 * DA_DIM
DA_WIDTH = DA_HEADS * DA_VDIM
M_HEADS = 4
M_DIM = 96
M_WIDTH = M_HEADS * M_DIM
M_CHUNK = 64
M_CONV = 3
MIX_WIDTH = F_WIDTH + DA_WIDTH + M_WIDTH
N_GATES = 4 * M_HEADS
Q_BLOCK = 128
ROPE_THETA = 10000.0
N_EXPERTS = 16
EC_CAPACITY = 2
D_FF_EXPERT = 2 * D_MODEL
ADA_CHUNKS = 6

OFF_F = 0
OFF_DQ = OFF_F + F_WIDTH
OFF_MO = OFF_DQ + 2 * DA_HEADS * DA_DIM
OFF_MQ = OFF_MO + M_WIDTH
OFF_MK = OFF_MQ + M_WIDTH
OFF_DK = OFF_MK + M_WIDTH
OFF_DV = OFF_DK + 2 * DA_HEADS * DA_DIM
OFF_MV = OFF_DV + DA_HEADS * DA_VDIM
OFF_G = OFF_MV + M_WIDTH
PROJ_WIDTH = OFF_G + N_GATES

kernel_name = 'hybrid_fourier_diffattn_mlstm_ecmoe_dit'


def rmsnorm(x, g):
    xf = x.astype(jnp.float32)
    r = lax.rsqrt(jnp.mean(xf * xf, axis=-1, keepdims=True) + EPS)
    return (xf * r).astype(x.dtype) * g


def adaln(cvec, w, b):
    return jnp.split(jax.nn.silu(cvec) @ w + b, ADA_CHUNKS, axis=-1)


def cols(p, off, width, base=0):
    return p[..., off - base: off - base + width]


def axial_rope_tables(rows):
    t_row = jnp.repeat(jnp.arange(rows), GRID_W)
    t_col = jnp.tile(jnp.arange(GRID_W), rows)
    nf = DA_DIM // 4
    inv = ROPE_THETA ** (-jnp.arange(nf, dtype=jnp.float32) / nf)
    ar = t_row[:, None].astype(jnp.float32) * inv
    ac = t_col[:, None].astype(jnp.float32) * inv
    ang = jnp.concatenate([ar, ar, ac, ac], axis=-1)
    return jnp.cos(ang), jnp.sin(ang)


def rope_2d(x, cos, sin):
    xs = x.reshape(x.shape[:-1] + (2, 2, DA_DIM // 4))
    rot = jnp.concatenate([-xs[..., 1:, :], xs[..., :1, :]], axis=-2).reshape(x.shape)
    c = cos.astype(x.dtype)[:, None, None, :]
    s = sin.astype(x.dtype)[:, None, None, :]
    return x * c + rot * s


def fourier_mix(p, w):
    B, N, _ = p.shape
    z = jnp.fft.fft2(p.reshape(B, N, F_GROUPS, F_GDIM).astype(jnp.float32), axes=(1, 3), norm='ortho').real
    return jnp.einsum('bngc,gcd->bngd', z.astype(p.dtype), w).reshape(B, N, F_WIDTH)


def diff_attend(q, k, v, lam):
    s = jnp.einsum('bqhcd,bkhcd->bhcqk', q, k).astype(jnp.float32) * (DA_DIM ** -0.5)
    a = jax.nn.softmax(s, axis=-1)
    a = a[:, :, 0] - lam * a[:, :, 1]
    return jnp.einsum('bhqk,bkhv->bqhv', a.astype(v.dtype), v)


def centred_conv(x, w, b):
    pad = M_CONV // 2
    n = x.shape[1]
    xp = jnp.pad(x, ((0, 0), (pad, pad), (0, 0)))
    out = b
    for j in range(M_CONV):
        out = out + xp[:, j:j + n] * w[j]
    return out


def heads(a):
    B, N, _ = a.shape
    return a.reshape(B, N, M_HEADS, M_DIM).transpose(0, 2, 1, 3).astype(jnp.float32)


def gate_logs(p, gate_b):
    B, N, _ = p.shape
    g = (p.astype(jnp.float32) + gate_b.astype(jnp.float32)).reshape(B, N, 2, 2, M_HEADS)
    g = jnp.transpose(g, (2, 3, 0, 4, 1))
    return g[0, 0], jax.nn.log_sigmoid(g[0, 1]), g[1, 0], jax.nn.log_sigmoid(g[1, 1])


def mlstm_zero_state(B):
    return (jnp.zeros((B, M_HEADS, M_DIM, M_DIM), jnp.float32),
            jnp.zeros((B, M_HEADS, M_DIM), jnp.float32),
            jnp.zeros((B, M_HEADS), jnp.float32))


def mlstm_state_update(state, k, v, li, lf):
    C, n, m = state
    b = jnp.cumsum(lf, axis=-1)
    b_last = b[..., -1]
    w_log = b_last[..., None] - b + li
    m_new = jnp.maximum(b_last + m, jnp.max(w_log, axis=-1))
    decay = jnp.exp(b_last + m - m_new)
    w = jnp.exp(w_log - m_new[..., None])
    C_new = decay[..., None, None] * C + jnp.einsum('bhlv,bhlk->bhvk', v * w[..., None], k)
    n_new = decay[..., None] * n + jnp.einsum('bhl,bhlk->bhk', w, k)
    return (C_new, n_new, m_new)


def mlstm_chunk(state, chunk):
    q, k, v, li, lf = chunk
    C, n, m = state
    L = q.shape[2]
    b = jnp.cumsum(lf, axis=-1)
    seen = jnp.tril(jnp.ones((L, L), dtype=bool))
    d_log = jnp.where(seen, b[..., :, None] - b[..., None, :] + li[..., None, :], -jnp.inf)
    inter_log = b + m[..., None]
    m_t = jnp.maximum(inter_log, jnp.max(d_log, axis=-1))
    w_inter = jnp.exp(inter_log - m_t)
    s = jnp.einsum('bhld,bhsd->bhls', q, k) * jnp.exp(d_log - m_t[..., None])
    num = w_inter[..., None] * jnp.einsum('bhvk,bhlk->bhlv', C, q) + jnp.einsum('bhls,bhsv->bhlv', s, v)
    den = w_inter * jnp.einsum('bhk,bhlk->bhl', n, q) + jnp.sum(s, axis=-1)
    h = num / jnp.maximum(jnp.abs(den), jnp.exp(-m_t))[..., None]
    return mlstm_state_update(state, k, v, li, lf), h


def mlstm_scan(q, k, v, li, lf, state):
    B, H, N, d = q.shape
    nc = N // M_CHUNK

    def to_chunks(a):
        return jnp.moveaxis(a.reshape((B, H, nc, M_CHUNK) + a.shape[3:]), 2, 0)

    state, h = lax.scan(mlstm_chunk, state, tuple(to_chunks(a) for a in (q, k, v, li, lf)))
    return jnp.moveaxis(h, 0, 2).reshape(B, H, N, d), state


def flip(a):
    return jnp.flip(a, axis=2)


def mlstm_out(h, o, g):
    B, H, N, d = h.shape
    hn = rmsnorm(jnp.transpose(h, (0, 2, 1, 3)).astype(o.dtype), g.reshape(H, d))
    return hn.reshape(B, N, M_WIDTH) * jax.nn.sigmoid(o)


def token_mixers(hl, hc, w_in, four_w, conv_w, conv_b, gate_b, m_norm_g, d_lam, d_norm_g,
                 lam_init, cos, sin, ctx_out):
    B, N, _ = hl.shape
    Bc, NC, _ = hc.shape
    base = 0 if ctx_out else OFF_MK
    pl = hl @ w_in
    pc = hc @ w_in[:, base:]
    qshape = (DA_HEADS, 2, DA_DIM)

    lam = (jnp.exp(jnp.sum(d_lam[0] * d_lam[1])) - jnp.exp(jnp.sum(d_lam[2] * d_lam[3]))
           + lam_init).astype(jnp.float32)
    q_l = rope_2d(pl[..., OFF_DQ:OFF_MO].reshape((B, N) + qshape), cos, sin)
    k_l = rope_2d(pl[..., OFF_DK:OFF_DV].reshape((B, N) + qshape), cos, sin)
    v_l = pl[..., OFF_DV:OFF_MV].reshape(B, N, DA_HEADS, DA_VDIM)
    k_c = cols(pc, OFF_DK, 2 * DA_HEADS * DA_DIM, base).reshape((Bc, NC) + qshape)
    v_c = cols(pc, OFF_DV, DA_WIDTH, base).reshape(Bc, NC, DA_HEADS, DA_VDIM)
    k_all = jnp.concatenate([k_c, k_l], axis=1)
    v_all = jnp.concatenate([v_c, v_l], axis=1)
    nb = N // Q_BLOCK
    q_blocks = jnp.moveaxis(q_l.reshape((B, nb, Q_BLOCK) + qshape), 1, 0)
    o_l = lax.map(lambda qb: diff_attend(qb, k_all, v_all, lam), q_blocks)
    o_l = jnp.moveaxis(o_l, 0, 1).reshape(B, N, DA_HEADS, DA_VDIM)
    da_l = (rmsnorm(o_l, d_norm_g) * (1.0 - lam_init)).reshape(B, N, DA_WIDTH)

    q_m = heads(jax.nn.silu(centred_conv(pl[..., OFF_MQ:OFF_MK], conv_w[:, :M_WIDTH], conv_b[:M_WIDTH])))
    k_m = heads(jax.nn.silu(centred_conv(pl[..., OFF_MK:OFF_DK], conv_w[:, M_WIDTH:], conv_b[M_WIDTH:]))) * (M_DIM ** -0.5)
    v_m = heads(pl[..., OFF_MV:OFF_G])
    li_f, lf_f, li_b, lf_b = gate_logs(pl[..., OFF_G:], gate_b)
    k_mc = heads(jax.nn.silu(centred_conv(cols(pc, OFF_MK, M_WIDTH, base), conv_w[:, M_WIDTH:], conv_b[M_WIDTH:]))) * (M_DIM ** -0.5)
    v_mc = heads(cols(pc, OFF_MV, M_WIDTH, base))
    lic_f, lfc_f, lic_b, lfc_b = gate_logs(cols(pc, OFF_G, N_GATES, base), gate_b)
    zero = mlstm_zero_state(Bc)
    if ctx_out:
        q_mc = heads(jax.nn.silu(centred_conv(pc[..., OFF_MQ:OFF_MK], conv_w[:, :M_WIDTH], conv_b[:M_WIDTH])))
        h_cf, st_f = mlstm_scan(q_mc, k_mc, v_mc, lic_f, lfc_f, zero)
        h_cb, st_b = mlstm_scan(flip(q_mc), flip(k_mc), flip(v_mc), flip(lic_b), flip(lfc_b), zero)
        m_c = mlstm_out(h_cf + flip(h_cb), pc[..., OFF_MO:OFF_MQ], m_norm_g)
    else:
        st_f = mlstm_state_update(zero, k_mc, v_mc, lic_f, lfc_f)
        st_b = mlstm_state_update(zero, flip(k_mc), flip(v_mc), flip(lic_b), flip(lfc_b))
    h_lf, _ = mlstm_scan(q_m, k_m, v_m, li_f, lf_f, st_f)
    h_lb, _ = mlstm_scan(flip(q_m), flip(k_m), flip(v_m), flip(li_b), flip(lf_b), st_b)
    m_l = mlstm_out(h_lf + flip(h_lb), pl[..., OFF_MO:OFF_MQ], m_norm_g)

    f_l = fourier_mix(pl[..., OFF_F:OFF_DQ], four_w)
    mix_l = jnp.concatenate([f_l, da_l, m_l], axis=-1)

    if ctx_out:
        q_c = pc[..., OFF_DQ:OFF_MO].reshape((Bc, NC) + qshape)
        o_c = diff_attend(q_c, k_c, v_c, lam)
        da_c = (rmsnorm(o_c, d_norm_g) * (1.0 - lam_init)).reshape(Bc, NC, DA_WIDTH)
        f_c = fourier_mix(pc[..., OFF_F:OFF_DQ], four_w)
        mix_c = jnp.concatenate([f_c, da_c, m_c], axis=-1)
    else:
        mix_c = None
    return mix_l, mix_c


def ec_moe(h, w_r, w1, w3, w2):
    B, N, D = h.shape
    cap = EC_CAPACITY * N // N_EXPERTS
    probs = jax.nn.softmax((h @ w_r).astype(jnp.float32), axis=-1)
    gate, idx = lax.top_k(jnp.swapaxes(probs, 1, 2), cap)
    xs = jax.vmap(lambda hb, ib: hb[ib])(h, idx)
    hid = jax.nn.silu(jnp.einsum('becd,edf->becf', xs, w1)) * jnp.einsum('becd,edf->becf', xs, w3)
    y = jnp.einsum('becf,efd->becd', hid, w2) * gate[..., None].astype(h.dtype)
    return jax.vmap(lambda ib, yb: jnp.zeros((N, D), h.dtype).at[ib.reshape(-1)].add(yb.reshape(-1, D)))(idx, y)


def setup_inputs(seed: int = 0) -> dict:
    key = jax.random.key(seed)
    ks = jax.random.split(key, 24)
    nrm = jax.random.normal
    f32 = jnp.float32
    D = D_MODEL
    gate_noise = 0.1 * nrm(ks[12], (DEPTH, 2, 2, M_HEADS), f32)
    gate_offset = jnp.array([0.0, 3.0], f32)[None, None, :, None]
    return {
        'x': nrm(ks[0], (BATCH, SEQ, D), f32),
        'c': nrm(ks[1], (BATCH, D), f32),
        'ctx': nrm(ks[2], (BATCH, CTX_LEN, D), f32),
        'c_ctx': nrm(ks[3], (D,), f32),
        'ada_w': nrm(ks[4], (DEPTH, D, ADA_CHUNKS * D), f32) * (0.5 * D ** -0.5),
        'ada_b': 0.02 * nrm(ks[5], (DEPTH, ADA_CHUNKS * D), f32),
        'norm1_g': 1.0 + 0.02 * nrm(ks[6], (DEPTH, D), f32),
        'norm2_g': 1.0 + 0.02 * nrm(ks[7], (DEPTH, D), f32),
        'w_in': nrm(ks[8], (DEPTH, D, PROJ_WIDTH), f32) * D ** -0.5,
        'four_w': nrm(ks[9], (DEPTH, F_GROUPS, F_GDIM, F_GDIM), f32) * F_GDIM ** -0.5,
        'm_conv_w': nrm(ks[10], (DEPTH, M_CONV, 2 * M_WIDTH), f32) * M_CONV ** -0.5,
        'm_conv_b': 0.02 * nrm(ks[11], (DEPTH, 2 * M_WIDTH), f32),
        'm_gate_b': (gate_noise + gate_offset).reshape(DEPTH, N_GATES),
        'm_norm_g': 1.0 + 0.02 * nrm(ks[13], (DEPTH, M_WIDTH), f32),
        'd_lam': 0.1 * nrm(ks[14], (DEPTH, 4, DA_DIM), f32),
        'd_norm_g': 1.0 + 0.02 * nrm(ks[15], (DEPTH, DA_VDIM), f32),
        'w_out': nrm(ks[16], (DEPTH, MIX_WIDTH, D), f32) * MIX_WIDTH ** -0.5,
        'router_w': nrm(ks[17], (DEPTH, D, N_EXPERTS), f32) * D ** -0.5,
        'exp_w1': nrm(ks[18], (DEPTH, N_EXPERTS, D, D_FF_EXPERT), f32) * D ** -0.5,
        'exp_w3': nrm(ks[19], (DEPTH, N_EXPERTS, D, D_FF_EXPERT), f32) * D ** -0.5,
        'exp_w2': nrm(ks[20], (DEPTH, N_EXPERTS, D_FF_EXPERT, D), f32) * D_FF_EXPERT ** -0.5,
        'final_g': 1.0 + 0.02 * nrm(ks[21], (D,), f32),
    }


def reference(x, c, ctx, c_ctx, ada_w, ada_b, norm1_g, norm2_g, w_in, four_w, m_conv_w, m_conv_b,
              m_gate_b, m_norm_g, d_lam, d_norm_g, w_out, router_w, exp_w1, exp_w3, exp_w2, final_g):
    B, S, _ = x.shape
    ROWS = S // GRID_W
    cos, sin = axial_rope_tables(ROWS)
    xl, xc = x, ctx
    for layer in range(DEPTH):
        ctx_out = layer < DEPTH - 1
        lam_init = 0.8 - 0.6 * math.exp(-0.3 * layer)
        sh1, sc1, gt1, sh2, sc2, gt2 = [m[:, None, :] for m in adaln(c, ada_w[layer], ada_b[layer])]
        csh1, csc1, cgt1, csh2, csc2, cgt2 = adaln(c_ctx, ada_w[layer], ada_b[layer])
        hl = rmsnorm(xl, norm1_g[layer]) * (1.0 + sc1) + sh1
        hc = rmsnorm(xc, norm1_g[layer]) * (1.0 + csc1) + csh1
        mix_l, mix_c = token_mixers(hl, hc, w_in[layer], four_w[layer], m_conv_w[layer], m_conv_b[layer],
                                    m_gate_b[layer], m_norm_g[layer], d_lam[layer], d_norm_g[layer],
                                    lam_init, cos, sin, ctx_out)
        xl = xl + gt1 * (mix_l @ w_out[layer])
        hl = rmsnorm(xl, norm2_g[layer]) * (1.0 + sc2) + sh2
        xl = xl + gt2 * ec_moe(hl, router_w[layer], exp_w1[layer], exp_w3[layer], exp_w2[layer])
        if ctx_out:
            xc = xc + cgt1 * (mix_c @ w_out[layer])
            hc = rmsnorm(xc, norm2_g[layer]) * (1.0 + csc2) + csh2
            xc = xc + cgt2 * ec_moe(hc, router_w[layer], exp_w1[layer], exp_w3[layer], exp_w2[layer])
    return rmsnorm(xl, final_g)
```

```python
import functools
import math

import numpy as np
import jax
import jax.numpy as jnp
from jax import lax
from jax.experimental import pallas as pl
from jax.experimental.pallas import tpu as pltpu

F32 = jnp.float32
BF16 = jnp.bfloat16
HI = lax.Precision.HIGHEST

D = 1024
EPS = 1e-6
GRID_W = 64
ROPE_THETA = 10000.0
F_GROUPS, F_GDIM = 4, 64
F_WIDTH = F_GROUPS * F_GDIM
DA_HEADS, DA_DIM = 6, 32
DA_VDIM = 2 * DA_DIM
DA_WIDTH = DA_HEADS * DA_VDIM
M_HEADS, M_DIM = 4, 96
M_WIDTH = M_HEADS * M_DIM
M_PAD = 128
MP_WIDTH = M_HEADS * M_PAD
N_GATES = 4 * M_HEADS
N_EXPERTS = 16
EC_CAPACITY = 2
D_FF = 2 * D
ADA_CHUNKS = 6

TOK = 256
FFT_N1 = 16
SLOT = 128
NEG = -1e30

OFF_F = 0
OFF_DQ = OFF_F + F_WIDTH
OFF_MO = OFF_DQ + 2 * DA_HEADS * DA_DIM
OFF_MQ = OFF_MO + M_WIDTH
OFF_MK = OFF_MQ + M_WIDTH
OFF_DK = OFF_MK + M_WIDTH
OFF_DV = OFF_DK + 2 * DA_HEADS * DA_DIM
OFF_MV = OFF_DV + DA_HEADS * DA_VDIM
OFF_G = OFF_MV + M_WIDTH

VMEM_LIMIT = 56 * 1024 * 1024


def _cp(sem, vmem=None):
    return pltpu.CompilerParams(dimension_semantics=sem, vmem_limit_bytes=vmem)


def _sigmoid(x):
    return 1.0 / (1.0 + jnp.exp(-x))


def _silu(x):
    return x * _sigmoid(x)


def _dot(a, b, precision=None):
    return jnp.dot(a, b, preferred_element_type=F32, precision=precision)


def _dot_nt(a, b, precision=None):
    return lax.dot_general(a, b, (((1,), (1,)), ((), ())), preferred_element_type=F32,
                           precision=precision)


def _ada_kernel(c_ref, w_ref, b_ref, o_ref):
    c = c_ref[...]
    o_ref[0] = _dot(_silu(c), w_ref[0], HI) + b_ref[0]


def _adaln(cvecs, ada_w, ada_b):
    depth = ada_w.shape[0]
    tn = 1536
    return pl.pallas_call(
        _ada_kernel,
        out_shape=jax.ShapeDtypeStruct((depth, 8, ADA_CHUNKS * D), F32),
        grid=(depth, ADA_CHUNKS * D // tn),
        in_specs=[pl.BlockSpec((8, D), lambda l, j: (0, 0)),
                  pl.BlockSpec((1, D, tn), lambda l, j: (l, 0, j)),
                  pl.BlockSpec((1, 1, tn), lambda l, j: (l, 0, j))],
        out_specs=pl.BlockSpec((1, 8, tn), lambda l, j: (l, 0, j)),
        compiler_params=_cp(("arbitrary", "arbitrary")),
        name="adaln",
    )(cvecs, ada_w, ada_b.reshape(depth, 1, ADA_CHUNKS * D))


def _inproj_kernel(x_ref, xp_ref, xn_ref, ada_ref, g_ref, wm_ref, wc_ref, wvt_ref, wg_ref, gb_ref, cs_ref,
                   cos_ref, sin_ref, cw_ref, cb_ref,
                   y_ref, dq_ref, dk_ref, dvt_ref, mo_ref, mq_ref, mk_ref, mv_ref, gl_ref, *, n_lat):
    b = pl.program_id(0)
    t = pl.program_id(1)
    n_tiles = pl.num_programs(1)
    is_ctx = t >= n_lat
    row = jnp.where(is_ctx, 4, b)
    mod = ada_ref[row]
    sh, sc = mod[0:1], mod[1:2]

    xa = jnp.concatenate([xp_ref[0], x_ref[0], xn_ref[0]], axis=0)
    r = lax.rsqrt(jnp.mean(xa * xa, axis=-1, keepdims=True) + EPS)
    ha = (xa * r) * g_ref[...] * (1.0 + sc) + sh
    h = ha[8:8 + TOK]
    hb = h.astype(BF16)

    pm = _dot(hb, wm_ref[...])
    o = 0
    pf = pm[:, o:o + F_WIDTH]; o += F_WIDTH
    q = pm[:, o:o + DA_WIDTH]; o += DA_WIDTH
    k = pm[:, o:o + DA_WIDTH]; o += DA_WIDTH
    mo = pm[:, o:o + MP_WIDTH]; o += MP_WIDTH
    mv = pm[:, o:o + MP_WIDTH]
    dvt_ref[0] = _dot_nt(wvt_ref[...], hb).astype(BF16)

    y_ref[0, 0] = _dot(pf, cs_ref[...], HI)

    cos = cos_ref[...]
    sin = sin_ref[...]
    lane = lax.broadcasted_iota(jnp.int32, (1, 128), 1)
    low = (lane % 16) < 8

    def rope(z):
        parts = []
        for c in range(DA_WIDTH // 128):
            zc = z[:, 128 * c:128 * (c + 1)]
            rot = jnp.where(low, pltpu.roll(zc, 120, 1), pltpu.roll(zc, 8, 1))
            parts.append(zc * cos + rot * sin)
        return jnp.concatenate(parts, axis=1)

    dq_ref[0] = (rope(q) * (DA_DIM ** -0.5 * math.log2(math.e))).astype(BF16)
    dk_ref[0] = rope(k).astype(BF16)
    mo_ref[0] = mo.astype(BF16)
    mv_ref[0] = mv.astype(BF16)

    gpre = _dot(h, wg_ref[...], HI) + gb_ref[...]
    gl = lax.broadcasted_iota(jnp.int32, (1, 128), 1)
    is_forget = (gl % 8) >= 4
    logsig = jnp.minimum(gpre, 0.0) - jnp.log(1.0 + jnp.exp(-jnp.abs(gpre)))
    gl_ref[0] = jnp.where(is_forget, logsig, gpre)

    pc = _dot(ha.astype(BF16), wc_ref[...])
    first = (t == 0) | (t == n_lat)
    last = (t == n_lat - 1) | (t == n_tiles - 1)
    ridx = lax.broadcasted_iota(jnp.int32, (TOK + 16, 1), 0)
    pc = jnp.where(((ridx < 8) & first) | ((ridx >= TOK + 8) & last), 0.0, pc)
    cw = cw_ref[...]
    conv = cb_ref[...] + pc[7:7 + TOK] * cw[0:1] + pc[8:8 + TOK] * cw[1:2] + pc[9:9 + TOK] * cw[2:3]
    act = _silu(conv)
    mq_ref[0] = act[:, :MP_WIDTH].astype(BF16)
    mk_ref[0] = (act[:, MP_WIDTH:] * (M_DIM ** -0.5)).astype(BF16)


def _inproj(xu, ada_l, g1, wm, wc, wvt, wg, gb, cs, cos_t, sin_t, cw, cb, *, n_lat, n2):
    B, NT, _ = xu.shape
    nt = n_lat + 1
    rper = n2 // TOK
    tok3 = lambda w: pl.BlockSpec((1, TOK, w), lambda b, t: (b, t, 0))
    full = lambda a: pl.BlockSpec(a.shape, lambda b, t: (0,) * a.ndim)
    nb8 = NT // 8
    outs = [jax.ShapeDtypeStruct((B, 2 * FFT_N1, n2, 2 * F_WIDTH), F32)]
    outs += [jax.ShapeDtypeStruct((B, NT, DA_WIDTH), BF16)] * 2
    outs += [jax.ShapeDtypeStruct((B, DA_WIDTH, NT), BF16)]
    outs += [jax.ShapeDtypeStruct((B, NT, MP_WIDTH), BF16)] * 4
    outs += [jax.ShapeDtypeStruct((B, NT, 128), F32)]
    out_specs = [pl.BlockSpec((1, 1, TOK, 2 * F_WIDTH), lambda b, t: (b, t // rper, t % rper, 0))]
    out_specs += [tok3(DA_WIDTH)] * 2 + [pl.BlockSpec((1, DA_WIDTH, TOK), lambda b, t: (b, 0, t))]
    out_specs += [tok3(MP_WIDTH)] * 4 + [tok3(128)]
    return pl.pallas_call(
        functools.partial(_inproj_kernel, n_lat=n_lat),
        out_shape=outs,
        grid=(B, nt),
        in_specs=[tok3(D),
                  pl.BlockSpec((1, 8, D), lambda b, t: (b, jnp.maximum(t * (TOK // 8) - 1, 0), 0)),
                  pl.BlockSpec((1, 8, D), lambda b, t: (b, jnp.minimum((t + 1) * (TOK // 8), nb8 - 1), 0)),
                  full(ada_l), full(g1), full(wm), full(wc), full(wvt), full(wg), full(gb), full(cs),
                  pl.BlockSpec((TOK, 128), lambda b, t: (t, 0)),
                  pl.BlockSpec((TOK, 128), lambda b, t: (t, 0)),
                  full(cw), full(cb)],
        out_specs=out_specs,
        compiler_params=_cp(("arbitrary", "arbitrary"), VMEM_LIMIT),
        name="norm1_inproj",
    )(xu, xu, xu, ada_l, g1, wm, wc, wvt, wg, gb, cs, cos_t, sin_t, cw, cb)


def _fft1_kernel(y_ref, kc_ref, ks_ref, tc_ref, ts_ref, o_ref, *, groups):
    for g in range(groups):
        blk = y_ref[0, :, 8 * g:8 * (g + 1), :].reshape(FFT_N1 * 8, 2 * F_WIDTH)
        p = _dot(kc_ref[...], blk, HI)
        q = _dot(ks_ref[...], blk, HI)
        ar = p[:, :F_WIDTH] - q[:, F_WIDTH:]
        ai = -p[:, F_WIDTH:] - q[:, :F_WIDTH]
        tc = tc_ref[128 * g:128 * (g + 1), :]
        ts = ts_ref[128 * g:128 * (g + 1), :]
        tc = jnp.concatenate([tc, tc], axis=1)
        ts = jnp.concatenate([ts, ts], axis=1)
        br = ar * tc + ai * ts
        bi = ai * tc - ar * ts
        o_ref[0, :, 8 * g:8 * (g + 1), :] = jnp.concatenate([br, bi], axis=1).reshape(FFT_N1, 8, 2 * F_WIDTH)


def _fft2_kernel(b_ref, c2_ref, s2_ref, wb_ref, perm_ref, o_ref, r_scr, *, n2):
    for i in range(8):
        blk = b_ref[0, i]
        xr = _dot(c2_ref[...], blk[:, :F_WIDTH], HI) + _dot(s2_ref[...], blk[:, F_WIDTH:], HI)
        r_scr[i] = _dot(xr.astype(BF16), wb_ref[...])
    for t in range(n2 // 32):
        rows = jnp.concatenate([r_scr[i, 32 * t:32 * (t + 1), :] for i in range(8)], axis=0)
        o_ref[0, 32 * t:32 * (t + 1), :, :] = _dot(perm_ref[...], rows, HI).reshape(32, 8, F_WIDTH)


def _fftc_kernel(y_ref, c_ref, s_ref, wb_ref, o_ref):
    y = y_ref[0, 0]
    z = _dot(c_ref[...], y[:, :F_WIDTH], HI) - _dot(s_ref[...], y[:, F_WIDTH:], HI)
    o_ref[0] = _dot(z.astype(BF16), wb_ref[...])


def _fourier_tables(n, ctx):
    n1, n2 = FFT_N1, n // FFT_N1
    a = np.arange(n1)
    ang1 = 2 * np.pi * np.outer(a, a) / n1
    eye8 = np.eye(8)
    kc = np.kron(np.cos(ang1), eye8)
    ks = np.kron(np.sin(ang1), eye8)
    n2i = np.arange(n2).reshape(n2 // 8, 1, 8)
    k1 = np.arange(n1).reshape(1, n1, 1)
    angt = (2 * np.pi * n2i * k1 / n).reshape(-1, 1)
    tc = np.broadcast_to(np.cos(angt), (n2 // 8 * 128, 128))
    ts = np.broadcast_to(np.sin(angt), (n2 // 8 * 128, 128))
    b = np.arange(n2)
    ang2 = 2 * np.pi * np.outer(b, b) / n2
    c2 = np.cos(ang2) / math.sqrt(n)
    s2 = np.sin(ang2) / math.sqrt(n)
    perm = np.zeros((256, 256))
    for kk in range(8):
        for j in range(32):
            perm[j * 8 + kk, kk * 32 + j] = 1.0
    cc = np.arange(ctx)
    angc = 2 * np.pi * np.outer(cc, cc) / ctx
    cctx = np.cos(angc) / math.sqrt(ctx)
    sctx = np.sin(angc) / math.sqrt(ctx)
    ch = np.arange(F_GDIM)
    angch = 2 * np.pi * np.outer(ch, ch) / F_GDIM
    cs = np.concatenate([np.kron(np.eye(F_GROUPS), np.cos(angch)),
                         np.kron(np.eye(F_GROUPS), np.sin(angch))], axis=1) / math.sqrt(F_GDIM)
    f = lambda z: jnp.asarray(np.ascontiguousarray(z), dtype=F32)
    return dict(kc=f(kc), ks=f(ks), tc=f(tc), ts=f(ts), c2=f(c2), s2=f(s2), perm=f(perm),
                cctx=f(cctx), sctx=f(sctx), cs=f(cs))


def _fourier(y4, tabs, wblk, *, n, ctx, with_ctx):
    B = y4.shape[0]
    n2 = n // FFT_N1
    groups = 4
    full = lambda a, nd: pl.BlockSpec(a.shape, lambda *i: (0,) * a.ndim)
    b4 = pl.pallas_call(
        functools.partial(_fft1_kernel, groups=groups),
        out_shape=jax.ShapeDtypeStruct((B, FFT_N1, n2, 2 * F_WIDTH), F32),
        grid=(B, n2 // (8 * groups)),
        in_specs=[pl.BlockSpec((1, FFT_N1, 8 * groups, 2 * F_WIDTH), lambda b, j: (b, 0, j, 0)),
                  full(tabs["kc"], 2), full(tabs["ks"], 2),
                  pl.BlockSpec((128 * groups, 128), lambda b, j: (j, 0)),
                  pl.BlockSpec((128 * groups, 128), lambda b, j: (j, 0))],
        out_specs=pl.BlockSpec((1, FFT_N1, 8 * groups, 2 * F_WIDTH), lambda b, j: (b, 0, j, 0)),
        compiler_params=_cp(("arbitrary", "arbitrary")),
        name="fourier_stage1",
    )(y4, tabs["kc"], tabs["ks"], tabs["tc"], tabs["ts"])
    f4 = pl.pallas_call(
        functools.partial(_fft2_kernel, n2=n2),
        out_shape=jax.ShapeDtypeStruct((B, n2, 16, F_WIDTH), F32),
        grid=(B, FFT_N1 // 8),
        in_specs=[pl.BlockSpec((1, 8, n2, 2 * F_WIDTH), lambda b, j: (b, j, 0, 0)),
                  full(tabs["c2"], 2), full(tabs["s2"], 2), full(wblk, 2), full(tabs["perm"], 2)],
        out_specs=pl.BlockSpec((1, n2, 8, F_WIDTH), lambda b, j: (b, 0, j, 0)),
        scratch_shapes=[pltpu.VMEM((8, n2, F_WIDTH), F32)],
        compiler_params=_cp(("arbitrary", "arbitrary"), VMEM_LIMIT),
        name="fourier_stage2",
    )(b4, tabs["c2"], tabs["s2"], wblk, tabs["perm"])
    f_ctx = None
    if with_ctx:
        f_ctx = pl.pallas_call(
            _fftc_kernel,
            out_shape=jax.ShapeDtypeStruct((B, ctx, F_WIDTH), F32),
            grid=(B,),
            in_specs=[pl.BlockSpec((1, 1, TOK, 2 * F_WIDTH), lambda b: (b, FFT_N1, 0, 0)),
                      full(tabs["cctx"], 1), full(tabs["sctx"], 1), full(wblk, 1)],
            out_specs=pl.BlockSpec((1, ctx, F_WIDTH), lambda b: (b, 0, 0)),
            compiler_params=_cp(("arbitrary",)),
            name="fourier_ctx",
        )(y4, tabs["cctx"], tabs["sctx"], wblk)
    return f4.reshape(B, n, F_WIDTH), f_ctx


VROWS = DA_VDIM + 16


def _attn_kernel(q_ref, k_ref, vt_ref, dl_ref, g_ref, o_ref, m_scr, acc_scr, *, lam_init):
    kt = pl.program_id(3)
    nk = pl.num_programs(3)

    @pl.when(kt == 0)
    def _():
        m_scr[...] = jnp.full(m_scr.shape, NEG, F32)
        acc_scr[...] = jnp.zeros(acc_scr.shape, F32)

    q = q_ref[0]
    k = k_ref[0]
    vt = vt_ref[0]
    ones = jnp.ones((16, vt.shape[1]), BF16)
    lhs = [jnp.concatenate([vt[DA_VDIM * h:DA_VDIM * (h + 1)], ones], axis=0) for h in range(2)]
    lane = lax.broadcasted_iota(jnp.int32, (1, 128), 1)
    zero = jnp.zeros((), BF16)
    for j in range(4):
        qj = jnp.where((lane // DA_DIM) == j, q, zero)
        st = _dot_nt(k, qj)
        m_old = m_scr[j]
        m_new = jnp.maximum(m_old, jnp.max(st, axis=0, keepdims=True))
        alpha = jnp.exp2(m_old - m_new)
        pt = jnp.exp2(st - m_new).astype(BF16)
        acc_scr[j] = alpha * acc_scr[j] + _dot(lhs[j // 2], pt)
        m_scr[j] = m_new

    @pl.when(kt == nk - 1)
    def _():
        dl = dl_ref[...]
        lam = (jnp.exp(jnp.sum(dl[0:1] * dl[1:2], keepdims=True))
               - jnp.exp(jnp.sum(dl[2:3] * dl[3:4], keepdims=True)) + lam_init)
        outs = []
        for h in range(2):
            a0 = acc_scr[2 * h]
            a1 = acc_scr[2 * h + 1]
            o = (a0[:DA_VDIM] / a0[DA_VDIM:DA_VDIM + 1]
                 - lam * (a1[:DA_VDIM] / a1[DA_VDIM:DA_VDIM + 1]))
            r = lax.rsqrt(jnp.mean(o * o, axis=0, keepdims=True) + EPS)
            outs.append(((o * r) * g_ref[...]) * (1.0 - lam_init))
        o_ref[0] = jnp.concatenate(outs, axis=0).astype(BF16)


def _attention(dq, dk, dvT, dlam, gcol, *, lam_init, tq, q0, nq, tk, k0, nk):
    B = dq.shape[0]
    return pl.pallas_call(
        functools.partial(_attn_kernel, lam_init=lam_init),
        out_shape=jax.ShapeDtypeStruct((B, DA_WIDTH, nq * tq), BF16),
        grid=(B, DA_WIDTH // 128, nq, nk),
        in_specs=[pl.BlockSpec((1, tq, 128), lambda b, p, i, j: (b, q0 + i, p)),
                  pl.BlockSpec((1, tk, 128), lambda b, p, i, j: (b, k0 + j, p)),
                  pl.BlockSpec((1, 128, tk), lambda b, p, i, j: (b, p, k0 + j)),
                  pl.BlockSpec(dlam.shape, lambda b, p, i, j: (0, 0)),
                  pl.BlockSpec(gcol.shape, lambda b, p, i, j: (0, 0))],
        out_specs=pl.BlockSpec((1, 128, tq), lambda b, p, i, j: (b, p, i)),
        scratch_shapes=[pltpu.VMEM((4, 1, tq), F32), pltpu.VMEM((4, VROWS, tq), F32)],
        compiler_params=_cp(("arbitrary",) * 4, VMEM_LIMIT),
        name="diff_attention",
    )(dq, dk, dvT, dlam, gcol)


def _mlstm_kernel(qf_ref, kf_ref, vf_ref, gcf_ref, grf_ref, qb_ref, kb_ref, vb_ref, gcb_ref, grb_ref,
                  hf_ref, hb_ref, c_scr, n_scr, m_scr):
    t = pl.program_id(1)

    @pl.when(t == 0)
    def _():
        c_scr[...] = jnp.zeros(c_scr.shape, F32)
        n_scr[...] = jnp.zeros(n_scr.shape, F32)
        m_scr[...] = jnp.zeros(m_scr.shape, F32)

    L = TOK
    ri = lax.broadcasted_iota(jnp.int32, (L, L), 0)
    ci = lax.broadcasted_iota(jnp.int32, (L, L), 1)
    dirs = ((qf_ref, kf_ref, vf_ref, gcf_ref, grf_ref, hf_ref, ci <= ri, L - 1),
            (qb_ref, kb_ref, vb_ref, gcb_ref, grb_ref, hb_ref, ci >= ri, 0))
    for d, (q_ref, k_ref, v_ref, gc_ref, gr_ref, h_ref, seen, last_row) in enumerate(dirs):
        seen_f = jnp.where(seen, 1.0, 0.0)
        gc = gc_ref[0]
        gr = gr_ref[0]
        bcols = _dot(seen_f, gc, HI)
        brows = _dot_nt(gr, seen_f, HI)
        for hd in range(M_HEADS):
            idx = d * M_HEADS + hd
            ji = d * 8 + hd
            jf = d * 8 + 4 + hd
            sl = slice(M_PAD * hd, M_PAD * (hd + 1))
            q = q_ref[0, :, sl]
            k = k_ref[0, :, sl]
            v = v_ref[0, :, sl]
            bc = bcols[:, jf:jf + 1]
            br = brows[jf:jf + 1, :]
            lic = gc[:, ji:ji + 1]
            lir = gr[ji:ji + 1, :]
            m_old = m_scr[idx][0:1, 0:1]
            c_old = c_scr[idx]
            n_old = n_scr[idx][0:1, :]

            dlog = jnp.where(seen, bc - br + lir, NEG)
            inter = bc + m_old
            m_t = jnp.maximum(inter, jnp.max(dlog, axis=1, keepdims=True))
            w_inter = jnp.exp(inter - m_t)
            s = _dot_nt(q, k) * jnp.exp(dlog - m_t)
            num = w_inter * _dot_nt(q, c_old.astype(BF16)) + _dot(s.astype(BF16), v)
            qf32 = q.astype(F32)
            den = w_inter * jnp.sum(qf32 * n_old, axis=1, keepdims=True) + jnp.sum(s, axis=1, keepdims=True)
            h_ref[0, :, sl] = num / jnp.maximum(jnp.abs(den), jnp.exp(-m_t))

            total = bcols[last_row:last_row + 1, jf:jf + 1]
            wlog_c = total - bc + lic
            m_new = jnp.maximum(total + m_old, jnp.max(wlog_c, axis=0, keepdims=True))
            decay = jnp.exp(total + m_old - m_new)
            wc = jnp.exp(wlog_c - m_new)
            kf32 = k.astype(F32)
            vw = (v.astype(F32) * wc).astype(BF16)
            c_scr[idx] = decay * c_old + lax.dot_general(vw, k, (((0,), (0,)), ((), ())),
                                                         preferred_element_type=F32)
            n_new = decay * n_old + jnp.sum(kf32 * wc, axis=0, keepdims=True)
            n_scr[idx] = jnp.broadcast_to(n_new, (8, M_PAD))
            m_scr[idx] = jnp.broadcast_to(m_new, (8, 128))


def _mlstm(mq, mk, mv, gl, glT, *, n_lat):
    B, NT, _ = mq.shape
    nt = n_lat + 1
    fwd = lambda t: jnp.where(t == 0, n_lat, t - 1)
    bwd = lambda t: jnp.where(t == 0, n_lat, n_lat - t)
    tok = lambda w, f: pl.BlockSpec((1, TOK, w), lambda b, t: (b, f(t), 0))
    lanes = lambda f: pl.BlockSpec((1, N_GATES, TOK), lambda b, t: (b, 0, f(t)))
    ins, specs = [], []
    for f in (fwd, bwd):
        ins += [mq, mk, mv, gl, glT]
        specs += [tok(MP_WIDTH, f)] * 3 + [tok(128, f), lanes(f)]
    return pl.pallas_call(
        _mlstm_kernel,
        out_shape=[jax.ShapeDtypeStruct((B, NT, MP_WIDTH), F32)] * 2,
        grid=(B, nt),
        in_specs=specs,
        out_specs=[tok(MP_WIDTH, fwd), tok(MP_WIDTH, bwd)],
        scratch_shapes=[pltpu.VMEM((2 * M_HEADS, M_PAD, M_PAD), F32),
                        pltpu.VMEM((2 * M_HEADS, 8, M_PAD), F32),
                        pltpu.VMEM((2 * M_HEADS, 8, 128), F32)],
        compiler_params=_cp(("arbitrary", "arbitrary"), VMEM_LIMIT),
        name="mlstm",
    )(*ins)


def _outproj_kernel(x_ref, f_ref, dat_ref, hf_ref, hb_ref, mo_ref, ada_ref, mg_ref, wo_ref, wod_ref, g2_ref, wr_ref,
                    xo_ref, hl_ref, pt_ref, *, is_ctx):
    b = pl.program_id(0)
    mod = ada_ref[4 if is_ctx else b]
    gt1, sh2, sc2 = mod[2:3], mod[3:4], mod[4:5]
    hs = hf_ref[0] + hb_ref[0]
    og = mo_ref[0].astype(F32)
    mg = mg_ref[...]
    parts = [f_ref[0].astype(BF16)]
    for hd in range(M_HEADS):
        sl = slice(M_PAD * hd, M_PAD * (hd + 1))
        hh = hs[:, sl]
        r = lax.rsqrt(jnp.sum(hh * hh, axis=1, keepdims=True) * (1.0 / M_DIM) + EPS)
        parts.append((((hh * r) * mg[:, sl]) * _sigmoid(og[:, sl])).astype(BF16))
    mix = jnp.concatenate(parts, axis=1)
    upd = _dot(mix, wo_ref[...]) + lax.dot_general(dat_ref[0], wod_ref[...], (((0,), (0,)), ((), ())),
                                                   preferred_element_type=F32)
    xn = x_ref[0] + gt1 * upd
    xo_ref[0] = xn
    r = lax.rsqrt(jnp.mean(xn * xn, axis=-1, keepdims=True) + EPS)
    h2 = (xn * r) * g2_ref[...] * (1.0 + sc2) + sh2
    hl_ref[0] = h2.astype(BF16)
    lt = _dot_nt(wr_ref[...], h2, HI)
    ex = jnp.exp(lt - jnp.max(lt, axis=0, keepdims=True))
    pt_ref[0] = ex / jnp.sum(ex, axis=0, keepdims=True)


def _outproj_kernel_aliased(x_ref, f_ref, dat_ref, hf_ref, hb_ref, mo_ref, ada_ref, mg_ref, wo_ref, wod_ref,
                            g2_ref, wr_ref, hlp_ref, xo_ref, hl_ref, pt_ref, *, is_ctx):
    del hlp_ref
    _outproj_kernel(x_ref, f_ref, dat_ref, hf_ref, hb_ref, mo_ref, ada_ref, mg_ref, wo_ref, wod_ref, g2_ref,
                    wr_ref, xo_ref, hl_ref, pt_ref, is_ctx=is_ctx)


def _outproj(xu, f, daT, hf, hb, mo, ada_l, mg, wo, wod, g2, wrT, hl_prev, *, t0, ntl, is_ctx):
    B, NT, _ = xu.shape
    tok = lambda w: pl.BlockSpec((1, TOK, w), lambda b, t: (b, t0 + t, 0))
    loc = lambda w: pl.BlockSpec((1, TOK, w), lambda b, t: (b, t, 0))
    full = lambda a: pl.BlockSpec(a.shape, lambda b, t: (0,) * a.ndim)
    return pl.pallas_call(
        functools.partial(_outproj_kernel_aliased, is_ctx=is_ctx),
        out_shape=[jax.ShapeDtypeStruct(xu.shape, F32), jax.ShapeDtypeStruct((B, NT, D), BF16),
                   jax.ShapeDtypeStruct((B, N_EXPERTS, ntl * TOK), F32)],
        grid=(B, ntl),
        in_specs=[tok(D), loc(F_WIDTH), pl.BlockSpec((1, DA_WIDTH, TOK), lambda b, t: (b, 0, t)),
                  tok(MP_WIDTH), tok(MP_WIDTH), tok(MP_WIDTH),
                  full(ada_l), full(mg), full(wo), full(wod), full(g2), full(wrT),
                  pl.BlockSpec(memory_space=pl.ANY)],
        out_specs=[tok(D), tok(D), pl.BlockSpec((1, N_EXPERTS, TOK), lambda b, t: (b, 0, t))],
        input_output_aliases={0: 0, 12: 1},
        compiler_params=_cp(("arbitrary", "arbitrary"), VMEM_LIMIT),
        name="outproj_norm2_router",
    )(xu, f, daT, hf, hb, mo, ada_l, mg, wo, wod, g2, wrT, hl_prev)


def _select_kernel(p_ref, rank_ref, offs_ref, *, n, cap):
    p = p_ref[0]
    xi = pltpu.bitcast(p, jnp.int32)

    def body(i, lo):
        cand = lo | jnp.left_shift(jnp.int32(1), 30 - i)
        cnt = jnp.sum(jnp.where(xi >= cand, 1.0, 0.0), axis=1, keepdims=True)
        return jnp.where(cnt >= cap, cand, lo)

    thr = lax.fori_loop(0, 31, body, jnp.zeros((N_EXPERTS, 1), jnp.int32))
    nb = n // TOK
    rows = lax.broadcasted_iota(jnp.int32, (n, 128), 0)
    cols = lax.broadcasted_iota(jnp.int32, (n, 128), 1)
    blk_ind = jnp.where((rows // TOK) == cols, 1.0, 0.0).astype(BF16)
    u128 = jnp.where(lax.broadcasted_iota(jnp.int32, (128, 128), 0)
                     < lax.broadcasted_iota(jnp.int32, (128, 128), 1), 1.0, 0.0).astype(BF16)
    utok = jnp.where(lax.broadcasted_iota(jnp.int32, (TOK, TOK), 0)
                     < lax.broadcasted_iota(jnp.int32, (TOK, TOK), 1), 1.0, 0.0).astype(BF16)

    def prefix(mf):
        mb = mf.astype(BF16)
        counts = _dot(mb, blk_ind)
        offs = _dot(counts.astype(BF16), u128)
        pieces = [_dot(mb[:, TOK * j:TOK * (j + 1)], utok) + offs[:, j:j + 1] for j in range(nb)]
        return (jnp.concatenate(pieces, axis=1) if nb > 1 else pieces[0]), offs

    gt = xi > thr
    eq = xi == thr
    need = cap - jnp.sum(jnp.where(gt, 1.0, 0.0), axis=1, keepdims=True)
    rank_eq, _ = prefix(jnp.where(eq, 1.0, 0.0))
    sel = gt | (eq & (rank_eq < need))
    rank, offs = prefix(jnp.where(sel, 1.0, 0.0))
    rank_ref[0] = jnp.where(sel, rank, -1.0)
    offs_ref[0] = offs.astype(jnp.int32)


def _select(pt, *, cap):
    B, _, n = pt.shape
    return pl.pallas_call(
        functools.partial(_select_kernel, n=n, cap=cap),
        out_shape=[jax.ShapeDtypeStruct((B, N_EXPERTS, n), F32),
                   jax.ShapeDtypeStruct((B, N_EXPERTS, 128), jnp.int32)],
        grid=(B,),
        in_specs=[pl.BlockSpec((1, N_EXPERTS, n), lambda b: (b, 0, 0))],
        out_specs=[pl.BlockSpec((1, N_EXPERTS, n), lambda b: (b, 0, 0)),
                   pl.BlockSpec((1, N_EXPERTS, 128), lambda b: (b, 0, 0))],
        compiler_params=_cp(("arbitrary",), VMEM_LIMIT),
        name="expert_choice_select",
    )(pt)


def _slot_tiles(offs_ref, b, e, j0, j1):
    lo = offs_ref[b, e, j0]
    hi = offs_ref[b, e, j1]
    t0 = lo // SLOT
    t1 = jnp.where(hi > lo, (hi - 1) // SLOT + 1, t0)
    return t0, t1


def _gather_kernel(offs_ref, h_ref, rank_ref, o_ref, *, eg, per):
    b, g, tb = pl.program_id(0), pl.program_id(1), pl.program_id(2)

    @pl.when(tb == 0)
    def _():
        o_ref[...] = jnp.zeros(o_ref.shape, BF16)

    h = h_ref[0]
    slot = lax.broadcasted_iota(jnp.int32, (SLOT, 1), 0).astype(F32)
    for i in range(eg):
        e = g * eg + i
        r = rank_ref[0, pl.ds(e, 1), :]
        t0, t1 = _slot_tiles(offs_ref, b, e, tb * per, (tb + 1) * per)

        def body(t, carry, i=i, r=r):
            base = pl.multiple_of(t * SLOT, SLOT)
            onehot = jnp.where(r == slot + base.astype(F32), 1.0, 0.0).astype(BF16)
            rows = _dot(onehot, h).astype(BF16)
            o_ref[0, i, pl.ds(base, SLOT), :] = o_ref[0, i, pl.ds(base, SLOT), :] + rows
            return carry

        lax.fori_loop(t0, t1, body, 0)


def _gather(offs, hl, rank, *, tb_tok, tb0, n, cap_pad, eg):
    B = hl.shape[0]
    per = tb_tok // TOK
    return pl.pallas_call(
        functools.partial(_gather_kernel, eg=eg, per=per),
        out_shape=jax.ShapeDtypeStruct((B, N_EXPERTS, cap_pad, D), BF16),
        grid_spec=pltpu.PrefetchScalarGridSpec(
            num_scalar_prefetch=1,
            grid=(B, N_EXPERTS // eg, n // tb_tok),
            in_specs=[pl.BlockSpec((1, tb_tok, D), lambda b, g, t, o: (b, tb0 + t, 0)),
                      pl.BlockSpec((1, N_EXPERTS, tb_tok), lambda b, g, t, o: (b, 0, t))],
            out_specs=pl.BlockSpec((1, eg, cap_pad, D), lambda b, g, t, o: (b, g, 0, 0))),
        compiler_params=_cp(("arbitrary",) * 3, VMEM_LIMIT),
        name="expert_gather",
    )(offs, hl, rank)


def _ffn_kernel(x_ref, w1_ref, w3_ref, w2_ref, y_ref, acc_ref):
    f = pl.program_id(2)
    x = x_ref[...].reshape(-1, D)
    a = _dot(x, w1_ref[0, 0].astype(BF16))
    g3 = _dot(x, w3_ref[0, 0].astype(BF16))
    hid = (_silu(a) * g3).astype(BF16)
    contrib = _dot(hid, w2_ref[0, 0].astype(BF16))

    @pl.when(f == 0)
    def _():
        acc_ref[...] = contrib

    @pl.when(f > 0)
    def _():
        acc_ref[...] += contrib

    @pl.when(f == pl.num_programs(2) - 1)
    def _():
        y_ref[...] = acc_ref[...].astype(BF16).reshape(y_ref.shape)


def _ffn(xs, w1, w3, w2, *, layer, mb, tf):
    B, E, cap_pad, _ = xs.shape
    return pl.pallas_call(
        _ffn_kernel,
        out_shape=jax.ShapeDtypeStruct(xs.shape, BF16),
        grid=(E, B // mb, D_FF // tf),
        in_specs=[pl.BlockSpec((mb, 1, cap_pad, D), lambda e, m, f: (m, e, 0, 0)),
                  pl.BlockSpec((1, 1, D, tf), lambda e, m, f: (layer, e, 0, f)),
                  pl.BlockSpec((1, 1, D, tf), lambda e, m, f: (layer, e, 0, f)),
                  pl.BlockSpec((1, 1, tf, D), lambda e, m, f: (layer, e, f, 0))],
        out_specs=pl.BlockSpec((mb, 1, cap_pad, D), lambda e, m, f: (m, e, 0, 0)),
        scratch_shapes=[pltpu.VMEM((mb * cap_pad, D), F32)],
        compiler_params=_cp(("arbitrary",) * 3, VMEM_LIMIT),
        name="expert_ffn",
    )(xs, w1, w3, w2)


CCOL = 256


def _combine_kernel(offs_ref, x_ref, y_ref, rankc_ref, probc_ref, ada_ref, o_ref, *, per, is_ctx):
    b, tb = pl.program_id(0), pl.program_id(2)
    gt2 = ada_ref[4 if is_ctx else b][5:6]
    rc_all = rankc_ref[0]
    pc_all = probc_ref[0]
    slot = lax.broadcasted_iota(jnp.int32, (1, SLOT), 1).astype(F32)
    ntok = rc_all.shape[0]
    total = jnp.zeros((ntok, CCOL), F32)
    for e in range(N_EXPERTS):
        rc = rc_all[:, e:e + 1]
        t0, t1 = _slot_tiles(offs_ref, b, e, tb * per, (tb + 1) * per)

        def body(t, acc, e=e, rc=rc):
            base = pl.multiple_of(t * SLOT, SLOT)
            onehot = jnp.where(rc == slot + base.astype(F32), 1.0, 0.0).astype(BF16)
            return acc + _dot(onehot, y_ref[0, e, pl.ds(base, SLOT), :])

        acc = lax.fori_loop(t0, t1, body, jnp.zeros((ntok, CCOL), F32))
        total = total + pc_all[:, e:e + 1] * acc
    o_ref[0] = x_ref[0] + gt2 * total


def _combine(offs, xu, ys, rank_c, prob_c, ada_l, *, tb_tok, tb0, n, is_ctx):
    B = xu.shape[0]
    cap_pad = ys.shape[2]
    per = tb_tok // TOK
    return pl.pallas_call(
        functools.partial(_combine_kernel, per=per, is_ctx=is_ctx),
        out_shape=jax.ShapeDtypeStruct(xu.shape, F32),
        grid_spec=pltpu.PrefetchScalarGridSpec(
            num_scalar_prefetch=1,
            grid=(B, D // CCOL, n // tb_tok),
            in_specs=[pl.BlockSpec((1, tb_tok, CCOL), lambda b, c, t, o: (b, tb0 + t, c)),
                      pl.BlockSpec((1, N_EXPERTS, cap_pad, CCOL), lambda b, c, t, o: (b, 0, 0, c)),
                      pl.BlockSpec((1, tb_tok, N_EXPERTS), lambda b, c, t, o: (b, t, 0)),
                      pl.BlockSpec((1, tb_tok, N_EXPERTS), lambda b, c, t, o: (b, t, 0)),
                      pl.BlockSpec((8, ADA_CHUNKS, CCOL), lambda b, c, t, o: (0, 0, c))],
            out_specs=pl.BlockSpec((1, tb_tok, CCOL), lambda b, c, t, o: (b, tb0 + t, c))),
        input_output_aliases={1: 0},
        compiler_params=_cp(("arbitrary",) * 3, VMEM_LIMIT),
        name="expert_combine",
    )(offs, xu, ys, rank_c, prob_c, ada_l)


def _moe(xu, hl, pt, ada_l, w1, w3, w2, *, layer, row0, is_ctx):
    B, _, n = pt.shape
    cap = EC_CAPACITY * n // N_EXPERTS
    cap_pad = -(-cap // SLOT) * SLOT
    nb = n // TOK
    rank, offs = _select(pt, cap=cap)
    offs = offs[:, :, :nb + 1]
    gt = min(n, 1024)
    ct = min(n, 512)
    xs = _gather(offs, hl, rank, tb_tok=gt, tb0=row0 // gt, n=n, cap_pad=cap_pad, eg=4)
    mb = 2 if (B % 2 == 0 and cap_pad >= 1024) else (B if cap_pad < 1024 else 1)
    ys = _ffn(xs, w1, w3, w2, layer=layer, mb=mb, tf=256)
    rank_c = jnp.swapaxes(rank, 1, 2)
    prob_c = jnp.swapaxes(pt, 1, 2)
    return _combine(offs, xu, ys, rank_c, prob_c, ada_l, tb_tok=ct, tb0=row0 // ct, n=n, is_ctx=is_ctx)


def _final_kernel(x_ref, g_ref, o_ref):
    x = x_ref[0]
    r = lax.rsqrt(jnp.mean(x * x, axis=-1, keepdims=True) + EPS)
    o_ref[0] = (x * r) * g_ref[...]


def _final_norm(xu, g, *, n):
    B = xu.shape[0]
    tm = 512
    return pl.pallas_call(
        _final_kernel,
        out_shape=jax.ShapeDtypeStruct((B, n, D), F32),
        grid=(B, n // tm),
        in_specs=[pl.BlockSpec((1, tm, D), lambda b, t: (b, t, 0)),
                  pl.BlockSpec((1, D), lambda b, t: (0, 0))],
        out_specs=pl.BlockSpec((1, tm, D), lambda b, t: (b, t, 0)),
        compiler_params=_cp(("arbitrary", "arbitrary")),
        name="final_norm",
    )(xu, g)


def _rope_tables(n, ctx):
    rows = n // GRID_W
    t_row = jnp.repeat(jnp.arange(rows), GRID_W)
    t_col = jnp.tile(jnp.arange(GRID_W), rows)
    nf = DA_DIM // 4
    inv = ROPE_THETA ** (-jnp.arange(nf, dtype=F32) / nf)
    ar = t_row[:, None].astype(F32) * inv
    ac = t_col[:, None].astype(F32) * inv
    ang = jnp.concatenate([ar, ar, ac, ac], axis=-1)
    sign = jnp.where((jnp.arange(DA_DIM) % 16) < 8, -1.0, 1.0).astype(F32)
    cos = jnp.concatenate([jnp.cos(ang), jnp.ones((ctx, DA_DIM), F32)], axis=0)
    sin = jnp.concatenate([jnp.sin(ang) * sign, jnp.zeros((ctx, DA_DIM), F32)], axis=0)
    return jnp.tile(cos, (1, 128 // DA_DIM)), jnp.tile(sin, (1, 128 // DA_DIM))


def _pad_heads_cols(w):
    lead = w.shape[:-1]
    w = w.reshape(lead + (M_HEADS, M_DIM))
    w = jnp.pad(w, [(0, 0)] * len(lead) + [(0, 0), (0, M_PAD - M_DIM)])
    return w.reshape(lead + (MP_WIDTH,))


def _kv_tile(nt):
    for cand in (1024, 768, 512, 256):
        if nt % cand == 0:
            return cand
    raise ValueError(nt)


def kernel(x, c, ctx, c_ctx, ada_w, ada_b, norm1_g, norm2_g, w_in, four_w, m_conv_w, m_conv_b, m_gate_b,
           m_norm_g, d_lam, d_norm_g, w_out, router_w, exp_w1, exp_w3, exp_w2, final_g):
    B, N, _ = x.shape
    CTX = ctx.shape[1]
    depth = w_in.shape[0]
    assert CTX == TOK and N % (FFT_N1 * TOK) == 0 and B <= 4
    NT = N + CTX
    PAD = 1024 - CTX
    n_lat = N // TOK
    n2 = N // FFT_N1

    xu = jnp.concatenate([x, ctx, jnp.zeros((B, PAD, D), F32)], axis=1)
    cvecs = jnp.zeros((8, D), F32).at[:B].set(c).at[4].set(c_ctx)
    ada = _adaln(cvecs, ada_w, ada_b).reshape(depth, 8, ADA_CHUNKS, D)
    cos_t, sin_t = _rope_tables(N, CTX + PAD)
    tabs = _fourier_tables(N, CTX)
    tk = _kv_tile(NT)
    tq = 512

    hl = jnp.zeros((B, NT + PAD, D), BF16)
    for layer in range(depth):
        ctx_out = layer < depth - 1
        lam_init = 0.8 - 0.6 * math.exp(-0.3 * layer)
        w = w_in[layer]
        wm = jnp.concatenate([w[:, OFF_F:OFF_DQ], w[:, OFF_DQ:OFF_MO], w[:, OFF_DK:OFF_DV],
                              _pad_heads_cols(w[:, OFF_MO:OFF_MQ]), _pad_heads_cols(w[:, OFF_MV:OFF_G])],
                             axis=1).astype(BF16)
        wvt = w[:, OFF_DV:OFF_MV].T.astype(BF16)
        wc = jnp.concatenate([_pad_heads_cols(w[:, OFF_MQ:OFF_MK]), _pad_heads_cols(w[:, OFF_MK:OFF_DK])],
                             axis=1).astype(BF16)
        wg = jnp.pad(w[:, OFF_G:], ((0, 0), (0, 128 - N_GATES)))
        gb = jnp.pad(m_gate_b[layer], (0, 128 - N_GATES)).reshape(1, 128)
        cw = jnp.concatenate([_pad_heads_cols(m_conv_w[layer][:, :M_WIDTH]),
                              _pad_heads_cols(m_conv_w[layer][:, M_WIDTH:])], axis=1)
        cb = jnp.concatenate([_pad_heads_cols(m_conv_b[layer][:M_WIDTH]),
                              _pad_heads_cols(m_conv_b[layer][M_WIDTH:])]).reshape(1, 2 * MP_WIDTH)
        ada_l = ada[layer]

        y4, dq, dk, dvT, mo, mq, mk, mv, gl = _inproj(
            xu, ada_l, norm1_g[layer].reshape(1, D), wm, wc, wvt, wg, gb, tabs["cs"], cos_t, sin_t, cw, cb,
            n_lat=n_lat, n2=n2)

        wblk = jnp.zeros((F_WIDTH, F_WIDTH), F32)
        for g in range(F_GROUPS):
            wblk = wblk.at[F_GDIM * g:F_GDIM * (g + 1), F_GDIM * g:F_GDIM * (g + 1)].set(four_w[layer, g])
        f_l, f_c = _fourier(y4, tabs, wblk.astype(BF16), n=N, ctx=CTX, with_ctx=ctx_out)

        dlam = d_lam[layer]
        g2 = d_norm_g[layer].reshape(DA_VDIM, 1)
        da_l = _attention(dq, dk, dvT, dlam, g2, lam_init=lam_init, tq=tq, q0=0, nq=N // tq,
                          tk=tk, k0=0, nk=NT // tk)

        glT = jnp.swapaxes(gl[:, :, :N_GATES], 1, 2)
        hf, hb = _mlstm(mq, mk, mv, gl, glT, n_lat=n_lat)

        mg = _pad_heads_cols(m_norm_g[layer]).reshape(1, MP_WIDTH)
        wol = w_out[layer]
        wo = jnp.concatenate([wol[:F_WIDTH],
                              jnp.pad(wol[F_WIDTH + DA_WIDTH:].reshape(M_HEADS, M_DIM, D),
                                      ((0, 0), (0, M_PAD - M_DIM), (0, 0))).reshape(MP_WIDTH, D)],
                             axis=0).astype(BF16)
        wod = wol[F_WIDTH:F_WIDTH + DA_WIDTH].astype(BF16)
        g2n = norm2_g[layer].reshape(1, D)
        wrT = router_w[layer].T
        xu, hl, pt_l = _outproj(xu, f_l, da_l, hf, hb, mo, ada_l, mg, wo, wod, g2n, wrT, hl,
                                t0=0, ntl=n_lat, is_ctx=False)
        if ctx_out:
            da_c = _attention(dq, dk, dvT, dlam, g2, lam_init=lam_init, tq=TOK, q0=n_lat, nq=1,
                              tk=TOK, k0=n_lat, nk=1)
            xu, hl, pt_c = _outproj(xu, f_c, da_c, hf, hb, mo, ada_l, mg, wo, wod, g2n, wrT, hl,
                                    t0=n_lat, ntl=1, is_ctx=True)

        xu = _moe(xu, hl, pt_l, ada_l, exp_w1, exp_w3, exp_w2, layer=layer, row0=0, is_ctx=False)
        if ctx_out:
            xu = _moe(xu, hl, pt_c, ada_l, exp_w1, exp_w3, exp_w2, layer=layer, row0=N, is_ctx=True)

    return _final_norm(xu, final_g.reshape(1, D), n=N)
```

```python
import functools
import math

import numpy as np
import jax
import jax.numpy as jnp
from jax import lax
from jax.experimental import pallas as pl
from jax.experimental.pallas import tpu as pltpu

F32 = jnp.float32
BF16 = jnp.bfloat16
HI = lax.Precision.HIGHEST

D = 1024
EPS = 1e-6
GRID_W = 64
ROPE_THETA = 10000.0
F_GROUPS, F_GDIM = 4, 64
F_WIDTH = F_GROUPS * F_GDIM
DA_HEADS, DA_DIM = 6, 32
DA_VDIM = 2 * DA_DIM
DA_WIDTH = DA_HEADS * DA_VDIM
M_HEADS, M_DIM = 4, 96
M_WIDTH = M_HEADS * M_DIM
M_PAD = 128
MP_WIDTH = M_HEADS * M_PAD
N_GATES = 4 * M_HEADS
N_EXPERTS = 16
EC_CAPACITY = 2
D_FF = 2 * D
ADA_CHUNKS = 6

TOK = 256
FFT_N1 = 16
SLOT = 128
NEG = -1e30

OFF_F = 0
OFF_DQ = OFF_F + F_WIDTH
OFF_MO = OFF_DQ + 2 * DA_HEADS * DA_DIM
OFF_MQ = OFF_MO + M_WIDTH
OFF_MK = OFF_MQ + M_WIDTH
OFF_DK = OFF_MK + M_WIDTH
OFF_DV = OFF_DK + 2 * DA_HEADS * DA_DIM
OFF_MV = OFF_DV + DA_HEADS * DA_VDIM
OFF_G = OFF_MV + M_WIDTH

VMEM_LIMIT = 56 * 1024 * 1024


def _cp(sem, vmem=None):
    return pltpu.CompilerParams(dimension_semantics=sem, vmem_limit_bytes=vmem)


def _sigmoid(x):
    return 1.0 / (1.0 + jnp.exp(-x))


def _silu(x):
    return x * _sigmoid(x)


def _dot(a, b, precision=None):
    return jnp.dot(a, b, preferred_element_type=F32, precision=precision)


def _dot_nt(a, b, precision=None):
    return lax.dot_general(a, b, (((1,), (1,)), ((), ())), preferred_element_type=F32,
                           precision=precision)


def _ada_kernel(c_ref, w_ref, b_ref, o_ref):
    c = c_ref[...]
    o_ref[0] = _dot(_silu(c), w_ref[0], HI) + b_ref[0]


def _adaln(cvecs, ada_w, ada_b):
    depth = ada_w.shape[0]
    tn = 1536
    return pl.pallas_call(
        _ada_kernel,
        out_shape=jax.ShapeDtypeStruct((depth, 8, ADA_CHUNKS * D), F32),
        grid=(depth, ADA_CHUNKS * D // tn),
        in_specs=[pl.BlockSpec((8, D), lambda l, j: (0, 0)),
                  pl.BlockSpec((1, D, tn), lambda l, j: (l, 0, j)),
                  pl.BlockSpec((1, 1, tn), lambda l, j: (l, 0, j))],
        out_specs=pl.BlockSpec((1, 8, tn), lambda l, j: (l, 0, j)),
        compiler_params=_cp(("arbitrary", "arbitrary")),
        name="adaln",
    )(cvecs, ada_w, ada_b.reshape(depth, 1, ADA_CHUNKS * D))


def _inproj_kernel(x_ref, xp_ref, xn_ref, ada_ref, g_ref, wm_ref, wc_ref, wvt_ref, wg_ref, gb_ref, cs_ref,
                   cos_ref, sin_ref, cw_ref, cb_ref,
                   y_ref, dq_ref, dk_ref, dvt_ref, mo_ref, mq_ref, mk_ref, mv_ref, gl_ref, *, n_lat):
    b = pl.program_id(0)
    t = pl.program_id(1)
    n_tiles = pl.num_programs(1)
    is_ctx = t >= n_lat
    row = jnp.where(is_ctx, 4, b)
    mod = ada_ref[row]
    sh, sc = mod[0:1], mod[1:2]

    xa = jnp.concatenate([xp_ref[0], x_ref[0], xn_ref[0]], axis=0)
    r = lax.rsqrt(jnp.mean(xa * xa, axis=-1, keepdims=True) + EPS)
    ha = (xa * r) * g_ref[...] * (1.0 + sc) + sh
    h = ha[8:8 + TOK]
    hb = h.astype(BF16)

    pm = _dot(hb, wm_ref[...])
    o = 0
    pf = pm[:, o:o + F_WIDTH]; o += F_WIDTH
    q = pm[:, o:o + DA_WIDTH]; o += DA_WIDTH
    k = pm[:, o:o + DA_WIDTH]; o += DA_WIDTH
    mo = pm[:, o:o + MP_WIDTH]; o += MP_WIDTH
    mv = pm[:, o:o + MP_WIDTH]
    dvt_ref[0] = _dot_nt(wvt_ref[...], hb).astype(BF16)

    y_ref[0, 0] = _dot(pf, cs_ref[...], HI)

    cos = cos_ref[...]
    sin = sin_ref[...]
    lane = lax.broadcasted_iota(jnp.int32, (1, 128), 1)
    low = (lane % 16) < 8

    def rope(z):
        parts = []
        for c in range(DA_WIDTH // 128):
            zc = z[:, 128 * c:128 * (c + 1)]
            rot = jnp.where(low, pltpu.roll(zc, 120, 1), pltpu.roll(zc, 8, 1))
            parts.append(zc * cos + rot * sin)
        return jnp.concatenate(parts, axis=1)

    dq_ref[0] = (rope(q) * (DA_DIM ** -0.5 * math.log2(math.e))).astype(BF16)
    dk_ref[0] = rope(k).astype(BF16)
    mo_ref[0] = mo.astype(BF16)
    mv_ref[0] = mv.astype(BF16)

    gpre = _dot(h, wg_ref[...], HI) + gb_ref[...]
    gl = lax.broadcasted_iota(jnp.int32, (1, 128), 1)
    is_forget = (gl % 8) >= 4
    logsig = jnp.minimum(gpre, 0.0) - jnp.log(1.0 + jnp.exp(-jnp.abs(gpre)))
    gl_ref[0] = jnp.where(is_forget, logsig, gpre)

    pc = _dot(ha.astype(BF16), wc_ref[...])
    first = (t == 0) | (t == n_lat)
    last = (t == n_lat - 1) | (t == n_tiles - 1)
    ridx = lax.broadcasted_iota(jnp.int32, (TOK + 16, 1), 0)
    pc = jnp.where(((ridx < 8) & first) | ((ridx >= TOK + 8) & last), 0.0, pc)
    cw = cw_ref[...]
    conv = cb_ref[...] + pc[7:7 + TOK] * cw[0:1] + pc[8:8 + TOK] * cw[1:2] + pc[9:9 + TOK] * cw[2:3]
    act = _silu(conv)
    mq_ref[0] = act[:, :MP_WIDTH].astype(BF16)
    mk_ref[0] = (act[:, MP_WIDTH:] * (M_DIM ** -0.5)).astype(BF16)


def _inproj(xu, ada_l, g1, wm, wc, wvt, wg, gb, cs, cos_t, sin_t, cw, cb, *, n_lat, n2):
    B, NT, _ = xu.shape
    nt = n_lat + 1
    rper = n2 // TOK
    tok3 = lambda w: pl.BlockSpec((1, TOK, w), lambda b, t: (b, t, 0))
    full = lambda a: pl.BlockSpec(a.shape, lambda b, t: (0,) * a.ndim)
    nb8 = NT // 8
    outs = [jax.ShapeDtypeStruct((B, 2 * FFT_N1, n2, 2 * F_WIDTH), F32)]
    outs += [jax.ShapeDtypeStruct((B, NT, DA_WIDTH), BF16)] * 2
    outs += [jax.ShapeDtypeStruct((B, DA_WIDTH, NT), BF16)]
    outs += [jax.ShapeDtypeStruct((B, NT, MP_WIDTH), BF16)] * 4
    outs += [jax.ShapeDtypeStruct((B, NT, 128), F32)]
    out_specs = [pl.BlockSpec((1, 1, TOK, 2 * F_WIDTH), lambda b, t: (b, t // rper, t % rper, 0))]
    out_specs += [tok3(DA_WIDTH)] * 2 + [pl.BlockSpec((1, DA_WIDTH, TOK), lambda b, t: (b, 0, t))]
    out_specs += [tok3(MP_WIDTH)] * 4 + [tok3(128)]
    return pl.pallas_call(
        functools.partial(_inproj_kernel, n_lat=n_lat),
        out_shape=outs,
        grid=(B, nt),
        in_specs=[tok3(D),
                  pl.BlockSpec((1, 8, D), lambda b, t: (b, jnp.maximum(t * (TOK // 8) - 1, 0), 0)),
                  pl.BlockSpec((1, 8, D), lambda b, t: (b, jnp.minimum((t + 1) * (TOK // 8), nb8 - 1), 0)),
                  full(ada_l), full(g1), full(wm), full(wc), full(wvt), full(wg), full(gb), full(cs),
                  pl.BlockSpec((TOK, 128), lambda b, t: (t, 0)),
                  pl.BlockSpec((TOK, 128), lambda b, t: (t, 0)),
                  full(cw), full(cb)],
        out_specs=out_specs,
        compiler_params=_cp(("arbitrary", "arbitrary"), VMEM_LIMIT),
        name="norm1_inproj",
    )(xu, xu, xu, ada_l, g1, wm, wc, wvt, wg, gb, cs, cos_t, sin_t, cw, cb)


def _fft1_kernel(y_ref, kc_ref, ks_ref, tc_ref, ts_ref, o_ref, *, groups):
    for g in range(groups):
        blk = y_ref[0, :, 8 * g:8 * (g + 1), :].reshape(FFT_N1 * 8, 2 * F_WIDTH)
        p = _dot(kc_ref[...], blk, HI)
        q = _dot(ks_ref[...], blk, HI)
        ar = p[:, :F_WIDTH] - q[:, F_WIDTH:]
        ai = -p[:, F_WIDTH:] - q[:, :F_WIDTH]
        tc = tc_ref[128 * g:128 * (g + 1), :]
        ts = ts_ref[128 * g:128 * (g + 1), :]
        tc = jnp.concatenate([tc, tc], axis=1)
        ts = jnp.concatenate([ts, ts], axis=1)
        br = ar * tc + ai * ts
        bi = ai * tc - ar * ts
        o_ref[0, :, 8 * g:8 * (g + 1), :] = jnp.concatenate([br, bi], axis=1).reshape(FFT_N1, 8, 2 * F_WIDTH)


def _fft2_kernel(b_ref, c2_ref, s2_ref, wb_ref, perm_ref, o_ref, r_scr, *, n2):
    for i in range(8):
        blk = b_ref[0, i]
        xr = _dot(c2_ref[...], blk[:, :F_WIDTH], HI) + _dot(s2_ref[...], blk[:, F_WIDTH:], HI)
        r_scr[i] = _dot(xr.astype(BF16), wb_ref[...])
    for t in range(n2 // 32):
        rows = jnp.concatenate([r_scr[i, 32 * t:32 * (t + 1), :] for i in range(8)], axis=0)
        o_ref[0, 32 * t:32 * (t + 1), :, :] = _dot(perm_ref[...], rows, HI).reshape(32, 8, F_WIDTH)


def _fftc_kernel(y_ref, c_ref, s_ref, wb_ref, o_ref):
    y = y_ref[0, 0]
    z = _dot(c_ref[...], y[:, :F_WIDTH], HI) - _dot(s_ref[...], y[:, F_WIDTH:], HI)
    o_ref[0] = _dot(z.astype(BF16), wb_ref[...])


def _fourier_tables(n, ctx):
    n1, n2 = FFT_N1, n // FFT_N1
    a = np.arange(n1)
    ang1 = 2 * np.pi * np.outer(a, a) / n1
    eye8 = np.eye(8)
    kc = np.kron(np.cos(ang1), eye8)
    ks = np.kron(np.sin(ang1), eye8)
    n2i = np.arange(n2).reshape(n2 // 8, 1, 8)
    k1 = np.arange(n1).reshape(1, n1, 1)
    angt = (2 * np.pi * n2i * k1 / n).reshape(-1, 1)
    tc = np.broadcast_to(np.cos(angt), (n2 // 8 * 128, 128))
    ts = np.broadcast_to(np.sin(angt), (n2 // 8 * 128, 128))
    b = np.arange(n2)
    ang2 = 2 * np.pi * np.outer(b, b) / n2
    c2 = np.cos(ang2) / math.sqrt(n)
    s2 = np.sin(ang2) / math.sqrt(n)
    perm = np.zeros((256, 256))
    for kk in range(8):
        for j in range(32):
            perm[j * 8 + kk, kk * 32 + j] = 1.0
    cc = np.arange(ctx)
    angc = 2 * np.pi * np.outer(cc, cc) / ctx
    cctx = np.cos(angc) / math.sqrt(ctx)
    sctx = np.sin(angc) / math.sqrt(ctx)
    ch = np.arange(F_GDIM)
    angch = 2 * np.pi * np.outer(ch, ch) / F_GDIM
    cs = np.concatenate([np.kron(np.eye(F_GROUPS), np.cos(angch)),
                         np.kron(np.eye(F_GROUPS), np.sin(angch))], axis=1) / math.sqrt(F_GDIM)
    f = lambda z: jnp.asarray(np.ascontiguousarray(z), dtype=F32)
    return dict(kc=f(kc), ks=f(ks), tc=f(tc), ts=f(ts), c2=f(c2), s2=f(s2), perm=f(perm),
                cctx=f(cctx), sctx=f(sctx), cs=f(cs))


def _fourier(y4, tabs, wblk, *, n, ctx, with_ctx):
    B = y4.shape[0]
    n2 = n // FFT_N1
    groups = 4
    full = lambda a, nd: pl.BlockSpec(a.shape, lambda *i: (0,) * a.ndim)
    b4 = pl.pallas_call(
        functools.partial(_fft1_kernel, groups=groups),
        out_shape=jax.ShapeDtypeStruct((B, FFT_N1, n2, 2 * F_WIDTH), F32),
        grid=(B, n2 // (8 * groups)),
        in_specs=[pl.BlockSpec((1, FFT_N1, 8 * groups, 2 * F_WIDTH), lambda b, j: (b, 0, j, 0)),
                  full(tabs["kc"], 2), full(tabs["ks"], 2),
                  pl.BlockSpec((128 * groups, 128), lambda b, j: (j, 0)),
                  pl.BlockSpec((128 * groups, 128), lambda b, j: (j, 0))],
        out_specs=pl.BlockSpec((1, FFT_N1, 8 * groups, 2 * F_WIDTH), lambda b, j: (b, 0, j, 0)),
        compiler_params=_cp(("arbitrary", "arbitrary")),
        name="fourier_stage1",
    )(y4, tabs["kc"], tabs["ks"], tabs["tc"], tabs["ts"])
    f4 = pl.pallas_call(
        functools.partial(_fft2_kernel, n2=n2),
        out_shape=jax.ShapeDtypeStruct((B, n2, 16, F_WIDTH), F32),
        grid=(B, FFT_N1 // 8),
        in_specs=[pl.BlockSpec((1, 8, n2, 2 * F_WIDTH), lambda b, j: (b, j, 0, 0)),
                  full(tabs["c2"], 2), full(tabs["s2"], 2), full(wblk, 2), full(tabs["perm"], 2)],
        out_specs=pl.BlockSpec((1, n2, 8, F_WIDTH), lambda b, j: (b, 0, j, 0)),
        scratch_shapes=[pltpu.VMEM((8, n2, F_WIDTH), F32)],
        compiler_params=_cp(("arbitrary", "arbitrary"), VMEM_LIMIT),
        name="fourier_stage2",
    )(b4, tabs["c2"], tabs["s2"], wblk, tabs["perm"])
    f_ctx = None
    if with_ctx:
        f_ctx = pl.pallas_call(
            _fftc_kernel,
            out_shape=jax.ShapeDtypeStruct((B, ctx, F_WIDTH), F32),
            grid=(B,),
            in_specs=[pl.BlockSpec((1, 1, TOK, 2 * F_WIDTH), lambda b: (b, FFT_N1, 0, 0)),
                      full(tabs["cctx"], 1), full(tabs["sctx"], 1), full(wblk, 1)],
            out_specs=pl.BlockSpec((1, ctx, F_WIDTH), lambda b: (b, 0, 0)),
            compiler_params=_cp(("arbitrary",)),
            name="fourier_ctx",
        )(y4, tabs["cctx"], tabs["sctx"], wblk)
    return f4.reshape(B, n, F_WIDTH), f_ctx


VROWS = DA_VDIM + 16


def _attn_kernel(q_ref, k_ref, vt_ref, dl_ref, g_ref, o_ref, m_scr, acc_scr, *, lam_init):
    kt = pl.program_id(3)
    nk = pl.num_programs(3)

    @pl.when(kt == 0)
    def _():
        m_scr[...] = jnp.full(m_scr.shape, NEG, F32)
        acc_scr[...] = jnp.zeros(acc_scr.shape, F32)

    q = q_ref[0]
    k = k_ref[0]
    vt = vt_ref[0]
    ones = jnp.ones((16, vt.shape[1]), BF16)
    lhs = [jnp.concatenate([vt[DA_VDIM * h:DA_VDIM * (h + 1)], ones], axis=0) for h in range(2)]
    lane = lax.broadcasted_iota(jnp.int32, (1, 128), 1)
    zero = jnp.zeros((), BF16)
    def scores(j):
        return _dot_nt(k, jnp.where((lane // DA_DIM) == j, q, zero))

    st_next = scores(0)
    for j in range(4):
        st = st_next
        if j < 3:
            st_next = scores(j + 1)
        m_old = m_scr[j]
        m_new = jnp.maximum(m_old, jnp.max(st, axis=0, keepdims=True))
        alpha = jnp.exp2(m_old - m_new)
        pt = jnp.exp2(st - m_new).astype(BF16)
        acc_scr[j] = alpha * acc_scr[j] + _dot(lhs[j // 2], pt)
        m_scr[j] = m_new

    @pl.when(kt == nk - 1)
    def _():
        dl = dl_ref[...]
        lam = (jnp.exp(jnp.sum(dl[0:1] * dl[1:2], keepdims=True))
               - jnp.exp(jnp.sum(dl[2:3] * dl[3:4], keepdims=True)) + lam_init)
        outs = []
        for h in range(2):
            a0 = acc_scr[2 * h]
            a1 = acc_scr[2 * h + 1]
            o = (a0[:DA_VDIM] / a0[DA_VDIM:DA_VDIM + 1]
                 - lam * (a1[:DA_VDIM] / a1[DA_VDIM:DA_VDIM + 1]))
            r = lax.rsqrt(jnp.mean(o * o, axis=0, keepdims=True) + EPS)
            outs.append(((o * r) * g_ref[...]) * (1.0 - lam_init))
        o_ref[0] = jnp.concatenate(outs, axis=0).astype(BF16)


def _attention(dq, dk, dvT, dlam, gcol, *, lam_init, tq, q0, nq, tk, k0, nk):
    B = dq.shape[0]
    return pl.pallas_call(
        functools.partial(_attn_kernel, lam_init=lam_init),
        out_shape=jax.ShapeDtypeStruct((B, DA_WIDTH, nq * tq), BF16),
        grid=(B, DA_WIDTH // 128, nq, nk),
        in_specs=[pl.BlockSpec((1, tq, 128), lambda b, p, i, j: (b, q0 + i, p)),
                  pl.BlockSpec((1, tk, 128), lambda b, p, i, j: (b, k0 + j, p)),
                  pl.BlockSpec((1, 128, tk), lambda b, p, i, j: (b, p, k0 + j)),
                  pl.BlockSpec(dlam.shape, lambda b, p, i, j: (0, 0)),
                  pl.BlockSpec(gcol.shape, lambda b, p, i, j: (0, 0))],
        out_specs=pl.BlockSpec((1, 128, tq), lambda b, p, i, j: (b, p, i)),
        scratch_shapes=[pltpu.VMEM((4, 1, tq), F32), pltpu.VMEM((4, VROWS, tq), F32)],
        compiler_params=_cp(("arbitrary",) * 4, VMEM_LIMIT),
        name="diff_attention",
    )(dq, dk, dvT, dlam, gcol)


def _mlstm_kernel(qf_ref, kf_ref, vf_ref, gcf_ref, grf_ref, qb_ref, kb_ref, vb_ref, gcb_ref, grb_ref,
                  hf_ref, hb_ref, c_scr, n_scr, m_scr):
    t = pl.program_id(1)

    @pl.when(t == 0)
    def _():
        c_scr[...] = jnp.zeros(c_scr.shape, F32)
        n_scr[...] = jnp.zeros(n_scr.shape, F32)
        m_scr[...] = jnp.zeros(m_scr.shape, F32)

    L = TOK
    ri = lax.broadcasted_iota(jnp.int32, (L, L), 0)
    ci = lax.broadcasted_iota(jnp.int32, (L, L), 1)
    dirs = ((qf_ref, kf_ref, vf_ref, gcf_ref, grf_ref, hf_ref, ci <= ri, L - 1),
            (qb_ref, kb_ref, vb_ref, gcb_ref, grb_ref, hb_ref, ci >= ri, 0))
    for d, (q_ref, k_ref, v_ref, gc_ref, gr_ref, h_ref, seen, last_row) in enumerate(dirs):
        seen_f = jnp.where(seen, 1.0, 0.0)
        gc = gc_ref[0]
        gr = gr_ref[0]
        bcols = _dot(seen_f, gc, HI)
        brows = _dot_nt(gr, seen_f, HI)
        for hd in range(M_HEADS):
            idx = d * M_HEADS + hd
            ji = d * 8 + hd
            jf = d * 8 + 4 + hd
            sl = slice(M_PAD * hd, M_PAD * (hd + 1))
            q = q_ref[0, :, sl]
            k = k_ref[0, :, sl]
            v = v_ref[0, :, sl]
            bc = bcols[:, jf:jf + 1]
            br = brows[jf:jf + 1, :]
            lic = gc[:, ji:ji + 1]
            lir = gr[ji:ji + 1, :]
            m_old = m_scr[idx][0:1, 0:1]
            c_old = c_scr[idx]
            n_old = n_scr[idx][0:1, :]

            dlog = jnp.where(seen, bc - br + lir, NEG)
            inter = bc + m_old
            m_t = jnp.maximum(inter, jnp.max(dlog, axis=1, keepdims=True))
            w_inter = jnp.exp(inter - m_t)
            s = _dot_nt(q, k) * jnp.exp(dlog - m_t)
            num = w_inter * _dot_nt(q, c_old.astype(BF16)) + _dot(s.astype(BF16), v)
            qf32 = q.astype(F32)
            den = w_inter * jnp.sum(qf32 * n_old, axis=1, keepdims=True) + jnp.sum(s, axis=1, keepdims=True)
            h_ref[0, :, sl] = num / jnp.maximum(jnp.abs(den), jnp.exp(-m_t))

            total = bcols[last_row:last_row + 1, jf:jf + 1]
            wlog_c = total - bc + lic
            m_new = jnp.maximum(total + m_old, jnp.max(wlog_c, axis=0, keepdims=True))
            decay = jnp.exp(total + m_old - m_new)
            wc = jnp.exp(wlog_c - m_new)
            kf32 = k.astype(F32)
            vw = (v.astype(F32) * wc).astype(BF16)
            c_scr[idx] = decay * c_old + lax.dot_general(vw, k, (((0,), (0,)), ((), ())),
                                                         preferred_element_type=F32)
            n_new = decay * n_old + jnp.sum(kf32 * wc, axis=0, keepdims=True)
            n_scr[idx] = jnp.broadcast_to(n_new, (8, M_PAD))
            m_scr[idx] = jnp.broadcast_to(m_new, (8, 128))


def _mlstm(mq, mk, mv, gl, glT, *, n_lat):
    B, NT, _ = mq.shape
    nt = n_lat + 1
    fwd = lambda t: jnp.where(t == 0, n_lat, t - 1)
    bwd = lambda t: jnp.where(t == 0, n_lat, n_lat - t)
    tok = lambda w, f: pl.BlockSpec((1, TOK, w), lambda b, t: (b, f(t), 0))
    lanes = lambda f: pl.BlockSpec((1, N_GATES, TOK), lambda b, t: (b, 0, f(t)))
    ins, specs = [], []
    for f in (fwd, bwd):
        ins += [mq, mk, mv, gl, glT]
        specs += [tok(MP_WIDTH, f)] * 3 + [tok(128, f), lanes(f)]
    return pl.pallas_call(
        _mlstm_kernel,
        out_shape=[jax.ShapeDtypeStruct((B, NT, MP_WIDTH), F32)] * 2,
        grid=(B, nt),
        in_specs=specs,
        out_specs=[tok(MP_WIDTH, fwd), tok(MP_WIDTH, bwd)],
        scratch_shapes=[pltpu.VMEM((2 * M_HEADS, M_PAD, M_PAD), F32),
                        pltpu.VMEM((2 * M_HEADS, 8, M_PAD), F32),
                        pltpu.VMEM((2 * M_HEADS, 8, 128), F32)],
        compiler_params=_cp(("arbitrary", "arbitrary"), VMEM_LIMIT),
        name="mlstm",
    )(*ins)


def _outproj_kernel(x_ref, f_ref, dat_ref, hf_ref, hb_ref, mo_ref, ada_ref, mg_ref, wo_ref, wod_ref, g2_ref, wr_ref,
                    xo_ref, hl_ref, pt_ref, *, is_ctx):
    b = pl.program_id(0)
    mod = ada_ref[4 if is_ctx else b]
    gt1, sh2, sc2 = mod[2:3], mod[3:4], mod[4:5]
    hs = hf_ref[0] + hb_ref[0]
    og = mo_ref[0].astype(F32)
    mg = mg_ref[...]
    parts = [f_ref[0].astype(BF16)]
    for hd in range(M_HEADS):
        sl = slice(M_PAD * hd, M_PAD * (hd + 1))
        hh = hs[:, sl]
        r = lax.rsqrt(jnp.sum(hh * hh, axis=1, keepdims=True) * (1.0 / M_DIM) + EPS)
        parts.append((((hh * r) * mg[:, sl]) * _sigmoid(og[:, sl])).astype(BF16))
    mix = jnp.concatenate(parts, axis=1)
    upd = _dot(mix, wo_ref[...]) + lax.dot_general(dat_ref[0], wod_ref[...], (((0,), (0,)), ((), ())),
                                                   preferred_element_type=F32)
    xn = x_ref[0] + gt1 * upd
    xo_ref[0] = xn
    r = lax.rsqrt(jnp.mean(xn * xn, axis=-1, keepdims=True) + EPS)
    h2 = (xn * r) * g2_ref[...] * (1.0 + sc2) + sh2
    hl_ref[0] = h2.astype(BF16)
    lt = _dot_nt(wr_ref[...], h2, HI)
    ex = jnp.exp(lt - jnp.max(lt, axis=0, keepdims=True))
    pt_ref[0] = ex / jnp.sum(ex, axis=0, keepdims=True)


def _outproj_kernel_aliased(x_ref, f_ref, dat_ref, hf_ref, hb_ref, mo_ref, ada_ref, mg_ref, wo_ref, wod_ref,
                            g2_ref, wr_ref, hlp_ref, xo_ref, hl_ref, pt_ref, *, is_ctx):
    del hlp_ref
    _outproj_kernel(x_ref, f_ref, dat_ref, hf_ref, hb_ref, mo_ref, ada_ref, mg_ref, wo_ref, wod_ref, g2_ref,
                    wr_ref, xo_ref, hl_ref, pt_ref, is_ctx=is_ctx)


def _outproj(xu, f, daT, hf, hb, mo, ada_l, mg, wo, wod, g2, wrT, hl_prev, *, t0, ntl, is_ctx):
    B, NT, _ = xu.shape
    tok = lambda w: pl.BlockSpec((1, TOK, w), lambda b, t: (b, t0 + t, 0))
    loc = lambda w: pl.BlockSpec((1, TOK, w), lambda b, t: (b, t, 0))
    full = lambda a: pl.BlockSpec(a.shape, lambda b, t: (0,) * a.ndim)
    return pl.pallas_call(
        functools.partial(_outproj_kernel_aliased, is_ctx=is_ctx),
        out_shape=[jax.ShapeDtypeStruct(xu.shape, F32), jax.ShapeDtypeStruct((B, NT, D), BF16),
                   jax.ShapeDtypeStruct((B, N_EXPERTS, ntl * TOK), F32)],
        grid=(B, ntl),
        in_specs=[tok(D), loc(F_WIDTH), pl.BlockSpec((1, DA_WIDTH, TOK), lambda b, t: (b, 0, t)),
                  tok(MP_WIDTH), tok(MP_WIDTH), tok(MP_WIDTH),
                  full(ada_l), full(mg), full(wo), full(wod), full(g2), full(wrT),
                  pl.BlockSpec(memory_space=pl.ANY)],
        out_specs=[tok(D), tok(D), pl.BlockSpec((1, N_EXPERTS, TOK), lambda b, t: (b, 0, t))],
        input_output_aliases={0: 0, 12: 1},
        compiler_params=_cp(("arbitrary", "arbitrary"), VMEM_LIMIT),
        name="outproj_norm2_router",
    )(xu, f, daT, hf, hb, mo, ada_l, mg, wo, wod, g2, wrT, hl_prev)


def _select_kernel(p_ref, rank_ref, offs_ref, *, n, cap):
    p = p_ref[0]
    xi = pltpu.bitcast(p, jnp.int32)

    def body(i, lo):
        cand = lo | jnp.left_shift(jnp.int32(1), 30 - i)
        cnt = jnp.sum(jnp.where(xi >= cand, 1.0, 0.0), axis=1, keepdims=True)
        return jnp.where(cnt >= cap, cand, lo)

    thr = lax.fori_loop(0, 31, body, jnp.zeros((N_EXPERTS, 1), jnp.int32))
    nb = n // TOK
    rows = lax.broadcasted_iota(jnp.int32, (n, 128), 0)
    cols = lax.broadcasted_iota(jnp.int32, (n, 128), 1)
    blk_ind = jnp.where((rows // TOK) == cols, 1.0, 0.0).astype(BF16)
    u128 = jnp.where(lax.broadcasted_iota(jnp.int32, (128, 128), 0)
                     < lax.broadcasted_iota(jnp.int32, (128, 128), 1), 1.0, 0.0).astype(BF16)
    utok = jnp.where(lax.broadcasted_iota(jnp.int32, (TOK, TOK), 0)
                     < lax.broadcasted_iota(jnp.int32, (TOK, TOK), 1), 1.0, 0.0).astype(BF16)

    def prefix(mf):
        mb = mf.astype(BF16)
        counts = _dot(mb, blk_ind)
        offs = _dot(counts.astype(BF16), u128)
        pieces = [_dot(mb[:, TOK * j:TOK * (j + 1)], utok) + offs[:, j:j + 1] for j in range(nb)]
        return (jnp.concatenate(pieces, axis=1) if nb > 1 else pieces[0]), offs

    gt = xi > thr
    eq = xi == thr
    need = cap - jnp.sum(jnp.where(gt, 1.0, 0.0), axis=1, keepdims=True)
    rank_eq, _ = prefix(jnp.where(eq, 1.0, 0.0))
    sel = gt | (eq & (rank_eq < need))
    rank, offs = prefix(jnp.where(sel, 1.0, 0.0))
    rank_ref[0] = jnp.where(sel, rank, -1.0)
    offs_ref[0] = offs.astype(jnp.int32)


def _select(pt, *, cap):
    B, _, n = pt.shape
    return pl.pallas_call(
        functools.partial(_select_kernel, n=n, cap=cap),
        out_shape=[jax.ShapeDtypeStruct((B, N_EXPERTS, n), F32),
                   jax.ShapeDtypeStruct((B, N_EXPERTS, 128), jnp.int32)],
        grid=(B,),
        in_specs=[pl.BlockSpec((1, N_EXPERTS, n), lambda b: (b, 0, 0))],
        out_specs=[pl.BlockSpec((1, N_EXPERTS, n), lambda b: (b, 0, 0)),
                   pl.BlockSpec((1, N_EXPERTS, 128), lambda b: (b, 0, 0))],
        compiler_params=_cp(("arbitrary",), VMEM_LIMIT),
        name="expert_choice_select",
    )(pt)


def _slot_tiles(offs_ref, b, e, j0, j1):
    lo = offs_ref[b, e, j0]
    hi = offs_ref[b, e, j1]
    t0 = lo // SLOT
    t1 = jnp.where(hi > lo, (hi - 1) // SLOT + 1, t0)
    return t0, t1


def _gather_kernel(offs_ref, h_ref, rank_ref, o_ref, *, eg, per):
    b, g, tb = pl.program_id(0), pl.program_id(1), pl.program_id(2)

    @pl.when(tb == 0)
    def _():
        o_ref[...] = jnp.zeros(o_ref.shape, BF16)

    h = h_ref[0]
    slot = lax.broadcasted_iota(jnp.int32, (SLOT, 1), 0).astype(F32)
    for i in range(eg):
        e = g * eg + i
        r = rank_ref[0, pl.ds(e, 1), :]
        t0, t1 = _slot_tiles(offs_ref, b, e, tb * per, (tb + 1) * per)

        def body(t, carry, i=i, r=r):
            base = pl.multiple_of(t * SLOT, SLOT)
            onehot = jnp.where(r == slot + base.astype(F32), 1.0, 0.0).astype(BF16)
            rows = _dot(onehot, h).astype(BF16)
            o_ref[0, i, pl.ds(base, SLOT), :] = o_ref[0, i, pl.ds(base, SLOT), :] + rows
            return carry

        lax.fori_loop(t0, t1, body, 0)


def _gather(offs, hl, rank, *, tb_tok, tb0, n, cap_pad, eg):
    B = hl.shape[0]
    per = tb_tok // TOK
    return pl.pallas_call(
        functools.partial(_gather_kernel, eg=eg, per=per),
        out_shape=jax.ShapeDtypeStruct((B, N_EXPERTS, cap_pad, D), BF16),
        grid_spec=pltpu.PrefetchScalarGridSpec(
            num_scalar_prefetch=1,
            grid=(B, N_EXPERTS // eg, n // tb_tok),
            in_specs=[pl.BlockSpec((1, tb_tok, D), lambda b, g, t, o: (b, tb0 + t, 0)),
                      pl.BlockSpec((1, N_EXPERTS, tb_tok), lambda b, g, t, o: (b, 0, t))],
            out_specs=pl.BlockSpec((1, eg, cap_pad, D), lambda b, g, t, o: (b, g, 0, 0))),
        compiler_params=_cp(("arbitrary",) * 3, VMEM_LIMIT),
        name="expert_gather",
    )(offs, hl, rank)


def _ffn_kernel(x_ref, w1_ref, w3_ref, w2_ref, y_ref, acc_ref):
    f = pl.program_id(2)
    x = x_ref[...].reshape(-1, D)
    a = _dot(x, w1_ref[0, 0].astype(BF16))
    g3 = _dot(x, w3_ref[0, 0].astype(BF16))
    hid = (_silu(a) * g3).astype(BF16)
    contrib = _dot(hid, w2_ref[0, 0].astype(BF16))

    @pl.when(f == 0)
    def _():
        acc_ref[...] = contrib

    @pl.when(f > 0)
    def _():
        acc_ref[...] += contrib

    @pl.when(f == pl.num_programs(2) - 1)
    def _():
        y_ref[...] = acc_ref[...].astype(BF16).reshape(y_ref.shape)


def _ffn(xs, w1, w3, w2, *, layer, mb, tf):
    B, E, cap_pad, _ = xs.shape
    return pl.pallas_call(
        _ffn_kernel,
        out_shape=jax.ShapeDtypeStruct(xs.shape, BF16),
        grid=(E, B // mb, D_FF // tf),
        in_specs=[pl.BlockSpec((mb, 1, cap_pad, D), lambda e, m, f: (m, e, 0, 0)),
                  pl.BlockSpec((1, 1, D, tf), lambda e, m, f: (layer, e, 0, f)),
                  pl.BlockSpec((1, 1, D, tf), lambda e, m, f: (layer, e, 0, f)),
                  pl.BlockSpec((1, 1, tf, D), lambda e, m, f: (layer, e, f, 0))],
        out_specs=pl.BlockSpec((mb, 1, cap_pad, D), lambda e, m, f: (m, e, 0, 0)),
        scratch_shapes=[pltpu.VMEM((mb * cap_pad, D), F32)],
        compiler_params=_cp(("arbitrary",) * 3, VMEM_LIMIT),
        name="expert_ffn",
    )(xs, w1, w3, w2)


CCOL = 256


def _combine_kernel(offs_ref, x_ref, y_ref, rankc_ref, probc_ref, ada_ref, o_ref, tot_scr, *, per, is_ctx):
    b, tb = pl.program_id(0), pl.program_id(2)
    gt2 = ada_ref[4 if is_ctx else b][5:6]
    rc_all = rankc_ref[0]
    pc_all = probc_ref[0]
    cap_pad = y_ref.shape[2]
    win = min(2 * SLOT, cap_pad)
    slotw = lax.broadcasted_iota(jnp.int32, (1, win), 1).astype(F32)
    slot = lax.broadcasted_iota(jnp.int32, (1, SLOT), 1).astype(F32)

    bases, his = [], []
    total = jnp.zeros(tot_scr.shape, F32)
    for e in range(N_EXPERTS):
        lo = offs_ref[b, e, tb * per]
        his.append(offs_ref[b, e, (tb + 1) * per])
        base = pl.multiple_of(jnp.minimum((lo // SLOT) * SLOT, cap_pad - win), SLOT)
        bases.append(base)
        onehot = jnp.where(rc_all[:, e:e + 1] == slotw + base.astype(F32), 1.0, 0.0).astype(BF16)
        total = total + pc_all[:, e:e + 1] * _dot(onehot, y_ref[0, e, pl.ds(base, win), :])
    tot_scr[...] = total

    for e in range(N_EXPERTS):
        end = bases[e] + win

        @pl.when(his[e] > end)
        def _(e=e, end=end):
            def body(t, carry):
                base = pl.multiple_of(t * SLOT, SLOT)
                onehot = jnp.where(rc_all[:, e:e + 1] == slot + base.astype(F32), 1.0, 0.0).astype(BF16)
                tot_scr[...] += pc_all[:, e:e + 1] * _dot(onehot, y_ref[0, e, pl.ds(base, SLOT), :])
                return carry

            lax.fori_loop(end // SLOT, (his[e] - 1) // SLOT + 1, body, 0)

    o_ref[0] = x_ref[0] + gt2 * tot_scr[...]


def _combine(offs, xu, ys, rank_c, prob_c, ada_l, *, tb_tok, tb0, n, is_ctx):
    B = xu.shape[0]
    cap_pad = ys.shape[2]
    per = tb_tok // TOK
    return pl.pallas_call(
        functools.partial(_combine_kernel, per=per, is_ctx=is_ctx),
        out_shape=jax.ShapeDtypeStruct(xu.shape, F32),
        grid_spec=pltpu.PrefetchScalarGridSpec(
            num_scalar_prefetch=1,
            grid=(B, D // CCOL, n // tb_tok),
            in_specs=[pl.BlockSpec((1, tb_tok, CCOL), lambda b, c, t, o: (b, tb0 + t, c)),
                      pl.BlockSpec((1, N_EXPERTS, cap_pad, CCOL), lambda b, c, t, o: (b, 0, 0, c)),
                      pl.BlockSpec((1, tb_tok, N_EXPERTS), lambda b, c, t, o: (b, t, 0)),
                      pl.BlockSpec((1, tb_tok, N_EXPERTS), lambda b, c, t, o: (b, t, 0)),
                      pl.BlockSpec((8, ADA_CHUNKS, CCOL), lambda b, c, t, o: (0, 0, c))],
            out_specs=pl.BlockSpec((1, tb_tok, CCOL), lambda b, c, t, o: (b, tb0 + t, c)),
            scratch_shapes=[pltpu.VMEM((tb_tok, CCOL), F32)]),
        input_output_aliases={1: 0},
        compiler_params=_cp(("arbitrary",) * 3, VMEM_LIMIT),
        name="expert_combine",
    )(offs, xu, ys, rank_c, prob_c, ada_l)


def _moe(xu, hl, pt, ada_l, w1, w3, w2, *, layer, row0, is_ctx):
    B, _, n = pt.shape
    cap = EC_CAPACITY * n // N_EXPERTS
    cap_pad = -(-cap // SLOT) * SLOT
    nb = n // TOK
    rank, offs = _select(pt, cap=cap)
    offs = offs[:, :, :nb + 1]
    gt = min(n, 1024)
    ct = min(n, 512)
    xs = _gather(offs, hl, rank, tb_tok=gt, tb0=row0 // gt, n=n, cap_pad=cap_pad, eg=4)
    mb = 2 if (B % 2 == 0 and cap_pad >= 1024) else (B if cap_pad < 1024 else 1)
    ys = _ffn(xs, w1, w3, w2, layer=layer, mb=mb, tf=256)
    rank_c = jnp.swapaxes(rank, 1, 2)
    prob_c = jnp.swapaxes(pt, 1, 2)
    return _combine(offs, xu, ys, rank_c, prob_c, ada_l, tb_tok=ct, tb0=row0 // ct, n=n, is_ctx=is_ctx)


def _final_kernel(x_ref, g_ref, o_ref):
    x = x_ref[0]
    r = lax.rsqrt(jnp.mean(x * x, axis=-1, keepdims=True) + EPS)
    o_ref[0] = (x * r) * g_ref[...]


def _final_norm(xu, g, *, n):
    B = xu.shape[0]
    tm = 512
    return pl.pallas_call(
        _final_kernel,
        out_shape=jax.ShapeDtypeStruct((B, n, D), F32),
        grid=(B, n // tm),
        in_specs=[pl.BlockSpec((1, tm, D), lambda b, t: (b, t, 0)),
                  pl.BlockSpec((1, D), lambda b, t: (0, 0))],
        out_specs=pl.BlockSpec((1, tm, D), lambda b, t: (b, t, 0)),
        compiler_params=_cp(("arbitrary", "arbitrary")),
        name="final_norm",
    )(xu, g)


def _rope_tables(n, ctx):
    rows = n // GRID_W
    t_row = jnp.repeat(jnp.arange(rows), GRID_W)
    t_col = jnp.tile(jnp.arange(GRID_W), rows)
    nf = DA_DIM // 4
    inv = ROPE_THETA ** (-jnp.arange(nf, dtype=F32) / nf)
    ar = t_row[:, None].astype(F32) * inv
    ac = t_col[:, None].astype(F32) * inv
    ang = jnp.concatenate([ar, ar, ac, ac], axis=-1)
    sign = jnp.where((jnp.arange(DA_DIM) % 16) < 8, -1.0, 1.0).astype(F32)
    cos = jnp.concatenate([jnp.cos(ang), jnp.ones((ctx, DA_DIM), F32)], axis=0)
    sin = jnp.concatenate([jnp.sin(ang) * sign, jnp.zeros((ctx, DA_DIM), F32)], axis=0)
    return jnp.tile(cos, (1, 128 // DA_DIM)), jnp.tile(sin, (1, 128 // DA_DIM))


def _pad_heads_cols(w):
    lead = w.shape[:-1]
    w = w.reshape(lead + (M_HEADS, M_DIM))
    w = jnp.pad(w, [(0, 0)] * len(lead) + [(0, 0), (0, M_PAD - M_DIM)])
    return w.reshape(lead + (MP_WIDTH,))


def _kv_tile(nt):
    for cand in (1024, 768, 512, 256):
        if nt % cand == 0:
            return cand
    raise ValueError(nt)


def kernel(x, c, ctx, c_ctx, ada_w, ada_b, norm1_g, norm2_g, w_in, four_w, m_conv_w, m_conv_b, m_gate_b,
           m_norm_g, d_lam, d_norm_g, w_out, router_w, exp_w1, exp_w3, exp_w2, final_g):
    B, N, _ = x.shape
    CTX = ctx.shape[1]
    depth = w_in.shape[0]
    assert CTX == TOK and N % (FFT_N1 * TOK) == 0 and B <= 4
    NT = N + CTX
    PAD = 1024 - CTX
    n_lat = N // TOK
    n2 = N // FFT_N1

    xu = jnp.concatenate([x, ctx, jnp.zeros((B, PAD, D), F32)], axis=1)
    cvecs = jnp.zeros((8, D), F32).at[:B].set(c).at[4].set(c_ctx)
    ada = _adaln(cvecs, ada_w, ada_b).reshape(depth, 8, ADA_CHUNKS, D)
    cos_t, sin_t = _rope_tables(N, CTX + PAD)
    tabs = _fourier_tables(N, CTX)
    tk = _kv_tile(NT)
    tq = 1024

    hl = jnp.zeros((B, NT + PAD, D), BF16)
    for layer in range(depth):
        ctx_out = layer < depth - 1
        lam_init = 0.8 - 0.6 * math.exp(-0.3 * layer)
        w = w_in[layer]
        wm = jnp.concatenate([w[:, OFF_F:OFF_DQ], w[:, OFF_DQ:OFF_MO], w[:, OFF_DK:OFF_DV],
                              _pad_heads_cols(w[:, OFF_MO:OFF_MQ]), _pad_heads_cols(w[:, OFF_MV:OFF_G])],
                             axis=1).astype(BF16)
        wvt = w[:, OFF_DV:OFF_MV].T.astype(BF16)
        wc = jnp.concatenate([_pad_heads_cols(w[:, OFF_MQ:OFF_MK]), _pad_heads_cols(w[:, OFF_MK:OFF_DK])],
                             axis=1).astype(BF16)
        wg = jnp.pad(w[:, OFF_G:], ((0, 0), (0, 128 - N_GATES)))
        gb = jnp.pad(m_gate_b[layer], (0, 128 - N_GATES)).reshape(1, 128)
        cw = jnp.concatenate([_pad_heads_cols(m_conv_w[layer][:, :M_WIDTH]),
                              _pad_heads_cols(m_conv_w[layer][:, M_WIDTH:])], axis=1)
        cb = jnp.concatenate([_pad_heads_cols(m_conv_b[layer][:M_WIDTH]),
                              _pad_heads_cols(m_conv_b[layer][M_WIDTH:])]).reshape(1, 2 * MP_WIDTH)
        ada_l = ada[layer]

        y4, dq, dk, dvT, mo, mq, mk, mv, gl = _inproj(
            xu, ada_l, norm1_g[layer].reshape(1, D), wm, wc, wvt, wg, gb, tabs["cs"], cos_t, sin_t, cw, cb,
            n_lat=n_lat, n2=n2)

        wblk = jnp.zeros((F_WIDTH, F_WIDTH), F32)
        for g in range(F_GROUPS):
            wblk = wblk.at[F_GDIM * g:F_GDIM * (g + 1), F_GDIM * g:F_GDIM * (g + 1)].set(four_w[layer, g])
        f_l, f_c = _fourier(y4, tabs, wblk.astype(BF16), n=N, ctx=CTX, with_ctx=ctx_out)

        dlam = d_lam[layer]
        g2 = d_norm_g[layer].reshape(DA_VDIM, 1)
        da_l = _attention(dq, dk, dvT, dlam, g2, lam_init=lam_init, tq=tq, q0=0, nq=N // tq,
                          tk=tk, k0=0, nk=NT // tk)

        glT = jnp.swapaxes(gl[:, :, :N_GATES], 1, 2)
        hf, hb = _mlstm(mq, mk, mv, gl, glT, n_lat=n_lat)

        mg = _pad_heads_cols(m_norm_g[layer]).reshape(1, MP_WIDTH)
        wol = w_out[layer]
        wo = jnp.concatenate([wol[:F_WIDTH],
                              jnp.pad(wol[F_WIDTH + DA_WIDTH:].reshape(M_HEADS, M_DIM, D),
                                      ((0, 0), (0, M_PAD - M_DIM), (0, 0))).reshape(MP_WIDTH, D)],
                             axis=0).astype(BF16)
        wod = wol[F_WIDTH:F_WIDTH + DA_WIDTH].astype(BF16)
        g2n = norm2_g[layer].reshape(1, D)
        wrT = router_w[layer].T
        xu, hl, pt_l = _outproj(xu, f_l, da_l, hf, hb, mo, ada_l, mg, wo, wod, g2n, wrT, hl,
                                t0=0, ntl=n_lat, is_ctx=False)
        if ctx_out:
            da_c = _attention(dq, dk, dvT, dlam, g2, lam_init=lam_init, tq=TOK, q0=n_lat, nq=1,
                              tk=TOK, k0=n_lat, nk=1)
            xu, hl, pt_c = _outproj(xu, f_c, da_c, hf, hb, mo, ada_l, mg, wo, wod, g2n, wrT, hl,
                                    t0=n_lat, ntl=1, is_ctx=True)

        xu = _moe(xu, hl, pt_l, ada_l, exp_w1, exp_w3, exp_w2, layer=layer, row0=0, is_ctx=False)
        if ctx_out:
            xu = _moe(xu, hl, pt_c, ada_l, exp_w1, exp_w3, exp_w2, layer=layer, row0=N, is_ctx=True)

    return _final_norm(xu, final_g.reshape(1, D), n=N)
```

```python
import functools
import math

import numpy as np
import jax
import jax.numpy as jnp
from jax import lax
from jax.experimental import pallas as pl
from jax.experimental.pallas import tpu as pltpu

F32 = jnp.float32
BF16 = jnp.bfloat16
HI = lax.Precision.HIGHEST

D = 1024
EPS = 1e-6
GRID_W = 64
ROPE_THETA = 10000.0
F_GROUPS, F_GDIM = 4, 64
F_WIDTH = F_GROUPS * F_GDIM
DA_HEADS, DA_DIM = 6, 32
DA_VDIM = 2 * DA_DIM
DA_WIDTH = DA_HEADS * DA_VDIM
M_HEADS, M_DIM = 4, 96
M_WIDTH = M_HEADS * M_DIM
M_PAD = 128
MP_WIDTH = M_HEADS * M_PAD
N_GATES = 4 * M_HEADS
N_EXPERTS = 16
EC_CAPACITY = 2
D_FF = 2 * D
ADA_CHUNKS = 6

TOK = 256
FFT_N1 = 16
SLOT = 128
NEG = -1e30

OFF_F = 0
OFF_DQ = OFF_F + F_WIDTH
OFF_MO = OFF_DQ + 2 * DA_HEADS * DA_DIM
OFF_MQ = OFF_MO + M_WIDTH
OFF_MK = OFF_MQ + M_WIDTH
OFF_DK = OFF_MK + M_WIDTH
OFF_DV = OFF_DK + 2 * DA_HEADS * DA_DIM
OFF_MV = OFF_DV + DA_HEADS * DA_VDIM
OFF_G = OFF_MV + M_WIDTH

VMEM_LIMIT = 56 * 1024 * 1024


def _cp(sem, vmem=None):
    return pltpu.CompilerParams(dimension_semantics=sem, vmem_limit_bytes=vmem)


def _sigmoid(x):
    return 1.0 / (1.0 + jnp.exp(-x))


def _silu(x):
    return x * _sigmoid(x)


def _dot(a, b, precision=None):
    return jnp.dot(a, b, preferred_element_type=F32, precision=precision)


def _dot_nt(a, b, precision=None):
    return lax.dot_general(a, b, (((1,), (1,)), ((), ())), preferred_element_type=F32,
                           precision=precision)


def _ada_kernel(c_ref, w_ref, b_ref, o_ref):
    c = c_ref[...]
    o_ref[0] = _dot(_silu(c), w_ref[0], HI) + b_ref[0]


def _adaln(cvecs, ada_w, ada_b):
    depth = ada_w.shape[0]
    tn = 1536
    return pl.pallas_call(
        _ada_kernel,
        out_shape=jax.ShapeDtypeStruct((depth, 8, ADA_CHUNKS * D), F32),
        grid=(depth, ADA_CHUNKS * D // tn),
        in_specs=[pl.BlockSpec((8, D), lambda l, j: (0, 0)),
                  pl.BlockSpec((1, D, tn), lambda l, j: (l, 0, j)),
                  pl.BlockSpec((1, 1, tn), lambda l, j: (l, 0, j))],
        out_specs=pl.BlockSpec((1, 8, tn), lambda l, j: (l, 0, j)),
        compiler_params=_cp(("arbitrary", "arbitrary")),
        name="adaln",
    )(cvecs, ada_w, ada_b.reshape(depth, 1, ADA_CHUNKS * D))


def _inproj_kernel(x_ref, xp_ref, xn_ref, ada_ref, g_ref, wm_ref, wc_ref, wvt_ref, wg_ref, gb_ref, cs_ref,
                   cos_ref, sin_ref, cw_ref, cb_ref,
                   y_ref, dq_ref, dk_ref, dvt_ref, mo_ref, mq_ref, mk_ref, mv_ref, gl_ref, *, n_lat):
    b = pl.program_id(0)
    t = pl.program_id(1)
    n_tiles = pl.num_programs(1)
    is_ctx = t >= n_lat
    row = jnp.where(is_ctx, 4, b)
    mod = ada_ref[row]
    sh, sc = mod[0:1], mod[1:2]

    xa = jnp.concatenate([xp_ref[0], x_ref[0], xn_ref[0]], axis=0)
    r = lax.rsqrt(jnp.mean(xa * xa, axis=-1, keepdims=True) + EPS)
    ha = (xa * r) * g_ref[...] * (1.0 + sc) + sh
    h = ha[8:8 + TOK]
    hb = h.astype(BF16)

    pm = _dot(hb, wm_ref[...])
    o = 0
    pf = pm[:, o:o + F_WIDTH]; o += F_WIDTH
    q = pm[:, o:o + DA_WIDTH]; o += DA_WIDTH
    k = pm[:, o:o + DA_WIDTH]; o += DA_WIDTH
    mo = pm[:, o:o + MP_WIDTH]; o += MP_WIDTH
    mv = pm[:, o:o + MP_WIDTH]
    dvt_ref[0] = _dot_nt(wvt_ref[...], hb).astype(BF16)

    y_ref[0, 0] = _dot(pf, cs_ref[...], HI)

    cos = cos_ref[...]
    sin = sin_ref[...]
    lane = lax.broadcasted_iota(jnp.int32, (1, 128), 1)
    low = (lane % 16) < 8

    def rope(z):
        parts = []
        for c in range(DA_WIDTH // 128):
            zc = z[:, 128 * c:128 * (c + 1)]
            rot = jnp.where(low, pltpu.roll(zc, 120, 1), pltpu.roll(zc, 8, 1))
            parts.append(zc * cos + rot * sin)
        return jnp.concatenate(parts, axis=1)

    dq_ref[0] = (rope(q) * (DA_DIM ** -0.5 * math.log2(math.e))).astype(BF16)
    dk_ref[0] = rope(k).astype(BF16)
    mo_ref[0] = mo.astype(BF16)
    mv_ref[0] = mv.astype(BF16)

    gpre = _dot(h, wg_ref[...], HI) + gb_ref[...]
    gl = lax.broadcasted_iota(jnp.int32, (1, 128), 1)
    is_forget = (gl % 8) >= 4
    logsig = jnp.minimum(gpre, 0.0) - jnp.log(1.0 + jnp.exp(-jnp.abs(gpre)))
    gl_ref[0] = jnp.where(is_forget, logsig, gpre)

    pc = _dot(ha.astype(BF16), wc_ref[...])
    first = (t == 0) | (t == n_lat)
    last = (t == n_lat - 1) | (t == n_tiles - 1)
    ridx = lax.broadcasted_iota(jnp.int32, (TOK + 16, 1), 0)
    pc = jnp.where(((ridx < 8) & first) | ((ridx >= TOK + 8) & last), 0.0, pc)
    cw = cw_ref[...]
    conv = cb_ref[...] + pc[7:7 + TOK] * cw[0:1] + pc[8:8 + TOK] * cw[1:2] + pc[9:9 + TOK] * cw[2:3]
    act = _silu(conv)
    mq_ref[0] = act[:, :MP_WIDTH].astype(BF16)
    mk_ref[0] = (act[:, MP_WIDTH:] * (M_DIM ** -0.5)).astype(BF16)


def _inproj(xu, ada_l, g1, wm, wc, wvt, wg, gb, cs, cos_t, sin_t, cw, cb, *, n_lat, n2):
    B, NT, _ = xu.shape
    nt = n_lat + 1
    rper = n2 // TOK
    tok3 = lambda w: pl.BlockSpec((1, TOK, w), lambda b, t: (b, t, 0))
    full = lambda a: pl.BlockSpec(a.shape, lambda b, t: (0,) * a.ndim)
    nb8 = NT // 8
    outs = [jax.ShapeDtypeStruct((B, 2 * FFT_N1, n2, 2 * F_WIDTH), F32)]
    outs += [jax.ShapeDtypeStruct((B, NT, DA_WIDTH), BF16), jax.ShapeDtypeStruct((B, nt * TOK, DA_WIDTH), BF16)]
    outs += [jax.ShapeDtypeStruct((B, DA_WIDTH, nt * TOK), BF16)]
    outs += [jax.ShapeDtypeStruct((B, NT, MP_WIDTH), BF16)] * 4
    outs += [jax.ShapeDtypeStruct((B, NT, 128), F32)]
    out_specs = [pl.BlockSpec((1, 1, TOK, 2 * F_WIDTH), lambda b, t: (b, t // rper, t % rper, 0))]
    out_specs += [tok3(DA_WIDTH)] * 2 + [pl.BlockSpec((1, DA_WIDTH, TOK), lambda b, t: (b, 0, t))]
    out_specs += [tok3(MP_WIDTH)] * 4 + [tok3(128)]
    return pl.pallas_call(
        functools.partial(_inproj_kernel, n_lat=n_lat),
        out_shape=outs,
        grid=(B, nt),
        in_specs=[tok3(D),
                  pl.BlockSpec((1, 8, D), lambda b, t: (b, jnp.maximum(t * (TOK // 8) - 1, 0), 0)),
                  pl.BlockSpec((1, 8, D), lambda b, t: (b, jnp.minimum((t + 1) * (TOK // 8), nb8 - 1), 0)),
                  full(ada_l), full(g1), full(wm), full(wc), full(wvt), full(wg), full(gb), full(cs),
                  pl.BlockSpec((TOK, 128), lambda b, t: (t, 0)),
                  pl.BlockSpec((TOK, 128), lambda b, t: (t, 0)),
                  full(cw), full(cb)],
        out_specs=out_specs,
        compiler_params=_cp(("arbitrary", "arbitrary"), VMEM_LIMIT),
        name="norm1_inproj",
    )(xu, xu, xu, ada_l, g1, wm, wc, wvt, wg, gb, cs, cos_t, sin_t, cw, cb)


def _fft1_kernel(y_ref, kc_ref, ks_ref, tc_ref, ts_ref, o_ref, *, groups):
    for g in range(groups):
        blk = y_ref[0, :, 8 * g:8 * (g + 1), :].reshape(FFT_N1 * 8, 2 * F_WIDTH)
        p = _dot(kc_ref[...], blk, HI)
        q = _dot(ks_ref[...], blk, HI)
        ar = p[:, :F_WIDTH] - q[:, F_WIDTH:]
        ai = -p[:, F_WIDTH:] - q[:, :F_WIDTH]
        tc = tc_ref[128 * g:128 * (g + 1), :]
        ts = ts_ref[128 * g:128 * (g + 1), :]
        tc = jnp.concatenate([tc, tc], axis=1)
        ts = jnp.concatenate([ts, ts], axis=1)
        br = ar * tc + ai * ts
        bi = ai * tc - ar * ts
        o_ref[0, :, 8 * g:8 * (g + 1), :] = jnp.concatenate([br, bi], axis=1).reshape(FFT_N1, 8, 2 * F_WIDTH)


def _fft2_kernel(b_ref, c2_ref, s2_ref, wb_ref, perm_ref, o_ref, r_scr, *, n2):
    for i in range(8):
        blk = b_ref[0, i]
        xr = _dot(c2_ref[...], blk[:, :F_WIDTH], HI) + _dot(s2_ref[...], blk[:, F_WIDTH:], HI)
        r_scr[i] = _dot(xr.astype(BF16), wb_ref[...])
    for t in range(n2 // 32):
        rows = jnp.concatenate([r_scr[i, 32 * t:32 * (t + 1), :] for i in range(8)], axis=0)
        o_ref[0, 32 * t:32 * (t + 1), :, :] = _dot(perm_ref[...], rows, HI).reshape(32, 8, F_WIDTH)


def _fftc_kernel(y_ref, c_ref, s_ref, wb_ref, o_ref):
    y = y_ref[0, 0]
    z = _dot(c_ref[...], y[:, :F_WIDTH], HI) - _dot(s_ref[...], y[:, F_WIDTH:], HI)
    o_ref[0] = _dot(z.astype(BF16), wb_ref[...])


def _fourier_tables(n, ctx):
    n1, n2 = FFT_N1, n // FFT_N1
    a = np.arange(n1)
    ang1 = 2 * np.pi * np.outer(a, a) / n1
    eye8 = np.eye(8)
    kc = np.kron(np.cos(ang1), eye8)
    ks = np.kron(np.sin(ang1), eye8)
    n2i = np.arange(n2).reshape(n2 // 8, 1, 8)
    k1 = np.arange(n1).reshape(1, n1, 1)
    angt = (2 * np.pi * n2i * k1 / n).reshape(-1, 1)
    tc = np.broadcast_to(np.cos(angt), (n2 // 8 * 128, 128))
    ts = np.broadcast_to(np.sin(angt), (n2 // 8 * 128, 128))
    b = np.arange(n2)
    ang2 = 2 * np.pi * np.outer(b, b) / n2
    c2 = np.cos(ang2) / math.sqrt(n)
    s2 = np.sin(ang2) / math.sqrt(n)
    perm = np.zeros((256, 256))
    for kk in range(8):
        for j in range(32):
            perm[j * 8 + kk, kk * 32 + j] = 1.0
    cc = np.arange(ctx)
    angc = 2 * np.pi * np.outer(cc, cc) / ctx
    cctx = np.cos(angc) / math.sqrt(ctx)
    sctx = np.sin(angc) / math.sqrt(ctx)
    ch = np.arange(F_GDIM)
    angch = 2 * np.pi * np.outer(ch, ch) / F_GDIM
    cs = np.concatenate([np.kron(np.eye(F_GROUPS), np.cos(angch)),
                         np.kron(np.eye(F_GROUPS), np.sin(angch))], axis=1) / math.sqrt(F_GDIM)
    f = lambda z: jnp.asarray(np.ascontiguousarray(z), dtype=F32)
    return dict(kc=f(kc), ks=f(ks), tc=f(tc), ts=f(ts), c2=f(c2), s2=f(s2), perm=f(perm),
                cctx=f(cctx), sctx=f(sctx), cs=f(cs))


def _fourier(y4, tabs, wblk, *, n, ctx, with_ctx):
    B = y4.shape[0]
    n2 = n // FFT_N1
    groups = 4
    full = lambda a, nd: pl.BlockSpec(a.shape, lambda *i: (0,) * a.ndim)
    b4 = pl.pallas_call(
        functools.partial(_fft1_kernel, groups=groups),
        out_shape=jax.ShapeDtypeStruct((B, FFT_N1, n2, 2 * F_WIDTH), F32),
        grid=(B, n2 // (8 * groups)),
        in_specs=[pl.BlockSpec((1, FFT_N1, 8 * groups, 2 * F_WIDTH), lambda b, j: (b, 0, j, 0)),
                  full(tabs["kc"], 2), full(tabs["ks"], 2),
                  pl.BlockSpec((128 * groups, 128), lambda b, j: (j, 0)),
                  pl.BlockSpec((128 * groups, 128), lambda b, j: (j, 0))],
        out_specs=pl.BlockSpec((1, FFT_N1, 8 * groups, 2 * F_WIDTH), lambda b, j: (b, 0, j, 0)),
        compiler_params=_cp(("arbitrary", "arbitrary")),
        name="fourier_stage1",
    )(y4, tabs["kc"], tabs["ks"], tabs["tc"], tabs["ts"])
    f4 = pl.pallas_call(
        functools.partial(_fft2_kernel, n2=n2),
        out_shape=jax.ShapeDtypeStruct((B, n2, 16, F_WIDTH), F32),
        grid=(B, FFT_N1 // 8),
        in_specs=[pl.BlockSpec((1, 8, n2, 2 * F_WIDTH), lambda b, j: (b, j, 0, 0)),
                  full(tabs["c2"], 2), full(tabs["s2"], 2), full(wblk, 2), full(tabs["perm"], 2)],
        out_specs=pl.BlockSpec((1, n2, 8, F_WIDTH), lambda b, j: (b, 0, j, 0)),
        scratch_shapes=[pltpu.VMEM((8, n2, F_WIDTH), F32)],
        compiler_params=_cp(("arbitrary", "arbitrary"), VMEM_LIMIT),
        name="fourier_stage2",
    )(b4, tabs["c2"], tabs["s2"], wblk, tabs["perm"])
    f_ctx = None
    if with_ctx:
        f_ctx = pl.pallas_call(
            _fftc_kernel,
            out_shape=jax.ShapeDtypeStruct((B, ctx, F_WIDTH), F32),
            grid=(B,),
            in_specs=[pl.BlockSpec((1, 1, TOK, 2 * F_WIDTH), lambda b: (b, FFT_N1, 0, 0)),
                      full(tabs["cctx"], 1), full(tabs["sctx"], 1), full(wblk, 1)],
            out_specs=pl.BlockSpec((1, ctx, F_WIDTH), lambda b: (b, 0, 0)),
            compiler_params=_cp(("arbitrary",)),
            name="fourier_ctx",
        )(y4, tabs["cctx"], tabs["sctx"], wblk)
    return f4.reshape(B, n, F_WIDTH), f_ctx


VROWS = DA_VDIM + 16


def _attn_kernel(q_ref, k_ref, vt_ref, dl_ref, g_ref, o_ref, m_scr, acc_scr, *, lam_init):
    kt = pl.program_id(3)
    nk = pl.num_programs(3)

    @pl.when(kt == 0)
    def _():
        m_scr[...] = jnp.full(m_scr.shape, NEG, F32)
        acc_scr[...] = jnp.zeros(acc_scr.shape, F32)

    q = q_ref[0]
    k = k_ref[0]
    vt = vt_ref[0]
    ones = jnp.ones((16, vt.shape[1]), BF16)
    lhs = [jnp.concatenate([vt[DA_VDIM * h:DA_VDIM * (h + 1)], ones], axis=0) for h in range(2)]
    lane = lax.broadcasted_iota(jnp.int32, (1, 128), 1)
    zero = jnp.zeros((), BF16)
    def scores(j):
        return _dot_nt(k, jnp.where((lane // DA_DIM) == j, q, zero))

    st_next = scores(0)
    for j in range(4):
        st = st_next
        if j < 3:
            st_next = scores(j + 1)
        m_old = m_scr[j]
        m_new = jnp.maximum(m_old, jnp.max(st, axis=0, keepdims=True))
        alpha = jnp.exp2(m_old - m_new)
        pt = jnp.exp2(st - m_new).astype(BF16)
        acc_scr[j] = alpha * acc_scr[j] + _dot(lhs[j // 2], pt)
        m_scr[j] = m_new

    @pl.when(kt == nk - 1)
    def _():
        dl = dl_ref[...]
        lam = (jnp.exp(jnp.sum(dl[0:1] * dl[1:2], keepdims=True))
               - jnp.exp(jnp.sum(dl[2:3] * dl[3:4], keepdims=True)) + lam_init)
        outs = []
        for h in range(2):
            a0 = acc_scr[2 * h]
            a1 = acc_scr[2 * h + 1]
            o = (a0[:DA_VDIM] / a0[DA_VDIM:DA_VDIM + 1]
                 - lam * (a1[:DA_VDIM] / a1[DA_VDIM:DA_VDIM + 1]))
            r = lax.rsqrt(jnp.mean(o * o, axis=0, keepdims=True) + EPS)
            outs.append(((o * r) * g_ref[...]) * (1.0 - lam_init))
        o_ref[0] = jnp.concatenate(outs, axis=0).astype(BF16)


def _attention(dq, dk, dvT, dlam, gcol, *, lam_init, tq, q0, nq, tk, k0, nk):
    B = dq.shape[0]
    return pl.pallas_call(
        functools.partial(_attn_kernel, lam_init=lam_init),
        out_shape=jax.ShapeDtypeStruct((B, DA_WIDTH, nq * tq), BF16),
        grid=(B, DA_WIDTH // 128, nq, nk),
        in_specs=[pl.BlockSpec((1, tq, 128), lambda b, p, i, j: (b, q0 + i, p)),
                  pl.BlockSpec((1, tk, 128), lambda b, p, i, j: (b, k0 + j, p)),
                  pl.BlockSpec((1, 128, tk), lambda b, p, i, j: (b, p, k0 + j)),
                  pl.BlockSpec(dlam.shape, lambda b, p, i, j: (0, 0)),
                  pl.BlockSpec(gcol.shape, lambda b, p, i, j: (0, 0))],
        out_specs=pl.BlockSpec((1, 128, tq), lambda b, p, i, j: (b, p, i)),
        scratch_shapes=[pltpu.VMEM((4, 1, tq), F32), pltpu.VMEM((4, VROWS, tq), F32)],
        compiler_params=_cp(("arbitrary",) * 4, VMEM_LIMIT),
        name="diff_attention",
    )(dq, dk, dvT, dlam, gcol)


def _mlstm_kernel(qf_ref, kf_ref, vf_ref, gcf_ref, grf_ref, qb_ref, kb_ref, vb_ref, gcb_ref, grb_ref,
                  hf_ref, hb_ref, c_scr, n_scr, m_scr):
    t = pl.program_id(1)

    @pl.when(t == 0)
    def _():
        c_scr[...] = jnp.zeros(c_scr.shape, F32)
        n_scr[...] = jnp.zeros(n_scr.shape, F32)
        m_scr[...] = jnp.zeros(m_scr.shape, F32)

    L = TOK
    ri = lax.broadcasted_iota(jnp.int32, (L, L), 0)
    ci = lax.broadcasted_iota(jnp.int32, (L, L), 1)
    dirs = ((qf_ref, kf_ref, vf_ref, gcf_ref, grf_ref, hf_ref, ci <= ri, L - 1),
            (qb_ref, kb_ref, vb_ref, gcb_ref, grb_ref, hb_ref, ci >= ri, 0))
    for d, (q_ref, k_ref, v_ref, gc_ref, gr_ref, h_ref, seen, last_row) in enumerate(dirs):
        seen_f = jnp.where(seen, 1.0, 0.0)
        gc = gc_ref[0]
        gr = gr_ref[0]
        bcols = _dot(seen_f, gc, HI)
        brows = _dot_nt(gr, seen_f, HI)
        for hd in range(M_HEADS):
            idx = d * M_HEADS + hd
            ji = d * 8 + hd
            jf = d * 8 + 4 + hd
            sl = slice(M_PAD * hd, M_PAD * (hd + 1))
            q = q_ref[0, :, sl]
            k = k_ref[0, :, sl]
            v = v_ref[0, :, sl]
            bc = bcols[:, jf:jf + 1]
            br = brows[jf:jf + 1, :]
            lic = gc[:, ji:ji + 1]
            lir = gr[ji:ji + 1, :]
            m_old = m_scr[idx][0:1, 0:1]
            c_old = c_scr[idx]
            n_old = n_scr[idx][0:1, :]

            dlog = jnp.where(seen, bc - br + lir, NEG)
            inter = bc + m_old
            m_t = jnp.maximum(inter, jnp.max(dlog, axis=1, keepdims=True))
            w_inter = jnp.exp(inter - m_t)
            s = _dot_nt(q, k) * jnp.exp(dlog - m_t)
            num = w_inter * _dot_nt(q, c_old.astype(BF16)) + _dot(s.astype(BF16), v)
            qf32 = q.astype(F32)
            den = w_inter * jnp.sum(qf32 * n_old, axis=1, keepdims=True) + jnp.sum(s, axis=1, keepdims=True)
            h_ref[0, :, sl] = num / jnp.maximum(jnp.abs(den), jnp.exp(-m_t))

            total = bcols[last_row:last_row + 1, jf:jf + 1]
            wlog_c = total - bc + lic
            m_new = jnp.maximum(total + m_old, jnp.max(wlog_c, axis=0, keepdims=True))
            decay = jnp.exp(total + m_old - m_new)
            wc = jnp.exp(wlog_c - m_new)
            kf32 = k.astype(F32)
            vw = (v.astype(F32) * wc).astype(BF16)
            c_scr[idx] = decay * c_old + lax.dot_general(vw, k, (((0,), (0,)), ((), ())),
                                                         preferred_element_type=F32)
            n_new = decay * n_old + jnp.sum(kf32 * wc, axis=0, keepdims=True)
            n_scr[idx] = jnp.broadcast_to(n_new, (8, M_PAD))
            m_scr[idx] = jnp.broadcast_to(m_new, (8, 128))


def _mlstm(mq, mk, mv, gl, glT, *, n_lat):
    B, NT, _ = mq.shape
    nt = n_lat + 1
    fwd = lambda t: jnp.where(t == 0, n_lat, t - 1)
    bwd = lambda t: jnp.where(t == 0, n_lat, n_lat - t)
    tok = lambda w, f: pl.BlockSpec((1, TOK, w), lambda b, t: (b, f(t), 0))
    lanes = lambda f: pl.BlockSpec((1, N_GATES, TOK), lambda b, t: (b, 0, f(t)))
    ins, specs = [], []
    for f in (fwd, bwd):
        ins += [mq, mk, mv, gl, glT]
        specs += [tok(MP_WIDTH, f)] * 3 + [tok(128, f), lanes(f)]
    return pl.pallas_call(
        _mlstm_kernel,
        out_shape=[jax.ShapeDtypeStruct((B, NT, MP_WIDTH), F32)] * 2,
        grid=(B, nt),
        in_specs=specs,
        out_specs=[tok(MP_WIDTH, fwd), tok(MP_WIDTH, bwd)],
        scratch_shapes=[pltpu.VMEM((2 * M_HEADS, M_PAD, M_PAD), F32),
                        pltpu.VMEM((2 * M_HEADS, 8, M_PAD), F32),
                        pltpu.VMEM((2 * M_HEADS, 8, 128), F32)],
        compiler_params=_cp(("arbitrary", "arbitrary"), VMEM_LIMIT),
        name="mlstm",
    )(*ins)


def _outproj_kernel(x_ref, f_ref, dat_ref, hf_ref, hb_ref, mo_ref, ada_ref, mg_ref, wo_ref, wod_ref, g2_ref, wr_ref,
                    xo_ref, hl_ref, pt_ref, *, is_ctx):
    b = pl.program_id(0)
    mod = ada_ref[4 if is_ctx else b]
    gt1, sh2, sc2 = mod[2:3], mod[3:4], mod[4:5]
    hs = hf_ref[0] + hb_ref[0]
    og = mo_ref[0].astype(F32)
    mg = mg_ref[...]
    parts = [f_ref[0].astype(BF16)]
    for hd in range(M_HEADS):
        sl = slice(M_PAD * hd, M_PAD * (hd + 1))
        hh = hs[:, sl]
        r = lax.rsqrt(jnp.sum(hh * hh, axis=1, keepdims=True) * (1.0 / M_DIM) + EPS)
        parts.append((((hh * r) * mg[:, sl]) * _sigmoid(og[:, sl])).astype(BF16))
    mix = jnp.concatenate(parts, axis=1)
    upd = _dot(mix, wo_ref[...]) + lax.dot_general(dat_ref[0], wod_ref[...], (((0,), (0,)), ((), ())),
                                                   preferred_element_type=F32)
    xn = x_ref[0] + gt1 * upd
    xo_ref[0] = xn
    r = lax.rsqrt(jnp.mean(xn * xn, axis=-1, keepdims=True) + EPS)
    h2 = (xn * r) * g2_ref[...] * (1.0 + sc2) + sh2
    hl_ref[0] = h2.astype(BF16)
    lt = _dot_nt(wr_ref[...], h2, HI)
    ex = jnp.exp(lt - jnp.max(lt, axis=0, keepdims=True))
    pt_ref[0] = ex / jnp.sum(ex, axis=0, keepdims=True)


def _outproj_kernel_aliased(x_ref, f_ref, dat_ref, hf_ref, hb_ref, mo_ref, ada_ref, mg_ref, wo_ref, wod_ref,
                            g2_ref, wr_ref, hlp_ref, xo_ref, hl_ref, pt_ref, *, is_ctx):
    del hlp_ref
    _outproj_kernel(x_ref, f_ref, dat_ref, hf_ref, hb_ref, mo_ref, ada_ref, mg_ref, wo_ref, wod_ref, g2_ref,
                    wr_ref, xo_ref, hl_ref, pt_ref, is_ctx=is_ctx)


def _outproj(xu, f, daT, hf, hb, mo, ada_l, mg, wo, wod, g2, wrT, hl_prev, *, t0, ntl, is_ctx):
    B, NT, _ = xu.shape
    tok = lambda w: pl.BlockSpec((1, TOK, w), lambda b, t: (b, t0 + t, 0))
    loc = lambda w: pl.BlockSpec((1, TOK, w), lambda b, t: (b, t, 0))
    full = lambda a: pl.BlockSpec(a.shape, lambda b, t: (0,) * a.ndim)
    return pl.pallas_call(
        functools.partial(_outproj_kernel_aliased, is_ctx=is_ctx),
        out_shape=[jax.ShapeDtypeStruct(xu.shape, F32), jax.ShapeDtypeStruct((B, NT, D), BF16),
                   jax.ShapeDtypeStruct((B, N_EXPERTS, ntl * TOK), F32)],
        grid=(B, ntl),
        in_specs=[tok(D), loc(F_WIDTH), pl.BlockSpec((1, DA_WIDTH, TOK), lambda b, t: (b, 0, t)),
                  tok(MP_WIDTH), tok(MP_WIDTH), tok(MP_WIDTH),
                  full(ada_l), full(mg), full(wo), full(wod), full(g2), full(wrT),
                  pl.BlockSpec(memory_space=pl.ANY)],
        out_specs=[tok(D), tok(D), pl.BlockSpec((1, N_EXPERTS, TOK), lambda b, t: (b, 0, t))],
        input_output_aliases={0: 0, 12: 1},
        compiler_params=_cp(("arbitrary", "arbitrary"), VMEM_LIMIT),
        name="outproj_norm2_router",
    )(xu, f, daT, hf, hb, mo, ada_l, mg, wo, wod, g2, wrT, hl_prev)


def _select_kernel(p_ref, rank_ref, offs_ref, *, n, cap):
    p = p_ref[0]
    xi = pltpu.bitcast(p, jnp.int32)

    def body(i, lo):
        cand = lo | jnp.left_shift(jnp.int32(1), 30 - i)
        cnt = jnp.sum(jnp.where(xi >= cand, 1.0, 0.0), axis=1, keepdims=True)
        return jnp.where(cnt >= cap, cand, lo)

    thr = lax.fori_loop(0, 31, body, jnp.zeros((N_EXPERTS, 1), jnp.int32))
    nb = n // TOK
    rows = lax.broadcasted_iota(jnp.int32, (n, 128), 0)
    cols = lax.broadcasted_iota(jnp.int32, (n, 128), 1)
    blk_ind = jnp.where((rows // TOK) == cols, 1.0, 0.0).astype(BF16)
    u128 = jnp.where(lax.broadcasted_iota(jnp.int32, (128, 128), 0)
                     < lax.broadcasted_iota(jnp.int32, (128, 128), 1), 1.0, 0.0).astype(BF16)
    utok = jnp.where(lax.broadcasted_iota(jnp.int32, (TOK, TOK), 0)
                     < lax.broadcasted_iota(jnp.int32, (TOK, TOK), 1), 1.0, 0.0).astype(BF16)

    def prefix(mf):
        mb = mf.astype(BF16)
        counts = _dot(mb, blk_ind)
        offs = _dot(counts.astype(BF16), u128)
        pieces = [_dot(mb[:, TOK * j:TOK * (j + 1)], utok) + offs[:, j:j + 1] for j in range(nb)]
        return (jnp.concatenate(pieces, axis=1) if nb > 1 else pieces[0]), offs

    gt = xi > thr
    eq = xi == thr
    need = cap - jnp.sum(jnp.where(gt, 1.0, 0.0), axis=1, keepdims=True)
    rank_eq, _ = prefix(jnp.where(eq, 1.0, 0.0))
    sel = gt | (eq & (rank_eq < need))
    rank, offs = prefix(jnp.where(sel, 1.0, 0.0))
    rank_ref[0] = jnp.where(sel, rank, -1.0)
    offs_ref[0] = offs.astype(jnp.int32)


def _select(pt, *, cap):
    B, _, n = pt.shape
    return pl.pallas_call(
        functools.partial(_select_kernel, n=n, cap=cap),
        out_shape=[jax.ShapeDtypeStruct((B, N_EXPERTS, n), F32),
                   jax.ShapeDtypeStruct((B, N_EXPERTS, 128), jnp.int32)],
        grid=(B,),
        in_specs=[pl.BlockSpec((1, N_EXPERTS, n), lambda b: (b, 0, 0))],
        out_specs=[pl.BlockSpec((1, N_EXPERTS, n), lambda b: (b, 0, 0)),
                   pl.BlockSpec((1, N_EXPERTS, 128), lambda b: (b, 0, 0))],
        compiler_params=_cp(("arbitrary",), VMEM_LIMIT),
        name="expert_choice_select",
    )(pt)


def _gather_kernel(offs_ref, h_ref, rank_ref, o_ref, *, eg, per):
    b, g, tb = pl.program_id(0), pl.program_id(1), pl.program_id(2)

    @pl.when(tb == 0)
    def _():
        o_ref[...] = jnp.zeros(o_ref.shape, BF16)

    h = h_ref[0]
    ntok = h.shape[0]
    cap_pad = o_ref.shape[2]
    win = min(2 * SLOT, cap_pad)

    def add_rows(i, r, base, width):
        slots = lax.broadcasted_iota(jnp.int32, (width, ntok), 0).astype(F32) + base.astype(F32)
        onehot = jnp.where(r == slots, 1.0, 0.0).astype(BF16)
        rows = _dot(onehot, h).astype(BF16)
        o_ref[0, i, pl.ds(base, width), :] = o_ref[0, i, pl.ds(base, width), :] + rows

    rs, ends, his = [], [], []
    for i in range(eg):
        e = g * eg + i
        r = rank_ref[0, pl.ds(e, 1), :]
        lo = offs_ref[b, e, tb * per]
        his.append(offs_ref[b, e, (tb + 1) * per])
        base = pl.multiple_of(jnp.minimum((lo // SLOT) * SLOT, cap_pad - win), SLOT)
        add_rows(i, r, base, win)
        rs.append(r)
        ends.append(base + win)

    for i in range(eg):
        @pl.when(his[i] > ends[i])
        def _(i=i):
            def body(t, carry):
                add_rows(i, rs[i], pl.multiple_of(t * SLOT, SLOT), SLOT)
                return carry

            lax.fori_loop(ends[i] // SLOT, (his[i] - 1) // SLOT + 1, body, 0)


def _gather(offs, hl, rank, *, tb_tok, tb0, n, cap_pad, eg):
    B = hl.shape[0]
    per = tb_tok // TOK
    return pl.pallas_call(
        functools.partial(_gather_kernel, eg=eg, per=per),
        out_shape=jax.ShapeDtypeStruct((B, N_EXPERTS, cap_pad, D), BF16),
        grid_spec=pltpu.PrefetchScalarGridSpec(
            num_scalar_prefetch=1,
            grid=(B, N_EXPERTS // eg, n // tb_tok),
            in_specs=[pl.BlockSpec((1, tb_tok, D), lambda b, g, t, o: (b, tb0 + t, 0)),
                      pl.BlockSpec((1, N_EXPERTS, tb_tok), lambda b, g, t, o: (b, 0, t))],
            out_specs=pl.BlockSpec((1, eg, cap_pad, D), lambda b, g, t, o: (b, g, 0, 0))),
        compiler_params=_cp(("arbitrary",) * 3, VMEM_LIMIT),
        name="expert_gather",
    )(offs, hl, rank)


def _ffn_kernel(x_ref, w1_ref, w3_ref, w2_ref, y_ref, acc_ref):
    f = pl.program_id(2)
    x = x_ref[...].reshape(-1, D)
    a = _dot(x, w1_ref[0, 0].astype(BF16))
    g3 = _dot(x, w3_ref[0, 0].astype(BF16))
    hid = (_silu(a) * g3).astype(BF16)

    @pl.when(f == 0)
    def _():
        acc_ref[...] = jnp.zeros(acc_ref.shape, F32)

    acc_ref[...] += _dot(hid, w2_ref[0, 0].astype(BF16))

    @pl.when(f == pl.num_programs(2) - 1)
    def _():
        y_ref[...] = acc_ref[...].astype(BF16).reshape(y_ref.shape)


def _ffn(xs, w1, w3, w2, *, layer, mb, tf):
    B, E, cap_pad, _ = xs.shape
    return pl.pallas_call(
        _ffn_kernel,
        out_shape=jax.ShapeDtypeStruct(xs.shape, BF16),
        grid=(E, B // mb, D_FF // tf),
        in_specs=[pl.BlockSpec((mb, 1, cap_pad, D), lambda e, m, f: (m, e, 0, 0)),
                  pl.BlockSpec((1, 1, D, tf), lambda e, m, f: (layer, e, 0, f)),
                  pl.BlockSpec((1, 1, D, tf), lambda e, m, f: (layer, e, 0, f)),
                  pl.BlockSpec((1, 1, tf, D), lambda e, m, f: (layer, e, f, 0))],
        out_specs=pl.BlockSpec((mb, 1, cap_pad, D), lambda e, m, f: (m, e, 0, 0)),
        scratch_shapes=[pltpu.VMEM((mb * cap_pad, D), F32)],
        compiler_params=_cp(("arbitrary",) * 3, VMEM_LIMIT),
        name="expert_ffn",
    )(xs, w1, w3, w2)


CCOL = 256


def _combine_kernel(offs_ref, x_ref, y_ref, rankc_ref, probc_ref, ada_ref, o_ref, tot_scr, *, per, is_ctx):
    b, tb = pl.program_id(0), pl.program_id(2)
    gt2 = ada_ref[4 if is_ctx else b][5:6]
    rc_all = rankc_ref[0]
    pc_all = probc_ref[0]
    cap_pad = y_ref.shape[2]
    win = min(2 * SLOT, cap_pad)
    slotw = lax.broadcasted_iota(jnp.int32, (1, win), 1).astype(F32)
    slot = lax.broadcasted_iota(jnp.int32, (1, SLOT), 1).astype(F32)

    bases, his = [], []
    total = jnp.zeros(tot_scr.shape, F32)
    for e in range(N_EXPERTS):
        lo = offs_ref[b, e, tb * per]
        his.append(offs_ref[b, e, (tb + 1) * per])
        base = pl.multiple_of(jnp.minimum((lo // SLOT) * SLOT, cap_pad - win), SLOT)
        bases.append(base)
        onehot = jnp.where(rc_all[:, e:e + 1] == slotw + base.astype(F32), 1.0, 0.0).astype(BF16)
        total = total + pc_all[:, e:e + 1] * _dot(onehot, y_ref[0, e, pl.ds(base, win), :])
    tot_scr[...] = total

    for e in range(N_EXPERTS):
        end = bases[e] + win

        @pl.when(his[e] > end)
        def _(e=e, end=end):
            def body(t, carry):
                base = pl.multiple_of(t * SLOT, SLOT)
                onehot = jnp.where(rc_all[:, e:e + 1] == slot + base.astype(F32), 1.0, 0.0).astype(BF16)
                tot_scr[...] += pc_all[:, e:e + 1] * _dot(onehot, y_ref[0, e, pl.ds(base, SLOT), :])
                return carry

            lax.fori_loop(end // SLOT, (his[e] - 1) // SLOT + 1, body, 0)

    o_ref[0] = x_ref[0] + gt2 * tot_scr[...]


def _combine(offs, xu, ys, rank_c, prob_c, ada_l, *, tb_tok, tb0, n, is_ctx):
    B = xu.shape[0]
    cap_pad = ys.shape[2]
    per = tb_tok // TOK
    return pl.pallas_call(
        functools.partial(_combine_kernel, per=per, is_ctx=is_ctx),
        out_shape=jax.ShapeDtypeStruct(xu.shape, F32),
        grid_spec=pltpu.PrefetchScalarGridSpec(
            num_scalar_prefetch=1,
            grid=(B, D // CCOL, n // tb_tok),
            in_specs=[pl.BlockSpec((1, tb_tok, CCOL), lambda b, c, t, o: (b, tb0 + t, c)),
                      pl.BlockSpec((1, N_EXPERTS, cap_pad, CCOL), lambda b, c, t, o: (b, 0, 0, c)),
                      pl.BlockSpec((1, tb_tok, N_EXPERTS), lambda b, c, t, o: (b, t, 0)),
                      pl.BlockSpec((1, tb_tok, N_EXPERTS), lambda b, c, t, o: (b, t, 0)),
                      pl.BlockSpec((8, ADA_CHUNKS, CCOL), lambda b, c, t, o: (0, 0, c))],
            out_specs=pl.BlockSpec((1, tb_tok, CCOL), lambda b, c, t, o: (b, tb0 + t, c)),
            scratch_shapes=[pltpu.VMEM((tb_tok, CCOL), F32)]),
        input_output_aliases={1: 0},
        compiler_params=_cp(("arbitrary",) * 3, VMEM_LIMIT),
        name="expert_combine",
    )(offs, xu, ys, rank_c, prob_c, ada_l)


def _moe(xu, hl, pt, ada_l, w1, w3, w2, *, layer, row0, is_ctx):
    B, _, n = pt.shape
    cap = EC_CAPACITY * n // N_EXPERTS
    cap_pad = -(-cap // SLOT) * SLOT
    nb = n // TOK
    rank, offs = _select(pt, cap=cap)
    offs = offs[:, :, :nb + 1]
    gt = min(n, 512)
    ct = min(n, 512)
    xs = _gather(offs, hl, rank, tb_tok=gt, tb0=row0 // gt, n=n, cap_pad=cap_pad, eg=4)
    mb = 2 if (B % 2 == 0 and cap_pad >= 1024) else (B if cap_pad < 1024 else 1)
    ys = _ffn(xs, w1, w3, w2, layer=layer, mb=mb, tf=512)
    rank_c = jnp.swapaxes(rank, 1, 2)
    prob_c = jnp.swapaxes(pt, 1, 2)
    return _combine(offs, xu, ys, rank_c, prob_c, ada_l, tb_tok=ct, tb0=row0 // ct, n=n, is_ctx=is_ctx)


def _final_kernel(x_ref, g_ref, o_ref):
    x = x_ref[0]
    r = lax.rsqrt(jnp.mean(x * x, axis=-1, keepdims=True) + EPS)
    o_ref[0] = (x * r) * g_ref[...]


def _final_norm(xu, g, *, n):
    B = xu.shape[0]
    tm = 512
    return pl.pallas_call(
        _final_kernel,
        out_shape=jax.ShapeDtypeStruct((B, n, D), F32),
        grid=(B, n // tm),
        in_specs=[pl.BlockSpec((1, tm, D), lambda b, t: (b, t, 0)),
                  pl.BlockSpec((1, D), lambda b, t: (0, 0))],
        out_specs=pl.BlockSpec((1, tm, D), lambda b, t: (b, t, 0)),
        compiler_params=_cp(("arbitrary", "arbitrary")),
        name="final_norm",
    )(xu, g)


def _rope_tables(n, ctx):
    rows = n // GRID_W
    t_row = jnp.repeat(jnp.arange(rows), GRID_W)
    t_col = jnp.tile(jnp.arange(GRID_W), rows)
    nf = DA_DIM // 4
    inv = ROPE_THETA ** (-jnp.arange(nf, dtype=F32) / nf)
    ar = t_row[:, None].astype(F32) * inv
    ac = t_col[:, None].astype(F32) * inv
    ang = jnp.concatenate([ar, ar, ac, ac], axis=-1)
    sign = jnp.where((jnp.arange(DA_DIM) % 16) < 8, -1.0, 1.0).astype(F32)
    cos = jnp.concatenate([jnp.cos(ang), jnp.ones((ctx, DA_DIM), F32)], axis=0)
    sin = jnp.concatenate([jnp.sin(ang) * sign, jnp.zeros((ctx, DA_DIM), F32)], axis=0)
    return jnp.tile(cos, (1, 128 // DA_DIM)), jnp.tile(sin, (1, 128 // DA_DIM))


def _pad_heads_cols(w):
    lead = w.shape[:-1]
    w = w.reshape(lead + (M_HEADS, M_DIM))
    w = jnp.pad(w, [(0, 0)] * len(lead) + [(0, 0), (0, M_PAD - M_DIM)])
    return w.reshape(lead + (MP_WIDTH,))


def _kv_tile(nt):
    for parts in range(1, nt // 128 + 1):
        if nt % parts == 0 and (nt // parts) % 128 == 0 and nt // parts <= 1408:
            return nt // parts
    raise ValueError(nt)


def kernel(x, c, ctx, c_ctx, ada_w, ada_b, norm1_g, norm2_g, w_in, four_w, m_conv_w, m_conv_b, m_gate_b,
           m_norm_g, d_lam, d_norm_g, w_out, router_w, exp_w1, exp_w3, exp_w2, final_g):
    B, N, _ = x.shape
    CTX = ctx.shape[1]
    depth = w_in.shape[0]
    assert CTX == TOK and N % (FFT_N1 * TOK) == 0 and B <= 4
    NT = N + CTX
    PAD = 1024 - CTX
    n_lat = N // TOK
    n2 = N // FFT_N1

    xu = jnp.concatenate([x, ctx, jnp.zeros((B, PAD, D), F32)], axis=1)
    cvecs = jnp.zeros((8, D), F32).at[:B].set(c).at[4].set(c_ctx)
    ada = _adaln(cvecs, ada_w, ada_b).reshape(depth, 8, ADA_CHUNKS, D)
    cos_t, sin_t = _rope_tables(N, CTX + PAD)
    tabs = _fourier_tables(N, CTX)
    tk = _kv_tile(NT)
    tq = 1024

    hl = jnp.zeros((B, NT + PAD, D), BF16)
    for layer in range(depth):
        ctx_out = layer < depth - 1
        lam_init = 0.8 - 0.6 * math.exp(-0.3 * layer)
        w = w_in[layer]
        wm = jnp.concatenate([w[:, OFF_F:OFF_DQ], w[:, OFF_DQ:OFF_MO], w[:, OFF_DK:OFF_DV],
                              _pad_heads_cols(w[:, OFF_MO:OFF_MQ]), _pad_heads_cols(w[:, OFF_MV:OFF_G])],
                             axis=1).astype(BF16)
        wvt = w[:, OFF_DV:OFF_MV].T.astype(BF16)
        wc = jnp.concatenate([_pad_heads_cols(w[:, OFF_MQ:OFF_MK]), _pad_heads_cols(w[:, OFF_MK:OFF_DK])],
                             axis=1).astype(BF16)
        wg = jnp.pad(w[:, OFF_G:], ((0, 0), (0, 128 - N_GATES)))
        gb = jnp.pad(m_gate_b[layer], (0, 128 - N_GATES)).reshape(1, 128)
        cw = jnp.concatenate([_pad_heads_cols(m_conv_w[layer][:, :M_WIDTH]),
                              _pad_heads_cols(m_conv_w[layer][:, M_WIDTH:])], axis=1)
        cb = jnp.concatenate([_pad_heads_cols(m_conv_b[layer][:M_WIDTH]),
                              _pad_heads_cols(m_conv_b[layer][M_WIDTH:])]).reshape(1, 2 * MP_WIDTH)
        ada_l = ada[layer]

        y4, dq, dk, dvT, mo, mq, mk, mv, gl = _inproj(
            xu, ada_l, norm1_g[layer].reshape(1, D), wm, wc, wvt, wg, gb, tabs["cs"], cos_t, sin_t, cw, cb,
            n_lat=n_lat, n2=n2)

        wblk = jnp.zeros((F_WIDTH, F_WIDTH), F32)
        for g in range(F_GROUPS):
            wblk = wblk.at[F_GDIM * g:F_GDIM * (g + 1), F_GDIM * g:F_GDIM * (g + 1)].set(four_w[layer, g])
        f_l, f_c = _fourier(y4, tabs, wblk.astype(BF16), n=N, ctx=CTX, with_ctx=ctx_out)

        dlam = d_lam[layer]
        g2 = d_norm_g[layer].reshape(DA_VDIM, 1)
        da_l = _attention(dq, dk, dvT, dlam, g2, lam_init=lam_init, tq=tq, q0=0, nq=N // tq,
                          tk=tk, k0=0, nk=NT // tk)

        glT = jnp.swapaxes(gl[:, :, :N_GATES], 1, 2)
        hf, hb = _mlstm(mq, mk, mv, gl, glT, n_lat=n_lat)

        mg = _pad_heads_cols(m_norm_g[layer]).reshape(1, MP_WIDTH)
        wol = w_out[layer]
        wo = jnp.concatenate([wol[:F_WIDTH],
                              jnp.pad(wol[F_WIDTH + DA_WIDTH:].reshape(M_HEADS, M_DIM, D),
                                      ((0, 0), (0, M_PAD - M_DIM), (0, 0))).reshape(MP_WIDTH, D)],
                             axis=0).astype(BF16)
        wod = wol[F_WIDTH:F_WIDTH + DA_WIDTH].astype(BF16)
        g2n = norm2_g[layer].reshape(1, D)
        wrT = router_w[layer].T
        xu, hl, pt_l = _outproj(xu, f_l, da_l, hf, hb, mo, ada_l, mg, wo, wod, g2n, wrT, hl,
                                t0=0, ntl=n_lat, is_ctx=False)
        if ctx_out:
            da_c = _attention(dq, dk, dvT, dlam, g2, lam_init=lam_init, tq=TOK, q0=n_lat, nq=1,
                              tk=TOK, k0=n_lat, nk=1)
            xu, hl, pt_c = _outproj(xu, f_c, da_c, hf, hb, mo, ada_l, mg, wo, wod, g2n, wrT, hl,
                                    t0=n_lat, ntl=1, is_ctx=True)

        xu = _moe(xu, hl, pt_l, ada_l, exp_w1, exp_w3, exp_w2, layer=layer, row0=0, is_ctx=False)
        if ctx_out:
            xu = _moe(xu, hl, pt_c, ada_l, exp_w1, exp_w3, exp_w2, layer=layer, row0=N, is_ctx=True)

    return _final_norm(xu, final_g.reshape(1, D), n=N)
```

```python
import functools
import math

import numpy as np
import jax
import jax.numpy as jnp
from jax import lax
from jax.experimental import pallas as pl
from jax.experimental.pallas import tpu as pltpu

F32 = jnp.float32
BF16 = jnp.bfloat16
HI = lax.Precision.HIGHEST

D = 1024
EPS = 1e-6
GRID_W = 64
ROPE_THETA = 10000.0
F_GROUPS, F_GDIM = 4, 64
F_WIDTH = F_GROUPS * F_GDIM
DA_HEADS, DA_DIM = 6, 32
DA_VDIM = 2 * DA_DIM
DA_WIDTH = DA_HEADS * DA_VDIM
M_HEADS, M_DIM = 4, 96
M_WIDTH = M_HEADS * M_DIM
M_PAD = 128
MP_WIDTH = M_HEADS * M_PAD
N_GATES = 4 * M_HEADS
N_EXPERTS = 16
EC_CAPACITY = 2
D_FF = 2 * D
ADA_CHUNKS = 6

TOK = 256
FFT_N1 = 16
SLOT = 128
NEG = -1e30

OFF_F = 0
OFF_DQ = OFF_F + F_WIDTH
OFF_MO = OFF_DQ + 2 * DA_HEADS * DA_DIM
OFF_MQ = OFF_MO + M_WIDTH
OFF_MK = OFF_MQ + M_WIDTH
OFF_DK = OFF_MK + M_WIDTH
OFF_DV = OFF_DK + 2 * DA_HEADS * DA_DIM
OFF_MV = OFF_DV + DA_HEADS * DA_VDIM
OFF_G = OFF_MV + M_WIDTH

VMEM_LIMIT = 56 * 1024 * 1024


def _cp(sem, vmem=None):
    return pltpu.CompilerParams(dimension_semantics=sem, vmem_limit_bytes=vmem)


def _sigmoid(x):
    return 1.0 / (1.0 + jnp.exp(-x))


def _silu(x):
    return x * _sigmoid(x)


def _dot(a, b, precision=None):
    return jnp.dot(a, b, preferred_element_type=F32, precision=precision)


def _dot_nt(a, b, precision=None):
    return lax.dot_general(a, b, (((1,), (1,)), ((), ())), preferred_element_type=F32,
                           precision=precision)


def _ada_kernel(c_ref, w_ref, b_ref, o_ref):
    c = c_ref[...]
    o_ref[0] = _dot(_silu(c), w_ref[0], HI) + b_ref[0]


def _adaln(cvecs, ada_w, ada_b):
    depth = ada_w.shape[0]
    tn = 1536
    return pl.pallas_call(
        _ada_kernel,
        out_shape=jax.ShapeDtypeStruct((depth, 8, ADA_CHUNKS * D), F32),
        grid=(depth, ADA_CHUNKS * D // tn),
        in_specs=[pl.BlockSpec((8, D), lambda l, j: (0, 0)),
                  pl.BlockSpec((1, D, tn), lambda l, j: (l, 0, j)),
                  pl.BlockSpec((1, 1, tn), lambda l, j: (l, 0, j))],
        out_specs=pl.BlockSpec((1, 8, tn), lambda l, j: (l, 0, j)),
        compiler_params=_cp(("arbitrary", "arbitrary")),
        name="adaln",
    )(cvecs, ada_w, ada_b.reshape(depth, 1, ADA_CHUNKS * D))


def _inproj_kernel(x_ref, xp_ref, xn_ref, ada_ref, g_ref, wm_ref, wc_ref, wvt_ref, wgt_ref, gb_ref, cs_ref,
                   cos_ref, sin_ref, cw_ref, cb_ref,
                   y_ref, dq_ref, dk_ref, dvt_ref, mo_ref, mq_ref, mk_ref, mv_ref, gl_ref, glt_ref, *, n_lat):
    b = pl.program_id(0)
    t = pl.program_id(1)
    n_tiles = pl.num_programs(1)
    is_ctx = t >= n_lat
    row = jnp.where(is_ctx, 4, b)
    mod = ada_ref[row]
    sh, sc = mod[0:1], mod[1:2]

    xa = jnp.concatenate([xp_ref[0], x_ref[0], xn_ref[0]], axis=0)
    r = lax.rsqrt(jnp.mean(xa * xa, axis=-1, keepdims=True) + EPS)
    ha = (xa * r) * g_ref[...] * (1.0 + sc) + sh
    h = ha[8:8 + TOK]
    hb = h.astype(BF16)

    pm = _dot(hb, wm_ref[...])
    o = 0
    pf = pm[:, o:o + F_WIDTH]; o += F_WIDTH
    q = pm[:, o:o + DA_WIDTH]; o += DA_WIDTH
    k = pm[:, o:o + DA_WIDTH]; o += DA_WIDTH
    mo = pm[:, o:o + MP_WIDTH]; o += MP_WIDTH
    mv = pm[:, o:o + MP_WIDTH]
    dvt_ref[0] = _dot_nt(wvt_ref[...], hb).astype(BF16)

    y_ref[0, 0] = _dot(pf, cs_ref[...], HI)

    cos = cos_ref[...]
    sin = sin_ref[...]
    lane = lax.broadcasted_iota(jnp.int32, (1, 128), 1)
    low = (lane % 16) < 8

    def rope(z):
        parts = []
        for c in range(DA_WIDTH // 128):
            zc = z[:, 128 * c:128 * (c + 1)]
            rot = jnp.where(low, pltpu.roll(zc, 120, 1), pltpu.roll(zc, 8, 1))
            parts.append(zc * cos + rot * sin)
        return jnp.concatenate(parts, axis=1)

    dq_ref[0] = (rope(q) * (DA_DIM ** -0.5 * math.log2(math.e))).astype(BF16)
    dk_ref[0] = rope(k).astype(BF16)
    mo_ref[0] = mo.astype(BF16)
    mv_ref[0] = mv.astype(BF16)

    gpre = _dot_nt(wgt_ref[...], h, HI) + gb_ref[...]
    is_forget = (lax.broadcasted_iota(jnp.int32, (N_GATES, 1), 0) % 8) >= 4
    logsig = jnp.minimum(gpre, 0.0) - jnp.log(1.0 + jnp.exp(-jnp.abs(gpre)))
    glt = jnp.where(is_forget, logsig, gpre)
    glt_ref[0] = glt
    gl_ref[0] = jnp.concatenate([glt, jnp.zeros((128 - N_GATES, TOK), F32)], axis=0).T

    pc = _dot(ha.astype(BF16), wc_ref[...])
    first = (t == 0) | (t == n_lat)
    last = (t == n_lat - 1) | (t == n_tiles - 1)
    ridx = lax.broadcasted_iota(jnp.int32, (TOK + 16, 1), 0)
    pc = jnp.where(((ridx < 8) & first) | ((ridx >= TOK + 8) & last), 0.0, pc)
    cw = cw_ref[...]
    conv = cb_ref[...] + pc[7:7 + TOK] * cw[0:1] + pc[8:8 + TOK] * cw[1:2] + pc[9:9 + TOK] * cw[2:3]
    act = _silu(conv)
    mq_ref[0] = act[:, :MP_WIDTH].astype(BF16)
    mk_ref[0] = (act[:, MP_WIDTH:] * (M_DIM ** -0.5)).astype(BF16)


def _inproj(xu, ada_l, g1, wm, wc, wvt, wgt, gb, cs, cos_t, sin_t, cw, cb, *, n_lat, n2):
    B, NT, _ = xu.shape
    nt = n_lat + 1
    rper = n2 // TOK
    tok3 = lambda w: pl.BlockSpec((1, TOK, w), lambda b, t: (b, t, 0))
    full = lambda a: pl.BlockSpec(a.shape, lambda b, t: (0,) * a.ndim)
    nb8 = NT // 8
    outs = [jax.ShapeDtypeStruct((B, 2 * FFT_N1, n2, 2 * F_WIDTH), F32)]
    outs += [jax.ShapeDtypeStruct((B, NT, DA_WIDTH), BF16), jax.ShapeDtypeStruct((B, nt * TOK, DA_WIDTH), BF16)]
    outs += [jax.ShapeDtypeStruct((B, DA_WIDTH, nt * TOK), BF16)]
    outs += [jax.ShapeDtypeStruct((B, NT, MP_WIDTH), BF16)] * 4
    outs += [jax.ShapeDtypeStruct((B, NT, 128), F32), jax.ShapeDtypeStruct((B, N_GATES, nt * TOK), F32)]
    out_specs = [pl.BlockSpec((1, 1, TOK, 2 * F_WIDTH), lambda b, t: (b, t // rper, t % rper, 0))]
    out_specs += [tok3(DA_WIDTH)] * 2 + [pl.BlockSpec((1, DA_WIDTH, TOK), lambda b, t: (b, 0, t))]
    out_specs += [tok3(MP_WIDTH)] * 4 + [tok3(128), pl.BlockSpec((1, N_GATES, TOK), lambda b, t: (b, 0, t))]
    return pl.pallas_call(
        functools.partial(_inproj_kernel, n_lat=n_lat),
        out_shape=outs,
        grid=(B, nt),
        in_specs=[tok3(D),
                  pl.BlockSpec((1, 8, D), lambda b, t: (b, jnp.maximum(t * (TOK // 8) - 1, 0), 0)),
                  pl.BlockSpec((1, 8, D), lambda b, t: (b, jnp.minimum((t + 1) * (TOK // 8), nb8 - 1), 0)),
                  full(ada_l), full(g1), full(wm), full(wc), full(wvt), full(wgt), full(gb), full(cs),
                  pl.BlockSpec((TOK, 128), lambda b, t: (t, 0)),
                  pl.BlockSpec((TOK, 128), lambda b, t: (t, 0)),
                  full(cw), full(cb)],
        out_specs=out_specs,
        compiler_params=_cp(("arbitrary", "arbitrary"), VMEM_LIMIT),
        name="norm1_inproj",
    )(xu, xu, xu, ada_l, g1, wm, wc, wvt, wgt, gb, cs, cos_t, sin_t, cw, cb)


def _fft1_kernel(y_ref, kc_ref, ks_ref, tc_ref, ts_ref, o_ref, *, groups):
    for g in range(groups):
        blk = y_ref[0, :, 8 * g:8 * (g + 1), :].reshape(FFT_N1 * 8, 2 * F_WIDTH)
        p = _dot(kc_ref[...], blk, HI)
        q = _dot(ks_ref[...], blk, HI)
        ar = p[:, :F_WIDTH] - q[:, F_WIDTH:]
        ai = -p[:, F_WIDTH:] - q[:, :F_WIDTH]
        tc = tc_ref[128 * g:128 * (g + 1), :]
        ts = ts_ref[128 * g:128 * (g + 1), :]
        tc = jnp.concatenate([tc, tc], axis=1)
        ts = jnp.concatenate([ts, ts], axis=1)
        br = ar * tc + ai * ts
        bi = ai * tc - ar * ts
        o_ref[0, :, 8 * g:8 * (g + 1), :] = jnp.concatenate([br, bi], axis=1).reshape(FFT_N1, 8, 2 * F_WIDTH)


def _fft2_kernel(b_ref, c2_ref, s2_ref, wb_ref, perm_ref, o_ref, r_scr, *, n2):
    for i in range(8):
        blk = b_ref[0, i]
        xr = _dot(c2_ref[...], blk[:, :F_WIDTH], HI) + _dot(s2_ref[...], blk[:, F_WIDTH:], HI)
        r_scr[i] = _dot(xr.astype(BF16), wb_ref[...])
    for t in range(n2 // 32):
        rows = jnp.concatenate([r_scr[i, 32 * t:32 * (t + 1), :] for i in range(8)], axis=0)
        o_ref[0, 32 * t:32 * (t + 1), :, :] = _dot(perm_ref[...], rows, HI).reshape(32, 8, F_WIDTH)


def _fftc_kernel(y_ref, c_ref, s_ref, wb_ref, o_ref):
    y = y_ref[0, 0]
    z = _dot(c_ref[...], y[:, :F_WIDTH], HI) - _dot(s_ref[...], y[:, F_WIDTH:], HI)
    o_ref[0] = _dot(z.astype(BF16), wb_ref[...])


def _fourier_tables(n, ctx):
    n1, n2 = FFT_N1, n // FFT_N1
    a = np.arange(n1)
    ang1 = 2 * np.pi * np.outer(a, a) / n1
    eye8 = np.eye(8)
    kc = np.kron(np.cos(ang1), eye8)
    ks = np.kron(np.sin(ang1), eye8)
    n2i = np.arange(n2).reshape(n2 // 8, 1, 8)
    k1 = np.arange(n1).reshape(1, n1, 1)
    angt = (2 * np.pi * n2i * k1 / n).reshape(-1, 1)
    tc = np.broadcast_to(np.cos(angt), (n2 // 8 * 128, 128))
    ts = np.broadcast_to(np.sin(angt), (n2 // 8 * 128, 128))
    b = np.arange(n2)
    ang2 = 2 * np.pi * np.outer(b, b) / n2
    c2 = np.cos(ang2) / math.sqrt(n)
    s2 = np.sin(ang2) / math.sqrt(n)
    perm = np.zeros((256, 256))
    for kk in range(8):
        for j in range(32):
            perm[j * 8 + kk, kk * 32 + j] = 1.0
    cc = np.arange(ctx)
    angc = 2 * np.pi * np.outer(cc, cc) / ctx
    cctx = np.cos(angc) / math.sqrt(ctx)
    sctx = np.sin(angc) / math.sqrt(ctx)
    ch = np.arange(F_GDIM)
    angch = 2 * np.pi * np.outer(ch, ch) / F_GDIM
    cs = np.concatenate([np.kron(np.eye(F_GROUPS), np.cos(angch)),
                         np.kron(np.eye(F_GROUPS), np.sin(angch))], axis=1) / math.sqrt(F_GDIM)
    f = lambda z: jnp.asarray(np.ascontiguousarray(z), dtype=F32)
    return dict(kc=f(kc), ks=f(ks), tc=f(tc), ts=f(ts), c2=f(c2), s2=f(s2), perm=f(perm),
                cctx=f(cctx), sctx=f(sctx), cs=f(cs))


def _fourier(y4, tabs, wblk, *, n, ctx, with_ctx):
    B = y4.shape[0]
    n2 = n // FFT_N1
    groups = 4
    full = lambda a, nd: pl.BlockSpec(a.shape, lambda *i: (0,) * a.ndim)
    b4 = pl.pallas_call(
        functools.partial(_fft1_kernel, groups=groups),
        out_shape=jax.ShapeDtypeStruct((B, FFT_N1, n2, 2 * F_WIDTH), F32),
        grid=(B, n2 // (8 * groups)),
        in_specs=[pl.BlockSpec((1, FFT_N1, 8 * groups, 2 * F_WIDTH), lambda b, j: (b, 0, j, 0)),
                  full(tabs["kc"], 2), full(tabs["ks"], 2),
                  pl.BlockSpec((128 * groups, 128), lambda b, j: (j, 0)),
                  pl.BlockSpec((128 * groups, 128), lambda b, j: (j, 0))],
        out_specs=pl.BlockSpec((1, FFT_N1, 8 * groups, 2 * F_WIDTH), lambda b, j: (b, 0, j, 0)),
        compiler_params=_cp(("arbitrary", "arbitrary")),
        name="fourier_stage1",
    )(y4, tabs["kc"], tabs["ks"], tabs["tc"], tabs["ts"])
    f4 = pl.pallas_call(
        functools.partial(_fft2_kernel, n2=n2),
        out_shape=jax.ShapeDtypeStruct((B, n2, 16, F_WIDTH), F32),
        grid=(B, FFT_N1 // 8),
        in_specs=[pl.BlockSpec((1, 8, n2, 2 * F_WIDTH), lambda b, j: (b, j, 0, 0)),
                  full(tabs["c2"], 2), full(tabs["s2"], 2), full(wblk, 2), full(tabs["perm"], 2)],
        out_specs=pl.BlockSpec((1, n2, 8, F_WIDTH), lambda b, j: (b, 0, j, 0)),
        scratch_shapes=[pltpu.VMEM((8, n2, F_WIDTH), F32)],
        compiler_params=_cp(("arbitrary", "arbitrary"), VMEM_LIMIT),
        name="fourier_stage2",
    )(b4, tabs["c2"], tabs["s2"], wblk, tabs["perm"])
    f_ctx = None
    if with_ctx:
        f_ctx = pl.pallas_call(
            _fftc_kernel,
            out_shape=jax.ShapeDtypeStruct((B, ctx, F_WIDTH), F32),
            grid=(B,),
            in_specs=[pl.BlockSpec((1, 1, TOK, 2 * F_WIDTH), lambda b: (b, FFT_N1, 0, 0)),
                      full(tabs["cctx"], 1), full(tabs["sctx"], 1), full(wblk, 1)],
            out_specs=pl.BlockSpec((1, ctx, F_WIDTH), lambda b: (b, 0, 0)),
            compiler_params=_cp(("arbitrary",)),
            name="fourier_ctx",
        )(y4, tabs["cctx"], tabs["sctx"], wblk)
    return f4.reshape(B, n, F_WIDTH), f_ctx


VROWS = DA_VDIM + 16


def _attn_kernel(q_ref, k_ref, vt_ref, dl_ref, g_ref, o_ref, m_scr, acc_scr, *, lam_init):
    kt = pl.program_id(3)
    nk = pl.num_programs(3)

    @pl.when(kt == 0)
    def _():
        m_scr[...] = jnp.full(m_scr.shape, NEG, F32)
        acc_scr[...] = jnp.zeros(acc_scr.shape, F32)

    q = q_ref[0]
    k = k_ref[0]
    vt = vt_ref[0]
    ones = jnp.ones((16, vt.shape[1]), BF16)
    lhs = [jnp.concatenate([vt[DA_VDIM * h:DA_VDIM * (h + 1)], ones], axis=0) for h in range(2)]
    lane = lax.broadcasted_iota(jnp.int32, (1, 128), 1)
    zero = jnp.zeros((), BF16)
    def scores(j):
        return _dot_nt(k, jnp.where((lane // DA_DIM) == j, q, zero))

    st_next = scores(0)
    for j in range(4):
        st = st_next
        if j < 3:
            st_next = scores(j + 1)
        m_old = m_scr[j]
        m_new = jnp.maximum(m_old, jnp.max(st, axis=0, keepdims=True))
        alpha = jnp.exp2(m_old - m_new)
        pt = jnp.exp2(st - m_new).astype(BF16)
        acc_scr[j] = alpha * acc_scr[j] + _dot(lhs[j // 2], pt)
        m_scr[j] = m_new

    @pl.when(kt == nk - 1)
    def _():
        dl = dl_ref[...]
        lam = (jnp.exp(jnp.sum(dl[0:1] * dl[1:2], keepdims=True))
               - jnp.exp(jnp.sum(dl[2:3] * dl[3:4], keepdims=True)) + lam_init)
        outs = []
        for h in range(2):
            a0 = acc_scr[2 * h]
            a1 = acc_scr[2 * h + 1]
            o = (a0[:DA_VDIM] / a0[DA_VDIM:DA_VDIM + 1]
                 - lam * (a1[:DA_VDIM] / a1[DA_VDIM:DA_VDIM + 1]))
            r = lax.rsqrt(jnp.mean(o * o, axis=0, keepdims=True) + EPS)
            outs.append(((o * r) * g_ref[...]) * (1.0 - lam_init))
        o_ref[0] = jnp.concatenate(outs, axis=0).astype(BF16)


def _attention(dq, dk, dvT, dlam, gcol, *, lam_init, tq, q0, nq, tk, k0, nk):
    B = dq.shape[0]
    return pl.pallas_call(
        functools.partial(_attn_kernel, lam_init=lam_init),
        out_shape=jax.ShapeDtypeStruct((B, DA_WIDTH, nq * tq), BF16),
        grid=(B, DA_WIDTH // 128, nq, nk),
        in_specs=[pl.BlockSpec((1, tq, 128), lambda b, p, i, j: (b, q0 + i, p)),
                  pl.BlockSpec((1, tk, 128), lambda b, p, i, j: (b, k0 + j, p)),
                  pl.BlockSpec((1, 128, tk), lambda b, p, i, j: (b, p, k0 + j)),
                  pl.BlockSpec(dlam.shape, lambda b, p, i, j: (0, 0)),
                  pl.BlockSpec(gcol.shape, lambda b, p, i, j: (0, 0))],
        out_specs=pl.BlockSpec((1, 128, tq), lambda b, p, i, j: (b, p, i)),
        scratch_shapes=[pltpu.VMEM((4, 1, tq), F32), pltpu.VMEM((4, VROWS, tq), F32)],
        compiler_params=_cp(("arbitrary",) * 4, VMEM_LIMIT),
        name="diff_attention",
    )(dq, dk, dvT, dlam, gcol)


def _mlstm_kernel(qf_ref, kf_ref, vf_ref, gcf_ref, grf_ref, qb_ref, kb_ref, vb_ref, gcb_ref, grb_ref,
                  hf_ref, hb_ref, c_scr, n_scr, m_scr):
    t = pl.program_id(1)

    @pl.when(t == 0)
    def _():
        c_scr[...] = jnp.zeros(c_scr.shape, F32)
        n_scr[...] = jnp.zeros(n_scr.shape, F32)
        m_scr[...] = jnp.zeros(m_scr.shape, F32)

    L = TOK
    ri = lax.broadcasted_iota(jnp.int32, (L, L), 0)
    ci = lax.broadcasted_iota(jnp.int32, (L, L), 1)
    dirs = ((qf_ref, kf_ref, vf_ref, gcf_ref, grf_ref, hf_ref, ci <= ri, L - 1),
            (qb_ref, kb_ref, vb_ref, gcb_ref, grb_ref, hb_ref, ci >= ri, 0))
    for d, (q_ref, k_ref, v_ref, gc_ref, gr_ref, h_ref, seen, last_row) in enumerate(dirs):
        seen_f = jnp.where(seen, 1.0, 0.0)
        gc = gc_ref[0]
        gr = gr_ref[0]
        bcols = _dot(seen_f, gc, HI)
        brows = _dot_nt(gr, seen_f, HI)
        for hd in range(M_HEADS):
            idx = d * M_HEADS + hd
            ji = d * 8 + hd
            jf = d * 8 + 4 + hd
            sl = slice(M_PAD * hd, M_PAD * (hd + 1))
            q = q_ref[0, :, sl]
            k = k_ref[0, :, sl]
            v = v_ref[0, :, sl]
            bc = bcols[:, jf:jf + 1]
            br = brows[jf:jf + 1, :]
            lic = gc[:, ji:ji + 1]
            lir = gr[ji:ji + 1, :]
            m_old = m_scr[idx][0:1, 0:1]
            c_old = c_scr[idx]
            n_old = n_scr[idx][0:1, :]

            dlog = jnp.where(seen, bc - br + lir, NEG)
            inter = bc + m_old
            m_t = jnp.maximum(inter, jnp.max(dlog, axis=1, keepdims=True))
            w_inter = jnp.exp(inter - m_t)
            s = _dot_nt(q, k) * jnp.exp(dlog - m_t)
            num = w_inter * _dot_nt(q, c_old.astype(BF16)) + _dot(s.astype(BF16), v)
            qf32 = q.astype(F32)
            den = w_inter * jnp.sum(qf32 * n_old, axis=1, keepdims=True) + jnp.sum(s, axis=1, keepdims=True)
            h_ref[0, :, sl] = num / jnp.maximum(jnp.abs(den), jnp.exp(-m_t))

            total = bcols[last_row:last_row + 1, jf:jf + 1]
            wlog_c = total - bc + lic
            m_new = jnp.maximum(total + m_old, jnp.max(wlog_c, axis=0, keepdims=True))
            decay = jnp.exp(total + m_old - m_new)
            wc = jnp.exp(wlog_c - m_new)
            kf32 = k.astype(F32)
            vw = (v.astype(F32) * wc).astype(BF16)
            c_scr[idx] = decay * c_old + lax.dot_general(vw, k, (((0,), (0,)), ((), ())),
                                                         preferred_element_type=F32)
            n_new = decay * n_old + jnp.sum(kf32 * wc, axis=0, keepdims=True)
            n_scr[idx] = jnp.broadcast_to(n_new, (8, M_PAD))
            m_scr[idx] = jnp.broadcast_to(m_new, (8, 128))


def _mlstm(mq, mk, mv, gl, glT, *, n_lat):
    B, NT, _ = mq.shape
    nt = n_lat + 1
    fwd = lambda t: jnp.where(t == 0, n_lat, t - 1)
    bwd = lambda t: jnp.where(t == 0, n_lat, n_lat - t)
    tok = lambda w, f: pl.BlockSpec((1, TOK, w), lambda b, t: (b, f(t), 0))
    lanes = lambda f: pl.BlockSpec((1, N_GATES, TOK), lambda b, t: (b, 0, f(t)))
    ins, specs = [], []
    for f in (fwd, bwd):
        ins += [mq, mk, mv, gl, glT]
        specs += [tok(MP_WIDTH, f)] * 3 + [tok(128, f), lanes(f)]
    return pl.pallas_call(
        _mlstm_kernel,
        out_shape=[jax.ShapeDtypeStruct((B, NT, MP_WIDTH), F32)] * 2,
        grid=(B, nt),
        in_specs=specs,
        out_specs=[tok(MP_WIDTH, fwd), tok(MP_WIDTH, bwd)],
        scratch_shapes=[pltpu.VMEM((2 * M_HEADS, M_PAD, M_PAD), F32),
                        pltpu.VMEM((2 * M_HEADS, 8, M_PAD), F32),
                        pltpu.VMEM((2 * M_HEADS, 8, 128), F32)],
        compiler_params=_cp(("arbitrary", "arbitrary"), VMEM_LIMIT),
        name="mlstm",
    )(*ins)


def _outproj_kernel(x_ref, f_ref, dat_ref, hf_ref, hb_ref, mo_ref, ada_ref, mg_ref, wo_ref, wod_ref, g2_ref, wr_ref,
                    xo_ref, hl_ref, pt_ref, *, is_ctx):
    b = pl.program_id(0)
    mod = ada_ref[4 if is_ctx else b]
    gt1, sh2, sc2 = mod[2:3], mod[3:4], mod[4:5]
    hs = hf_ref[0] + hb_ref[0]
    og = mo_ref[0].astype(F32)
    mg = mg_ref[...]
    parts = [f_ref[0].astype(BF16)]
    for hd in range(M_HEADS):
        sl = slice(M_PAD * hd, M_PAD * (hd + 1))
        hh = hs[:, sl]
        r = lax.rsqrt(jnp.sum(hh * hh, axis=1, keepdims=True) * (1.0 / M_DIM) + EPS)
        parts.append((((hh * r) * mg[:, sl]) * _sigmoid(og[:, sl])).astype(BF16))
    mix = jnp.concatenate(parts, axis=1)
    upd = _dot(mix, wo_ref[...]) + lax.dot_general(dat_ref[0], wod_ref[...], (((0,), (0,)), ((), ())),
                                                   preferred_element_type=F32)
    xn = x_ref[0] + gt1 * upd
    xo_ref[0] = xn
    r = lax.rsqrt(jnp.mean(xn * xn, axis=-1, keepdims=True) + EPS)
    h2 = (xn * r) * g2_ref[...] * (1.0 + sc2) + sh2
    hl_ref[0] = h2.astype(BF16)
    lt = _dot_nt(wr_ref[...], h2, HI)
    ex = jnp.exp(lt - jnp.max(lt, axis=0, keepdims=True))
    pt_ref[0] = ex / jnp.sum(ex, axis=0, keepdims=True)


def _outproj_kernel_aliased(x_ref, f_ref, dat_ref, hf_ref, hb_ref, mo_ref, ada_ref, mg_ref, wo_ref, wod_ref,
                            g2_ref, wr_ref, hlp_ref, xo_ref, hl_ref, pt_ref, *, is_ctx):
    del hlp_ref
    _outproj_kernel(x_ref, f_ref, dat_ref, hf_ref, hb_ref, mo_ref, ada_ref, mg_ref, wo_ref, wod_ref, g2_ref,
                    wr_ref, xo_ref, hl_ref, pt_ref, is_ctx=is_ctx)


def _outproj(xu, f, daT, hf, hb, mo, ada_l, mg, wo, wod, g2, wrT, hl_prev, *, t0, ntl, is_ctx):
    B, NT, _ = xu.shape
    tok = lambda w: pl.BlockSpec((1, TOK, w), lambda b, t: (b, t0 + t, 0))
    loc = lambda w: pl.BlockSpec((1, TOK, w), lambda b, t: (b, t, 0))
    full = lambda a: pl.BlockSpec(a.shape, lambda b, t: (0,) * a.ndim)
    return pl.pallas_call(
        functools.partial(_outproj_kernel_aliased, is_ctx=is_ctx),
        out_shape=[jax.ShapeDtypeStruct(xu.shape, F32), jax.ShapeDtypeStruct((B, NT, D), BF16),
                   jax.ShapeDtypeStruct((B, N_EXPERTS, ntl * TOK), F32)],
        grid=(B, ntl),
        in_specs=[tok(D), loc(F_WIDTH), pl.BlockSpec((1, DA_WIDTH, TOK), lambda b, t: (b, 0, t)),
                  tok(MP_WIDTH), tok(MP_WIDTH), tok(MP_WIDTH),
                  full(ada_l), full(mg), full(wo), full(wod), full(g2), full(wrT),
                  pl.BlockSpec(memory_space=pl.ANY)],
        out_specs=[tok(D), tok(D), pl.BlockSpec((1, N_EXPERTS, TOK), lambda b, t: (b, 0, t))],
        input_output_aliases={0: 0, 12: 1},
        compiler_params=_cp(("arbitrary", "arbitrary"), VMEM_LIMIT),
        name="outproj_norm2_router",
    )(xu, f, daT, hf, hb, mo, ada_l, mg, wo, wod, g2, wrT, hl_prev)


def _select_kernel(p_ref, rank_ref, offs_ref, *, n, cap):
    p = p_ref[0]
    xi = pltpu.bitcast(p, jnp.int32)

    def body(i, lo):
        cand = lo | jnp.left_shift(jnp.int32(1), 30 - i)
        cnt = jnp.sum(jnp.where(xi >= cand, 1.0, 0.0), axis=1, keepdims=True)
        return jnp.where(cnt >= cap, cand, lo)

    thr = lax.fori_loop(0, 31, body, jnp.zeros((N_EXPERTS, 1), jnp.int32))
    nb = n // TOK
    rows = lax.broadcasted_iota(jnp.int32, (n, 128), 0)
    cols = lax.broadcasted_iota(jnp.int32, (n, 128), 1)
    blk_ind = jnp.where((rows // TOK) == cols, 1.0, 0.0).astype(BF16)
    u128 = jnp.where(lax.broadcasted_iota(jnp.int32, (128, 128), 0)
                     < lax.broadcasted_iota(jnp.int32, (128, 128), 1), 1.0, 0.0).astype(BF16)
    utok = jnp.where(lax.broadcasted_iota(jnp.int32, (TOK, TOK), 0)
                     < lax.broadcasted_iota(jnp.int32, (TOK, TOK), 1), 1.0, 0.0).astype(BF16)

    def prefix(mf):
        mb = mf.astype(BF16)
        counts = _dot(mb, blk_ind)
        offs = _dot(counts.astype(BF16), u128)
        pieces = [_dot(mb[:, TOK * j:TOK * (j + 1)], utok) + offs[:, j:j + 1] for j in range(nb)]
        return (jnp.concatenate(pieces, axis=1) if nb > 1 else pieces[0]), offs

    gt = xi > thr
    eq = xi == thr
    need = cap - jnp.sum(jnp.where(gt, 1.0, 0.0), axis=1, keepdims=True)
    rank_eq, _ = prefix(jnp.where(eq, 1.0, 0.0))
    sel = gt | (eq & (rank_eq < need))
    rank, offs = prefix(jnp.where(sel, 1.0, 0.0))
    rank_ref[0] = jnp.where(sel, rank, -1.0)
    offs_ref[0] = offs.astype(jnp.int32)


def _select(pt, *, cap):
    B, _, n = pt.shape
    return pl.pallas_call(
        functools.partial(_select_kernel, n=n, cap=cap),
        out_shape=[jax.ShapeDtypeStruct((B, N_EXPERTS, n), F32),
                   jax.ShapeDtypeStruct((B, N_EXPERTS, 128), jnp.int32)],
        grid=(B,),
        in_specs=[pl.BlockSpec((1, N_EXPERTS, n), lambda b: (b, 0, 0))],
        out_specs=[pl.BlockSpec((1, N_EXPERTS, n), lambda b: (b, 0, 0)),
                   pl.BlockSpec((1, N_EXPERTS, 128), lambda b: (b, 0, 0))],
        compiler_params=_cp(("arbitrary",), VMEM_LIMIT),
        name="expert_choice_select",
    )(pt)


def _gather_kernel(offs_ref, h_ref, rank_ref, prob_ref, o_ref, gate_ref, *, eg, per):
    b, g, tb = pl.program_id(0), pl.program_id(1), pl.program_id(2)

    @pl.when(tb == 0)
    def _():
        o_ref[...] = jnp.zeros(o_ref.shape, BF16)
        gate_ref[...] = jnp.zeros(gate_ref.shape, F32)

    h = h_ref[0]
    ntok = h.shape[0]
    cap_pad = o_ref.shape[2]
    win = min(2 * SLOT, cap_pad)

    def add_rows(i, r, p, base, width):
        slots = lax.broadcasted_iota(jnp.int32, (width, ntok), 0).astype(F32) + base.astype(F32)
        hit = r == slots
        rows = _dot(jnp.where(hit, 1.0, 0.0).astype(BF16), h).astype(BF16)
        o_ref[0, i, pl.ds(base, width), :] = o_ref[0, i, pl.ds(base, width), :] + rows
        gate_ref[0, i, pl.ds(base, width), :] = (gate_ref[0, i, pl.ds(base, width), :]
                                                 + jnp.sum(jnp.where(hit, p, 0.0), axis=1, keepdims=True))

    rs, ps, ends, his = [], [], [], []
    for i in range(eg):
        e = g * eg + i
        r = rank_ref[0, pl.ds(e, 1), :]
        p = prob_ref[0, pl.ds(e, 1), :]
        lo = offs_ref[b, e, tb * per]
        his.append(offs_ref[b, e, (tb + 1) * per])
        base = pl.multiple_of(jnp.minimum((lo // SLOT) * SLOT, cap_pad - win), SLOT)
        add_rows(i, r, p, base, win)
        rs.append(r)
        ps.append(p)
        ends.append(base + win)

    for i in range(eg):
        @pl.when(his[i] > ends[i])
        def _(i=i):
            def body(t, carry):
                add_rows(i, rs[i], ps[i], pl.multiple_of(t * SLOT, SLOT), SLOT)
                return carry

            lax.fori_loop(ends[i] // SLOT, (his[i] - 1) // SLOT + 1, body, 0)


def _gather(offs, hl, rank, pt, *, tb_tok, tb0, n, cap_pad, eg):
    B = hl.shape[0]
    per = tb_tok // TOK
    return pl.pallas_call(
        functools.partial(_gather_kernel, eg=eg, per=per),
        out_shape=[jax.ShapeDtypeStruct((B, N_EXPERTS, cap_pad, D), BF16),
                   jax.ShapeDtypeStruct((B, N_EXPERTS, cap_pad, 1), F32)],
        grid_spec=pltpu.PrefetchScalarGridSpec(
            num_scalar_prefetch=1,
            grid=(B, N_EXPERTS // eg, n // tb_tok),
            in_specs=[pl.BlockSpec((1, tb_tok, D), lambda b, g, t, o: (b, tb0 + t, 0)),
                      pl.BlockSpec((1, N_EXPERTS, tb_tok), lambda b, g, t, o: (b, 0, t)),
                      pl.BlockSpec((1, N_EXPERTS, tb_tok), lambda b, g, t, o: (b, 0, t))],
            out_specs=[pl.BlockSpec((1, eg, cap_pad, D), lambda b, g, t, o: (b, g, 0, 0)),
                       pl.BlockSpec((1, eg, cap_pad, 1), lambda b, g, t, o: (b, g, 0, 0))]),
        compiler_params=_cp(("arbitrary",) * 3, VMEM_LIMIT),
        name="expert_gather",
    )(offs, hl, rank, pt)


def _ffn_kernel(x_ref, gate_ref, w1_ref, w3_ref, w2_ref, y_ref, acc_ref):
    f = pl.program_id(2)
    x = x_ref[...].reshape(-1, D)
    a = _dot(x, w1_ref[0, 0].astype(BF16))
    g3 = _dot(x, w3_ref[0, 0].astype(BF16))
    hid = (_silu(a) * g3).astype(BF16)

    @pl.when(f == 0)
    def _():
        acc_ref[...] = jnp.zeros(acc_ref.shape, F32)

    acc_ref[...] += _dot(hid, w2_ref[0, 0].astype(BF16))

    @pl.when(f == pl.num_programs(2) - 1)
    def _():
        gate = gate_ref[...].reshape(-1, 1)
        y_ref[...] = (acc_ref[...] * gate).astype(BF16).reshape(y_ref.shape)


def _ffn(xs, gates, w1, w3, w2, *, layer, mb, tf):
    B, E, cap_pad, _ = xs.shape
    return pl.pallas_call(
        _ffn_kernel,
        out_shape=jax.ShapeDtypeStruct(xs.shape, BF16),
        grid=(E, B // mb, D_FF // tf),
        in_specs=[pl.BlockSpec((mb, 1, cap_pad, D), lambda e, m, f: (m, e, 0, 0)),
                  pl.BlockSpec((mb, 1, cap_pad, 1), lambda e, m, f: (m, e, 0, 0)),
                  pl.BlockSpec((1, 1, D, tf), lambda e, m, f: (layer, e, 0, f)),
                  pl.BlockSpec((1, 1, D, tf), lambda e, m, f: (layer, e, 0, f)),
                  pl.BlockSpec((1, 1, tf, D), lambda e, m, f: (layer, e, f, 0))],
        out_specs=pl.BlockSpec((mb, 1, cap_pad, D), lambda e, m, f: (m, e, 0, 0)),
        scratch_shapes=[pltpu.VMEM((mb * cap_pad, D), F32)],
        compiler_params=_cp(("arbitrary",) * 3, VMEM_LIMIT),
        name="expert_ffn",
    )(xs, gates, w1, w3, w2)


CCOL = 512


def _combine_kernel(offs_ref, x_ref, y_ref, rankc_ref, ada_ref, o_ref, tot_scr, *, per, is_ctx):
    b, tb = pl.program_id(0), pl.program_id(2)
    gt2 = ada_ref[4 if is_ctx else b][5:6]
    rc_all = rankc_ref[0]
    cap_pad = y_ref.shape[2]
    win = min(2 * SLOT, cap_pad)
    slotw = lax.broadcasted_iota(jnp.int32, (1, win), 1).astype(F32)
    slot = lax.broadcasted_iota(jnp.int32, (1, SLOT), 1).astype(F32)

    bases, his = [], []
    total = jnp.zeros(tot_scr.shape, F32)
    for e in range(N_EXPERTS):
        lo = offs_ref[b, e, tb * per]
        his.append(offs_ref[b, e, (tb + 1) * per])
        base = pl.multiple_of(jnp.minimum((lo // SLOT) * SLOT, cap_pad - win), SLOT)
        bases.append(base)
        onehot = jnp.where(rc_all[:, e:e + 1] == slotw + base.astype(F32), 1.0, 0.0).astype(BF16)
        total = total + _dot(onehot, y_ref[0, e, pl.ds(base, win), :])
    tot_scr[...] = total

    for e in range(N_EXPERTS):
        end = bases[e] + win

        @pl.when(his[e] > end)
        def _(e=e, end=end):
            def body(t, carry):
                base = pl.multiple_of(t * SLOT, SLOT)
                onehot = jnp.where(rc_all[:, e:e + 1] == slot + base.astype(F32), 1.0, 0.0).astype(BF16)
                tot_scr[...] += _dot(onehot, y_ref[0, e, pl.ds(base, SLOT), :])
                return carry

            lax.fori_loop(end // SLOT, (his[e] - 1) // SLOT + 1, body, 0)

    o_ref[0] = x_ref[0] + gt2 * tot_scr[...]


def _combine(offs, xu, ys, rank_c, ada_l, *, tb_tok, tb0, n, is_ctx):
    B = xu.shape[0]
    cap_pad = ys.shape[2]
    per = tb_tok // TOK
    return pl.pallas_call(
        functools.partial(_combine_kernel, per=per, is_ctx=is_ctx),
        out_shape=jax.ShapeDtypeStruct(xu.shape, F32),
        grid_spec=pltpu.PrefetchScalarGridSpec(
            num_scalar_prefetch=1,
            grid=(B, D // CCOL, n // tb_tok),
            in_specs=[pl.BlockSpec((1, tb_tok, CCOL), lambda b, c, t, o: (b, tb0 + t, c)),
                      pl.BlockSpec((1, N_EXPERTS, cap_pad, CCOL), lambda b, c, t, o: (b, 0, 0, c),
                                   pipeline_mode=pl.Buffered(1)),
                      pl.BlockSpec((1, tb_tok, N_EXPERTS), lambda b, c, t, o: (b, t, 0)),
                      pl.BlockSpec((8, ADA_CHUNKS, CCOL), lambda b, c, t, o: (0, 0, c))],
            out_specs=pl.BlockSpec((1, tb_tok, CCOL), lambda b, c, t, o: (b, tb0 + t, c)),
            scratch_shapes=[pltpu.VMEM((tb_tok, CCOL), F32)]),
        input_output_aliases={1: 0},
        compiler_params=_cp(("arbitrary",) * 3, VMEM_LIMIT),
        name="expert_combine",
    )(offs, xu, ys, rank_c, ada_l)


def _moe(xu, hl, pt, ada_l, w1, w3, w2, *, layer, row0, is_ctx):
    B, _, n = pt.shape
    cap = EC_CAPACITY * n // N_EXPERTS
    cap_pad = -(-cap // SLOT) * SLOT
    nb = n // TOK
    rank, offs = _select(pt, cap=cap)
    offs = offs[:, :, :nb + 1]
    gt = min(n, 512)
    ct = min(n, 512)
    xs, gates = _gather(offs, hl, rank, pt, tb_tok=gt, tb0=row0 // gt, n=n, cap_pad=cap_pad, eg=4)
    mb = 2 if (B % 2 == 0 and cap_pad >= 1024) else (B if cap_pad < 1024 else 1)
    ys = _ffn(xs, gates, w1, w3, w2, layer=layer, mb=mb, tf=512)
    rank_c = jnp.swapaxes(rank, 1, 2)
    return _combine(offs, xu, ys, rank_c, ada_l, tb_tok=ct, tb0=row0 // ct, n=n, is_ctx=is_ctx)


def _final_kernel(x_ref, g_ref, o_ref):
    x = x_ref[0]
    r = lax.rsqrt(jnp.mean(x * x, axis=-1, keepdims=True) + EPS)
    o_ref[0] = (x * r) * g_ref[...]


def _final_norm(xu, g, *, n):
    B = xu.shape[0]
    tm = 512
    return pl.pallas_call(
        _final_kernel,
        out_shape=jax.ShapeDtypeStruct((B, n, D), F32),
        grid=(B, n // tm),
        in_specs=[pl.BlockSpec((1, tm, D), lambda b, t: (b, t, 0)),
                  pl.BlockSpec((1, D), lambda b, t: (0, 0))],
        out_specs=pl.BlockSpec((1, tm, D), lambda b, t: (b, t, 0)),
        compiler_params=_cp(("arbitrary", "arbitrary")),
        name="final_norm",
    )(xu, g)


def _rope_tables(n, ctx):
    rows = n // GRID_W
    t_row = jnp.repeat(jnp.arange(rows), GRID_W)
    t_col = jnp.tile(jnp.arange(GRID_W), rows)
    nf = DA_DIM // 4
    inv = ROPE_THETA ** (-jnp.arange(nf, dtype=F32) / nf)
    ar = t_row[:, None].astype(F32) * inv
    ac = t_col[:, None].astype(F32) * inv
    ang = jnp.concatenate([ar, ar, ac, ac], axis=-1)
    sign = jnp.where((jnp.arange(DA_DIM) % 16) < 8, -1.0, 1.0).astype(F32)
    cos = jnp.concatenate([jnp.cos(ang), jnp.ones((ctx, DA_DIM), F32)], axis=0)
    sin = jnp.concatenate([jnp.sin(ang) * sign, jnp.zeros((ctx, DA_DIM), F32)], axis=0)
    return jnp.tile(cos, (1, 128 // DA_DIM)), jnp.tile(sin, (1, 128 // DA_DIM))


def _pad_heads_cols(w):
    lead = w.shape[:-1]
    w = w.reshape(lead + (M_HEADS, M_DIM))
    w = jnp.pad(w, [(0, 0)] * len(lead) + [(0, 0), (0, M_PAD - M_DIM)])
    return w.reshape(lead + (MP_WIDTH,))


def _kv_tile(nt):
    for parts in range(1, nt // 128 + 1):
        if nt % parts == 0 and (nt // parts) % 128 == 0 and nt // parts <= 1408:
            return nt // parts
    raise ValueError(nt)


def kernel(x, c, ctx, c_ctx, ada_w, ada_b, norm1_g, norm2_g, w_in, four_w, m_conv_w, m_conv_b, m_gate_b,
           m_norm_g, d_lam, d_norm_g, w_out, router_w, exp_w1, exp_w3, exp_w2, final_g):
    B, N, _ = x.shape
    CTX = ctx.shape[1]
    depth = w_in.shape[0]
    assert CTX == TOK and N % (FFT_N1 * TOK) == 0 and B <= 4
    NT = N + CTX
    PAD = 1024 - CTX
    n_lat = N // TOK
    n2 = N // FFT_N1

    xu = jnp.concatenate([x, ctx, jnp.zeros((B, PAD, D), F32)], axis=1)
    cvecs = jnp.zeros((8, D), F32).at[:B].set(c).at[4].set(c_ctx)
    ada = _adaln(cvecs, ada_w, ada_b).reshape(depth, 8, ADA_CHUNKS, D)
    cos_t, sin_t = _rope_tables(N, CTX + PAD)
    tabs = _fourier_tables(N, CTX)
    tk = _kv_tile(NT)
    tq = 1024

    hl = jnp.zeros((B, NT + PAD, D), BF16)
    for layer in range(depth):
        ctx_out = layer < depth - 1
        lam_init = 0.8 - 0.6 * math.exp(-0.3 * layer)
        w = w_in[layer]
        wm = jnp.concatenate([w[:, OFF_F:OFF_DQ], w[:, OFF_DQ:OFF_MO], w[:, OFF_DK:OFF_DV],
                              _pad_heads_cols(w[:, OFF_MO:OFF_MQ]), _pad_heads_cols(w[:, OFF_MV:OFF_G])],
                             axis=1).astype(BF16)
        wvt = w[:, OFF_DV:OFF_MV].T.astype(BF16)
        wgt = w[:, OFF_G:].T
        wc = jnp.concatenate([_pad_heads_cols(w[:, OFF_MQ:OFF_MK]), _pad_heads_cols(w[:, OFF_MK:OFF_DK])],
                             axis=1).astype(BF16)
        gb = m_gate_b[layer].reshape(N_GATES, 1)
        cw = jnp.concatenate([_pad_heads_cols(m_conv_w[layer][:, :M_WIDTH]),
                              _pad_heads_cols(m_conv_w[layer][:, M_WIDTH:])], axis=1)
        cb = jnp.concatenate([_pad_heads_cols(m_conv_b[layer][:M_WIDTH]),
                              _pad_heads_cols(m_conv_b[layer][M_WIDTH:])]).reshape(1, 2 * MP_WIDTH)
        ada_l = ada[layer]

        y4, dq, dk, dvT, mo, mq, mk, mv, gl, glT = _inproj(
            xu, ada_l, norm1_g[layer].reshape(1, D), wm, wc, wvt, wgt, gb, tabs["cs"], cos_t, sin_t, cw, cb,
            n_lat=n_lat, n2=n2)

        wblk = jnp.zeros((F_WIDTH, F_WIDTH), F32)
        for g in range(F_GROUPS):
            wblk = wblk.at[F_GDIM * g:F_GDIM * (g + 1), F_GDIM * g:F_GDIM * (g + 1)].set(four_w[layer, g])
        f_l, f_c = _fourier(y4, tabs, wblk.astype(BF16), n=N, ctx=CTX, with_ctx=ctx_out)

        dlam = d_lam[layer]
        g2 = d_norm_g[layer].reshape(DA_VDIM, 1)
        da_l = _attention(dq, dk, dvT, dlam, g2, lam_init=lam_init, tq=tq, q0=0, nq=N // tq,
                          tk=tk, k0=0, nk=NT // tk)

        hf, hb = _mlstm(mq, mk, mv, gl, glT, n_lat=n_lat)

        mg = _pad_heads_cols(m_norm_g[layer]).reshape(1, MP_WIDTH)
        wol = w_out[layer]
        wo = jnp.concatenate([wol[:F_WIDTH],
                              jnp.pad(wol[F_WIDTH + DA_WIDTH:].reshape(M_HEADS, M_DIM, D),
                                      ((0, 0), (0, M_PAD - M_DIM), (0, 0))).reshape(MP_WIDTH, D)],
                             axis=0).astype(BF16)
        wod = wol[F_WIDTH:F_WIDTH + DA_WIDTH].astype(BF16)
        g2n = norm2_g[layer].reshape(1, D)
        wrT = router_w[layer].T
        xu, hl, pt_l = _outproj(xu, f_l, da_l, hf, hb, mo, ada_l, mg, wo, wod, g2n, wrT, hl,
                                t0=0, ntl=n_lat, is_ctx=False)
        if ctx_out:
            da_c = _attention(dq, dk, dvT, dlam, g2, lam_init=lam_init, tq=TOK, q0=n_lat, nq=1,
                              tk=TOK, k0=n_lat, nk=1)
            xu, hl, pt_c = _outproj(xu, f_c, da_c, hf, hb, mo, ada_l, mg, wo, wod, g2n, wrT, hl,
                                    t0=n_lat, ntl=1, is_ctx=True)

        xu = _moe(xu, hl, pt_l, ada_l, exp_w1, exp_w3, exp_w2, layer=layer, row0=0, is_ctx=False)
        if ctx_out:
            xu = _moe(xu, hl, pt_c, ada_l, exp_w1, exp_w3, exp_w2, layer=layer, row0=N, is_ctx=True)

    return _final_norm(xu, final_g.reshape(1, D), n=N)
```

```python
import functools
import math

import numpy as np
import jax
import jax.numpy as jnp
from jax import lax
from jax.experimental import pallas as pl
from jax.experimental.pallas import tpu as pltpu

F32 = jnp.float32
BF16 = jnp.bfloat16
HI = lax.Precision.HIGHEST

D = 1024
EPS = 1e-6
GRID_W = 64
ROPE_THETA = 10000.0
F_GROUPS, F_GDIM = 4, 64
F_WIDTH = F_GROUPS * F_GDIM
DA_HEADS, DA_DIM = 6, 32
DA_VDIM = 2 * DA_DIM
DA_WIDTH = DA_HEADS * DA_VDIM
M_HEADS, M_DIM = 4, 96
M_WIDTH = M_HEADS * M_DIM
M_PAD = 128
MP_WIDTH = M_HEADS * M_PAD
N_GATES = 4 * M_HEADS
N_EXPERTS = 16
EC_CAPACITY = 2
D_FF = 2 * D
ADA_CHUNKS = 6

TOK = 256
FFT_N1 = 16
SLOT = 128
NEG = -1e30

OFF_F = 0
OFF_DQ = OFF_F + F_WIDTH
OFF_MO = OFF_DQ + 2 * DA_HEADS * DA_DIM
OFF_MQ = OFF_MO + M_WIDTH
OFF_MK = OFF_MQ + M_WIDTH
OFF_DK = OFF_MK + M_WIDTH
OFF_DV = OFF_DK + 2 * DA_HEADS * DA_DIM
OFF_MV = OFF_DV + DA_HEADS * DA_VDIM
OFF_G = OFF_MV + M_WIDTH

VMEM_LIMIT = 56 * 1024 * 1024


def _cp(sem, vmem=None):
    return pltpu.CompilerParams(dimension_semantics=sem, vmem_limit_bytes=vmem)


def _sigmoid(x):
    return 1.0 / (1.0 + jnp.exp(-x))


def _silu(x):
    return x * _sigmoid(x)


def _dot(a, b, precision=None):
    return jnp.dot(a, b, preferred_element_type=F32, precision=precision)


def _dot_nt(a, b, precision=None):
    return lax.dot_general(a, b, (((1,), (1,)), ((), ())), preferred_element_type=F32,
                           precision=precision)


def _ada_kernel(c_ref, w_ref, b_ref, o_ref):
    c = c_ref[...]
    o_ref[0] = _dot(_silu(c), w_ref[0], HI) + b_ref[0]


def _adaln(cvecs, ada_w, ada_b):
    depth = ada_w.shape[0]
    tn = 1536
    return pl.pallas_call(
        _ada_kernel,
        out_shape=jax.ShapeDtypeStruct((depth, 8, ADA_CHUNKS * D), F32),
        grid=(depth, ADA_CHUNKS * D // tn),
        in_specs=[pl.BlockSpec((8, D), lambda l, j: (0, 0)),
                  pl.BlockSpec((1, D, tn), lambda l, j: (l, 0, j)),
                  pl.BlockSpec((1, 1, tn), lambda l, j: (l, 0, j))],
        out_specs=pl.BlockSpec((1, 8, tn), lambda l, j: (l, 0, j)),
        compiler_params=_cp(("arbitrary", "arbitrary")),
        name="adaln",
    )(cvecs, ada_w, ada_b.reshape(depth, 1, ADA_CHUNKS * D))


def _inproj_kernel(x_ref, xp_ref, xn_ref, ada_ref, g_ref, wm_ref, wc_ref, wvt_ref, wgt_ref, gb_ref, cs_ref,
                   cos_ref, sin_ref, cw_ref, cb_ref,
                   y_ref, dq_ref, dk_ref, dvt_ref, mo_ref, mq_ref, mk_ref, mv_ref, gl_ref, glt_ref, *, n_lat):
    b = pl.program_id(0)
    t = pl.program_id(1)
    n_tiles = pl.num_programs(1)
    is_ctx = t >= n_lat
    row = jnp.where(is_ctx, 4, b)
    mod = ada_ref[row]
    sh, sc = mod[0:1], mod[1:2]

    xa = jnp.concatenate([xp_ref[0], x_ref[0], xn_ref[0]], axis=0)
    r = lax.rsqrt(jnp.mean(xa * xa, axis=-1, keepdims=True) + EPS)
    ha = (xa * r) * g_ref[...] * (1.0 + sc) + sh
    h = ha[8:8 + TOK]
    hb = h.astype(BF16)

    pm = _dot(hb, wm_ref[...])
    o = 0
    pf = pm[:, o:o + F_WIDTH]; o += F_WIDTH
    q = pm[:, o:o + DA_WIDTH]; o += DA_WIDTH
    k = pm[:, o:o + DA_WIDTH]; o += DA_WIDTH
    pt = _dot_nt(wvt_ref[...], hb)
    dvt_ref[0] = pt[:DA_WIDTH].astype(BF16)
    mo_ref[0] = pt[DA_WIDTH:DA_WIDTH + MP_WIDTH].astype(BF16)
    mv_ref[0] = pt[DA_WIDTH + MP_WIDTH:].astype(BF16)

    y_ref[0, 0] = _dot(pf, cs_ref[...], HI)

    cos = cos_ref[...]
    sin = sin_ref[...]
    lane = lax.broadcasted_iota(jnp.int32, (1, 128), 1)
    low = (lane % 16) < 8

    def rope(z):
        parts = []
        for c in range(DA_WIDTH // 128):
            zc = z[:, 128 * c:128 * (c + 1)]
            rot = jnp.where(low, pltpu.roll(zc, 120, 1), pltpu.roll(zc, 8, 1))
            parts.append(zc * cos + rot * sin)
        return jnp.concatenate(parts, axis=1)

    dq_ref[0] = (rope(q) * (DA_DIM ** -0.5 * math.log2(math.e))).astype(BF16)
    dk_ref[0] = rope(k).astype(BF16)

    gpre = _dot_nt(wgt_ref[...], h, HI) + gb_ref[...]
    is_forget = (lax.broadcasted_iota(jnp.int32, (N_GATES, 1), 0) % 8) >= 4
    logsig = jnp.minimum(gpre, 0.0) - jnp.log(1.0 + jnp.exp(-jnp.abs(gpre)))
    glt = jnp.where(is_forget, logsig, gpre)
    glt_ref[0] = glt
    gl_ref[0] = jnp.concatenate([glt, jnp.zeros((128 - N_GATES, TOK), F32)], axis=0).T

    pc = _dot(ha.astype(BF16), wc_ref[...])
    first = (t == 0) | (t == n_lat)
    last = (t == n_lat - 1) | (t == n_tiles - 1)
    ridx = lax.broadcasted_iota(jnp.int32, (TOK + 16, 1), 0)
    pc = jnp.where(((ridx < 8) & first) | ((ridx >= TOK + 8) & last), 0.0, pc)
    cw = cw_ref[...]
    conv = cb_ref[...] + pc[7:7 + TOK] * cw[0:1] + pc[8:8 + TOK] * cw[1:2] + pc[9:9 + TOK] * cw[2:3]
    act = _silu(conv)
    mq_ref[0] = act[:, :MP_WIDTH].astype(BF16)
    mk_ref[0] = (act[:, MP_WIDTH:] * (M_DIM ** -0.5)).astype(BF16)


def _inproj(xu, ada_l, g1, wm, wc, wvt, wgt, gb, cs, cos_t, sin_t, cw, cb, *, n_lat, n2):
    B, NT, _ = xu.shape
    nt = n_lat + 1
    rper = n2 // TOK
    tok3 = lambda w: pl.BlockSpec((1, TOK, w), lambda b, t: (b, t, 0))
    full = lambda a: pl.BlockSpec(a.shape, lambda b, t: (0,) * a.ndim)
    nb8 = NT // 8
    outs = [jax.ShapeDtypeStruct((B, 2 * FFT_N1, n2, 2 * F_WIDTH), F32)]
    outs += [jax.ShapeDtypeStruct((B, NT, DA_WIDTH), BF16), jax.ShapeDtypeStruct((B, nt * TOK, DA_WIDTH), BF16)]
    outs += [jax.ShapeDtypeStruct((B, DA_WIDTH, nt * TOK), BF16)]
    tr = lambda r: jax.ShapeDtypeStruct((B, r, nt * TOK), BF16)
    trs = lambda r: pl.BlockSpec((1, r, TOK), lambda b, t: (b, 0, t))
    outs += [tr(MP_WIDTH), jax.ShapeDtypeStruct((B, NT, MP_WIDTH), BF16),
             jax.ShapeDtypeStruct((B, NT, MP_WIDTH), BF16), tr(MP_WIDTH)]
    outs += [jax.ShapeDtypeStruct((B, NT, 128), F32), jax.ShapeDtypeStruct((B, N_GATES, nt * TOK), F32)]
    out_specs = [pl.BlockSpec((1, 1, TOK, 2 * F_WIDTH), lambda b, t: (b, t // rper, t % rper, 0))]
    out_specs += [tok3(DA_WIDTH)] * 2 + [trs(DA_WIDTH)]
    out_specs += [trs(MP_WIDTH), tok3(MP_WIDTH), tok3(MP_WIDTH), trs(MP_WIDTH)]
    out_specs += [tok3(128), pl.BlockSpec((1, N_GATES, TOK), lambda b, t: (b, 0, t))]
    return pl.pallas_call(
        functools.partial(_inproj_kernel, n_lat=n_lat),
        out_shape=outs,
        grid=(B, nt),
        in_specs=[tok3(D),
                  pl.BlockSpec((1, 8, D), lambda b, t: (b, jnp.maximum(t * (TOK // 8) - 1, 0), 0)),
                  pl.BlockSpec((1, 8, D), lambda b, t: (b, jnp.minimum((t + 1) * (TOK // 8), nb8 - 1), 0)),
                  full(ada_l), full(g1), full(wm), full(wc), full(wvt), full(wgt), full(gb), full(cs),
                  pl.BlockSpec((TOK, 128), lambda b, t: (t, 0)),
                  pl.BlockSpec((TOK, 128), lambda b, t: (t, 0)),
                  full(cw), full(cb)],
        out_specs=out_specs,
        compiler_params=_cp(("arbitrary", "arbitrary"), VMEM_LIMIT),
        name="norm1_inproj",
    )(xu, xu, xu, ada_l, g1, wm, wc, wvt, wgt, gb, cs, cos_t, sin_t, cw, cb)


def _fft1_kernel(y_ref, kc_ref, ks_ref, tc_ref, ts_ref, o_ref, *, groups):
    for g in range(groups):
        blk = y_ref[0, :, 8 * g:8 * (g + 1), :].reshape(FFT_N1 * 8, 2 * F_WIDTH)
        p = _dot(kc_ref[...], blk, HI)
        q = _dot(ks_ref[...], blk, HI)
        ar = p[:, :F_WIDTH] - q[:, F_WIDTH:]
        ai = -p[:, F_WIDTH:] - q[:, :F_WIDTH]
        tc = tc_ref[128 * g:128 * (g + 1), :]
        ts = ts_ref[128 * g:128 * (g + 1), :]
        tc = jnp.concatenate([tc, tc], axis=1)
        ts = jnp.concatenate([ts, ts], axis=1)
        br = ar * tc + ai * ts
        bi = ai * tc - ar * ts
        o_ref[0, :, 8 * g:8 * (g + 1), :] = jnp.concatenate([br, bi], axis=1).reshape(FFT_N1, 8, 2 * F_WIDTH)


def _fft2_kernel(b_ref, c2_ref, s2_ref, wb_ref, perm_ref, o_ref, r_scr, *, n2):
    for i in range(8):
        blk = b_ref[0, i]
        xr = _dot(c2_ref[...], blk[:, :F_WIDTH], HI) + _dot(s2_ref[...], blk[:, F_WIDTH:], HI)
        r_scr[i] = _dot(xr.astype(BF16), wb_ref[...])
    for t in range(n2 // 32):
        rows = jnp.concatenate([r_scr[i, 32 * t:32 * (t + 1), :] for i in range(8)], axis=0)
        o_ref[0, 32 * t:32 * (t + 1), :, :] = _dot(perm_ref[...], rows, HI).reshape(32, 8, F_WIDTH)


def _fftc_kernel(y_ref, c_ref, s_ref, wb_ref, o_ref):
    y = y_ref[0, 0]
    z = _dot(c_ref[...], y[:, :F_WIDTH], HI) - _dot(s_ref[...], y[:, F_WIDTH:], HI)
    o_ref[0] = _dot(z.astype(BF16), wb_ref[...])


def _fourier_tables(n, ctx):
    n1, n2 = FFT_N1, n // FFT_N1
    a = np.arange(n1)
    ang1 = 2 * np.pi * np.outer(a, a) / n1
    eye8 = np.eye(8)
    kc = np.kron(np.cos(ang1), eye8)
    ks = np.kron(np.sin(ang1), eye8)
    n2i = np.arange(n2).reshape(n2 // 8, 1, 8)
    k1 = np.arange(n1).reshape(1, n1, 1)
    angt = (2 * np.pi * n2i * k1 / n).reshape(-1, 1)
    tc = np.broadcast_to(np.cos(angt), (n2 // 8 * 128, 128))
    ts = np.broadcast_to(np.sin(angt), (n2 // 8 * 128, 128))
    b = np.arange(n2)
    ang2 = 2 * np.pi * np.outer(b, b) / n2
    c2 = np.cos(ang2) / math.sqrt(n)
    s2 = np.sin(ang2) / math.sqrt(n)
    perm = np.zeros((256, 256))
    for kk in range(8):
        for j in range(32):
            perm[j * 8 + kk, kk * 32 + j] = 1.0
    cc = np.arange(ctx)
    angc = 2 * np.pi * np.outer(cc, cc) / ctx
    cctx = np.cos(angc) / math.sqrt(ctx)
    sctx = np.sin(angc) / math.sqrt(ctx)
    ch = np.arange(F_GDIM)
    angch = 2 * np.pi * np.outer(ch, ch) / F_GDIM
    cs = np.concatenate([np.kron(np.eye(F_GROUPS), np.cos(angch)),
                         np.kron(np.eye(F_GROUPS), np.sin(angch))], axis=1) / math.sqrt(F_GDIM)
    f = lambda z: jnp.asarray(np.ascontiguousarray(z), dtype=F32)
    return dict(kc=f(kc), ks=f(ks), tc=f(tc), ts=f(ts), c2=f(c2), s2=f(s2), perm=f(perm),
                cctx=f(cctx), sctx=f(sctx), cs=f(cs))


def _fourier(y4, tabs, wblk, *, n, ctx, with_ctx):
    B = y4.shape[0]
    n2 = n // FFT_N1
    groups = 4
    full = lambda a, nd: pl.BlockSpec(a.shape, lambda *i: (0,) * a.ndim)
    b4 = pl.pallas_call(
        functools.partial(_fft1_kernel, groups=groups),
        out_shape=jax.ShapeDtypeStruct((B, FFT_N1, n2, 2 * F_WIDTH), F32),
        grid=(B, n2 // (8 * groups)),
        in_specs=[pl.BlockSpec((1, FFT_N1, 8 * groups, 2 * F_WIDTH), lambda b, j: (b, 0, j, 0)),
                  full(tabs["kc"], 2), full(tabs["ks"], 2),
                  pl.BlockSpec((128 * groups, 128), lambda b, j: (j, 0)),
                  pl.BlockSpec((128 * groups, 128), lambda b, j: (j, 0))],
        out_specs=pl.BlockSpec((1, FFT_N1, 8 * groups, 2 * F_WIDTH), lambda b, j: (b, 0, j, 0)),
        compiler_params=_cp(("arbitrary", "arbitrary")),
        name="fourier_stage1",
    )(y4, tabs["kc"], tabs["ks"], tabs["tc"], tabs["ts"])
    f4 = pl.pallas_call(
        functools.partial(_fft2_kernel, n2=n2),
        out_shape=jax.ShapeDtypeStruct((B, n2, 16, F_WIDTH), F32),
        grid=(B, FFT_N1 // 8),
        in_specs=[pl.BlockSpec((1, 8, n2, 2 * F_WIDTH), lambda b, j: (b, j, 0, 0)),
                  full(tabs["c2"], 2), full(tabs["s2"], 2), full(wblk, 2), full(tabs["perm"], 2)],
        out_specs=pl.BlockSpec((1, n2, 8, F_WIDTH), lambda b, j: (b, 0, j, 0)),
        scratch_shapes=[pltpu.VMEM((8, n2, F_WIDTH), F32)],
        compiler_params=_cp(("arbitrary", "arbitrary"), VMEM_LIMIT),
        name="fourier_stage2",
    )(b4, tabs["c2"], tabs["s2"], wblk, tabs["perm"])
    f_ctx = None
    if with_ctx:
        f_ctx = pl.pallas_call(
            _fftc_kernel,
            out_shape=jax.ShapeDtypeStruct((B, ctx, F_WIDTH), F32),
            grid=(B,),
            in_specs=[pl.BlockSpec((1, 1, TOK, 2 * F_WIDTH), lambda b: (b, FFT_N1, 0, 0)),
                      full(tabs["cctx"], 1), full(tabs["sctx"], 1), full(wblk, 1)],
            out_specs=pl.BlockSpec((1, ctx, F_WIDTH), lambda b: (b, 0, 0)),
            compiler_params=_cp(("arbitrary",)),
            name="fourier_ctx",
        )(y4, tabs["cctx"], tabs["sctx"], wblk)
    return f4.reshape(B, n, F_WIDTH), f_ctx


VROWS = DA_VDIM + 16


def _attn_kernel(q_ref, k_ref, vt_ref, dl_ref, g_ref, o_ref, m_scr, acc_scr, *, lam_init):
    kt = pl.program_id(3)
    nk = pl.num_programs(3)

    @pl.when(kt == 0)
    def _():
        m_scr[...] = jnp.full(m_scr.shape, NEG, F32)
        acc_scr[...] = jnp.zeros(acc_scr.shape, F32)

    q = q_ref[0]
    k = k_ref[0]
    vt = vt_ref[0]
    ones = jnp.ones((16, vt.shape[1]), BF16)
    lhs = [jnp.concatenate([vt[DA_VDIM * h:DA_VDIM * (h + 1)], ones], axis=0) for h in range(2)]
    lane = lax.broadcasted_iota(jnp.int32, (1, 128), 1)
    zero = jnp.zeros((), BF16)
    def scores(j):
        return _dot_nt(k, jnp.where((lane // DA_DIM) == j, q, zero))

    st_next = scores(0)
    for j in range(4):
        st = st_next
        if j < 3:
            st_next = scores(j + 1)
        m_old = m_scr[j]
        m_new = jnp.maximum(m_old, jnp.max(st, axis=0, keepdims=True))
        alpha = jnp.exp2(m_old - m_new)
        pt = jnp.exp2(st - m_new).astype(BF16)
        acc_scr[j] = alpha * acc_scr[j] + _dot(lhs[j // 2], pt)
        m_scr[j] = m_new

    @pl.when(kt == nk - 1)
    def _():
        dl = dl_ref[...]
        lam = (jnp.exp(jnp.sum(dl[0:1] * dl[1:2], keepdims=True))
               - jnp.exp(jnp.sum(dl[2:3] * dl[3:4], keepdims=True)) + lam_init)
        outs = []
        for h in range(2):
            a0 = acc_scr[2 * h]
            a1 = acc_scr[2 * h + 1]
            o = (a0[:DA_VDIM] / a0[DA_VDIM:DA_VDIM + 1]
                 - lam * (a1[:DA_VDIM] / a1[DA_VDIM:DA_VDIM + 1]))
            r = lax.rsqrt(jnp.mean(o * o, axis=0, keepdims=True) + EPS)
            outs.append(((o * r) * g_ref[...]) * (1.0 - lam_init))
        o_ref[0] = jnp.concatenate(outs, axis=0).astype(BF16)


def _attention(dq, dk, dvT, dlam, gcol, *, lam_init, tq, q0, nq, tk, k0, nk):
    B = dq.shape[0]
    return pl.pallas_call(
        functools.partial(_attn_kernel, lam_init=lam_init),
        out_shape=jax.ShapeDtypeStruct((B, DA_WIDTH, nq * tq), BF16),
        grid=(B, DA_WIDTH // 128, nq, nk),
        in_specs=[pl.BlockSpec((1, tq, 128), lambda b, p, i, j: (b, q0 + i, p)),
                  pl.BlockSpec((1, tk, 128), lambda b, p, i, j: (b, k0 + j, p)),
                  pl.BlockSpec((1, 128, tk), lambda b, p, i, j: (b, p, k0 + j)),
                  pl.BlockSpec(dlam.shape, lambda b, p, i, j: (0, 0)),
                  pl.BlockSpec(gcol.shape, lambda b, p, i, j: (0, 0))],
        out_specs=pl.BlockSpec((1, 128, tq), lambda b, p, i, j: (b, p, i)),
        scratch_shapes=[pltpu.VMEM((4, 1, tq), F32), pltpu.VMEM((4, VROWS, tq), F32)],
        compiler_params=_cp(("arbitrary",) * 4, VMEM_LIMIT),
        name="diff_attention",
    )(dq, dk, dvT, dlam, gcol)


def _mlstm_kernel(qf_ref, kf_ref, vf_ref, gcf_ref, grf_ref, qb_ref, kb_ref, vb_ref, gcb_ref, grb_ref,
                  hf_ref, hb_ref, c_scr, m_scr):
    t = pl.program_id(1)

    @pl.when(t == 0)
    def _():
        c_scr[...] = jnp.zeros(c_scr.shape, F32)
        m_scr[...] = jnp.zeros(m_scr.shape, F32)

    L = TOK
    si = lax.broadcasted_iota(jnp.int32, (L, L), 0)
    li = lax.broadcasted_iota(jnp.int32, (L, L), 1)
    dirs = ((qf_ref, kf_ref, vf_ref, gcf_ref, grf_ref, hf_ref, si <= li, li <= si, L - 1),
            (qb_ref, kb_ref, vb_ref, gcb_ref, grb_ref, hb_ref, si >= li, li >= si, 0))
    ones = jnp.ones((16, L), F32)
    for d, (q_ref, k_ref, vt_ref, gc_ref, gr_ref, h_ref, seen, seen_t, last) in enumerate(dirs):
        gc = gc_ref[0]
        gr = gr_ref[0]
        bcols = _dot(jnp.where(seen_t, 1.0, 0.0), gc, HI)
        brows = _dot(gr, jnp.where(seen, 1.0, 0.0), HI)
        for hd in range(M_HEADS):
            idx = d * M_HEADS + hd
            ji = d * 8 + hd
            jf = d * 8 + 4 + hd
            sl = slice(M_PAD * hd, M_PAD * (hd + 1))
            q = q_ref[0, :, sl]
            k = k_ref[0, :, sl]
            vt = vt_ref[0, sl, :]
            b_row = brows[jf:jf + 1, :]
            cs = gc[:, ji:ji + 1] - bcols[:, jf:jf + 1]
            li_row = gr[ji:ji + 1, :]
            m_old = m_scr[idx][0:1, 0:1]
            c_old = c_scr[idx]

            dlog = jnp.where(seen, b_row + cs, NEG)
            inter = b_row + m_old
            m_t = jnp.maximum(inter, jnp.max(dlog, axis=0, keepdims=True))
            w_inter = jnp.exp(inter - m_t)
            st = _dot_nt(k, q) * jnp.exp(dlog - m_t)
            cq = _dot_nt(c_old.astype(BF16), q)
            num = w_inter * cq[:M_PAD] + _dot(vt, st.astype(BF16))
            den = w_inter * cq[M_PAD:M_PAD + 1] + jnp.sum(st, axis=0, keepdims=True)
            h_ref[0, sl, :] = num / jnp.maximum(jnp.abs(den), jnp.exp(-m_t))

            total = b_row[:, last:last + 1]
            wlog = total - b_row + li_row
            m_new = jnp.maximum(total + m_old, jnp.max(wlog, axis=1, keepdims=True))
            decay = jnp.exp(total + m_old - m_new)
            w = jnp.exp(wlog - m_new)
            vw = jnp.concatenate([vt.astype(F32) * w, ones * w], axis=0).astype(BF16)
            c_scr[idx] = decay * c_old + _dot(vw, k)
            m_scr[idx] = jnp.broadcast_to(m_new, (8, 128))


def _mlstm(mq, mk, mvT, gl, glT, *, n_lat):
    B, NT, _ = mq.shape
    nt = n_lat + 1
    fwd = lambda t: jnp.where(t == 0, n_lat, t - 1)
    bwd = lambda t: jnp.where(t == 0, n_lat, n_lat - t)
    tok = lambda w, f: pl.BlockSpec((1, TOK, w), lambda b, t: (b, f(t), 0))
    lanes = lambda r, f: pl.BlockSpec((1, r, TOK), lambda b, t: (b, 0, f(t)))
    ins, specs = [], []
    for f in (fwd, bwd):
        ins += [mq, mk, mvT, gl, glT]
        specs += [tok(MP_WIDTH, f)] * 2 + [lanes(MP_WIDTH, f), tok(128, f), lanes(N_GATES, f)]
    return pl.pallas_call(
        _mlstm_kernel,
        out_shape=[jax.ShapeDtypeStruct((B, MP_WIDTH, nt * TOK), F32)] * 2,
        grid=(B, nt),
        in_specs=specs,
        out_specs=[lanes(MP_WIDTH, fwd), lanes(MP_WIDTH, bwd)],
        scratch_shapes=[pltpu.VMEM((2 * M_HEADS, M_PAD + 16, M_PAD), F32),
                        pltpu.VMEM((2 * M_HEADS, 8, 128), F32)],
        compiler_params=_cp(("arbitrary", "arbitrary"), VMEM_LIMIT),
        name="mlstm",
    )(*ins)


def _outproj_kernel(x_ref, f_ref, dat_ref, hf_ref, hb_ref, mo_ref, ada_ref, mg_ref, wo_ref, wod_ref, g2_ref, wr_ref,
                    xo_ref, hl_ref, pt_ref, *, is_ctx):
    b = pl.program_id(0)
    mod = ada_ref[4 if is_ctx else b]
    gt1, sh2, sc2 = mod[2:3], mod[3:4], mod[4:5]
    hs = hf_ref[0] + hb_ref[0]
    og = mo_ref[0].astype(F32)
    mg = mg_ref[...]
    parts = [dat_ref[0]]
    for hd in range(M_HEADS):
        sl = slice(M_PAD * hd, M_PAD * (hd + 1))
        hh = hs[sl]
        r = lax.rsqrt(jnp.sum(hh * hh, axis=0, keepdims=True) * (1.0 / M_DIM) + EPS)
        parts.append((((hh * r) * mg[sl]) * _sigmoid(og[sl])).astype(BF16))
    mix_t = jnp.concatenate(parts, axis=0)
    upd = _dot(f_ref[0].astype(BF16), wo_ref[...]) + lax.dot_general(
        mix_t, wod_ref[...], (((0,), (0,)), ((), ())), preferred_element_type=F32)
    xn = x_ref[0] + gt1 * upd
    xo_ref[0] = xn
    r = lax.rsqrt(jnp.mean(xn * xn, axis=-1, keepdims=True) + EPS)
    h2 = (xn * r) * g2_ref[...] * (1.0 + sc2) + sh2
    hl_ref[0] = h2.astype(BF16)
    lt = _dot_nt(wr_ref[...], h2, HI)
    ex = jnp.exp(lt - jnp.max(lt, axis=0, keepdims=True))
    pt_ref[0] = ex / jnp.sum(ex, axis=0, keepdims=True)


def _outproj_kernel_aliased(x_ref, f_ref, dat_ref, hf_ref, hb_ref, mo_ref, ada_ref, mg_ref, wo_ref, wod_ref,
                            g2_ref, wr_ref, hlp_ref, xo_ref, hl_ref, pt_ref, *, is_ctx):
    del hlp_ref
    _outproj_kernel(x_ref, f_ref, dat_ref, hf_ref, hb_ref, mo_ref, ada_ref, mg_ref, wo_ref, wod_ref, g2_ref,
                    wr_ref, xo_ref, hl_ref, pt_ref, is_ctx=is_ctx)


def _outproj(xu, f, daT, hf, hb, mo, ada_l, mg, wo, wod, g2, wrT, hl_prev, *, t0, ntl, is_ctx):
    B, NT, _ = xu.shape
    tok = lambda w: pl.BlockSpec((1, TOK, w), lambda b, t: (b, t0 + t, 0))
    trs = lambda r: pl.BlockSpec((1, r, TOK), lambda b, t: (b, 0, t0 + t))
    loc = lambda w: pl.BlockSpec((1, TOK, w), lambda b, t: (b, t, 0))
    full = lambda a: pl.BlockSpec(a.shape, lambda b, t: (0,) * a.ndim)
    return pl.pallas_call(
        functools.partial(_outproj_kernel_aliased, is_ctx=is_ctx),
        out_shape=[jax.ShapeDtypeStruct(xu.shape, F32), jax.ShapeDtypeStruct((B, NT, D), BF16),
                   jax.ShapeDtypeStruct((B, N_EXPERTS, ntl * TOK), F32)],
        grid=(B, ntl),
        in_specs=[tok(D), loc(F_WIDTH), pl.BlockSpec((1, DA_WIDTH, TOK), lambda b, t: (b, 0, t)),
                  trs(MP_WIDTH), trs(MP_WIDTH), trs(MP_WIDTH),
                  full(ada_l), full(mg), full(wo), full(wod), full(g2), full(wrT),
                  pl.BlockSpec(memory_space=pl.ANY)],
        out_specs=[tok(D), tok(D), pl.BlockSpec((1, N_EXPERTS, TOK), lambda b, t: (b, 0, t))],
        input_output_aliases={0: 0, 12: 1},
        compiler_params=_cp(("arbitrary", "arbitrary"), VMEM_LIMIT),
        name="outproj_norm2_router",
    )(xu, f, daT, hf, hb, mo, ada_l, mg, wo, wod, g2, wrT, hl_prev)


def _select_kernel(p_ref, rank_ref, offs_ref, *, n, cap):
    p = p_ref[0]
    xi = pltpu.bitcast(p, jnp.int32)

    def body(i, lo):
        cand = lo | jnp.left_shift(jnp.int32(1), 30 - i)
        cnt = jnp.sum(jnp.where(xi >= cand, 1.0, 0.0), axis=1, keepdims=True)
        return jnp.where(cnt >= cap, cand, lo)

    thr = lax.fori_loop(0, 31, body, jnp.zeros((N_EXPERTS, 1), jnp.int32))
    nb = n // TOK
    rows = lax.broadcasted_iota(jnp.int32, (n, 128), 0)
    cols = lax.broadcasted_iota(jnp.int32, (n, 128), 1)
    blk_ind = jnp.where((rows // TOK) == cols, 1.0, 0.0).astype(BF16)
    u128 = jnp.where(lax.broadcasted_iota(jnp.int32, (128, 128), 0)
                     < lax.broadcasted_iota(jnp.int32, (128, 128), 1), 1.0, 0.0).astype(BF16)
    utok = jnp.where(lax.broadcasted_iota(jnp.int32, (TOK, TOK), 0)
                     < lax.broadcasted_iota(jnp.int32, (TOK, TOK), 1), 1.0, 0.0).astype(BF16)

    def prefix(mf):
        mb = mf.astype(BF16)
        counts = _dot(mb, blk_ind)
        offs = _dot(counts.astype(BF16), u128)
        pieces = [_dot(mb[:, TOK * j:TOK * (j + 1)], utok) + offs[:, j:j + 1] for j in range(nb)]
        return (jnp.concatenate(pieces, axis=1) if nb > 1 else pieces[0]), offs

    gt = xi > thr
    eq = xi == thr
    need = cap - jnp.sum(jnp.where(gt, 1.0, 0.0), axis=1, keepdims=True)
    rank_eq, _ = prefix(jnp.where(eq, 1.0, 0.0))
    sel = gt | (eq & (rank_eq < need))
    rank, offs = prefix(jnp.where(sel, 1.0, 0.0))
    rank_ref[0] = jnp.where(sel, rank, -1.0)
    offs_ref[0] = offs.astype(jnp.int32)


def _select(pt, *, cap):
    B, _, n = pt.shape
    return pl.pallas_call(
        functools.partial(_select_kernel, n=n, cap=cap),
        out_shape=[jax.ShapeDtypeStruct((B, N_EXPERTS, n), F32),
                   jax.ShapeDtypeStruct((B, N_EXPERTS, 128), jnp.int32)],
        grid=(B,),
        in_specs=[pl.BlockSpec((1, N_EXPERTS, n), lambda b: (b, 0, 0))],
        out_specs=[pl.BlockSpec((1, N_EXPERTS, n), lambda b: (b, 0, 0)),
                   pl.BlockSpec((1, N_EXPERTS, 128), lambda b: (b, 0, 0))],
        compiler_params=_cp(("arbitrary",), VMEM_LIMIT),
        name="expert_choice_select",
    )(pt)


def _gather_kernel(offs_ref, h_ref, rank_ref, prob_ref, o_ref, gate_ref, *, eg, per):
    b, g, tb = pl.program_id(0), pl.program_id(1), pl.program_id(2)

    @pl.when(tb == 0)
    def _():
        o_ref[...] = jnp.zeros(o_ref.shape, BF16)
        gate_ref[...] = jnp.zeros(gate_ref.shape, F32)

    h = h_ref[0]
    ntok = h.shape[0]
    cap_pad = o_ref.shape[2]
    win = min(2 * SLOT, cap_pad)

    def add_rows(i, r, p, base, width):
        slots = lax.broadcasted_iota(jnp.int32, (width, ntok), 0).astype(F32) + base.astype(F32)
        hit = r == slots
        rows = _dot(jnp.where(hit, 1.0, 0.0).astype(BF16), h).astype(BF16)
        o_ref[0, i, pl.ds(base, width), :] = o_ref[0, i, pl.ds(base, width), :] + rows
        gate_ref[0, i, pl.ds(base, width), :] = (gate_ref[0, i, pl.ds(base, width), :]
                                                 + jnp.sum(jnp.where(hit, p, 0.0), axis=1, keepdims=True))

    rs, ps, ends, his = [], [], [], []
    for i in range(eg):
        e = g * eg + i
        r = rank_ref[0, pl.ds(e, 1), :]
        p = prob_ref[0, pl.ds(e, 1), :]
        lo = offs_ref[b, e, tb * per]
        his.append(offs_ref[b, e, (tb + 1) * per])
        base = pl.multiple_of(jnp.minimum((lo // SLOT) * SLOT, cap_pad - win), SLOT)
        add_rows(i, r, p, base, win)
        rs.append(r)
        ps.append(p)
        ends.append(base + win)

    for i in range(eg):
        @pl.when(his[i] > ends[i])
        def _(i=i):
            def body(t, carry):
                add_rows(i, rs[i], ps[i], pl.multiple_of(t * SLOT, SLOT), SLOT)
                return carry

            lax.fori_loop(ends[i] // SLOT, (his[i] - 1) // SLOT + 1, body, 0)


def _gather(offs, hl, rank, pt, *, tb_tok, tb0, n, cap_pad, eg):
    B = hl.shape[0]
    per = tb_tok // TOK
    return pl.pallas_call(
        functools.partial(_gather_kernel, eg=eg, per=per),
        out_shape=[jax.ShapeDtypeStruct((B, N_EXPERTS, cap_pad, D), BF16),
                   jax.ShapeDtypeStruct((B, N_EXPERTS, cap_pad, 1), F32)],
        grid_spec=pltpu.PrefetchScalarGridSpec(
            num_scalar_prefetch=1,
            grid=(B, N_EXPERTS // eg, n // tb_tok),
            in_specs=[pl.BlockSpec((1, tb_tok, D), lambda b, g, t, o: (b, tb0 + t, 0)),
                      pl.BlockSpec((1, N_EXPERTS, tb_tok), lambda b, g, t, o: (b, 0, t)),
                      pl.BlockSpec((1, N_EXPERTS, tb_tok), lambda b, g, t, o: (b, 0, t))],
            out_specs=[pl.BlockSpec((1, eg, cap_pad, D), lambda b, g, t, o: (b, g, 0, 0)),
                       pl.BlockSpec((1, eg, cap_pad, 1), lambda b, g, t, o: (b, g, 0, 0))]),
        compiler_params=_cp(("arbitrary",) * 3, VMEM_LIMIT),
        name="expert_gather",
    )(offs, hl, rank, pt)


FFN_ROWS = 1024


def _ffn_kernel(x_ref, gate_ref, w1_ref, w3_ref, w2_ref, y_ref, acc_ref):
    f = pl.program_id(2)

    @pl.when(f == 0)
    def _():
        acc_ref[...] = jnp.zeros(acc_ref.shape, F32)

    w1 = w1_ref[0, 0].astype(BF16)
    w3 = w3_ref[0, 0].astype(BF16)
    w2 = w2_ref[0, 0].astype(BF16)
    mb, _, cap_pad, _ = x_ref.shape
    rows = min(FFN_ROWS, cap_pad)
    for i in range(mb):
        for r in range(0, cap_pad, rows):
            x = x_ref[i, 0, r:r + rows, :]
            hid = (_silu(_dot(x, w1)) * _dot(x, w3)).astype(BF16)
            acc_ref[i * cap_pad + r:i * cap_pad + r + rows, :] += _dot(hid, w2)

    @pl.when(f == pl.num_programs(2) - 1)
    def _():
        gate = gate_ref[...].reshape(-1, 1)
        y_ref[...] = (acc_ref[...] * gate).astype(BF16).reshape(y_ref.shape)


def _ffn(xs, gates, w1, w3, w2, *, layer, mb, tf):
    B, E, cap_pad, _ = xs.shape
    return pl.pallas_call(
        _ffn_kernel,
        out_shape=jax.ShapeDtypeStruct(xs.shape, BF16),
        grid=(E, B // mb, D_FF // tf),
        in_specs=[pl.BlockSpec((mb, 1, cap_pad, D), lambda e, m, f: (m, e, 0, 0)),
                  pl.BlockSpec((mb, 1, cap_pad, 1), lambda e, m, f: (m, e, 0, 0)),
                  pl.BlockSpec((1, 1, D, tf), lambda e, m, f: (layer, e, 0, f)),
                  pl.BlockSpec((1, 1, D, tf), lambda e, m, f: (layer, e, 0, f)),
                  pl.BlockSpec((1, 1, tf, D), lambda e, m, f: (layer, e, f, 0))],
        out_specs=pl.BlockSpec((mb, 1, cap_pad, D), lambda e, m, f: (m, e, 0, 0)),
        scratch_shapes=[pltpu.VMEM((mb * cap_pad, D), F32)],
        compiler_params=_cp(("arbitrary",) * 3, VMEM_LIMIT),
        name="expert_ffn",
    )(xs, gates, w1, w3, w2)


CCOL = 512


def _combine_kernel(offs_ref, x_ref, y_ref, rankc_ref, ada_ref, o_ref, tot_scr, *, per, is_ctx):
    b, tb = pl.program_id(0), pl.program_id(2)
    gt2 = ada_ref[4 if is_ctx else b][5:6]
    rc_all = rankc_ref[0]
    cap_pad = y_ref.shape[2]
    win = min(2 * SLOT, cap_pad)
    slotw = lax.broadcasted_iota(jnp.int32, (1, win), 1).astype(F32)
    slot = lax.broadcasted_iota(jnp.int32, (1, SLOT), 1).astype(F32)

    bases, his = [], []
    total = jnp.zeros(tot_scr.shape, F32)
    for e in range(N_EXPERTS):
        lo = offs_ref[b, e, tb * per]
        his.append(offs_ref[b, e, (tb + 1) * per])
        base = pl.multiple_of(jnp.minimum((lo // SLOT) * SLOT, cap_pad - win), SLOT)
        bases.append(base)
        onehot = jnp.where(rc_all[:, e:e + 1] == slotw + base.astype(F32), 1.0, 0.0).astype(BF16)
        total = total + _dot(onehot, y_ref[0, e, pl.ds(base, win), :])
    tot_scr[...] = total

    for e in range(N_EXPERTS):
        end = bases[e] + win

        @pl.when(his[e] > end)
        def _(e=e, end=end):
            def body(t, carry):
                base = pl.multiple_of(t * SLOT, SLOT)
                onehot = jnp.where(rc_all[:, e:e + 1] == slot + base.astype(F32), 1.0, 0.0).astype(BF16)
                tot_scr[...] += _dot(onehot, y_ref[0, e, pl.ds(base, SLOT), :])
                return carry

            lax.fori_loop(end // SLOT, (his[e] - 1) // SLOT + 1, body, 0)

    o_ref[0] = x_ref[0] + gt2 * tot_scr[...]


def _combine(offs, xu, ys, rank_c, ada_l, *, tb_tok, tb0, n, is_ctx):
    B = xu.shape[0]
    cap_pad = ys.shape[2]
    per = tb_tok // TOK
    return pl.pallas_call(
        functools.partial(_combine_kernel, per=per, is_ctx=is_ctx),
        out_shape=jax.ShapeDtypeStruct(xu.shape, F32),
        grid_spec=pltpu.PrefetchScalarGridSpec(
            num_scalar_prefetch=1,
            grid=(B, D // CCOL, n // tb_tok),
            in_specs=[pl.BlockSpec((1, tb_tok, CCOL), lambda b, c, t, o: (b, tb0 + t, c)),
                      pl.BlockSpec((1, N_EXPERTS, cap_pad, CCOL), lambda b, c, t, o: (b, 0, 0, c),
                                   pipeline_mode=pl.Buffered(1)),
                      pl.BlockSpec((1, tb_tok, N_EXPERTS), lambda b, c, t, o: (b, t, 0)),
                      pl.BlockSpec((8, ADA_CHUNKS, CCOL), lambda b, c, t, o: (0, 0, c))],
            out_specs=pl.BlockSpec((1, tb_tok, CCOL), lambda b, c, t, o: (b, tb0 + t, c)),
            scratch_shapes=[pltpu.VMEM((tb_tok, CCOL), F32)]),
        input_output_aliases={1: 0},
        compiler_params=_cp(("arbitrary",) * 3, VMEM_LIMIT),
        name="expert_combine",
    )(offs, xu, ys, rank_c, ada_l)


def _moe(xu, hl, pt, ada_l, w1, w3, w2, *, layer, row0, is_ctx):
    B, _, n = pt.shape
    cap = EC_CAPACITY * n // N_EXPERTS
    cap_pad = -(-cap // SLOT) * SLOT
    nb = n // TOK
    rank, offs = _select(pt, cap=cap)
    offs = offs[:, :, :nb + 1]
    gt = min(n, 512)
    ct = min(n, 512)
    xs, gates = _gather(offs, hl, rank, pt, tb_tok=gt, tb0=row0 // gt, n=n, cap_pad=cap_pad, eg=4)
    mb = 2 if (B % 2 == 0 and cap_pad >= 1024) else (B if cap_pad < 1024 else 1)
    ys = _ffn(xs, gates, w1, w3, w2, layer=layer, mb=mb, tf=512)
    rank_c = jnp.swapaxes(rank, 1, 2)
    return _combine(offs, xu, ys, rank_c, ada_l, tb_tok=ct, tb0=row0 // ct, n=n, is_ctx=is_ctx)


def _final_kernel(x_ref, g_ref, o_ref):
    x = x_ref[0]
    r = lax.rsqrt(jnp.mean(x * x, axis=-1, keepdims=True) + EPS)
    o_ref[0] = (x * r) * g_ref[...]


def _final_norm(xu, g, *, n):
    B = xu.shape[0]
    tm = 512
    return pl.pallas_call(
        _final_kernel,
        out_shape=jax.ShapeDtypeStruct((B, n, D), F32),
        grid=(B, n // tm),
        in_specs=[pl.BlockSpec((1, tm, D), lambda b, t: (b, t, 0)),
                  pl.BlockSpec((1, D), lambda b, t: (0, 0))],
        out_specs=pl.BlockSpec((1, tm, D), lambda b, t: (b, t, 0)),
        compiler_params=_cp(("arbitrary", "arbitrary")),
        name="final_norm",
    )(xu, g)


def _rope_tables(n, ctx):
    rows = n // GRID_W
    t_row = jnp.repeat(jnp.arange(rows), GRID_W)
    t_col = jnp.tile(jnp.arange(GRID_W), rows)
    nf = DA_DIM // 4
    inv = ROPE_THETA ** (-jnp.arange(nf, dtype=F32) / nf)
    ar = t_row[:, None].astype(F32) * inv
    ac = t_col[:, None].astype(F32) * inv
    ang = jnp.concatenate([ar, ar, ac, ac], axis=-1)
    sign = jnp.where((jnp.arange(DA_DIM) % 16) < 8, -1.0, 1.0).astype(F32)
    cos = jnp.concatenate([jnp.cos(ang), jnp.ones((ctx, DA_DIM), F32)], axis=0)
    sin = jnp.concatenate([jnp.sin(ang) * sign, jnp.zeros((ctx, DA_DIM), F32)], axis=0)
    return jnp.tile(cos, (1, 128 // DA_DIM)), jnp.tile(sin, (1, 128 // DA_DIM))


def _pad_heads_cols(w):
    lead = w.shape[:-1]
    w = w.reshape(lead + (M_HEADS, M_DIM))
    w = jnp.pad(w, [(0, 0)] * len(lead) + [(0, 0), (0, M_PAD - M_DIM)])
    return w.reshape(lead + (MP_WIDTH,))


def _kv_tile(nt):
    for parts in range(1, nt // 128 + 1):
        if nt % parts == 0 and (nt // parts) % 128 == 0 and nt // parts <= 1408:
            return nt // parts
    raise ValueError(nt)


def kernel(x, c, ctx, c_ctx, ada_w, ada_b, norm1_g, norm2_g, w_in, four_w, m_conv_w, m_conv_b, m_gate_b,
           m_norm_g, d_lam, d_norm_g, w_out, router_w, exp_w1, exp_w3, exp_w2, final_g):
    B, N, _ = x.shape
    CTX = ctx.shape[1]
    depth = w_in.shape[0]
    assert CTX == TOK and N % (FFT_N1 * TOK) == 0 and B <= 4
    NT = N + CTX
    PAD = 1024 - CTX
    n_lat = N // TOK
    n2 = N // FFT_N1

    xu = jnp.concatenate([x, ctx, jnp.zeros((B, PAD, D), F32)], axis=1)
    cvecs = jnp.zeros((8, D), F32).at[:B].set(c).at[4].set(c_ctx)
    ada = _adaln(cvecs, ada_w, ada_b).reshape(depth, 8, ADA_CHUNKS, D)
    cos_t, sin_t = _rope_tables(N, CTX + PAD)
    tabs = _fourier_tables(N, CTX)
    tk = _kv_tile(NT)
    tq = 1024

    hl = jnp.zeros((B, NT + PAD, D), BF16)
    for layer in range(depth):
        ctx_out = layer < depth - 1
        lam_init = 0.8 - 0.6 * math.exp(-0.3 * layer)
        w = w_in[layer]
        wm = jnp.concatenate([w[:, OFF_F:OFF_DQ], w[:, OFF_DQ:OFF_MO], w[:, OFF_DK:OFF_DV]], axis=1).astype(BF16)
        wvt = jnp.concatenate([w[:, OFF_DV:OFF_MV], _pad_heads_cols(w[:, OFF_MO:OFF_MQ]),
                               _pad_heads_cols(w[:, OFF_MV:OFF_G])], axis=1).T.astype(BF16)
        wgt = w[:, OFF_G:].T
        wc = jnp.concatenate([_pad_heads_cols(w[:, OFF_MQ:OFF_MK]), _pad_heads_cols(w[:, OFF_MK:OFF_DK])],
                             axis=1).astype(BF16)
        gb = m_gate_b[layer].reshape(N_GATES, 1)
        cw = jnp.concatenate([_pad_heads_cols(m_conv_w[layer][:, :M_WIDTH]),
                              _pad_heads_cols(m_conv_w[layer][:, M_WIDTH:])], axis=1)
        cb = jnp.concatenate([_pad_heads_cols(m_conv_b[layer][:M_WIDTH]),
                              _pad_heads_cols(m_conv_b[layer][M_WIDTH:])]).reshape(1, 2 * MP_WIDTH)
        ada_l = ada[layer]

        y4, dq, dk, dvT, mo, mq, mk, mv, gl, glT = _inproj(
            xu, ada_l, norm1_g[layer].reshape(1, D), wm, wc, wvt, wgt, gb, tabs["cs"], cos_t, sin_t, cw, cb,
            n_lat=n_lat, n2=n2)

        wblk = jnp.zeros((F_WIDTH, F_WIDTH), F32)
        for g in range(F_GROUPS):
            wblk = wblk.at[F_GDIM * g:F_GDIM * (g + 1), F_GDIM * g:F_GDIM * (g + 1)].set(four_w[layer, g])
        f_l, f_c = _fourier(y4, tabs, wblk.astype(BF16), n=N, ctx=CTX, with_ctx=ctx_out)

        dlam = d_lam[layer]
        g2 = d_norm_g[layer].reshape(DA_VDIM, 1)
        da_l = _attention(dq, dk, dvT, dlam, g2, lam_init=lam_init, tq=tq, q0=0, nq=N // tq,
                          tk=tk, k0=0, nk=NT // tk)

        hf, hb = _mlstm(mq, mk, mv, gl, glT, n_lat=n_lat)

        mg = _pad_heads_cols(m_norm_g[layer]).reshape(MP_WIDTH, 1)
        wol = w_out[layer]
        wo = wol[:F_WIDTH].astype(BF16)
        wod = jnp.concatenate([wol[F_WIDTH:F_WIDTH + DA_WIDTH],
                               jnp.pad(wol[F_WIDTH + DA_WIDTH:].reshape(M_HEADS, M_DIM, D),
                                       ((0, 0), (0, M_PAD - M_DIM), (0, 0))).reshape(MP_WIDTH, D)],
                              axis=0).astype(BF16)
        g2n = norm2_g[layer].reshape(1, D)
        wrT = router_w[layer].T
        xu, hl, pt_l = _outproj(xu, f_l, da_l, hf, hb, mo, ada_l, mg, wo, wod, g2n, wrT, hl,
                                t0=0, ntl=n_lat, is_ctx=False)
        if ctx_out:
            da_c = _attention(dq, dk, dvT, dlam, g2, lam_init=lam_init, tq=TOK, q0=n_lat, nq=1,
                              tk=TOK, k0=n_lat, nk=1)
            xu, hl, pt_c = _outproj(xu, f_c, da_c, hf, hb, mo, ada_l, mg, wo, wod, g2n, wrT, hl,
                                    t0=n_lat, ntl=1, is_ctx=True)

        xu = _moe(xu, hl, pt_l, ada_l, exp_w1, exp_w3, exp_w2, layer=layer, row0=0, is_ctx=False)
        if ctx_out:
            xu = _moe(xu, hl, pt_c, ada_l, exp_w1, exp_w3, exp_w2, layer=layer, row0=N, is_ctx=True)

    return _final_norm(xu, final_g.reshape(1, D), n=N)
```

```python
import functools
import math

import numpy as np
import jax
import jax.numpy as jnp
from jax import lax
from jax.experimental import pallas as pl
from jax.experimental.pallas import tpu as pltpu

F32 = jnp.float32
BF16 = jnp.bfloat16
HI = lax.Precision.HIGHEST

D = 1024
EPS = 1e-6
GRID_W = 64
ROPE_THETA = 10000.0
F_GROUPS, F_GDIM = 4, 64
F_WIDTH = F_GROUPS * F_GDIM
DA_HEADS, DA_DIM = 6, 32
DA_VDIM = 2 * DA_DIM
DA_WIDTH = DA_HEADS * DA_VDIM
M_HEADS, M_DIM = 4, 96
M_WIDTH = M_HEADS * M_DIM
M_PAD = 128
MP_WIDTH = M_HEADS * M_PAD
N_GATES = 4 * M_HEADS
N_EXPERTS = 16
EC_CAPACITY = 2
D_FF = 2 * D
ADA_CHUNKS = 6

TOK = 256
FFT_N1 = 16
SLOT = 128
NEG = -1e30

OFF_F = 0
OFF_DQ = OFF_F + F_WIDTH
OFF_MO = OFF_DQ + 2 * DA_HEADS * DA_DIM
OFF_MQ = OFF_MO + M_WIDTH
OFF_MK = OFF_MQ + M_WIDTH
OFF_DK = OFF_MK + M_WIDTH
OFF_DV = OFF_DK + 2 * DA_HEADS * DA_DIM
OFF_MV = OFF_DV + DA_HEADS * DA_VDIM
OFF_G = OFF_MV + M_WIDTH

VMEM_LIMIT = 56 * 1024 * 1024


def _cp(sem, vmem=None):
    return pltpu.CompilerParams(dimension_semantics=sem, vmem_limit_bytes=vmem)


def _sigmoid(x):
    return 1.0 / (1.0 + jnp.exp(-x))


def _silu(x):
    return x * _sigmoid(x)


def _dot(a, b, precision=None):
    return jnp.dot(a, b, preferred_element_type=F32, precision=precision)


def _split(a):
    hi = a.astype(BF16)
    return hi, (a - hi.astype(F32)).astype(BF16)


def _dot3(a, b):
    a_hi, a_lo = a if isinstance(a, tuple) else _split(a)
    b_hi, b_lo = b if isinstance(b, tuple) else _split(b)
    return _dot(a_hi, b_hi) + _dot(a_hi, b_lo) + _dot(a_lo, b_hi)


def _dot_nt(a, b, precision=None):
    return lax.dot_general(a, b, (((1,), (1,)), ((), ())), preferred_element_type=F32,
                           precision=precision)


def _ada_kernel(c_ref, w_ref, b_ref, o_ref):
    c = c_ref[...]
    o_ref[0] = _dot(_silu(c), w_ref[0], HI) + b_ref[0]


def _adaln(cvecs, ada_w, ada_b):
    depth = ada_w.shape[0]
    tn = 1536
    return pl.pallas_call(
        _ada_kernel,
        out_shape=jax.ShapeDtypeStruct((depth, 8, ADA_CHUNKS * D), F32),
        grid=(depth, ADA_CHUNKS * D // tn),
        in_specs=[pl.BlockSpec((8, D), lambda l, j: (0, 0)),
                  pl.BlockSpec((1, D, tn), lambda l, j: (l, 0, j)),
                  pl.BlockSpec((1, 1, tn), lambda l, j: (l, 0, j))],
        out_specs=pl.BlockSpec((1, 8, tn), lambda l, j: (l, 0, j)),
        compiler_params=_cp(("arbitrary", "arbitrary")),
        name="adaln",
    )(cvecs, ada_w, ada_b.reshape(depth, 1, ADA_CHUNKS * D))


def _inproj_kernel(x_ref, xp_ref, xn_ref, ada_ref, g_ref, wm_ref, wc_ref, wvt_ref, wgt_ref, gb_ref, cs_ref,
                   cos_ref, sin_ref, cw_ref, cb_ref,
                   y_ref, dq_ref, dk_ref, dvt_ref, mo_ref, mq_ref, mk_ref, mv_ref, gl_ref, glt_ref, *, n_lat):
    b = pl.program_id(0)
    t = pl.program_id(1)
    n_tiles = pl.num_programs(1)
    is_ctx = t >= n_lat
    row = jnp.where(is_ctx, 4, b)
    mod = ada_ref[row]
    sh, sc = mod[0:1], mod[1:2]

    xa = jnp.concatenate([xp_ref[0], x_ref[0], xn_ref[0]], axis=0)
    r = lax.rsqrt(jnp.mean(xa * xa, axis=-1, keepdims=True) + EPS)
    ha = (xa * r) * g_ref[...] * (1.0 + sc) + sh
    h = ha[8:8 + TOK]
    hb = h.astype(BF16)

    pm = _dot(hb, wm_ref[...])
    o = 0
    pf = pm[:, o:o + F_WIDTH]; o += F_WIDTH
    q = pm[:, o:o + DA_WIDTH]; o += DA_WIDTH
    k = pm[:, o:o + DA_WIDTH]; o += DA_WIDTH
    pt = _dot_nt(wvt_ref[...], hb)
    dvt_ref[0] = pt[:DA_WIDTH].astype(BF16)
    mo_ref[0] = pt[DA_WIDTH:DA_WIDTH + MP_WIDTH].astype(BF16)
    mv_ref[0] = pt[DA_WIDTH + MP_WIDTH:].astype(BF16)

    y_ref[0, 0] = _dot3(pf, cs_ref[...])

    cos = cos_ref[...]
    sin = sin_ref[...]
    lane = lax.broadcasted_iota(jnp.int32, (1, 128), 1)
    low = (lane % 16) < 8

    def rope(z):
        parts = []
        for c in range(DA_WIDTH // 128):
            zc = z[:, 128 * c:128 * (c + 1)]
            rot = jnp.where(low, pltpu.roll(zc, 120, 1), pltpu.roll(zc, 8, 1))
            parts.append(zc * cos + rot * sin)
        return jnp.concatenate(parts, axis=1)

    dq_ref[0] = (rope(q) * (DA_DIM ** -0.5 * math.log2(math.e))).astype(BF16)
    dk_ref[0] = rope(k).astype(BF16)

    gpre = _dot_nt(wgt_ref[...], h, HI) + gb_ref[...]
    is_forget = (lax.broadcasted_iota(jnp.int32, (N_GATES, 1), 0) % 8) >= 4
    logsig = jnp.minimum(gpre, 0.0) - jnp.log(1.0 + jnp.exp(-jnp.abs(gpre)))
    glt = jnp.where(is_forget, logsig, gpre)
    glt_ref[0] = glt
    gl_ref[0] = jnp.concatenate([glt, jnp.zeros((128 - N_GATES, TOK), F32)], axis=0).T

    pc = _dot(ha.astype(BF16), wc_ref[...])
    first = (t == 0) | (t == n_lat)
    last = (t == n_lat - 1) | (t == n_tiles - 1)
    ridx = lax.broadcasted_iota(jnp.int32, (TOK + 16, 1), 0)
    pc = jnp.where(((ridx < 8) & first) | ((ridx >= TOK + 8) & last), 0.0, pc)
    cw = cw_ref[...]
    conv = cb_ref[...] + pc[7:7 + TOK] * cw[0:1] + pc[8:8 + TOK] * cw[1:2] + pc[9:9 + TOK] * cw[2:3]
    act = _silu(conv)
    mq_ref[0] = act[:, :MP_WIDTH].astype(BF16)
    mk_ref[0] = (act[:, MP_WIDTH:] * (M_DIM ** -0.5)).astype(BF16)


def _inproj(xu, ada_l, g1, wm, wc, wvt, wgt, gb, cs, cos_t, sin_t, cw, cb, *, n_lat, n2):
    B, NT, _ = xu.shape
    nt = n_lat + 1
    rper = n2 // TOK
    tok3 = lambda w: pl.BlockSpec((1, TOK, w), lambda b, t: (b, t, 0))
    full = lambda a: pl.BlockSpec(a.shape, lambda b, t: (0,) * a.ndim)
    nb8 = NT // 8
    outs = [jax.ShapeDtypeStruct((B, 2 * FFT_N1, n2, 2 * F_WIDTH), F32)]
    outs += [jax.ShapeDtypeStruct((B, NT, DA_WIDTH), BF16), jax.ShapeDtypeStruct((B, nt * TOK, DA_WIDTH), BF16)]
    outs += [jax.ShapeDtypeStruct((B, DA_WIDTH, nt * TOK), BF16)]
    trs = lambda r: pl.BlockSpec((1, r, TOK), lambda b, t: (b, 0, t))
    trp = jax.ShapeDtypeStruct((B, MP_WIDTH, NT), BF16)
    outs += [trp, jax.ShapeDtypeStruct((B, NT, MP_WIDTH), BF16), jax.ShapeDtypeStruct((B, NT, MP_WIDTH), BF16), trp]
    outs += [jax.ShapeDtypeStruct((B, NT, 128), F32), jax.ShapeDtypeStruct((B, N_GATES, nt * TOK), F32)]
    out_specs = [pl.BlockSpec((1, 1, TOK, 2 * F_WIDTH), lambda b, t: (b, t // rper, t % rper, 0))]
    out_specs += [tok3(DA_WIDTH)] * 2 + [trs(DA_WIDTH)]
    out_specs += [trs(MP_WIDTH), tok3(MP_WIDTH), tok3(MP_WIDTH), trs(MP_WIDTH)]
    out_specs += [tok3(128), pl.BlockSpec((1, N_GATES, TOK), lambda b, t: (b, 0, t))]
    return pl.pallas_call(
        functools.partial(_inproj_kernel, n_lat=n_lat),
        out_shape=outs,
        grid=(B, nt),
        in_specs=[tok3(D),
                  pl.BlockSpec((1, 8, D), lambda b, t: (b, jnp.maximum(t * (TOK // 8) - 1, 0), 0)),
                  pl.BlockSpec((1, 8, D), lambda b, t: (b, jnp.minimum((t + 1) * (TOK // 8), nb8 - 1), 0)),
                  full(ada_l), full(g1), full(wm), full(wc), full(wvt), full(wgt), full(gb), full(cs),
                  pl.BlockSpec((TOK, 128), lambda b, t: (t, 0)),
                  pl.BlockSpec((TOK, 128), lambda b, t: (t, 0)),
                  full(cw), full(cb)],
        out_specs=out_specs,
        compiler_params=_cp(("arbitrary", "arbitrary"), VMEM_LIMIT),
        name="norm1_inproj",
    )(xu, xu, xu, ada_l, g1, wm, wc, wvt, wgt, gb, cs, cos_t, sin_t, cw, cb)


def _fft1_kernel(y_ref, kc_ref, ks_ref, tc_ref, ts_ref, o_ref, *, groups):
    kc = _split(kc_ref[...])
    ks = _split(ks_ref[...])
    for g in range(groups):
        blk = _split(y_ref[0, :, 8 * g:8 * (g + 1), :].reshape(FFT_N1 * 8, 2 * F_WIDTH))
        p = _dot3(kc, blk)
        q = _dot3(ks, blk)
        ar = p[:, :F_WIDTH] - q[:, F_WIDTH:]
        ai = -p[:, F_WIDTH:] - q[:, :F_WIDTH]
        tc = tc_ref[128 * g:128 * (g + 1), :]
        ts = ts_ref[128 * g:128 * (g + 1), :]
        tc = jnp.concatenate([tc, tc], axis=1)
        ts = jnp.concatenate([ts, ts], axis=1)
        br = ar * tc + ai * ts
        bi = ai * tc - ar * ts
        o_ref[0, :, 8 * g:8 * (g + 1), :] = jnp.concatenate([br, bi], axis=1).reshape(FFT_N1, 8, 2 * F_WIDTH)


def _fft2_kernel(b_ref, c2_ref, s2_ref, wb_ref, perm_ref, o_ref, r_scr, *, n2):
    c2 = _split(c2_ref[...])
    s2 = _split(s2_ref[...])
    for i in range(8):
        blk = b_ref[0, i]
        xr = _dot3(c2, blk[:, :F_WIDTH]) + _dot3(s2, blk[:, F_WIDTH:])
        r_scr[i] = _dot(xr.astype(BF16), wb_ref[...]).astype(BF16)
    for t in range(n2 // 32):
        rows = jnp.concatenate([r_scr[i, 32 * t:32 * (t + 1), :] for i in range(8)], axis=0)
        o_ref[0, 32 * t:32 * (t + 1), :, :] = _dot(perm_ref[...], rows).reshape(32, 8, F_WIDTH)


def _fftc_kernel(y_ref, c_ref, s_ref, wb_ref, o_ref):
    y = y_ref[0, 0]
    z = _dot3(c_ref[...], y[:, :F_WIDTH]) - _dot3(s_ref[...], y[:, F_WIDTH:])
    o_ref[0] = _dot(z.astype(BF16), wb_ref[...])


def _fourier_tables(n, ctx):
    n1, n2 = FFT_N1, n // FFT_N1
    a = np.arange(n1)
    ang1 = 2 * np.pi * np.outer(a, a) / n1
    eye8 = np.eye(8)
    kc = np.kron(np.cos(ang1), eye8)
    ks = np.kron(np.sin(ang1), eye8)
    n2i = np.arange(n2).reshape(n2 // 8, 1, 8)
    k1 = np.arange(n1).reshape(1, n1, 1)
    angt = (2 * np.pi * n2i * k1 / n).reshape(-1, 1)
    tc = np.broadcast_to(np.cos(angt), (n2 // 8 * 128, 128))
    ts = np.broadcast_to(np.sin(angt), (n2 // 8 * 128, 128))
    b = np.arange(n2)
    ang2 = 2 * np.pi * np.outer(b, b) / n2
    c2 = np.cos(ang2) / math.sqrt(n)
    s2 = np.sin(ang2) / math.sqrt(n)
    perm = np.zeros((256, 256))
    for kk in range(8):
        for j in range(32):
            perm[j * 8 + kk, kk * 32 + j] = 1.0
    cc = np.arange(ctx)
    angc = 2 * np.pi * np.outer(cc, cc) / ctx
    cctx = np.cos(angc) / math.sqrt(ctx)
    sctx = np.sin(angc) / math.sqrt(ctx)
    ch = np.arange(F_GDIM)
    angch = 2 * np.pi * np.outer(ch, ch) / F_GDIM
    cs = np.concatenate([np.kron(np.eye(F_GROUPS), np.cos(angch)),
                         np.kron(np.eye(F_GROUPS), np.sin(angch))], axis=1) / math.sqrt(F_GDIM)
    f = lambda z: jnp.asarray(np.ascontiguousarray(z), dtype=F32)
    return dict(kc=f(kc), ks=f(ks), tc=f(tc), ts=f(ts), c2=f(c2), s2=f(s2), perm=f(perm).astype(BF16),
                cctx=f(cctx), sctx=f(sctx), cs=f(cs))


def _fourier(y4, tabs, wblk, *, n, ctx, with_ctx):
    B = y4.shape[0]
    n2 = n // FFT_N1
    groups = 4
    full = lambda a, nd: pl.BlockSpec(a.shape, lambda *i: (0,) * a.ndim)
    b4 = pl.pallas_call(
        functools.partial(_fft1_kernel, groups=groups),
        out_shape=jax.ShapeDtypeStruct((B, FFT_N1, n2, 2 * F_WIDTH), F32),
        grid=(B, n2 // (8 * groups)),
        in_specs=[pl.BlockSpec((1, FFT_N1, 8 * groups, 2 * F_WIDTH), lambda b, j: (b, 0, j, 0)),
                  full(tabs["kc"], 2), full(tabs["ks"], 2),
                  pl.BlockSpec((128 * groups, 128), lambda b, j: (j, 0)),
                  pl.BlockSpec((128 * groups, 128), lambda b, j: (j, 0))],
        out_specs=pl.BlockSpec((1, FFT_N1, 8 * groups, 2 * F_WIDTH), lambda b, j: (b, 0, j, 0)),
        compiler_params=_cp(("arbitrary", "arbitrary")),
        name="fourier_stage1",
    )(y4, tabs["kc"], tabs["ks"], tabs["tc"], tabs["ts"])
    f4 = pl.pallas_call(
        functools.partial(_fft2_kernel, n2=n2),
        out_shape=jax.ShapeDtypeStruct((B, n2, 16, F_WIDTH), F32),
        grid=(B, FFT_N1 // 8),
        in_specs=[pl.BlockSpec((1, 8, n2, 2 * F_WIDTH), lambda b, j: (b, j, 0, 0)),
                  full(tabs["c2"], 2), full(tabs["s2"], 2), full(wblk, 2), full(tabs["perm"], 2)],
        out_specs=pl.BlockSpec((1, n2, 8, F_WIDTH), lambda b, j: (b, 0, j, 0)),
        scratch_shapes=[pltpu.VMEM((8, n2, F_WIDTH), BF16)],
        compiler_params=_cp(("arbitrary", "arbitrary"), VMEM_LIMIT),
        name="fourier_stage2",
    )(b4, tabs["c2"], tabs["s2"], wblk, tabs["perm"])
    f_ctx = None
    if with_ctx:
        f_ctx = pl.pallas_call(
            _fftc_kernel,
            out_shape=jax.ShapeDtypeStruct((B, ctx, F_WIDTH), F32),
            grid=(B,),
            in_specs=[pl.BlockSpec((1, 1, TOK, 2 * F_WIDTH), lambda b: (b, FFT_N1, 0, 0)),
                      full(tabs["cctx"], 1), full(tabs["sctx"], 1), full(wblk, 1)],
            out_specs=pl.BlockSpec((1, ctx, F_WIDTH), lambda b: (b, 0, 0)),
            compiler_params=_cp(("arbitrary",)),
            name="fourier_ctx",
        )(y4, tabs["cctx"], tabs["sctx"], wblk)
    return f4.reshape(B, n, F_WIDTH), f_ctx


VROWS = DA_VDIM + 16


def _attn_kernel(q_ref, k_ref, vt_ref, dl_ref, g_ref, o_ref, m_scr, acc_scr, *, lam_init):
    kt = pl.program_id(3)
    nk = pl.num_programs(3)

    @pl.when(kt == 0)
    def _():
        m_scr[...] = jnp.full(m_scr.shape, NEG, F32)
        acc_scr[...] = jnp.zeros(acc_scr.shape, F32)

    q = q_ref[0]
    k = k_ref[0]
    vt = vt_ref[0]
    ones = jnp.ones((16, vt.shape[1]), BF16)
    lhs = [jnp.concatenate([vt[DA_VDIM * h:DA_VDIM * (h + 1)], ones], axis=0) for h in range(2)]
    lane = lax.broadcasted_iota(jnp.int32, (1, 128), 1)
    zero = jnp.zeros((), BF16)
    def scores(j):
        return _dot_nt(k, jnp.where((lane // DA_DIM) == j, q, zero))

    st_next = scores(0)
    for j in range(4):
        st = st_next
        if j < 3:
            st_next = scores(j + 1)
        m_old = m_scr[j]
        m_new = jnp.maximum(m_old, jnp.max(st, axis=0, keepdims=True))
        alpha = jnp.exp2(m_old - m_new)
        pt = jnp.exp2(st - m_new).astype(BF16)
        acc_scr[j] = alpha * acc_scr[j] + _dot(lhs[j // 2], pt)
        m_scr[j] = m_new

    @pl.when(kt == nk - 1)
    def _():
        dl = dl_ref[...]
        lam = (jnp.exp(jnp.sum(dl[0:1] * dl[1:2], keepdims=True))
               - jnp.exp(jnp.sum(dl[2:3] * dl[3:4], keepdims=True)) + lam_init)
        outs = []
        for h in range(2):
            a0 = acc_scr[2 * h]
            a1 = acc_scr[2 * h + 1]
            o = (a0[:DA_VDIM] / a0[DA_VDIM:DA_VDIM + 1]
                 - lam * (a1[:DA_VDIM] / a1[DA_VDIM:DA_VDIM + 1]))
            r = lax.rsqrt(jnp.mean(o * o, axis=0, keepdims=True) + EPS)
            outs.append(((o * r) * g_ref[...]) * (1.0 - lam_init))
        o_ref[0] = jnp.concatenate(outs, axis=0).astype(BF16)


def _attention(dq, dk, dvT, dlam, gcol, *, lam_init, tq, q0, nq, tk, k0, nk):
    B = dq.shape[0]
    return pl.pallas_call(
        functools.partial(_attn_kernel, lam_init=lam_init),
        out_shape=jax.ShapeDtypeStruct((B, DA_WIDTH, nq * tq), BF16),
        grid=(B, DA_WIDTH // 128, nq, nk),
        in_specs=[pl.BlockSpec((1, tq, 128), lambda b, p, i, j: (b, q0 + i, p)),
                  pl.BlockSpec((1, tk, 128), lambda b, p, i, j: (b, k0 + j, p)),
                  pl.BlockSpec((1, 128, tk), lambda b, p, i, j: (b, p, k0 + j)),
                  pl.BlockSpec(dlam.shape, lambda b, p, i, j: (0, 0)),
                  pl.BlockSpec(gcol.shape, lambda b, p, i, j: (0, 0))],
        out_specs=pl.BlockSpec((1, 128, tq), lambda b, p, i, j: (b, p, i)),
        scratch_shapes=[pltpu.VMEM((4, 1, tq), F32), pltpu.VMEM((4, VROWS, tq), F32)],
        compiler_params=_cp(("arbitrary",) * 4, VMEM_LIMIT),
        name="diff_attention",
    )(dq, dk, dvT, dlam, gcol)


def _mlstm_kernel(qf_ref, kf_ref, vf_ref, gcf_ref, grf_ref, qb_ref, kb_ref, vb_ref, gcb_ref, grb_ref,
                  hf_ref, hb_ref, c_scr, m_scr):
    t = pl.program_id(1)

    @pl.when(t == 0)
    def _():
        c_scr[...] = jnp.zeros(c_scr.shape, F32)
        m_scr[...] = jnp.zeros(m_scr.shape, F32)

    L = TOK
    si = lax.broadcasted_iota(jnp.int32, (L, L), 0)
    li = lax.broadcasted_iota(jnp.int32, (L, L), 1)
    dirs = ((qf_ref, kf_ref, vf_ref, gcf_ref, grf_ref, hf_ref, si <= li, li <= si, L - 1),
            (qb_ref, kb_ref, vb_ref, gcb_ref, grb_ref, hb_ref, si >= li, li >= si, 0))
    ones = jnp.ones((16, L), F32)
    for d, (q_ref, k_ref, vt_ref, gc_ref, gr_ref, h_ref, seen, seen_t, last) in enumerate(dirs):
        gc = gc_ref[0]
        gr = gr_ref[0]
        bcols = _dot(jnp.where(seen_t, 1.0, 0.0), gc, HI)
        brows = _dot(gr, jnp.where(seen, 1.0, 0.0), HI)
        for hd in range(M_HEADS):
            idx = d * M_HEADS + hd
            ji = d * 8 + hd
            jf = d * 8 + 4 + hd
            sl = slice(M_PAD * hd, M_PAD * (hd + 1))
            q = q_ref[0, :, sl]
            k = k_ref[0, :, sl]
            vt = vt_ref[0, sl, :]
            b_row = brows[jf:jf + 1, :]
            cs = gc[:, ji:ji + 1] - bcols[:, jf:jf + 1]
            li_row = gr[ji:ji + 1, :]
            m_old = m_scr[idx][0:1, 0:1]
            c_old = c_scr[idx]

            dlog = jnp.where(seen, b_row + cs, NEG)
            inter = b_row + m_old
            m_t = jnp.maximum(inter, jnp.max(dlog, axis=0, keepdims=True))
            w_inter = jnp.exp(inter - m_t)
            st = _dot_nt(k, q) * jnp.exp(dlog - m_t)
            cq = _dot_nt(c_old.astype(BF16), q)
            num = w_inter * cq[:M_PAD] + _dot(vt, st.astype(BF16))
            den = w_inter * cq[M_PAD:M_PAD + 1] + jnp.sum(st, axis=0, keepdims=True)
            h_ref[0, sl, :] = num / jnp.maximum(jnp.abs(den), jnp.exp(-m_t))

            total = b_row[:, last:last + 1]
            wlog = total - b_row + li_row
            m_new = jnp.maximum(total + m_old, jnp.max(wlog, axis=1, keepdims=True))
            decay = jnp.exp(total + m_old - m_new)
            w = jnp.exp(wlog - m_new)
            vw = jnp.concatenate([vt.astype(F32) * w, ones * w], axis=0).astype(BF16)
            c_scr[idx] = decay * c_old + _dot(vw, k)
            m_scr[idx] = jnp.broadcast_to(m_new, (8, 128))


def _mlstm(mq, mk, mvT, gl, glT, *, n_lat):
    B, NT, _ = mq.shape
    nt = n_lat + 1
    fwd = lambda t: jnp.where(t == 0, n_lat, t - 1)
    bwd = lambda t: jnp.where(t == 0, n_lat, n_lat - t)
    tok = lambda w, f: pl.BlockSpec((1, TOK, w), lambda b, t: (b, f(t), 0))
    lanes = lambda r, f: pl.BlockSpec((1, r, TOK), lambda b, t: (b, 0, f(t)))
    ins, specs = [], []
    for f in (fwd, bwd):
        ins += [mq, mk, mvT, gl, glT]
        specs += [tok(MP_WIDTH, f)] * 2 + [lanes(MP_WIDTH, f), tok(128, f), lanes(N_GATES, f)]
    return pl.pallas_call(
        _mlstm_kernel,
        out_shape=[jax.ShapeDtypeStruct((B, MP_WIDTH, NT), F32)] * 2,
        grid=(B, nt),
        in_specs=specs,
        out_specs=[lanes(MP_WIDTH, fwd), lanes(MP_WIDTH, bwd)],
        scratch_shapes=[pltpu.VMEM((2 * M_HEADS, M_PAD + 16, M_PAD), F32),
                        pltpu.VMEM((2 * M_HEADS, 8, 128), F32)],
        compiler_params=_cp(("arbitrary", "arbitrary"), VMEM_LIMIT),
        name="mlstm",
    )(*ins)


def _outproj_kernel(x_ref, f_ref, dat_ref, hf_ref, hb_ref, mo_ref, ada_ref, mg_ref, wo_ref, wod_ref, g2_ref, wr_ref,
                    xo_ref, hl_ref, pt_ref, *, is_ctx):
    b = pl.program_id(0)
    mod = ada_ref[4 if is_ctx else b]
    gt1, sh2, sc2 = mod[2:3], mod[3:4], mod[4:5]
    mg = mg_ref[...]
    for s in range(x_ref.shape[1] // TOK):
        tk = slice(TOK * s, TOK * (s + 1))
        hs = hf_ref[0, :, tk] + hb_ref[0, :, tk]
        og = mo_ref[0, :, tk].astype(F32)
        parts = [dat_ref[0, :, tk]]
        for hd in range(M_HEADS):
            sl = slice(M_PAD * hd, M_PAD * (hd + 1))
            hh = hs[sl]
            r = lax.rsqrt(jnp.sum(hh * hh, axis=0, keepdims=True) * (1.0 / M_DIM) + EPS)
            parts.append((((hh * r) * mg[sl]) * _sigmoid(og[sl])).astype(BF16))
        mix_t = jnp.concatenate(parts, axis=0)
        upd = _dot(f_ref[0, tk].astype(BF16), wo_ref[...]) + lax.dot_general(
            mix_t, wod_ref[...], (((0,), (0,)), ((), ())), preferred_element_type=F32)
        xn = x_ref[0, tk] + gt1 * upd
        xo_ref[0, tk] = xn
        r = lax.rsqrt(jnp.mean(xn * xn, axis=-1, keepdims=True) + EPS)
        h2 = (xn * r) * g2_ref[...] * (1.0 + sc2) + sh2
        hl_ref[0, tk] = h2.astype(BF16)
        lt = _dot_nt(wr_ref[...], h2, HI)
        ex = jnp.exp(lt - jnp.max(lt, axis=0, keepdims=True))
        pt_ref[0, :, tk] = ex / jnp.sum(ex, axis=0, keepdims=True)


def _outproj_kernel_aliased(x_ref, f_ref, dat_ref, hf_ref, hb_ref, mo_ref, ada_ref, mg_ref, wo_ref, wod_ref,
                            g2_ref, wr_ref, hlp_ref, xo_ref, hl_ref, pt_ref, *, is_ctx):
    del hlp_ref
    _outproj_kernel(x_ref, f_ref, dat_ref, hf_ref, hb_ref, mo_ref, ada_ref, mg_ref, wo_ref, wod_ref, g2_ref,
                    wr_ref, xo_ref, hl_ref, pt_ref, is_ctx=is_ctx)


def _outproj(xu, f, daT, hf, hb, mo, ada_l, mg, wo, wod, g2, wrT, hl_prev, *, t0, ntl, is_ctx):
    B, NT, _ = xu.shape
    n = ntl * TOK
    tile = 2 * TOK if n % (2 * TOK) == 0 else TOK
    o = t0 * TOK // tile
    tok = lambda w: pl.BlockSpec((1, tile, w), lambda b, t: (b, o + t, 0))
    trs = lambda r: pl.BlockSpec((1, r, tile), lambda b, t: (b, 0, o + t))
    loc = lambda w: pl.BlockSpec((1, tile, w), lambda b, t: (b, t, 0))
    full = lambda a: pl.BlockSpec(a.shape, lambda b, t: (0,) * a.ndim)
    return pl.pallas_call(
        functools.partial(_outproj_kernel_aliased, is_ctx=is_ctx),
        out_shape=[jax.ShapeDtypeStruct(xu.shape, F32), jax.ShapeDtypeStruct((B, NT, D), BF16),
                   jax.ShapeDtypeStruct((B, N_EXPERTS, n), F32)],
        grid=(B, n // tile),
        in_specs=[tok(D), loc(F_WIDTH), pl.BlockSpec((1, DA_WIDTH, tile), lambda b, t: (b, 0, t)),
                  trs(MP_WIDTH), trs(MP_WIDTH), trs(MP_WIDTH),
                  full(ada_l), full(mg), full(wo), full(wod), full(g2), full(wrT),
                  pl.BlockSpec(memory_space=pl.ANY)],
        out_specs=[tok(D), tok(D), pl.BlockSpec((1, N_EXPERTS, tile), lambda b, t: (b, 0, t))],
        input_output_aliases={0: 0, 12: 1},
        compiler_params=_cp(("arbitrary", "arbitrary"), VMEM_LIMIT),
        name="outproj_norm2_router",
    )(xu, f, daT, hf, hb, mo, ada_l, mg, wo, wod, g2, wrT, hl_prev)


def _select_kernel(p_ref, rank_ref, offs_ref, *, n, cap):
    p = p_ref[0]
    xi = pltpu.bitcast(p, jnp.int32)

    def body(i, lo):
        cand = lo | jnp.left_shift(jnp.int32(1), 30 - i)
        cnt = jnp.sum(jnp.where(xi >= cand, 1.0, 0.0), axis=1, keepdims=True)
        return jnp.where(cnt >= cap, cand, lo)

    thr = lax.fori_loop(0, 31, body, jnp.zeros((N_EXPERTS, 1), jnp.int32))
    nb = n // TOK
    rows = lax.broadcasted_iota(jnp.int32, (n, 128), 0)
    cols = lax.broadcasted_iota(jnp.int32, (n, 128), 1)
    blk_ind = jnp.where((rows // TOK) == cols, 1.0, 0.0).astype(BF16)
    u128 = jnp.where(lax.broadcasted_iota(jnp.int32, (128, 128), 0)
                     < lax.broadcasted_iota(jnp.int32, (128, 128), 1), 1.0, 0.0).astype(BF16)
    utok = jnp.where(lax.broadcasted_iota(jnp.int32, (TOK, TOK), 0)
                     < lax.broadcasted_iota(jnp.int32, (TOK, TOK), 1), 1.0, 0.0).astype(BF16)

    def prefix(mf):
        mb = mf.astype(BF16)
        counts = _dot(mb, blk_ind)
        offs = _dot(counts.astype(BF16), u128)
        pieces = [_dot(mb[:, TOK * j:TOK * (j + 1)], utok) + offs[:, j:j + 1] for j in range(nb)]
        return (jnp.concatenate(pieces, axis=1) if nb > 1 else pieces[0]), offs

    gt = xi > thr
    eq = xi == thr
    need = cap - jnp.sum(jnp.where(gt, 1.0, 0.0), axis=1, keepdims=True)
    rank_eq, _ = prefix(jnp.where(eq, 1.0, 0.0))
    sel = gt | (eq & (rank_eq < need))
    rank, offs = prefix(jnp.where(sel, 1.0, 0.0))
    rank_ref[0] = jnp.where(sel, rank, -1.0)
    offs_ref[0] = offs.astype(jnp.int32)


def _select(pt, *, cap):
    B, _, n = pt.shape
    return pl.pallas_call(
        functools.partial(_select_kernel, n=n, cap=cap),
        out_shape=[jax.ShapeDtypeStruct((B, N_EXPERTS, n), F32),
                   jax.ShapeDtypeStruct((B, N_EXPERTS, 128), jnp.int32)],
        grid=(B,),
        in_specs=[pl.BlockSpec((1, N_EXPERTS, n), lambda b: (b, 0, 0))],
        out_specs=[pl.BlockSpec((1, N_EXPERTS, n), lambda b: (b, 0, 0)),
                   pl.BlockSpec((1, N_EXPERTS, 128), lambda b: (b, 0, 0))],
        compiler_params=_cp(("arbitrary",), VMEM_LIMIT),
        name="expert_choice_select",
    )(pt)


def _gather_kernel(offs_ref, h_ref, rank_ref, prob_ref, o_ref, gate_ref, *, eg, per):
    b, g, tb = pl.program_id(0), pl.program_id(1), pl.program_id(2)

    @pl.when(tb == 0)
    def _():
        o_ref[...] = jnp.zeros(o_ref.shape, BF16)
        gate_ref[...] = jnp.zeros(gate_ref.shape, F32)

    h = h_ref[0]
    ntok = h.shape[0]
    cap_pad = o_ref.shape[2]
    win = min(2 * SLOT, cap_pad)

    def add_rows(i, r, p, base, width):
        slots = lax.broadcasted_iota(jnp.int32, (width, ntok), 0).astype(F32) + base.astype(F32)
        hit = r == slots
        rows = _dot(jnp.where(hit, 1.0, 0.0).astype(BF16), h).astype(BF16)
        o_ref[0, i, pl.ds(base, width), :] = o_ref[0, i, pl.ds(base, width), :] + rows
        gate_ref[0, i, pl.ds(base, width), :] = (gate_ref[0, i, pl.ds(base, width), :]
                                                 + jnp.sum(jnp.where(hit, p, 0.0), axis=1, keepdims=True))

    rs, ps, ends, his = [], [], [], []
    for i in range(eg):
        e = g * eg + i
        r = rank_ref[0, pl.ds(e, 1), :]
        p = prob_ref[0, pl.ds(e, 1), :]
        lo = offs_ref[b, e, tb * per]
        his.append(offs_ref[b, e, (tb + 1) * per])
        base = pl.multiple_of(jnp.minimum((lo // SLOT) * SLOT, cap_pad - win), SLOT)
        add_rows(i, r, p, base, win)
        rs.append(r)
        ps.append(p)
        ends.append(base + win)

    for i in range(eg):
        @pl.when(his[i] > ends[i])
        def _(i=i):
            def body(t, carry):
                add_rows(i, rs[i], ps[i], pl.multiple_of(t * SLOT, SLOT), SLOT)
                return carry

            lax.fori_loop(ends[i] // SLOT, (his[i] - 1) // SLOT + 1, body, 0)


def _gather(offs, hl, rank, pt, *, tb_tok, tb0, n, cap_pad, eg):
    B = hl.shape[0]
    per = tb_tok // TOK
    return pl.pallas_call(
        functools.partial(_gather_kernel, eg=eg, per=per),
        out_shape=[jax.ShapeDtypeStruct((B, N_EXPERTS, cap_pad, D), BF16),
                   jax.ShapeDtypeStruct((B, N_EXPERTS, cap_pad, 1), F32)],
        grid_spec=pltpu.PrefetchScalarGridSpec(
            num_scalar_prefetch=1,
            grid=(B, N_EXPERTS // eg, n // tb_tok),
            in_specs=[pl.BlockSpec((1, tb_tok, D), lambda b, g, t, o: (b, tb0 + t, 0)),
                      pl.BlockSpec((1, N_EXPERTS, tb_tok), lambda b, g, t, o: (b, 0, t)),
                      pl.BlockSpec((1, N_EXPERTS, tb_tok), lambda b, g, t, o: (b, 0, t))],
            out_specs=[pl.BlockSpec((1, eg, cap_pad, D), lambda b, g, t, o: (b, g, 0, 0)),
                       pl.BlockSpec((1, eg, cap_pad, 1), lambda b, g, t, o: (b, g, 0, 0))]),
        compiler_params=_cp(("arbitrary",) * 3, VMEM_LIMIT),
        name="expert_gather",
    )(offs, hl, rank, pt)


FFN_ROWS = 1024


def _ffn_kernel(x_ref, gate_ref, w1_ref, w3_ref, w2_ref, y_ref, acc_ref):
    f = pl.program_id(2)

    @pl.when(f == 0)
    def _():
        acc_ref[...] = jnp.zeros(acc_ref.shape, F32)

    w1 = w1_ref[0, 0].astype(BF16)
    w3 = w3_ref[0, 0].astype(BF16)
    w2 = w2_ref[0, 0].astype(BF16)
    mb, _, cap_pad, _ = x_ref.shape
    rows = min(FFN_ROWS, cap_pad)
    for i in range(mb):
        for r in range(0, cap_pad, rows):
            x = x_ref[i, 0, r:r + rows, :]
            hid = (_silu(_dot(x, w1)) * _dot(x, w3)).astype(BF16)
            acc_ref[i * cap_pad + r:i * cap_pad + r + rows, :] += _dot(hid, w2)

    @pl.when(f == pl.num_programs(2) - 1)
    def _():
        gate = gate_ref[...].reshape(-1, 1)
        y_ref[...] = (acc_ref[...] * gate).astype(BF16).reshape(y_ref.shape)


def _ffn(xs, gates, w1, w3, w2, *, layer, mb, tf):
    B, E, cap_pad, _ = xs.shape
    return pl.pallas_call(
        _ffn_kernel,
        out_shape=jax.ShapeDtypeStruct(xs.shape, BF16),
        grid=(E, B // mb, D_FF // tf),
        in_specs=[pl.BlockSpec((mb, 1, cap_pad, D), lambda e, m, f: (m, e, 0, 0)),
                  pl.BlockSpec((mb, 1, cap_pad, 1), lambda e, m, f: (m, e, 0, 0)),
                  pl.BlockSpec((1, 1, D, tf), lambda e, m, f: (layer, e, 0, f)),
                  pl.BlockSpec((1, 1, D, tf), lambda e, m, f: (layer, e, 0, f)),
                  pl.BlockSpec((1, 1, tf, D), lambda e, m, f: (layer, e, f, 0))],
        out_specs=pl.BlockSpec((mb, 1, cap_pad, D), lambda e, m, f: (m, e, 0, 0)),
        scratch_shapes=[pltpu.VMEM((mb * cap_pad, D), F32)],
        compiler_params=_cp(("arbitrary",) * 3, VMEM_LIMIT),
        name="expert_ffn",
    )(xs, gates, w1, w3, w2)


CCOL = 512


def _combine_kernel(offs_ref, x_ref, y_ref, rankc_ref, ada_ref, o_ref, tot_scr, *, per, is_ctx):
    b, tb = pl.program_id(0), pl.program_id(2)
    gt2 = ada_ref[4 if is_ctx else b][5:6]
    rc_all = rankc_ref[0]
    cap_pad = y_ref.shape[2]
    win = min(2 * SLOT, cap_pad)
    slotw = lax.broadcasted_iota(jnp.int32, (1, win), 1).astype(F32)
    slot = lax.broadcasted_iota(jnp.int32, (1, SLOT), 1).astype(F32)

    bases, his = [], []
    total = jnp.zeros(tot_scr.shape, F32)
    for e in range(N_EXPERTS):
        lo = offs_ref[b, e, tb * per]
        his.append(offs_ref[b, e, (tb + 1) * per])
        base = pl.multiple_of(jnp.minimum((lo // SLOT) * SLOT, cap_pad - win), SLOT)
        bases.append(base)
        onehot = jnp.where(rc_all[:, e:e + 1] == slotw + base.astype(F32), 1.0, 0.0).astype(BF16)
        total = total + _dot(onehot, y_ref[0, e, pl.ds(base, win), :])
    tot_scr[...] = total

    for e in range(N_EXPERTS):
        end = bases[e] + win

        @pl.when(his[e] > end)
        def _(e=e, end=end):
            def body(t, carry):
                base = pl.multiple_of(t * SLOT, SLOT)
                onehot = jnp.where(rc_all[:, e:e + 1] == slot + base.astype(F32), 1.0, 0.0).astype(BF16)
                tot_scr[...] += _dot(onehot, y_ref[0, e, pl.ds(base, SLOT), :])
                return carry

            lax.fori_loop(end // SLOT, (his[e] - 1) // SLOT + 1, body, 0)

    o_ref[0] = x_ref[0] + gt2 * tot_scr[...]


def _combine(offs, xu, ys, rank_c, ada_l, *, tb_tok, tb0, n, is_ctx):
    B = xu.shape[0]
    cap_pad = ys.shape[2]
    per = tb_tok // TOK
    return pl.pallas_call(
        functools.partial(_combine_kernel, per=per, is_ctx=is_ctx),
        out_shape=jax.ShapeDtypeStruct(xu.shape, F32),
        grid_spec=pltpu.PrefetchScalarGridSpec(
            num_scalar_prefetch=1,
            grid=(B, D // CCOL, n // tb_tok),
            in_specs=[pl.BlockSpec((1, tb_tok, CCOL), lambda b, c, t, o: (b, tb0 + t, c)),
                      pl.BlockSpec((1, N_EXPERTS, cap_pad, CCOL), lambda b, c, t, o: (b, 0, 0, c),
                                   pipeline_mode=pl.Buffered(1)),
                      pl.BlockSpec((1, tb_tok, N_EXPERTS), lambda b, c, t, o: (b, t, 0)),
                      pl.BlockSpec((8, ADA_CHUNKS, CCOL), lambda b, c, t, o: (0, 0, c))],
            out_specs=pl.BlockSpec((1, tb_tok, CCOL), lambda b, c, t, o: (b, tb0 + t, c)),
            scratch_shapes=[pltpu.VMEM((tb_tok, CCOL), F32)]),
        input_output_aliases={1: 0},
        compiler_params=_cp(("arbitrary",) * 3, VMEM_LIMIT),
        name="expert_combine",
    )(offs, xu, ys, rank_c, ada_l)


def _moe(xu, hl, pt, ada_l, w1, w3, w2, *, layer, row0, is_ctx):
    B, _, n = pt.shape
    cap = EC_CAPACITY * n // N_EXPERTS
    cap_pad = -(-cap // SLOT) * SLOT
    nb = n // TOK
    rank, offs = _select(pt, cap=cap)
    offs = offs[:, :, :nb + 1]
    gt = min(n, 512)
    ct = min(n, 512)
    xs, gates = _gather(offs, hl, rank, pt, tb_tok=gt, tb0=row0 // gt, n=n, cap_pad=cap_pad, eg=4)
    mb = 2 if (B % 2 == 0 and cap_pad >= 1024) else (B if cap_pad < 1024 else 1)
    ys = _ffn(xs, gates, w1, w3, w2, layer=layer, mb=mb, tf=512)
    rank_c = jnp.swapaxes(rank, 1, 2)
    return _combine(offs, xu, ys, rank_c, ada_l, tb_tok=ct, tb0=row0 // ct, n=n, is_ctx=is_ctx)


def _final_kernel(x_ref, g_ref, o_ref):
    x = x_ref[0]
    r = lax.rsqrt(jnp.mean(x * x, axis=-1, keepdims=True) + EPS)
    o_ref[0] = (x * r) * g_ref[...]


def _final_norm(xu, g, *, n):
    B = xu.shape[0]
    tm = 512
    return pl.pallas_call(
        _final_kernel,
        out_shape=jax.ShapeDtypeStruct((B, n, D), F32),
        grid=(B, n // tm),
        in_specs=[pl.BlockSpec((1, tm, D), lambda b, t: (b, t, 0)),
                  pl.BlockSpec((1, D), lambda b, t: (0, 0))],
        out_specs=pl.BlockSpec((1, tm, D), lambda b, t: (b, t, 0)),
        compiler_params=_cp(("arbitrary", "arbitrary")),
        name="final_norm",
    )(xu, g)


def _rope_tables(n, ctx):
    rows = n // GRID_W
    t_row = jnp.repeat(jnp.arange(rows), GRID_W)
    t_col = jnp.tile(jnp.arange(GRID_W), rows)
    nf = DA_DIM // 4
    inv = ROPE_THETA ** (-jnp.arange(nf, dtype=F32) / nf)
    ar = t_row[:, None].astype(F32) * inv
    ac = t_col[:, None].astype(F32) * inv
    ang = jnp.concatenate([ar, ar, ac, ac], axis=-1)
    sign = jnp.where((jnp.arange(DA_DIM) % 16) < 8, -1.0, 1.0).astype(F32)
    cos = jnp.concatenate([jnp.cos(ang), jnp.ones((ctx, DA_DIM), F32)], axis=0)
    sin = jnp.concatenate([jnp.sin(ang) * sign, jnp.zeros((ctx, DA_DIM), F32)], axis=0)
    return jnp.tile(cos, (1, 128 // DA_DIM)), jnp.tile(sin, (1, 128 // DA_DIM))


def _pad_heads_cols(w):
    lead = w.shape[:-1]
    w = w.reshape(lead + (M_HEADS, M_DIM))
    w = jnp.pad(w, [(0, 0)] * len(lead) + [(0, 0), (0, M_PAD - M_DIM)])
    return w.reshape(lead + (MP_WIDTH,))


def _kv_tile(nt):
    for parts in range(1, nt // 128 + 1):
        if nt % parts == 0 and (nt // parts) % 128 == 0 and nt // parts <= 1408:
            return nt // parts
    raise ValueError(nt)


def kernel(x, c, ctx, c_ctx, ada_w, ada_b, norm1_g, norm2_g, w_in, four_w, m_conv_w, m_conv_b, m_gate_b,
           m_norm_g, d_lam, d_norm_g, w_out, router_w, exp_w1, exp_w3, exp_w2, final_g):
    B, N, _ = x.shape
    CTX = ctx.shape[1]
    depth = w_in.shape[0]
    assert CTX == TOK and N % (FFT_N1 * TOK) == 0 and B <= 4
    NT = N + CTX
    PAD = 1024 - CTX
    n_lat = N // TOK
    n2 = N // FFT_N1

    xu = jnp.concatenate([x, ctx, jnp.zeros((B, PAD, D), F32)], axis=1)
    cvecs = jnp.zeros((8, D), F32).at[:B].set(c).at[4].set(c_ctx)
    ada = _adaln(cvecs, ada_w, ada_b).reshape(depth, 8, ADA_CHUNKS, D)
    cos_t, sin_t = _rope_tables(N, CTX + PAD)
    tabs = _fourier_tables(N, CTX)
    tk = _kv_tile(NT)
    tq = 1024

    hl = jnp.zeros((B, NT + PAD, D), BF16)
    for layer in range(depth):
        ctx_out = layer < depth - 1
        lam_init = 0.8 - 0.6 * math.exp(-0.3 * layer)
        w = w_in[layer]
        wm = jnp.concatenate([w[:, OFF_F:OFF_DQ], w[:, OFF_DQ:OFF_MO], w[:, OFF_DK:OFF_DV]], axis=1).astype(BF16)
        wvt = jnp.concatenate([w[:, OFF_DV:OFF_MV], _pad_heads_cols(w[:, OFF_MO:OFF_MQ]),
                               _pad_heads_cols(w[:, OFF_MV:OFF_G])], axis=1).T.astype(BF16)
        wgt = w[:, OFF_G:].T
        wc = jnp.concatenate([_pad_heads_cols(w[:, OFF_MQ:OFF_MK]), _pad_heads_cols(w[:, OFF_MK:OFF_DK])],
                             axis=1).astype(BF16)
        gb = m_gate_b[layer].reshape(N_GATES, 1)
        cw = jnp.concatenate([_pad_heads_cols(m_conv_w[layer][:, :M_WIDTH]),
                              _pad_heads_cols(m_conv_w[layer][:, M_WIDTH:])], axis=1)
        cb = jnp.concatenate([_pad_heads_cols(m_conv_b[layer][:M_WIDTH]),
                              _pad_heads_cols(m_conv_b[layer][M_WIDTH:])]).reshape(1, 2 * MP_WIDTH)
        ada_l = ada[layer]

        y4, dq, dk, dvT, mo, mq, mk, mv, gl, glT = _inproj(
            xu, ada_l, norm1_g[layer].reshape(1, D), wm, wc, wvt, wgt, gb, tabs["cs"], cos_t, sin_t, cw, cb,
            n_lat=n_lat, n2=n2)

        wblk = jnp.zeros((F_WIDTH, F_WIDTH), F32)
        for g in range(F_GROUPS):
            wblk = wblk.at[F_GDIM * g:F_GDIM * (g + 1), F_GDIM * g:F_GDIM * (g + 1)].set(four_w[layer, g])
        f_l, f_c = _fourier(y4, tabs, wblk.astype(BF16), n=N, ctx=CTX, with_ctx=ctx_out)

        dlam = d_lam[layer]
        g2 = d_norm_g[layer].reshape(DA_VDIM, 1)
        da_l = _attention(dq, dk, dvT, dlam, g2, lam_init=lam_init, tq=tq, q0=0, nq=N // tq,
                          tk=tk, k0=0, nk=NT // tk)

        hf, hb = _mlstm(mq, mk, mv, gl, glT, n_lat=n_lat)

        mg = _pad_heads_cols(m_norm_g[layer]).reshape(MP_WIDTH, 1)
        wol = w_out[layer]
        wo = wol[:F_WIDTH].astype(BF16)
        wod = jnp.concatenate([wol[F_WIDTH:F_WIDTH + DA_WIDTH],
                               jnp.pad(wol[F_WIDTH + DA_WIDTH:].reshape(M_HEADS, M_DIM, D),
                                       ((0, 0), (0, M_PAD - M_DIM), (0, 0))).reshape(MP_WIDTH, D)],
                              axis=0).astype(BF16)
        g2n = norm2_g[layer].reshape(1, D)
        wrT = router_w[layer].T
        xu, hl, pt_l = _outproj(xu, f_l, da_l, hf, hb, mo, ada_l, mg, wo, wod, g2n, wrT, hl,
                                t0=0, ntl=n_lat, is_ctx=False)
        if ctx_out:
            da_c = _attention(dq, dk, dvT, dlam, g2, lam_init=lam_init, tq=TOK, q0=n_lat, nq=1,
                              tk=TOK, k0=n_lat, nk=1)
            xu, hl, pt_c = _outproj(xu, f_c, da_c, hf, hb, mo, ada_l, mg, wo, wod, g2n, wrT, hl,
                                    t0=n_lat, ntl=1, is_ctx=True)

        xu = _moe(xu, hl, pt_l, ada_l, exp_w1, exp_w3, exp_w2, layer=layer, row0=0, is_ctx=False)
        if ctx_out:
            xu = _moe(xu, hl, pt_c, ada_l, exp_w1, exp_w3, exp_w2, layer=layer, row0=N, is_ctx=True)

    return _final_norm(xu, final_g.reshape(1, D), n=N)
```

```python
import functools
import math

import numpy as np
import jax
import jax.numpy as jnp
from jax import lax
from jax.experimental import pallas as pl
from jax.experimental.pallas import tpu as pltpu

F32 = jnp.float32
BF16 = jnp.bfloat16
HI = lax.Precision.HIGHEST

D = 1024
EPS = 1e-6
GRID_W = 64
ROPE_THETA = 10000.0
F_GROUPS, F_GDIM = 4, 64
F_WIDTH = F_GROUPS * F_GDIM
DA_HEADS, DA_DIM = 6, 32
DA_VDIM = 2 * DA_DIM
DA_WIDTH = DA_HEADS * DA_VDIM
M_HEADS, M_DIM = 4, 96
M_WIDTH = M_HEADS * M_DIM
M_PAD = 128
MP_WIDTH = M_HEADS * M_PAD
N_GATES = 4 * M_HEADS
N_EXPERTS = 16
EC_CAPACITY = 2
D_FF = 2 * D
ADA_CHUNKS = 6

TOK = 256
FFT_N1 = 16
SLOT = 128
NEG = -1e30

OFF_F = 0
OFF_DQ = OFF_F + F_WIDTH
OFF_MO = OFF_DQ + 2 * DA_HEADS * DA_DIM
OFF_MQ = OFF_MO + M_WIDTH
OFF_MK = OFF_MQ + M_WIDTH
OFF_DK = OFF_MK + M_WIDTH
OFF_DV = OFF_DK + 2 * DA_HEADS * DA_DIM
OFF_MV = OFF_DV + DA_HEADS * DA_VDIM
OFF_G = OFF_MV + M_WIDTH

VMEM_LIMIT = 56 * 1024 * 1024


def _cp(sem, vmem=None):
    return pltpu.CompilerParams(dimension_semantics=sem, vmem_limit_bytes=vmem)


def _sigmoid(x):
    return 1.0 / (1.0 + jnp.exp(-x))


def _silu(x):
    return x * _sigmoid(x)


def _dot(a, b, precision=None):
    return jnp.dot(a, b, preferred_element_type=F32, precision=precision)


def _split(a):
    hi = a.astype(BF16)
    return hi, (a - hi.astype(F32)).astype(BF16)


def _dot3(a, b):
    a_hi, a_lo = a if isinstance(a, tuple) else _split(a)
    b_hi, b_lo = b if isinstance(b, tuple) else _split(b)
    return _dot(a_hi, b_hi) + _dot(a_hi, b_lo) + _dot(a_lo, b_hi)


def _dot_nt(a, b, precision=None):
    return lax.dot_general(a, b, (((1,), (1,)), ((), ())), preferred_element_type=F32,
                           precision=precision)


def _ada_kernel(c_ref, w_ref, b_ref, o_ref):
    c = c_ref[...]
    o_ref[0] = _dot(_silu(c), w_ref[0], HI) + b_ref[0]


def _adaln(cvecs, ada_w, ada_b):
    depth = ada_w.shape[0]
    tn = 1536
    return pl.pallas_call(
        _ada_kernel,
        out_shape=jax.ShapeDtypeStruct((depth, 8, ADA_CHUNKS * D), F32),
        grid=(depth, ADA_CHUNKS * D // tn),
        in_specs=[pl.BlockSpec((8, D), lambda l, j: (0, 0)),
                  pl.BlockSpec((1, D, tn), lambda l, j: (l, 0, j)),
                  pl.BlockSpec((1, 1, tn), lambda l, j: (l, 0, j))],
        out_specs=pl.BlockSpec((1, 8, tn), lambda l, j: (l, 0, j)),
        compiler_params=_cp(("arbitrary", "arbitrary")),
        name="adaln",
    )(cvecs, ada_w, ada_b.reshape(depth, 1, ADA_CHUNKS * D))


def _inproj_kernel(x_ref, xp_ref, xn_ref, ada_ref, g_ref, wm_ref, wc_ref, wvt_ref, wgt_ref, gb_ref, cs_ref,
                   cos_ref, sin_ref, cw_ref, cb_ref,
                   y_ref, dq_ref, dk_ref, dvt_ref, mo_ref, mq_ref, mk_ref, mv_ref, gl_ref, glt_ref, *, n_lat):
    b = pl.program_id(0)
    t = pl.program_id(1)
    n_tiles = pl.num_programs(1)
    is_ctx = t >= n_lat
    row = jnp.where(is_ctx, 4, b)
    mod = ada_ref[row]
    sh, sc = mod[0:1], mod[1:2]

    xa = jnp.concatenate([xp_ref[0], x_ref[0], xn_ref[0]], axis=0)
    r = lax.rsqrt(jnp.mean(xa * xa, axis=-1, keepdims=True) + EPS)
    ha = (xa * r) * g_ref[...] * (1.0 + sc) + sh
    h = ha[8:8 + TOK]
    hb = h.astype(BF16)

    pm = _dot(hb, wm_ref[...])
    o = 0
    pf = pm[:, o:o + F_WIDTH]; o += F_WIDTH
    q = pm[:, o:o + DA_WIDTH]; o += DA_WIDTH
    k = pm[:, o:o + DA_WIDTH]; o += DA_WIDTH
    pt = _dot_nt(wvt_ref[...], hb)
    dvt_ref[0] = pt[:DA_WIDTH].astype(BF16)
    mo_ref[0] = pt[DA_WIDTH:DA_WIDTH + MP_WIDTH].astype(BF16)
    mv_ref[0] = pt[DA_WIDTH + MP_WIDTH:].astype(BF16)

    y_ref[0, 0] = _dot3(pf, cs_ref[...])

    cos = cos_ref[...]
    sin = sin_ref[...]
    lane = lax.broadcasted_iota(jnp.int32, (1, 128), 1)
    low = (lane % 16) < 8

    def rope(z):
        parts = []
        for c in range(DA_WIDTH // 128):
            zc = z[:, 128 * c:128 * (c + 1)]
            rot = jnp.where(low, pltpu.roll(zc, 120, 1), pltpu.roll(zc, 8, 1))
            parts.append(zc * cos + rot * sin)
        return jnp.concatenate(parts, axis=1)

    dq_ref[0] = (rope(q) * (DA_DIM ** -0.5 * math.log2(math.e))).astype(BF16)
    dk_ref[0] = rope(k).astype(BF16)

    gpre = _dot_nt(wgt_ref[...], h, HI) + gb_ref[...]
    is_forget = (lax.broadcasted_iota(jnp.int32, (N_GATES, 1), 0) % 8) >= 4
    logsig = jnp.minimum(gpre, 0.0) - jnp.log(1.0 + jnp.exp(-jnp.abs(gpre)))
    glt = jnp.where(is_forget, logsig, gpre)
    glt_ref[0] = glt
    gl_ref[0] = jnp.concatenate([glt, jnp.zeros((128 - N_GATES, TOK), F32)], axis=0).T

    pc = _dot(ha.astype(BF16), wc_ref[...])
    first = (t == 0) | (t == n_lat)
    last = (t == n_lat - 1) | (t == n_tiles - 1)
    ridx = lax.broadcasted_iota(jnp.int32, (TOK + 16, 1), 0)
    pc = jnp.where(((ridx < 8) & first) | ((ridx >= TOK + 8) & last), 0.0, pc)
    cw = cw_ref[...]
    conv = cb_ref[...] + pc[7:7 + TOK] * cw[0:1] + pc[8:8 + TOK] * cw[1:2] + pc[9:9 + TOK] * cw[2:3]
    act = _silu(conv)
    mq_ref[0] = act[:, :MP_WIDTH].astype(BF16)
    mk_ref[0] = (act[:, MP_WIDTH:] * (M_DIM ** -0.5)).astype(BF16)


def _inproj(xu, ada_l, g1, wm, wc, wvt, wgt, gb, cs, cos_t, sin_t, cw, cb, *, n_lat, n2):
    B, NT, _ = xu.shape
    nt = n_lat + 1
    rper = n2 // TOK
    tok3 = lambda w: pl.BlockSpec((1, TOK, w), lambda b, t: (b, t, 0))
    full = lambda a: pl.BlockSpec(a.shape, lambda b, t: (0,) * a.ndim)
    nb8 = NT // 8
    outs = [jax.ShapeDtypeStruct((B, 2 * FFT_N1, n2, 2 * F_WIDTH), F32)]
    nq = -(-NT // 2048) * 2048
    outs += [jax.ShapeDtypeStruct((B, nq, DA_WIDTH), BF16), jax.ShapeDtypeStruct((B, nt * TOK, DA_WIDTH), BF16)]
    outs += [jax.ShapeDtypeStruct((B, DA_WIDTH, nt * TOK), BF16)]
    trs = lambda r: pl.BlockSpec((1, r, TOK), lambda b, t: (b, 0, t))
    trp = jax.ShapeDtypeStruct((B, MP_WIDTH, NT), BF16)
    outs += [trp, jax.ShapeDtypeStruct((B, NT, MP_WIDTH), BF16), jax.ShapeDtypeStruct((B, NT, MP_WIDTH), BF16), trp]
    outs += [jax.ShapeDtypeStruct((B, NT, 128), F32), jax.ShapeDtypeStruct((B, N_GATES, nt * TOK), F32)]
    out_specs = [pl.BlockSpec((1, 1, TOK, 2 * F_WIDTH), lambda b, t: (b, t // rper, t % rper, 0))]
    out_specs += [tok3(DA_WIDTH)] * 2 + [trs(DA_WIDTH)]
    out_specs += [trs(MP_WIDTH), tok3(MP_WIDTH), tok3(MP_WIDTH), trs(MP_WIDTH)]
    out_specs += [tok3(128), pl.BlockSpec((1, N_GATES, TOK), lambda b, t: (b, 0, t))]
    return pl.pallas_call(
        functools.partial(_inproj_kernel, n_lat=n_lat),
        out_shape=outs,
        grid=(B, nt),
        in_specs=[tok3(D),
                  pl.BlockSpec((1, 8, D), lambda b, t: (b, jnp.maximum(t * (TOK // 8) - 1, 0), 0)),
                  pl.BlockSpec((1, 8, D), lambda b, t: (b, jnp.minimum((t + 1) * (TOK // 8), nb8 - 1), 0)),
                  full(ada_l), full(g1), full(wm), full(wc), full(wvt), full(wgt), full(gb), full(cs),
                  pl.BlockSpec((TOK, 128), lambda b, t: (t, 0)),
                  pl.BlockSpec((TOK, 128), lambda b, t: (t, 0)),
                  full(cw), full(cb)],
        out_specs=out_specs,
        compiler_params=_cp(("arbitrary", "arbitrary"), VMEM_LIMIT),
        name="norm1_inproj",
    )(xu, xu, xu, ada_l, g1, wm, wc, wvt, wgt, gb, cs, cos_t, sin_t, cw, cb)


def _fft1_kernel(y_ref, kc_ref, ks_ref, tc_ref, ts_ref, o_ref, *, groups):
    kc = _split(kc_ref[...])
    ks = _split(ks_ref[...])
    for g in range(groups):
        blk = _split(y_ref[0, :, 8 * g:8 * (g + 1), :].reshape(FFT_N1 * 8, 2 * F_WIDTH))
        p = _dot3(kc, blk)
        q = _dot3(ks, blk)
        ar = p[:, :F_WIDTH] - q[:, F_WIDTH:]
        ai = -p[:, F_WIDTH:] - q[:, :F_WIDTH]
        tc = tc_ref[128 * g:128 * (g + 1), :]
        ts = ts_ref[128 * g:128 * (g + 1), :]
        tc = jnp.concatenate([tc, tc], axis=1)
        ts = jnp.concatenate([ts, ts], axis=1)
        br = ar * tc + ai * ts
        bi = ai * tc - ar * ts
        o_ref[0, :, 8 * g:8 * (g + 1), :] = jnp.concatenate([br, bi], axis=1).reshape(FFT_N1, 8, 2 * F_WIDTH)


def _fft2_kernel(b_ref, c2_ref, s2_ref, wb_ref, perm_ref, o_ref, r_scr, *, n2):
    c2 = _split(c2_ref[...])
    s2 = _split(s2_ref[...])
    for i in range(8):
        blk = b_ref[0, i]
        xr = _dot3(c2, blk[:, :F_WIDTH]) + _dot3(s2, blk[:, F_WIDTH:])
        r_scr[i] = _dot(xr.astype(BF16), wb_ref[...]).astype(BF16)
    for t in range(n2 // 32):
        rows = jnp.concatenate([r_scr[i, 32 * t:32 * (t + 1), :] for i in range(8)], axis=0)
        o_ref[0, 32 * t:32 * (t + 1), :, :] = _dot(perm_ref[...], rows).reshape(32, 8, F_WIDTH)


def _fftc_kernel(y_ref, c_ref, s_ref, wb_ref, o_ref):
    y = y_ref[0, 0]
    z = _dot3(c_ref[...], y[:, :F_WIDTH]) - _dot3(s_ref[...], y[:, F_WIDTH:])
    o_ref[0] = _dot(z.astype(BF16), wb_ref[...])


def _fourier_tables(n, ctx):
    n1, n2 = FFT_N1, n // FFT_N1
    a = np.arange(n1)
    ang1 = 2 * np.pi * np.outer(a, a) / n1
    eye8 = np.eye(8)
    kc = np.kron(np.cos(ang1), eye8)
    ks = np.kron(np.sin(ang1), eye8)
    n2i = np.arange(n2).reshape(n2 // 8, 1, 8)
    k1 = np.arange(n1).reshape(1, n1, 1)
    angt = (2 * np.pi * n2i * k1 / n).reshape(-1, 1)
    tc = np.broadcast_to(np.cos(angt), (n2 // 8 * 128, 128))
    ts = np.broadcast_to(np.sin(angt), (n2 // 8 * 128, 128))
    b = np.arange(n2)
    ang2 = 2 * np.pi * np.outer(b, b) / n2
    c2 = np.cos(ang2) / math.sqrt(n)
    s2 = np.sin(ang2) / math.sqrt(n)
    perm = np.zeros((256, 256))
    for kk in range(8):
        for j in range(32):
            perm[j * 8 + kk, kk * 32 + j] = 1.0
    cc = np.arange(ctx)
    angc = 2 * np.pi * np.outer(cc, cc) / ctx
    cctx = np.cos(angc) / math.sqrt(ctx)
    sctx = np.sin(angc) / math.sqrt(ctx)
    ch = np.arange(F_GDIM)
    angch = 2 * np.pi * np.outer(ch, ch) / F_GDIM
    cs = np.concatenate([np.kron(np.eye(F_GROUPS), np.cos(angch)),
                         np.kron(np.eye(F_GROUPS), np.sin(angch))], axis=1) / math.sqrt(F_GDIM)
    f = lambda z: jnp.asarray(np.ascontiguousarray(z), dtype=F32)
    return dict(kc=f(kc), ks=f(ks), tc=f(tc), ts=f(ts), c2=f(c2), s2=f(s2), perm=f(perm).astype(BF16),
                cctx=f(cctx), sctx=f(sctx), cs=f(cs))


def _fourier(y4, tabs, wblk, *, n, ctx, with_ctx):
    B = y4.shape[0]
    n2 = n // FFT_N1
    groups = 4
    full = lambda a, nd: pl.BlockSpec(a.shape, lambda *i: (0,) * a.ndim)
    b4 = pl.pallas_call(
        functools.partial(_fft1_kernel, groups=groups),
        out_shape=jax.ShapeDtypeStruct((B, FFT_N1, n2, 2 * F_WIDTH), F32),
        grid=(B, n2 // (8 * groups)),
        in_specs=[pl.BlockSpec((1, FFT_N1, 8 * groups, 2 * F_WIDTH), lambda b, j: (b, 0, j, 0)),
                  full(tabs["kc"], 2), full(tabs["ks"], 2),
                  pl.BlockSpec((128 * groups, 128), lambda b, j: (j, 0)),
                  pl.BlockSpec((128 * groups, 128), lambda b, j: (j, 0))],
        out_specs=pl.BlockSpec((1, FFT_N1, 8 * groups, 2 * F_WIDTH), lambda b, j: (b, 0, j, 0)),
        compiler_params=_cp(("arbitrary", "arbitrary")),
        name="fourier_stage1",
    )(y4, tabs["kc"], tabs["ks"], tabs["tc"], tabs["ts"])
    f4 = pl.pallas_call(
        functools.partial(_fft2_kernel, n2=n2),
        out_shape=jax.ShapeDtypeStruct((B, n2, 16, F_WIDTH), F32),
        grid=(B, FFT_N1 // 8),
        in_specs=[pl.BlockSpec((1, 8, n2, 2 * F_WIDTH), lambda b, j: (b, j, 0, 0)),
                  full(tabs["c2"], 2), full(tabs["s2"], 2), full(wblk, 2), full(tabs["perm"], 2)],
        out_specs=pl.BlockSpec((1, n2, 8, F_WIDTH), lambda b, j: (b, 0, j, 0)),
        scratch_shapes=[pltpu.VMEM((8, n2, F_WIDTH), BF16)],
        compiler_params=_cp(("arbitrary", "arbitrary"), VMEM_LIMIT),
        name="fourier_stage2",
    )(b4, tabs["c2"], tabs["s2"], wblk, tabs["perm"])
    f_ctx = None
    if with_ctx:
        f_ctx = pl.pallas_call(
            _fftc_kernel,
            out_shape=jax.ShapeDtypeStruct((B, ctx, F_WIDTH), F32),
            grid=(B,),
            in_specs=[pl.BlockSpec((1, 1, TOK, 2 * F_WIDTH), lambda b: (b, FFT_N1, 0, 0)),
                      full(tabs["cctx"], 1), full(tabs["sctx"], 1), full(wblk, 1)],
            out_specs=pl.BlockSpec((1, ctx, F_WIDTH), lambda b: (b, 0, 0)),
            compiler_params=_cp(("arbitrary",)),
            name="fourier_ctx",
        )(y4, tabs["cctx"], tabs["sctx"], wblk)
    return f4.reshape(B, n, F_WIDTH), f_ctx


VROWS = DA_VDIM + 16


def _attn_kernel(q_ref, k_ref, vt_ref, dl_ref, g_ref, o_ref, m_scr, acc_scr, *, lam_init):
    kt = pl.program_id(3)
    nk = pl.num_programs(3)

    @pl.when(kt == 0)
    def _():
        m_scr[...] = jnp.full(m_scr.shape, NEG, F32)
        acc_scr[...] = jnp.zeros(acc_scr.shape, F32)

    q = q_ref[0]
    k = k_ref[0]
    vt = vt_ref[0]
    ones = jnp.ones((16, vt.shape[1]), BF16)
    lhs = [jnp.concatenate([vt[DA_VDIM * h:DA_VDIM * (h + 1)], ones], axis=0) for h in range(2)]
    lane = lax.broadcasted_iota(jnp.int32, (1, 128), 1)
    zero = jnp.zeros((), BF16)
    def scores(j):
        return _dot_nt(k, jnp.where((lane // DA_DIM) == j, q, zero))

    st_next = scores(0)
    for j in range(4):
        st = st_next
        if j < 3:
            st_next = scores(j + 1)
        m_old = m_scr[j]
        m_new = jnp.maximum(m_old, jnp.max(st, axis=0, keepdims=True))
        alpha = jnp.exp2(m_old - m_new)
        pt = jnp.exp2(st - m_new).astype(BF16)
        acc_scr[j] = alpha * acc_scr[j] + _dot(lhs[j // 2], pt)
        m_scr[j] = m_new

    @pl.when(kt == nk - 1)
    def _():
        dl = dl_ref[...]
        lam = (jnp.exp(jnp.sum(dl[0:1] * dl[1:2], keepdims=True))
               - jnp.exp(jnp.sum(dl[2:3] * dl[3:4], keepdims=True)) + lam_init)
        outs = []
        for h in range(2):
            a0 = acc_scr[2 * h]
            a1 = acc_scr[2 * h + 1]
            o = (a0[:DA_VDIM] / a0[DA_VDIM:DA_VDIM + 1]
                 - lam * (a1[:DA_VDIM] / a1[DA_VDIM:DA_VDIM + 1]))
            r = lax.rsqrt(jnp.mean(o * o, axis=0, keepdims=True) + EPS)
            outs.append(((o * r) * g_ref[...]) * (1.0 - lam_init))
        o_ref[0] = jnp.concatenate(outs, axis=0).astype(BF16)


def _attention(dq, dk, dvT, dlam, gcol, *, lam_init, tq, q0, nq, tk, k0, nk):
    B = dq.shape[0]
    return pl.pallas_call(
        functools.partial(_attn_kernel, lam_init=lam_init),
        out_shape=jax.ShapeDtypeStruct((B, DA_WIDTH, nq * tq), BF16),
        grid=(B, DA_WIDTH // 128, nq, nk),
        in_specs=[pl.BlockSpec((1, tq, 128), lambda b, p, i, j: (b, q0 + i, p)),
                  pl.BlockSpec((1, tk, 128), lambda b, p, i, j: (b, k0 + j, p)),
                  pl.BlockSpec((1, 128, tk), lambda b, p, i, j: (b, p, k0 + j)),
                  pl.BlockSpec(dlam.shape, lambda b, p, i, j: (0, 0)),
                  pl.BlockSpec(gcol.shape, lambda b, p, i, j: (0, 0))],
        out_specs=pl.BlockSpec((1, 128, tq), lambda b, p, i, j: (b, p, i)),
        scratch_shapes=[pltpu.VMEM((4, 1, tq), F32), pltpu.VMEM((4, VROWS, tq), F32)],
        compiler_params=_cp(("arbitrary",) * 4, VMEM_LIMIT),
        name="diff_attention",
    )(dq, dk, dvT, dlam, gcol)


def _mlstm_kernel(qf_ref, kf_ref, vf_ref, gcf_ref, grf_ref, qb_ref, kb_ref, vb_ref, gcb_ref, grb_ref,
                  hf_ref, hb_ref, c_scr, m_scr):
    t = pl.program_id(1)

    @pl.when(t == 0)
    def _():
        c_scr[...] = jnp.zeros(c_scr.shape, F32)
        m_scr[...] = jnp.zeros(m_scr.shape, F32)

    L = TOK
    si = lax.broadcasted_iota(jnp.int32, (L, L), 0)
    li = lax.broadcasted_iota(jnp.int32, (L, L), 1)
    dirs = ((qf_ref, kf_ref, vf_ref, gcf_ref, grf_ref, hf_ref, si <= li, li <= si, L - 1),
            (qb_ref, kb_ref, vb_ref, gcb_ref, grb_ref, hb_ref, si >= li, li >= si, 0))
    ones = jnp.ones((16, L), F32)
    for d, (q_ref, k_ref, vt_ref, gc_ref, gr_ref, h_ref, seen, seen_t, last) in enumerate(dirs):
        gc = gc_ref[0]
        gr = gr_ref[0]
        bcols = _dot(jnp.where(seen_t, 1.0, 0.0), gc, HI)
        brows = _dot(gr, jnp.where(seen, 1.0, 0.0), HI)
        for hd in range(M_HEADS):
            idx = d * M_HEADS + hd
            ji = d * 8 + hd
            jf = d * 8 + 4 + hd
            sl = slice(M_PAD * hd, M_PAD * (hd + 1))
            q = q_ref[0, :, sl]
            k = k_ref[0, :, sl]
            vt = vt_ref[0, sl, :]
            b_row = brows[jf:jf + 1, :]
            cs = gc[:, ji:ji + 1] - bcols[:, jf:jf + 1]
            li_row = gr[ji:ji + 1, :]
            m_old = m_scr[idx][0:1, 0:1]
            c_old = c_scr[idx]

            dlog = jnp.where(seen, b_row + cs, NEG)
            inter = b_row + m_old
            m_t = jnp.maximum(inter, jnp.max(dlog, axis=0, keepdims=True))
            w_inter = jnp.exp(inter - m_t)
            st = _dot_nt(k, q) * jnp.exp(dlog - m_t)
            cq = _dot_nt(c_old.astype(BF16), q)
            num = w_inter * cq[:M_PAD] + _dot(vt, st.astype(BF16))
            den = w_inter * cq[M_PAD:M_PAD + 1] + jnp.sum(st, axis=0, keepdims=True)
            h_ref[0, sl, :] = num / jnp.maximum(jnp.abs(den), jnp.exp(-m_t))

            total = b_row[:, last:last + 1]
            wlog = total - b_row + li_row
            m_new = jnp.maximum(total + m_old, jnp.max(wlog, axis=1, keepdims=True))
            decay = jnp.exp(total + m_old - m_new)
            w = jnp.exp(wlog - m_new)
            vw = jnp.concatenate([vt.astype(F32) * w, ones * w], axis=0).astype(BF16)
            c_scr[idx] = decay * c_old + _dot(vw, k)
            m_scr[idx] = jnp.broadcast_to(m_new, (8, 128))


def _mlstm(mq, mk, mvT, gl, glT, *, n_lat):
    B, NT, _ = mq.shape
    nt = n_lat + 1
    fwd = lambda t: jnp.where(t == 0, n_lat, t - 1)
    bwd = lambda t: jnp.where(t == 0, n_lat, n_lat - t)
    tok = lambda w, f: pl.BlockSpec((1, TOK, w), lambda b, t: (b, f(t), 0))
    lanes = lambda r, f: pl.BlockSpec((1, r, TOK), lambda b, t: (b, 0, f(t)))
    ins, specs = [], []
    for f in (fwd, bwd):
        ins += [mq, mk, mvT, gl, glT]
        specs += [tok(MP_WIDTH, f)] * 2 + [lanes(MP_WIDTH, f), tok(128, f), lanes(N_GATES, f)]
    return pl.pallas_call(
        _mlstm_kernel,
        out_shape=[jax.ShapeDtypeStruct((B, MP_WIDTH, NT), F32)] * 2,
        grid=(B, nt),
        in_specs=specs,
        out_specs=[lanes(MP_WIDTH, fwd), lanes(MP_WIDTH, bwd)],
        scratch_shapes=[pltpu.VMEM((2 * M_HEADS, M_PAD + 16, M_PAD), F32),
                        pltpu.VMEM((2 * M_HEADS, 8, 128), F32)],
        compiler_params=_cp(("arbitrary", "arbitrary"), VMEM_LIMIT),
        name="mlstm",
    )(*ins)


def _outproj_kernel(x_ref, f_ref, dat_ref, hf_ref, hb_ref, mo_ref, ada_ref, mg_ref, wo_ref, wod_ref, g2_ref, wr_ref,
                    xo_ref, hl_ref, pt_ref, *, is_ctx):
    b = pl.program_id(0)
    mod = ada_ref[4 if is_ctx else b]
    gt1, sh2, sc2 = mod[2:3], mod[3:4], mod[4:5]
    mg = mg_ref[...]
    for s in range(x_ref.shape[1] // TOK):
        tk = slice(TOK * s, TOK * (s + 1))
        hs = hf_ref[0, :, tk] + hb_ref[0, :, tk]
        og = mo_ref[0, :, tk].astype(F32)
        parts = [dat_ref[0, :, tk]]
        for hd in range(M_HEADS):
            sl = slice(M_PAD * hd, M_PAD * (hd + 1))
            hh = hs[sl]
            r = lax.rsqrt(jnp.sum(hh * hh, axis=0, keepdims=True) * (1.0 / M_DIM) + EPS)
            parts.append((((hh * r) * mg[sl]) * _sigmoid(og[sl])).astype(BF16))
        mix_t = jnp.concatenate(parts, axis=0)
        upd = _dot(f_ref[0, tk].astype(BF16), wo_ref[...]) + lax.dot_general(
            mix_t, wod_ref[...], (((0,), (0,)), ((), ())), preferred_element_type=F32)
        xn = x_ref[0, tk] + gt1 * upd
        xo_ref[0, tk] = xn
        r = lax.rsqrt(jnp.mean(xn * xn, axis=-1, keepdims=True) + EPS)
        h2 = (xn * r) * g2_ref[...] * (1.0 + sc2) + sh2
        hl_ref[0, tk] = h2.astype(BF16)
        lt = _dot_nt(wr_ref[...], h2, HI)
        ex = jnp.exp(lt - jnp.max(lt, axis=0, keepdims=True))
        pt_ref[0, :, tk] = ex / jnp.sum(ex, axis=0, keepdims=True)


def _outproj_kernel_aliased(x_ref, f_ref, dat_ref, hf_ref, hb_ref, mo_ref, ada_ref, mg_ref, wo_ref, wod_ref,
                            g2_ref, wr_ref, hlp_ref, xo_ref, hl_ref, pt_ref, *, is_ctx):
    del hlp_ref
    _outproj_kernel(x_ref, f_ref, dat_ref, hf_ref, hb_ref, mo_ref, ada_ref, mg_ref, wo_ref, wod_ref, g2_ref,
                    wr_ref, xo_ref, hl_ref, pt_ref, is_ctx=is_ctx)


def _outproj(xu, f, daT, hf, hb, mo, ada_l, mg, wo, wod, g2, wrT, hl_prev, *, t0, ntl, is_ctx):
    B, NT, _ = xu.shape
    n = ntl * TOK
    tile = 2 * TOK if n % (2 * TOK) == 0 else TOK
    o = t0 * TOK // tile
    tok = lambda w: pl.BlockSpec((1, tile, w), lambda b, t: (b, o + t, 0))
    trs = lambda r: pl.BlockSpec((1, r, tile), lambda b, t: (b, 0, o + t))
    loc = lambda w: pl.BlockSpec((1, tile, w), lambda b, t: (b, t, 0))
    full = lambda a: pl.BlockSpec(a.shape, lambda b, t: (0,) * a.ndim)
    return pl.pallas_call(
        functools.partial(_outproj_kernel_aliased, is_ctx=is_ctx),
        out_shape=[jax.ShapeDtypeStruct(xu.shape, F32), jax.ShapeDtypeStruct((B, NT, D), BF16),
                   jax.ShapeDtypeStruct((B, N_EXPERTS, n), F32)],
        grid=(B, n // tile),
        in_specs=[tok(D), loc(F_WIDTH), pl.BlockSpec((1, DA_WIDTH, tile), lambda b, t: (b, 0, t)),
                  trs(MP_WIDTH), trs(MP_WIDTH), trs(MP_WIDTH),
                  full(ada_l), full(mg), full(wo), full(wod), full(g2), full(wrT),
                  pl.BlockSpec(memory_space=pl.ANY)],
        out_specs=[tok(D), tok(D), pl.BlockSpec((1, N_EXPERTS, tile), lambda b, t: (b, 0, t))],
        input_output_aliases={0: 0, 12: 1},
        compiler_params=_cp(("arbitrary", "arbitrary"), VMEM_LIMIT),
        name="outproj_norm2_router",
    )(xu, f, daT, hf, hb, mo, ada_l, mg, wo, wod, g2, wrT, hl_prev)


def _select_kernel(p_ref, rank_ref, offs_ref, *, n, cap):
    p = p_ref[0]
    xi = pltpu.bitcast(p, jnp.int32)

    def body(i, lo):
        cand = lo | jnp.left_shift(jnp.int32(1), 30 - i)
        cnt = jnp.sum(jnp.where(xi >= cand, 1.0, 0.0), axis=1, keepdims=True)
        return jnp.where(cnt >= cap, cand, lo)

    thr = lax.fori_loop(0, 31, body, jnp.zeros((N_EXPERTS, 1), jnp.int32))
    nb = n // TOK
    rows = lax.broadcasted_iota(jnp.int32, (n, 128), 0)
    cols = lax.broadcasted_iota(jnp.int32, (n, 128), 1)
    blk_ind = jnp.where((rows // TOK) == cols, 1.0, 0.0).astype(BF16)
    u128 = jnp.where(lax.broadcasted_iota(jnp.int32, (128, 128), 0)
                     < lax.broadcasted_iota(jnp.int32, (128, 128), 1), 1.0, 0.0).astype(BF16)
    utok = jnp.where(lax.broadcasted_iota(jnp.int32, (TOK, TOK), 0)
                     < lax.broadcasted_iota(jnp.int32, (TOK, TOK), 1), 1.0, 0.0).astype(BF16)

    def prefix(mf):
        mb = mf.astype(BF16)
        counts = _dot(mb, blk_ind)
        offs = _dot(counts.astype(BF16), u128)
        pieces = [_dot(mb[:, TOK * j:TOK * (j + 1)], utok) + offs[:, j:j + 1] for j in range(nb)]
        return (jnp.concatenate(pieces, axis=1) if nb > 1 else pieces[0]), offs

    gt = xi > thr
    eq = xi == thr
    need = cap - jnp.sum(jnp.where(gt, 1.0, 0.0), axis=1, keepdims=True)
    rank_eq, _ = prefix(jnp.where(eq, 1.0, 0.0))
    sel = gt | (eq & (rank_eq < need))
    rank, offs = prefix(jnp.where(sel, 1.0, 0.0))
    rank_ref[0] = jnp.where(sel, rank, -1.0)
    offs_ref[0] = offs.astype(jnp.int32)


def _select(pt, *, cap):
    B, _, n = pt.shape
    return pl.pallas_call(
        functools.partial(_select_kernel, n=n, cap=cap),
        out_shape=[jax.ShapeDtypeStruct((B, N_EXPERTS, n), F32),
                   jax.ShapeDtypeStruct((B, N_EXPERTS, 128), jnp.int32)],
        grid=(B,),
        in_specs=[pl.BlockSpec((1, N_EXPERTS, n), lambda b: (b, 0, 0))],
        out_specs=[pl.BlockSpec((1, N_EXPERTS, n), lambda b: (b, 0, 0)),
                   pl.BlockSpec((1, N_EXPERTS, 128), lambda b: (b, 0, 0))],
        compiler_params=_cp(("arbitrary",), VMEM_LIMIT),
        name="expert_choice_select",
    )(pt)


def _gather_kernel(offs_ref, h_ref, rank_ref, prob_ref, o_ref, gate_ref, *, eg, per):
    b, g, tb = pl.program_id(0), pl.program_id(1), pl.program_id(2)

    @pl.when(tb == 0)
    def _():
        o_ref[...] = jnp.zeros(o_ref.shape, BF16)
        gate_ref[...] = jnp.zeros(gate_ref.shape, F32)

    h = h_ref[0]
    ntok = h.shape[0]
    cap_pad = o_ref.shape[2]
    win = min(2 * SLOT, cap_pad)

    def add_rows(i, r, p, base, width):
        slots = lax.broadcasted_iota(jnp.int32, (width, ntok), 0).astype(F32) + base.astype(F32)
        hit = r == slots
        rows = _dot(jnp.where(hit, 1.0, 0.0).astype(BF16), h).astype(BF16)
        o_ref[0, i, pl.ds(base, width), :] = o_ref[0, i, pl.ds(base, width), :] + rows
        gate_ref[0, i, pl.ds(base, width), :] = (gate_ref[0, i, pl.ds(base, width), :]
                                                 + jnp.sum(jnp.where(hit, p, 0.0), axis=1, keepdims=True))

    rs, ps, ends, his = [], [], [], []
    for i in range(eg):
        e = g * eg + i
        r = rank_ref[0, pl.ds(e, 1), :]
        p = prob_ref[0, pl.ds(e, 1), :]
        lo = offs_ref[b, e, tb * per]
        his.append(offs_ref[b, e, (tb + 1) * per])
        base = pl.multiple_of(jnp.minimum((lo // SLOT) * SLOT, cap_pad - win), SLOT)
        add_rows(i, r, p, base, win)
        rs.append(r)
        ps.append(p)
        ends.append(base + win)

    for i in range(eg):
        @pl.when(his[i] > ends[i])
        def _(i=i):
            def body(t, carry):
                add_rows(i, rs[i], ps[i], pl.multiple_of(t * SLOT, SLOT), SLOT)
                return carry

            lax.fori_loop(ends[i] // SLOT, (his[i] - 1) // SLOT + 1, body, 0)


def _gather(offs, hl, rank, pt, *, tb_tok, tb0, n, cap_pad, eg):
    B = hl.shape[0]
    per = tb_tok // TOK
    return pl.pallas_call(
        functools.partial(_gather_kernel, eg=eg, per=per),
        out_shape=[jax.ShapeDtypeStruct((B, N_EXPERTS, cap_pad, D), BF16),
                   jax.ShapeDtypeStruct((B, N_EXPERTS, cap_pad, 1), F32)],
        grid_spec=pltpu.PrefetchScalarGridSpec(
            num_scalar_prefetch=1,
            grid=(B, N_EXPERTS // eg, n // tb_tok),
            in_specs=[pl.BlockSpec((1, tb_tok, D), lambda b, g, t, o: (b, tb0 + t, 0)),
                      pl.BlockSpec((1, N_EXPERTS, tb_tok), lambda b, g, t, o: (b, 0, t)),
                      pl.BlockSpec((1, N_EXPERTS, tb_tok), lambda b, g, t, o: (b, 0, t))],
            out_specs=[pl.BlockSpec((1, eg, cap_pad, D), lambda b, g, t, o: (b, g, 0, 0)),
                       pl.BlockSpec((1, eg, cap_pad, 1), lambda b, g, t, o: (b, g, 0, 0))]),
        compiler_params=_cp(("arbitrary",) * 3, VMEM_LIMIT),
        name="expert_gather",
    )(offs, hl, rank, pt)


FFN_ROWS = 1024


def _ffn_kernel(x_ref, gate_ref, w1_ref, w3_ref, w2_ref, y_ref, acc_ref):
    f = pl.program_id(2)

    @pl.when(f == 0)
    def _():
        acc_ref[...] = jnp.zeros(acc_ref.shape, F32)

    w1 = w1_ref[0, 0].astype(BF16)
    w3 = w3_ref[0, 0].astype(BF16)
    w2 = w2_ref[0, 0].astype(BF16)
    mb, _, cap_pad, _ = x_ref.shape
    rows = min(FFN_ROWS, cap_pad)
    for i in range(mb):
        for r in range(0, cap_pad, rows):
            x = x_ref[i, 0, r:r + rows, :]
            hid = (_silu(_dot(x, w1)) * _dot(x, w3)).astype(BF16)
            acc_ref[i * cap_pad + r:i * cap_pad + r + rows, :] += _dot(hid, w2)

    @pl.when(f == pl.num_programs(2) - 1)
    def _():
        gate = gate_ref[...].reshape(-1, 1)
        y_ref[...] = (acc_ref[...] * gate).astype(BF16).reshape(y_ref.shape)


def _ffn(xs, gates, w1, w3, w2, *, layer, mb, tf):
    B, E, cap_pad, _ = xs.shape
    return pl.pallas_call(
        _ffn_kernel,
        out_shape=jax.ShapeDtypeStruct(xs.shape, BF16),
        grid=(E, B // mb, D_FF // tf),
        in_specs=[pl.BlockSpec((mb, 1, cap_pad, D), lambda e, m, f: (m, e, 0, 0)),
                  pl.BlockSpec((mb, 1, cap_pad, 1), lambda e, m, f: (m, e, 0, 0)),
                  pl.BlockSpec((1, 1, D, tf), lambda e, m, f: (layer, e, 0, f)),
                  pl.BlockSpec((1, 1, D, tf), lambda e, m, f: (layer, e, 0, f)),
                  pl.BlockSpec((1, 1, tf, D), lambda e, m, f: (layer, e, f, 0))],
        out_specs=pl.BlockSpec((mb, 1, cap_pad, D), lambda e, m, f: (m, e, 0, 0)),
        scratch_shapes=[pltpu.VMEM((mb * cap_pad, D), F32)],
        compiler_params=_cp(("arbitrary",) * 3, VMEM_LIMIT),
        name="expert_ffn",
    )(xs, gates, w1, w3, w2)


CCOL = 512


def _combine_kernel(offs_ref, x_ref, y_ref, rankc_ref, ada_ref, o_ref, tot_scr, *, per, is_ctx):
    b, tb = pl.program_id(0), pl.program_id(2)
    gt2 = ada_ref[4 if is_ctx else b][5:6]
    rc_all = rankc_ref[0]
    cap_pad = y_ref.shape[2]
    win = min(2 * SLOT, cap_pad)
    slotw = lax.broadcasted_iota(jnp.int32, (1, win), 1).astype(F32)
    slot = lax.broadcasted_iota(jnp.int32, (1, SLOT), 1).astype(F32)

    bases, his = [], []
    total = jnp.zeros(tot_scr.shape, F32)
    for e in range(N_EXPERTS):
        lo = offs_ref[b, e, tb * per]
        his.append(offs_ref[b, e, (tb + 1) * per])
        base = pl.multiple_of(jnp.minimum((lo // SLOT) * SLOT, cap_pad - win), SLOT)
        bases.append(base)
        onehot = jnp.where(rc_all[:, e:e + 1] == slotw + base.astype(F32), 1.0, 0.0).astype(BF16)
        total = total + _dot(onehot, y_ref[0, e, pl.ds(base, win), :])
    tot_scr[...] = total

    for e in range(N_EXPERTS):
        end = bases[e] + win

        @pl.when(his[e] > end)
        def _(e=e, end=end):
            def body(t, carry):
                base = pl.multiple_of(t * SLOT, SLOT)
                onehot = jnp.where(rc_all[:, e:e + 1] == slot + base.astype(F32), 1.0, 0.0).astype(BF16)
                tot_scr[...] += _dot(onehot, y_ref[0, e, pl.ds(base, SLOT), :])
                return carry

            lax.fori_loop(end // SLOT, (his[e] - 1) // SLOT + 1, body, 0)

    o_ref[0] = x_ref[0] + gt2 * tot_scr[...]


def _combine(offs, xu, ys, rank_c, ada_l, *, tb_tok, tb0, n, is_ctx):
    B = xu.shape[0]
    cap_pad = ys.shape[2]
    per = tb_tok // TOK
    return pl.pallas_call(
        functools.partial(_combine_kernel, per=per, is_ctx=is_ctx),
        out_shape=jax.ShapeDtypeStruct(xu.shape, F32),
        grid_spec=pltpu.PrefetchScalarGridSpec(
            num_scalar_prefetch=1,
            grid=(B, D // CCOL, n // tb_tok),
            in_specs=[pl.BlockSpec((1, tb_tok, CCOL), lambda b, c, t, o: (b, tb0 + t, c)),
                      pl.BlockSpec((1, N_EXPERTS, cap_pad, CCOL), lambda b, c, t, o: (b, 0, 0, c),
                                   pipeline_mode=pl.Buffered(1)),
                      pl.BlockSpec((1, tb_tok, N_EXPERTS), lambda b, c, t, o: (b, t, 0)),
                      pl.BlockSpec((8, ADA_CHUNKS, CCOL), lambda b, c, t, o: (0, 0, c))],
            out_specs=pl.BlockSpec((1, tb_tok, CCOL), lambda b, c, t, o: (b, tb0 + t, c)),
            scratch_shapes=[pltpu.VMEM((tb_tok, CCOL), F32)]),
        input_output_aliases={1: 0},
        compiler_params=_cp(("arbitrary",) * 3, VMEM_LIMIT),
        name="expert_combine",
    )(offs, xu, ys, rank_c, ada_l)


def _moe(xu, hl, pt, ada_l, w1, w3, w2, *, layer, row0, is_ctx):
    B, _, n = pt.shape
    cap = EC_CAPACITY * n // N_EXPERTS
    cap_pad = -(-cap // SLOT) * SLOT
    nb = n // TOK
    rank, offs = _select(pt, cap=cap)
    offs = offs[:, :, :nb + 1]
    gt = min(n, 512)
    ct = min(n, 512)
    xs, gates = _gather(offs, hl, rank, pt, tb_tok=gt, tb0=row0 // gt, n=n, cap_pad=cap_pad, eg=4)
    mb = 2 if (B % 2 == 0 and cap_pad >= 1024) else (B if cap_pad < 1024 else 1)
    ys = _ffn(xs, gates, w1, w3, w2, layer=layer, mb=mb, tf=512)
    rank_c = jnp.swapaxes(rank, 1, 2)
    return _combine(offs, xu, ys, rank_c, ada_l, tb_tok=ct, tb0=row0 // ct, n=n, is_ctx=is_ctx)


def _final_kernel(x_ref, g_ref, o_ref):
    x = x_ref[0]
    r = lax.rsqrt(jnp.mean(x * x, axis=-1, keepdims=True) + EPS)
    o_ref[0] = (x * r) * g_ref[...]


def _final_norm(xu, g, *, n):
    B = xu.shape[0]
    tm = 512
    return pl.pallas_call(
        _final_kernel,
        out_shape=jax.ShapeDtypeStruct((B, n, D), F32),
        grid=(B, n // tm),
        in_specs=[pl.BlockSpec((1, tm, D), lambda b, t: (b, t, 0)),
                  pl.BlockSpec((1, D), lambda b, t: (0, 0))],
        out_specs=pl.BlockSpec((1, tm, D), lambda b, t: (b, t, 0)),
        compiler_params=_cp(("arbitrary", "arbitrary")),
        name="final_norm",
    )(xu, g)


def _rope_tables(n, ctx):
    rows = n // GRID_W
    t_row = jnp.repeat(jnp.arange(rows), GRID_W)
    t_col = jnp.tile(jnp.arange(GRID_W), rows)
    nf = DA_DIM // 4
    inv = ROPE_THETA ** (-jnp.arange(nf, dtype=F32) / nf)
    ar = t_row[:, None].astype(F32) * inv
    ac = t_col[:, None].astype(F32) * inv
    ang = jnp.concatenate([ar, ar, ac, ac], axis=-1)
    sign = jnp.where((jnp.arange(DA_DIM) % 16) < 8, -1.0, 1.0).astype(F32)
    cos = jnp.concatenate([jnp.cos(ang), jnp.ones((ctx, DA_DIM), F32)], axis=0)
    sin = jnp.concatenate([jnp.sin(ang) * sign, jnp.zeros((ctx, DA_DIM), F32)], axis=0)
    return jnp.tile(cos, (1, 128 // DA_DIM)), jnp.tile(sin, (1, 128 // DA_DIM))


def _pad_heads_cols(w):
    lead = w.shape[:-1]
    w = w.reshape(lead + (M_HEADS, M_DIM))
    w = jnp.pad(w, [(0, 0)] * len(lead) + [(0, 0), (0, M_PAD - M_DIM)])
    return w.reshape(lead + (MP_WIDTH,))


def _kv_tile(nt):
    for parts in range(1, nt // 128 + 1):
        if nt % parts == 0 and (nt // parts) % 128 == 0 and nt // parts <= 1408:
            return nt // parts
    raise ValueError(nt)


def kernel(x, c, ctx, c_ctx, ada_w, ada_b, norm1_g, norm2_g, w_in, four_w, m_conv_w, m_conv_b, m_gate_b,
           m_norm_g, d_lam, d_norm_g, w_out, router_w, exp_w1, exp_w3, exp_w2, final_g):
    B, N, _ = x.shape
    CTX = ctx.shape[1]
    depth = w_in.shape[0]
    assert CTX == TOK and N % (FFT_N1 * TOK) == 0 and B <= 4
    NT = N + CTX
    PAD = 1024 - CTX
    n_lat = N // TOK
    n2 = N // FFT_N1

    xu = jnp.concatenate([x, ctx, jnp.zeros((B, PAD, D), F32)], axis=1)
    cvecs = jnp.zeros((8, D), F32).at[:B].set(c).at[4].set(c_ctx)
    ada = _adaln(cvecs, ada_w, ada_b).reshape(depth, 8, ADA_CHUNKS, D)
    cos_t, sin_t = _rope_tables(N, CTX + PAD)
    tabs = _fourier_tables(N, CTX)
    tk = _kv_tile(NT)
    tq = 2048

    hl = jnp.zeros((B, NT + PAD, D), BF16)
    for layer in range(depth):
        ctx_out = layer < depth - 1
        lam_init = 0.8 - 0.6 * math.exp(-0.3 * layer)
        w = w_in[layer]
        wm = jnp.concatenate([w[:, OFF_F:OFF_DQ], w[:, OFF_DQ:OFF_MO], w[:, OFF_DK:OFF_DV]], axis=1).astype(BF16)
        wvt = jnp.concatenate([w[:, OFF_DV:OFF_MV], _pad_heads_cols(w[:, OFF_MO:OFF_MQ]),
                               _pad_heads_cols(w[:, OFF_MV:OFF_G])], axis=1).T.astype(BF16)
        wgt = w[:, OFF_G:].T
        wc = jnp.concatenate([_pad_heads_cols(w[:, OFF_MQ:OFF_MK]), _pad_heads_cols(w[:, OFF_MK:OFF_DK])],
                             axis=1).astype(BF16)
        gb = m_gate_b[layer].reshape(N_GATES, 1)
        cw = jnp.concatenate([_pad_heads_cols(m_conv_w[layer][:, :M_WIDTH]),
                              _pad_heads_cols(m_conv_w[layer][:, M_WIDTH:])], axis=1)
        cb = jnp.concatenate([_pad_heads_cols(m_conv_b[layer][:M_WIDTH]),
                              _pad_heads_cols(m_conv_b[layer][M_WIDTH:])]).reshape(1, 2 * MP_WIDTH)
        ada_l = ada[layer]

        y4, dq, dk, dvT, mo, mq, mk, mv, gl, glT = _inproj(
            xu, ada_l, norm1_g[layer].reshape(1, D), wm, wc, wvt, wgt, gb, tabs["cs"], cos_t, sin_t, cw, cb,
            n_lat=n_lat, n2=n2)

        wblk = jnp.zeros((F_WIDTH, F_WIDTH), F32)
        for g in range(F_GROUPS):
            wblk = wblk.at[F_GDIM * g:F_GDIM * (g + 1), F_GDIM * g:F_GDIM * (g + 1)].set(four_w[layer, g])
        f_l, f_c = _fourier(y4, tabs, wblk.astype(BF16), n=N, ctx=CTX, with_ctx=ctx_out)

        dlam = d_lam[layer]
        g2 = d_norm_g[layer].reshape(DA_VDIM, 1)
        da_l = _attention(dq, dk, dvT, dlam, g2, lam_init=lam_init, tq=tq, q0=0, nq=N // tq,
                          tk=tk, k0=0, nk=NT // tk)

        hf, hb = _mlstm(mq, mk, mv, gl, glT, n_lat=n_lat)

        mg = _pad_heads_cols(m_norm_g[layer]).reshape(MP_WIDTH, 1)
        wol = w_out[layer]
        wo = wol[:F_WIDTH].astype(BF16)
        wod = jnp.concatenate([wol[F_WIDTH:F_WIDTH + DA_WIDTH],
                               jnp.pad(wol[F_WIDTH + DA_WIDTH:].reshape(M_HEADS, M_DIM, D),
                                       ((0, 0), (0, M_PAD - M_DIM), (0, 0))).reshape(MP_WIDTH, D)],
                              axis=0).astype(BF16)
        g2n = norm2_g[layer].reshape(1, D)
        wrT = router_w[layer].T
        xu, hl, pt_l = _outproj(xu, f_l, da_l, hf, hb, mo, ada_l, mg, wo, wod, g2n, wrT, hl,
                                t0=0, ntl=n_lat, is_ctx=False)
        if ctx_out:
            da_c = _attention(dq, dk, dvT, dlam, g2, lam_init=lam_init, tq=TOK, q0=n_lat, nq=1,
                              tk=TOK, k0=n_lat, nk=1)
            xu, hl, pt_c = _outproj(xu, f_c, da_c, hf, hb, mo, ada_l, mg, wo, wod, g2n, wrT, hl,
                                    t0=n_lat, ntl=1, is_ctx=True)

        xu = _moe(xu, hl, pt_l, ada_l, exp_w1, exp_w3, exp_w2, layer=layer, row0=0, is_ctx=False)
        if ctx_out:
            xu = _moe(xu, hl, pt_c, ada_l, exp_w1, exp_w3, exp_w2, layer=layer, row0=N, is_ctx=True)

    return _final_norm(xu, final_g.reshape(1, D), n=N)
```

```python
import functools
import math

import numpy as np
import jax
import jax.numpy as jnp
from jax import lax
from jax.experimental import pallas as pl
from jax.experimental.pallas import tpu as pltpu

F32 = jnp.float32
BF16 = jnp.bfloat16
HI = lax.Precision.HIGHEST

D = 1024
EPS = 1e-6
GRID_W = 64
ROPE_THETA = 10000.0
F_GROUPS, F_GDIM = 4, 64
F_WIDTH = F_GROUPS * F_GDIM
DA_HEADS, DA_DIM = 6, 32
DA_VDIM = 2 * DA_DIM
DA_WIDTH = DA_HEADS * DA_VDIM
M_HEADS, M_DIM = 4, 96
M_WIDTH = M_HEADS * M_DIM
M_PAD = 128
MP_WIDTH = M_HEADS * M_PAD
N_GATES = 4 * M_HEADS
N_EXPERTS = 16
EC_CAPACITY = 2
D_FF = 2 * D
ADA_CHUNKS = 6
ADA_ROWS = 8
CTX_ROW = 4

LANES = 128
MXU_DIM = 256
V7X_VMEM_BYTES = 64 * 1024 * 1024

TOK = MXU_DIM
FFT_N1 = 16
SLOT = LANES
Q_TILE = 2048
KV_TILE_MAX = 1408
MOE_TOK = 512
FFN_TF = 512
GATHER_EXPERTS = 4
NEG = -1e30

OFF_F = 0
OFF_DQ = OFF_F + F_WIDTH
OFF_MO = OFF_DQ + 2 * DA_HEADS * DA_DIM
OFF_MQ = OFF_MO + M_WIDTH
OFF_MK = OFF_MQ + M_WIDTH
OFF_DK = OFF_MK + M_WIDTH
OFF_DV = OFF_DK + 2 * DA_HEADS * DA_DIM
OFF_MV = OFF_DV + DA_HEADS * DA_VDIM
OFF_G = OFF_MV + M_WIDTH

VMEM_LIMIT = V7X_VMEM_BYTES * 7 // 8


def _cp(sem, vmem=None):
    return pltpu.CompilerParams(dimension_semantics=sem, vmem_limit_bytes=vmem)


def _sigmoid(x):
    return 1.0 / (1.0 + jnp.exp(-x))


def _silu(x):
    return x * _sigmoid(x)


def _dot(a, b, precision=None):
    return jnp.dot(a, b, preferred_element_type=F32, precision=precision)


def _split(a):
    hi = a.astype(BF16)
    return hi, (a - hi.astype(F32)).astype(BF16)


def _dot3(a, b):
    a_hi, a_lo = a if isinstance(a, tuple) else _split(a)
    b_hi, b_lo = b if isinstance(b, tuple) else _split(b)
    return _dot(a_hi, b_hi) + _dot(a_hi, b_lo) + _dot(a_lo, b_hi)


def _dot_nt(a, b, precision=None):
    return lax.dot_general(a, b, (((1,), (1,)), ((), ())), preferred_element_type=F32,
                           precision=precision)


def _ada_kernel(c_ref, w_ref, b_ref, o_ref):
    c = c_ref[...]
    o_ref[0] = _dot(_silu(c), w_ref[0], HI) + b_ref[0]


def _adaln(cvecs, ada_w, ada_b):
    depth = ada_w.shape[0]
    tn = 1536
    return pl.pallas_call(
        _ada_kernel,
        out_shape=jax.ShapeDtypeStruct((depth, ADA_ROWS, ADA_CHUNKS * D), F32),
        grid=(depth, ADA_CHUNKS * D // tn),
        in_specs=[pl.BlockSpec((ADA_ROWS, D), lambda l, j: (0, 0)),
                  pl.BlockSpec((1, D, tn), lambda l, j: (l, 0, j)),
                  pl.BlockSpec((1, 1, tn), lambda l, j: (l, 0, j))],
        out_specs=pl.BlockSpec((1, 8, tn), lambda l, j: (l, 0, j)),
        compiler_params=_cp(("arbitrary", "arbitrary")),
        name="adaln",
    )(cvecs, ada_w, ada_b.reshape(depth, 1, ADA_CHUNKS * D))


def _inproj_kernel(x_ref, xp_ref, xn_ref, ada_ref, g_ref, wm_ref, wc_ref, wvt_ref, wgt_ref, gb_ref, cs_ref,
                   cos_ref, sin_ref, cw_ref, cb_ref,
                   y_ref, dq_ref, dk_ref, dvt_ref, mo_ref, mq_ref, mk_ref, mv_ref, gl_ref, glt_ref, *, n_lat):
    b = pl.program_id(0)
    t = pl.program_id(1)
    n_tiles = pl.num_programs(1)
    is_ctx = t >= n_lat
    row = jnp.where(is_ctx, CTX_ROW, b)
    mod = ada_ref[row]
    sh, sc = mod[0:1], mod[1:2]

    xa = jnp.concatenate([xp_ref[0], x_ref[0], xn_ref[0]], axis=0)
    r = lax.rsqrt(jnp.mean(xa * xa, axis=-1, keepdims=True) + EPS)
    ha = (xa * r) * g_ref[...] * (1.0 + sc) + sh
    h = ha[8:8 + TOK]
    hb = h.astype(BF16)

    pm = _dot(hb, wm_ref[...])
    o = 0
    pf = pm[:, o:o + F_WIDTH]; o += F_WIDTH
    q = pm[:, o:o + DA_WIDTH]; o += DA_WIDTH
    k = pm[:, o:o + DA_WIDTH]; o += DA_WIDTH
    pt = _dot_nt(wvt_ref[...], hb)
    dvt_ref[0] = pt[:DA_WIDTH].astype(BF16)
    mo_ref[0] = pt[DA_WIDTH:DA_WIDTH + MP_WIDTH].astype(BF16)
    mv_ref[0] = pt[DA_WIDTH + MP_WIDTH:].astype(BF16)

    y_ref[0, 0] = _dot3(pf, cs_ref[...])

    cos = cos_ref[...]
    sin = sin_ref[...]
    lane = lax.broadcasted_iota(jnp.int32, (1, 128), 1)
    low = (lane % 16) < 8

    def rope(z):
        parts = []
        for c in range(DA_WIDTH // 128):
            zc = z[:, 128 * c:128 * (c + 1)]
            rot = jnp.where(low, pltpu.roll(zc, 120, 1), pltpu.roll(zc, 8, 1))
            parts.append(zc * cos + rot * sin)
        return jnp.concatenate(parts, axis=1)

    dq_ref[0] = (rope(q) * (DA_DIM ** -0.5 * math.log2(math.e))).astype(BF16)
    dk_ref[0] = rope(k).astype(BF16)

    gpre = _dot_nt(wgt_ref[...], h, HI) + gb_ref[...]
    is_forget = (lax.broadcasted_iota(jnp.int32, (N_GATES, 1), 0) % 8) >= 4
    logsig = jnp.minimum(gpre, 0.0) - jnp.log(1.0 + jnp.exp(-jnp.abs(gpre)))
    glt = jnp.where(is_forget, logsig, gpre)
    glt_ref[0] = glt
    gl_ref[0] = jnp.concatenate([glt, jnp.zeros((128 - N_GATES, TOK), F32)], axis=0).T

    pc = _dot(ha.astype(BF16), wc_ref[...])
    first = (t == 0) | (t == n_lat)
    last = (t == n_lat - 1) | (t == n_tiles - 1)
    ridx = lax.broadcasted_iota(jnp.int32, (TOK + 16, 1), 0)
    pc = jnp.where(((ridx < 8) & first) | ((ridx >= TOK + 8) & last), 0.0, pc)
    cw = cw_ref[...]
    conv = cb_ref[...] + pc[7:7 + TOK] * cw[0:1] + pc[8:8 + TOK] * cw[1:2] + pc[9:9 + TOK] * cw[2:3]
    act = _silu(conv)
    mq_ref[0] = act[:, :MP_WIDTH].astype(BF16)
    mk_ref[0] = (act[:, MP_WIDTH:] * (M_DIM ** -0.5)).astype(BF16)


def _inproj(xu, ada_l, g1, wm, wc, wvt, wgt, gb, cs, cos_t, sin_t, cw, cb, *, n_lat, n2):
    B, NT, _ = xu.shape
    nt = n_lat + 1
    rper = n2 // TOK
    tok3 = lambda w: pl.BlockSpec((1, TOK, w), lambda b, t: (b, t, 0))
    full = lambda a: pl.BlockSpec(a.shape, lambda b, t: (0,) * a.ndim)
    nb8 = NT // 8
    outs = [jax.ShapeDtypeStruct((B, 2 * FFT_N1, n2, 2 * F_WIDTH), F32)]
    nq = -(-NT // Q_TILE) * Q_TILE
    outs += [jax.ShapeDtypeStruct((B, nq, DA_WIDTH), BF16), jax.ShapeDtypeStruct((B, nt * TOK, DA_WIDTH), BF16)]
    outs += [jax.ShapeDtypeStruct((B, DA_WIDTH, nt * TOK), BF16)]
    trs = lambda r: pl.BlockSpec((1, r, TOK), lambda b, t: (b, 0, t))
    trp = jax.ShapeDtypeStruct((B, MP_WIDTH, NT), BF16)
    outs += [trp, jax.ShapeDtypeStruct((B, NT, MP_WIDTH), BF16), jax.ShapeDtypeStruct((B, NT, MP_WIDTH), BF16), trp]
    outs += [jax.ShapeDtypeStruct((B, NT, 128), F32), jax.ShapeDtypeStruct((B, N_GATES, nt * TOK), F32)]
    out_specs = [pl.BlockSpec((1, 1, TOK, 2 * F_WIDTH), lambda b, t: (b, t // rper, t % rper, 0))]
    out_specs += [tok3(DA_WIDTH)] * 2 + [trs(DA_WIDTH)]
    out_specs += [trs(MP_WIDTH), tok3(MP_WIDTH), tok3(MP_WIDTH), trs(MP_WIDTH)]
    out_specs += [tok3(128), pl.BlockSpec((1, N_GATES, TOK), lambda b, t: (b, 0, t))]
    return pl.pallas_call(
        functools.partial(_inproj_kernel, n_lat=n_lat),
        out_shape=outs,
        grid=(B, nt),
        in_specs=[tok3(D),
                  pl.BlockSpec((1, 8, D), lambda b, t: (b, jnp.maximum(t * (TOK // 8) - 1, 0), 0)),
                  pl.BlockSpec((1, 8, D), lambda b, t: (b, jnp.minimum((t + 1) * (TOK // 8), nb8 - 1), 0)),
                  full(ada_l), full(g1), full(wm), full(wc), full(wvt), full(wgt), full(gb), full(cs),
                  pl.BlockSpec((TOK, 128), lambda b, t: (t, 0)),
                  pl.BlockSpec((TOK, 128), lambda b, t: (t, 0)),
                  full(cw), full(cb)],
        out_specs=out_specs,
        compiler_params=_cp(("arbitrary", "arbitrary"), VMEM_LIMIT),
        name="norm1_inproj",
    )(xu, xu, xu, ada_l, g1, wm, wc, wvt, wgt, gb, cs, cos_t, sin_t, cw, cb)


def _fft1_kernel(y_ref, kc_ref, ks_ref, tc_ref, ts_ref, o_ref, *, groups):
    kc = _split(kc_ref[...])
    ks = _split(ks_ref[...])
    for g in range(groups):
        blk = _split(y_ref[0, :, 8 * g:8 * (g + 1), :].reshape(FFT_N1 * 8, 2 * F_WIDTH))
        p = _dot3(kc, blk)
        q = _dot3(ks, blk)
        ar = p[:, :F_WIDTH] - q[:, F_WIDTH:]
        ai = -p[:, F_WIDTH:] - q[:, :F_WIDTH]
        tc = tc_ref[128 * g:128 * (g + 1), :]
        ts = ts_ref[128 * g:128 * (g + 1), :]
        tc = jnp.concatenate([tc, tc], axis=1)
        ts = jnp.concatenate([ts, ts], axis=1)
        br = ar * tc + ai * ts
        bi = ai * tc - ar * ts
        o_ref[0, :, 8 * g:8 * (g + 1), :] = jnp.concatenate([br, bi], axis=1).reshape(FFT_N1, 8, 2 * F_WIDTH)


def _fft2_kernel(b_ref, c2_ref, s2_ref, wb_ref, perm_ref, o_ref, r_scr, *, n2):
    c2 = _split(c2_ref[...])
    s2 = _split(s2_ref[...])
    for i in range(8):
        blk = b_ref[0, i]
        xr = _dot3(c2, blk[:, :F_WIDTH]) + _dot3(s2, blk[:, F_WIDTH:])
        r_scr[i] = _dot(xr.astype(BF16), wb_ref[...]).astype(BF16)
    for t in range(n2 // 32):
        rows = jnp.concatenate([r_scr[i, 32 * t:32 * (t + 1), :] for i in range(8)], axis=0)
        o_ref[0, 32 * t:32 * (t + 1), :, :] = _dot(perm_ref[...], rows).reshape(32, 8, F_WIDTH)


def _fftc_kernel(y_ref, c_ref, s_ref, wb_ref, o_ref):
    y = y_ref[0, 0]
    z = _dot3(c_ref[...], y[:, :F_WIDTH]) - _dot3(s_ref[...], y[:, F_WIDTH:])
    o_ref[0] = _dot(z.astype(BF16), wb_ref[...])


def _fourier_tables(n, ctx):
    n1, n2 = FFT_N1, n // FFT_N1
    a = np.arange(n1)
    ang1 = 2 * np.pi * np.outer(a, a) / n1
    eye8 = np.eye(8)
    kc = np.kron(np.cos(ang1), eye8)
    ks = np.kron(np.sin(ang1), eye8)
    n2i = np.arange(n2).reshape(n2 // 8, 1, 8)
    k1 = np.arange(n1).reshape(1, n1, 1)
    angt = (2 * np.pi * n2i * k1 / n).reshape(-1, 1)
    tc = np.broadcast_to(np.cos(angt), (n2 // 8 * 128, 128))
    ts = np.broadcast_to(np.sin(angt), (n2 // 8 * 128, 128))
    b = np.arange(n2)
    ang2 = 2 * np.pi * np.outer(b, b) / n2
    c2 = np.cos(ang2) / math.sqrt(n)
    s2 = np.sin(ang2) / math.sqrt(n)
    perm = np.zeros((256, 256))
    for kk in range(8):
        for j in range(32):
            perm[j * 8 + kk, kk * 32 + j] = 1.0
    cc = np.arange(ctx)
    angc = 2 * np.pi * np.outer(cc, cc) / ctx
    cctx = np.cos(angc) / math.sqrt(ctx)
    sctx = np.sin(angc) / math.sqrt(ctx)
    ch = np.arange(F_GDIM)
    angch = 2 * np.pi * np.outer(ch, ch) / F_GDIM
    cs = np.concatenate([np.kron(np.eye(F_GROUPS), np.cos(angch)),
                         np.kron(np.eye(F_GROUPS), np.sin(angch))], axis=1) / math.sqrt(F_GDIM)
    f = lambda z: jnp.asarray(np.ascontiguousarray(z), dtype=F32)
    return dict(kc=f(kc), ks=f(ks), tc=f(tc), ts=f(ts), c2=f(c2), s2=f(s2), perm=f(perm).astype(BF16),
                cctx=f(cctx), sctx=f(sctx), cs=f(cs))


def _fourier(y4, tabs, wblk, *, n, ctx, with_ctx):
    B = y4.shape[0]
    n2 = n // FFT_N1
    groups = 4
    full = lambda a, nd: pl.BlockSpec(a.shape, lambda *i: (0,) * a.ndim)
    b4 = pl.pallas_call(
        functools.partial(_fft1_kernel, groups=groups),
        out_shape=jax.ShapeDtypeStruct((B, FFT_N1, n2, 2 * F_WIDTH), F32),
        grid=(B, n2 // (8 * groups)),
        in_specs=[pl.BlockSpec((1, FFT_N1, 8 * groups, 2 * F_WIDTH), lambda b, j: (b, 0, j, 0)),
                  full(tabs["kc"], 2), full(tabs["ks"], 2),
                  pl.BlockSpec((128 * groups, 128), lambda b, j: (j, 0)),
                  pl.BlockSpec((128 * groups, 128), lambda b, j: (j, 0))],
        out_specs=pl.BlockSpec((1, FFT_N1, 8 * groups, 2 * F_WIDTH), lambda b, j: (b, 0, j, 0)),
        compiler_params=_cp(("arbitrary", "arbitrary")),
        name="fourier_stage1",
    )(y4, tabs["kc"], tabs["ks"], tabs["tc"], tabs["ts"])
    f4 = pl.pallas_call(
        functools.partial(_fft2_kernel, n2=n2),
        out_shape=jax.ShapeDtypeStruct((B, n2, 16, F_WIDTH), F32),
        grid=(B, FFT_N1 // 8),
        in_specs=[pl.BlockSpec((1, 8, n2, 2 * F_WIDTH), lambda b, j: (b, j, 0, 0)),
                  full(tabs["c2"], 2), full(tabs["s2"], 2), full(wblk, 2), full(tabs["perm"], 2)],
        out_specs=pl.BlockSpec((1, n2, 8, F_WIDTH), lambda b, j: (b, 0, j, 0)),
        scratch_shapes=[pltpu.VMEM((8, n2, F_WIDTH), BF16)],
        compiler_params=_cp(("arbitrary", "arbitrary"), VMEM_LIMIT),
        name="fourier_stage2",
    )(b4, tabs["c2"], tabs["s2"], wblk, tabs["perm"])
    f_ctx = None
    if with_ctx:
        f_ctx = pl.pallas_call(
            _fftc_kernel,
            out_shape=jax.ShapeDtypeStruct((B, ctx, F_WIDTH), F32),
            grid=(B,),
            in_specs=[pl.BlockSpec((1, 1, TOK, 2 * F_WIDTH), lambda b: (b, FFT_N1, 0, 0)),
                      full(tabs["cctx"], 1), full(tabs["sctx"], 1), full(wblk, 1)],
            out_specs=pl.BlockSpec((1, ctx, F_WIDTH), lambda b: (b, 0, 0)),
            compiler_params=_cp(("arbitrary",)),
            name="fourier_ctx",
        )(y4, tabs["cctx"], tabs["sctx"], wblk)
    return f4.reshape(B, n, F_WIDTH), f_ctx


VROWS = DA_VDIM + 16


def _attn_kernel(q_ref, k_ref, vt_ref, dl_ref, g_ref, o_ref, m_scr, acc_scr, *, lam_init):
    kt = pl.program_id(3)
    nk = pl.num_programs(3)

    @pl.when(kt == 0)
    def _():
        m_scr[...] = jnp.full(m_scr.shape, NEG, F32)
        acc_scr[...] = jnp.zeros(acc_scr.shape, F32)

    q = q_ref[0]
    k = k_ref[0]
    vt = vt_ref[0]
    ones = jnp.ones((16, vt.shape[1]), BF16)
    lhs = [jnp.concatenate([vt[DA_VDIM * h:DA_VDIM * (h + 1)], ones], axis=0) for h in range(2)]
    lane = lax.broadcasted_iota(jnp.int32, (1, 128), 1)
    zero = jnp.zeros((), BF16)
    def scores(j):
        return _dot_nt(k, jnp.where((lane // DA_DIM) == j, q, zero))

    st_next = scores(0)
    for j in range(4):
        st = st_next
        if j < 3:
            st_next = scores(j + 1)
        m_old = m_scr[j]
        m_new = jnp.maximum(m_old, jnp.max(st, axis=0, keepdims=True))
        alpha = jnp.exp2(m_old - m_new)
        pt = jnp.exp2(st - m_new).astype(BF16)
        acc_scr[j] = alpha * acc_scr[j] + _dot(lhs[j // 2], pt)
        m_scr[j] = m_new

    @pl.when(kt == nk - 1)
    def _():
        dl = dl_ref[...]
        lam = (jnp.exp(jnp.sum(dl[0:1] * dl[1:2], keepdims=True))
               - jnp.exp(jnp.sum(dl[2:3] * dl[3:4], keepdims=True)) + lam_init)
        outs = []
        for h in range(2):
            a0 = acc_scr[2 * h]
            a1 = acc_scr[2 * h + 1]
            o = (a0[:DA_VDIM] / a0[DA_VDIM:DA_VDIM + 1]
                 - lam * (a1[:DA_VDIM] / a1[DA_VDIM:DA_VDIM + 1]))
            r = lax.rsqrt(jnp.mean(o * o, axis=0, keepdims=True) + EPS)
            outs.append(((o * r) * g_ref[...]) * (1.0 - lam_init))
        o_ref[0] = jnp.concatenate(outs, axis=0).astype(BF16)


def _attention(dq, dk, dvT, dlam, gcol, *, lam_init, tq, q0, nq, tk, k0, nk):
    B = dq.shape[0]
    return pl.pallas_call(
        functools.partial(_attn_kernel, lam_init=lam_init),
        out_shape=jax.ShapeDtypeStruct((B, DA_WIDTH, nq * tq), BF16),
        grid=(B, DA_WIDTH // 128, nq, nk),
        in_specs=[pl.BlockSpec((1, tq, 128), lambda b, p, i, j: (b, q0 + i, p)),
                  pl.BlockSpec((1, tk, 128), lambda b, p, i, j: (b, k0 + j, p)),
                  pl.BlockSpec((1, 128, tk), lambda b, p, i, j: (b, p, k0 + j)),
                  pl.BlockSpec(dlam.shape, lambda b, p, i, j: (0, 0)),
                  pl.BlockSpec(gcol.shape, lambda b, p, i, j: (0, 0))],
        out_specs=pl.BlockSpec((1, 128, tq), lambda b, p, i, j: (b, p, i)),
        scratch_shapes=[pltpu.VMEM((4, 1, tq), F32), pltpu.VMEM((4, VROWS, tq), F32)],
        compiler_params=_cp(("arbitrary",) * 4, VMEM_LIMIT),
        name="diff_attention",
    )(dq, dk, dvT, dlam, gcol)


def _mlstm_kernel(qf_ref, kf_ref, vf_ref, gcf_ref, grf_ref, qb_ref, kb_ref, vb_ref, gcb_ref, grb_ref,
                  hf_ref, hb_ref, c_scr, m_scr):
    t = pl.program_id(1)

    @pl.when(t == 0)
    def _():
        c_scr[...] = jnp.zeros(c_scr.shape, F32)
        m_scr[...] = jnp.zeros(m_scr.shape, F32)

    L = TOK
    si = lax.broadcasted_iota(jnp.int32, (L, L), 0)
    li = lax.broadcasted_iota(jnp.int32, (L, L), 1)
    dirs = ((qf_ref, kf_ref, vf_ref, gcf_ref, grf_ref, hf_ref, si <= li, li <= si, L - 1),
            (qb_ref, kb_ref, vb_ref, gcb_ref, grb_ref, hb_ref, si >= li, li >= si, 0))
    ones = jnp.ones((16, L), F32)
    for d, (q_ref, k_ref, vt_ref, gc_ref, gr_ref, h_ref, seen, seen_t, last) in enumerate(dirs):
        gc = gc_ref[0]
        gr = gr_ref[0]
        bcols = _dot(jnp.where(seen_t, 1.0, 0.0), gc, HI)
        brows = _dot(gr, jnp.where(seen, 1.0, 0.0), HI)
        for hd in range(M_HEADS):
            idx = d * M_HEADS + hd
            ji = d * 8 + hd
            jf = d * 8 + 4 + hd
            sl = slice(M_PAD * hd, M_PAD * (hd + 1))
            q = q_ref[0, :, sl]
            k = k_ref[0, :, sl]
            vt = vt_ref[0, sl, :]
            b_row = brows[jf:jf + 1, :]
            cs = gc[:, ji:ji + 1] - bcols[:, jf:jf + 1]
            li_row = gr[ji:ji + 1, :]
            m_old = m_scr[idx][0:1, 0:1]
            c_old = c_scr[idx]

            dlog = jnp.where(seen, b_row + cs, NEG)
            inter = b_row + m_old
            m_t = jnp.maximum(inter, jnp.max(dlog, axis=0, keepdims=True))
            w_inter = jnp.exp(inter - m_t)
            st = _dot_nt(k, q) * jnp.exp(dlog - m_t)
            cq = _dot_nt(c_old.astype(BF16), q)
            num = w_inter * cq[:M_PAD] + _dot(vt, st.astype(BF16))
            den = w_inter * cq[M_PAD:M_PAD + 1] + jnp.sum(st, axis=0, keepdims=True)
            h_ref[0, sl, :] = num / jnp.maximum(jnp.abs(den), jnp.exp(-m_t))

            total = b_row[:, last:last + 1]
            wlog = total - b_row + li_row
            m_new = jnp.maximum(total + m_old, jnp.max(wlog, axis=1, keepdims=True))
            decay = jnp.exp(total + m_old - m_new)
            w = jnp.exp(wlog - m_new)
            vw = jnp.concatenate([vt.astype(F32) * w, ones * w], axis=0).astype(BF16)
            c_scr[idx] = decay * c_old + _dot(vw, k)
            m_scr[idx] = jnp.broadcast_to(m_new, (8, 128))


def _mlstm(mq, mk, mvT, gl, glT, *, n_lat):
    B, NT, _ = mq.shape
    nt = n_lat + 1
    fwd = lambda t: jnp.where(t == 0, n_lat, t - 1)
    bwd = lambda t: jnp.where(t == 0, n_lat, n_lat - t)
    tok = lambda w, f: pl.BlockSpec((1, TOK, w), lambda b, t: (b, f(t), 0))
    lanes = lambda r, f: pl.BlockSpec((1, r, TOK), lambda b, t: (b, 0, f(t)))
    ins, specs = [], []
    for f in (fwd, bwd):
        ins += [mq, mk, mvT, gl, glT]
        specs += [tok(MP_WIDTH, f)] * 2 + [lanes(MP_WIDTH, f), tok(128, f), lanes(N_GATES, f)]
    return pl.pallas_call(
        _mlstm_kernel,
        out_shape=[jax.ShapeDtypeStruct((B, MP_WIDTH, NT), F32)] * 2,
        grid=(B, nt),
        in_specs=specs,
        out_specs=[lanes(MP_WIDTH, fwd), lanes(MP_WIDTH, bwd)],
        scratch_shapes=[pltpu.VMEM((2 * M_HEADS, M_PAD + 16, M_PAD), F32),
                        pltpu.VMEM((2 * M_HEADS, 8, 128), F32)],
        compiler_params=_cp(("arbitrary", "arbitrary"), VMEM_LIMIT),
        name="mlstm",
    )(*ins)


def _outproj_kernel(x_ref, f_ref, dat_ref, hf_ref, hb_ref, mo_ref, ada_ref, mg_ref, wo_ref, wod_ref, g2_ref, wr_ref,
                    xo_ref, hl_ref, pt_ref, *, is_ctx):
    b = pl.program_id(0)
    mod = ada_ref[CTX_ROW if is_ctx else b]
    gt1, sh2, sc2 = mod[2:3], mod[3:4], mod[4:5]
    mg = mg_ref[...]
    for s in range(x_ref.shape[1] // TOK):
        tk = slice(TOK * s, TOK * (s + 1))
        hs = hf_ref[0, :, tk] + hb_ref[0, :, tk]
        og = mo_ref[0, :, tk].astype(F32)
        parts = [dat_ref[0, :, tk]]
        for hd in range(M_HEADS):
            sl = slice(M_PAD * hd, M_PAD * (hd + 1))
            hh = hs[sl]
            r = lax.rsqrt(jnp.sum(hh * hh, axis=0, keepdims=True) * (1.0 / M_DIM) + EPS)
            parts.append((((hh * r) * mg[sl]) * _sigmoid(og[sl])).astype(BF16))
        mix_t = jnp.concatenate(parts, axis=0)
        upd = _dot(f_ref[0, tk].astype(BF16), wo_ref[...]) + lax.dot_general(
            mix_t, wod_ref[...], (((0,), (0,)), ((), ())), preferred_element_type=F32)
        xn = x_ref[0, tk] + gt1 * upd
        xo_ref[0, tk] = xn
        r = lax.rsqrt(jnp.mean(xn * xn, axis=-1, keepdims=True) + EPS)
        h2 = (xn * r) * g2_ref[...] * (1.0 + sc2) + sh2
        hl_ref[0, tk] = h2.astype(BF16)
        lt = _dot_nt(wr_ref[...], h2, HI)
        ex = jnp.exp(lt - jnp.max(lt, axis=0, keepdims=True))
        pt_ref[0, :, tk] = ex / jnp.sum(ex, axis=0, keepdims=True)


def _outproj_kernel_aliased(x_ref, f_ref, dat_ref, hf_ref, hb_ref, mo_ref, ada_ref, mg_ref, wo_ref, wod_ref,
                            g2_ref, wr_ref, hlp_ref, xo_ref, hl_ref, pt_ref, *, is_ctx):
    del hlp_ref
    _outproj_kernel(x_ref, f_ref, dat_ref, hf_ref, hb_ref, mo_ref, ada_ref, mg_ref, wo_ref, wod_ref, g2_ref,
                    wr_ref, xo_ref, hl_ref, pt_ref, is_ctx=is_ctx)


def _outproj(xu, f, daT, hf, hb, mo, ada_l, mg, wo, wod, g2, wrT, hl_prev, *, t0, ntl, is_ctx):
    B, NT, _ = xu.shape
    n = ntl * TOK
    tile = 2 * TOK if n % (2 * TOK) == 0 else TOK
    o = t0 * TOK // tile
    tok = lambda w: pl.BlockSpec((1, tile, w), lambda b, t: (b, o + t, 0))
    trs = lambda r: pl.BlockSpec((1, r, tile), lambda b, t: (b, 0, o + t))
    loc = lambda w: pl.BlockSpec((1, tile, w), lambda b, t: (b, t, 0))
    full = lambda a: pl.BlockSpec(a.shape, lambda b, t: (0,) * a.ndim)
    return pl.pallas_call(
        functools.partial(_outproj_kernel_aliased, is_ctx=is_ctx),
        out_shape=[jax.ShapeDtypeStruct(xu.shape, F32), jax.ShapeDtypeStruct((B, NT, D), BF16),
                   jax.ShapeDtypeStruct((B, N_EXPERTS, n), F32)],
        grid=(B, n // tile),
        in_specs=[tok(D), loc(F_WIDTH), pl.BlockSpec((1, DA_WIDTH, tile), lambda b, t: (b, 0, t)),
                  trs(MP_WIDTH), trs(MP_WIDTH), trs(MP_WIDTH),
                  full(ada_l), full(mg), full(wo), full(wod), full(g2), full(wrT),
                  pl.BlockSpec(memory_space=pl.ANY)],
        out_specs=[tok(D), tok(D), pl.BlockSpec((1, N_EXPERTS, tile), lambda b, t: (b, 0, t))],
        input_output_aliases={0: 0, 12: 1},
        compiler_params=_cp(("arbitrary", "arbitrary"), VMEM_LIMIT),
        name="outproj_norm2_router",
    )(xu, f, daT, hf, hb, mo, ada_l, mg, wo, wod, g2, wrT, hl_prev)


def _select_kernel(p_ref, rank_ref, offs_ref, *, n, cap):
    p = p_ref[0]
    xi = pltpu.bitcast(p, jnp.int32)

    def body(i, lo):
        cand = lo | jnp.left_shift(jnp.int32(1), 30 - i)
        cnt = jnp.sum(jnp.where(xi >= cand, 1.0, 0.0), axis=1, keepdims=True)
        return jnp.where(cnt >= cap, cand, lo)

    thr = lax.fori_loop(0, 31, body, jnp.zeros((N_EXPERTS, 1), jnp.int32))
    nb = n // TOK
    rows = lax.broadcasted_iota(jnp.int32, (n, 128), 0)
    cols = lax.broadcasted_iota(jnp.int32, (n, 128), 1)
    blk_ind = jnp.where((rows // TOK) == cols, 1.0, 0.0).astype(BF16)
    u128 = jnp.where(lax.broadcasted_iota(jnp.int32, (128, 128), 0)
                     < lax.broadcasted_iota(jnp.int32, (128, 128), 1), 1.0, 0.0).astype(BF16)
    utok = jnp.where(lax.broadcasted_iota(jnp.int32, (TOK, TOK), 0)
                     < lax.broadcasted_iota(jnp.int32, (TOK, TOK), 1), 1.0, 0.0).astype(BF16)

    def prefix(mf):
        mb = mf.astype(BF16)
        counts = _dot(mb, blk_ind)
        offs = _dot(counts.astype(BF16), u128)
        pieces = [_dot(mb[:, TOK * j:TOK * (j + 1)], utok) + offs[:, j:j + 1] for j in range(nb)]
        return (jnp.concatenate(pieces, axis=1) if nb > 1 else pieces[0]), offs

    gt = xi > thr
    eq = xi == thr
    need = cap - jnp.sum(jnp.where(gt, 1.0, 0.0), axis=1, keepdims=True)
    rank_eq, _ = prefix(jnp.where(eq, 1.0, 0.0))
    sel = gt | (eq & (rank_eq < need))
    rank, offs = prefix(jnp.where(sel, 1.0, 0.0))
    rank_ref[0] = jnp.where(sel, rank, -1.0)
    offs_ref[0] = offs.astype(jnp.int32)


def _select(pt, *, cap):
    B, _, n = pt.shape
    return pl.pallas_call(
        functools.partial(_select_kernel, n=n, cap=cap),
        out_shape=[jax.ShapeDtypeStruct((B, N_EXPERTS, n), F32),
                   jax.ShapeDtypeStruct((B, N_EXPERTS, 128), jnp.int32)],
        grid=(B,),
        in_specs=[pl.BlockSpec((1, N_EXPERTS, n), lambda b: (b, 0, 0))],
        out_specs=[pl.BlockSpec((1, N_EXPERTS, n), lambda b: (b, 0, 0)),
                   pl.BlockSpec((1, N_EXPERTS, 128), lambda b: (b, 0, 0))],
        compiler_params=_cp(("arbitrary",), VMEM_LIMIT),
        name="expert_choice_select",
    )(pt)


def _gather_kernel(offs_ref, h_ref, rank_ref, prob_ref, o_ref, gate_ref, *, eg, per):
    b, g, tb = pl.program_id(0), pl.program_id(1), pl.program_id(2)

    @pl.when(tb == 0)
    def _():
        o_ref[...] = jnp.zeros(o_ref.shape, BF16)
        gate_ref[...] = jnp.zeros(gate_ref.shape, F32)

    h = h_ref[0]
    ntok = h.shape[0]
    cap_pad = o_ref.shape[2]
    win = min(2 * SLOT, cap_pad)

    def add_rows(i, r, p, base, width):
        slots = lax.broadcasted_iota(jnp.int32, (width, ntok), 0).astype(F32) + base.astype(F32)
        hit = r == slots
        rows = _dot(jnp.where(hit, 1.0, 0.0).astype(BF16), h).astype(BF16)
        o_ref[0, i, pl.ds(base, width), :] = o_ref[0, i, pl.ds(base, width), :] + rows
        gate_ref[0, i, pl.ds(base, width), :] = (gate_ref[0, i, pl.ds(base, width), :]
                                                 + jnp.sum(jnp.where(hit, p, 0.0), axis=1, keepdims=True))

    rs, ps, ends, his = [], [], [], []
    for i in range(eg):
        e = g * eg + i
        r = rank_ref[0, pl.ds(e, 1), :]
        p = prob_ref[0, pl.ds(e, 1), :]
        lo = offs_ref[b, e, tb * per]
        his.append(offs_ref[b, e, (tb + 1) * per])
        base = pl.multiple_of(jnp.minimum((lo // SLOT) * SLOT, cap_pad - win), SLOT)
        add_rows(i, r, p, base, win)
        rs.append(r)
        ps.append(p)
        ends.append(base + win)

    for i in range(eg):
        @pl.when(his[i] > ends[i])
        def _(i=i):
            def body(t, carry):
                add_rows(i, rs[i], ps[i], pl.multiple_of(t * SLOT, SLOT), SLOT)
                return carry

            lax.fori_loop(ends[i] // SLOT, (his[i] - 1) // SLOT + 1, body, 0)


def _gather(offs, hl, rank, pt, *, tb_tok, tb0, n, cap_pad, eg):
    B = hl.shape[0]
    per = tb_tok // TOK
    return pl.pallas_call(
        functools.partial(_gather_kernel, eg=eg, per=per),
        out_shape=[jax.ShapeDtypeStruct((B, N_EXPERTS, cap_pad, D), BF16),
                   jax.ShapeDtypeStruct((B, N_EXPERTS, cap_pad, 1), F32)],
        grid_spec=pltpu.PrefetchScalarGridSpec(
            num_scalar_prefetch=1,
            grid=(B, N_EXPERTS // eg, n // tb_tok),
            in_specs=[pl.BlockSpec((1, tb_tok, D), lambda b, g, t, o: (b, tb0 + t, 0)),
                      pl.BlockSpec((1, N_EXPERTS, tb_tok), lambda b, g, t, o: (b, 0, t)),
                      pl.BlockSpec((1, N_EXPERTS, tb_tok), lambda b, g, t, o: (b, 0, t))],
            out_specs=[pl.BlockSpec((1, eg, cap_pad, D), lambda b, g, t, o: (b, g, 0, 0)),
                       pl.BlockSpec((1, eg, cap_pad, 1), lambda b, g, t, o: (b, g, 0, 0))]),
        compiler_params=_cp(("arbitrary",) * 3, VMEM_LIMIT),
        name="expert_gather",
    )(offs, hl, rank, pt)


FFN_ROWS = 1024


def _ffn_kernel(x_ref, gate_ref, w1_ref, w3_ref, w2_ref, y_ref, acc_ref):
    f = pl.program_id(2)

    @pl.when(f == 0)
    def _():
        acc_ref[...] = jnp.zeros(acc_ref.shape, F32)

    w1 = w1_ref[0, 0].astype(BF16)
    w3 = w3_ref[0, 0].astype(BF16)
    w2 = w2_ref[0, 0].astype(BF16)
    mb, _, cap_pad, _ = x_ref.shape
    rows = min(FFN_ROWS, cap_pad)
    for i in range(mb):
        for r in range(0, cap_pad, rows):
            x = x_ref[i, 0, r:r + rows, :]
            hid = (_silu(_dot(x, w1)) * _dot(x, w3)).astype(BF16)
            acc_ref[i * cap_pad + r:i * cap_pad + r + rows, :] += _dot(hid, w2)

    @pl.when(f == pl.num_programs(2) - 1)
    def _():
        gate = gate_ref[...].reshape(-1, 1)
        y_ref[...] = (acc_ref[...] * gate).astype(BF16).reshape(y_ref.shape)


def _ffn(xs, gates, w1, w3, w2, *, layer, mb, tf):
    B, E, cap_pad, _ = xs.shape
    return pl.pallas_call(
        _ffn_kernel,
        out_shape=jax.ShapeDtypeStruct(xs.shape, BF16),
        grid=(E, B // mb, D_FF // tf),
        in_specs=[pl.BlockSpec((mb, 1, cap_pad, D), lambda e, m, f: (m, e, 0, 0)),
                  pl.BlockSpec((mb, 1, cap_pad, 1), lambda e, m, f: (m, e, 0, 0)),
                  pl.BlockSpec((1, 1, D, tf), lambda e, m, f: (layer, e, 0, f)),
                  pl.BlockSpec((1, 1, D, tf), lambda e, m, f: (layer, e, 0, f)),
                  pl.BlockSpec((1, 1, tf, D), lambda e, m, f: (layer, e, f, 0))],
        out_specs=pl.BlockSpec((mb, 1, cap_pad, D), lambda e, m, f: (m, e, 0, 0)),
        scratch_shapes=[pltpu.VMEM((mb * cap_pad, D), F32)],
        compiler_params=_cp(("arbitrary",) * 3, VMEM_LIMIT),
        name="expert_ffn",
    )(xs, gates, w1, w3, w2)


CCOL = 512


def _combine_kernel(offs_ref, x_ref, y_ref, rankc_ref, ada_ref, o_ref, tot_scr, *, per, is_ctx):
    b, tb = pl.program_id(0), pl.program_id(2)
    gt2 = ada_ref[CTX_ROW if is_ctx else b][5:6]
    rc_all = rankc_ref[0]
    cap_pad = y_ref.shape[2]
    win = min(2 * SLOT, cap_pad)
    slotw = lax.broadcasted_iota(jnp.int32, (1, win), 1).astype(F32)
    slot = lax.broadcasted_iota(jnp.int32, (1, SLOT), 1).astype(F32)

    bases, his = [], []
    total = jnp.zeros(tot_scr.shape, F32)
    for e in range(N_EXPERTS):
        lo = offs_ref[b, e, tb * per]
        his.append(offs_ref[b, e, (tb + 1) * per])
        base = pl.multiple_of(jnp.minimum((lo // SLOT) * SLOT, cap_pad - win), SLOT)
        bases.append(base)
        onehot = jnp.where(rc_all[:, e:e + 1] == slotw + base.astype(F32), 1.0, 0.0).astype(BF16)
        total = total + _dot(onehot, y_ref[0, e, pl.ds(base, win), :])
    tot_scr[...] = total

    for e in range(N_EXPERTS):
        end = bases[e] + win

        @pl.when(his[e] > end)
        def _(e=e, end=end):
            def body(t, carry):
                base = pl.multiple_of(t * SLOT, SLOT)
                onehot = jnp.where(rc_all[:, e:e + 1] == slot + base.astype(F32), 1.0, 0.0).astype(BF16)
                tot_scr[...] += _dot(onehot, y_ref[0, e, pl.ds(base, SLOT), :])
                return carry

            lax.fori_loop(end // SLOT, (his[e] - 1) // SLOT + 1, body, 0)

    o_ref[0] = x_ref[0] + gt2 * tot_scr[...]


def _combine(offs, xu, ys, rank_c, ada_l, *, tb_tok, tb0, n, is_ctx):
    B = xu.shape[0]
    cap_pad = ys.shape[2]
    per = tb_tok // TOK
    return pl.pallas_call(
        functools.partial(_combine_kernel, per=per, is_ctx=is_ctx),
        out_shape=jax.ShapeDtypeStruct(xu.shape, F32),
        grid_spec=pltpu.PrefetchScalarGridSpec(
            num_scalar_prefetch=1,
            grid=(B, D // CCOL, n // tb_tok),
            in_specs=[pl.BlockSpec((1, tb_tok, CCOL), lambda b, c, t, o: (b, tb0 + t, c)),
                      pl.BlockSpec((1, N_EXPERTS, cap_pad, CCOL), lambda b, c, t, o: (b, 0, 0, c),
                                   pipeline_mode=pl.Buffered(1)),
                      pl.BlockSpec((1, tb_tok, N_EXPERTS), lambda b, c, t, o: (b, t, 0)),
                      pl.BlockSpec((ADA_ROWS, ADA_CHUNKS, CCOL), lambda b, c, t, o: (0, 0, c))],
            out_specs=pl.BlockSpec((1, tb_tok, CCOL), lambda b, c, t, o: (b, tb0 + t, c)),
            scratch_shapes=[pltpu.VMEM((tb_tok, CCOL), F32)]),
        input_output_aliases={1: 0},
        compiler_params=_cp(("arbitrary",) * 3, VMEM_LIMIT),
        name="expert_combine",
    )(offs, xu, ys, rank_c, ada_l)


def _moe(xu, hl, pt, ada_l, w1, w3, w2, *, layer, row0, is_ctx):
    B, _, n = pt.shape
    cap = EC_CAPACITY * n // N_EXPERTS
    cap_pad = -(-cap // SLOT) * SLOT
    nb = n // TOK
    rank, offs = _select(pt, cap=cap)
    offs = offs[:, :, :nb + 1]
    gt = ct = min(n, MOE_TOK)
    xs, gates = _gather(offs, hl, rank, pt, tb_tok=gt, tb0=row0 // gt, n=n, cap_pad=cap_pad, eg=GATHER_EXPERTS)
    mb = 2 if (B % 2 == 0 and cap_pad >= 1024) else (B if cap_pad < 1024 else 1)
    ys = _ffn(xs, gates, w1, w3, w2, layer=layer, mb=mb, tf=FFN_TF)
    rank_c = jnp.swapaxes(rank, 1, 2)
    return _combine(offs, xu, ys, rank_c, ada_l, tb_tok=ct, tb0=row0 // ct, n=n, is_ctx=is_ctx)


def _final_kernel(x_ref, g_ref, o_ref):
    x = x_ref[0]
    r = lax.rsqrt(jnp.mean(x * x, axis=-1, keepdims=True) + EPS)
    o_ref[0] = (x * r) * g_ref[...]


def _final_norm(xu, g, *, n):
    B = xu.shape[0]
    tm = MOE_TOK
    return pl.pallas_call(
        _final_kernel,
        out_shape=jax.ShapeDtypeStruct((B, n, D), F32),
        grid=(B, n // tm),
        in_specs=[pl.BlockSpec((1, tm, D), lambda b, t: (b, t, 0)),
                  pl.BlockSpec((1, D), lambda b, t: (0, 0))],
        out_specs=pl.BlockSpec((1, tm, D), lambda b, t: (b, t, 0)),
        compiler_params=_cp(("arbitrary", "arbitrary")),
        name="final_norm",
    )(xu, g)


def _rope_tables(n, ctx):
    rows = n // GRID_W
    t_row = jnp.repeat(jnp.arange(rows), GRID_W)
    t_col = jnp.tile(jnp.arange(GRID_W), rows)
    nf = DA_DIM // 4
    inv = ROPE_THETA ** (-jnp.arange(nf, dtype=F32) / nf)
    ar = t_row[:, None].astype(F32) * inv
    ac = t_col[:, None].astype(F32) * inv
    ang = jnp.concatenate([ar, ar, ac, ac], axis=-1)
    sign = jnp.where((jnp.arange(DA_DIM) % 16) < 8, -1.0, 1.0).astype(F32)
    cos = jnp.concatenate([jnp.cos(ang), jnp.ones((ctx, DA_DIM), F32)], axis=0)
    sin = jnp.concatenate([jnp.sin(ang) * sign, jnp.zeros((ctx, DA_DIM), F32)], axis=0)
    return jnp.tile(cos, (1, 128 // DA_DIM)), jnp.tile(sin, (1, 128 // DA_DIM))


def _pad_heads_cols(w):
    lead = w.shape[:-1]
    w = w.reshape(lead + (M_HEADS, M_DIM))
    w = jnp.pad(w, [(0, 0)] * len(lead) + [(0, 0), (0, M_PAD - M_DIM)])
    return w.reshape(lead + (MP_WIDTH,))


def _kv_tile(nt):
    for parts in range(1, nt // LANES + 1):
        if nt % parts == 0 and (nt // parts) % LANES == 0 and nt // parts <= KV_TILE_MAX:
            return nt // parts
    raise ValueError(nt)


def kernel(x, c, ctx, c_ctx, ada_w, ada_b, norm1_g, norm2_g, w_in, four_w, m_conv_w, m_conv_b, m_gate_b,
           m_norm_g, d_lam, d_norm_g, w_out, router_w, exp_w1, exp_w3, exp_w2, final_g):
    B, N, _ = x.shape
    CTX = ctx.shape[1]
    depth = w_in.shape[0]
    assert CTX == TOK and N % (FFT_N1 * TOK) == 0 and N % Q_TILE == 0 and B <= CTX_ROW
    NT = N + CTX
    PAD = -NT % MOE_TOK
    n_lat = N // TOK
    n2 = N // FFT_N1

    xu = jnp.concatenate([x, ctx, jnp.zeros((B, PAD, D), F32)], axis=1)
    cvecs = jnp.zeros((ADA_ROWS, D), F32).at[:B].set(c).at[CTX_ROW].set(c_ctx)
    ada = _adaln(cvecs, ada_w, ada_b).reshape(depth, ADA_ROWS, ADA_CHUNKS, D)
    cos_t, sin_t = _rope_tables(N, CTX + PAD)
    tabs = _fourier_tables(N, CTX)
    tk = _kv_tile(NT)
    tq = Q_TILE

    hl = jnp.zeros((B, NT + PAD, D), BF16)
    for layer in range(depth):
        ctx_out = layer < depth - 1
        lam_init = 0.8 - 0.6 * math.exp(-0.3 * layer)
        w = w_in[layer]
        wm = jnp.concatenate([w[:, OFF_F:OFF_DQ], w[:, OFF_DQ:OFF_MO], w[:, OFF_DK:OFF_DV]], axis=1).astype(BF16)
        wvt = jnp.concatenate([w[:, OFF_DV:OFF_MV], _pad_heads_cols(w[:, OFF_MO:OFF_MQ]),
                               _pad_heads_cols(w[:, OFF_MV:OFF_G])], axis=1).T.astype(BF16)
        wgt = w[:, OFF_G:].T
        wc = jnp.concatenate([_pad_heads_cols(w[:, OFF_MQ:OFF_MK]), _pad_heads_cols(w[:, OFF_MK:OFF_DK])],
                             axis=1).astype(BF16)
        gb = m_gate_b[layer].reshape(N_GATES, 1)
        cw = jnp.concatenate([_pad_heads_cols(m_conv_w[layer][:, :M_WIDTH]),
                              _pad_heads_cols(m_conv_w[layer][:, M_WIDTH:])], axis=1)
        cb = jnp.concatenate([_pad_heads_cols(m_conv_b[layer][:M_WIDTH]),
                              _pad_heads_cols(m_conv_b[layer][M_WIDTH:])]).reshape(1, 2 * MP_WIDTH)
        ada_l = ada[layer]

        y4, dq, dk, dvT, mo, mq, mk, mv, gl, glT = _inproj(
            xu, ada_l, norm1_g[layer].reshape(1, D), wm, wc, wvt, wgt, gb, tabs["cs"], cos_t, sin_t, cw, cb,
            n_lat=n_lat, n2=n2)

        wblk = jnp.zeros((F_WIDTH, F_WIDTH), F32)
        for g in range(F_GROUPS):
            wblk = wblk.at[F_GDIM * g:F_GDIM * (g + 1), F_GDIM * g:F_GDIM * (g + 1)].set(four_w[layer, g])
        f_l, f_c = _fourier(y4, tabs, wblk.astype(BF16), n=N, ctx=CTX, with_ctx=ctx_out)

        dlam = d_lam[layer]
        g2 = d_norm_g[layer].reshape(DA_VDIM, 1)
        da_l = _attention(dq, dk, dvT, dlam, g2, lam_init=lam_init, tq=tq, q0=0, nq=N // tq,
                          tk=tk, k0=0, nk=NT // tk)

        hf, hb = _mlstm(mq, mk, mv, gl, glT, n_lat=n_lat)

        mg = _pad_heads_cols(m_norm_g[layer]).reshape(MP_WIDTH, 1)
        wol = w_out[layer]
        wo = wol[:F_WIDTH].astype(BF16)
        wod = jnp.concatenate([wol[F_WIDTH:F_WIDTH + DA_WIDTH],
                               jnp.pad(wol[F_WIDTH + DA_WIDTH:].reshape(M_HEADS, M_DIM, D),
                                       ((0, 0), (0, M_PAD - M_DIM), (0, 0))).reshape(MP_WIDTH, D)],
                              axis=0).astype(BF16)
        g2n = norm2_g[layer].reshape(1, D)
        wrT = router_w[layer].T
        xu, hl, pt_l = _outproj(xu, f_l, da_l, hf, hb, mo, ada_l, mg, wo, wod, g2n, wrT, hl,
                                t0=0, ntl=n_lat, is_ctx=False)
        if ctx_out:
            da_c = _attention(dq, dk, dvT, dlam, g2, lam_init=lam_init, tq=TOK, q0=n_lat, nq=1,
                              tk=TOK, k0=n_lat, nk=1)
            xu, hl, pt_c = _outproj(xu, f_c, da_c, hf, hb, mo, ada_l, mg, wo, wod, g2n, wrT, hl,
                                    t0=n_lat, ntl=1, is_ctx=True)

        xu = _moe(xu, hl, pt_l, ada_l, exp_w1, exp_w3, exp_w2, layer=layer, row0=0, is_ctx=False)
        if ctx_out:
            xu = _moe(xu, hl, pt_c, ada_l, exp_w1, exp_w3, exp_w2, layer=layer, row0=N, is_ctx=True)

    return _final_norm(xu, final_g.reshape(1, D), n=N)
```

```python
import functools
import math

import numpy as np
import jax
import jax.numpy as jnp
from jax import lax
from jax.experimental import pallas as pl
from jax.experimental.pallas import tpu as pltpu

F32 = jnp.float32
BF16 = jnp.bfloat16
HI = lax.Precision.HIGHEST

D = 1024
EPS = 1e-6
GRID_W = 64
ROPE_THETA = 10000.0
F_GROUPS, F_GDIM = 4, 64
F_WIDTH = F_GROUPS * F_GDIM
DA_HEADS, DA_DIM = 6, 32
DA_VDIM = 2 * DA_DIM
DA_WIDTH = DA_HEADS * DA_VDIM
M_HEADS, M_DIM = 4, 96
M_WIDTH = M_HEADS * M_DIM
M_PAD = 128
MP_WIDTH = M_HEADS * M_PAD
N_GATES = 4 * M_HEADS
N_EXPERTS = 16
EC_CAPACITY = 2
D_FF = 2 * D
ADA_CHUNKS = 6
ADA_ROWS = 8
CTX_ROW = 4

LANES = 128
MXU_DIM = 256
V7X_VMEM_BYTES = 64 * 1024 * 1024

TOK = MXU_DIM
FFT_N1 = 16
SLOT = LANES
Q_TILE = 2048
KV_TILE_MAX = 1408
MOE_TOK = 512
FFN_TF = 512
GATHER_EXPERTS = 4
NEG = -1e30

OFF_F = 0
OFF_DQ = OFF_F + F_WIDTH
OFF_MO = OFF_DQ + 2 * DA_HEADS * DA_DIM
OFF_MQ = OFF_MO + M_WIDTH
OFF_MK = OFF_MQ + M_WIDTH
OFF_DK = OFF_MK + M_WIDTH
OFF_DV = OFF_DK + 2 * DA_HEADS * DA_DIM
OFF_MV = OFF_DV + DA_HEADS * DA_VDIM
OFF_G = OFF_MV + M_WIDTH

VMEM_LIMIT = V7X_VMEM_BYTES * 7 // 8


def _cp(sem, vmem=None):
    return pltpu.CompilerParams(dimension_semantics=sem, vmem_limit_bytes=vmem)


def _sigmoid(x):
    return 1.0 / (1.0 + jnp.exp(-x))


def _silu(x):
    return x * _sigmoid(x)


def _dot(a, b, precision=None):
    return jnp.dot(a, b, preferred_element_type=F32, precision=precision)


def _split(a):
    hi = a.astype(BF16)
    return hi, (a - hi.astype(F32)).astype(BF16)


def _dot3(a, b):
    a_hi, a_lo = a if isinstance(a, tuple) else _split(a)
    b_hi, b_lo = b if isinstance(b, tuple) else _split(b)
    return _dot(a_hi, b_hi) + _dot(a_hi, b_lo) + _dot(a_lo, b_hi)


def _split3(a):
    hi = a.astype(BF16)
    r = a - hi.astype(F32)
    mid = r.astype(BF16)
    return hi, mid, (r - mid.astype(F32)).astype(BF16)


def _dot_nt(a, b, precision=None):
    return lax.dot_general(a, b, (((1,), (1,)), ((), ())), preferred_element_type=F32,
                           precision=precision)


def _ada_kernel(c_ref, w_ref, b_ref, o_ref):
    c = c_ref[...]
    o_ref[0] = _dot(_silu(c), w_ref[0], HI) + b_ref[0]


def _adaln(cvecs, ada_w, ada_b):
    depth = ada_w.shape[0]
    tn = 1536
    return pl.pallas_call(
        _ada_kernel,
        out_shape=jax.ShapeDtypeStruct((depth, ADA_ROWS, ADA_CHUNKS * D), F32),
        grid=(depth, ADA_CHUNKS * D // tn),
        in_specs=[pl.BlockSpec((ADA_ROWS, D), lambda l, j: (0, 0)),
                  pl.BlockSpec((1, D, tn), lambda l, j: (l, 0, j)),
                  pl.BlockSpec((1, 1, tn), lambda l, j: (l, 0, j))],
        out_specs=pl.BlockSpec((1, 8, tn), lambda l, j: (l, 0, j)),
        compiler_params=_cp(("arbitrary", "arbitrary")),
        name="adaln",
    )(cvecs, ada_w, ada_b.reshape(depth, 1, ADA_CHUNKS * D))


def _inproj_kernel(x_ref, xp_ref, xn_ref, ada_ref, g_ref, wm_ref, wc_ref, wvt_ref, wg_ref, gb_ref, cs_ref,
                   cos_ref, sin_ref, cw_ref, cb_ref,
                   y_ref, dq_ref, dk_ref, dvt_ref, mo_ref, mq_ref, mk_ref, mv_ref, gl_ref, glt_ref, *, n_lat):
    b = pl.program_id(0)
    t = pl.program_id(1)
    n_tiles = pl.num_programs(1)
    is_ctx = t >= n_lat
    row = jnp.where(is_ctx, CTX_ROW, b)
    mod = ada_ref[row]
    sh, sc = mod[0:1], mod[1:2]

    xa = jnp.concatenate([xp_ref[0], x_ref[0], xn_ref[0]], axis=0)
    r = lax.rsqrt(jnp.mean(xa * xa, axis=-1, keepdims=True) + EPS)
    ha = (xa * r) * g_ref[...] * (1.0 + sc) + sh
    h = ha[8:8 + TOK]
    hb = h.astype(BF16)

    pm = _dot(hb, wm_ref[...])
    o = 0
    pf = pm[:, o:o + F_WIDTH]; o += F_WIDTH
    q = pm[:, o:o + DA_WIDTH]; o += DA_WIDTH
    k = pm[:, o:o + DA_WIDTH]; o += DA_WIDTH
    pt = _dot_nt(wvt_ref[...], hb)
    dvt_ref[0] = pt[:DA_WIDTH].astype(BF16)
    mo_ref[0] = pt[DA_WIDTH:DA_WIDTH + MP_WIDTH].astype(BF16)
    mv_ref[0] = pt[DA_WIDTH + MP_WIDTH:].astype(BF16)

    y_ref[0, 0] = _dot3(pf, cs_ref[...])

    cos = cos_ref[...]
    sin = sin_ref[...]
    lane = lax.broadcasted_iota(jnp.int32, (1, 128), 1)
    low = (lane % 16) < 8

    def rope(z):
        parts = []
        for c in range(DA_WIDTH // 128):
            zc = z[:, 128 * c:128 * (c + 1)]
            rot = jnp.where(low, pltpu.roll(zc, 120, 1), pltpu.roll(zc, 8, 1))
            parts.append(zc * cos + rot * sin)
        return jnp.concatenate(parts, axis=1)

    dq_ref[0] = (rope(q) * (DA_DIM ** -0.5 * math.log2(math.e))).astype(BF16)
    dk_ref[0] = rope(k).astype(BF16)

    gpre = _dot3(h, wg_ref[...]) + gb_ref[...]
    is_forget = (lax.broadcasted_iota(jnp.int32, (1, LANES), 1) % 8) >= 4
    logsig = jnp.minimum(gpre, 0.0) - jnp.log(1.0 + jnp.exp(-jnp.abs(gpre)))
    gl = jnp.where(is_forget, logsig, gpre)
    gl_ref[0] = gl
    glt_ref[0] = gl.T[:N_GATES]

    pc = _dot(ha.astype(BF16), wc_ref[...])
    first = (t == 0) | (t == n_lat)
    last = (t == n_lat - 1) | (t == n_tiles - 1)
    ridx = lax.broadcasted_iota(jnp.int32, (TOK + 16, 1), 0)
    pc = jnp.where(((ridx < 8) & first) | ((ridx >= TOK + 8) & last), 0.0, pc)
    cw = cw_ref[...]
    conv = cb_ref[...] + pc[7:7 + TOK] * cw[0:1] + pc[8:8 + TOK] * cw[1:2] + pc[9:9 + TOK] * cw[2:3]
    act = _silu(conv)
    mq_ref[0] = act[:, :MP_WIDTH].astype(BF16)
    mk_ref[0] = (act[:, MP_WIDTH:] * (M_DIM ** -0.5)).astype(BF16)


def _inproj(xu, ada_l, g1, wm, wc, wvt, wg, gb, cs, cos_t, sin_t, cw, cb, *, n_lat, n2):
    B, NT, _ = xu.shape
    nt = n_lat + 1
    rper = n2 // TOK
    tok3 = lambda w: pl.BlockSpec((1, TOK, w), lambda b, t: (b, t, 0))
    full = lambda a: pl.BlockSpec(a.shape, lambda b, t: (0,) * a.ndim)
    nb8 = NT // 8
    outs = [jax.ShapeDtypeStruct((B, 2 * FFT_N1, n2, 2 * F_WIDTH), F32)]
    nq = -(-NT // Q_TILE) * Q_TILE
    outs += [jax.ShapeDtypeStruct((B, nq, DA_WIDTH), BF16), jax.ShapeDtypeStruct((B, nt * TOK, DA_WIDTH), BF16)]
    outs += [jax.ShapeDtypeStruct((B, DA_WIDTH, nt * TOK), BF16)]
    trs = lambda r: pl.BlockSpec((1, r, TOK), lambda b, t: (b, 0, t))
    trp = jax.ShapeDtypeStruct((B, MP_WIDTH, NT), BF16)
    outs += [trp, jax.ShapeDtypeStruct((B, NT, MP_WIDTH), BF16), jax.ShapeDtypeStruct((B, NT, MP_WIDTH), BF16), trp]
    outs += [jax.ShapeDtypeStruct((B, NT, 128), F32), jax.ShapeDtypeStruct((B, N_GATES, nt * TOK), F32)]
    out_specs = [pl.BlockSpec((1, 1, TOK, 2 * F_WIDTH), lambda b, t: (b, t // rper, t % rper, 0))]
    out_specs += [tok3(DA_WIDTH)] * 2 + [trs(DA_WIDTH)]
    out_specs += [trs(MP_WIDTH), tok3(MP_WIDTH), tok3(MP_WIDTH), trs(MP_WIDTH)]
    out_specs += [tok3(128), pl.BlockSpec((1, N_GATES, TOK), lambda b, t: (b, 0, t))]
    return pl.pallas_call(
        functools.partial(_inproj_kernel, n_lat=n_lat),
        out_shape=outs,
        grid=(B, nt),
        in_specs=[tok3(D),
                  pl.BlockSpec((1, 8, D), lambda b, t: (b, jnp.maximum(t * (TOK // 8) - 1, 0), 0)),
                  pl.BlockSpec((1, 8, D), lambda b, t: (b, jnp.minimum((t + 1) * (TOK // 8), nb8 - 1), 0)),
                  full(ada_l), full(g1), full(wm), full(wc), full(wvt), full(wg), full(gb), full(cs),
                  pl.BlockSpec((TOK, LANES), lambda b, t: (t, 0)),
                  pl.BlockSpec((TOK, LANES), lambda b, t: (t, 0)),
                  full(cw), full(cb)],
        out_specs=out_specs,
        compiler_params=_cp(("arbitrary", "arbitrary"), VMEM_LIMIT),
        name="norm1_inproj",
    )(xu, xu, xu, ada_l, g1, wm, wc, wvt, wg, gb, cs, cos_t, sin_t, cw, cb)


def _fft1_kernel(y_ref, kc_ref, ks_ref, tc_ref, ts_ref, o_ref, *, groups):
    kc = _split(kc_ref[...])
    ks = _split(ks_ref[...])
    for g in range(groups):
        blk = _split(y_ref[0, :, 8 * g:8 * (g + 1), :].reshape(FFT_N1 * 8, 2 * F_WIDTH))
        p = _dot3(kc, blk)
        q = _dot3(ks, blk)
        ar = p[:, :F_WIDTH] - q[:, F_WIDTH:]
        ai = -p[:, F_WIDTH:] - q[:, :F_WIDTH]
        tc = tc_ref[128 * g:128 * (g + 1), :]
        ts = ts_ref[128 * g:128 * (g + 1), :]
        tc = jnp.concatenate([tc, tc], axis=1)
        ts = jnp.concatenate([ts, ts], axis=1)
        br = ar * tc + ai * ts
        bi = ai * tc - ar * ts
        o_ref[0, :, 8 * g:8 * (g + 1), :] = jnp.concatenate([br, bi], axis=1).reshape(FFT_N1, 8, 2 * F_WIDTH)


def _fft2_kernel(b_ref, c2_ref, s2_ref, wb_ref, perm_ref, o_ref, r_scr, *, n2):
    c2 = _split(c2_ref[...])
    s2 = _split(s2_ref[...])
    for i in range(8):
        blk = b_ref[0, i]
        xr = _dot3(c2, blk[:, :F_WIDTH]) + _dot3(s2, blk[:, F_WIDTH:])
        r_scr[i] = _dot(xr.astype(BF16), wb_ref[...]).astype(BF16)
    for t in range(n2 // 32):
        rows = jnp.concatenate([r_scr[i, 32 * t:32 * (t + 1), :] for i in range(8)], axis=0)
        o_ref[0, 32 * t:32 * (t + 1), :, :] = _dot(perm_ref[...], rows).reshape(32, 8, F_WIDTH)


def _fftc_kernel(y_ref, c_ref, s_ref, wb_ref, o_ref):
    y = y_ref[0, 0]
    z = _dot3(c_ref[...], y[:, :F_WIDTH]) - _dot3(s_ref[...], y[:, F_WIDTH:])
    o_ref[0] = _dot(z.astype(BF16), wb_ref[...])


def _fourier_tables(n, ctx):
    n1, n2 = FFT_N1, n // FFT_N1
    a = np.arange(n1)
    ang1 = 2 * np.pi * np.outer(a, a) / n1
    eye8 = np.eye(8)
    kc = np.kron(np.cos(ang1), eye8)
    ks = np.kron(np.sin(ang1), eye8)
    n2i = np.arange(n2).reshape(n2 // 8, 1, 8)
    k1 = np.arange(n1).reshape(1, n1, 1)
    angt = (2 * np.pi * n2i * k1 / n).reshape(-1, 1)
    tc = np.broadcast_to(np.cos(angt), (n2 // 8 * 128, 128))
    ts = np.broadcast_to(np.sin(angt), (n2 // 8 * 128, 128))
    b = np.arange(n2)
    ang2 = 2 * np.pi * np.outer(b, b) / n2
    c2 = np.cos(ang2) / math.sqrt(n)
    s2 = np.sin(ang2) / math.sqrt(n)
    perm = np.zeros((256, 256))
    for kk in range(8):
        for j in range(32):
            perm[j * 8 + kk, kk * 32 + j] = 1.0
    cc = np.arange(ctx)
    angc = 2 * np.pi * np.outer(cc, cc) / ctx
    cctx = np.cos(angc) / math.sqrt(ctx)
    sctx = np.sin(angc) / math.sqrt(ctx)
    ch = np.arange(F_GDIM)
    angch = 2 * np.pi * np.outer(ch, ch) / F_GDIM
    cs = np.concatenate([np.kron(np.eye(F_GROUPS), np.cos(angch)),
                         np.kron(np.eye(F_GROUPS), np.sin(angch))], axis=1) / math.sqrt(F_GDIM)
    f = lambda z: jnp.asarray(np.ascontiguousarray(z), dtype=F32)
    return dict(kc=f(kc), ks=f(ks), tc=f(tc), ts=f(ts), c2=f(c2), s2=f(s2), perm=f(perm).astype(BF16),
                cctx=f(cctx), sctx=f(sctx), cs=f(cs))


def _fourier(y4, tabs, wblk, *, n, ctx, with_ctx):
    B = y4.shape[0]
    n2 = n // FFT_N1
    groups = 4
    full = lambda a, nd: pl.BlockSpec(a.shape, lambda *i: (0,) * a.ndim)
    b4 = pl.pallas_call(
        functools.partial(_fft1_kernel, groups=groups),
        out_shape=jax.ShapeDtypeStruct((B, FFT_N1, n2, 2 * F_WIDTH), F32),
        grid=(B, n2 // (8 * groups)),
        in_specs=[pl.BlockSpec((1, FFT_N1, 8 * groups, 2 * F_WIDTH), lambda b, j: (b, 0, j, 0)),
                  full(tabs["kc"], 2), full(tabs["ks"], 2),
                  pl.BlockSpec((128 * groups, 128), lambda b, j: (j, 0)),
                  pl.BlockSpec((128 * groups, 128), lambda b, j: (j, 0))],
        out_specs=pl.BlockSpec((1, FFT_N1, 8 * groups, 2 * F_WIDTH), lambda b, j: (b, 0, j, 0)),
        compiler_params=_cp(("arbitrary", "arbitrary")),
        name="fourier_stage1",
    )(y4, tabs["kc"], tabs["ks"], tabs["tc"], tabs["ts"])
    f4 = pl.pallas_call(
        functools.partial(_fft2_kernel, n2=n2),
        out_shape=jax.ShapeDtypeStruct((B, n2, 16, F_WIDTH), F32),
        grid=(B, FFT_N1 // 8),
        in_specs=[pl.BlockSpec((1, 8, n2, 2 * F_WIDTH), lambda b, j: (b, j, 0, 0)),
                  full(tabs["c2"], 2), full(tabs["s2"], 2), full(wblk, 2), full(tabs["perm"], 2)],
        out_specs=pl.BlockSpec((1, n2, 8, F_WIDTH), lambda b, j: (b, 0, j, 0)),
        scratch_shapes=[pltpu.VMEM((8, n2, F_WIDTH), BF16)],
        compiler_params=_cp(("arbitrary", "arbitrary"), VMEM_LIMIT),
        name="fourier_stage2",
    )(b4, tabs["c2"], tabs["s2"], wblk, tabs["perm"])
    f_ctx = None
    if with_ctx:
        f_ctx = pl.pallas_call(
            _fftc_kernel,
            out_shape=jax.ShapeDtypeStruct((B, ctx, F_WIDTH), F32),
            grid=(B,),
            in_specs=[pl.BlockSpec((1, 1, TOK, 2 * F_WIDTH), lambda b: (b, FFT_N1, 0, 0)),
                      full(tabs["cctx"], 1), full(tabs["sctx"], 1), full(wblk, 1)],
            out_specs=pl.BlockSpec((1, ctx, F_WIDTH), lambda b: (b, 0, 0)),
            compiler_params=_cp(("arbitrary",)),
            name="fourier_ctx",
        )(y4, tabs["cctx"], tabs["sctx"], wblk)
    return f4.reshape(B, n, F_WIDTH), f_ctx


VROWS = DA_VDIM + 16


def _attn_kernel(q_ref, k_ref, vt_ref, dl_ref, g_ref, o_ref, m_scr, acc_scr, *, lam_init):
    kt = pl.program_id(3)
    nk = pl.num_programs(3)

    @pl.when(kt == 0)
    def _():
        m_scr[...] = jnp.full(m_scr.shape, NEG, F32)
        acc_scr[...] = jnp.zeros(acc_scr.shape, F32)

    q = q_ref[0]
    k = k_ref[0]
    vt = vt_ref[0]
    ones = jnp.ones((16, vt.shape[1]), BF16)
    lhs = [jnp.concatenate([vt[DA_VDIM * h:DA_VDIM * (h + 1)], ones], axis=0) for h in range(2)]
    lane = lax.broadcasted_iota(jnp.int32, (1, LANES), 1)
    zero = jnp.zeros((), BF16)

    def scores(j):
        return _dot_nt(k, jnp.where((lane // DA_DIM) == j, q, zero))

    st_next = scores(0)
    for j in range(4):
        st = st_next
        if j < 3:
            st_next = scores(j + 1)
        m_old = m_scr[j]
        m_new = jnp.maximum(m_old, jnp.max(st, axis=0, keepdims=True))
        alpha = jnp.exp2(m_old - m_new)
        pt = jnp.exp2(st - m_new).astype(BF16)
        acc_scr[j] = alpha * acc_scr[j] + _dot(lhs[j // 2], pt)
        m_scr[j] = m_new

    @pl.when(kt == nk - 1)
    def _():
        dl = dl_ref[...]
        lam = (jnp.exp(jnp.sum(dl[0:1] * dl[1:2], keepdims=True))
               - jnp.exp(jnp.sum(dl[2:3] * dl[3:4], keepdims=True)) + lam_init)
        outs = []
        for h in range(2):
            a0 = acc_scr[2 * h]
            a1 = acc_scr[2 * h + 1]
            o = (a0[:DA_VDIM] / a0[DA_VDIM:DA_VDIM + 1]
                 - lam * (a1[:DA_VDIM] / a1[DA_VDIM:DA_VDIM + 1]))
            r = lax.rsqrt(jnp.mean(o * o, axis=0, keepdims=True) + EPS)
            outs.append(((o * r) * g_ref[...]) * (1.0 - lam_init))
        o_ref[0] = jnp.concatenate(outs, axis=0).astype(BF16)


def _attention(dq, dk, dvT, dlam, gcol, *, lam_init, tq, q0, nq, tk, k0, nk):
    B = dq.shape[0]
    return pl.pallas_call(
        functools.partial(_attn_kernel, lam_init=lam_init),
        out_shape=jax.ShapeDtypeStruct((B, DA_WIDTH, nq * tq), BF16),
        grid=(B, DA_WIDTH // 128, nq, nk),
        in_specs=[pl.BlockSpec((1, tq, 128), lambda b, p, i, j: (b, q0 + i, p)),
                  pl.BlockSpec((1, tk, 128), lambda b, p, i, j: (b, k0 + j, p)),
                  pl.BlockSpec((1, 128, tk), lambda b, p, i, j: (b, p, k0 + j)),
                  pl.BlockSpec(dlam.shape, lambda b, p, i, j: (0, 0)),
                  pl.BlockSpec(gcol.shape, lambda b, p, i, j: (0, 0))],
        out_specs=pl.BlockSpec((1, 128, tq), lambda b, p, i, j: (b, p, i)),
        scratch_shapes=[pltpu.VMEM((4, 1, tq), F32), pltpu.VMEM((4, VROWS, tq), F32)],
        compiler_params=_cp(("arbitrary",) * 4, VMEM_LIMIT),
        name="diff_attention",
    )(dq, dk, dvT, dlam, gcol)


def _mlstm_kernel(qf_ref, kf_ref, vf_ref, gcf_ref, grf_ref, qb_ref, kb_ref, vb_ref, gcb_ref, grb_ref,
                  hf_ref, hb_ref, c_scr, m_scr):
    t = pl.program_id(1)

    @pl.when(t == 0)
    def _():
        c_scr[...] = jnp.zeros(c_scr.shape, F32)
        m_scr[...] = jnp.zeros(m_scr.shape, F32)

    L = TOK
    si = lax.broadcasted_iota(jnp.int32, (L, L), 0)
    li = lax.broadcasted_iota(jnp.int32, (L, L), 1)
    dirs = ((qf_ref, kf_ref, vf_ref, gcf_ref, grf_ref, hf_ref, si <= li, li <= si, L - 1),
            (qb_ref, kb_ref, vb_ref, gcb_ref, grb_ref, hb_ref, si >= li, li >= si, 0))
    ones = jnp.ones((16, L), F32)
    for d, (q_ref, k_ref, vt_ref, gc_ref, gr_ref, h_ref, seen, seen_t, last) in enumerate(dirs):
        gc = gc_ref[0]
        gr = gr_ref[0]
        seen_b = jnp.where(seen, 1.0, 0.0).astype(BF16)
        seen_tb = jnp.where(seen_t, 1.0, 0.0).astype(BF16)
        bcols = sum(_dot(seen_tb, piece) for piece in _split3(gc))
        brows = sum(_dot(piece, seen_b) for piece in _split3(gr))
        for hd in range(M_HEADS):
            idx = d * M_HEADS + hd
            ji = d * 8 + hd
            jf = d * 8 + 4 + hd
            sl = slice(M_PAD * hd, M_PAD * (hd + 1))
            q = q_ref[0, :, sl]
            k = k_ref[0, :, sl]
            vt = vt_ref[0, sl, :]
            b_row = brows[jf:jf + 1, :]
            cs = gc[:, ji:ji + 1] - bcols[:, jf:jf + 1]
            li_row = gr[ji:ji + 1, :]
            m_old = m_scr[idx][0:1, 0:1]
            c_old = c_scr[idx]

            dlog = jnp.where(seen, b_row + cs, NEG)
            inter = b_row + m_old
            m_t = jnp.maximum(inter, jnp.max(dlog, axis=0, keepdims=True))
            w_inter = jnp.exp(inter - m_t)
            st = _dot_nt(k, q) * jnp.exp(dlog - m_t)
            cq = _dot_nt(c_old.astype(BF16), q)
            num = w_inter * cq[:M_PAD] + _dot(vt, st.astype(BF16))
            den = w_inter * cq[M_PAD:M_PAD + 1] + jnp.sum(st, axis=0, keepdims=True)
            h_ref[0, sl, :] = num / jnp.maximum(jnp.abs(den), jnp.exp(-m_t))

            total = b_row[:, last:last + 1]
            wlog = total - b_row + li_row
            m_new = jnp.maximum(total + m_old, jnp.max(wlog, axis=1, keepdims=True))
            decay = jnp.exp(total + m_old - m_new)
            w = jnp.exp(wlog - m_new)
            vw = jnp.concatenate([vt.astype(F32) * w, ones * w], axis=0).astype(BF16)
            c_scr[idx] = decay * c_old + _dot(vw, k)
            m_scr[idx] = jnp.broadcast_to(m_new, (8, 128))


def _mlstm(mq, mk, mvT, gl, glT, *, n_lat):
    B, NT, _ = mq.shape
    nt = n_lat + 1
    fwd = lambda t: jnp.where(t == 0, n_lat, t - 1)
    bwd = lambda t: jnp.where(t == 0, n_lat, n_lat - t)
    tok = lambda w, f: pl.BlockSpec((1, TOK, w), lambda b, t: (b, f(t), 0))
    lanes = lambda r, f: pl.BlockSpec((1, r, TOK), lambda b, t: (b, 0, f(t)))
    ins, specs = [], []
    for f in (fwd, bwd):
        ins += [mq, mk, mvT, gl, glT]
        specs += [tok(MP_WIDTH, f)] * 2 + [lanes(MP_WIDTH, f), tok(128, f), lanes(N_GATES, f)]
    return pl.pallas_call(
        _mlstm_kernel,
        out_shape=[jax.ShapeDtypeStruct((B, MP_WIDTH, NT), F32)] * 2,
        grid=(B, nt),
        in_specs=specs,
        out_specs=[lanes(MP_WIDTH, fwd), lanes(MP_WIDTH, bwd)],
        scratch_shapes=[pltpu.VMEM((2 * M_HEADS, M_PAD + 16, M_PAD), F32),
                        pltpu.VMEM((2 * M_HEADS, 8, 128), F32)],
        compiler_params=_cp(("arbitrary", "arbitrary"), VMEM_LIMIT),
        name="mlstm",
    )(*ins)


def _outproj_kernel(x_ref, f_ref, dat_ref, hf_ref, hb_ref, mo_ref, ada_ref, mg_ref, wo_ref, wod_ref, g2_ref, wr_ref,
                    xo_ref, hl_ref, pt_ref, *, is_ctx):
    b = pl.program_id(0)
    mod = ada_ref[CTX_ROW if is_ctx else b]
    gt1, sh2, sc2 = mod[2:3], mod[3:4], mod[4:5]
    mg = mg_ref[...]
    for s in range(x_ref.shape[1] // TOK):
        tk = slice(TOK * s, TOK * (s + 1))
        hs = hf_ref[0, :, tk] + hb_ref[0, :, tk]
        og = mo_ref[0, :, tk].astype(F32)
        parts = [dat_ref[0, :, tk]]
        for hd in range(M_HEADS):
            sl = slice(M_PAD * hd, M_PAD * (hd + 1))
            hh = hs[sl]
            r = lax.rsqrt(jnp.sum(hh * hh, axis=0, keepdims=True) * (1.0 / M_DIM) + EPS)
            parts.append((((hh * r) * mg[sl]) * _sigmoid(og[sl])).astype(BF16))
        mix_t = jnp.concatenate(parts, axis=0)
        upd = _dot(f_ref[0, tk].astype(BF16), wo_ref[...]) + lax.dot_general(
            mix_t, wod_ref[...], (((0,), (0,)), ((), ())), preferred_element_type=F32)
        xn = x_ref[0, tk] + gt1 * upd
        xo_ref[0, tk] = xn
        r = lax.rsqrt(jnp.mean(xn * xn, axis=-1, keepdims=True) + EPS)
        h2 = (xn * r) * g2_ref[...] * (1.0 + sc2) + sh2
        hl_ref[0, tk] = h2.astype(BF16)
        lt = _dot3(h2, wr_ref[...]).T[:N_EXPERTS]
        ex = jnp.exp(lt - jnp.max(lt, axis=0, keepdims=True))
        pt_ref[0, :, tk] = ex / jnp.sum(ex, axis=0, keepdims=True)


def _outproj_kernel_aliased(x_ref, f_ref, dat_ref, hf_ref, hb_ref, mo_ref, ada_ref, mg_ref, wo_ref, wod_ref,
                            g2_ref, wr_ref, hlp_ref, xo_ref, hl_ref, pt_ref, *, is_ctx):
    del hlp_ref
    _outproj_kernel(x_ref, f_ref, dat_ref, hf_ref, hb_ref, mo_ref, ada_ref, mg_ref, wo_ref, wod_ref, g2_ref,
                    wr_ref, xo_ref, hl_ref, pt_ref, is_ctx=is_ctx)


def _outproj(xu, f, daT, hf, hb, mo, ada_l, mg, wo, wod, g2, wrp, hl_prev, *, t0, ntl, is_ctx):
    B, NT, _ = xu.shape
    n = ntl * TOK
    tile = 2 * TOK if n % (2 * TOK) == 0 else TOK
    o = t0 * TOK // tile
    tok = lambda w: pl.BlockSpec((1, tile, w), lambda b, t: (b, o + t, 0))
    trs = lambda r: pl.BlockSpec((1, r, tile), lambda b, t: (b, 0, o + t))
    loc = lambda w: pl.BlockSpec((1, tile, w), lambda b, t: (b, t, 0))
    full = lambda a: pl.BlockSpec(a.shape, lambda b, t: (0,) * a.ndim)
    return pl.pallas_call(
        functools.partial(_outproj_kernel_aliased, is_ctx=is_ctx),
        out_shape=[jax.ShapeDtypeStruct(xu.shape, F32), jax.ShapeDtypeStruct((B, NT, D), BF16),
                   jax.ShapeDtypeStruct((B, N_EXPERTS, n), F32)],
        grid=(B, n // tile),
        in_specs=[tok(D), loc(F_WIDTH), pl.BlockSpec((1, DA_WIDTH, tile), lambda b, t: (b, 0, t)),
                  trs(MP_WIDTH), trs(MP_WIDTH), trs(MP_WIDTH),
                  full(ada_l), full(mg), full(wo), full(wod), full(g2), full(wrp),
                  pl.BlockSpec(memory_space=pl.ANY)],
        out_specs=[tok(D), tok(D), pl.BlockSpec((1, N_EXPERTS, tile), lambda b, t: (b, 0, t))],
        input_output_aliases={0: 0, 12: 1},
        compiler_params=_cp(("arbitrary", "arbitrary"), VMEM_LIMIT),
        name="outproj_norm2_router",
    )(xu, f, daT, hf, hb, mo, ada_l, mg, wo, wod, g2, wrp, hl_prev)


def _select_kernel(p_ref, rank_ref, offs_ref, *, n, cap):
    p = p_ref[0]
    xi = pltpu.bitcast(p, jnp.int32)

    def body(i, lo):
        cand = lo | jnp.left_shift(jnp.int32(1), 30 - i)
        cnt = jnp.sum(jnp.where(xi >= cand, 1.0, 0.0), axis=1, keepdims=True)
        return jnp.where(cnt >= cap, cand, lo)

    thr = lax.fori_loop(0, 31, body, jnp.zeros((N_EXPERTS, 1), jnp.int32))
    nb = n // TOK
    rows = lax.broadcasted_iota(jnp.int32, (n, 128), 0)
    cols = lax.broadcasted_iota(jnp.int32, (n, 128), 1)
    blk_ind = jnp.where((rows // TOK) == cols, 1.0, 0.0).astype(BF16)
    u128 = jnp.where(lax.broadcasted_iota(jnp.int32, (128, 128), 0)
                     < lax.broadcasted_iota(jnp.int32, (128, 128), 1), 1.0, 0.0).astype(BF16)
    utok = jnp.where(lax.broadcasted_iota(jnp.int32, (TOK, TOK), 0)
                     < lax.broadcasted_iota(jnp.int32, (TOK, TOK), 1), 1.0, 0.0).astype(BF16)

    def prefix(mf):
        mb = mf.astype(BF16)
        counts = _dot(mb, blk_ind)
        offs = _dot(counts.astype(BF16), u128)
        pieces = [_dot(mb[:, TOK * j:TOK * (j + 1)], utok) + offs[:, j:j + 1] for j in range(nb)]
        return (jnp.concatenate(pieces, axis=1) if nb > 1 else pieces[0]), offs

    gt = xi > thr
    eq = xi == thr
    need = cap - jnp.sum(jnp.where(gt, 1.0, 0.0), axis=1, keepdims=True)
    rank_eq, _ = prefix(jnp.where(eq, 1.0, 0.0))
    sel = gt | (eq & (rank_eq < need))
    rank, offs = prefix(jnp.where(sel, 1.0, 0.0))
    rank_ref[0] = jnp.where(sel, rank, -1.0)
    offs_ref[0] = offs.astype(jnp.int32)


def _select(pt, *, cap):
    B, _, n = pt.shape
    return pl.pallas_call(
        functools.partial(_select_kernel, n=n, cap=cap),
        out_shape=[jax.ShapeDtypeStruct((B, N_EXPERTS, n), F32),
                   jax.ShapeDtypeStruct((B, N_EXPERTS, 128), jnp.int32)],
        grid=(B,),
        in_specs=[pl.BlockSpec((1, N_EXPERTS, n), lambda b: (b, 0, 0))],
        out_specs=[pl.BlockSpec((1, N_EXPERTS, n), lambda b: (b, 0, 0)),
                   pl.BlockSpec((1, N_EXPERTS, 128), lambda b: (b, 0, 0))],
        compiler_params=_cp(("arbitrary",), VMEM_LIMIT),
        name="expert_choice_select",
    )(pt)


def _gather_kernel(offs_ref, h_ref, rank_ref, prob_ref, o_ref, gate_ref, *, eg, per):
    b, g, tb = pl.program_id(0), pl.program_id(1), pl.program_id(2)

    @pl.when(tb == 0)
    def _():
        o_ref[...] = jnp.zeros(o_ref.shape, BF16)
        gate_ref[...] = jnp.zeros(gate_ref.shape, F32)

    h = h_ref[0]
    ntok = h.shape[0]
    cap_pad = o_ref.shape[2]
    win = min(2 * SLOT, cap_pad)

    def add_rows(i, r, p, base, width):
        slots = lax.broadcasted_iota(jnp.int32, (width, ntok), 0).astype(F32) + base.astype(F32)
        hit = r == slots
        rows = _dot(jnp.where(hit, 1.0, 0.0).astype(BF16), h).astype(BF16)
        o_ref[0, i, pl.ds(base, width), :] = o_ref[0, i, pl.ds(base, width), :] + rows
        gate_ref[0, i, pl.ds(base, width), :] = (gate_ref[0, i, pl.ds(base, width), :]
                                                 + jnp.sum(jnp.where(hit, p, 0.0), axis=1, keepdims=True))

    rs, ps, ends, his = [], [], [], []
    for i in range(eg):
        e = g * eg + i
        r = rank_ref[0, pl.ds(e, 1), :]
        p = prob_ref[0, pl.ds(e, 1), :]
        lo = offs_ref[b, e, tb * per]
        his.append(offs_ref[b, e, (tb + 1) * per])
        base = pl.multiple_of(jnp.minimum((lo // SLOT) * SLOT, cap_pad - win), SLOT)
        add_rows(i, r, p, base, win)
        rs.append(r)
        ps.append(p)
        ends.append(base + win)

    for i in range(eg):
        @pl.when(his[i] > ends[i])
        def _(i=i):
            def body(t, carry):
                add_rows(i, rs[i], ps[i], pl.multiple_of(t * SLOT, SLOT), SLOT)
                return carry

            lax.fori_loop(ends[i] // SLOT, (his[i] - 1) // SLOT + 1, body, 0)


def _gather(offs, hl, rank, pt, *, tb_tok, tb0, n, cap_pad, eg):
    B = hl.shape[0]
    per = tb_tok // TOK
    return pl.pallas_call(
        functools.partial(_gather_kernel, eg=eg, per=per),
        out_shape=[jax.ShapeDtypeStruct((B, N_EXPERTS, cap_pad, D), BF16),
                   jax.ShapeDtypeStruct((B, N_EXPERTS, cap_pad, 1), F32)],
        grid_spec=pltpu.PrefetchScalarGridSpec(
            num_scalar_prefetch=1,
            grid=(B, N_EXPERTS // eg, n // tb_tok),
            in_specs=[pl.BlockSpec((1, tb_tok, D), lambda b, g, t, o: (b, tb0 + t, 0)),
                      pl.BlockSpec((1, N_EXPERTS, tb_tok), lambda b, g, t, o: (b, 0, t)),
                      pl.BlockSpec((1, N_EXPERTS, tb_tok), lambda b, g, t, o: (b, 0, t))],
            out_specs=[pl.BlockSpec((1, eg, cap_pad, D), lambda b, g, t, o: (b, g, 0, 0)),
                       pl.BlockSpec((1, eg, cap_pad, 1), lambda b, g, t, o: (b, g, 0, 0))]),
        compiler_params=_cp(("arbitrary",) * 3, VMEM_LIMIT),
        name="expert_gather",
    )(offs, hl, rank, pt)


FFN_ROWS = 1024


def _ffn_kernel(x_ref, gate_ref, w1_ref, w3_ref, w2_ref, y_ref, acc_ref):
    f = pl.program_id(2)

    @pl.when(f == 0)
    def _():
        acc_ref[...] = jnp.zeros(acc_ref.shape, F32)

    w1 = w1_ref[0, 0].astype(BF16)
    w3 = w3_ref[0, 0].astype(BF16)
    w2 = w2_ref[0, 0].astype(BF16)
    mb, _, cap_pad, _ = x_ref.shape
    rows = min(FFN_ROWS, cap_pad)
    for i in range(mb):
        for r in range(0, cap_pad, rows):
            x = x_ref[i, 0, r:r + rows, :]
            hid = (_silu(_dot(x, w1)) * _dot(x, w3)).astype(BF16)
            acc_ref[i * cap_pad + r:i * cap_pad + r + rows, :] += _dot(hid, w2)

    @pl.when(f == pl.num_programs(2) - 1)
    def _():
        gate = gate_ref[...].reshape(-1, 1)
        y_ref[...] = (acc_ref[...] * gate).astype(BF16).reshape(y_ref.shape)


def _ffn(xs, gates, w1, w3, w2, *, layer, mb, tf):
    B, E, cap_pad, _ = xs.shape
    return pl.pallas_call(
        _ffn_kernel,
        out_shape=jax.ShapeDtypeStruct(xs.shape, BF16),
        grid=(E, B // mb, D_FF // tf),
        in_specs=[pl.BlockSpec((mb, 1, cap_pad, D), lambda e, m, f: (m, e, 0, 0)),
                  pl.BlockSpec((mb, 1, cap_pad, 1), lambda e, m, f: (m, e, 0, 0)),
                  pl.BlockSpec((1, 1, D, tf), lambda e, m, f: (layer, e, 0, f)),
                  pl.BlockSpec((1, 1, D, tf), lambda e, m, f: (layer, e, 0, f)),
                  pl.BlockSpec((1, 1, tf, D), lambda e, m, f: (layer, e, f, 0))],
        out_specs=pl.BlockSpec((mb, 1, cap_pad, D), lambda e, m, f: (m, e, 0, 0)),
        scratch_shapes=[pltpu.VMEM((mb * cap_pad, D), F32)],
        compiler_params=_cp(("arbitrary",) * 3, VMEM_LIMIT),
        name="expert_ffn",
    )(xs, gates, w1, w3, w2)


CCOL = 512


def _combine_kernel(offs_ref, x_ref, y_ref, rankc_ref, ada_ref, o_ref, tot_scr, *, per, is_ctx):
    b, tb = pl.program_id(0), pl.program_id(2)
    gt2 = ada_ref[CTX_ROW if is_ctx else b][5:6]
    rc_all = rankc_ref[0]
    cap_pad = y_ref.shape[2]
    win = min(2 * SLOT, cap_pad)
    slotw = lax.broadcasted_iota(jnp.int32, (1, win), 1).astype(F32)
    slot = lax.broadcasted_iota(jnp.int32, (1, SLOT), 1).astype(F32)

    bases, his = [], []
    total = jnp.zeros(tot_scr.shape, F32)
    for e in range(N_EXPERTS):
        lo = offs_ref[b, e, tb * per]
        his.append(offs_ref[b, e, (tb + 1) * per])
        base = pl.multiple_of(jnp.minimum((lo // SLOT) * SLOT, cap_pad - win), SLOT)
        bases.append(base)
        onehot = jnp.where(rc_all[:, e:e + 1] == slotw + base.astype(F32), 1.0, 0.0).astype(BF16)
        total = total + _dot(onehot, y_ref[0, e, pl.ds(base, win), :])
    tot_scr[...] = total

    for e in range(N_EXPERTS):
        end = bases[e] + win

        @pl.when(his[e] > end)
        def _(e=e, end=end):
            def body(t, carry):
                base = pl.multiple_of(t * SLOT, SLOT)
                onehot = jnp.where(rc_all[:, e:e + 1] == slot + base.astype(F32), 1.0, 0.0).astype(BF16)
                tot_scr[...] += _dot(onehot, y_ref[0, e, pl.ds(base, SLOT), :])
                return carry

            lax.fori_loop(end // SLOT, (his[e] - 1) // SLOT + 1, body, 0)

    o_ref[0] = x_ref[0] + gt2 * tot_scr[...]


def _combine(offs, xu, ys, rank_c, ada_l, *, tb_tok, tb0, n, is_ctx):
    B = xu.shape[0]
    cap_pad = ys.shape[2]
    per = tb_tok // TOK
    return pl.pallas_call(
        functools.partial(_combine_kernel, per=per, is_ctx=is_ctx),
        out_shape=jax.ShapeDtypeStruct(xu.shape, F32),
        grid_spec=pltpu.PrefetchScalarGridSpec(
            num_scalar_prefetch=1,
            grid=(B, D // CCOL, n // tb_tok),
            in_specs=[pl.BlockSpec((1, tb_tok, CCOL), lambda b, c, t, o: (b, tb0 + t, c)),
                      pl.BlockSpec((1, N_EXPERTS, cap_pad, CCOL), lambda b, c, t, o: (b, 0, 0, c),
                                   pipeline_mode=pl.Buffered(1)),
                      pl.BlockSpec((1, tb_tok, N_EXPERTS), lambda b, c, t, o: (b, t, 0)),
                      pl.BlockSpec((ADA_ROWS, ADA_CHUNKS, CCOL), lambda b, c, t, o: (0, 0, c))],
            out_specs=pl.BlockSpec((1, tb_tok, CCOL), lambda b, c, t, o: (b, tb0 + t, c)),
            scratch_shapes=[pltpu.VMEM((tb_tok, CCOL), F32)]),
        input_output_aliases={1: 0},
        compiler_params=_cp(("arbitrary",) * 3, VMEM_LIMIT),
        name="expert_combine",
    )(offs, xu, ys, rank_c, ada_l)


def _moe(xu, hl, pt, ada_l, w1, w3, w2, *, layer, row0, is_ctx):
    B, _, n = pt.shape
    cap = EC_CAPACITY * n // N_EXPERTS
    cap_pad = -(-cap // SLOT) * SLOT
    nb = n // TOK
    rank, offs = _select(pt, cap=cap)
    offs = offs[:, :, :nb + 1]
    gt = ct = min(n, MOE_TOK)
    xs, gates = _gather(offs, hl, rank, pt, tb_tok=gt, tb0=row0 // gt, n=n, cap_pad=cap_pad, eg=GATHER_EXPERTS)
    mb = 2 if (B % 2 == 0 and cap_pad >= 1024) else (B if cap_pad < 1024 else 1)
    ys = _ffn(xs, gates, w1, w3, w2, layer=layer, mb=mb, tf=FFN_TF)
    rank_c = jnp.swapaxes(rank, 1, 2)
    return _combine(offs, xu, ys, rank_c, ada_l, tb_tok=ct, tb0=row0 // ct, n=n, is_ctx=is_ctx)


def _final_kernel(x_ref, g_ref, o_ref):
    x = x_ref[0]
    r = lax.rsqrt(jnp.mean(x * x, axis=-1, keepdims=True) + EPS)
    o_ref[0] = (x * r) * g_ref[...]


def _final_norm(xu, g, *, n):
    B = xu.shape[0]
    tm = MOE_TOK
    return pl.pallas_call(
        _final_kernel,
        out_shape=jax.ShapeDtypeStruct((B, n, D), F32),
        grid=(B, n // tm),
        in_specs=[pl.BlockSpec((1, tm, D), lambda b, t: (b, t, 0)),
                  pl.BlockSpec((1, D), lambda b, t: (0, 0))],
        out_specs=pl.BlockSpec((1, tm, D), lambda b, t: (b, t, 0)),
        compiler_params=_cp(("arbitrary", "arbitrary")),
        name="final_norm",
    )(xu, g)


def _rope_tables(n, ctx):
    rows = n // GRID_W
    t_row = jnp.repeat(jnp.arange(rows), GRID_W)
    t_col = jnp.tile(jnp.arange(GRID_W), rows)
    nf = DA_DIM // 4
    inv = ROPE_THETA ** (-jnp.arange(nf, dtype=F32) / nf)
    ar = t_row[:, None].astype(F32) * inv
    ac = t_col[:, None].astype(F32) * inv
    ang = jnp.concatenate([ar, ar, ac, ac], axis=-1)
    sign = jnp.where((jnp.arange(DA_DIM) % 16) < 8, -1.0, 1.0).astype(F32)
    cos = jnp.concatenate([jnp.cos(ang), jnp.ones((ctx, DA_DIM), F32)], axis=0)
    sin = jnp.concatenate([jnp.sin(ang) * sign, jnp.zeros((ctx, DA_DIM), F32)], axis=0)
    return jnp.tile(cos, (1, 128 // DA_DIM)), jnp.tile(sin, (1, 128 // DA_DIM))


def _pad_heads_cols(w):
    lead = w.shape[:-1]
    w = w.reshape(lead + (M_HEADS, M_DIM))
    w = jnp.pad(w, [(0, 0)] * len(lead) + [(0, 0), (0, M_PAD - M_DIM)])
    return w.reshape(lead + (MP_WIDTH,))


def _kv_tile(nt):
    for parts in range(1, nt // LANES + 1):
        if nt % parts == 0 and (nt // parts) % LANES == 0 and nt // parts <= KV_TILE_MAX:
            return nt // parts
    raise ValueError(nt)


def kernel(x, c, ctx, c_ctx, ada_w, ada_b, norm1_g, norm2_g, w_in, four_w, m_conv_w, m_conv_b, m_gate_b,
           m_norm_g, d_lam, d_norm_g, w_out, router_w, exp_w1, exp_w3, exp_w2, final_g):
    B, N, _ = x.shape
    CTX = ctx.shape[1]
    depth = w_in.shape[0]
    assert CTX == TOK and N % (FFT_N1 * TOK) == 0 and N % Q_TILE == 0 and B <= CTX_ROW
    NT = N + CTX
    PAD = -NT % MOE_TOK
    n_lat = N // TOK
    n2 = N // FFT_N1

    xu = jnp.concatenate([x, ctx, jnp.zeros((B, PAD, D), F32)], axis=1)
    cvecs = jnp.zeros((ADA_ROWS, D), F32).at[:B].set(c).at[CTX_ROW].set(c_ctx)
    ada = _adaln(cvecs, ada_w, ada_b).reshape(depth, ADA_ROWS, ADA_CHUNKS, D)
    cos_t, sin_t = _rope_tables(N, CTX + PAD)
    tabs = _fourier_tables(N, CTX)
    tk = _kv_tile(NT)
    tq = Q_TILE

    hl = jnp.zeros((B, NT + PAD, D), BF16)
    for layer in range(depth):
        ctx_out = layer < depth - 1
        lam_init = 0.8 - 0.6 * math.exp(-0.3 * layer)
        w = w_in[layer]
        wm = jnp.concatenate([w[:, OFF_F:OFF_DQ], w[:, OFF_DQ:OFF_MO], w[:, OFF_DK:OFF_DV]], axis=1).astype(BF16)
        wvt = jnp.concatenate([w[:, OFF_DV:OFF_MV], _pad_heads_cols(w[:, OFF_MO:OFF_MQ]),
                               _pad_heads_cols(w[:, OFF_MV:OFF_G])], axis=1).T.astype(BF16)
        wg = jnp.pad(w[:, OFF_G:], ((0, 0), (0, LANES - N_GATES)))
        wc = jnp.concatenate([_pad_heads_cols(w[:, OFF_MQ:OFF_MK]), _pad_heads_cols(w[:, OFF_MK:OFF_DK])],
                             axis=1).astype(BF16)
        gb = jnp.pad(m_gate_b[layer], (0, LANES - N_GATES)).reshape(1, LANES)
        cw = jnp.concatenate([_pad_heads_cols(m_conv_w[layer][:, :M_WIDTH]),
                              _pad_heads_cols(m_conv_w[layer][:, M_WIDTH:])], axis=1)
        cb = jnp.concatenate([_pad_heads_cols(m_conv_b[layer][:M_WIDTH]),
                              _pad_heads_cols(m_conv_b[layer][M_WIDTH:])]).reshape(1, 2 * MP_WIDTH)
        ada_l = ada[layer]

        y4, dq, dk, dvT, mo, mq, mk, mv, gl, glT = _inproj(
            xu, ada_l, norm1_g[layer].reshape(1, D), wm, wc, wvt, wg, gb, tabs["cs"], cos_t, sin_t, cw, cb,
            n_lat=n_lat, n2=n2)

        wblk = jnp.zeros((F_WIDTH, F_WIDTH), F32)
        for g in range(F_GROUPS):
            wblk = wblk.at[F_GDIM * g:F_GDIM * (g + 1), F_GDIM * g:F_GDIM * (g + 1)].set(four_w[layer, g])
        f_l, f_c = _fourier(y4, tabs, wblk.astype(BF16), n=N, ctx=CTX, with_ctx=ctx_out)

        dlam = d_lam[layer]
        g2 = d_norm_g[layer].reshape(DA_VDIM, 1)
        da_l = _attention(dq, dk, dvT, dlam, g2, lam_init=lam_init, tq=tq, q0=0, nq=N // tq,
                          tk=tk, k0=0, nk=NT // tk)

        hf, hb = _mlstm(mq, mk, mv, gl, glT, n_lat=n_lat)

        mg = _pad_heads_cols(m_norm_g[layer]).reshape(MP_WIDTH, 1)
        wol = w_out[layer]
        wo = wol[:F_WIDTH].astype(BF16)
        wod = jnp.concatenate([wol[F_WIDTH:F_WIDTH + DA_WIDTH],
                               jnp.pad(wol[F_WIDTH + DA_WIDTH:].reshape(M_HEADS, M_DIM, D),
                                       ((0, 0), (0, M_PAD - M_DIM), (0, 0))).reshape(MP_WIDTH, D)],
                              axis=0).astype(BF16)
        g2n = norm2_g[layer].reshape(1, D)
        wrp = jnp.pad(router_w[layer], ((0, 0), (0, LANES - N_EXPERTS)))
        xu, hl, pt_l = _outproj(xu, f_l, da_l, hf, hb, mo, ada_l, mg, wo, wod, g2n, wrp, hl,
                                t0=0, ntl=n_lat, is_ctx=False)
        if ctx_out:
            da_c = _attention(dq, dk, dvT, dlam, g2, lam_init=lam_init, tq=TOK, q0=n_lat, nq=1,
                              tk=TOK, k0=n_lat, nk=1)
            xu, hl, pt_c = _outproj(xu, f_c, da_c, hf, hb, mo, ada_l, mg, wo, wod, g2n, wrp, hl,
                                    t0=n_lat, ntl=1, is_ctx=True)

        xu = _moe(xu, hl, pt_l, ada_l, exp_w1, exp_w3, exp_w2, layer=layer, row0=0, is_ctx=False)
        if ctx_out:
            xu = _moe(xu, hl, pt_c, ada_l, exp_w1, exp_w3, exp_w2, layer=layer, row0=N, is_ctx=True)

    return _final_norm(xu, final_g.reshape(1, D), n=N)
```

```python
import functools
import math

import numpy as np
import jax
import jax.numpy as jnp
from jax import lax
from jax.experimental import pallas as pl
from jax.experimental.pallas import tpu as pltpu

F32 = jnp.float32
BF16 = jnp.bfloat16
HI = lax.Precision.HIGHEST

D = 1024
EPS = 1e-6
GRID_W = 64
ROPE_THETA = 10000.0
F_GROUPS, F_GDIM = 4, 64
F_WIDTH = F_GROUPS * F_GDIM
DA_HEADS, DA_DIM = 6, 32
DA_VDIM = 2 * DA_DIM
DA_WIDTH = DA_HEADS * DA_VDIM
M_HEADS, M_DIM = 4, 96
M_WIDTH = M_HEADS * M_DIM
M_PAD = 128
MP_WIDTH = M_HEADS * M_PAD
N_GATES = 4 * M_HEADS
N_EXPERTS = 16
EC_CAPACITY = 2
D_FF = 2 * D
ADA_CHUNKS = 6
ADA_ROWS = 8
CTX_ROW = 4

LANES = 128
MXU_DIM = 256
V7X_VMEM_BYTES = 64 * 1024 * 1024

TOK = MXU_DIM
FFT_N1 = 16
SLOT = LANES
Q_TILE = 2048
KV_TILE_MAX = 1408
MOE_TOK = 512
COMBINE_TOK = 256
FFN_TF = 512
GATHER_EXPERTS = 4
NEG = -1e30

OFF_F = 0
OFF_DQ = OFF_F + F_WIDTH
OFF_MO = OFF_DQ + 2 * DA_HEADS * DA_DIM
OFF_MQ = OFF_MO + M_WIDTH
OFF_MK = OFF_MQ + M_WIDTH
OFF_DK = OFF_MK + M_WIDTH
OFF_DV = OFF_DK + 2 * DA_HEADS * DA_DIM
OFF_MV = OFF_DV + DA_HEADS * DA_VDIM
OFF_G = OFF_MV + M_WIDTH

VMEM_LIMIT = V7X_VMEM_BYTES * 7 // 8


def _cp(sem, vmem=None):
    return pltpu.CompilerParams(dimension_semantics=sem, vmem_limit_bytes=vmem)


def _sigmoid(x):
    return 1.0 / (1.0 + jnp.exp(-x))


def _silu(x):
    return x * _sigmoid(x)


def _dot(a, b, precision=None):
    return jnp.dot(a, b, preferred_element_type=F32, precision=precision)


def _split(a):
    hi = a.astype(BF16)
    return hi, (a - hi.astype(F32)).astype(BF16)


def _dot3(a, b):
    a_hi, a_lo = a if isinstance(a, tuple) else _split(a)
    b_hi, b_lo = b if isinstance(b, tuple) else _split(b)
    return _dot(a_hi, b_hi) + _dot(a_hi, b_lo) + _dot(a_lo, b_hi)


def _split3(a):
    hi = a.astype(BF16)
    r = a - hi.astype(F32)
    mid = r.astype(BF16)
    return hi, mid, (r - mid.astype(F32)).astype(BF16)


def _dot_nt(a, b, precision=None):
    return lax.dot_general(a, b, (((1,), (1,)), ((), ())), preferred_element_type=F32,
                           precision=precision)


def _ada_kernel(c_ref, w_ref, b_ref, o_ref):
    c = c_ref[...]
    o_ref[0] = _dot(_silu(c), w_ref[0], HI) + b_ref[0]


def _adaln(cvecs, ada_w, ada_b):
    depth = ada_w.shape[0]
    tn = 1536
    return pl.pallas_call(
        _ada_kernel,
        out_shape=jax.ShapeDtypeStruct((depth, ADA_ROWS, ADA_CHUNKS * D), F32),
        grid=(depth, ADA_CHUNKS * D // tn),
        in_specs=[pl.BlockSpec((ADA_ROWS, D), lambda l, j: (0, 0)),
                  pl.BlockSpec((1, D, tn), lambda l, j: (l, 0, j)),
                  pl.BlockSpec((1, 1, tn), lambda l, j: (l, 0, j))],
        out_specs=pl.BlockSpec((1, 8, tn), lambda l, j: (l, 0, j)),
        compiler_params=_cp(("arbitrary", "arbitrary")),
        name="adaln",
    )(cvecs, ada_w, ada_b.reshape(depth, 1, ADA_CHUNKS * D))


def _inproj_kernel(x_ref, xp_ref, xn_ref, ada_ref, g_ref, wm_ref, wc_ref, wvt_ref, wg_ref, gb_ref, cs_ref,
                   cos_ref, sin_ref, cw_ref, cb_ref,
                   y_ref, dq_ref, dk_ref, dvt_ref, mo_ref, mq_ref, mk_ref, mv_ref, gl_ref, glt_ref, *, n_lat):
    b = pl.program_id(0)
    t = pl.program_id(1)
    n_tiles = pl.num_programs(1)
    is_ctx = t >= n_lat
    row = jnp.where(is_ctx, CTX_ROW, b)
    mod = ada_ref[row]
    sh, sc = mod[0:1], mod[1:2]

    xa = jnp.concatenate([xp_ref[0], x_ref[0], xn_ref[0]], axis=0)
    r = lax.rsqrt(jnp.mean(xa * xa, axis=-1, keepdims=True) + EPS)
    ha = (xa * r) * g_ref[...] * (1.0 + sc) + sh
    h = ha[8:8 + TOK]
    hb = h.astype(BF16)

    pm = _dot(hb, wm_ref[...])
    o = 0
    pf = pm[:, o:o + F_WIDTH]; o += F_WIDTH
    q = pm[:, o:o + DA_WIDTH]; o += DA_WIDTH
    k = pm[:, o:o + DA_WIDTH]; o += DA_WIDTH
    pt = _dot_nt(wvt_ref[...], hb)
    dvt_ref[0] = pt[:DA_WIDTH].astype(BF16)
    mo_ref[0] = pt[DA_WIDTH:DA_WIDTH + MP_WIDTH].astype(BF16)
    mv_ref[0] = pt[DA_WIDTH + MP_WIDTH:].astype(BF16)

    y_ref[0, 0] = _dot3(pf, cs_ref[...])

    cos = cos_ref[...]
    sin = sin_ref[...]
    lane = lax.broadcasted_iota(jnp.int32, (1, 128), 1)
    low = (lane % 16) < 8

    def rope(z):
        parts = []
        for c in range(DA_WIDTH // 128):
            zc = z[:, 128 * c:128 * (c + 1)]
            rot = jnp.where(low, pltpu.roll(zc, 120, 1), pltpu.roll(zc, 8, 1))
            parts.append(zc * cos + rot * sin)
        return jnp.concatenate(parts, axis=1)

    dq_ref[0] = (rope(q) * (DA_DIM ** -0.5 * math.log2(math.e))).astype(BF16)
    dk_ref[0] = rope(k).astype(BF16)

    gpre = _dot3(h, wg_ref[...]) + gb_ref[...]
    is_forget = (lax.broadcasted_iota(jnp.int32, (1, LANES), 1) % 8) >= 4
    logsig = jnp.minimum(gpre, 0.0) - jnp.log(1.0 + jnp.exp(-jnp.abs(gpre)))
    gl = jnp.where(is_forget, logsig, gpre)
    gl_ref[0] = gl
    glt_ref[0] = gl.T[:N_GATES]

    pc = _dot(ha.astype(BF16), wc_ref[...])
    first = (t == 0) | (t == n_lat)
    last = (t == n_lat - 1) | (t == n_tiles - 1)
    ridx = lax.broadcasted_iota(jnp.int32, (TOK + 16, 1), 0)
    pc = jnp.where(((ridx < 8) & first) | ((ridx >= TOK + 8) & last), 0.0, pc)
    cw = cw_ref[...]
    conv = cb_ref[...] + pc[7:7 + TOK] * cw[0:1] + pc[8:8 + TOK] * cw[1:2] + pc[9:9 + TOK] * cw[2:3]
    act = _silu(conv)
    mq_ref[0] = act[:, :MP_WIDTH].astype(BF16)
    mk_ref[0] = (act[:, MP_WIDTH:] * (M_DIM ** -0.5)).astype(BF16)


def _inproj(xu, ada_l, g1, wm, wc, wvt, wg, gb, cs, cos_t, sin_t, cw, cb, *, n_lat, n2):
    B, NT, _ = xu.shape
    nt = n_lat + 1
    rper = n2 // TOK
    tok3 = lambda w: pl.BlockSpec((1, TOK, w), lambda b, t: (b, t, 0))
    full = lambda a: pl.BlockSpec(a.shape, lambda b, t: (0,) * a.ndim)
    nb8 = NT // 8
    outs = [jax.ShapeDtypeStruct((B, 2 * FFT_N1, n2, 2 * F_WIDTH), F32)]
    nq = -(-NT // Q_TILE) * Q_TILE
    outs += [jax.ShapeDtypeStruct((B, nq, DA_WIDTH), BF16), jax.ShapeDtypeStruct((B, nt * TOK, DA_WIDTH), BF16)]
    outs += [jax.ShapeDtypeStruct((B, DA_WIDTH, nt * TOK), BF16)]
    trs = lambda r: pl.BlockSpec((1, r, TOK), lambda b, t: (b, 0, t))
    trp = jax.ShapeDtypeStruct((B, MP_WIDTH, NT), BF16)
    outs += [trp, jax.ShapeDtypeStruct((B, NT, MP_WIDTH), BF16), jax.ShapeDtypeStruct((B, NT, MP_WIDTH), BF16), trp]
    outs += [jax.ShapeDtypeStruct((B, NT, 128), F32), jax.ShapeDtypeStruct((B, N_GATES, nt * TOK), F32)]
    out_specs = [pl.BlockSpec((1, 1, TOK, 2 * F_WIDTH), lambda b, t: (b, t // rper, t % rper, 0))]
    out_specs += [tok3(DA_WIDTH)] * 2 + [trs(DA_WIDTH)]
    out_specs += [trs(MP_WIDTH), tok3(MP_WIDTH), tok3(MP_WIDTH), trs(MP_WIDTH)]
    out_specs += [tok3(128), pl.BlockSpec((1, N_GATES, TOK), lambda b, t: (b, 0, t))]
    return pl.pallas_call(
        functools.partial(_inproj_kernel, n_lat=n_lat),
        out_shape=outs,
        grid=(B, nt),
        in_specs=[tok3(D),
                  pl.BlockSpec((1, 8, D), lambda b, t: (b, jnp.maximum(t * (TOK // 8) - 1, 0), 0)),
                  pl.BlockSpec((1, 8, D), lambda b, t: (b, jnp.minimum((t + 1) * (TOK // 8), nb8 - 1), 0)),
                  full(ada_l), full(g1), full(wm), full(wc), full(wvt), full(wg), full(gb), full(cs),
                  pl.BlockSpec((TOK, LANES), lambda b, t: (t, 0)),
                  pl.BlockSpec((TOK, LANES), lambda b, t: (t, 0)),
                  full(cw), full(cb)],
        out_specs=out_specs,
        compiler_params=_cp(("arbitrary", "arbitrary"), VMEM_LIMIT),
        name="norm1_inproj",
    )(xu, xu, xu, ada_l, g1, wm, wc, wvt, wg, gb, cs, cos_t, sin_t, cw, cb)


def _fft1_kernel(y_ref, kc_ref, ks_ref, tc_ref, ts_ref, o_ref, *, groups):
    kc = _split(kc_ref[...])
    ks = _split(ks_ref[...])
    for g in range(groups):
        blk = _split(y_ref[0, :, 8 * g:8 * (g + 1), :].reshape(FFT_N1 * 8, 2 * F_WIDTH))
        p = _dot3(kc, blk)
        q = _dot3(ks, blk)
        ar = p[:, :F_WIDTH] - q[:, F_WIDTH:]
        ai = -p[:, F_WIDTH:] - q[:, :F_WIDTH]
        tc = tc_ref[128 * g:128 * (g + 1), :]
        ts = ts_ref[128 * g:128 * (g + 1), :]
        tc = jnp.concatenate([tc, tc], axis=1)
        ts = jnp.concatenate([ts, ts], axis=1)
        br = ar * tc + ai * ts
        bi = ai * tc - ar * ts
        o_ref[0, :, 8 * g:8 * (g + 1), :] = jnp.concatenate([br, bi], axis=1).reshape(FFT_N1, 8, 2 * F_WIDTH)


def _fft2_kernel(b_ref, c2_ref, s2_ref, wb_ref, perm_ref, o_ref, r_scr, *, n2):
    c2 = _split(c2_ref[...])
    s2 = _split(s2_ref[...])
    for i in range(8):
        blk = b_ref[0, i]
        xr = _dot3(c2, blk[:, :F_WIDTH]) + _dot3(s2, blk[:, F_WIDTH:])
        r_scr[i] = _dot(xr.astype(BF16), wb_ref[...]).astype(BF16)
    for t in range(n2 // 32):
        rows = jnp.concatenate([r_scr[i, 32 * t:32 * (t + 1), :] for i in range(8)], axis=0)
        o_ref[0, 32 * t:32 * (t + 1), :, :] = _dot(perm_ref[...], rows).reshape(32, 8, F_WIDTH)


def _fftc_kernel(y_ref, c_ref, s_ref, wb_ref, o_ref):
    y = y_ref[0, 0]
    z = _dot3(c_ref[...], y[:, :F_WIDTH]) - _dot3(s_ref[...], y[:, F_WIDTH:])
    o_ref[0] = _dot(z.astype(BF16), wb_ref[...])


def _fourier_tables(n, ctx):
    n1, n2 = FFT_N1, n // FFT_N1
    a = np.arange(n1)
    ang1 = 2 * np.pi * np.outer(a, a) / n1
    eye8 = np.eye(8)
    kc = np.kron(np.cos(ang1), eye8)
    ks = np.kron(np.sin(ang1), eye8)
    n2i = np.arange(n2).reshape(n2 // 8, 1, 8)
    k1 = np.arange(n1).reshape(1, n1, 1)
    angt = (2 * np.pi * n2i * k1 / n).reshape(-1, 1)
    tc = np.broadcast_to(np.cos(angt), (n2 // 8 * 128, 128))
    ts = np.broadcast_to(np.sin(angt), (n2 // 8 * 128, 128))
    b = np.arange(n2)
    ang2 = 2 * np.pi * np.outer(b, b) / n2
    c2 = np.cos(ang2) / math.sqrt(n)
    s2 = np.sin(ang2) / math.sqrt(n)
    perm = np.zeros((256, 256))
    for kk in range(8):
        for j in range(32):
            perm[j * 8 + kk, kk * 32 + j] = 1.0
    cc = np.arange(ctx)
    angc = 2 * np.pi * np.outer(cc, cc) / ctx
    cctx = np.cos(angc) / math.sqrt(ctx)
    sctx = np.sin(angc) / math.sqrt(ctx)
    ch = np.arange(F_GDIM)
    angch = 2 * np.pi * np.outer(ch, ch) / F_GDIM
    cs = np.concatenate([np.kron(np.eye(F_GROUPS), np.cos(angch)),
                         np.kron(np.eye(F_GROUPS), np.sin(angch))], axis=1) / math.sqrt(F_GDIM)
    f = lambda z: jnp.asarray(np.ascontiguousarray(z), dtype=F32)
    return dict(kc=f(kc), ks=f(ks), tc=f(tc), ts=f(ts), c2=f(c2), s2=f(s2), perm=f(perm).astype(BF16),
                cctx=f(cctx), sctx=f(sctx), cs=f(cs))


def _fourier(y4, tabs, wblk, *, n, ctx, with_ctx):
    B = y4.shape[0]
    n2 = n // FFT_N1
    groups = 4
    full = lambda a, nd: pl.BlockSpec(a.shape, lambda *i: (0,) * a.ndim)
    b4 = pl.pallas_call(
        functools.partial(_fft1_kernel, groups=groups),
        out_shape=jax.ShapeDtypeStruct((B, FFT_N1, n2, 2 * F_WIDTH), F32),
        grid=(B, n2 // (8 * groups)),
        in_specs=[pl.BlockSpec((1, FFT_N1, 8 * groups, 2 * F_WIDTH), lambda b, j: (b, 0, j, 0)),
                  full(tabs["kc"], 2), full(tabs["ks"], 2),
                  pl.BlockSpec((128 * groups, 128), lambda b, j: (j, 0)),
                  pl.BlockSpec((128 * groups, 128), lambda b, j: (j, 0))],
        out_specs=pl.BlockSpec((1, FFT_N1, 8 * groups, 2 * F_WIDTH), lambda b, j: (b, 0, j, 0)),
        compiler_params=_cp(("arbitrary", "arbitrary")),
        name="fourier_stage1",
    )(y4, tabs["kc"], tabs["ks"], tabs["tc"], tabs["ts"])
    f4 = pl.pallas_call(
        functools.partial(_fft2_kernel, n2=n2),
        out_shape=jax.ShapeDtypeStruct((B, n2, 16, F_WIDTH), F32),
        grid=(B, FFT_N1 // 8),
        in_specs=[pl.BlockSpec((1, 8, n2, 2 * F_WIDTH), lambda b, j: (b, j, 0, 0)),
                  full(tabs["c2"], 2), full(tabs["s2"], 2), full(wblk, 2), full(tabs["perm"], 2)],
        out_specs=pl.BlockSpec((1, n2, 8, F_WIDTH), lambda b, j: (b, 0, j, 0)),
        scratch_shapes=[pltpu.VMEM((8, n2, F_WIDTH), BF16)],
        compiler_params=_cp(("arbitrary", "arbitrary"), VMEM_LIMIT),
        name="fourier_stage2",
    )(b4, tabs["c2"], tabs["s2"], wblk, tabs["perm"])
    f_ctx = None
    if with_ctx:
        f_ctx = pl.pallas_call(
            _fftc_kernel,
            out_shape=jax.ShapeDtypeStruct((B, ctx, F_WIDTH), F32),
            grid=(B,),
            in_specs=[pl.BlockSpec((1, 1, TOK, 2 * F_WIDTH), lambda b: (b, FFT_N1, 0, 0)),
                      full(tabs["cctx"], 1), full(tabs["sctx"], 1), full(wblk, 1)],
            out_specs=pl.BlockSpec((1, ctx, F_WIDTH), lambda b: (b, 0, 0)),
            compiler_params=_cp(("arbitrary",)),
            name="fourier_ctx",
        )(y4, tabs["cctx"], tabs["sctx"], wblk)
    return f4.reshape(B, n, F_WIDTH), f_ctx


VROWS = DA_VDIM + 16


def _attn_kernel(q_ref, k_ref, vt_ref, dl_ref, g_ref, o_ref, m_scr, acc_scr, *, lam_init):
    kt = pl.program_id(3)
    nk = pl.num_programs(3)

    @pl.when(kt == 0)
    def _():
        m_scr[...] = jnp.full(m_scr.shape, NEG, F32)
        acc_scr[...] = jnp.zeros(acc_scr.shape, F32)

    q = q_ref[0]
    k = k_ref[0]
    vt = vt_ref[0]
    ones = jnp.ones((16, vt.shape[1]), BF16)
    lhs = [jnp.concatenate([vt[DA_VDIM * h:DA_VDIM * (h + 1)], ones], axis=0) for h in range(2)]
    lane = lax.broadcasted_iota(jnp.int32, (1, LANES), 1)
    zero = jnp.zeros((), BF16)

    def scores(j):
        return _dot_nt(k, jnp.where((lane // DA_DIM) == j, q, zero))

    st_next = scores(0)
    for j in range(4):
        st = st_next
        if j < 3:
            st_next = scores(j + 1)
        m_old = m_scr[j]
        m_new = jnp.maximum(m_old, jnp.max(st, axis=0, keepdims=True))
        alpha = jnp.exp2(m_old - m_new)
        pt = jnp.exp2(st - m_new).astype(BF16)
        acc_scr[j] = alpha * acc_scr[j] + _dot(lhs[j // 2], pt)
        m_scr[j] = m_new

    @pl.when(kt == nk - 1)
    def _():
        dl = dl_ref[...]
        lam = (jnp.exp(jnp.sum(dl[0:1] * dl[1:2], keepdims=True))
               - jnp.exp(jnp.sum(dl[2:3] * dl[3:4], keepdims=True)) + lam_init)
        outs = []
        for h in range(2):
            a0 = acc_scr[2 * h]
            a1 = acc_scr[2 * h + 1]
            o = (a0[:DA_VDIM] / a0[DA_VDIM:DA_VDIM + 1]
                 - lam * (a1[:DA_VDIM] / a1[DA_VDIM:DA_VDIM + 1]))
            r = lax.rsqrt(jnp.mean(o * o, axis=0, keepdims=True) + EPS)
            outs.append(((o * r) * g_ref[...]) * (1.0 - lam_init))
        o_ref[0] = jnp.concatenate(outs, axis=0).astype(BF16)


def _attention(dq, dk, dvT, dlam, gcol, *, lam_init, tq, q0, nq, tk, k0, nk):
    B = dq.shape[0]
    return pl.pallas_call(
        functools.partial(_attn_kernel, lam_init=lam_init),
        out_shape=jax.ShapeDtypeStruct((B, DA_WIDTH, nq * tq), BF16),
        grid=(B, DA_WIDTH // 128, nq, nk),
        in_specs=[pl.BlockSpec((1, tq, 128), lambda b, p, i, j: (b, q0 + i, p)),
                  pl.BlockSpec((1, tk, 128), lambda b, p, i, j: (b, k0 + j, p)),
                  pl.BlockSpec((1, 128, tk), lambda b, p, i, j: (b, p, k0 + j)),
                  pl.BlockSpec(dlam.shape, lambda b, p, i, j: (0, 0)),
                  pl.BlockSpec(gcol.shape, lambda b, p, i, j: (0, 0))],
        out_specs=pl.BlockSpec((1, 128, tq), lambda b, p, i, j: (b, p, i)),
        scratch_shapes=[pltpu.VMEM((4, 1, tq), F32), pltpu.VMEM((4, VROWS, tq), F32)],
        compiler_params=_cp(("arbitrary",) * 4, VMEM_LIMIT),
        name="diff_attention",
    )(dq, dk, dvT, dlam, gcol)


def _mlstm_kernel(qf_ref, kf_ref, vf_ref, gcf_ref, grf_ref, qb_ref, kb_ref, vb_ref, gcb_ref, grb_ref,
                  hf_ref, hb_ref, c_scr, m_scr):
    t = pl.program_id(1)

    @pl.when(t == 0)
    def _():
        c_scr[...] = jnp.zeros(c_scr.shape, F32)
        m_scr[...] = jnp.zeros(m_scr.shape, F32)

    L = TOK
    si = lax.broadcasted_iota(jnp.int32, (L, L), 0)
    li = lax.broadcasted_iota(jnp.int32, (L, L), 1)
    dirs = ((qf_ref, kf_ref, vf_ref, gcf_ref, grf_ref, hf_ref, si <= li, li <= si, L - 1),
            (qb_ref, kb_ref, vb_ref, gcb_ref, grb_ref, hb_ref, si >= li, li >= si, 0))
    ones = jnp.ones((16, L), F32)
    for d, (q_ref, k_ref, vt_ref, gc_ref, gr_ref, h_ref, seen, seen_t, last) in enumerate(dirs):
        gc = gc_ref[0]
        gr = gr_ref[0]
        seen_b = jnp.where(seen, 1.0, 0.0).astype(BF16)
        seen_tb = jnp.where(seen_t, 1.0, 0.0).astype(BF16)
        bcols = sum(_dot(seen_tb, piece) for piece in _split3(gc))
        brows = sum(_dot(piece, seen_b) for piece in _split3(gr))
        for hd in range(M_HEADS):
            idx = d * M_HEADS + hd
            ji = d * 8 + hd
            jf = d * 8 + 4 + hd
            sl = slice(M_PAD * hd, M_PAD * (hd + 1))
            q = q_ref[0, :, sl]
            k = k_ref[0, :, sl]
            vt = vt_ref[0, sl, :]
            b_row = brows[jf:jf + 1, :]
            cs = gc[:, ji:ji + 1] - bcols[:, jf:jf + 1]
            li_row = gr[ji:ji + 1, :]
            m_old = m_scr[idx][0:1, 0:1]
            c_old = c_scr[idx]

            dlog = jnp.where(seen, b_row + cs, NEG)
            inter = b_row + m_old
            m_t = jnp.maximum(inter, jnp.max(dlog, axis=0, keepdims=True))
            w_inter = jnp.exp(inter - m_t)
            st = _dot_nt(k, q) * jnp.exp(dlog - m_t)
            cq = _dot_nt(c_old.astype(BF16), q)
            num = w_inter * cq[:M_PAD] + _dot(vt, st.astype(BF16))
            den = w_inter * cq[M_PAD:M_PAD + 1] + jnp.sum(st, axis=0, keepdims=True)
            h_ref[0, sl, :] = num / jnp.maximum(jnp.abs(den), jnp.exp(-m_t))

            total = b_row[:, last:last + 1]
            wlog = total - b_row + li_row
            m_new = jnp.maximum(total + m_old, jnp.max(wlog, axis=1, keepdims=True))
            decay = jnp.exp(total + m_old - m_new)
            w = jnp.exp(wlog - m_new)
            vw = jnp.concatenate([vt.astype(F32) * w, ones * w], axis=0).astype(BF16)
            c_scr[idx] = decay * c_old + _dot(vw, k)
            m_scr[idx] = jnp.broadcast_to(m_new, (8, 128))


def _mlstm(mq, mk, mvT, gl, glT, *, n_lat):
    B, NT, _ = mq.shape
    nt = n_lat + 1
    fwd = lambda t: jnp.where(t == 0, n_lat, t - 1)
    bwd = lambda t: jnp.where(t == 0, n_lat, n_lat - t)
    tok = lambda w, f: pl.BlockSpec((1, TOK, w), lambda b, t: (b, f(t), 0))
    lanes = lambda r, f: pl.BlockSpec((1, r, TOK), lambda b, t: (b, 0, f(t)))
    ins, specs = [], []
    for f in (fwd, bwd):
        ins += [mq, mk, mvT, gl, glT]
        specs += [tok(MP_WIDTH, f)] * 2 + [lanes(MP_WIDTH, f), tok(128, f), lanes(N_GATES, f)]
    return pl.pallas_call(
        _mlstm_kernel,
        out_shape=[jax.ShapeDtypeStruct((B, MP_WIDTH, NT), F32)] * 2,
        grid=(B, nt),
        in_specs=specs,
        out_specs=[lanes(MP_WIDTH, fwd), lanes(MP_WIDTH, bwd)],
        scratch_shapes=[pltpu.VMEM((2 * M_HEADS, M_PAD + 16, M_PAD), F32),
                        pltpu.VMEM((2 * M_HEADS, 8, 128), F32)],
        compiler_params=_cp(("arbitrary", "arbitrary"), VMEM_LIMIT),
        name="mlstm",
    )(*ins)


def _outproj_kernel(x_ref, f_ref, dat_ref, hf_ref, hb_ref, mo_ref, ada_ref, mg_ref, wo_ref, wod_ref, g2_ref, wr_ref,
                    xo_ref, hl_ref, pt_ref, *, is_ctx):
    b = pl.program_id(0)
    mod = ada_ref[CTX_ROW if is_ctx else b]
    gt1, sh2, sc2 = mod[2:3], mod[3:4], mod[4:5]
    mg = mg_ref[...]
    for s in range(x_ref.shape[1] // TOK):
        tk = slice(TOK * s, TOK * (s + 1))
        hs = hf_ref[0, :, tk] + hb_ref[0, :, tk]
        og = mo_ref[0, :, tk].astype(F32)
        parts = [dat_ref[0, :, tk]]
        for hd in range(M_HEADS):
            sl = slice(M_PAD * hd, M_PAD * (hd + 1))
            hh = hs[sl]
            r = lax.rsqrt(jnp.sum(hh * hh, axis=0, keepdims=True) * (1.0 / M_DIM) + EPS)
            parts.append((((hh * r) * mg[sl]) * _sigmoid(og[sl])).astype(BF16))
        mix_t = jnp.concatenate(parts, axis=0)
        upd = _dot(f_ref[0, tk].astype(BF16), wo_ref[...]) + lax.dot_general(
            mix_t, wod_ref[...], (((0,), (0,)), ((), ())), preferred_element_type=F32)
        xn = x_ref[0, tk] + gt1 * upd
        xo_ref[0, tk] = xn
        r = lax.rsqrt(jnp.mean(xn * xn, axis=-1, keepdims=True) + EPS)
        h2 = (xn * r) * g2_ref[...] * (1.0 + sc2) + sh2
        hl_ref[0, tk] = h2.astype(BF16)
        lt = _dot3(h2, wr_ref[...]).T[:N_EXPERTS]
        ex = jnp.exp(lt - jnp.max(lt, axis=0, keepdims=True))
        pt_ref[0, :, tk] = ex / jnp.sum(ex, axis=0, keepdims=True)


def _outproj_kernel_aliased(x_ref, f_ref, dat_ref, hf_ref, hb_ref, mo_ref, ada_ref, mg_ref, wo_ref, wod_ref,
                            g2_ref, wr_ref, hlp_ref, xo_ref, hl_ref, pt_ref, *, is_ctx):
    del hlp_ref
    _outproj_kernel(x_ref, f_ref, dat_ref, hf_ref, hb_ref, mo_ref, ada_ref, mg_ref, wo_ref, wod_ref, g2_ref,
                    wr_ref, xo_ref, hl_ref, pt_ref, is_ctx=is_ctx)


def _outproj(xu, f, daT, hf, hb, mo, ada_l, mg, wo, wod, g2, wrp, hl_prev, *, t0, ntl, is_ctx):
    B, NT, _ = xu.shape
    n = ntl * TOK
    tile = 2 * TOK if n % (2 * TOK) == 0 else TOK
    o = t0 * TOK // tile
    tok = lambda w: pl.BlockSpec((1, tile, w), lambda b, t: (b, o + t, 0))
    trs = lambda r: pl.BlockSpec((1, r, tile), lambda b, t: (b, 0, o + t))
    loc = lambda w: pl.BlockSpec((1, tile, w), lambda b, t: (b, t, 0))
    full = lambda a: pl.BlockSpec(a.shape, lambda b, t: (0,) * a.ndim)
    return pl.pallas_call(
        functools.partial(_outproj_kernel_aliased, is_ctx=is_ctx),
        out_shape=[jax.ShapeDtypeStruct(xu.shape, F32), jax.ShapeDtypeStruct((B, NT, D), BF16),
                   jax.ShapeDtypeStruct((B, N_EXPERTS, n), F32)],
        grid=(B, n // tile),
        in_specs=[tok(D), loc(F_WIDTH), pl.BlockSpec((1, DA_WIDTH, tile), lambda b, t: (b, 0, t)),
                  trs(MP_WIDTH), trs(MP_WIDTH), trs(MP_WIDTH),
                  full(ada_l), full(mg), full(wo), full(wod), full(g2), full(wrp),
                  pl.BlockSpec(memory_space=pl.ANY)],
        out_specs=[tok(D), tok(D), pl.BlockSpec((1, N_EXPERTS, tile), lambda b, t: (b, 0, t))],
        input_output_aliases={0: 0, 12: 1},
        compiler_params=_cp(("arbitrary", "arbitrary"), VMEM_LIMIT),
        name="outproj_norm2_router",
    )(xu, f, daT, hf, hb, mo, ada_l, mg, wo, wod, g2, wrp, hl_prev)


def _select_kernel(p_ref, rank_ref, offs_ref, *, n, cap):
    p = p_ref[0]
    xi = pltpu.bitcast(p, jnp.int32)

    def body(i, lo):
        cand = lo | jnp.left_shift(jnp.int32(1), 30 - i)
        cnt = jnp.sum(jnp.where(xi >= cand, 1.0, 0.0), axis=1, keepdims=True)
        return jnp.where(cnt >= cap, cand, lo)

    thr = lax.fori_loop(0, 31, body, jnp.zeros((N_EXPERTS, 1), jnp.int32))
    nb = n // TOK
    rows = lax.broadcasted_iota(jnp.int32, (n, 128), 0)
    cols = lax.broadcasted_iota(jnp.int32, (n, 128), 1)
    blk_ind = jnp.where((rows // TOK) == cols, 1.0, 0.0).astype(BF16)
    u128 = jnp.where(lax.broadcasted_iota(jnp.int32, (128, 128), 0)
                     < lax.broadcasted_iota(jnp.int32, (128, 128), 1), 1.0, 0.0).astype(BF16)
    utok = jnp.where(lax.broadcasted_iota(jnp.int32, (TOK, TOK), 0)
                     < lax.broadcasted_iota(jnp.int32, (TOK, TOK), 1), 1.0, 0.0).astype(BF16)

    def prefix(mf):
        mb = mf.astype(BF16)
        counts = _dot(mb, blk_ind)
        offs = _dot(counts.astype(BF16), u128)
        pieces = [_dot(mb[:, TOK * j:TOK * (j + 1)], utok) + offs[:, j:j + 1] for j in range(nb)]
        return (jnp.concatenate(pieces, axis=1) if nb > 1 else pieces[0]), offs

    gt = xi > thr
    eq = xi == thr
    need = cap - jnp.sum(jnp.where(gt, 1.0, 0.0), axis=1, keepdims=True)
    rank_eq, _ = prefix(jnp.where(eq, 1.0, 0.0))
    sel = gt | (eq & (rank_eq < need))
    rank, offs = prefix(jnp.where(sel, 1.0, 0.0))
    rank_ref[0] = jnp.where(sel, rank, -1.0)
    offs_ref[0] = offs.astype(jnp.int32)


def _select(pt, *, cap):
    B, _, n = pt.shape
    return pl.pallas_call(
        functools.partial(_select_kernel, n=n, cap=cap),
        out_shape=[jax.ShapeDtypeStruct((B, N_EXPERTS, n), F32),
                   jax.ShapeDtypeStruct((B, N_EXPERTS, 128), jnp.int32)],
        grid=(B,),
        in_specs=[pl.BlockSpec((1, N_EXPERTS, n), lambda b: (b, 0, 0))],
        out_specs=[pl.BlockSpec((1, N_EXPERTS, n), lambda b: (b, 0, 0)),
                   pl.BlockSpec((1, N_EXPERTS, 128), lambda b: (b, 0, 0))],
        compiler_params=_cp(("arbitrary",), VMEM_LIMIT),
        name="expert_choice_select",
    )(pt)


def _gather_kernel(offs_ref, h_ref, rank_ref, prob_ref, o_ref, gate_ref, *, eg, per):
    b, g, tb = pl.program_id(0), pl.program_id(1), pl.program_id(2)

    @pl.when(tb == 0)
    def _():
        o_ref[...] = jnp.zeros(o_ref.shape, BF16)
        gate_ref[...] = jnp.zeros(gate_ref.shape, F32)

    h = h_ref[0]
    ntok = h.shape[0]
    cap_pad = o_ref.shape[2]
    win = min(2 * SLOT, cap_pad)

    def add_rows(i, r, p, base, width):
        slots = lax.broadcasted_iota(jnp.int32, (width, ntok), 0).astype(F32) + base.astype(F32)
        hit = r == slots
        rows = _dot(jnp.where(hit, 1.0, 0.0).astype(BF16), h).astype(BF16)
        o_ref[0, i, pl.ds(base, width), :] = o_ref[0, i, pl.ds(base, width), :] + rows
        gate_ref[0, i, pl.ds(base, width), :] = (gate_ref[0, i, pl.ds(base, width), :]
                                                 + jnp.sum(jnp.where(hit, p, 0.0), axis=1, keepdims=True))

    rs, ps, ends, his = [], [], [], []
    for i in range(eg):
        e = g * eg + i
        r = rank_ref[0, pl.ds(e, 1), :]
        p = prob_ref[0, pl.ds(e, 1), :]
        lo = offs_ref[b, e, tb * per]
        his.append(offs_ref[b, e, (tb + 1) * per])
        base = pl.multiple_of(jnp.minimum((lo // SLOT) * SLOT, cap_pad - win), SLOT)
        add_rows(i, r, p, base, win)
        rs.append(r)
        ps.append(p)
        ends.append(base + win)

    for i in range(eg):
        @pl.when(his[i] > ends[i])
        def _(i=i):
            def body(t, carry):
                add_rows(i, rs[i], ps[i], pl.multiple_of(t * SLOT, SLOT), SLOT)
                return carry

            lax.fori_loop(ends[i] // SLOT, (his[i] - 1) // SLOT + 1, body, 0)


def _gather(offs, hl, rank, pt, *, tb_tok, tb0, n, cap_pad, eg):
    B = hl.shape[0]
    per = tb_tok // TOK
    return pl.pallas_call(
        functools.partial(_gather_kernel, eg=eg, per=per),
        out_shape=[jax.ShapeDtypeStruct((B, N_EXPERTS, cap_pad, D), BF16),
                   jax.ShapeDtypeStruct((B, N_EXPERTS, cap_pad, 1), F32)],
        grid_spec=pltpu.PrefetchScalarGridSpec(
            num_scalar_prefetch=1,
            grid=(B, N_EXPERTS // eg, n // tb_tok),
            in_specs=[pl.BlockSpec((1, tb_tok, D), lambda b, g, t, o: (b, tb0 + t, 0)),
                      pl.BlockSpec((1, N_EXPERTS, tb_tok), lambda b, g, t, o: (b, 0, t)),
                      pl.BlockSpec((1, N_EXPERTS, tb_tok), lambda b, g, t, o: (b, 0, t))],
            out_specs=[pl.BlockSpec((1, eg, cap_pad, D), lambda b, g, t, o: (b, g, 0, 0)),
                       pl.BlockSpec((1, eg, cap_pad, 1), lambda b, g, t, o: (b, g, 0, 0))]),
        compiler_params=_cp(("arbitrary",) * 3, VMEM_LIMIT),
        name="expert_gather",
    )(offs, hl, rank, pt)


FFN_ROWS = 1024


def _ffn_kernel(x_ref, gate_ref, w1_ref, w3_ref, w2_ref, y_ref, acc_ref):
    f = pl.program_id(2)

    @pl.when(f == 0)
    def _():
        acc_ref[...] = jnp.zeros(acc_ref.shape, F32)

    w1 = w1_ref[0, 0].astype(BF16)
    w3 = w3_ref[0, 0].astype(BF16)
    w2 = w2_ref[0, 0].astype(BF16)
    mb, _, cap_pad, _ = x_ref.shape
    rows = min(FFN_ROWS, cap_pad)
    for i in range(mb):
        for r in range(0, cap_pad, rows):
            x = x_ref[i, 0, r:r + rows, :]
            hid = (_silu(_dot(x, w1)) * _dot(x, w3)).astype(BF16)
            acc_ref[i * cap_pad + r:i * cap_pad + r + rows, :] += _dot(hid, w2)

    @pl.when(f == pl.num_programs(2) - 1)
    def _():
        gate = gate_ref[...].reshape(-1, 1)
        y_ref[...] = (acc_ref[...] * gate).astype(BF16).reshape(y_ref.shape)


def _ffn(xs, gates, w1, w3, w2, *, layer, mb, tf):
    B, E, cap_pad, _ = xs.shape
    return pl.pallas_call(
        _ffn_kernel,
        out_shape=jax.ShapeDtypeStruct(xs.shape, BF16),
        grid=(E, B // mb, D_FF // tf),
        in_specs=[pl.BlockSpec((mb, 1, cap_pad, D), lambda e, m, f: (m, e, 0, 0)),
                  pl.BlockSpec((mb, 1, cap_pad, 1), lambda e, m, f: (m, e, 0, 0)),
                  pl.BlockSpec((1, 1, D, tf), lambda e, m, f: (layer, e, 0, f)),
                  pl.BlockSpec((1, 1, D, tf), lambda e, m, f: (layer, e, 0, f)),
                  pl.BlockSpec((1, 1, tf, D), lambda e, m, f: (layer, e, f, 0))],
        out_specs=pl.BlockSpec((mb, 1, cap_pad, D), lambda e, m, f: (m, e, 0, 0)),
        scratch_shapes=[pltpu.VMEM((mb * cap_pad, D), F32)],
        compiler_params=_cp(("arbitrary",) * 3, VMEM_LIMIT),
        name="expert_ffn",
    )(xs, gates, w1, w3, w2)


CCOL = 512


def _combine_kernel(offs_ref, x_ref, y_ref, rankc_ref, ada_ref, o_ref, tot_scr, *, per, is_ctx):
    b, tb = pl.program_id(0), pl.program_id(2)
    gt2 = ada_ref[CTX_ROW if is_ctx else b][5:6]
    rc_all = rankc_ref[0]
    cap_pad = y_ref.shape[2]
    half = SLOT // 2
    slot = lax.broadcasted_iota(jnp.int32, (1, SLOT), 1).astype(F32)

    ends, his = [], []
    total = jnp.zeros(tot_scr.shape, F32)
    for e0 in range(0, N_EXPERTS, 2):
        hots, rows = [], []
        for e in (e0, e0 + 1):
            lo = offs_ref[b, e, tb * per]
            his.append(offs_ref[b, e, (tb + 1) * per])
            base = pl.multiple_of(jnp.minimum((lo // half) * half, cap_pad - SLOT), half)
            ends.append(base + SLOT)
            hots.append(jnp.where(rc_all[:, e:e + 1] == slot + base.astype(F32), 1.0, 0.0).astype(BF16))
            rows.append(y_ref[0, e, pl.ds(base, SLOT), :])
        total = total + _dot(jnp.concatenate(hots, axis=1), jnp.concatenate(rows, axis=0))
    tot_scr[...] = total

    for e in range(N_EXPERTS):
        @pl.when(his[e] > ends[e])
        def _(e=e):
            rc = rc_all[:, e:e + 1]

            def body(t, carry):
                start = ends[e] + t * SLOT
                base = pl.multiple_of(jnp.minimum(start, cap_pad - SLOT), half)
                hit = (rc == slot + base.astype(F32)) & (rc >= start.astype(F32))
                tot_scr[...] += _dot(jnp.where(hit, 1.0, 0.0).astype(BF16), y_ref[0, e, pl.ds(base, SLOT), :])
                return carry

            lax.fori_loop(0, (his[e] - ends[e] + SLOT - 1) // SLOT, body, 0)

    o_ref[0] = x_ref[0] + gt2 * tot_scr[...]


def _combine(offs, xu, ys, rank_c, ada_l, *, tb_tok, tb0, n, is_ctx):
    B = xu.shape[0]
    cap_pad = ys.shape[2]
    per = tb_tok // TOK
    return pl.pallas_call(
        functools.partial(_combine_kernel, per=per, is_ctx=is_ctx),
        out_shape=jax.ShapeDtypeStruct(xu.shape, F32),
        grid_spec=pltpu.PrefetchScalarGridSpec(
            num_scalar_prefetch=1,
            grid=(B, D // CCOL, n // tb_tok),
            in_specs=[pl.BlockSpec((1, tb_tok, CCOL), lambda b, c, t, o: (b, tb0 + t, c)),
                      pl.BlockSpec((1, N_EXPERTS, cap_pad, CCOL), lambda b, c, t, o: (b, 0, 0, c),
                                   pipeline_mode=pl.Buffered(1)),
                      pl.BlockSpec((1, tb_tok, N_EXPERTS), lambda b, c, t, o: (b, t, 0)),
                      pl.BlockSpec((ADA_ROWS, ADA_CHUNKS, CCOL), lambda b, c, t, o: (0, 0, c))],
            out_specs=pl.BlockSpec((1, tb_tok, CCOL), lambda b, c, t, o: (b, tb0 + t, c)),
            scratch_shapes=[pltpu.VMEM((tb_tok, CCOL), F32)]),
        input_output_aliases={1: 0},
        compiler_params=_cp(("arbitrary",) * 3, VMEM_LIMIT),
        name="expert_combine",
    )(offs, xu, ys, rank_c, ada_l)


def _moe(xu, hl, pt, ada_l, w1, w3, w2, *, layer, row0, is_ctx):
    B, _, n = pt.shape
    cap = EC_CAPACITY * n // N_EXPERTS
    cap_pad = -(-cap // SLOT) * SLOT
    nb = n // TOK
    rank, offs = _select(pt, cap=cap)
    offs = offs[:, :, :nb + 1]
    gt = min(n, MOE_TOK)
    ct = min(n, COMBINE_TOK)
    xs, gates = _gather(offs, hl, rank, pt, tb_tok=gt, tb0=row0 // gt, n=n, cap_pad=cap_pad, eg=GATHER_EXPERTS)
    mb = 2 if (B % 2 == 0 and cap_pad >= 1024) else (B if cap_pad < 1024 else 1)
    ys = _ffn(xs, gates, w1, w3, w2, layer=layer, mb=mb, tf=FFN_TF)
    rank_c = jnp.swapaxes(rank, 1, 2)
    return _combine(offs, xu, ys, rank_c, ada_l, tb_tok=ct, tb0=row0 // ct, n=n, is_ctx=is_ctx)


def _final_kernel(x_ref, g_ref, o_ref):
    x = x_ref[0]
    r = lax.rsqrt(jnp.mean(x * x, axis=-1, keepdims=True) + EPS)
    o_ref[0] = (x * r) * g_ref[...]


def _final_norm(xu, g, *, n):
    B = xu.shape[0]
    tm = MOE_TOK
    return pl.pallas_call(
        _final_kernel,
        out_shape=jax.ShapeDtypeStruct((B, n, D), F32),
        grid=(B, n // tm),
        in_specs=[pl.BlockSpec((1, tm, D), lambda b, t: (b, t, 0)),
                  pl.BlockSpec((1, D), lambda b, t: (0, 0))],
        out_specs=pl.BlockSpec((1, tm, D), lambda b, t: (b, t, 0)),
        compiler_params=_cp(("arbitrary", "arbitrary")),
        name="final_norm",
    )(xu, g)


def _rope_tables(n, ctx):
    rows = n // GRID_W
    t_row = jnp.repeat(jnp.arange(rows), GRID_W)
    t_col = jnp.tile(jnp.arange(GRID_W), rows)
    nf = DA_DIM // 4
    inv = ROPE_THETA ** (-jnp.arange(nf, dtype=F32) / nf)
    ar = t_row[:, None].astype(F32) * inv
    ac = t_col[:, None].astype(F32) * inv
    ang = jnp.concatenate([ar, ar, ac, ac], axis=-1)
    sign = jnp.where((jnp.arange(DA_DIM) % 16) < 8, -1.0, 1.0).astype(F32)
    cos = jnp.concatenate([jnp.cos(ang), jnp.ones((ctx, DA_DIM), F32)], axis=0)
    sin = jnp.concatenate([jnp.sin(ang) * sign, jnp.zeros((ctx, DA_DIM), F32)], axis=0)
    return jnp.tile(cos, (1, 128 // DA_DIM)), jnp.tile(sin, (1, 128 // DA_DIM))


def _pad_heads_cols(w):
    lead = w.shape[:-1]
    w = w.reshape(lead + (M_HEADS, M_DIM))
    w = jnp.pad(w, [(0, 0)] * len(lead) + [(0, 0), (0, M_PAD - M_DIM)])
    return w.reshape(lead + (MP_WIDTH,))


def _kv_tile(nt):
    for parts in range(1, nt // LANES + 1):
        if nt % parts == 0 and (nt // parts) % LANES == 0 and nt // parts <= KV_TILE_MAX:
            return nt // parts
    raise ValueError(nt)


def kernel(x, c, ctx, c_ctx, ada_w, ada_b, norm1_g, norm2_g, w_in, four_w, m_conv_w, m_conv_b, m_gate_b,
           m_norm_g, d_lam, d_norm_g, w_out, router_w, exp_w1, exp_w3, exp_w2, final_g):
    B, N, _ = x.shape
    CTX = ctx.shape[1]
    depth = w_in.shape[0]
    assert CTX == TOK and N % (FFT_N1 * TOK) == 0 and N % Q_TILE == 0 and B <= CTX_ROW
    NT = N + CTX
    PAD = -NT % MOE_TOK
    n_lat = N // TOK
    n2 = N // FFT_N1

    xu = jnp.concatenate([x, ctx, jnp.zeros((B, PAD, D), F32)], axis=1)
    cvecs = jnp.zeros((ADA_ROWS, D), F32).at[:B].set(c).at[CTX_ROW].set(c_ctx)
    ada = _adaln(cvecs, ada_w, ada_b).reshape(depth, ADA_ROWS, ADA_CHUNKS, D)
    cos_t, sin_t = _rope_tables(N, CTX + PAD)
    tabs = _fourier_tables(N, CTX)
    tk = _kv_tile(NT)
    tq = Q_TILE

    hl = jnp.zeros((B, NT + PAD, D), BF16)
    for layer in range(depth):
        ctx_out = layer < depth - 1
        lam_init = 0.8 - 0.6 * math.exp(-0.3 * layer)
        w = w_in[layer]
        wm = jnp.concatenate([w[:, OFF_F:OFF_DQ], w[:, OFF_DQ:OFF_MO], w[:, OFF_DK:OFF_DV]], axis=1).astype(BF16)
        wvt = jnp.concatenate([w[:, OFF_DV:OFF_MV], _pad_heads_cols(w[:, OFF_MO:OFF_MQ]),
                               _pad_heads_cols(w[:, OFF_MV:OFF_G])], axis=1).T.astype(BF16)
        wg = jnp.pad(w[:, OFF_G:], ((0, 0), (0, LANES - N_GATES)))
        wc = jnp.concatenate([_pad_heads_cols(w[:, OFF_MQ:OFF_MK]), _pad_heads_cols(w[:, OFF_MK:OFF_DK])],
                             axis=1).astype(BF16)
        gb = jnp.pad(m_gate_b[layer], (0, LANES - N_GATES)).reshape(1, LANES)
        cw = jnp.concatenate([_pad_heads_cols(m_conv_w[layer][:, :M_WIDTH]),
                              _pad_heads_cols(m_conv_w[layer][:, M_WIDTH:])], axis=1)
        cb = jnp.concatenate([_pad_heads_cols(m_conv_b[layer][:M_WIDTH]),
                              _pad_heads_cols(m_conv_b[layer][M_WIDTH:])]).reshape(1, 2 * MP_WIDTH)
        ada_l = ada[layer]

        y4, dq, dk, dvT, mo, mq, mk, mv, gl, glT = _inproj(
            xu, ada_l, norm1_g[layer].reshape(1, D), wm, wc, wvt, wg, gb, tabs["cs"], cos_t, sin_t, cw, cb,
            n_lat=n_lat, n2=n2)

        wblk = jnp.zeros((F_WIDTH, F_WIDTH), F32)
        for g in range(F_GROUPS):
            wblk = wblk.at[F_GDIM * g:F_GDIM * (g + 1), F_GDIM * g:F_GDIM * (g + 1)].set(four_w[layer, g])
        f_l, f_c = _fourier(y4, tabs, wblk.astype(BF16), n=N, ctx=CTX, with_ctx=ctx_out)

        dlam = d_lam[layer]
        g2 = d_norm_g[layer].reshape(DA_VDIM, 1)
        da_l = _attention(dq, dk, dvT, dlam, g2, lam_init=lam_init, tq=tq, q0=0, nq=N // tq,
                          tk=tk, k0=0, nk=NT // tk)

        hf, hb = _mlstm(mq, mk, mv, gl, glT, n_lat=n_lat)

        mg = _pad_heads_cols(m_norm_g[layer]).reshape(MP_WIDTH, 1)
        wol = w_out[layer]
        wo = wol[:F_WIDTH].astype(BF16)
        wod = jnp.concatenate([wol[F_WIDTH:F_WIDTH + DA_WIDTH],
                               jnp.pad(wol[F_WIDTH + DA_WIDTH:].reshape(M_HEADS, M_DIM, D),
                                       ((0, 0), (0, M_PAD - M_DIM), (0, 0))).reshape(MP_WIDTH, D)],
                              axis=0).astype(BF16)
        g2n = norm2_g[layer].reshape(1, D)
        wrp = jnp.pad(router_w[layer], ((0, 0), (0, LANES - N_EXPERTS)))
        xu, hl, pt_l = _outproj(xu, f_l, da_l, hf, hb, mo, ada_l, mg, wo, wod, g2n, wrp, hl,
                                t0=0, ntl=n_lat, is_ctx=False)
        if ctx_out:
            da_c = _attention(dq, dk, dvT, dlam, g2, lam_init=lam_init, tq=TOK, q0=n_lat, nq=1,
                              tk=TOK, k0=n_lat, nk=1)
            xu, hl, pt_c = _outproj(xu, f_c, da_c, hf, hb, mo, ada_l, mg, wo, wod, g2n, wrp, hl,
                                    t0=n_lat, ntl=1, is_ctx=True)

        xu = _moe(xu, hl, pt_l, ada_l, exp_w1, exp_w3, exp_w2, layer=layer, row0=0, is_ctx=False)
        if ctx_out:
            xu = _moe(xu, hl, pt_c, ada_l, exp_w1, exp_w3, exp_w2, layer=layer, row0=N, is_ctx=True)

    return _final_norm(xu, final_g.reshape(1, D), n=N)
```

```python
import functools
import math

import numpy as np
import jax
import jax.numpy as jnp
from jax import lax
from jax.experimental import pallas as pl
from jax.experimental.pallas import tpu as pltpu

F32 = jnp.float32
BF16 = jnp.bfloat16
HI = lax.Precision.HIGHEST

D = 1024
EPS = 1e-6
GRID_W = 64
ROPE_THETA = 10000.0
F_GROUPS, F_GDIM = 4, 64
F_WIDTH = F_GROUPS * F_GDIM
DA_HEADS, DA_DIM = 6, 32
DA_VDIM = 2 * DA_DIM
DA_WIDTH = DA_HEADS * DA_VDIM
M_HEADS, M_DIM = 4, 96
M_WIDTH = M_HEADS * M_DIM
M_PAD = 128
MP_WIDTH = M_HEADS * M_PAD
N_GATES = 4 * M_HEADS
N_EXPERTS = 16
EC_CAPACITY = 2
D_FF = 2 * D
ADA_CHUNKS = 6
ADA_ROWS = 8
CTX_ROW = 4

LANES = 128
MXU_DIM = 256
V7X_VMEM_BYTES = 64 * 1024 * 1024

TOK = MXU_DIM
FFT_N1 = 16
SLOT = LANES
Q_TILE = 2048
KV_TILE_MAX = 1408
MOE_TOK = 512
COMBINE_TOK = 512
FFN_TF = 512
GATHER_EXPERTS = 4
NEG = -1e30

OFF_F = 0
OFF_DQ = OFF_F + F_WIDTH
OFF_MO = OFF_DQ + 2 * DA_HEADS * DA_DIM
OFF_MQ = OFF_MO + M_WIDTH
OFF_MK = OFF_MQ + M_WIDTH
OFF_DK = OFF_MK + M_WIDTH
OFF_DV = OFF_DK + 2 * DA_HEADS * DA_DIM
OFF_MV = OFF_DV + DA_HEADS * DA_VDIM
OFF_G = OFF_MV + M_WIDTH

VMEM_LIMIT = V7X_VMEM_BYTES * 7 // 8


def _cp(sem, vmem=None):
    return pltpu.CompilerParams(dimension_semantics=sem, vmem_limit_bytes=vmem)


def _sigmoid(x):
    return 1.0 / (1.0 + jnp.exp(-x))


def _silu(x):
    return x * _sigmoid(x)


def _dot(a, b, precision=None):
    return jnp.dot(a, b, preferred_element_type=F32, precision=precision)


def _split(a):
    hi = a.astype(BF16)
    return hi, (a - hi.astype(F32)).astype(BF16)


def _dot3(a, b):
    a_hi, a_lo = a if isinstance(a, tuple) else _split(a)
    b_hi, b_lo = b if isinstance(b, tuple) else _split(b)
    return _dot(a_hi, b_hi) + _dot(a_hi, b_lo) + _dot(a_lo, b_hi)


def _split3(a):
    hi = a.astype(BF16)
    r = a - hi.astype(F32)
    mid = r.astype(BF16)
    return hi, mid, (r - mid.astype(F32)).astype(BF16)


def _dot_nt(a, b, precision=None):
    return lax.dot_general(a, b, (((1,), (1,)), ((), ())), preferred_element_type=F32,
                           precision=precision)


def _ada_kernel(c_ref, w_ref, b_ref, o_ref):
    c = c_ref[...]
    o_ref[0] = _dot(_silu(c), w_ref[0], HI) + b_ref[0]


def _adaln(cvecs, ada_w, ada_b):
    depth = ada_w.shape[0]
    tn = 1536
    return pl.pallas_call(
        _ada_kernel,
        out_shape=jax.ShapeDtypeStruct((depth, ADA_ROWS, ADA_CHUNKS * D), F32),
        grid=(depth, ADA_CHUNKS * D // tn),
        in_specs=[pl.BlockSpec((ADA_ROWS, D), lambda l, j: (0, 0)),
                  pl.BlockSpec((1, D, tn), lambda l, j: (l, 0, j)),
                  pl.BlockSpec((1, 1, tn), lambda l, j: (l, 0, j))],
        out_specs=pl.BlockSpec((1, 8, tn), lambda l, j: (l, 0, j)),
        compiler_params=_cp(("arbitrary", "arbitrary")),
        name="adaln",
    )(cvecs, ada_w, ada_b.reshape(depth, 1, ADA_CHUNKS * D))


def _inproj_kernel(x_ref, xp_ref, xn_ref, ada_ref, g_ref, wm_ref, wc_ref, wvt_ref, wg_ref, gb_ref, cs_ref,
                   cos_ref, sin_ref, cw_ref, cb_ref,
                   y_ref, dq_ref, dk_ref, dvt_ref, mo_ref, mq_ref, mk_ref, mv_ref, gl_ref, glt_ref, *, n_lat):
    b = pl.program_id(0)
    t = pl.program_id(1)
    n_tiles = pl.num_programs(1)
    is_ctx = t >= n_lat
    row = jnp.where(is_ctx, CTX_ROW, b)
    mod = ada_ref[row]
    sh, sc = mod[0:1], mod[1:2]

    xa = jnp.concatenate([xp_ref[0], x_ref[0], xn_ref[0]], axis=0)
    r = lax.rsqrt(jnp.mean(xa * xa, axis=-1, keepdims=True) + EPS)
    ha = (xa * r) * g_ref[...] * (1.0 + sc) + sh
    h = ha[8:8 + TOK]
    hb = h.astype(BF16)

    pm = _dot(hb, wm_ref[...])
    o = 0
    pf = pm[:, o:o + F_WIDTH]; o += F_WIDTH
    q = pm[:, o:o + DA_WIDTH]; o += DA_WIDTH
    k = pm[:, o:o + DA_WIDTH]; o += DA_WIDTH
    pt = _dot_nt(wvt_ref[...], hb)
    dvt_ref[0] = pt[:DA_WIDTH].astype(BF16)
    mo_ref[0] = pt[DA_WIDTH:DA_WIDTH + MP_WIDTH].astype(BF16)
    mv_ref[0] = pt[DA_WIDTH + MP_WIDTH:].astype(BF16)

    y_ref[0, 0] = _dot3(pf, cs_ref[...])

    cos = cos_ref[...]
    sin = sin_ref[...]
    lane = lax.broadcasted_iota(jnp.int32, (1, 128), 1)
    low = (lane % 16) < 8

    def rope(z):
        parts = []
        for c in range(DA_WIDTH // 128):
            zc = z[:, 128 * c:128 * (c + 1)]
            rot = jnp.where(low, pltpu.roll(zc, 120, 1), pltpu.roll(zc, 8, 1))
            parts.append(zc * cos + rot * sin)
        return jnp.concatenate(parts, axis=1)

    dq_ref[0] = (rope(q) * (DA_DIM ** -0.5 * math.log2(math.e))).astype(BF16)
    dk_ref[0] = rope(k).astype(BF16)

    gpre = _dot3(h, wg_ref[...]) + gb_ref[...]
    is_forget = (lax.broadcasted_iota(jnp.int32, (1, LANES), 1) % 8) >= 4
    logsig = jnp.minimum(gpre, 0.0) - jnp.log(1.0 + jnp.exp(-jnp.abs(gpre)))
    gl = jnp.where(is_forget, logsig, gpre)
    gl_ref[0] = gl
    glt_ref[0] = gl.T[:N_GATES]

    pc = _dot(ha.astype(BF16), wc_ref[...])
    first = (t == 0) | (t == n_lat)
    last = (t == n_lat - 1) | (t == n_tiles - 1)
    ridx = lax.broadcasted_iota(jnp.int32, (TOK + 16, 1), 0)
    pc = jnp.where(((ridx < 8) & first) | ((ridx >= TOK + 8) & last), 0.0, pc)
    cw = cw_ref[...]
    conv = cb_ref[...] + pc[7:7 + TOK] * cw[0:1] + pc[8:8 + TOK] * cw[1:2] + pc[9:9 + TOK] * cw[2:3]
    act = _silu(conv)
    mq_ref[0] = act[:, :MP_WIDTH].astype(BF16)
    mk_ref[0] = (act[:, MP_WIDTH:] * (M_DIM ** -0.5)).astype(BF16)


def _inproj(xu, ada_l, g1, wm, wc, wvt, wg, gb, cs, cos_t, sin_t, cw, cb, *, n_lat, n2):
    B, NT, _ = xu.shape
    nt = n_lat + 1
    rper = n2 // TOK
    tok3 = lambda w: pl.BlockSpec((1, TOK, w), lambda b, t: (b, t, 0))
    full = lambda a: pl.BlockSpec(a.shape, lambda b, t: (0,) * a.ndim)
    nb8 = NT // 8
    outs = [jax.ShapeDtypeStruct((B, 2 * FFT_N1, n2, 2 * F_WIDTH), F32)]
    nq = -(-NT // Q_TILE) * Q_TILE
    outs += [jax.ShapeDtypeStruct((B, nq, DA_WIDTH), BF16), jax.ShapeDtypeStruct((B, nt * TOK, DA_WIDTH), BF16)]
    outs += [jax.ShapeDtypeStruct((B, DA_WIDTH, nt * TOK), BF16)]
    trs = lambda r: pl.BlockSpec((1, r, TOK), lambda b, t: (b, 0, t))
    trp = jax.ShapeDtypeStruct((B, MP_WIDTH, NT), BF16)
    outs += [trp, jax.ShapeDtypeStruct((B, NT, MP_WIDTH), BF16), jax.ShapeDtypeStruct((B, NT, MP_WIDTH), BF16), trp]
    outs += [jax.ShapeDtypeStruct((B, NT, 128), F32), jax.ShapeDtypeStruct((B, N_GATES, nt * TOK), F32)]
    out_specs = [pl.BlockSpec((1, 1, TOK, 2 * F_WIDTH), lambda b, t: (b, t // rper, t % rper, 0))]
    out_specs += [tok3(DA_WIDTH)] * 2 + [trs(DA_WIDTH)]
    out_specs += [trs(MP_WIDTH), tok3(MP_WIDTH), tok3(MP_WIDTH), trs(MP_WIDTH)]
    out_specs += [tok3(128), pl.BlockSpec((1, N_GATES, TOK), lambda b, t: (b, 0, t))]
    return pl.pallas_call(
        functools.partial(_inproj_kernel, n_lat=n_lat),
        out_shape=outs,
        grid=(B, nt),
        in_specs=[tok3(D),
                  pl.BlockSpec((1, 8, D), lambda b, t: (b, jnp.maximum(t * (TOK // 8) - 1, 0), 0)),
                  pl.BlockSpec((1, 8, D), lambda b, t: (b, jnp.minimum((t + 1) * (TOK // 8), nb8 - 1), 0)),
                  full(ada_l), full(g1), full(wm), full(wc), full(wvt), full(wg), full(gb), full(cs),
                  pl.BlockSpec((TOK, LANES), lambda b, t: (t, 0)),
                  pl.BlockSpec((TOK, LANES), lambda b, t: (t, 0)),
                  full(cw), full(cb)],
        out_specs=out_specs,
        compiler_params=_cp(("arbitrary", "arbitrary"), VMEM_LIMIT),
        name="norm1_inproj",
    )(xu, xu, xu, ada_l, g1, wm, wc, wvt, wg, gb, cs, cos_t, sin_t, cw, cb)


def _fft1_kernel(y_ref, kc_ref, ks_ref, tc_ref, ts_ref, o_ref, *, groups):
    kc = _split(kc_ref[...])
    ks = _split(ks_ref[...])
    for g in range(groups):
        blk = _split(y_ref[0, :, 8 * g:8 * (g + 1), :].reshape(FFT_N1 * 8, 2 * F_WIDTH))
        p = _dot3(kc, blk)
        q = _dot3(ks, blk)
        ar = p[:, :F_WIDTH] - q[:, F_WIDTH:]
        ai = -p[:, F_WIDTH:] - q[:, :F_WIDTH]
        tc = tc_ref[128 * g:128 * (g + 1), :]
        ts = ts_ref[128 * g:128 * (g + 1), :]
        tc = jnp.concatenate([tc, tc], axis=1)
        ts = jnp.concatenate([ts, ts], axis=1)
        br = ar * tc + ai * ts
        bi = ai * tc - ar * ts
        o_ref[0, :, 8 * g:8 * (g + 1), :] = jnp.concatenate([br, bi], axis=1).reshape(FFT_N1, 8, 2 * F_WIDTH)


def _fft2_kernel(b_ref, c2_ref, s2_ref, wb_ref, perm_ref, o_ref, r_scr, *, n2):
    c2 = _split(c2_ref[...])
    s2 = _split(s2_ref[...])
    for i in range(8):
        blk = b_ref[0, i]
        xr = _dot3(c2, blk[:, :F_WIDTH]) + _dot3(s2, blk[:, F_WIDTH:])
        r_scr[i] = _dot(xr.astype(BF16), wb_ref[...]).astype(BF16)
    for t in range(n2 // 32):
        rows = jnp.concatenate([r_scr[i, 32 * t:32 * (t + 1), :] for i in range(8)], axis=0)
        o_ref[0, 32 * t:32 * (t + 1), :, :] = _dot(perm_ref[...], rows).reshape(32, 8, F_WIDTH)


def _fftc_kernel(y_ref, c_ref, s_ref, wb_ref, o_ref):
    y = y_ref[0, 0]
    z = _dot3(c_ref[...], y[:, :F_WIDTH]) - _dot3(s_ref[...], y[:, F_WIDTH:])
    o_ref[0] = _dot(z.astype(BF16), wb_ref[...])


def _fourier_tables(n, ctx):
    n1, n2 = FFT_N1, n // FFT_N1
    a = np.arange(n1)
    ang1 = 2 * np.pi * np.outer(a, a) / n1
    eye8 = np.eye(8)
    kc = np.kron(np.cos(ang1), eye8)
    ks = np.kron(np.sin(ang1), eye8)
    n2i = np.arange(n2).reshape(n2 // 8, 1, 8)
    k1 = np.arange(n1).reshape(1, n1, 1)
    angt = (2 * np.pi * n2i * k1 / n).reshape(-1, 1)
    tc = np.broadcast_to(np.cos(angt), (n2 // 8 * 128, 128))
    ts = np.broadcast_to(np.sin(angt), (n2 // 8 * 128, 128))
    b = np.arange(n2)
    ang2 = 2 * np.pi * np.outer(b, b) / n2
    c2 = np.cos(ang2) / math.sqrt(n)
    s2 = np.sin(ang2) / math.sqrt(n)
    perm = np.zeros((256, 256))
    for kk in range(8):
        for j in range(32):
            perm[j * 8 + kk, kk * 32 + j] = 1.0
    cc = np.arange(ctx)
    angc = 2 * np.pi * np.outer(cc, cc) / ctx
    cctx = np.cos(angc) / math.sqrt(ctx)
    sctx = np.sin(angc) / math.sqrt(ctx)
    ch = np.arange(F_GDIM)
    angch = 2 * np.pi * np.outer(ch, ch) / F_GDIM
    cs = np.concatenate([np.kron(np.eye(F_GROUPS), np.cos(angch)),
                         np.kron(np.eye(F_GROUPS), np.sin(angch))], axis=1) / math.sqrt(F_GDIM)
    f = lambda z: jnp.asarray(np.ascontiguousarray(z), dtype=F32)
    return dict(kc=f(kc), ks=f(ks), tc=f(tc), ts=f(ts), c2=f(c2), s2=f(s2), perm=f(perm).astype(BF16),
                cctx=f(cctx), sctx=f(sctx), cs=f(cs))


def _fourier(y4, tabs, wblk, *, n, ctx, with_ctx):
    B = y4.shape[0]
    n2 = n // FFT_N1
    groups = 4
    full = lambda a, nd: pl.BlockSpec(a.shape, lambda *i: (0,) * a.ndim)
    b4 = pl.pallas_call(
        functools.partial(_fft1_kernel, groups=groups),
        out_shape=jax.ShapeDtypeStruct((B, FFT_N1, n2, 2 * F_WIDTH), F32),
        grid=(B, n2 // (8 * groups)),
        in_specs=[pl.BlockSpec((1, FFT_N1, 8 * groups, 2 * F_WIDTH), lambda b, j: (b, 0, j, 0)),
                  full(tabs["kc"], 2), full(tabs["ks"], 2),
                  pl.BlockSpec((128 * groups, 128), lambda b, j: (j, 0)),
                  pl.BlockSpec((128 * groups, 128), lambda b, j: (j, 0))],
        out_specs=pl.BlockSpec((1, FFT_N1, 8 * groups, 2 * F_WIDTH), lambda b, j: (b, 0, j, 0)),
        compiler_params=_cp(("arbitrary", "arbitrary")),
        name="fourier_stage1",
    )(y4, tabs["kc"], tabs["ks"], tabs["tc"], tabs["ts"])
    f4 = pl.pallas_call(
        functools.partial(_fft2_kernel, n2=n2),
        out_shape=jax.ShapeDtypeStruct((B, n2, 16, F_WIDTH), F32),
        grid=(B, FFT_N1 // 8),
        in_specs=[pl.BlockSpec((1, 8, n2, 2 * F_WIDTH), lambda b, j: (b, j, 0, 0)),
                  full(tabs["c2"], 2), full(tabs["s2"], 2), full(wblk, 2), full(tabs["perm"], 2)],
        out_specs=pl.BlockSpec((1, n2, 8, F_WIDTH), lambda b, j: (b, 0, j, 0)),
        scratch_shapes=[pltpu.VMEM((8, n2, F_WIDTH), BF16)],
        compiler_params=_cp(("arbitrary", "arbitrary"), VMEM_LIMIT),
        name="fourier_stage2",
    )(b4, tabs["c2"], tabs["s2"], wblk, tabs["perm"])
    f_ctx = None
    if with_ctx:
        f_ctx = pl.pallas_call(
            _fftc_kernel,
            out_shape=jax.ShapeDtypeStruct((B, ctx, F_WIDTH), F32),
            grid=(B,),
            in_specs=[pl.BlockSpec((1, 1, TOK, 2 * F_WIDTH), lambda b: (b, FFT_N1, 0, 0)),
                      full(tabs["cctx"], 1), full(tabs["sctx"], 1), full(wblk, 1)],
            out_specs=pl.BlockSpec((1, ctx, F_WIDTH), lambda b: (b, 0, 0)),
            compiler_params=_cp(("arbitrary",)),
            name="fourier_ctx",
        )(y4, tabs["cctx"], tabs["sctx"], wblk)
    return f4.reshape(B, n, F_WIDTH), f_ctx


VROWS = DA_VDIM + 16


def _attn_kernel(q_ref, k_ref, vt_ref, dl_ref, g_ref, o_ref, m_scr, acc_scr, *, lam_init):
    kt = pl.program_id(3)
    nk = pl.num_programs(3)

    @pl.when(kt == 0)
    def _():
        m_scr[...] = jnp.full(m_scr.shape, NEG, F32)
        acc_scr[...] = jnp.zeros(acc_scr.shape, F32)

    q = q_ref[0]
    k = k_ref[0]
    vt = vt_ref[0]
    ones = jnp.ones((16, vt.shape[1]), BF16)
    lhs = [jnp.concatenate([vt[DA_VDIM * h:DA_VDIM * (h + 1)], ones], axis=0) for h in range(2)]
    lane = lax.broadcasted_iota(jnp.int32, (1, LANES), 1)
    zero = jnp.zeros((), BF16)

    def scores(j):
        return _dot_nt(k, jnp.where((lane // DA_DIM) == j, q, zero))

    st_next = scores(0)
    for j in range(4):
        st = st_next
        if j < 3:
            st_next = scores(j + 1)
        m_old = m_scr[j]
        m_new = jnp.maximum(m_old, jnp.max(st, axis=0, keepdims=True))
        alpha = jnp.exp2(m_old - m_new)
        pt = jnp.exp2(st - m_new).astype(BF16)
        acc_scr[j] = alpha * acc_scr[j] + _dot(lhs[j // 2], pt)
        m_scr[j] = m_new

    @pl.when(kt == nk - 1)
    def _():
        dl = dl_ref[...]
        lam = (jnp.exp(jnp.sum(dl[0:1] * dl[1:2], keepdims=True))
               - jnp.exp(jnp.sum(dl[2:3] * dl[3:4], keepdims=True)) + lam_init)
        outs = []
        for h in range(2):
            a0 = acc_scr[2 * h]
            a1 = acc_scr[2 * h + 1]
            o = (a0[:DA_VDIM] / a0[DA_VDIM:DA_VDIM + 1]
                 - lam * (a1[:DA_VDIM] / a1[DA_VDIM:DA_VDIM + 1]))
            r = lax.rsqrt(jnp.mean(o * o, axis=0, keepdims=True) + EPS)
            outs.append(((o * r) * g_ref[...]) * (1.0 - lam_init))
        o_ref[0] = jnp.concatenate(outs, axis=0).astype(BF16)


def _attention(dq, dk, dvT, dlam, gcol, *, lam_init, tq, q0, nq, tk, k0, nk):
    B = dq.shape[0]
    return pl.pallas_call(
        functools.partial(_attn_kernel, lam_init=lam_init),
        out_shape=jax.ShapeDtypeStruct((B, DA_WIDTH, nq * tq), BF16),
        grid=(B, DA_WIDTH // 128, nq, nk),
        in_specs=[pl.BlockSpec((1, tq, 128), lambda b, p, i, j: (b, q0 + i, p)),
                  pl.BlockSpec((1, tk, 128), lambda b, p, i, j: (b, k0 + j, p)),
                  pl.BlockSpec((1, 128, tk), lambda b, p, i, j: (b, p, k0 + j)),
                  pl.BlockSpec(dlam.shape, lambda b, p, i, j: (0, 0)),
                  pl.BlockSpec(gcol.shape, lambda b, p, i, j: (0, 0))],
        out_specs=pl.BlockSpec((1, 128, tq), lambda b, p, i, j: (b, p, i)),
        scratch_shapes=[pltpu.VMEM((4, 1, tq), F32), pltpu.VMEM((4, VROWS, tq), F32)],
        compiler_params=_cp(("arbitrary",) * 4, VMEM_LIMIT),
        name="diff_attention",
    )(dq, dk, dvT, dlam, gcol)


def _mlstm_kernel(qf_ref, kf_ref, vf_ref, gcf_ref, grf_ref, qb_ref, kb_ref, vb_ref, gcb_ref, grb_ref,
                  hf_ref, hb_ref, c_scr, m_scr):
    t = pl.program_id(1)

    @pl.when(t == 0)
    def _():
        c_scr[...] = jnp.zeros(c_scr.shape, F32)
        m_scr[...] = jnp.zeros(m_scr.shape, F32)

    L = TOK
    si = lax.broadcasted_iota(jnp.int32, (L, L), 0)
    li = lax.broadcasted_iota(jnp.int32, (L, L), 1)
    dirs = ((qf_ref, kf_ref, vf_ref, gcf_ref, grf_ref, hf_ref, si <= li, li <= si, L - 1),
            (qb_ref, kb_ref, vb_ref, gcb_ref, grb_ref, hb_ref, si >= li, li >= si, 0))
    ones = jnp.ones((16, L), F32)
    for d, (q_ref, k_ref, vt_ref, gc_ref, gr_ref, h_ref, seen, seen_t, last) in enumerate(dirs):
        gc = gc_ref[0]
        gr = gr_ref[0]
        seen_b = jnp.where(seen, 1.0, 0.0).astype(BF16)
        seen_tb = jnp.where(seen_t, 1.0, 0.0).astype(BF16)
        bcols = sum(_dot(seen_tb, piece) for piece in _split3(gc))
        brows = sum(_dot(piece, seen_b) for piece in _split3(gr))
        for hd in range(M_HEADS):
            idx = d * M_HEADS + hd
            ji = d * 8 + hd
            jf = d * 8 + 4 + hd
            sl = slice(M_PAD * hd, M_PAD * (hd + 1))
            q = q_ref[0, :, sl]
            k = k_ref[0, :, sl]
            vt = vt_ref[0, sl, :]
            b_row = brows[jf:jf + 1, :]
            cs = gc[:, ji:ji + 1] - bcols[:, jf:jf + 1]
            li_row = gr[ji:ji + 1, :]
            m_old = m_scr[idx][0:1, 0:1]
            c_old = c_scr[idx]

            dlog = jnp.where(seen, b_row + cs, NEG)
            inter = b_row + m_old
            m_t = jnp.maximum(inter, jnp.max(dlog, axis=0, keepdims=True))
            w_inter = jnp.exp(inter - m_t)
            st = _dot_nt(k, q) * jnp.exp(dlog - m_t)
            cq = _dot_nt(c_old.astype(BF16), q)
            num = w_inter * cq[:M_PAD] + _dot(vt, st.astype(BF16))
            den = w_inter * cq[M_PAD:M_PAD + 1] + jnp.sum(st, axis=0, keepdims=True)
            h_ref[0, sl, :] = num / jnp.maximum(jnp.abs(den), jnp.exp(-m_t))

            total = b_row[:, last:last + 1]
            wlog = total - b_row + li_row
            m_new = jnp.maximum(total + m_old, jnp.max(wlog, axis=1, keepdims=True))
            decay = jnp.exp(total + m_old - m_new)
            w = jnp.exp(wlog - m_new)
            vw = jnp.concatenate([vt.astype(F32) * w, ones * w], axis=0).astype(BF16)
            c_scr[idx] = decay * c_old + _dot(vw, k)
            m_scr[idx] = jnp.broadcast_to(m_new, (8, 128))


def _mlstm(mq, mk, mvT, gl, glT, *, n_lat):
    B, NT, _ = mq.shape
    nt = n_lat + 1
    fwd = lambda t: jnp.where(t == 0, n_lat, t - 1)
    bwd = lambda t: jnp.where(t == 0, n_lat, n_lat - t)
    tok = lambda w, f: pl.BlockSpec((1, TOK, w), lambda b, t: (b, f(t), 0))
    lanes = lambda r, f: pl.BlockSpec((1, r, TOK), lambda b, t: (b, 0, f(t)))
    ins, specs = [], []
    for f in (fwd, bwd):
        ins += [mq, mk, mvT, gl, glT]
        specs += [tok(MP_WIDTH, f)] * 2 + [lanes(MP_WIDTH, f), tok(128, f), lanes(N_GATES, f)]
    return pl.pallas_call(
        _mlstm_kernel,
        out_shape=[jax.ShapeDtypeStruct((B, MP_WIDTH, NT), F32)] * 2,
        grid=(B, nt),
        in_specs=specs,
        out_specs=[lanes(MP_WIDTH, fwd), lanes(MP_WIDTH, bwd)],
        scratch_shapes=[pltpu.VMEM((2 * M_HEADS, M_PAD + 16, M_PAD), F32),
                        pltpu.VMEM((2 * M_HEADS, 8, 128), F32)],
        compiler_params=_cp(("arbitrary", "arbitrary"), VMEM_LIMIT),
        name="mlstm",
    )(*ins)


def _outproj_kernel(x_ref, f_ref, dat_ref, hf_ref, hb_ref, mo_ref, ada_ref, mg_ref, wo_ref, wod_ref, g2_ref, wr_ref,
                    xo_ref, hl_ref, pt_ref, *, is_ctx):
    b = pl.program_id(0)
    mod = ada_ref[CTX_ROW if is_ctx else b]
    gt1, sh2, sc2 = mod[2:3], mod[3:4], mod[4:5]
    mg = mg_ref[...]
    for s in range(x_ref.shape[1] // TOK):
        tk = slice(TOK * s, TOK * (s + 1))
        hs = hf_ref[0, :, tk] + hb_ref[0, :, tk]
        og = mo_ref[0, :, tk].astype(F32)
        parts = [dat_ref[0, :, tk]]
        for hd in range(M_HEADS):
            sl = slice(M_PAD * hd, M_PAD * (hd + 1))
            hh = hs[sl]
            r = lax.rsqrt(jnp.sum(hh * hh, axis=0, keepdims=True) * (1.0 / M_DIM) + EPS)
            parts.append((((hh * r) * mg[sl]) * _sigmoid(og[sl])).astype(BF16))
        mix_t = jnp.concatenate(parts, axis=0)
        upd = _dot(f_ref[0, tk].astype(BF16), wo_ref[...]) + lax.dot_general(
            mix_t, wod_ref[...], (((0,), (0,)), ((), ())), preferred_element_type=F32)
        xn = x_ref[0, tk] + gt1 * upd
        xo_ref[0, tk] = xn
        r = lax.rsqrt(jnp.mean(xn * xn, axis=-1, keepdims=True) + EPS)
        h2 = (xn * r) * g2_ref[...] * (1.0 + sc2) + sh2
        hl_ref[0, tk] = h2.astype(BF16)
        lt = _dot3(h2, wr_ref[...]).T[:N_EXPERTS]
        ex = jnp.exp(lt - jnp.max(lt, axis=0, keepdims=True))
        pt_ref[0, :, tk] = ex / jnp.sum(ex, axis=0, keepdims=True)


def _outproj_kernel_aliased(x_ref, f_ref, dat_ref, hf_ref, hb_ref, mo_ref, ada_ref, mg_ref, wo_ref, wod_ref,
                            g2_ref, wr_ref, hlp_ref, xo_ref, hl_ref, pt_ref, *, is_ctx):
    del hlp_ref
    _outproj_kernel(x_ref, f_ref, dat_ref, hf_ref, hb_ref, mo_ref, ada_ref, mg_ref, wo_ref, wod_ref, g2_ref,
                    wr_ref, xo_ref, hl_ref, pt_ref, is_ctx=is_ctx)


def _outproj(xu, f, daT, hf, hb, mo, ada_l, mg, wo, wod, g2, wrp, hl_prev, *, t0, ntl, is_ctx):
    B, NT, _ = xu.shape
    n = ntl * TOK
    tile = 2 * TOK if n % (2 * TOK) == 0 else TOK
    o = t0 * TOK // tile
    tok = lambda w: pl.BlockSpec((1, tile, w), lambda b, t: (b, o + t, 0))
    trs = lambda r: pl.BlockSpec((1, r, tile), lambda b, t: (b, 0, o + t))
    loc = lambda w: pl.BlockSpec((1, tile, w), lambda b, t: (b, t, 0))
    full = lambda a: pl.BlockSpec(a.shape, lambda b, t: (0,) * a.ndim)
    return pl.pallas_call(
        functools.partial(_outproj_kernel_aliased, is_ctx=is_ctx),
        out_shape=[jax.ShapeDtypeStruct(xu.shape, F32), jax.ShapeDtypeStruct((B, NT, D), BF16),
                   jax.ShapeDtypeStruct((B, N_EXPERTS, n), F32)],
        grid=(B, n // tile),
        in_specs=[tok(D), loc(F_WIDTH), pl.BlockSpec((1, DA_WIDTH, tile), lambda b, t: (b, 0, t)),
                  trs(MP_WIDTH), trs(MP_WIDTH), trs(MP_WIDTH),
                  full(ada_l), full(mg), full(wo), full(wod), full(g2), full(wrp),
                  pl.BlockSpec(memory_space=pl.ANY)],
        out_specs=[tok(D), tok(D), pl.BlockSpec((1, N_EXPERTS, tile), lambda b, t: (b, 0, t))],
        input_output_aliases={0: 0, 12: 1},
        compiler_params=_cp(("arbitrary", "arbitrary"), VMEM_LIMIT),
        name="outproj_norm2_router",
    )(xu, f, daT, hf, hb, mo, ada_l, mg, wo, wod, g2, wrp, hl_prev)


def _select_kernel(p_ref, rank_ref, offs_ref, *, n, cap):
    p = p_ref[0]
    xi = pltpu.bitcast(p, jnp.int32)

    def body(i, lo):
        cand = lo | jnp.left_shift(jnp.int32(1), 30 - i)
        cnt = jnp.sum(jnp.where(xi >= cand, 1.0, 0.0), axis=1, keepdims=True)
        return jnp.where(cnt >= cap, cand, lo)

    thr = lax.fori_loop(0, 31, body, jnp.zeros((N_EXPERTS, 1), jnp.int32))
    nb = n // TOK
    rows = lax.broadcasted_iota(jnp.int32, (n, 128), 0)
    cols = lax.broadcasted_iota(jnp.int32, (n, 128), 1)
    blk_ind = jnp.where((rows // TOK) == cols, 1.0, 0.0).astype(BF16)
    u128 = jnp.where(lax.broadcasted_iota(jnp.int32, (128, 128), 0)
                     < lax.broadcasted_iota(jnp.int32, (128, 128), 1), 1.0, 0.0).astype(BF16)
    utok = jnp.where(lax.broadcasted_iota(jnp.int32, (TOK, TOK), 0)
                     < lax.broadcasted_iota(jnp.int32, (TOK, TOK), 1), 1.0, 0.0).astype(BF16)

    def prefix(mf):
        mb = mf.astype(BF16)
        counts = _dot(mb, blk_ind)
        offs = _dot(counts.astype(BF16), u128)
        pieces = [_dot(mb[:, TOK * j:TOK * (j + 1)], utok) + offs[:, j:j + 1] for j in range(nb)]
        return (jnp.concatenate(pieces, axis=1) if nb > 1 else pieces[0]), offs

    gt = xi > thr
    eq = xi == thr
    need = cap - jnp.sum(jnp.where(gt, 1.0, 0.0), axis=1, keepdims=True)
    rank_eq, _ = prefix(jnp.where(eq, 1.0, 0.0))
    sel = gt | (eq & (rank_eq < need))
    rank, offs = prefix(jnp.where(sel, 1.0, 0.0))
    rank_ref[0] = jnp.where(sel, rank, -1.0)
    offs_ref[0] = offs.astype(jnp.int32)


def _select(pt, *, cap):
    B, _, n = pt.shape
    return pl.pallas_call(
        functools.partial(_select_kernel, n=n, cap=cap),
        out_shape=[jax.ShapeDtypeStruct((B, N_EXPERTS, n), F32),
                   jax.ShapeDtypeStruct((B, N_EXPERTS, 128), jnp.int32)],
        grid=(B,),
        in_specs=[pl.BlockSpec((1, N_EXPERTS, n), lambda b: (b, 0, 0))],
        out_specs=[pl.BlockSpec((1, N_EXPERTS, n), lambda b: (b, 0, 0)),
                   pl.BlockSpec((1, N_EXPERTS, 128), lambda b: (b, 0, 0))],
        compiler_params=_cp(("arbitrary",), VMEM_LIMIT),
        name="expert_choice_select",
    )(pt)


def _gather_kernel(offs_ref, h_ref, rank_ref, prob_ref, o_ref, gate_ref, *, eg, per):
    b, g, tb = pl.program_id(0), pl.program_id(1), pl.program_id(2)

    @pl.when(tb == 0)
    def _():
        o_ref[...] = jnp.zeros(o_ref.shape, BF16)
        gate_ref[...] = jnp.zeros(gate_ref.shape, F32)

    cap_pad = o_ref.shape[2]
    half = SLOT // 2
    slot = lax.broadcasted_iota(jnp.int32, (SLOT, TOK), 0).astype(F32)

    def add_rows(i, r, p, h, base, start=None):
        hit = r == slot + base.astype(F32)
        if start is not None:
            hit = hit & (r >= start.astype(F32))
        rows = _dot(jnp.where(hit, 1.0, 0.0).astype(BF16), h).astype(BF16)
        o_ref[0, i, pl.ds(base, SLOT), :] = o_ref[0, i, pl.ds(base, SLOT), :] + rows
        gate_ref[0, i, pl.ds(base, SLOT), :] = (gate_ref[0, i, pl.ds(base, SLOT), :]
                                                + jnp.sum(jnp.where(hit, p, 0.0), axis=1, keepdims=True))

    def operands(s, i):
        tk = slice(TOK * s, TOK * (s + 1))
        e = g * eg + i
        return rank_ref[0, pl.ds(e, 1), tk], prob_ref[0, pl.ds(e, 1), tk], h_ref[0, tk, :]

    ends, his = {}, {}
    for s in range(per):
        for i in range(eg):
            e = g * eg + i
            lo = offs_ref[b, e, tb * per + s]
            his[s, i] = offs_ref[b, e, tb * per + s + 1]
            base = pl.multiple_of(jnp.minimum((lo // half) * half, cap_pad - SLOT), half)
            add_rows(i, *operands(s, i), base)
            ends[s, i] = base + SLOT

    for s in range(per):
        for i in range(eg):
            @pl.when(his[s, i] > ends[s, i])
            def _(s=s, i=i):
                def body(t, carry):
                    start = ends[s, i] + t * SLOT
                    base = pl.multiple_of(jnp.minimum(start, cap_pad - SLOT), half)
                    add_rows(i, *operands(s, i), base, start)
                    return carry

                lax.fori_loop(0, (his[s, i] - ends[s, i] + SLOT - 1) // SLOT, body, 0)


def _gather(offs, hl, rank, pt, *, tb_tok, tb0, n, cap_pad, eg):
    B = hl.shape[0]
    per = tb_tok // TOK
    return pl.pallas_call(
        functools.partial(_gather_kernel, eg=eg, per=per),
        out_shape=[jax.ShapeDtypeStruct((B, N_EXPERTS, cap_pad, D), BF16),
                   jax.ShapeDtypeStruct((B, N_EXPERTS, cap_pad, 1), F32)],
        grid_spec=pltpu.PrefetchScalarGridSpec(
            num_scalar_prefetch=1,
            grid=(B, N_EXPERTS // eg, n // tb_tok),
            in_specs=[pl.BlockSpec((1, tb_tok, D), lambda b, g, t, o: (b, tb0 + t, 0)),
                      pl.BlockSpec((1, N_EXPERTS, tb_tok), lambda b, g, t, o: (b, 0, t)),
                      pl.BlockSpec((1, N_EXPERTS, tb_tok), lambda b, g, t, o: (b, 0, t))],
            out_specs=[pl.BlockSpec((1, eg, cap_pad, D), lambda b, g, t, o: (b, g, 0, 0)),
                       pl.BlockSpec((1, eg, cap_pad, 1), lambda b, g, t, o: (b, g, 0, 0))]),
        compiler_params=_cp(("arbitrary",) * 3, VMEM_LIMIT),
        name="expert_gather",
    )(offs, hl, rank, pt)


FFN_ROWS = 1024


def _ffn_kernel(x_ref, gate_ref, w1_ref, w3_ref, w2_ref, y_ref, acc_ref):
    f = pl.program_id(2)

    @pl.when(f == 0)
    def _():
        acc_ref[...] = jnp.zeros(acc_ref.shape, F32)

    w1 = w1_ref[0, 0].astype(BF16)
    w3 = w3_ref[0, 0].astype(BF16)
    w2 = w2_ref[0, 0].astype(BF16)
    mb, _, cap_pad, _ = x_ref.shape
    rows = min(FFN_ROWS, cap_pad)
    for i in range(mb):
        for r in range(0, cap_pad, rows):
            x = x_ref[i, 0, r:r + rows, :]
            hid = (_silu(_dot(x, w1)) * _dot(x, w3)).astype(BF16)
            acc_ref[i * cap_pad + r:i * cap_pad + r + rows, :] += _dot(hid, w2)

    @pl.when(f == pl.num_programs(2) - 1)
    def _():
        gate = gate_ref[...].reshape(-1, 1)
        y_ref[...] = (acc_ref[...] * gate).astype(BF16).reshape(y_ref.shape)


def _ffn(xs, gates, w1, w3, w2, *, layer, mb, tf):
    B, E, cap_pad, _ = xs.shape
    return pl.pallas_call(
        _ffn_kernel,
        out_shape=jax.ShapeDtypeStruct(xs.shape, BF16),
        grid=(E, B // mb, D_FF // tf),
        in_specs=[pl.BlockSpec((mb, 1, cap_pad, D), lambda e, m, f: (m, e, 0, 0)),
                  pl.BlockSpec((mb, 1, cap_pad, 1), lambda e, m, f: (m, e, 0, 0)),
                  pl.BlockSpec((1, 1, D, tf), lambda e, m, f: (layer, e, 0, f)),
                  pl.BlockSpec((1, 1, D, tf), lambda e, m, f: (layer, e, 0, f)),
                  pl.BlockSpec((1, 1, tf, D), lambda e, m, f: (layer, e, f, 0))],
        out_specs=pl.BlockSpec((mb, 1, cap_pad, D), lambda e, m, f: (m, e, 0, 0)),
        scratch_shapes=[pltpu.VMEM((mb * cap_pad, D), F32)],
        compiler_params=_cp(("arbitrary",) * 3, VMEM_LIMIT),
        name="expert_ffn",
    )(xs, gates, w1, w3, w2)


CCOL = 512


def _combine_kernel(offs_ref, x_ref, y_ref, rankc_ref, ada_ref, o_ref, tot_scr, *, per, is_ctx):
    b, tb = pl.program_id(0), pl.program_id(2)
    gt2 = ada_ref[CTX_ROW if is_ctx else b][5:6]
    rc_all = rankc_ref[0]
    cap_pad = y_ref.shape[2]
    half = SLOT // 2
    slot = lax.broadcasted_iota(jnp.int32, (1, SLOT), 1).astype(F32)

    ends, his = {}, {}
    for s in range(per):
        tk = slice(TOK * s, TOK * (s + 1))
        total = jnp.zeros((TOK, tot_scr.shape[1]), F32)
        for e0 in range(0, N_EXPERTS, 2):
            hots, rows = [], []
            for e in (e0, e0 + 1):
                lo = offs_ref[b, e, tb * per + s]
                his[s, e] = offs_ref[b, e, tb * per + s + 1]
                base = pl.multiple_of(jnp.minimum((lo // half) * half, cap_pad - SLOT), half)
                ends[s, e] = base + SLOT
                hots.append(jnp.where(rc_all[tk, e:e + 1] == slot + base.astype(F32), 1.0, 0.0).astype(BF16))
                rows.append(y_ref[0, e, pl.ds(base, SLOT), :])
            total = total + _dot(jnp.concatenate(hots, axis=1), jnp.concatenate(rows, axis=0))
        tot_scr[tk] = total

    for s in range(per):
        tk = slice(TOK * s, TOK * (s + 1))
        for e in range(N_EXPERTS):
            @pl.when(his[s, e] > ends[s, e])
            def _(s=s, e=e, tk=tk):
                rc = rc_all[tk, e:e + 1]

                def body(t, carry):
                    start = ends[s, e] + t * SLOT
                    base = pl.multiple_of(jnp.minimum(start, cap_pad - SLOT), half)
                    hit = (rc == slot + base.astype(F32)) & (rc >= start.astype(F32))
                    tot_scr[tk] += _dot(jnp.where(hit, 1.0, 0.0).astype(BF16), y_ref[0, e, pl.ds(base, SLOT), :])
                    return carry

                lax.fori_loop(0, (his[s, e] - ends[s, e] + SLOT - 1) // SLOT, body, 0)

    o_ref[0] = x_ref[0] + gt2 * tot_scr[...]


def _combine(offs, xu, ys, rank_c, ada_l, *, tb_tok, tb0, n, is_ctx):
    B = xu.shape[0]
    cap_pad = ys.shape[2]
    per = tb_tok // TOK
    return pl.pallas_call(
        functools.partial(_combine_kernel, per=per, is_ctx=is_ctx),
        out_shape=jax.ShapeDtypeStruct(xu.shape, F32),
        grid_spec=pltpu.PrefetchScalarGridSpec(
            num_scalar_prefetch=1,
            grid=(B, D // CCOL, n // tb_tok),
            in_specs=[pl.BlockSpec((1, tb_tok, CCOL), lambda b, c, t, o: (b, tb0 + t, c)),
                      pl.BlockSpec((1, N_EXPERTS, cap_pad, CCOL), lambda b, c, t, o: (b, 0, 0, c),
                                   pipeline_mode=pl.Buffered(1)),
                      pl.BlockSpec((1, tb_tok, N_EXPERTS), lambda b, c, t, o: (b, t, 0)),
                      pl.BlockSpec((ADA_ROWS, ADA_CHUNKS, CCOL), lambda b, c, t, o: (0, 0, c))],
            out_specs=pl.BlockSpec((1, tb_tok, CCOL), lambda b, c, t, o: (b, tb0 + t, c)),
            scratch_shapes=[pltpu.VMEM((tb_tok, CCOL), F32)]),
        input_output_aliases={1: 0},
        compiler_params=_cp(("arbitrary",) * 3, VMEM_LIMIT),
        name="expert_combine",
    )(offs, xu, ys, rank_c, ada_l)


def _moe(xu, hl, pt, ada_l, w1, w3, w2, *, layer, row0, is_ctx):
    B, _, n = pt.shape
    cap = EC_CAPACITY * n // N_EXPERTS
    cap_pad = -(-cap // SLOT) * SLOT
    nb = n // TOK
    rank, offs = _select(pt, cap=cap)
    offs = offs[:, :, :nb + 1]
    gt = min(n, MOE_TOK)
    ct = min(n, COMBINE_TOK)
    xs, gates = _gather(offs, hl, rank, pt, tb_tok=gt, tb0=row0 // gt, n=n, cap_pad=cap_pad, eg=GATHER_EXPERTS)
    mb = 2 if (B % 2 == 0 and cap_pad >= 1024) else (B if cap_pad < 1024 else 1)
    ys = _ffn(xs, gates, w1, w3, w2, layer=layer, mb=mb, tf=FFN_TF)
    rank_c = jnp.swapaxes(rank, 1, 2)
    return _combine(offs, xu, ys, rank_c, ada_l, tb_tok=ct, tb0=row0 // ct, n=n, is_ctx=is_ctx)


def _final_kernel(x_ref, g_ref, o_ref):
    x = x_ref[0]
    r = lax.rsqrt(jnp.mean(x * x, axis=-1, keepdims=True) + EPS)
    o_ref[0] = (x * r) * g_ref[...]


def _final_norm(xu, g, *, n):
    B = xu.shape[0]
    tm = MOE_TOK
    return pl.pallas_call(
        _final_kernel,
        out_shape=jax.ShapeDtypeStruct((B, n, D), F32),
        grid=(B, n // tm),
        in_specs=[pl.BlockSpec((1, tm, D), lambda b, t: (b, t, 0)),
                  pl.BlockSpec((1, D), lambda b, t: (0, 0))],
        out_specs=pl.BlockSpec((1, tm, D), lambda b, t: (b, t, 0)),
        compiler_params=_cp(("arbitrary", "arbitrary")),
        name="final_norm",
    )(xu, g)


def _rope_tables(n, ctx):
    rows = n // GRID_W
    t_row = jnp.repeat(jnp.arange(rows), GRID_W)
    t_col = jnp.tile(jnp.arange(GRID_W), rows)
    nf = DA_DIM // 4
    inv = ROPE_THETA ** (-jnp.arange(nf, dtype=F32) / nf)
    ar = t_row[:, None].astype(F32) * inv
    ac = t_col[:, None].astype(F32) * inv
    ang = jnp.concatenate([ar, ar, ac, ac], axis=-1)
    sign = jnp.where((jnp.arange(DA_DIM) % 16) < 8, -1.0, 1.0).astype(F32)
    cos = jnp.concatenate([jnp.cos(ang), jnp.ones((ctx, DA_DIM), F32)], axis=0)
    sin = jnp.concatenate([jnp.sin(ang) * sign, jnp.zeros((ctx, DA_DIM), F32)], axis=0)
    return jnp.tile(cos, (1, 128 // DA_DIM)), jnp.tile(sin, (1, 128 // DA_DIM))


def _pad_heads_cols(w):
    lead = w.shape[:-1]
    w = w.reshape(lead + (M_HEADS, M_DIM))
    w = jnp.pad(w, [(0, 0)] * len(lead) + [(0, 0), (0, M_PAD - M_DIM)])
    return w.reshape(lead + (MP_WIDTH,))


def _kv_tile(nt):
    for parts in range(1, nt // LANES + 1):
        if nt % parts == 0 and (nt // parts) % LANES == 0 and nt // parts <= KV_TILE_MAX:
            return nt // parts
    raise ValueError(nt)


def kernel(x, c, ctx, c_ctx, ada_w, ada_b, norm1_g, norm2_g, w_in, four_w, m_conv_w, m_conv_b, m_gate_b,
           m_norm_g, d_lam, d_norm_g, w_out, router_w, exp_w1, exp_w3, exp_w2, final_g):
    B, N, _ = x.shape
    CTX = ctx.shape[1]
    depth = w_in.shape[0]
    assert CTX == TOK and N % (FFT_N1 * TOK) == 0 and N % Q_TILE == 0 and B <= CTX_ROW
    NT = N + CTX
    PAD = -NT % MOE_TOK
    n_lat = N // TOK
    n2 = N // FFT_N1

    xu = jnp.concatenate([x, ctx, jnp.zeros((B, PAD, D), F32)], axis=1)
    cvecs = jnp.zeros((ADA_ROWS, D), F32).at[:B].set(c).at[CTX_ROW].set(c_ctx)
    ada = _adaln(cvecs, ada_w, ada_b).reshape(depth, ADA_ROWS, ADA_CHUNKS, D)
    cos_t, sin_t = _rope_tables(N, CTX + PAD)
    tabs = _fourier_tables(N, CTX)
    tk = _kv_tile(NT)
    tq = Q_TILE

    hl = jnp.zeros((B, NT + PAD, D), BF16)
    for layer in range(depth):
        ctx_out = layer < depth - 1
        lam_init = 0.8 - 0.6 * math.exp(-0.3 * layer)
        w = w_in[layer]
        wm = jnp.concatenate([w[:, OFF_F:OFF_DQ], w[:, OFF_DQ:OFF_MO], w[:, OFF_DK:OFF_DV]], axis=1).astype(BF16)
        wvt = jnp.concatenate([w[:, OFF_DV:OFF_MV], _pad_heads_cols(w[:, OFF_MO:OFF_MQ]),
                               _pad_heads_cols(w[:, OFF_MV:OFF_G])], axis=1).T.astype(BF16)
        wg = jnp.pad(w[:, OFF_G:], ((0, 0), (0, LANES - N_GATES)))
        wc = jnp.concatenate([_pad_heads_cols(w[:, OFF_MQ:OFF_MK]), _pad_heads_cols(w[:, OFF_MK:OFF_DK])],
                             axis=1).astype(BF16)
        gb = jnp.pad(m_gate_b[layer], (0, LANES - N_GATES)).reshape(1, LANES)
        cw = jnp.concatenate([_pad_heads_cols(m_conv_w[layer][:, :M_WIDTH]),
                              _pad_heads_cols(m_conv_w[layer][:, M_WIDTH:])], axis=1)
        cb = jnp.concatenate([_pad_heads_cols(m_conv_b[layer][:M_WIDTH]),
                              _pad_heads_cols(m_conv_b[layer][M_WIDTH:])]).reshape(1, 2 * MP_WIDTH)
        ada_l = ada[layer]

        y4, dq, dk, dvT, mo, mq, mk, mv, gl, glT = _inproj(
            xu, ada_l, norm1_g[layer].reshape(1, D), wm, wc, wvt, wg, gb, tabs["cs"], cos_t, sin_t, cw, cb,
            n_lat=n_lat, n2=n2)

        wblk = jnp.zeros((F_WIDTH, F_WIDTH), F32)
        for g in range(F_GROUPS):
            wblk = wblk.at[F_GDIM * g:F_GDIM * (g + 1), F_GDIM * g:F_GDIM * (g + 1)].set(four_w[layer, g])
        f_l, f_c = _fourier(y4, tabs, wblk.astype(BF16), n=N, ctx=CTX, with_ctx=ctx_out)

        dlam = d_lam[layer]
        g2 = d_norm_g[layer].reshape(DA_VDIM, 1)
        da_l = _attention(dq, dk, dvT, dlam, g2, lam_init=lam_init, tq=tq, q0=0, nq=N // tq,
                          tk=tk, k0=0, nk=NT // tk)

        hf, hb = _mlstm(mq, mk, mv, gl, glT, n_lat=n_lat)

        mg = _pad_heads_cols(m_norm_g[layer]).reshape(MP_WIDTH, 1)
        wol = w_out[layer]
        wo = wol[:F_WIDTH].astype(BF16)
        wod = jnp.concatenate([wol[F_WIDTH:F_WIDTH + DA_WIDTH],
                               jnp.pad(wol[F_WIDTH + DA_WIDTH:].reshape(M_HEADS, M_DIM, D),
                                       ((0, 0), (0, M_PAD - M_DIM), (0, 0))).reshape(MP_WIDTH, D)],
                              axis=0).astype(BF16)
        g2n = norm2_g[layer].reshape(1, D)
        wrp = jnp.pad(router_w[layer], ((0, 0), (0, LANES - N_EXPERTS)))
        xu, hl, pt_l = _outproj(xu, f_l, da_l, hf, hb, mo, ada_l, mg, wo, wod, g2n, wrp, hl,
                                t0=0, ntl=n_lat, is_ctx=False)
        if ctx_out:
            da_c = _attention(dq, dk, dvT, dlam, g2, lam_init=lam_init, tq=TOK, q0=n_lat, nq=1,
                              tk=TOK, k0=n_lat, nk=1)
            xu, hl, pt_c = _outproj(xu, f_c, da_c, hf, hb, mo, ada_l, mg, wo, wod, g2n, wrp, hl,
                                    t0=n_lat, ntl=1, is_ctx=True)

        xu = _moe(xu, hl, pt_l, ada_l, exp_w1, exp_w3, exp_w2, layer=layer, row0=0, is_ctx=False)
        if ctx_out:
            xu = _moe(xu, hl, pt_c, ada_l, exp_w1, exp_w3, exp_w2, layer=layer, row0=N, is_ctx=True)

    return _final_norm(xu, final_g.reshape(1, D), n=N)
```

```python
import functools
import math

import numpy as np
import jax
import jax.numpy as jnp
from jax import lax
from jax.experimental import pallas as pl
from jax.experimental.pallas import tpu as pltpu

F32 = jnp.float32
BF16 = jnp.bfloat16
HI = lax.Precision.HIGHEST

D = 1024
EPS = 1e-6
GRID_W = 64
ROPE_THETA = 10000.0
F_GROUPS, F_GDIM = 4, 64
F_WIDTH = F_GROUPS * F_GDIM
DA_HEADS, DA_DIM = 6, 32
DA_VDIM = 2 * DA_DIM
DA_WIDTH = DA_HEADS * DA_VDIM
M_HEADS, M_DIM = 4, 96
M_WIDTH = M_HEADS * M_DIM
M_PAD = 128
MP_WIDTH = M_HEADS * M_PAD
N_GATES = 4 * M_HEADS
N_EXPERTS = 16
EC_CAPACITY = 2
D_FF = 2 * D
ADA_CHUNKS = 6
ADA_ROWS = 8
CTX_ROW = 4

LANES = 128
MXU_DIM = 256
V7X_VMEM_BYTES = 64 * 1024 * 1024

TOK = MXU_DIM
FFT_N1 = 16
SLOT = LANES
Q_TILE = 2048
KV_TILE_MAX = 1408
MOE_TOK = 512
COMBINE_TOK = 512
FFN_TF = 512
GATHER_EXPERTS = 8
NEG = -1e30

OFF_F = 0
OFF_DQ = OFF_F + F_WIDTH
OFF_MO = OFF_DQ + 2 * DA_HEADS * DA_DIM
OFF_MQ = OFF_MO + M_WIDTH
OFF_MK = OFF_MQ + M_WIDTH
OFF_DK = OFF_MK + M_WIDTH
OFF_DV = OFF_DK + 2 * DA_HEADS * DA_DIM
OFF_MV = OFF_DV + DA_HEADS * DA_VDIM
OFF_G = OFF_MV + M_WIDTH

VMEM_LIMIT = V7X_VMEM_BYTES * 7 // 8


def _cp(sem, vmem=None):
    return pltpu.CompilerParams(dimension_semantics=sem, vmem_limit_bytes=vmem)


def _sigmoid(x):
    return 1.0 / (1.0 + jnp.exp(-x))


def _silu(x):
    return x * _sigmoid(x)


def _dot(a, b, precision=None):
    return jnp.dot(a, b, preferred_element_type=F32, precision=precision)


def _split(a):
    hi = a.astype(BF16)
    return hi, (a - hi.astype(F32)).astype(BF16)


def _dot3(a, b):
    a_hi, a_lo = a if isinstance(a, tuple) else _split(a)
    b_hi, b_lo = b if isinstance(b, tuple) else _split(b)
    return _dot(a_hi, b_hi) + _dot(a_hi, b_lo) + _dot(a_lo, b_hi)


def _split3(a):
    hi = a.astype(BF16)
    r = a - hi.astype(F32)
    mid = r.astype(BF16)
    return hi, mid, (r - mid.astype(F32)).astype(BF16)


def _dot_nt(a, b, precision=None):
    return lax.dot_general(a, b, (((1,), (1,)), ((), ())), preferred_element_type=F32,
                           precision=precision)


def _ada_kernel(c_ref, w_ref, b_ref, o_ref):
    c = c_ref[...]
    o_ref[0] = _dot(_silu(c), w_ref[0], HI) + b_ref[0]


def _adaln(cvecs, ada_w, ada_b):
    depth = ada_w.shape[0]
    tn = 1536
    return pl.pallas_call(
        _ada_kernel,
        out_shape=jax.ShapeDtypeStruct((depth, ADA_ROWS, ADA_CHUNKS * D), F32),
        grid=(depth, ADA_CHUNKS * D // tn),
        in_specs=[pl.BlockSpec((ADA_ROWS, D), lambda l, j: (0, 0)),
                  pl.BlockSpec((1, D, tn), lambda l, j: (l, 0, j)),
                  pl.BlockSpec((1, 1, tn), lambda l, j: (l, 0, j))],
        out_specs=pl.BlockSpec((1, 8, tn), lambda l, j: (l, 0, j)),
        compiler_params=_cp(("arbitrary", "arbitrary")),
        name="adaln",
    )(cvecs, ada_w, ada_b.reshape(depth, 1, ADA_CHUNKS * D))


def _inproj_kernel(x_ref, xp_ref, xn_ref, ada_ref, g_ref, wm_ref, wc_ref, wvt_ref, wg_ref, gb_ref, cs_ref,
                   cos_ref, sin_ref, cw_ref, cb_ref,
                   y_ref, dq_ref, dk_ref, dvt_ref, mo_ref, mq_ref, mk_ref, mv_ref, gl_ref, glt_ref, *, n_lat):
    b = pl.program_id(0)
    t = pl.program_id(1)
    n_tiles = pl.num_programs(1)
    is_ctx = t >= n_lat
    row = jnp.where(is_ctx, CTX_ROW, b)
    mod = ada_ref[row]
    sh, sc = mod[0:1], mod[1:2]

    xa = jnp.concatenate([xp_ref[0], x_ref[0], xn_ref[0]], axis=0)
    r = lax.rsqrt(jnp.mean(xa * xa, axis=-1, keepdims=True) + EPS)
    ha = (xa * r) * g_ref[...] * (1.0 + sc) + sh
    h = ha[8:8 + TOK]
    hb = h.astype(BF16)

    pm = _dot(hb, wm_ref[...])
    o = 0
    pf = pm[:, o:o + F_WIDTH]; o += F_WIDTH
    q = pm[:, o:o + DA_WIDTH]; o += DA_WIDTH
    k = pm[:, o:o + DA_WIDTH]; o += DA_WIDTH
    pt = _dot_nt(wvt_ref[...], hb)
    dvt_ref[0] = pt[:DA_WIDTH].astype(BF16)
    mo_ref[0] = pt[DA_WIDTH:DA_WIDTH + MP_WIDTH].astype(BF16)
    mv_ref[0] = pt[DA_WIDTH + MP_WIDTH:].astype(BF16)

    y_ref[0, 0] = _dot3(pf, cs_ref[...])

    cos = cos_ref[...]
    sin = sin_ref[...]
    lane = lax.broadcasted_iota(jnp.int32, (1, 128), 1)
    low = (lane % 16) < 8

    def rope(z):
        parts = []
        for c in range(DA_WIDTH // 128):
            zc = z[:, 128 * c:128 * (c + 1)]
            rot = jnp.where(low, pltpu.roll(zc, 120, 1), pltpu.roll(zc, 8, 1))
            parts.append(zc * cos + rot * sin)
        return jnp.concatenate(parts, axis=1)

    dq_ref[0] = (rope(q) * (DA_DIM ** -0.5 * math.log2(math.e))).astype(BF16)
    dk_ref[0] = rope(k).astype(BF16)

    gpre = _dot3(h, wg_ref[...]) + gb_ref[...]
    is_forget = (lax.broadcasted_iota(jnp.int32, (1, LANES), 1) % 8) >= 4
    logsig = jnp.minimum(gpre, 0.0) - jnp.log(1.0 + jnp.exp(-jnp.abs(gpre)))
    gl = jnp.where(is_forget, logsig, gpre)
    gl_ref[0] = gl
    glt_ref[0] = gl.T[:N_GATES]

    pc = _dot(ha.astype(BF16), wc_ref[...])
    first = (t == 0) | (t == n_lat)
    last = (t == n_lat - 1) | (t == n_tiles - 1)
    ridx = lax.broadcasted_iota(jnp.int32, (TOK + 16, 1), 0)
    pc = jnp.where(((ridx < 8) & first) | ((ridx >= TOK + 8) & last), 0.0, pc)
    cw = cw_ref[...]
    conv = cb_ref[...] + pc[7:7 + TOK] * cw[0:1] + pc[8:8 + TOK] * cw[1:2] + pc[9:9 + TOK] * cw[2:3]
    act = _silu(conv)
    mq_ref[0] = act[:, :MP_WIDTH].astype(BF16)
    mk_ref[0] = (act[:, MP_WIDTH:] * (M_DIM ** -0.5)).astype(BF16)


def _inproj(xu, ada_l, g1, wm, wc, wvt, wg, gb, cs, cos_t, sin_t, cw, cb, *, n_lat, n2):
    B, NT, _ = xu.shape
    nt = n_lat + 1
    rper = n2 // TOK
    tok3 = lambda w: pl.BlockSpec((1, TOK, w), lambda b, t: (b, t, 0))
    full = lambda a: pl.BlockSpec(a.shape, lambda b, t: (0,) * a.ndim)
    nb8 = NT // 8
    outs = [jax.ShapeDtypeStruct((B, 2 * FFT_N1, n2, 2 * F_WIDTH), F32)]
    nq = -(-NT // Q_TILE) * Q_TILE
    outs += [jax.ShapeDtypeStruct((B, nq, DA_WIDTH), BF16), jax.ShapeDtypeStruct((B, nt * TOK, DA_WIDTH), BF16)]
    outs += [jax.ShapeDtypeStruct((B, DA_WIDTH, nt * TOK), BF16)]
    trs = lambda r: pl.BlockSpec((1, r, TOK), lambda b, t: (b, 0, t))
    trp = jax.ShapeDtypeStruct((B, MP_WIDTH, NT), BF16)
    outs += [trp, jax.ShapeDtypeStruct((B, NT, MP_WIDTH), BF16), jax.ShapeDtypeStruct((B, NT, MP_WIDTH), BF16), trp]
    outs += [jax.ShapeDtypeStruct((B, NT, 128), F32), jax.ShapeDtypeStruct((B, N_GATES, nt * TOK), F32)]
    out_specs = [pl.BlockSpec((1, 1, TOK, 2 * F_WIDTH), lambda b, t: (b, t // rper, t % rper, 0))]
    out_specs += [tok3(DA_WIDTH)] * 2 + [trs(DA_WIDTH)]
    out_specs += [trs(MP_WIDTH), tok3(MP_WIDTH), tok3(MP_WIDTH), trs(MP_WIDTH)]
    out_specs += [tok3(128), pl.BlockSpec((1, N_GATES, TOK), lambda b, t: (b, 0, t))]
    return pl.pallas_call(
        functools.partial(_inproj_kernel, n_lat=n_lat),
        out_shape=outs,
        grid=(B, nt),
        in_specs=[tok3(D),
                  pl.BlockSpec((1, 8, D), lambda b, t: (b, jnp.maximum(t * (TOK // 8) - 1, 0), 0)),
                  pl.BlockSpec((1, 8, D), lambda b, t: (b, jnp.minimum((t + 1) * (TOK // 8), nb8 - 1), 0)),
                  full(ada_l), full(g1), full(wm), full(wc), full(wvt), full(wg), full(gb), full(cs),
                  pl.BlockSpec((TOK, LANES), lambda b, t: (t, 0)),
                  pl.BlockSpec((TOK, LANES), lambda b, t: (t, 0)),
                  full(cw), full(cb)],
        out_specs=out_specs,
        compiler_params=_cp(("arbitrary", "arbitrary"), VMEM_LIMIT),
        name="norm1_inproj",
    )(xu, xu, xu, ada_l, g1, wm, wc, wvt, wg, gb, cs, cos_t, sin_t, cw, cb)


def _fft1_kernel(y_ref, kc_ref, ks_ref, tc_ref, ts_ref, o_ref, *, groups):
    kc = _split(kc_ref[...])
    ks = _split(ks_ref[...])
    for g in range(groups):
        blk = _split(y_ref[0, :, 8 * g:8 * (g + 1), :].reshape(FFT_N1 * 8, 2 * F_WIDTH))
        p = _dot3(kc, blk)
        q = _dot3(ks, blk)
        ar = p[:, :F_WIDTH] - q[:, F_WIDTH:]
        ai = -p[:, F_WIDTH:] - q[:, :F_WIDTH]
        tc = tc_ref[128 * g:128 * (g + 1), :]
        ts = ts_ref[128 * g:128 * (g + 1), :]
        tc = jnp.concatenate([tc, tc], axis=1)
        ts = jnp.concatenate([ts, ts], axis=1)
        br = ar * tc + ai * ts
        bi = ai * tc - ar * ts
        o_ref[0, :, 8 * g:8 * (g + 1), :] = jnp.concatenate([br, bi], axis=1).reshape(FFT_N1, 8, 2 * F_WIDTH)


def _fft2_kernel(b_ref, c2_ref, s2_ref, wb_ref, perm_ref, o_ref, r_scr, *, n2):
    c2 = _split(c2_ref[...])
    s2 = _split(s2_ref[...])
    for i in range(8):
        blk = b_ref[0, i]
        xr = _dot3(c2, blk[:, :F_WIDTH]) + _dot3(s2, blk[:, F_WIDTH:])
        r_scr[i] = _dot(xr.astype(BF16), wb_ref[...]).astype(BF16)
    for t in range(n2 // 32):
        rows = jnp.concatenate([r_scr[i, 32 * t:32 * (t + 1), :] for i in range(8)], axis=0)
        o_ref[0, 32 * t:32 * (t + 1), :, :] = _dot(perm_ref[...], rows).reshape(32, 8, F_WIDTH)


def _fftc_kernel(y_ref, c_ref, s_ref, wb_ref, o_ref):
    y = y_ref[0, 0]
    z = _dot3(c_ref[...], y[:, :F_WIDTH]) - _dot3(s_ref[...], y[:, F_WIDTH:])
    o_ref[0] = _dot(z.astype(BF16), wb_ref[...])


def _fourier_tables(n, ctx):
    n1, n2 = FFT_N1, n // FFT_N1
    a = np.arange(n1)
    ang1 = 2 * np.pi * np.outer(a, a) / n1
    eye8 = np.eye(8)
    kc = np.kron(np.cos(ang1), eye8)
    ks = np.kron(np.sin(ang1), eye8)
    n2i = np.arange(n2).reshape(n2 // 8, 1, 8)
    k1 = np.arange(n1).reshape(1, n1, 1)
    angt = (2 * np.pi * n2i * k1 / n).reshape(-1, 1)
    tc = np.broadcast_to(np.cos(angt), (n2 // 8 * 128, 128))
    ts = np.broadcast_to(np.sin(angt), (n2 // 8 * 128, 128))
    b = np.arange(n2)
    ang2 = 2 * np.pi * np.outer(b, b) / n2
    c2 = np.cos(ang2) / math.sqrt(n)
    s2 = np.sin(ang2) / math.sqrt(n)
    perm = np.zeros((256, 256))
    for kk in range(8):
        for j in range(32):
            perm[j * 8 + kk, kk * 32 + j] = 1.0
    cc = np.arange(ctx)
    angc = 2 * np.pi * np.outer(cc, cc) / ctx
    cctx = np.cos(angc) / math.sqrt(ctx)
    sctx = np.sin(angc) / math.sqrt(ctx)
    ch = np.arange(F_GDIM)
    angch = 2 * np.pi * np.outer(ch, ch) / F_GDIM
    cs = np.concatenate([np.kron(np.eye(F_GROUPS), np.cos(angch)),
                         np.kron(np.eye(F_GROUPS), np.sin(angch))], axis=1) / math.sqrt(F_GDIM)
    f = lambda z: jnp.asarray(np.ascontiguousarray(z), dtype=F32)
    return dict(kc=f(kc), ks=f(ks), tc=f(tc), ts=f(ts), c2=f(c2), s2=f(s2), perm=f(perm).astype(BF16),
                cctx=f(cctx), sctx=f(sctx), cs=f(cs))


def _fourier(y4, tabs, wblk, *, n, ctx, with_ctx):
    B = y4.shape[0]
    n2 = n // FFT_N1
    groups = 4
    full = lambda a, nd: pl.BlockSpec(a.shape, lambda *i: (0,) * a.ndim)
    b4 = pl.pallas_call(
        functools.partial(_fft1_kernel, groups=groups),
        out_shape=jax.ShapeDtypeStruct((B, FFT_N1, n2, 2 * F_WIDTH), F32),
        grid=(B, n2 // (8 * groups)),
        in_specs=[pl.BlockSpec((1, FFT_N1, 8 * groups, 2 * F_WIDTH), lambda b, j: (b, 0, j, 0)),
                  full(tabs["kc"], 2), full(tabs["ks"], 2),
                  pl.BlockSpec((128 * groups, 128), lambda b, j: (j, 0)),
                  pl.BlockSpec((128 * groups, 128), lambda b, j: (j, 0))],
        out_specs=pl.BlockSpec((1, FFT_N1, 8 * groups, 2 * F_WIDTH), lambda b, j: (b, 0, j, 0)),
        compiler_params=_cp(("arbitrary", "arbitrary")),
        name="fourier_stage1",
    )(y4, tabs["kc"], tabs["ks"], tabs["tc"], tabs["ts"])
    f4 = pl.pallas_call(
        functools.partial(_fft2_kernel, n2=n2),
        out_shape=jax.ShapeDtypeStruct((B, n2, 16, F_WIDTH), F32),
        grid=(B, FFT_N1 // 8),
        in_specs=[pl.BlockSpec((1, 8, n2, 2 * F_WIDTH), lambda b, j: (b, j, 0, 0)),
                  full(tabs["c2"], 2), full(tabs["s2"], 2), full(wblk, 2), full(tabs["perm"], 2)],
        out_specs=pl.BlockSpec((1, n2, 8, F_WIDTH), lambda b, j: (b, 0, j, 0)),
        scratch_shapes=[pltpu.VMEM((8, n2, F_WIDTH), BF16)],
        compiler_params=_cp(("arbitrary", "arbitrary"), VMEM_LIMIT),
        name="fourier_stage2",
    )(b4, tabs["c2"], tabs["s2"], wblk, tabs["perm"])
    f_ctx = None
    if with_ctx:
        f_ctx = pl.pallas_call(
            _fftc_kernel,
            out_shape=jax.ShapeDtypeStruct((B, ctx, F_WIDTH), F32),
            grid=(B,),
            in_specs=[pl.BlockSpec((1, 1, TOK, 2 * F_WIDTH), lambda b: (b, FFT_N1, 0, 0)),
                      full(tabs["cctx"], 1), full(tabs["sctx"], 1), full(wblk, 1)],
            out_specs=pl.BlockSpec((1, ctx, F_WIDTH), lambda b: (b, 0, 0)),
            compiler_params=_cp(("arbitrary",)),
            name="fourier_ctx",
        )(y4, tabs["cctx"], tabs["sctx"], wblk)
    return f4.reshape(B, n, F_WIDTH), f_ctx


VROWS = DA_VDIM + 16


def _attn_kernel(q_ref, k_ref, vt_ref, dl_ref, g_ref, o_ref, m_scr, acc_scr, *, lam_init):
    kt = pl.program_id(3)
    nk = pl.num_programs(3)

    @pl.when(kt == 0)
    def _():
        m_scr[...] = jnp.full(m_scr.shape, NEG, F32)
        acc_scr[...] = jnp.zeros(acc_scr.shape, F32)

    q = q_ref[0]
    k = k_ref[0]
    vt = vt_ref[0]
    ones = jnp.ones((16, vt.shape[1]), BF16)
    lhs = [jnp.concatenate([vt[DA_VDIM * h:DA_VDIM * (h + 1)], ones], axis=0) for h in range(2)]
    lane = lax.broadcasted_iota(jnp.int32, (1, LANES), 1)
    zero = jnp.zeros((), BF16)

    def scores(j):
        return _dot_nt(k, jnp.where((lane // DA_DIM) == j, q, zero))

    st_next = scores(0)
    for j in range(4):
        st = st_next
        if j < 3:
            st_next = scores(j + 1)
        m_old = m_scr[j]
        m_new = jnp.maximum(m_old, jnp.max(st, axis=0, keepdims=True))
        alpha = jnp.exp2(m_old - m_new)
        pt = jnp.exp2(st - m_new).astype(BF16)
        acc_scr[j] = alpha * acc_scr[j] + _dot(lhs[j // 2], pt)
        m_scr[j] = m_new

    @pl.when(kt == nk - 1)
    def _():
        dl = dl_ref[...]
        lam = (jnp.exp(jnp.sum(dl[0:1] * dl[1:2], keepdims=True))
               - jnp.exp(jnp.sum(dl[2:3] * dl[3:4], keepdims=True)) + lam_init)
        outs = []
        for h in range(2):
            a0 = acc_scr[2 * h]
            a1 = acc_scr[2 * h + 1]
            o = (a0[:DA_VDIM] / a0[DA_VDIM:DA_VDIM + 1]
                 - lam * (a1[:DA_VDIM] / a1[DA_VDIM:DA_VDIM + 1]))
            r = lax.rsqrt(jnp.mean(o * o, axis=0, keepdims=True) + EPS)
            outs.append(((o * r) * g_ref[...]) * (1.0 - lam_init))
        o_ref[0] = jnp.concatenate(outs, axis=0).astype(BF16)


def _attention(dq, dk, dvT, dlam, gcol, *, lam_init, tq, q0, nq, tk, k0, nk):
    B = dq.shape[0]
    return pl.pallas_call(
        functools.partial(_attn_kernel, lam_init=lam_init),
        out_shape=jax.ShapeDtypeStruct((B, DA_WIDTH, nq * tq), BF16),
        grid=(B, DA_WIDTH // 128, nq, nk),
        in_specs=[pl.BlockSpec((1, tq, 128), lambda b, p, i, j: (b, q0 + i, p)),
                  pl.BlockSpec((1, tk, 128), lambda b, p, i, j: (b, k0 + j, p)),
                  pl.BlockSpec((1, 128, tk), lambda b, p, i, j: (b, p, k0 + j)),
                  pl.BlockSpec(dlam.shape, lambda b, p, i, j: (0, 0)),
                  pl.BlockSpec(gcol.shape, lambda b, p, i, j: (0, 0))],
        out_specs=pl.BlockSpec((1, 128, tq), lambda b, p, i, j: (b, p, i)),
        scratch_shapes=[pltpu.VMEM((4, 1, tq), F32), pltpu.VMEM((4, VROWS, tq), F32)],
        compiler_params=_cp(("arbitrary",) * 4, VMEM_LIMIT),
        name="diff_attention",
    )(dq, dk, dvT, dlam, gcol)


def _mlstm_kernel(qf_ref, kf_ref, vf_ref, gcf_ref, grf_ref, qb_ref, kb_ref, vb_ref, gcb_ref, grb_ref,
                  hf_ref, hb_ref, c_scr, m_scr):
    t = pl.program_id(1)

    @pl.when(t == 0)
    def _():
        c_scr[...] = jnp.zeros(c_scr.shape, F32)
        m_scr[...] = jnp.zeros(m_scr.shape, F32)

    L = TOK
    si = lax.broadcasted_iota(jnp.int32, (L, L), 0)
    li = lax.broadcasted_iota(jnp.int32, (L, L), 1)
    dirs = ((qf_ref, kf_ref, vf_ref, gcf_ref, grf_ref, hf_ref, si <= li, li <= si, L - 1),
            (qb_ref, kb_ref, vb_ref, gcb_ref, grb_ref, hb_ref, si >= li, li >= si, 0))
    ones = jnp.ones((16, L), F32)
    for d, (q_ref, k_ref, vt_ref, gc_ref, gr_ref, h_ref, seen, seen_t, last) in enumerate(dirs):
        gc = gc_ref[0]
        gr = gr_ref[0]
        seen_b = jnp.where(seen, 1.0, 0.0).astype(BF16)
        seen_tb = jnp.where(seen_t, 1.0, 0.0).astype(BF16)
        bcols = sum(_dot(seen_tb, piece) for piece in _split3(gc))
        brows = sum(_dot(piece, seen_b) for piece in _split3(gr))
        for hd in range(M_HEADS):
            idx = d * M_HEADS + hd
            ji = d * 8 + hd
            jf = d * 8 + 4 + hd
            sl = slice(M_PAD * hd, M_PAD * (hd + 1))
            q = q_ref[0, :, sl]
            k = k_ref[0, :, sl]
            vt = vt_ref[0, sl, :]
            b_row = brows[jf:jf + 1, :]
            cs = gc[:, ji:ji + 1] - bcols[:, jf:jf + 1]
            li_row = gr[ji:ji + 1, :]
            m_old = m_scr[idx][0:1, 0:1]
            c_old = c_scr[idx]

            dlog = jnp.where(seen, b_row + cs, NEG)
            inter = b_row + m_old
            m_t = jnp.maximum(inter, jnp.max(dlog, axis=0, keepdims=True))
            w_inter = jnp.exp(inter - m_t)
            st = _dot_nt(k, q) * jnp.exp(dlog - m_t)
            cq = _dot_nt(c_old.astype(BF16), q)
            num = w_inter * cq[:M_PAD] + _dot(vt, st.astype(BF16))
            den = w_inter * cq[M_PAD:M_PAD + 1] + jnp.sum(st, axis=0, keepdims=True)
            h_ref[0, sl, :] = num / jnp.maximum(jnp.abs(den), jnp.exp(-m_t))

            total = b_row[:, last:last + 1]
            wlog = total - b_row + li_row
            m_new = jnp.maximum(total + m_old, jnp.max(wlog, axis=1, keepdims=True))
            decay = jnp.exp(total + m_old - m_new)
            w = jnp.exp(wlog - m_new)
            vw = jnp.concatenate([vt.astype(F32) * w, ones * w], axis=0).astype(BF16)
            c_scr[idx] = decay * c_old + _dot(vw, k)
            m_scr[idx] = jnp.broadcast_to(m_new, (8, 128))


def _mlstm(mq, mk, mvT, gl, glT, *, n_lat):
    B, NT, _ = mq.shape
    nt = n_lat + 1
    fwd = lambda t: jnp.where(t == 0, n_lat, t - 1)
    bwd = lambda t: jnp.where(t == 0, n_lat, n_lat - t)
    tok = lambda w, f: pl.BlockSpec((1, TOK, w), lambda b, t: (b, f(t), 0))
    lanes = lambda r, f: pl.BlockSpec((1, r, TOK), lambda b, t: (b, 0, f(t)))
    ins, specs = [], []
    for f in (fwd, bwd):
        ins += [mq, mk, mvT, gl, glT]
        specs += [tok(MP_WIDTH, f)] * 2 + [lanes(MP_WIDTH, f), tok(128, f), lanes(N_GATES, f)]
    return pl.pallas_call(
        _mlstm_kernel,
        out_shape=[jax.ShapeDtypeStruct((B, MP_WIDTH, NT), F32)] * 2,
        grid=(B, nt),
        in_specs=specs,
        out_specs=[lanes(MP_WIDTH, fwd), lanes(MP_WIDTH, bwd)],
        scratch_shapes=[pltpu.VMEM((2 * M_HEADS, M_PAD + 16, M_PAD), F32),
                        pltpu.VMEM((2 * M_HEADS, 8, 128), F32)],
        compiler_params=_cp(("arbitrary", "arbitrary"), VMEM_LIMIT),
        name="mlstm",
    )(*ins)


def _outproj_kernel(x_ref, f_ref, dat_ref, hf_ref, hb_ref, mo_ref, ada_ref, mg_ref, wo_ref, wod_ref, g2_ref, wr_ref,
                    xo_ref, hl_ref, pt_ref, *, is_ctx):
    b = pl.program_id(0)
    mod = ada_ref[CTX_ROW if is_ctx else b]
    gt1, sh2, sc2 = mod[2:3], mod[3:4], mod[4:5]
    mg = mg_ref[...]
    for s in range(x_ref.shape[1] // TOK):
        tk = slice(TOK * s, TOK * (s + 1))
        hs = hf_ref[0, :, tk] + hb_ref[0, :, tk]
        og = mo_ref[0, :, tk].astype(F32)
        parts = [dat_ref[0, :, tk]]
        for hd in range(M_HEADS):
            sl = slice(M_PAD * hd, M_PAD * (hd + 1))
            hh = hs[sl]
            r = lax.rsqrt(jnp.sum(hh * hh, axis=0, keepdims=True) * (1.0 / M_DIM) + EPS)
            parts.append((((hh * r) * mg[sl]) * _sigmoid(og[sl])).astype(BF16))
        mix_t = jnp.concatenate(parts, axis=0)
        upd = _dot(f_ref[0, tk].astype(BF16), wo_ref[...]) + lax.dot_general(
            mix_t, wod_ref[...], (((0,), (0,)), ((), ())), preferred_element_type=F32)
        xn = x_ref[0, tk] + gt1 * upd
        xo_ref[0, tk] = xn
        r = lax.rsqrt(jnp.mean(xn * xn, axis=-1, keepdims=True) + EPS)
        h2 = (xn * r) * g2_ref[...] * (1.0 + sc2) + sh2
        hl_ref[0, tk] = h2.astype(BF16)
        lt = _dot3(h2, wr_ref[...]).T[:N_EXPERTS]
        ex = jnp.exp(lt - jnp.max(lt, axis=0, keepdims=True))
        pt_ref[0, :, tk] = ex / jnp.sum(ex, axis=0, keepdims=True)


def _outproj_kernel_aliased(x_ref, f_ref, dat_ref, hf_ref, hb_ref, mo_ref, ada_ref, mg_ref, wo_ref, wod_ref,
                            g2_ref, wr_ref, hlp_ref, xo_ref, hl_ref, pt_ref, *, is_ctx):
    del hlp_ref
    _outproj_kernel(x_ref, f_ref, dat_ref, hf_ref, hb_ref, mo_ref, ada_ref, mg_ref, wo_ref, wod_ref, g2_ref,
                    wr_ref, xo_ref, hl_ref, pt_ref, is_ctx=is_ctx)


def _outproj(xu, f, daT, hf, hb, mo, ada_l, mg, wo, wod, g2, wrp, hl_prev, *, t0, ntl, is_ctx):
    B, NT, _ = xu.shape
    n = ntl * TOK
    tile = 2 * TOK if n % (2 * TOK) == 0 else TOK
    o = t0 * TOK // tile
    tok = lambda w: pl.BlockSpec((1, tile, w), lambda b, t: (b, o + t, 0))
    trs = lambda r: pl.BlockSpec((1, r, tile), lambda b, t: (b, 0, o + t))
    loc = lambda w: pl.BlockSpec((1, tile, w), lambda b, t: (b, t, 0))
    full = lambda a: pl.BlockSpec(a.shape, lambda b, t: (0,) * a.ndim)
    return pl.pallas_call(
        functools.partial(_outproj_kernel_aliased, is_ctx=is_ctx),
        out_shape=[jax.ShapeDtypeStruct(xu.shape, F32), jax.ShapeDtypeStruct((B, NT, D), BF16),
                   jax.ShapeDtypeStruct((B, N_EXPERTS, n), F32)],
        grid=(B, n // tile),
        in_specs=[tok(D), loc(F_WIDTH), pl.BlockSpec((1, DA_WIDTH, tile), lambda b, t: (b, 0, t)),
                  trs(MP_WIDTH), trs(MP_WIDTH), trs(MP_WIDTH),
                  full(ada_l), full(mg), full(wo), full(wod), full(g2), full(wrp),
                  pl.BlockSpec(memory_space=pl.ANY)],
        out_specs=[tok(D), tok(D), pl.BlockSpec((1, N_EXPERTS, tile), lambda b, t: (b, 0, t))],
        input_output_aliases={0: 0, 12: 1},
        compiler_params=_cp(("arbitrary", "arbitrary"), VMEM_LIMIT),
        name="outproj_norm2_router",
    )(xu, f, daT, hf, hb, mo, ada_l, mg, wo, wod, g2, wrp, hl_prev)


def _select_kernel(p_ref, rank_ref, offs_ref, *, n, cap):
    p = p_ref[0]
    xi = pltpu.bitcast(p, jnp.int32)

    def body(i, lo):
        cand = lo | jnp.left_shift(jnp.int32(1), 30 - i)
        cnt = jnp.sum(jnp.where(xi >= cand, 1.0, 0.0), axis=1, keepdims=True)
        return jnp.where(cnt >= cap, cand, lo)

    thr = lax.fori_loop(0, 31, body, jnp.zeros((N_EXPERTS, 1), jnp.int32))
    nb = n // TOK
    rows = lax.broadcasted_iota(jnp.int32, (n, 128), 0)
    cols = lax.broadcasted_iota(jnp.int32, (n, 128), 1)
    blk_ind = jnp.where((rows // TOK) == cols, 1.0, 0.0).astype(BF16)
    u128 = jnp.where(lax.broadcasted_iota(jnp.int32, (128, 128), 0)
                     < lax.broadcasted_iota(jnp.int32, (128, 128), 1), 1.0, 0.0).astype(BF16)
    utok = jnp.where(lax.broadcasted_iota(jnp.int32, (TOK, TOK), 0)
                     < lax.broadcasted_iota(jnp.int32, (TOK, TOK), 1), 1.0, 0.0).astype(BF16)

    def prefix(mf):
        mb = mf.astype(BF16)
        counts = _dot(mb, blk_ind)
        offs = _dot(counts.astype(BF16), u128)
        pieces = [_dot(mb[:, TOK * j:TOK * (j + 1)], utok) + offs[:, j:j + 1] for j in range(nb)]
        return (jnp.concatenate(pieces, axis=1) if nb > 1 else pieces[0]), offs

    gt = xi > thr
    eq = xi == thr
    need = cap - jnp.sum(jnp.where(gt, 1.0, 0.0), axis=1, keepdims=True)
    rank_eq, _ = prefix(jnp.where(eq, 1.0, 0.0))
    sel = gt | (eq & (rank_eq < need))
    rank, offs = prefix(jnp.where(sel, 1.0, 0.0))
    rank_ref[0] = jnp.where(sel, rank, -1.0)
    offs_ref[0] = offs.astype(jnp.int32)


def _select(pt, *, cap):
    B, _, n = pt.shape
    return pl.pallas_call(
        functools.partial(_select_kernel, n=n, cap=cap),
        out_shape=[jax.ShapeDtypeStruct((B, N_EXPERTS, n), F32),
                   jax.ShapeDtypeStruct((B, N_EXPERTS, 128), jnp.int32)],
        grid=(B,),
        in_specs=[pl.BlockSpec((1, N_EXPERTS, n), lambda b: (b, 0, 0))],
        out_specs=[pl.BlockSpec((1, N_EXPERTS, n), lambda b: (b, 0, 0)),
                   pl.BlockSpec((1, N_EXPERTS, 128), lambda b: (b, 0, 0))],
        compiler_params=_cp(("arbitrary",), VMEM_LIMIT),
        name="expert_choice_select",
    )(pt)


def _gather_kernel(offs_ref, h_ref, rank_ref, prob_ref, o_ref, gate_ref, *, eg, per):
    b, g, tb = pl.program_id(0), pl.program_id(1), pl.program_id(2)

    @pl.when(tb == 0)
    def _():
        o_ref[...] = jnp.zeros(o_ref.shape, BF16)
        gate_ref[...] = jnp.zeros(gate_ref.shape, F32)

    cap_pad = o_ref.shape[2]
    half = SLOT // 2
    slot = lax.broadcasted_iota(jnp.int32, (SLOT, TOK), 0).astype(F32)

    def add_rows(i, r, p, h, base, start=None):
        hit = r == slot + base.astype(F32)
        if start is not None:
            hit = hit & (r >= start.astype(F32))
        rows = _dot(jnp.where(hit, 1.0, 0.0).astype(BF16), h).astype(BF16)
        o_ref[0, i, pl.ds(base, SLOT), :] = o_ref[0, i, pl.ds(base, SLOT), :] + rows
        gate_ref[0, i, pl.ds(base, SLOT), :] = (gate_ref[0, i, pl.ds(base, SLOT), :]
                                                + jnp.sum(jnp.where(hit, p, 0.0), axis=1, keepdims=True))

    def operands(s, i):
        tk = slice(TOK * s, TOK * (s + 1))
        e = g * eg + i
        return rank_ref[0, pl.ds(e, 1), tk], prob_ref[0, pl.ds(e, 1), tk], h_ref[0, tk, :]

    ends, his = {}, {}
    for s in range(per):
        for i in range(eg):
            e = g * eg + i
            lo = offs_ref[b, e, tb * per + s]
            his[s, i] = offs_ref[b, e, tb * per + s + 1]
            base = pl.multiple_of(jnp.minimum((lo // half) * half, cap_pad - SLOT), half)
            add_rows(i, *operands(s, i), base)
            ends[s, i] = base + SLOT

    for s in range(per):
        for i in range(eg):
            @pl.when(his[s, i] > ends[s, i])
            def _(s=s, i=i):
                def body(t, carry):
                    start = ends[s, i] + t * SLOT
                    base = pl.multiple_of(jnp.minimum(start, cap_pad - SLOT), half)
                    add_rows(i, *operands(s, i), base, start)
                    return carry

                lax.fori_loop(0, (his[s, i] - ends[s, i] + SLOT - 1) // SLOT, body, 0)


def _gather(offs, hl, rank, pt, *, tb_tok, tb0, n, cap_pad, eg):
    B = hl.shape[0]
    per = tb_tok // TOK
    return pl.pallas_call(
        functools.partial(_gather_kernel, eg=eg, per=per),
        out_shape=[jax.ShapeDtypeStruct((B, N_EXPERTS, cap_pad, D), BF16),
                   jax.ShapeDtypeStruct((B, N_EXPERTS, cap_pad, 1), F32)],
        grid_spec=pltpu.PrefetchScalarGridSpec(
            num_scalar_prefetch=1,
            grid=(B, N_EXPERTS // eg, n // tb_tok),
            in_specs=[pl.BlockSpec((1, tb_tok, D), lambda b, g, t, o: (b, tb0 + t, 0)),
                      pl.BlockSpec((1, N_EXPERTS, tb_tok), lambda b, g, t, o: (b, 0, t)),
                      pl.BlockSpec((1, N_EXPERTS, tb_tok), lambda b, g, t, o: (b, 0, t))],
            out_specs=[pl.BlockSpec((1, eg, cap_pad, D), lambda b, g, t, o: (b, g, 0, 0)),
                       pl.BlockSpec((1, eg, cap_pad, 1), lambda b, g, t, o: (b, g, 0, 0))]),
        compiler_params=_cp(("arbitrary",) * 3, VMEM_LIMIT),
        name="expert_gather",
    )(offs, hl, rank, pt)


FFN_ROWS = 1024


def _ffn_kernel(x_ref, gate_ref, w1_ref, w3_ref, w2_ref, y_ref, acc_ref):
    f = pl.program_id(2)

    @pl.when(f == 0)
    def _():
        acc_ref[...] = jnp.zeros(acc_ref.shape, F32)

    w1 = w1_ref[0, 0].astype(BF16)
    w3 = w3_ref[0, 0].astype(BF16)
    w2 = w2_ref[0, 0].astype(BF16)
    mb, _, cap_pad, _ = x_ref.shape
    rows = min(FFN_ROWS, cap_pad)
    for i in range(mb):
        for r in range(0, cap_pad, rows):
            x = x_ref[i, 0, r:r + rows, :]
            hid = (_silu(_dot(x, w1)) * _dot(x, w3)).astype(BF16)
            acc_ref[i * cap_pad + r:i * cap_pad + r + rows, :] += _dot(hid, w2)

    @pl.when(f == pl.num_programs(2) - 1)
    def _():
        gate = gate_ref[...].reshape(-1, 1)
        y_ref[...] = (acc_ref[...] * gate).astype(BF16).reshape(y_ref.shape)


def _ffn(xs, gates, w1, w3, w2, *, layer, mb, tf):
    B, E, cap_pad, _ = xs.shape
    return pl.pallas_call(
        _ffn_kernel,
        out_shape=jax.ShapeDtypeStruct(xs.shape, BF16),
        grid=(E, B // mb, D_FF // tf),
        in_specs=[pl.BlockSpec((mb, 1, cap_pad, D), lambda e, m, f: (m, e, 0, 0)),
                  pl.BlockSpec((mb, 1, cap_pad, 1), lambda e, m, f: (m, e, 0, 0)),
                  pl.BlockSpec((1, 1, D, tf), lambda e, m, f: (layer, e, 0, f)),
                  pl.BlockSpec((1, 1, D, tf), lambda e, m, f: (layer, e, 0, f)),
                  pl.BlockSpec((1, 1, tf, D), lambda e, m, f: (layer, e, f, 0))],
        out_specs=pl.BlockSpec((mb, 1, cap_pad, D), lambda e, m, f: (m, e, 0, 0)),
        scratch_shapes=[pltpu.VMEM((mb * cap_pad, D), F32)],
        compiler_params=_cp(("arbitrary",) * 3, VMEM_LIMIT),
        name="expert_ffn",
    )(xs, gates, w1, w3, w2)


CCOL = 512


def _combine_kernel(offs_ref, x_ref, y_ref, rankc_ref, ada_ref, o_ref, tot_scr, *, per, is_ctx):
    b, tb = pl.program_id(0), pl.program_id(2)
    gt2 = ada_ref[CTX_ROW if is_ctx else b][5:6]
    rc_all = rankc_ref[0]
    cap_pad = y_ref.shape[2]
    half = SLOT // 2
    slot = lax.broadcasted_iota(jnp.int32, (1, SLOT), 1).astype(F32)

    ends, his = {}, {}
    for s in range(per):
        tk = slice(TOK * s, TOK * (s + 1))
        total = jnp.zeros((TOK, tot_scr.shape[1]), F32)
        for e0 in range(0, N_EXPERTS, 2):
            hots, rows = [], []
            for e in (e0, e0 + 1):
                lo = offs_ref[b, e, tb * per + s]
                his[s, e] = offs_ref[b, e, tb * per + s + 1]
                base = pl.multiple_of(jnp.minimum((lo // half) * half, cap_pad - SLOT), half)
                ends[s, e] = base + SLOT
                hots.append(jnp.where(rc_all[tk, e:e + 1] == slot + base.astype(F32), 1.0, 0.0).astype(BF16))
                rows.append(y_ref[0, e, pl.ds(base, SLOT), :])
            total = total + _dot(jnp.concatenate(hots, axis=1), jnp.concatenate(rows, axis=0))
        tot_scr[tk] = total

    for s in range(per):
        tk = slice(TOK * s, TOK * (s + 1))
        for e in range(N_EXPERTS):
            @pl.when(his[s, e] > ends[s, e])
            def _(s=s, e=e, tk=tk):
                rc = rc_all[tk, e:e + 1]

                def body(t, carry):
                    start = ends[s, e] + t * SLOT
                    base = pl.multiple_of(jnp.minimum(start, cap_pad - SLOT), half)
                    hit = (rc == slot + base.astype(F32)) & (rc >= start.astype(F32))
                    tot_scr[tk] += _dot(jnp.where(hit, 1.0, 0.0).astype(BF16), y_ref[0, e, pl.ds(base, SLOT), :])
                    return carry

                lax.fori_loop(0, (his[s, e] - ends[s, e] + SLOT - 1) // SLOT, body, 0)

    o_ref[0] = x_ref[0] + gt2 * tot_scr[...]


def _combine(offs, xu, ys, rank_c, ada_l, *, tb_tok, tb0, n, is_ctx):
    B = xu.shape[0]
    cap_pad = ys.shape[2]
    per = tb_tok // TOK
    return pl.pallas_call(
        functools.partial(_combine_kernel, per=per, is_ctx=is_ctx),
        out_shape=jax.ShapeDtypeStruct(xu.shape, F32),
        grid_spec=pltpu.PrefetchScalarGridSpec(
            num_scalar_prefetch=1,
            grid=(B, D // CCOL, n // tb_tok),
            in_specs=[pl.BlockSpec((1, tb_tok, CCOL), lambda b, c, t, o: (b, tb0 + t, c)),
                      pl.BlockSpec((1, N_EXPERTS, cap_pad, CCOL), lambda b, c, t, o: (b, 0, 0, c),
                                   pipeline_mode=pl.Buffered(1)),
                      pl.BlockSpec((1, tb_tok, N_EXPERTS), lambda b, c, t, o: (b, t, 0)),
                      pl.BlockSpec((ADA_ROWS, ADA_CHUNKS, CCOL), lambda b, c, t, o: (0, 0, c))],
            out_specs=pl.BlockSpec((1, tb_tok, CCOL), lambda b, c, t, o: (b, tb0 + t, c)),
            scratch_shapes=[pltpu.VMEM((tb_tok, CCOL), F32)]),
        input_output_aliases={1: 0},
        compiler_params=_cp(("arbitrary",) * 3, VMEM_LIMIT),
        name="expert_combine",
    )(offs, xu, ys, rank_c, ada_l)


def _moe(xu, hl, pt, ada_l, w1, w3, w2, *, layer, row0, is_ctx):
    B, _, n = pt.shape
    cap = EC_CAPACITY * n // N_EXPERTS
    cap_pad = -(-cap // SLOT) * SLOT
    nb = n // TOK
    rank, offs = _select(pt, cap=cap)
    offs = offs[:, :, :nb + 1]
    gt = min(n, MOE_TOK)
    ct = min(n, COMBINE_TOK)
    xs, gates = _gather(offs, hl, rank, pt, tb_tok=gt, tb0=row0 // gt, n=n, cap_pad=cap_pad, eg=GATHER_EXPERTS)
    mb = 2 if (B % 2 == 0 and cap_pad >= 1024) else (B if cap_pad < 1024 else 1)
    ys = _ffn(xs, gates, w1, w3, w2, layer=layer, mb=mb, tf=FFN_TF)
    rank_c = jnp.swapaxes(rank, 1, 2)
    return _combine(offs, xu, ys, rank_c, ada_l, tb_tok=ct, tb0=row0 // ct, n=n, is_ctx=is_ctx)


def _final_kernel(x_ref, g_ref, o_ref):
    x = x_ref[0]
    r = lax.rsqrt(jnp.mean(x * x, axis=-1, keepdims=True) + EPS)
    o_ref[0] = (x * r) * g_ref[...]


def _final_norm(xu, g, *, n):
    B = xu.shape[0]
    tm = MOE_TOK
    return pl.pallas_call(
        _final_kernel,
        out_shape=jax.ShapeDtypeStruct((B, n, D), F32),
        grid=(B, n // tm),
        in_specs=[pl.BlockSpec((1, tm, D), lambda b, t: (b, t, 0)),
                  pl.BlockSpec((1, D), lambda b, t: (0, 0))],
        out_specs=pl.BlockSpec((1, tm, D), lambda b, t: (b, t, 0)),
        compiler_params=_cp(("arbitrary", "arbitrary")),
        name="final_norm",
    )(xu, g)


def _rope_tables(n, ctx):
    rows = n // GRID_W
    t_row = jnp.repeat(jnp.arange(rows), GRID_W)
    t_col = jnp.tile(jnp.arange(GRID_W), rows)
    nf = DA_DIM // 4
    inv = ROPE_THETA ** (-jnp.arange(nf, dtype=F32) / nf)
    ar = t_row[:, None].astype(F32) * inv
    ac = t_col[:, None].astype(F32) * inv
    ang = jnp.concatenate([ar, ar, ac, ac], axis=-1)
    sign = jnp.where((jnp.arange(DA_DIM) % 16) < 8, -1.0, 1.0).astype(F32)
    cos = jnp.concatenate([jnp.cos(ang), jnp.ones((ctx, DA_DIM), F32)], axis=0)
    sin = jnp.concatenate([jnp.sin(ang) * sign, jnp.zeros((ctx, DA_DIM), F32)], axis=0)
    return jnp.tile(cos, (1, 128 // DA_DIM)), jnp.tile(sin, (1, 128 // DA_DIM))


def _pad_heads_cols(w):
    lead = w.shape[:-1]
    w = w.reshape(lead + (M_HEADS, M_DIM))
    w = jnp.pad(w, [(0, 0)] * len(lead) + [(0, 0), (0, M_PAD - M_DIM)])
    return w.reshape(lead + (MP_WIDTH,))


def _kv_tile(nt):
    for parts in range(1, nt // LANES + 1):
        if nt % parts == 0 and (nt // parts) % LANES == 0 and nt // parts <= KV_TILE_MAX:
            return nt // parts
    raise ValueError(nt)


def kernel(x, c, ctx, c_ctx, ada_w, ada_b, norm1_g, norm2_g, w_in, four_w, m_conv_w, m_conv_b, m_gate_b,
           m_norm_g, d_lam, d_norm_g, w_out, router_w, exp_w1, exp_w3, exp_w2, final_g):
    B, N, _ = x.shape
    CTX = ctx.shape[1]
    depth = w_in.shape[0]
    assert CTX == TOK and N % (FFT_N1 * TOK) == 0 and N % Q_TILE == 0 and B <= CTX_ROW
    NT = N + CTX
    PAD = -NT % MOE_TOK
    n_lat = N // TOK
    n2 = N // FFT_N1

    xu = jnp.concatenate([x, ctx, jnp.zeros((B, PAD, D), F32)], axis=1)
    cvecs = jnp.zeros((ADA_ROWS, D), F32).at[:B].set(c).at[CTX_ROW].set(c_ctx)
    ada = _adaln(cvecs, ada_w, ada_b).reshape(depth, ADA_ROWS, ADA_CHUNKS, D)
    cos_t, sin_t = _rope_tables(N, CTX + PAD)
    tabs = _fourier_tables(N, CTX)
    tk = _kv_tile(NT)
    tq = Q_TILE

    hl = jnp.zeros((B, NT + PAD, D), BF16)
    for layer in range(depth):
        ctx_out = layer < depth - 1
        lam_init = 0.8 - 0.6 * math.exp(-0.3 * layer)
        w = w_in[layer]
        wm = jnp.concatenate([w[:, OFF_F:OFF_DQ], w[:, OFF_DQ:OFF_MO], w[:, OFF_DK:OFF_DV]], axis=1).astype(BF16)
        wvt = jnp.concatenate([w[:, OFF_DV:OFF_MV], _pad_heads_cols(w[:, OFF_MO:OFF_MQ]),
                               _pad_heads_cols(w[:, OFF_MV:OFF_G])], axis=1).T.astype(BF16)
        wg = jnp.pad(w[:, OFF_G:], ((0, 0), (0, LANES - N_GATES)))
        wc = jnp.concatenate([_pad_heads_cols(w[:, OFF_MQ:OFF_MK]), _pad_heads_cols(w[:, OFF_MK:OFF_DK])],
                             axis=1).astype(BF16)
        gb = jnp.pad(m_gate_b[layer], (0, LANES - N_GATES)).reshape(1, LANES)
        cw = jnp.concatenate([_pad_heads_cols(m_conv_w[layer][:, :M_WIDTH]),
                              _pad_heads_cols(m_conv_w[layer][:, M_WIDTH:])], axis=1)
        cb = jnp.concatenate([_pad_heads_cols(m_conv_b[layer][:M_WIDTH]),
                              _pad_heads_cols(m_conv_b[layer][M_WIDTH:])]).reshape(1, 2 * MP_WIDTH)
        ada_l = ada[layer]

        y4, dq, dk, dvT, mo, mq, mk, mv, gl, glT = _inproj(
            xu, ada_l, norm1_g[layer].reshape(1, D), wm, wc, wvt, wg, gb, tabs["cs"], cos_t, sin_t, cw, cb,
            n_lat=n_lat, n2=n2)

        wblk = jnp.zeros((F_WIDTH, F_WIDTH), F32)
        for g in range(F_GROUPS):
            wblk = wblk.at[F_GDIM * g:F_GDIM * (g + 1), F_GDIM * g:F_GDIM * (g + 1)].set(four_w[layer, g])
        f_l, f_c = _fourier(y4, tabs, wblk.astype(BF16), n=N, ctx=CTX, with_ctx=ctx_out)

        dlam = d_lam[layer]
        g2 = d_norm_g[layer].reshape(DA_VDIM, 1)
        da_l = _attention(dq, dk, dvT, dlam, g2, lam_init=lam_init, tq=tq, q0=0, nq=N // tq,
                          tk=tk, k0=0, nk=NT // tk)

        hf, hb = _mlstm(mq, mk, mv, gl, glT, n_lat=n_lat)

        mg = _pad_heads_cols(m_norm_g[layer]).reshape(MP_WIDTH, 1)
        wol = w_out[layer]
        wo = wol[:F_WIDTH].astype(BF16)
        wod = jnp.concatenate([wol[F_WIDTH:F_WIDTH + DA_WIDTH],
                               jnp.pad(wol[F_WIDTH + DA_WIDTH:].reshape(M_HEADS, M_DIM, D),
                                       ((0, 0), (0, M_PAD - M_DIM), (0, 0))).reshape(MP_WIDTH, D)],
                              axis=0).astype(BF16)
        g2n = norm2_g[layer].reshape(1, D)
        wrp = jnp.pad(router_w[layer], ((0, 0), (0, LANES - N_EXPERTS)))
        xu, hl, pt_l = _outproj(xu, f_l, da_l, hf, hb, mo, ada_l, mg, wo, wod, g2n, wrp, hl,
                                t0=0, ntl=n_lat, is_ctx=False)
        if ctx_out:
            da_c = _attention(dq, dk, dvT, dlam, g2, lam_init=lam_init, tq=TOK, q0=n_lat, nq=1,
                              tk=TOK, k0=n_lat, nk=1)
            xu, hl, pt_c = _outproj(xu, f_c, da_c, hf, hb, mo, ada_l, mg, wo, wod, g2n, wrp, hl,
                                    t0=n_lat, ntl=1, is_ctx=True)

        xu = _moe(xu, hl, pt_l, ada_l, exp_w1, exp_w3, exp_w2, layer=layer, row0=0, is_ctx=False)
        if ctx_out:
            xu = _moe(xu, hl, pt_c, ada_l, exp_w1, exp_w3, exp_w2, layer=layer, row0=N, is_ctx=True)

    return _final_norm(xu, final_g.reshape(1, D), n=N)
```

```python
import functools
import math

import numpy as np
import jax
import jax.numpy as jnp
from jax import lax
from jax.experimental import pallas as pl
from jax.experimental.pallas import tpu as pltpu

F32 = jnp.float32
BF16 = jnp.bfloat16
HI = lax.Precision.HIGHEST

D = 1024
EPS = 1e-6
GRID_W = 64
ROPE_THETA = 10000.0
F_GROUPS, F_GDIM = 4, 64
F_WIDTH = F_GROUPS * F_GDIM
DA_HEADS, DA_DIM = 6, 32
DA_VDIM = 2 * DA_DIM
DA_WIDTH = DA_HEADS * DA_VDIM
M_HEADS, M_DIM = 4, 96
M_WIDTH = M_HEADS * M_DIM
M_PAD = 128
MP_WIDTH = M_HEADS * M_PAD
N_GATES = 4 * M_HEADS
N_EXPERTS = 16
EC_CAPACITY = 2
D_FF = 2 * D
ADA_CHUNKS = 6
ADA_ROWS = 8
CTX_ROW = 4

LANES = 128
MXU_DIM = 256
V7X_VMEM_BYTES = 64 * 1024 * 1024

TOK = MXU_DIM
FFT_N1 = 16
SLOT = LANES
Q_TILE = 1024
KV_TILE_MAX = 2816
MOE_TOK = 512
COMBINE_TOK = 512
FFN_TF = 512
GATHER_EXPERTS = 8
NEG = -1e30

OFF_F = 0
OFF_DQ = OFF_F + F_WIDTH
OFF_MO = OFF_DQ + 2 * DA_HEADS * DA_DIM
OFF_MQ = OFF_MO + M_WIDTH
OFF_MK = OFF_MQ + M_WIDTH
OFF_DK = OFF_MK + M_WIDTH
OFF_DV = OFF_DK + 2 * DA_HEADS * DA_DIM
OFF_MV = OFF_DV + DA_HEADS * DA_VDIM
OFF_G = OFF_MV + M_WIDTH

VMEM_LIMIT = V7X_VMEM_BYTES * 7 // 8


def _cp(sem, vmem=None):
    return pltpu.CompilerParams(dimension_semantics=sem, vmem_limit_bytes=vmem)


def _sigmoid(x):
    return 1.0 / (1.0 + jnp.exp(-x))


def _silu(x):
    return x * _sigmoid(x)


def _dot(a, b, precision=None):
    return jnp.dot(a, b, preferred_element_type=F32, precision=precision)


def _split(a):
    hi = a.astype(BF16)
    return hi, (a - hi.astype(F32)).astype(BF16)


def _dot3(a, b):
    a_hi, a_lo = a if isinstance(a, tuple) else _split(a)
    b_hi, b_lo = b if isinstance(b, tuple) else _split(b)
    return _dot(a_hi, b_hi) + _dot(a_hi, b_lo) + _dot(a_lo, b_hi)


def _split3(a):
    hi = a.astype(BF16)
    r = a - hi.astype(F32)
    mid = r.astype(BF16)
    return hi, mid, (r - mid.astype(F32)).astype(BF16)


def _dot_nt(a, b, precision=None):
    return lax.dot_general(a, b, (((1,), (1,)), ((), ())), preferred_element_type=F32,
                           precision=precision)


def _ada_kernel(c_ref, w_ref, b_ref, o_ref):
    c = c_ref[...]
    o_ref[0] = _dot(_silu(c), w_ref[0], HI) + b_ref[0]


def _adaln(cvecs, ada_w, ada_b):
    depth = ada_w.shape[0]
    tn = 1536
    return pl.pallas_call(
        _ada_kernel,
        out_shape=jax.ShapeDtypeStruct((depth, ADA_ROWS, ADA_CHUNKS * D), F32),
        grid=(depth, ADA_CHUNKS * D // tn),
        in_specs=[pl.BlockSpec((ADA_ROWS, D), lambda l, j: (0, 0)),
                  pl.BlockSpec((1, D, tn), lambda l, j: (l, 0, j)),
                  pl.BlockSpec((1, 1, tn), lambda l, j: (l, 0, j))],
        out_specs=pl.BlockSpec((1, 8, tn), lambda l, j: (l, 0, j)),
        compiler_params=_cp(("arbitrary", "arbitrary")),
        name="adaln",
    )(cvecs, ada_w, ada_b.reshape(depth, 1, ADA_CHUNKS * D))


def _inproj_kernel(x_ref, xp_ref, xn_ref, ada_ref, g_ref, wm_ref, wc_ref, wvt_ref, wg_ref, gb_ref, cs_ref,
                   cos_ref, sin_ref, cw_ref, cb_ref,
                   y_ref, dq_ref, dk_ref, dvt_ref, mo_ref, mq_ref, mk_ref, mv_ref, gl_ref, glt_ref, *, n_lat):
    b = pl.program_id(0)
    t = pl.program_id(1)
    n_tiles = pl.num_programs(1)
    is_ctx = t >= n_lat
    row = jnp.where(is_ctx, CTX_ROW, b)
    mod = ada_ref[row]
    sh, sc = mod[0:1], mod[1:2]

    xa = jnp.concatenate([xp_ref[0], x_ref[0], xn_ref[0]], axis=0)
    r = lax.rsqrt(jnp.mean(xa * xa, axis=-1, keepdims=True) + EPS)
    ha = (xa * r) * g_ref[...] * (1.0 + sc) + sh
    h = ha[8:8 + TOK]
    hb = h.astype(BF16)

    pm = _dot(hb, wm_ref[...])
    o = 0
    pf = pm[:, o:o + F_WIDTH]; o += F_WIDTH
    q = pm[:, o:o + DA_WIDTH]; o += DA_WIDTH
    k = pm[:, o:o + DA_WIDTH]; o += DA_WIDTH
    pt = _dot_nt(wvt_ref[...], hb)
    dvt_ref[0] = pt[:DA_WIDTH].astype(BF16)
    mo_ref[0] = pt[DA_WIDTH:DA_WIDTH + MP_WIDTH].astype(BF16)
    mv_ref[0] = pt[DA_WIDTH + MP_WIDTH:].astype(BF16)

    y_ref[0, 0] = _dot3(pf, cs_ref[...])

    cos = cos_ref[...]
    sin = sin_ref[...]
    lane = lax.broadcasted_iota(jnp.int32, (1, 128), 1)
    low = (lane % 16) < 8

    def rope(z):
        parts = []
        for c in range(DA_WIDTH // 128):
            zc = z[:, 128 * c:128 * (c + 1)]
            rot = jnp.where(low, pltpu.roll(zc, 120, 1), pltpu.roll(zc, 8, 1))
            parts.append(zc * cos + rot * sin)
        return jnp.concatenate(parts, axis=1)

    dq_ref[0] = (rope(q) * (DA_DIM ** -0.5 * math.log2(math.e))).astype(BF16)
    dk_ref[0] = rope(k).astype(BF16)

    gpre = _dot3(h, wg_ref[...]) + gb_ref[...]
    is_forget = (lax.broadcasted_iota(jnp.int32, (1, LANES), 1) % 8) >= 4
    logsig = jnp.minimum(gpre, 0.0) - jnp.log(1.0 + jnp.exp(-jnp.abs(gpre)))
    gl = jnp.where(is_forget, logsig, gpre)
    gl_ref[0] = gl
    glt_ref[0] = gl.T[:N_GATES]

    pc = _dot(ha.astype(BF16), wc_ref[...])
    first = (t == 0) | (t == n_lat)
    last = (t == n_lat - 1) | (t == n_tiles - 1)
    ridx = lax.broadcasted_iota(jnp.int32, (TOK + 16, 1), 0)
    pc = jnp.where(((ridx < 8) & first) | ((ridx >= TOK + 8) & last), 0.0, pc)
    cw = cw_ref[...]
    conv = cb_ref[...] + pc[7:7 + TOK] * cw[0:1] + pc[8:8 + TOK] * cw[1:2] + pc[9:9 + TOK] * cw[2:3]
    act = _silu(conv)
    mq_ref[0] = act[:, :MP_WIDTH].astype(BF16)
    mk_ref[0] = (act[:, MP_WIDTH:] * (M_DIM ** -0.5)).astype(BF16)


def _inproj(xu, ada_l, g1, wm, wc, wvt, wg, gb, cs, cos_t, sin_t, cw, cb, *, n_lat, n2):
    B, NT, _ = xu.shape
    nt = n_lat + 1
    rper = n2 // TOK
    tok3 = lambda w: pl.BlockSpec((1, TOK, w), lambda b, t: (b, t, 0))
    full = lambda a: pl.BlockSpec(a.shape, lambda b, t: (0,) * a.ndim)
    nb8 = NT // 8
    outs = [jax.ShapeDtypeStruct((B, 2 * FFT_N1, n2, 2 * F_WIDTH), F32)]
    nq = -(-NT // Q_TILE) * Q_TILE
    outs += [jax.ShapeDtypeStruct((B, nq, DA_WIDTH), BF16), jax.ShapeDtypeStruct((B, nt * TOK, DA_WIDTH), BF16)]
    outs += [jax.ShapeDtypeStruct((B, DA_WIDTH, nt * TOK), BF16)]
    trs = lambda r: pl.BlockSpec((1, r, TOK), lambda b, t: (b, 0, t))
    trp = jax.ShapeDtypeStruct((B, MP_WIDTH, NT), BF16)
    outs += [trp, jax.ShapeDtypeStruct((B, NT, MP_WIDTH), BF16), jax.ShapeDtypeStruct((B, NT, MP_WIDTH), BF16), trp]
    outs += [jax.ShapeDtypeStruct((B, NT, 128), F32), jax.ShapeDtypeStruct((B, N_GATES, nt * TOK), F32)]
    out_specs = [pl.BlockSpec((1, 1, TOK, 2 * F_WIDTH), lambda b, t: (b, t // rper, t % rper, 0))]
    out_specs += [tok3(DA_WIDTH)] * 2 + [trs(DA_WIDTH)]
    out_specs += [trs(MP_WIDTH), tok3(MP_WIDTH), tok3(MP_WIDTH), trs(MP_WIDTH)]
    out_specs += [tok3(128), pl.BlockSpec((1, N_GATES, TOK), lambda b, t: (b, 0, t))]
    return pl.pallas_call(
        functools.partial(_inproj_kernel, n_lat=n_lat),
        out_shape=outs,
        grid=(B, nt),
        in_specs=[tok3(D),
                  pl.BlockSpec((1, 8, D), lambda b, t: (b, jnp.maximum(t * (TOK // 8) - 1, 0), 0)),
                  pl.BlockSpec((1, 8, D), lambda b, t: (b, jnp.minimum((t + 1) * (TOK // 8), nb8 - 1), 0)),
                  full(ada_l), full(g1), full(wm), full(wc), full(wvt), full(wg), full(gb), full(cs),
                  pl.BlockSpec((TOK, LANES), lambda b, t: (t, 0)),
                  pl.BlockSpec((TOK, LANES), lambda b, t: (t, 0)),
                  full(cw), full(cb)],
        out_specs=out_specs,
        compiler_params=_cp(("arbitrary", "arbitrary"), VMEM_LIMIT),
        name="norm1_inproj",
    )(xu, xu, xu, ada_l, g1, wm, wc, wvt, wg, gb, cs, cos_t, sin_t, cw, cb)


def _fft1_kernel(y_ref, kc_ref, ks_ref, tc_ref, ts_ref, o_ref, *, groups):
    kc = _split(kc_ref[...])
    ks = _split(ks_ref[...])
    for g in range(groups):
        blk = _split(y_ref[0, :, 8 * g:8 * (g + 1), :].reshape(FFT_N1 * 8, 2 * F_WIDTH))
        p = _dot3(kc, blk)
        q = _dot3(ks, blk)
        ar = p[:, :F_WIDTH] - q[:, F_WIDTH:]
        ai = -p[:, F_WIDTH:] - q[:, :F_WIDTH]
        tc = tc_ref[128 * g:128 * (g + 1), :]
        ts = ts_ref[128 * g:128 * (g + 1), :]
        tc = jnp.concatenate([tc, tc], axis=1)
        ts = jnp.concatenate([ts, ts], axis=1)
        br = ar * tc + ai * ts
        bi = ai * tc - ar * ts
        o_ref[0, :, 8 * g:8 * (g + 1), :] = jnp.concatenate([br, bi], axis=1).reshape(FFT_N1, 8, 2 * F_WIDTH)


def _fft2_kernel(b_ref, c2_ref, s2_ref, wb_ref, perm_ref, o_ref, r_scr, *, n2):
    c2 = _split(c2_ref[...])
    s2 = _split(s2_ref[...])
    for i in range(8):
        blk = b_ref[0, i]
        xr = _dot3(c2, blk[:, :F_WIDTH]) + _dot3(s2, blk[:, F_WIDTH:])
        r_scr[i] = _dot(xr.astype(BF16), wb_ref[...]).astype(BF16)
    for t in range(n2 // 32):
        rows = jnp.concatenate([r_scr[i, 32 * t:32 * (t + 1), :] for i in range(8)], axis=0)
        o_ref[0, 32 * t:32 * (t + 1), :, :] = _dot(perm_ref[...], rows).reshape(32, 8, F_WIDTH)


def _fftc_kernel(y_ref, c_ref, s_ref, wb_ref, o_ref):
    y = y_ref[0, 0]
    z = _dot3(c_ref[...], y[:, :F_WIDTH]) - _dot3(s_ref[...], y[:, F_WIDTH:])
    o_ref[0] = _dot(z.astype(BF16), wb_ref[...])


def _fourier_tables(n, ctx):
    n1, n2 = FFT_N1, n // FFT_N1
    a = np.arange(n1)
    ang1 = 2 * np.pi * np.outer(a, a) / n1
    eye8 = np.eye(8)
    kc = np.kron(np.cos(ang1), eye8)
    ks = np.kron(np.sin(ang1), eye8)
    n2i = np.arange(n2).reshape(n2 // 8, 1, 8)
    k1 = np.arange(n1).reshape(1, n1, 1)
    angt = (2 * np.pi * n2i * k1 / n).reshape(-1, 1)
    tc = np.broadcast_to(np.cos(angt), (n2 // 8 * 128, 128))
    ts = np.broadcast_to(np.sin(angt), (n2 // 8 * 128, 128))
    b = np.arange(n2)
    ang2 = 2 * np.pi * np.outer(b, b) / n2
    c2 = np.cos(ang2) / math.sqrt(n)
    s2 = np.sin(ang2) / math.sqrt(n)
    perm = np.zeros((256, 256))
    for kk in range(8):
        for j in range(32):
            perm[j * 8 + kk, kk * 32 + j] = 1.0
    cc = np.arange(ctx)
    angc = 2 * np.pi * np.outer(cc, cc) / ctx
    cctx = np.cos(angc) / math.sqrt(ctx)
    sctx = np.sin(angc) / math.sqrt(ctx)
    ch = np.arange(F_GDIM)
    angch = 2 * np.pi * np.outer(ch, ch) / F_GDIM
    cs = np.concatenate([np.kron(np.eye(F_GROUPS), np.cos(angch)),
                         np.kron(np.eye(F_GROUPS), np.sin(angch))], axis=1) / math.sqrt(F_GDIM)
    f = lambda z: jnp.asarray(np.ascontiguousarray(z), dtype=F32)
    return dict(kc=f(kc), ks=f(ks), tc=f(tc), ts=f(ts), c2=f(c2), s2=f(s2), perm=f(perm).astype(BF16),
                cctx=f(cctx), sctx=f(sctx), cs=f(cs))


def _fourier(y4, tabs, wblk, *, n, ctx, with_ctx):
    B = y4.shape[0]
    n2 = n // FFT_N1
    groups = 4
    full = lambda a, nd: pl.BlockSpec(a.shape, lambda *i: (0,) * a.ndim)
    b4 = pl.pallas_call(
        functools.partial(_fft1_kernel, groups=groups),
        out_shape=jax.ShapeDtypeStruct((B, FFT_N1, n2, 2 * F_WIDTH), F32),
        grid=(B, n2 // (8 * groups)),
        in_specs=[pl.BlockSpec((1, FFT_N1, 8 * groups, 2 * F_WIDTH), lambda b, j: (b, 0, j, 0)),
                  full(tabs["kc"], 2), full(tabs["ks"], 2),
                  pl.BlockSpec((128 * groups, 128), lambda b, j: (j, 0)),
                  pl.BlockSpec((128 * groups, 128), lambda b, j: (j, 0))],
        out_specs=pl.BlockSpec((1, FFT_N1, 8 * groups, 2 * F_WIDTH), lambda b, j: (b, 0, j, 0)),
        compiler_params=_cp(("arbitrary", "arbitrary")),
        name="fourier_stage1",
    )(y4, tabs["kc"], tabs["ks"], tabs["tc"], tabs["ts"])
    f4 = pl.pallas_call(
        functools.partial(_fft2_kernel, n2=n2),
        out_shape=jax.ShapeDtypeStruct((B, n2, 16, F_WIDTH), F32),
        grid=(B, FFT_N1 // 8),
        in_specs=[pl.BlockSpec((1, 8, n2, 2 * F_WIDTH), lambda b, j: (b, j, 0, 0)),
                  full(tabs["c2"], 2), full(tabs["s2"], 2), full(wblk, 2), full(tabs["perm"], 2)],
        out_specs=pl.BlockSpec((1, n2, 8, F_WIDTH), lambda b, j: (b, 0, j, 0)),
        scratch_shapes=[pltpu.VMEM((8, n2, F_WIDTH), BF16)],
        compiler_params=_cp(("arbitrary", "arbitrary"), VMEM_LIMIT),
        name="fourier_stage2",
    )(b4, tabs["c2"], tabs["s2"], wblk, tabs["perm"])
    f_ctx = None
    if with_ctx:
        f_ctx = pl.pallas_call(
            _fftc_kernel,
            out_shape=jax.ShapeDtypeStruct((B, ctx, F_WIDTH), F32),
            grid=(B,),
            in_specs=[pl.BlockSpec((1, 1, TOK, 2 * F_WIDTH), lambda b: (b, FFT_N1, 0, 0)),
                      full(tabs["cctx"], 1), full(tabs["sctx"], 1), full(wblk, 1)],
            out_specs=pl.BlockSpec((1, ctx, F_WIDTH), lambda b: (b, 0, 0)),
            compiler_params=_cp(("arbitrary",)),
            name="fourier_ctx",
        )(y4, tabs["cctx"], tabs["sctx"], wblk)
    return f4.reshape(B, n, F_WIDTH), f_ctx


VROWS = DA_VDIM + 16


def _attn_kernel(q_ref, k_ref, vt_ref, dl_ref, g_ref, o_ref, m_scr, acc_scr, *, lam_init):
    kt = pl.program_id(3)
    nk = pl.num_programs(3)

    @pl.when(kt == 0)
    def _():
        m_scr[...] = jnp.full(m_scr.shape, NEG, F32)
        acc_scr[...] = jnp.zeros(acc_scr.shape, F32)

    q = q_ref[0]
    k = k_ref[0]
    vt = vt_ref[0]
    ones = jnp.ones((16, vt.shape[1]), BF16)
    lhs = [jnp.concatenate([vt[DA_VDIM * h:DA_VDIM * (h + 1)], ones], axis=0) for h in range(2)]
    lane = lax.broadcasted_iota(jnp.int32, (1, LANES), 1)
    zero = jnp.zeros((), BF16)

    def scores(j):
        return _dot_nt(k, jnp.where((lane // DA_DIM) == j, q, zero))

    st_next = scores(0)
    for j in range(4):
        st = st_next
        if j < 3:
            st_next = scores(j + 1)
        m_old = m_scr[j]
        m_new = jnp.maximum(m_old, jnp.max(st, axis=0, keepdims=True))
        alpha = jnp.exp2(m_old - m_new)
        pt = jnp.exp2(st - m_new).astype(BF16)
        acc_scr[j] = alpha * acc_scr[j] + _dot(lhs[j // 2], pt)
        m_scr[j] = m_new

    @pl.when(kt == nk - 1)
    def _():
        dl = dl_ref[...]
        lam = (jnp.exp(jnp.sum(dl[0:1] * dl[1:2], keepdims=True))
               - jnp.exp(jnp.sum(dl[2:3] * dl[3:4], keepdims=True)) + lam_init)
        outs = []
        for h in range(2):
            a0 = acc_scr[2 * h]
            a1 = acc_scr[2 * h + 1]
            o = (a0[:DA_VDIM] / a0[DA_VDIM:DA_VDIM + 1]
                 - lam * (a1[:DA_VDIM] / a1[DA_VDIM:DA_VDIM + 1]))
            r = lax.rsqrt(jnp.mean(o * o, axis=0, keepdims=True) + EPS)
            outs.append(((o * r) * g_ref[...]) * (1.0 - lam_init))
        o_ref[0] = jnp.concatenate(outs, axis=0).astype(BF16)


def _attention(dq, dk, dvT, dlam, gcol, *, lam_init, tq, q0, nq, tk, k0, nk):
    B = dq.shape[0]
    return pl.pallas_call(
        functools.partial(_attn_kernel, lam_init=lam_init),
        out_shape=jax.ShapeDtypeStruct((B, DA_WIDTH, nq * tq), BF16),
        grid=(B, DA_WIDTH // 128, nq, nk),
        in_specs=[pl.BlockSpec((1, tq, 128), lambda b, p, i, j: (b, q0 + i, p)),
                  pl.BlockSpec((1, tk, 128), lambda b, p, i, j: (b, k0 + j, p)),
                  pl.BlockSpec((1, 128, tk), lambda b, p, i, j: (b, p, k0 + j)),
                  pl.BlockSpec(dlam.shape, lambda b, p, i, j: (0, 0)),
                  pl.BlockSpec(gcol.shape, lambda b, p, i, j: (0, 0))],
        out_specs=pl.BlockSpec((1, 128, tq), lambda b, p, i, j: (b, p, i)),
        scratch_shapes=[pltpu.VMEM((4, 1, tq), F32), pltpu.VMEM((4, VROWS, tq), F32)],
        compiler_params=_cp(("arbitrary",) * 4, VMEM_LIMIT),
        name="diff_attention",
    )(dq, dk, dvT, dlam, gcol)


def _mlstm_kernel(qf_ref, kf_ref, vf_ref, gcf_ref, grf_ref, qb_ref, kb_ref, vb_ref, gcb_ref, grb_ref,
                  hf_ref, hb_ref, c_scr, m_scr):
    t = pl.program_id(1)

    @pl.when(t == 0)
    def _():
        c_scr[...] = jnp.zeros(c_scr.shape, F32)
        m_scr[...] = jnp.zeros(m_scr.shape, F32)

    L = TOK
    si = lax.broadcasted_iota(jnp.int32, (L, L), 0)
    li = lax.broadcasted_iota(jnp.int32, (L, L), 1)
    dirs = ((qf_ref, kf_ref, vf_ref, gcf_ref, grf_ref, hf_ref, si <= li, li <= si, L - 1),
            (qb_ref, kb_ref, vb_ref, gcb_ref, grb_ref, hb_ref, si >= li, li >= si, 0))
    ones = jnp.ones((16, L), F32)
    for d, (q_ref, k_ref, vt_ref, gc_ref, gr_ref, h_ref, seen, seen_t, last) in enumerate(dirs):
        gc = gc_ref[0]
        gr = gr_ref[0]
        seen_b = jnp.where(seen, 1.0, 0.0).astype(BF16)
        seen_tb = jnp.where(seen_t, 1.0, 0.0).astype(BF16)
        bcols = sum(_dot(seen_tb, piece) for piece in _split3(gc))
        brows = sum(_dot(piece, seen_b) for piece in _split3(gr))
        for hd in range(M_HEADS):
            idx = d * M_HEADS + hd
            ji = d * 8 + hd
            jf = d * 8 + 4 + hd
            sl = slice(M_PAD * hd, M_PAD * (hd + 1))
            q = q_ref[0, :, sl]
            k = k_ref[0, :, sl]
            vt = vt_ref[0, sl, :]
            b_row = brows[jf:jf + 1, :]
            cs = gc[:, ji:ji + 1] - bcols[:, jf:jf + 1]
            li_row = gr[ji:ji + 1, :]
            m_old = m_scr[idx][0:1, 0:1]
            c_old = c_scr[idx]

            dlog = jnp.where(seen, b_row + cs, NEG)
            inter = b_row + m_old
            m_t = jnp.maximum(inter, jnp.max(dlog, axis=0, keepdims=True))
            w_inter = jnp.exp(inter - m_t)
            st = _dot_nt(k, q) * jnp.exp(dlog - m_t)
            cq = _dot_nt(c_old.astype(BF16), q)
            num = w_inter * cq[:M_PAD] + _dot(vt, st.astype(BF16))
            den = w_inter * cq[M_PAD:M_PAD + 1] + jnp.sum(st, axis=0, keepdims=True)
            h_ref[0, sl, :] = num / jnp.maximum(jnp.abs(den), jnp.exp(-m_t))

            total = b_row[:, last:last + 1]
            wlog = total - b_row + li_row
            m_new = jnp.maximum(total + m_old, jnp.max(wlog, axis=1, keepdims=True))
            decay = jnp.exp(total + m_old - m_new)
            w = jnp.exp(wlog - m_new)
            vw = jnp.concatenate([vt.astype(F32) * w, ones * w], axis=0).astype(BF16)
            c_scr[idx] = decay * c_old + _dot(vw, k)
            m_scr[idx] = jnp.broadcast_to(m_new, (8, 128))


def _mlstm(mq, mk, mvT, gl, glT, *, n_lat):
    B, NT, _ = mq.shape
    nt = n_lat + 1
    fwd = lambda t: jnp.where(t == 0, n_lat, t - 1)
    bwd = lambda t: jnp.where(t == 0, n_lat, n_lat - t)
    tok = lambda w, f: pl.BlockSpec((1, TOK, w), lambda b, t: (b, f(t), 0))
    lanes = lambda r, f: pl.BlockSpec((1, r, TOK), lambda b, t: (b, 0, f(t)))
    ins, specs = [], []
    for f in (fwd, bwd):
        ins += [mq, mk, mvT, gl, glT]
        specs += [tok(MP_WIDTH, f)] * 2 + [lanes(MP_WIDTH, f), tok(128, f), lanes(N_GATES, f)]
    return pl.pallas_call(
        _mlstm_kernel,
        out_shape=[jax.ShapeDtypeStruct((B, MP_WIDTH, NT), F32)] * 2,
        grid=(B, nt),
        in_specs=specs,
        out_specs=[lanes(MP_WIDTH, fwd), lanes(MP_WIDTH, bwd)],
        scratch_shapes=[pltpu.VMEM((2 * M_HEADS, M_PAD + 16, M_PAD), F32),
                        pltpu.VMEM((2 * M_HEADS, 8, 128), F32)],
        compiler_params=_cp(("arbitrary", "arbitrary"), VMEM_LIMIT),
        name="mlstm",
    )(*ins)


def _outproj_kernel(x_ref, f_ref, dat_ref, hf_ref, hb_ref, mo_ref, ada_ref, mg_ref, wo_ref, wod_ref, g2_ref, wr_ref,
                    xo_ref, hl_ref, pt_ref, *, is_ctx):
    b = pl.program_id(0)
    mod = ada_ref[CTX_ROW if is_ctx else b]
    gt1, sh2, sc2 = mod[2:3], mod[3:4], mod[4:5]
    mg = mg_ref[...]
    for s in range(x_ref.shape[1] // TOK):
        tk = slice(TOK * s, TOK * (s + 1))
        hs = hf_ref[0, :, tk] + hb_ref[0, :, tk]
        og = mo_ref[0, :, tk].astype(F32)
        parts = [dat_ref[0, :, tk]]
        for hd in range(M_HEADS):
            sl = slice(M_PAD * hd, M_PAD * (hd + 1))
            hh = hs[sl]
            r = lax.rsqrt(jnp.sum(hh * hh, axis=0, keepdims=True) * (1.0 / M_DIM) + EPS)
            parts.append((((hh * r) * mg[sl]) * _sigmoid(og[sl])).astype(BF16))
        mix_t = jnp.concatenate(parts, axis=0)
        upd = _dot(f_ref[0, tk].astype(BF16), wo_ref[...]) + lax.dot_general(
            mix_t, wod_ref[...], (((0,), (0,)), ((), ())), preferred_element_type=F32)
        xn = x_ref[0, tk] + gt1 * upd
        xo_ref[0, tk] = xn
        r = lax.rsqrt(jnp.mean(xn * xn, axis=-1, keepdims=True) + EPS)
        h2 = (xn * r) * g2_ref[...] * (1.0 + sc2) + sh2
        hl_ref[0, tk] = h2.astype(BF16)
        lt = _dot3(h2, wr_ref[...]).T[:N_EXPERTS]
        ex = jnp.exp(lt - jnp.max(lt, axis=0, keepdims=True))
        pt_ref[0, :, tk] = ex / jnp.sum(ex, axis=0, keepdims=True)


def _outproj_kernel_aliased(x_ref, f_ref, dat_ref, hf_ref, hb_ref, mo_ref, ada_ref, mg_ref, wo_ref, wod_ref,
                            g2_ref, wr_ref, hlp_ref, xo_ref, hl_ref, pt_ref, *, is_ctx):
    del hlp_ref
    _outproj_kernel(x_ref, f_ref, dat_ref, hf_ref, hb_ref, mo_ref, ada_ref, mg_ref, wo_ref, wod_ref, g2_ref,
                    wr_ref, xo_ref, hl_ref, pt_ref, is_ctx=is_ctx)


def _outproj(xu, f, daT, hf, hb, mo, ada_l, mg, wo, wod, g2, wrp, hl_prev, *, t0, ntl, is_ctx):
    B, NT, _ = xu.shape
    n = ntl * TOK
    tile = 2 * TOK if n % (2 * TOK) == 0 else TOK
    o = t0 * TOK // tile
    tok = lambda w: pl.BlockSpec((1, tile, w), lambda b, t: (b, o + t, 0))
    trs = lambda r: pl.BlockSpec((1, r, tile), lambda b, t: (b, 0, o + t))
    loc = lambda w: pl.BlockSpec((1, tile, w), lambda b, t: (b, t, 0))
    full = lambda a: pl.BlockSpec(a.shape, lambda b, t: (0,) * a.ndim)
    return pl.pallas_call(
        functools.partial(_outproj_kernel_aliased, is_ctx=is_ctx),
        out_shape=[jax.ShapeDtypeStruct(xu.shape, F32), jax.ShapeDtypeStruct((B, NT, D), BF16),
                   jax.ShapeDtypeStruct((B, N_EXPERTS, n), F32)],
        grid=(B, n // tile),
        in_specs=[tok(D), loc(F_WIDTH), pl.BlockSpec((1, DA_WIDTH, tile), lambda b, t: (b, 0, t)),
                  trs(MP_WIDTH), trs(MP_WIDTH), trs(MP_WIDTH),
                  full(ada_l), full(mg), full(wo), full(wod), full(g2), full(wrp),
                  pl.BlockSpec(memory_space=pl.ANY)],
        out_specs=[tok(D), tok(D), pl.BlockSpec((1, N_EXPERTS, tile), lambda b, t: (b, 0, t))],
        input_output_aliases={0: 0, 12: 1},
        compiler_params=_cp(("arbitrary", "arbitrary"), VMEM_LIMIT),
        name="outproj_norm2_router",
    )(xu, f, daT, hf, hb, mo, ada_l, mg, wo, wod, g2, wrp, hl_prev)


def _select_kernel(p_ref, rank_ref, offs_ref, *, n, cap):
    p = p_ref[0]
    xi = pltpu.bitcast(p, jnp.int32)

    def body(i, lo):
        cand = lo | jnp.left_shift(jnp.int32(1), 30 - i)
        cnt = jnp.sum(jnp.where(xi >= cand, 1.0, 0.0), axis=1, keepdims=True)
        return jnp.where(cnt >= cap, cand, lo)

    thr = lax.fori_loop(0, 31, body, jnp.zeros((N_EXPERTS, 1), jnp.int32))
    nb = n // TOK
    rows = lax.broadcasted_iota(jnp.int32, (n, 128), 0)
    cols = lax.broadcasted_iota(jnp.int32, (n, 128), 1)
    blk_ind = jnp.where((rows // TOK) == cols, 1.0, 0.0).astype(BF16)
    u128 = jnp.where(lax.broadcasted_iota(jnp.int32, (128, 128), 0)
                     < lax.broadcasted_iota(jnp.int32, (128, 128), 1), 1.0, 0.0).astype(BF16)
    utok = jnp.where(lax.broadcasted_iota(jnp.int32, (TOK, TOK), 0)
                     < lax.broadcasted_iota(jnp.int32, (TOK, TOK), 1), 1.0, 0.0).astype(BF16)

    def prefix(mf):
        mb = mf.astype(BF16)
        counts = _dot(mb, blk_ind)
        offs = _dot(counts.astype(BF16), u128)
        pieces = [_dot(mb[:, TOK * j:TOK * (j + 1)], utok) + offs[:, j:j + 1] for j in range(nb)]
        return (jnp.concatenate(pieces, axis=1) if nb > 1 else pieces[0]), offs

    gt = xi > thr
    eq = xi == thr
    need = cap - jnp.sum(jnp.where(gt, 1.0, 0.0), axis=1, keepdims=True)
    rank_eq, _ = prefix(jnp.where(eq, 1.0, 0.0))
    sel = gt | (eq & (rank_eq < need))
    rank, offs = prefix(jnp.where(sel, 1.0, 0.0))
    rank_ref[0] = jnp.where(sel, rank, -1.0)
    offs_ref[0] = offs.astype(jnp.int32)


def _select(pt, *, cap):
    B, _, n = pt.shape
    return pl.pallas_call(
        functools.partial(_select_kernel, n=n, cap=cap),
        out_shape=[jax.ShapeDtypeStruct((B, N_EXPERTS, n), F32),
                   jax.ShapeDtypeStruct((B, N_EXPERTS, 128), jnp.int32)],
        grid=(B,),
        in_specs=[pl.BlockSpec((1, N_EXPERTS, n), lambda b: (b, 0, 0))],
        out_specs=[pl.BlockSpec((1, N_EXPERTS, n), lambda b: (b, 0, 0)),
                   pl.BlockSpec((1, N_EXPERTS, 128), lambda b: (b, 0, 0))],
        compiler_params=_cp(("arbitrary",), VMEM_LIMIT),
        name="expert_choice_select",
    )(pt)


def _gather_kernel(offs_ref, h_ref, rank_ref, prob_ref, o_ref, gate_ref, *, eg, per):
    b, g, tb = pl.program_id(0), pl.program_id(1), pl.program_id(2)

    @pl.when(tb == 0)
    def _():
        o_ref[...] = jnp.zeros(o_ref.shape, BF16)
        gate_ref[...] = jnp.zeros(gate_ref.shape, F32)

    cap_pad = o_ref.shape[2]
    half = SLOT // 2
    slot = lax.broadcasted_iota(jnp.int32, (SLOT, TOK), 0).astype(F32)

    def add_rows(i, r, p, h, base, start=None):
        hit = r == slot + base.astype(F32)
        if start is not None:
            hit = hit & (r >= start.astype(F32))
        rows = _dot(jnp.where(hit, 1.0, 0.0).astype(BF16), h).astype(BF16)
        o_ref[0, i, pl.ds(base, SLOT), :] = o_ref[0, i, pl.ds(base, SLOT), :] + rows
        gate_ref[0, i, pl.ds(base, SLOT), :] = (gate_ref[0, i, pl.ds(base, SLOT), :]
                                                + jnp.sum(jnp.where(hit, p, 0.0), axis=1, keepdims=True))

    def operands(s, i):
        tk = slice(TOK * s, TOK * (s + 1))
        e = g * eg + i
        return rank_ref[0, pl.ds(e, 1), tk], prob_ref[0, pl.ds(e, 1), tk], h_ref[0, tk, :]

    ends, his = {}, {}
    for s in range(per):
        for i in range(eg):
            e = g * eg + i
            lo = offs_ref[b, e, tb * per + s]
            his[s, i] = offs_ref[b, e, tb * per + s + 1]
            base = pl.multiple_of(jnp.minimum((lo // half) * half, cap_pad - SLOT), half)
            add_rows(i, *operands(s, i), base)
            ends[s, i] = base + SLOT

    for s in range(per):
        for i in range(eg):
            @pl.when(his[s, i] > ends[s, i])
            def _(s=s, i=i):
                def body(t, carry):
                    start = ends[s, i] + t * SLOT
                    base = pl.multiple_of(jnp.minimum(start, cap_pad - SLOT), half)
                    add_rows(i, *operands(s, i), base, start)
                    return carry

                lax.fori_loop(0, (his[s, i] - ends[s, i] + SLOT - 1) // SLOT, body, 0)


def _gather(offs, hl, rank, pt, *, tb_tok, tb0, n, cap_pad, eg):
    B = hl.shape[0]
    per = tb_tok // TOK
    return pl.pallas_call(
        functools.partial(_gather_kernel, eg=eg, per=per),
        out_shape=[jax.ShapeDtypeStruct((B, N_EXPERTS, cap_pad, D), BF16),
                   jax.ShapeDtypeStruct((B, N_EXPERTS, cap_pad, 1), F32)],
        grid_spec=pltpu.PrefetchScalarGridSpec(
            num_scalar_prefetch=1,
            grid=(B, N_EXPERTS // eg, n // tb_tok),
            in_specs=[pl.BlockSpec((1, tb_tok, D), lambda b, g, t, o: (b, tb0 + t, 0)),
                      pl.BlockSpec((1, N_EXPERTS, tb_tok), lambda b, g, t, o: (b, 0, t)),
                      pl.BlockSpec((1, N_EXPERTS, tb_tok), lambda b, g, t, o: (b, 0, t))],
            out_specs=[pl.BlockSpec((1, eg, cap_pad, D), lambda b, g, t, o: (b, g, 0, 0)),
                       pl.BlockSpec((1, eg, cap_pad, 1), lambda b, g, t, o: (b, g, 0, 0))]),
        compiler_params=_cp(("arbitrary",) * 3, VMEM_LIMIT),
        name="expert_gather",
    )(offs, hl, rank, pt)


FFN_ROWS = 1024


def _ffn_kernel(x_ref, gate_ref, w1_ref, w3_ref, w2_ref, y_ref, acc_ref):
    f = pl.program_id(2)

    @pl.when(f == 0)
    def _():
        acc_ref[...] = jnp.zeros(acc_ref.shape, F32)

    w1 = w1_ref[0, 0].astype(BF16)
    w3 = w3_ref[0, 0].astype(BF16)
    w2 = w2_ref[0, 0].astype(BF16)
    mb, _, cap_pad, _ = x_ref.shape
    rows = min(FFN_ROWS, cap_pad)
    for i in range(mb):
        for r in range(0, cap_pad, rows):
            x = x_ref[i, 0, r:r + rows, :]
            hid = (_silu(_dot(x, w1)) * _dot(x, w3)).astype(BF16)
            acc_ref[i * cap_pad + r:i * cap_pad + r + rows, :] += _dot(hid, w2)

    @pl.when(f == pl.num_programs(2) - 1)
    def _():
        gate = gate_ref[...].reshape(-1, 1)
        y_ref[...] = (acc_ref[...] * gate).astype(BF16).reshape(y_ref.shape)


def _ffn(xs, gates, w1, w3, w2, *, layer, mb, tf):
    B, E, cap_pad, _ = xs.shape
    return pl.pallas_call(
        _ffn_kernel,
        out_shape=jax.ShapeDtypeStruct(xs.shape, BF16),
        grid=(E, B // mb, D_FF // tf),
        in_specs=[pl.BlockSpec((mb, 1, cap_pad, D), lambda e, m, f: (m, e, 0, 0)),
                  pl.BlockSpec((mb, 1, cap_pad, 1), lambda e, m, f: (m, e, 0, 0)),
                  pl.BlockSpec((1, 1, D, tf), lambda e, m, f: (layer, e, 0, f)),
                  pl.BlockSpec((1, 1, D, tf), lambda e, m, f: (layer, e, 0, f)),
                  pl.BlockSpec((1, 1, tf, D), lambda e, m, f: (layer, e, f, 0))],
        out_specs=pl.BlockSpec((mb, 1, cap_pad, D), lambda e, m, f: (m, e, 0, 0)),
        scratch_shapes=[pltpu.VMEM((mb * cap_pad, D), F32)],
        compiler_params=_cp(("arbitrary",) * 3, VMEM_LIMIT),
        name="expert_ffn",
    )(xs, gates, w1, w3, w2)


CCOL = 512


def _combine_kernel(offs_ref, x_ref, y_ref, rankc_ref, ada_ref, o_ref, tot_scr, *, per, is_ctx):
    b, tb = pl.program_id(0), pl.program_id(2)
    gt2 = ada_ref[CTX_ROW if is_ctx else b][5:6]
    rc_all = rankc_ref[0]
    cap_pad = y_ref.shape[2]
    half = SLOT // 2
    slot = lax.broadcasted_iota(jnp.int32, (1, SLOT), 1).astype(F32)

    ends, his = {}, {}
    for s in range(per):
        tk = slice(TOK * s, TOK * (s + 1))
        total = jnp.zeros((TOK, tot_scr.shape[1]), F32)
        for e0 in range(0, N_EXPERTS, 2):
            hots, rows = [], []
            for e in (e0, e0 + 1):
                lo = offs_ref[b, e, tb * per + s]
                his[s, e] = offs_ref[b, e, tb * per + s + 1]
                base = pl.multiple_of(jnp.minimum((lo // half) * half, cap_pad - SLOT), half)
                ends[s, e] = base + SLOT
                hots.append(jnp.where(rc_all[tk, e:e + 1] == slot + base.astype(F32), 1.0, 0.0).astype(BF16))
                rows.append(y_ref[0, e, pl.ds(base, SLOT), :])
            total = total + _dot(jnp.concatenate(hots, axis=1), jnp.concatenate(rows, axis=0))
        tot_scr[tk] = total

    for s in range(per):
        tk = slice(TOK * s, TOK * (s + 1))
        for e in range(N_EXPERTS):
            @pl.when(his[s, e] > ends[s, e])
            def _(s=s, e=e, tk=tk):
                rc = rc_all[tk, e:e + 1]

                def body(t, carry):
                    start = ends[s, e] + t * SLOT
                    base = pl.multiple_of(jnp.minimum(start, cap_pad - SLOT), half)
                    hit = (rc == slot + base.astype(F32)) & (rc >= start.astype(F32))
                    tot_scr[tk] += _dot(jnp.where(hit, 1.0, 0.0).astype(BF16), y_ref[0, e, pl.ds(base, SLOT), :])
                    return carry

                lax.fori_loop(0, (his[s, e] - ends[s, e] + SLOT - 1) // SLOT, body, 0)

    o_ref[0] = x_ref[0] + gt2 * tot_scr[...]


def _combine(offs, xu, ys, rank_c, ada_l, *, tb_tok, tb0, n, is_ctx):
    B = xu.shape[0]
    cap_pad = ys.shape[2]
    per = tb_tok // TOK
    return pl.pallas_call(
        functools.partial(_combine_kernel, per=per, is_ctx=is_ctx),
        out_shape=jax.ShapeDtypeStruct(xu.shape, F32),
        grid_spec=pltpu.PrefetchScalarGridSpec(
            num_scalar_prefetch=1,
            grid=(B, D // CCOL, n // tb_tok),
            in_specs=[pl.BlockSpec((1, tb_tok, CCOL), lambda b, c, t, o: (b, tb0 + t, c)),
                      pl.BlockSpec((1, N_EXPERTS, cap_pad, CCOL), lambda b, c, t, o: (b, 0, 0, c),
                                   pipeline_mode=pl.Buffered(1)),
                      pl.BlockSpec((1, tb_tok, N_EXPERTS), lambda b, c, t, o: (b, t, 0)),
                      pl.BlockSpec((ADA_ROWS, ADA_CHUNKS, CCOL), lambda b, c, t, o: (0, 0, c))],
            out_specs=pl.BlockSpec((1, tb_tok, CCOL), lambda b, c, t, o: (b, tb0 + t, c)),
            scratch_shapes=[pltpu.VMEM((tb_tok, CCOL), F32)]),
        input_output_aliases={1: 0},
        compiler_params=_cp(("arbitrary",) * 3, VMEM_LIMIT),
        name="expert_combine",
    )(offs, xu, ys, rank_c, ada_l)


def _moe(xu, hl, pt, ada_l, w1, w3, w2, *, layer, row0, is_ctx):
    B, _, n = pt.shape
    cap = EC_CAPACITY * n // N_EXPERTS
    cap_pad = -(-cap // SLOT) * SLOT
    nb = n // TOK
    rank, offs = _select(pt, cap=cap)
    offs = offs[:, :, :nb + 1]
    gt = min(n, MOE_TOK)
    ct = min(n, COMBINE_TOK)
    xs, gates = _gather(offs, hl, rank, pt, tb_tok=gt, tb0=row0 // gt, n=n, cap_pad=cap_pad, eg=GATHER_EXPERTS)
    mb = 2 if (B % 2 == 0 and cap_pad >= 1024) else (B if cap_pad < 1024 else 1)
    ys = _ffn(xs, gates, w1, w3, w2, layer=layer, mb=mb, tf=FFN_TF)
    rank_c = jnp.swapaxes(rank, 1, 2)
    return _combine(offs, xu, ys, rank_c, ada_l, tb_tok=ct, tb0=row0 // ct, n=n, is_ctx=is_ctx)


def _final_kernel(x_ref, g_ref, o_ref):
    x = x_ref[0]
    r = lax.rsqrt(jnp.mean(x * x, axis=-1, keepdims=True) + EPS)
    o_ref[0] = (x * r) * g_ref[...]


def _final_norm(xu, g, *, n):
    B = xu.shape[0]
    tm = MOE_TOK
    return pl.pallas_call(
        _final_kernel,
        out_shape=jax.ShapeDtypeStruct((B, n, D), F32),
        grid=(B, n // tm),
        in_specs=[pl.BlockSpec((1, tm, D), lambda b, t: (b, t, 0)),
                  pl.BlockSpec((1, D), lambda b, t: (0, 0))],
        out_specs=pl.BlockSpec((1, tm, D), lambda b, t: (b, t, 0)),
        compiler_params=_cp(("arbitrary", "arbitrary")),
        name="final_norm",
    )(xu, g)


def _rope_tables(n, ctx):
    rows = n // GRID_W
    t_row = jnp.repeat(jnp.arange(rows), GRID_W)
    t_col = jnp.tile(jnp.arange(GRID_W), rows)
    nf = DA_DIM // 4
    inv = ROPE_THETA ** (-jnp.arange(nf, dtype=F32) / nf)
    ar = t_row[:, None].astype(F32) * inv
    ac = t_col[:, None].astype(F32) * inv
    ang = jnp.concatenate([ar, ar, ac, ac], axis=-1)
    sign = jnp.where((jnp.arange(DA_DIM) % 16) < 8, -1.0, 1.0).astype(F32)
    cos = jnp.concatenate([jnp.cos(ang), jnp.ones((ctx, DA_DIM), F32)], axis=0)
    sin = jnp.concatenate([jnp.sin(ang) * sign, jnp.zeros((ctx, DA_DIM), F32)], axis=0)
    return jnp.tile(cos, (1, 128 // DA_DIM)), jnp.tile(sin, (1, 128 // DA_DIM))


def _pad_heads_cols(w):
    lead = w.shape[:-1]
    w = w.reshape(lead + (M_HEADS, M_DIM))
    w = jnp.pad(w, [(0, 0)] * len(lead) + [(0, 0), (0, M_PAD - M_DIM)])
    return w.reshape(lead + (MP_WIDTH,))


def _kv_tile(nt):
    for parts in range(1, nt // LANES + 1):
        if nt % parts == 0 and (nt // parts) % LANES == 0 and nt // parts <= KV_TILE_MAX:
            return nt // parts
    raise ValueError(nt)


def kernel(x, c, ctx, c_ctx, ada_w, ada_b, norm1_g, norm2_g, w_in, four_w, m_conv_w, m_conv_b, m_gate_b,
           m_norm_g, d_lam, d_norm_g, w_out, router_w, exp_w1, exp_w3, exp_w2, final_g):
    B, N, _ = x.shape
    CTX = ctx.shape[1]
    depth = w_in.shape[0]
    assert CTX == TOK and N % (FFT_N1 * TOK) == 0 and N % Q_TILE == 0 and B <= CTX_ROW
    NT = N + CTX
    PAD = -NT % MOE_TOK
    n_lat = N // TOK
    n2 = N // FFT_N1

    xu = jnp.concatenate([x, ctx, jnp.zeros((B, PAD, D), F32)], axis=1)
    cvecs = jnp.zeros((ADA_ROWS, D), F32).at[:B].set(c).at[CTX_ROW].set(c_ctx)
    ada = _adaln(cvecs, ada_w, ada_b).reshape(depth, ADA_ROWS, ADA_CHUNKS, D)
    cos_t, sin_t = _rope_tables(N, CTX + PAD)
    tabs = _fourier_tables(N, CTX)
    tk = _kv_tile(NT)
    tq = Q_TILE

    hl = jnp.zeros((B, NT + PAD, D), BF16)
    for layer in range(depth):
        ctx_out = layer < depth - 1
        lam_init = 0.8 - 0.6 * math.exp(-0.3 * layer)
        w = w_in[layer]
        wm = jnp.concatenate([w[:, OFF_F:OFF_DQ], w[:, OFF_DQ:OFF_MO], w[:, OFF_DK:OFF_DV]], axis=1).astype(BF16)
        wvt = jnp.concatenate([w[:, OFF_DV:OFF_MV], _pad_heads_cols(w[:, OFF_MO:OFF_MQ]),
                               _pad_heads_cols(w[:, OFF_MV:OFF_G])], axis=1).T.astype(BF16)
        wg = jnp.pad(w[:, OFF_G:], ((0, 0), (0, LANES - N_GATES)))
        wc = jnp.concatenate([_pad_heads_cols(w[:, OFF_MQ:OFF_MK]), _pad_heads_cols(w[:, OFF_MK:OFF_DK])],
                             axis=1).astype(BF16)
        gb = jnp.pad(m_gate_b[layer], (0, LANES - N_GATES)).reshape(1, LANES)
        cw = jnp.concatenate([_pad_heads_cols(m_conv_w[layer][:, :M_WIDTH]),
                              _pad_heads_cols(m_conv_w[layer][:, M_WIDTH:])], axis=1)
        cb = jnp.concatenate([_pad_heads_cols(m_conv_b[layer][:M_WIDTH]),
                              _pad_heads_cols(m_conv_b[layer][M_WIDTH:])]).reshape(1, 2 * MP_WIDTH)
        ada_l = ada[layer]

        y4, dq, dk, dvT, mo, mq, mk, mv, gl, glT = _inproj(
            xu, ada_l, norm1_g[layer].reshape(1, D), wm, wc, wvt, wg, gb, tabs["cs"], cos_t, sin_t, cw, cb,
            n_lat=n_lat, n2=n2)

        wblk = jnp.zeros((F_WIDTH, F_WIDTH), F32)
        for g in range(F_GROUPS):
            wblk = wblk.at[F_GDIM * g:F_GDIM * (g + 1), F_GDIM * g:F_GDIM * (g + 1)].set(four_w[layer, g])
        f_l, f_c = _fourier(y4, tabs, wblk.astype(BF16), n=N, ctx=CTX, with_ctx=ctx_out)

        dlam = d_lam[layer]
        g2 = d_norm_g[layer].reshape(DA_VDIM, 1)
        da_l = _attention(dq, dk, dvT, dlam, g2, lam_init=lam_init, tq=tq, q0=0, nq=N // tq,
                          tk=tk, k0=0, nk=NT // tk)

        hf, hb = _mlstm(mq, mk, mv, gl, glT, n_lat=n_lat)

        mg = _pad_heads_cols(m_norm_g[layer]).reshape(MP_WIDTH, 1)
        wol = w_out[layer]
        wo = wol[:F_WIDTH].astype(BF16)
        wod = jnp.concatenate([wol[F_WIDTH:F_WIDTH + DA_WIDTH],
                               jnp.pad(wol[F_WIDTH + DA_WIDTH:].reshape(M_HEADS, M_DIM, D),
                                       ((0, 0), (0, M_PAD - M_DIM), (0, 0))).reshape(MP_WIDTH, D)],
                              axis=0).astype(BF16)
        g2n = norm2_g[layer].reshape(1, D)
        wrp = jnp.pad(router_w[layer], ((0, 0), (0, LANES - N_EXPERTS)))
        xu, hl, pt_l = _outproj(xu, f_l, da_l, hf, hb, mo, ada_l, mg, wo, wod, g2n, wrp, hl,
                                t0=0, ntl=n_lat, is_ctx=False)
        if ctx_out:
            da_c = _attention(dq, dk, dvT, dlam, g2, lam_init=lam_init, tq=TOK, q0=n_lat, nq=1,
                              tk=TOK, k0=n_lat, nk=1)
            xu, hl, pt_c = _outproj(xu, f_c, da_c, hf, hb, mo, ada_l, mg, wo, wod, g2n, wrp, hl,
                                    t0=n_lat, ntl=1, is_ctx=True)

        xu = _moe(xu, hl, pt_l, ada_l, exp_w1, exp_w3, exp_w2, layer=layer, row0=0, is_ctx=False)
        if ctx_out:
            xu = _moe(xu, hl, pt_c, ada_l, exp_w1, exp_w3, exp_w2, layer=layer, row0=N, is_ctx=True)

    return _final_norm(xu, final_g.reshape(1, D), n=N)
```

```python
import functools
import math

import numpy as np
import jax
import jax.numpy as jnp
from jax import lax
from jax.experimental import pallas as pl
from jax.experimental.pallas import tpu as pltpu

F32 = jnp.float32
BF16 = jnp.bfloat16
HI = lax.Precision.HIGHEST

D = 1024
EPS = 1e-6
GRID_W = 64
ROPE_THETA = 10000.0
F_GROUPS, F_GDIM = 4, 64
F_WIDTH = F_GROUPS * F_GDIM
DA_HEADS, DA_DIM = 6, 32
DA_VDIM = 2 * DA_DIM
DA_WIDTH = DA_HEADS * DA_VDIM
M_HEADS, M_DIM = 4, 96
M_WIDTH = M_HEADS * M_DIM
M_PAD = 128
MP_WIDTH = M_HEADS * M_PAD
N_GATES = 4 * M_HEADS
N_EXPERTS = 16
EC_CAPACITY = 2
D_FF = 2 * D
ADA_CHUNKS = 6
ADA_ROWS = 8
CTX_ROW = 4

LANES = 128
MXU_DIM = 256
V7X_VMEM_BYTES = 64 * 1024 * 1024

TOK = MXU_DIM
FFT_N1 = 16
SLOT = LANES
Q_TILE = 512
KV_TILE_MAX = 8448
MOE_TOK = 512
COMBINE_TOK = 512
FFN_TF = 512
GATHER_EXPERTS = 8
NEG = -1e30

OFF_F = 0
OFF_DQ = OFF_F + F_WIDTH
OFF_MO = OFF_DQ + 2 * DA_HEADS * DA_DIM
OFF_MQ = OFF_MO + M_WIDTH
OFF_MK = OFF_MQ + M_WIDTH
OFF_DK = OFF_MK + M_WIDTH
OFF_DV = OFF_DK + 2 * DA_HEADS * DA_DIM
OFF_MV = OFF_DV + DA_HEADS * DA_VDIM
OFF_G = OFF_MV + M_WIDTH

VMEM_LIMIT = V7X_VMEM_BYTES * 7 // 8


def _cp(sem, vmem=None):
    return pltpu.CompilerParams(dimension_semantics=sem, vmem_limit_bytes=vmem)


def _sigmoid(x):
    return 1.0 / (1.0 + jnp.exp(-x))


def _silu(x):
    return x * _sigmoid(x)


def _dot(a, b, precision=None):
    return jnp.dot(a, b, preferred_element_type=F32, precision=precision)


def _split(a):
    hi = a.astype(BF16)
    return hi, (a - hi.astype(F32)).astype(BF16)


def _dot3(a, b):
    a_hi, a_lo = a if isinstance(a, tuple) else _split(a)
    b_hi, b_lo = b if isinstance(b, tuple) else _split(b)
    return _dot(a_hi, b_hi) + _dot(a_hi, b_lo) + _dot(a_lo, b_hi)


def _split3(a):
    hi = a.astype(BF16)
    r = a - hi.astype(F32)
    mid = r.astype(BF16)
    return hi, mid, (r - mid.astype(F32)).astype(BF16)


def _dot_nt(a, b, precision=None):
    return lax.dot_general(a, b, (((1,), (1,)), ((), ())), preferred_element_type=F32,
                           precision=precision)


def _ada_kernel(c_ref, w_ref, b_ref, o_ref):
    c = c_ref[...]
    o_ref[0] = _dot(_silu(c), w_ref[0], HI) + b_ref[0]


def _adaln(cvecs, ada_w, ada_b):
    depth = ada_w.shape[0]
    tn = 1536
    return pl.pallas_call(
        _ada_kernel,
        out_shape=jax.ShapeDtypeStruct((depth, ADA_ROWS, ADA_CHUNKS * D), F32),
        grid=(depth, ADA_CHUNKS * D // tn),
        in_specs=[pl.BlockSpec((ADA_ROWS, D), lambda l, j: (0, 0)),
                  pl.BlockSpec((1, D, tn), lambda l, j: (l, 0, j)),
                  pl.BlockSpec((1, 1, tn), lambda l, j: (l, 0, j))],
        out_specs=pl.BlockSpec((1, 8, tn), lambda l, j: (l, 0, j)),
        compiler_params=_cp(("arbitrary", "arbitrary")),
        name="adaln",
    )(cvecs, ada_w, ada_b.reshape(depth, 1, ADA_CHUNKS * D))


def _inproj_kernel(x_ref, xp_ref, xn_ref, ada_ref, g_ref, wm_ref, wc_ref, wvt_ref, wg_ref, gb_ref, cs_ref,
                   cos_ref, sin_ref, cw_ref, cb_ref,
                   y_ref, dq_ref, dk_ref, dvt_ref, mo_ref, mq_ref, mk_ref, mv_ref, gl_ref, glt_ref, *, n_lat):
    b = pl.program_id(0)
    t = pl.program_id(1)
    n_tiles = pl.num_programs(1)
    is_ctx = t >= n_lat
    row = jnp.where(is_ctx, CTX_ROW, b)
    mod = ada_ref[row]
    sh, sc = mod[0:1], mod[1:2]

    xa = jnp.concatenate([xp_ref[0], x_ref[0], xn_ref[0]], axis=0)
    r = lax.rsqrt(jnp.mean(xa * xa, axis=-1, keepdims=True) + EPS)
    ha = (xa * r) * g_ref[...] * (1.0 + sc) + sh
    h = ha[8:8 + TOK]
    hb = h.astype(BF16)

    pm = _dot(hb, wm_ref[...])
    o = 0
    pf = pm[:, o:o + F_WIDTH]; o += F_WIDTH
    q = pm[:, o:o + DA_WIDTH]; o += DA_WIDTH
    k = pm[:, o:o + DA_WIDTH]; o += DA_WIDTH
    pt = _dot_nt(wvt_ref[...], hb)
    dvt_ref[0] = pt[:DA_WIDTH].astype(BF16)
    mo_ref[0] = pt[DA_WIDTH:DA_WIDTH + MP_WIDTH].astype(BF16)
    mv_ref[0] = pt[DA_WIDTH + MP_WIDTH:].astype(BF16)

    y_ref[0, 0] = _dot3(pf, cs_ref[...])

    cos = cos_ref[...]
    sin = sin_ref[...]
    lane = lax.broadcasted_iota(jnp.int32, (1, 128), 1)
    low = (lane % 16) < 8

    def rope(z):
        parts = []
        for c in range(DA_WIDTH // 128):
            zc = z[:, 128 * c:128 * (c + 1)]
            rot = jnp.where(low, pltpu.roll(zc, 120, 1), pltpu.roll(zc, 8, 1))
            parts.append(zc * cos + rot * sin)
        return jnp.concatenate(parts, axis=1)

    dq_ref[0] = (rope(q) * (DA_DIM ** -0.5 * math.log2(math.e))).astype(BF16)
    dk_ref[0] = rope(k).astype(BF16)

    gpre = _dot3(h, wg_ref[...]) + gb_ref[...]
    is_forget = (lax.broadcasted_iota(jnp.int32, (1, LANES), 1) % 8) >= 4
    logsig = jnp.minimum(gpre, 0.0) - jnp.log(1.0 + jnp.exp(-jnp.abs(gpre)))
    gl = jnp.where(is_forget, logsig, gpre)
    gl_ref[0] = gl
    glt_ref[0] = gl.T[:N_GATES]

    pc = _dot(ha.astype(BF16), wc_ref[...])
    first = (t == 0) | (t == n_lat)
    last = (t == n_lat - 1) | (t == n_tiles - 1)
    ridx = lax.broadcasted_iota(jnp.int32, (TOK + 16, 1), 0)
    pc = jnp.where(((ridx < 8) & first) | ((ridx >= TOK + 8) & last), 0.0, pc)
    cw = cw_ref[...]
    conv = cb_ref[...] + pc[7:7 + TOK] * cw[0:1] + pc[8:8 + TOK] * cw[1:2] + pc[9:9 + TOK] * cw[2:3]
    act = _silu(conv)
    mq_ref[0] = act[:, :MP_WIDTH].astype(BF16)
    mk_ref[0] = (act[:, MP_WIDTH:] * (M_DIM ** -0.5)).astype(BF16)


def _inproj(xu, ada_l, g1, wm, wc, wvt, wg, gb, cs, cos_t, sin_t, cw, cb, *, n_lat, n2):
    B, NT, _ = xu.shape
    nt = n_lat + 1
    rper = n2 // TOK
    tok3 = lambda w: pl.BlockSpec((1, TOK, w), lambda b, t: (b, t, 0))
    full = lambda a: pl.BlockSpec(a.shape, lambda b, t: (0,) * a.ndim)
    nb8 = NT // 8
    outs = [jax.ShapeDtypeStruct((B, 2 * FFT_N1, n2, 2 * F_WIDTH), F32)]
    nq = -(-NT // Q_TILE) * Q_TILE
    outs += [jax.ShapeDtypeStruct((B, nq, DA_WIDTH), BF16), jax.ShapeDtypeStruct((B, nt * TOK, DA_WIDTH), BF16)]
    outs += [jax.ShapeDtypeStruct((B, DA_WIDTH, nt * TOK), BF16)]
    trs = lambda r: pl.BlockSpec((1, r, TOK), lambda b, t: (b, 0, t))
    trp = jax.ShapeDtypeStruct((B, MP_WIDTH, NT), BF16)
    outs += [trp, jax.ShapeDtypeStruct((B, NT, MP_WIDTH), BF16), jax.ShapeDtypeStruct((B, NT, MP_WIDTH), BF16), trp]
    outs += [jax.ShapeDtypeStruct((B, NT, 128), F32), jax.ShapeDtypeStruct((B, N_GATES, nt * TOK), F32)]
    out_specs = [pl.BlockSpec((1, 1, TOK, 2 * F_WIDTH), lambda b, t: (b, t // rper, t % rper, 0))]
    out_specs += [tok3(DA_WIDTH)] * 2 + [trs(DA_WIDTH)]
    out_specs += [trs(MP_WIDTH), tok3(MP_WIDTH), tok3(MP_WIDTH), trs(MP_WIDTH)]
    out_specs += [tok3(128), pl.BlockSpec((1, N_GATES, TOK), lambda b, t: (b, 0, t))]
    return pl.pallas_call(
        functools.partial(_inproj_kernel, n_lat=n_lat),
        out_shape=outs,
        grid=(B, nt),
        in_specs=[tok3(D),
                  pl.BlockSpec((1, 8, D), lambda b, t: (b, jnp.maximum(t * (TOK // 8) - 1, 0), 0)),
                  pl.BlockSpec((1, 8, D), lambda b, t: (b, jnp.minimum((t + 1) * (TOK // 8), nb8 - 1), 0)),
                  full(ada_l), full(g1), full(wm), full(wc), full(wvt), full(wg), full(gb), full(cs),
                  pl.BlockSpec((TOK, LANES), lambda b, t: (t, 0)),
                  pl.BlockSpec((TOK, LANES), lambda b, t: (t, 0)),
                  full(cw), full(cb)],
        out_specs=out_specs,
        compiler_params=_cp(("arbitrary", "arbitrary"), VMEM_LIMIT),
        name="norm1_inproj",
    )(xu, xu, xu, ada_l, g1, wm, wc, wvt, wg, gb, cs, cos_t, sin_t, cw, cb)


def _fft1_kernel(y_ref, kc_ref, ks_ref, tc_ref, ts_ref, o_ref, *, groups):
    kc = _split(kc_ref[...])
    ks = _split(ks_ref[...])
    for g in range(groups):
        blk = _split(y_ref[0, :, 8 * g:8 * (g + 1), :].reshape(FFT_N1 * 8, 2 * F_WIDTH))
        p = _dot3(kc, blk)
        q = _dot3(ks, blk)
        ar = p[:, :F_WIDTH] - q[:, F_WIDTH:]
        ai = -p[:, F_WIDTH:] - q[:, :F_WIDTH]
        tc = tc_ref[128 * g:128 * (g + 1), :]
        ts = ts_ref[128 * g:128 * (g + 1), :]
        tc = jnp.concatenate([tc, tc], axis=1)
        ts = jnp.concatenate([ts, ts], axis=1)
        br = ar * tc + ai * ts
        bi = ai * tc - ar * ts
        o_ref[0, :, 8 * g:8 * (g + 1), :] = jnp.concatenate([br, bi], axis=1).reshape(FFT_N1, 8, 2 * F_WIDTH)


def _fft2_kernel(b_ref, c2_ref, s2_ref, wb_ref, perm_ref, o_ref, r_scr, *, n2):
    c2 = _split(c2_ref[...])
    s2 = _split(s2_ref[...])
    for i in range(8):
        blk = b_ref[0, i]
        xr = _dot3(c2, blk[:, :F_WIDTH]) + _dot3(s2, blk[:, F_WIDTH:])
        r_scr[i] = _dot(xr.astype(BF16), wb_ref[...]).astype(BF16)
    for t in range(n2 // 32):
        rows = jnp.concatenate([r_scr[i, 32 * t:32 * (t + 1), :] for i in range(8)], axis=0)
        o_ref[0, 32 * t:32 * (t + 1), :, :] = _dot(perm_ref[...], rows).reshape(32, 8, F_WIDTH)


def _fftc_kernel(y_ref, c_ref, s_ref, wb_ref, o_ref):
    y = y_ref[0, 0]
    z = _dot3(c_ref[...], y[:, :F_WIDTH]) - _dot3(s_ref[...], y[:, F_WIDTH:])
    o_ref[0] = _dot(z.astype(BF16), wb_ref[...])


def _fourier_tables(n, ctx):
    n1, n2 = FFT_N1, n // FFT_N1
    a = np.arange(n1)
    ang1 = 2 * np.pi * np.outer(a, a) / n1
    eye8 = np.eye(8)
    kc = np.kron(np.cos(ang1), eye8)
    ks = np.kron(np.sin(ang1), eye8)
    n2i = np.arange(n2).reshape(n2 // 8, 1, 8)
    k1 = np.arange(n1).reshape(1, n1, 1)
    angt = (2 * np.pi * n2i * k1 / n).reshape(-1, 1)
    tc = np.broadcast_to(np.cos(angt), (n2 // 8 * 128, 128))
    ts = np.broadcast_to(np.sin(angt), (n2 // 8 * 128, 128))
    b = np.arange(n2)
    ang2 = 2 * np.pi * np.outer(b, b) / n2
    c2 = np.cos(ang2) / math.sqrt(n)
    s2 = np.sin(ang2) / math.sqrt(n)
    perm = np.zeros((256, 256))
    for kk in range(8):
        for j in range(32):
            perm[j * 8 + kk, kk * 32 + j] = 1.0
    cc = np.arange(ctx)
    angc = 2 * np.pi * np.outer(cc, cc) / ctx
    cctx = np.cos(angc) / math.sqrt(ctx)
    sctx = np.sin(angc) / math.sqrt(ctx)
    ch = np.arange(F_GDIM)
    angch = 2 * np.pi * np.outer(ch, ch) / F_GDIM
    cs = np.concatenate([np.kron(np.eye(F_GROUPS), np.cos(angch)),
                         np.kron(np.eye(F_GROUPS), np.sin(angch))], axis=1) / math.sqrt(F_GDIM)
    f = lambda z: jnp.asarray(np.ascontiguousarray(z), dtype=F32)
    return dict(kc=f(kc), ks=f(ks), tc=f(tc), ts=f(ts), c2=f(c2), s2=f(s2), perm=f(perm).astype(BF16),
                cctx=f(cctx), sctx=f(sctx), cs=f(cs))


def _fourier(y4, tabs, wblk, *, n, ctx, with_ctx):
    B = y4.shape[0]
    n2 = n // FFT_N1
    groups = 4
    full = lambda a, nd: pl.BlockSpec(a.shape, lambda *i: (0,) * a.ndim)
    b4 = pl.pallas_call(
        functools.partial(_fft1_kernel, groups=groups),
        out_shape=jax.ShapeDtypeStruct((B, FFT_N1, n2, 2 * F_WIDTH), F32),
        grid=(B, n2 // (8 * groups)),
        in_specs=[pl.BlockSpec((1, FFT_N1, 8 * groups, 2 * F_WIDTH), lambda b, j: (b, 0, j, 0)),
                  full(tabs["kc"], 2), full(tabs["ks"], 2),
                  pl.BlockSpec((128 * groups, 128), lambda b, j: (j, 0)),
                  pl.BlockSpec((128 * groups, 128), lambda b, j: (j, 0))],
        out_specs=pl.BlockSpec((1, FFT_N1, 8 * groups, 2 * F_WIDTH), lambda b, j: (b, 0, j, 0)),
        compiler_params=_cp(("arbitrary", "arbitrary")),
        name="fourier_stage1",
    )(y4, tabs["kc"], tabs["ks"], tabs["tc"], tabs["ts"])
    f4 = pl.pallas_call(
        functools.partial(_fft2_kernel, n2=n2),
        out_shape=jax.ShapeDtypeStruct((B, n2, 16, F_WIDTH), F32),
        grid=(B, FFT_N1 // 8),
        in_specs=[pl.BlockSpec((1, 8, n2, 2 * F_WIDTH), lambda b, j: (b, j, 0, 0)),
                  full(tabs["c2"], 2), full(tabs["s2"], 2), full(wblk, 2), full(tabs["perm"], 2)],
        out_specs=pl.BlockSpec((1, n2, 8, F_WIDTH), lambda b, j: (b, 0, j, 0)),
        scratch_shapes=[pltpu.VMEM((8, n2, F_WIDTH), BF16)],
        compiler_params=_cp(("arbitrary", "arbitrary"), VMEM_LIMIT),
        name="fourier_stage2",
    )(b4, tabs["c2"], tabs["s2"], wblk, tabs["perm"])
    f_ctx = None
    if with_ctx:
        f_ctx = pl.pallas_call(
            _fftc_kernel,
            out_shape=jax.ShapeDtypeStruct((B, ctx, F_WIDTH), F32),
            grid=(B,),
            in_specs=[pl.BlockSpec((1, 1, TOK, 2 * F_WIDTH), lambda b: (b, FFT_N1, 0, 0)),
                      full(tabs["cctx"], 1), full(tabs["sctx"], 1), full(wblk, 1)],
            out_specs=pl.BlockSpec((1, ctx, F_WIDTH), lambda b: (b, 0, 0)),
            compiler_params=_cp(("arbitrary",)),
            name="fourier_ctx",
        )(y4, tabs["cctx"], tabs["sctx"], wblk)
    return f4.reshape(B, n, F_WIDTH), f_ctx


VROWS = DA_VDIM + 16


def _attn_kernel(q_ref, k_ref, vt_ref, dl_ref, g_ref, o_ref, m_scr, acc_scr, *, lam_init):
    kt = pl.program_id(3)
    nk = pl.num_programs(3)

    @pl.when(kt == 0)
    def _():
        m_scr[...] = jnp.full(m_scr.shape, NEG, F32)
        acc_scr[...] = jnp.zeros(acc_scr.shape, F32)

    q = q_ref[0]
    k = k_ref[0]
    vt = vt_ref[0]
    ones = jnp.ones((16, vt.shape[1]), BF16)
    lhs = [jnp.concatenate([vt[DA_VDIM * h:DA_VDIM * (h + 1)], ones], axis=0) for h in range(2)]
    lane = lax.broadcasted_iota(jnp.int32, (1, LANES), 1)
    zero = jnp.zeros((), BF16)

    def scores(j):
        return _dot_nt(k, jnp.where((lane // DA_DIM) == j, q, zero))

    st_next = scores(0)
    for j in range(4):
        st = st_next
        if j < 3:
            st_next = scores(j + 1)
        m_old = m_scr[j]
        m_new = jnp.maximum(m_old, jnp.max(st, axis=0, keepdims=True))
        alpha = jnp.exp2(m_old - m_new)
        pt = jnp.exp2(st - m_new).astype(BF16)
        acc_scr[j] = alpha * acc_scr[j] + _dot(lhs[j // 2], pt)
        m_scr[j] = m_new

    @pl.when(kt == nk - 1)
    def _():
        dl = dl_ref[...]
        lam = (jnp.exp(jnp.sum(dl[0:1] * dl[1:2], keepdims=True))
               - jnp.exp(jnp.sum(dl[2:3] * dl[3:4], keepdims=True)) + lam_init)
        outs = []
        for h in range(2):
            a0 = acc_scr[2 * h]
            a1 = acc_scr[2 * h + 1]
            o = (a0[:DA_VDIM] / a0[DA_VDIM:DA_VDIM + 1]
                 - lam * (a1[:DA_VDIM] / a1[DA_VDIM:DA_VDIM + 1]))
            r = lax.rsqrt(jnp.mean(o * o, axis=0, keepdims=True) + EPS)
            outs.append(((o * r) * g_ref[...]) * (1.0 - lam_init))
        o_ref[0] = jnp.concatenate(outs, axis=0).astype(BF16)


def _attention(dq, dk, dvT, dlam, gcol, *, lam_init, tq, q0, nq, tk, k0, nk):
    B = dq.shape[0]
    return pl.pallas_call(
        functools.partial(_attn_kernel, lam_init=lam_init),
        out_shape=jax.ShapeDtypeStruct((B, DA_WIDTH, nq * tq), BF16),
        grid=(B, DA_WIDTH // 128, nq, nk),
        in_specs=[pl.BlockSpec((1, tq, 128), lambda b, p, i, j: (b, q0 + i, p)),
                  pl.BlockSpec((1, tk, 128), lambda b, p, i, j: (b, k0 + j, p)),
                  pl.BlockSpec((1, 128, tk), lambda b, p, i, j: (b, p, k0 + j)),
                  pl.BlockSpec(dlam.shape, lambda b, p, i, j: (0, 0)),
                  pl.BlockSpec(gcol.shape, lambda b, p, i, j: (0, 0))],
        out_specs=pl.BlockSpec((1, 128, tq), lambda b, p, i, j: (b, p, i)),
        scratch_shapes=[pltpu.VMEM((4, 1, tq), F32), pltpu.VMEM((4, VROWS, tq), F32)],
        compiler_params=_cp(("arbitrary",) * 4, VMEM_LIMIT),
        name="diff_attention",
    )(dq, dk, dvT, dlam, gcol)


def _mlstm_kernel(qf_ref, kf_ref, vf_ref, gcf_ref, grf_ref, qb_ref, kb_ref, vb_ref, gcb_ref, grb_ref,
                  hf_ref, hb_ref, c_scr, m_scr):
    t = pl.program_id(1)

    @pl.when(t == 0)
    def _():
        c_scr[...] = jnp.zeros(c_scr.shape, F32)
        m_scr[...] = jnp.zeros(m_scr.shape, F32)

    L = TOK
    si = lax.broadcasted_iota(jnp.int32, (L, L), 0)
    li = lax.broadcasted_iota(jnp.int32, (L, L), 1)
    dirs = ((qf_ref, kf_ref, vf_ref, gcf_ref, grf_ref, hf_ref, si <= li, li <= si, L - 1),
            (qb_ref, kb_ref, vb_ref, gcb_ref, grb_ref, hb_ref, si >= li, li >= si, 0))
    ones = jnp.ones((16, L), F32)
    for d, (q_ref, k_ref, vt_ref, gc_ref, gr_ref, h_ref, seen, seen_t, last) in enumerate(dirs):
        gc = gc_ref[0]
        gr = gr_ref[0]
        seen_b = jnp.where(seen, 1.0, 0.0).astype(BF16)
        seen_tb = jnp.where(seen_t, 1.0, 0.0).astype(BF16)
        bcols = sum(_dot(seen_tb, piece) for piece in _split3(gc))
        brows = sum(_dot(piece, seen_b) for piece in _split3(gr))
        for hd in range(M_HEADS):
            idx = d * M_HEADS + hd
            ji = d * 8 + hd
            jf = d * 8 + 4 + hd
            sl = slice(M_PAD * hd, M_PAD * (hd + 1))
            q = q_ref[0, :, sl]
            k = k_ref[0, :, sl]
            vt = vt_ref[0, sl, :]
            b_row = brows[jf:jf + 1, :]
            cs = gc[:, ji:ji + 1] - bcols[:, jf:jf + 1]
            li_row = gr[ji:ji + 1, :]
            m_old = m_scr[idx][0:1, 0:1]
            c_old = c_scr[idx]

            dlog = jnp.where(seen, b_row + cs, NEG)
            inter = b_row + m_old
            m_t = jnp.maximum(inter, jnp.max(dlog, axis=0, keepdims=True))
            w_inter = jnp.exp(inter - m_t)
            st = _dot_nt(k, q) * jnp.exp(dlog - m_t)
            cq = _dot_nt(c_old.astype(BF16), q)
            num = w_inter * cq[:M_PAD] + _dot(vt, st.astype(BF16))
            den = w_inter * cq[M_PAD:M_PAD + 1] + jnp.sum(st, axis=0, keepdims=True)
            h_ref[0, sl, :] = num / jnp.maximum(jnp.abs(den), jnp.exp(-m_t))

            total = b_row[:, last:last + 1]
            wlog = total - b_row + li_row
            m_new = jnp.maximum(total + m_old, jnp.max(wlog, axis=1, keepdims=True))
            decay = jnp.exp(total + m_old - m_new)
            w = jnp.exp(wlog - m_new)
            vw = jnp.concatenate([vt.astype(F32) * w, ones * w], axis=0).astype(BF16)
            c_scr[idx] = decay * c_old + _dot(vw, k)
            m_scr[idx] = jnp.broadcast_to(m_new, (8, 128))


def _mlstm(mq, mk, mvT, gl, glT, *, n_lat):
    B, NT, _ = mq.shape
    nt = n_lat + 1
    fwd = lambda t: jnp.where(t == 0, n_lat, t - 1)
    bwd = lambda t: jnp.where(t == 0, n_lat, n_lat - t)
    tok = lambda w, f: pl.BlockSpec((1, TOK, w), lambda b, t: (b, f(t), 0))
    lanes = lambda r, f: pl.BlockSpec((1, r, TOK), lambda b, t: (b, 0, f(t)))
    ins, specs = [], []
    for f in (fwd, bwd):
        ins += [mq, mk, mvT, gl, glT]
        specs += [tok(MP_WIDTH, f)] * 2 + [lanes(MP_WIDTH, f), tok(128, f), lanes(N_GATES, f)]
    return pl.pallas_call(
        _mlstm_kernel,
        out_shape=[jax.ShapeDtypeStruct((B, MP_WIDTH, NT), F32)] * 2,
        grid=(B, nt),
        in_specs=specs,
        out_specs=[lanes(MP_WIDTH, fwd), lanes(MP_WIDTH, bwd)],
        scratch_shapes=[pltpu.VMEM((2 * M_HEADS, M_PAD + 16, M_PAD), F32),
                        pltpu.VMEM((2 * M_HEADS, 8, 128), F32)],
        compiler_params=_cp(("arbitrary", "arbitrary"), VMEM_LIMIT),
        name="mlstm",
    )(*ins)


def _outproj_kernel(x_ref, f_ref, dat_ref, hf_ref, hb_ref, mo_ref, ada_ref, mg_ref, wo_ref, wod_ref, g2_ref, wr_ref,
                    xo_ref, hl_ref, pt_ref, *, is_ctx):
    b = pl.program_id(0)
    mod = ada_ref[CTX_ROW if is_ctx else b]
    gt1, sh2, sc2 = mod[2:3], mod[3:4], mod[4:5]
    mg = mg_ref[...]
    for s in range(x_ref.shape[1] // TOK):
        tk = slice(TOK * s, TOK * (s + 1))
        hs = hf_ref[0, :, tk] + hb_ref[0, :, tk]
        og = mo_ref[0, :, tk].astype(F32)
        parts = [dat_ref[0, :, tk]]
        for hd in range(M_HEADS):
            sl = slice(M_PAD * hd, M_PAD * (hd + 1))
            hh = hs[sl]
            r = lax.rsqrt(jnp.sum(hh * hh, axis=0, keepdims=True) * (1.0 / M_DIM) + EPS)
            parts.append((((hh * r) * mg[sl]) * _sigmoid(og[sl])).astype(BF16))
        mix_t = jnp.concatenate(parts, axis=0)
        upd = _dot(f_ref[0, tk].astype(BF16), wo_ref[...]) + lax.dot_general(
            mix_t, wod_ref[...], (((0,), (0,)), ((), ())), preferred_element_type=F32)
        xn = x_ref[0, tk] + gt1 * upd
        xo_ref[0, tk] = xn
        r = lax.rsqrt(jnp.mean(xn * xn, axis=-1, keepdims=True) + EPS)
        h2 = (xn * r) * g2_ref[...] * (1.0 + sc2) + sh2
        hl_ref[0, tk] = h2.astype(BF16)
        lt = _dot3(h2, wr_ref[...]).T[:N_EXPERTS]
        ex = jnp.exp(lt - jnp.max(lt, axis=0, keepdims=True))
        pt_ref[0, :, tk] = ex / jnp.sum(ex, axis=0, keepdims=True)


def _outproj_kernel_aliased(x_ref, f_ref, dat_ref, hf_ref, hb_ref, mo_ref, ada_ref, mg_ref, wo_ref, wod_ref,
                            g2_ref, wr_ref, hlp_ref, xo_ref, hl_ref, pt_ref, *, is_ctx):
    del hlp_ref
    _outproj_kernel(x_ref, f_ref, dat_ref, hf_ref, hb_ref, mo_ref, ada_ref, mg_ref, wo_ref, wod_ref, g2_ref,
                    wr_ref, xo_ref, hl_ref, pt_ref, is_ctx=is_ctx)


def _outproj(xu, f, daT, hf, hb, mo, ada_l, mg, wo, wod, g2, wrp, hl_prev, *, t0, ntl, is_ctx):
    B, NT, _ = xu.shape
    n = ntl * TOK
    tile = 2 * TOK if n % (2 * TOK) == 0 else TOK
    o = t0 * TOK // tile
    tok = lambda w: pl.BlockSpec((1, tile, w), lambda b, t: (b, o + t, 0))
    trs = lambda r: pl.BlockSpec((1, r, tile), lambda b, t: (b, 0, o + t))
    loc = lambda w: pl.BlockSpec((1, tile, w), lambda b, t: (b, t, 0))
    full = lambda a: pl.BlockSpec(a.shape, lambda b, t: (0,) * a.ndim)
    return pl.pallas_call(
        functools.partial(_outproj_kernel_aliased, is_ctx=is_ctx),
        out_shape=[jax.ShapeDtypeStruct(xu.shape, F32), jax.ShapeDtypeStruct((B, NT, D), BF16),
                   jax.ShapeDtypeStruct((B, N_EXPERTS, n), F32)],
        grid=(B, n // tile),
        in_specs=[tok(D), loc(F_WIDTH), pl.BlockSpec((1, DA_WIDTH, tile), lambda b, t: (b, 0, t)),
                  trs(MP_WIDTH), trs(MP_WIDTH), trs(MP_WIDTH),
                  full(ada_l), full(mg), full(wo), full(wod), full(g2), full(wrp),
                  pl.BlockSpec(memory_space=pl.ANY)],
        out_specs=[tok(D), tok(D), pl.BlockSpec((1, N_EXPERTS, tile), lambda b, t: (b, 0, t))],
        input_output_aliases={0: 0, 12: 1},
        compiler_params=_cp(("arbitrary", "arbitrary"), VMEM_LIMIT),
        name="outproj_norm2_router",
    )(xu, f, daT, hf, hb, mo, ada_l, mg, wo, wod, g2, wrp, hl_prev)


def _select_kernel(p_ref, rank_ref, offs_ref, *, n, cap):
    p = p_ref[0]
    xi = pltpu.bitcast(p, jnp.int32)

    def body(i, lo):
        cand = lo | jnp.left_shift(jnp.int32(1), 30 - i)
        cnt = jnp.sum(jnp.where(xi >= cand, 1.0, 0.0), axis=1, keepdims=True)
        return jnp.where(cnt >= cap, cand, lo)

    thr = lax.fori_loop(0, 31, body, jnp.zeros((N_EXPERTS, 1), jnp.int32))
    nb = n // TOK
    rows = lax.broadcasted_iota(jnp.int32, (n, 128), 0)
    cols = lax.broadcasted_iota(jnp.int32, (n, 128), 1)
    blk_ind = jnp.where((rows // TOK) == cols, 1.0, 0.0).astype(BF16)
    u128 = jnp.where(lax.broadcasted_iota(jnp.int32, (128, 128), 0)
                     < lax.broadcasted_iota(jnp.int32, (128, 128), 1), 1.0, 0.0).astype(BF16)
    utok = jnp.where(lax.broadcasted_iota(jnp.int32, (TOK, TOK), 0)
                     < lax.broadcasted_iota(jnp.int32, (TOK, TOK), 1), 1.0, 0.0).astype(BF16)

    def prefix(mf):
        mb = mf.astype(BF16)
        counts = _dot(mb, blk_ind)
        offs = _dot(counts.astype(BF16), u128)
        pieces = [_dot(mb[:, TOK * j:TOK * (j + 1)], utok) + offs[:, j:j + 1] for j in range(nb)]
        return (jnp.concatenate(pieces, axis=1) if nb > 1 else pieces[0]), offs

    gt = xi > thr
    eq = xi == thr
    need = cap - jnp.sum(jnp.where(gt, 1.0, 0.0), axis=1, keepdims=True)
    rank_eq, _ = prefix(jnp.where(eq, 1.0, 0.0))
    sel = gt | (eq & (rank_eq < need))
    rank, offs = prefix(jnp.where(sel, 1.0, 0.0))
    rank_ref[0] = jnp.where(sel, rank, -1.0)
    offs_ref[0] = offs.astype(jnp.int32)


def _select(pt, *, cap):
    B, _, n = pt.shape
    return pl.pallas_call(
        functools.partial(_select_kernel, n=n, cap=cap),
        out_shape=[jax.ShapeDtypeStruct((B, N_EXPERTS, n), F32),
                   jax.ShapeDtypeStruct((B, N_EXPERTS, 128), jnp.int32)],
        grid=(B,),
        in_specs=[pl.BlockSpec((1, N_EXPERTS, n), lambda b: (b, 0, 0))],
        out_specs=[pl.BlockSpec((1, N_EXPERTS, n), lambda b: (b, 0, 0)),
                   pl.BlockSpec((1, N_EXPERTS, 128), lambda b: (b, 0, 0))],
        compiler_params=_cp(("arbitrary",), VMEM_LIMIT),
        name="expert_choice_select",
    )(pt)


def _gather_kernel(offs_ref, h_ref, rank_ref, prob_ref, o_ref, gate_ref, *, eg, per):
    b, g, tb = pl.program_id(0), pl.program_id(1), pl.program_id(2)

    @pl.when(tb == 0)
    def _():
        o_ref[...] = jnp.zeros(o_ref.shape, BF16)
        gate_ref[...] = jnp.zeros(gate_ref.shape, F32)

    cap_pad = o_ref.shape[2]
    half = SLOT // 2
    slot = lax.broadcasted_iota(jnp.int32, (SLOT, TOK), 0).astype(F32)

    def add_rows(i, r, p, h, base, start=None):
        hit = r == slot + base.astype(F32)
        if start is not None:
            hit = hit & (r >= start.astype(F32))
        rows = _dot(jnp.where(hit, 1.0, 0.0).astype(BF16), h).astype(BF16)
        o_ref[0, i, pl.ds(base, SLOT), :] = o_ref[0, i, pl.ds(base, SLOT), :] + rows
        gate_ref[0, i, pl.ds(base, SLOT), :] = (gate_ref[0, i, pl.ds(base, SLOT), :]
                                                + jnp.sum(jnp.where(hit, p, 0.0), axis=1, keepdims=True))

    def operands(s, i):
        tk = slice(TOK * s, TOK * (s + 1))
        e = g * eg + i
        return rank_ref[0, pl.ds(e, 1), tk], prob_ref[0, pl.ds(e, 1), tk], h_ref[0, tk, :]

    ends, his = {}, {}
    for s in range(per):
        for i in range(eg):
            e = g * eg + i
            lo = offs_ref[b, e, tb * per + s]
            his[s, i] = offs_ref[b, e, tb * per + s + 1]
            base = pl.multiple_of(jnp.minimum((lo // half) * half, cap_pad - SLOT), half)
            add_rows(i, *operands(s, i), base)
            ends[s, i] = base + SLOT

    for s in range(per):
        for i in range(eg):
            @pl.when(his[s, i] > ends[s, i])
            def _(s=s, i=i):
                def body(t, carry):
                    start = ends[s, i] + t * SLOT
                    base = pl.multiple_of(jnp.minimum(start, cap_pad - SLOT), half)
                    add_rows(i, *operands(s, i), base, start)
                    return carry

                lax.fori_loop(0, (his[s, i] - ends[s, i] + SLOT - 1) // SLOT, body, 0)


def _gather(offs, hl, rank, pt, *, tb_tok, tb0, n, cap_pad, eg):
    B = hl.shape[0]
    per = tb_tok // TOK
    return pl.pallas_call(
        functools.partial(_gather_kernel, eg=eg, per=per),
        out_shape=[jax.ShapeDtypeStruct((B, N_EXPERTS, cap_pad, D), BF16),
                   jax.ShapeDtypeStruct((B, N_EXPERTS, cap_pad, 1), F32)],
        grid_spec=pltpu.PrefetchScalarGridSpec(
            num_scalar_prefetch=1,
            grid=(B, N_EXPERTS // eg, n // tb_tok),
            in_specs=[pl.BlockSpec((1, tb_tok, D), lambda b, g, t, o: (b, tb0 + t, 0)),
                      pl.BlockSpec((1, N_EXPERTS, tb_tok), lambda b, g, t, o: (b, 0, t)),
                      pl.BlockSpec((1, N_EXPERTS, tb_tok), lambda b, g, t, o: (b, 0, t))],
            out_specs=[pl.BlockSpec((1, eg, cap_pad, D), lambda b, g, t, o: (b, g, 0, 0)),
                       pl.BlockSpec((1, eg, cap_pad, 1), lambda b, g, t, o: (b, g, 0, 0))]),
        compiler_params=_cp(("arbitrary",) * 3, VMEM_LIMIT),
        name="expert_gather",
    )(offs, hl, rank, pt)


FFN_ROWS = 1024


def _ffn_kernel(x_ref, gate_ref, w1_ref, w3_ref, w2_ref, y_ref, acc_ref):
    f = pl.program_id(2)

    @pl.when(f == 0)
    def _():
        acc_ref[...] = jnp.zeros(acc_ref.shape, F32)

    w1 = w1_ref[0, 0].astype(BF16)
    w3 = w3_ref[0, 0].astype(BF16)
    w2 = w2_ref[0, 0].astype(BF16)
    mb, _, cap_pad, _ = x_ref.shape
    rows = min(FFN_ROWS, cap_pad)
    for i in range(mb):
        for r in range(0, cap_pad, rows):
            x = x_ref[i, 0, r:r + rows, :]
            hid = (_silu(_dot(x, w1)) * _dot(x, w3)).astype(BF16)
            acc_ref[i * cap_pad + r:i * cap_pad + r + rows, :] += _dot(hid, w2)

    @pl.when(f == pl.num_programs(2) - 1)
    def _():
        gate = gate_ref[...].reshape(-1, 1)
        y_ref[...] = (acc_ref[...] * gate).astype(BF16).reshape(y_ref.shape)


def _ffn(xs, gates, w1, w3, w2, *, layer, mb, tf):
    B, E, cap_pad, _ = xs.shape
    return pl.pallas_call(
        _ffn_kernel,
        out_shape=jax.ShapeDtypeStruct(xs.shape, BF16),
        grid=(E, B // mb, D_FF // tf),
        in_specs=[pl.BlockSpec((mb, 1, cap_pad, D), lambda e, m, f: (m, e, 0, 0)),
                  pl.BlockSpec((mb, 1, cap_pad, 1), lambda e, m, f: (m, e, 0, 0)),
                  pl.BlockSpec((1, 1, D, tf), lambda e, m, f: (layer, e, 0, f)),
                  pl.BlockSpec((1, 1, D, tf), lambda e, m, f: (layer, e, 0, f)),
                  pl.BlockSpec((1, 1, tf, D), lambda e, m, f: (layer, e, f, 0))],
        out_specs=pl.BlockSpec((mb, 1, cap_pad, D), lambda e, m, f: (m, e, 0, 0)),
        scratch_shapes=[pltpu.VMEM((mb * cap_pad, D), F32)],
        compiler_params=_cp(("arbitrary",) * 3, VMEM_LIMIT),
        name="expert_ffn",
    )(xs, gates, w1, w3, w2)


CCOL = 512


def _combine_kernel(offs_ref, x_ref, y_ref, rankc_ref, ada_ref, o_ref, tot_scr, *, per, is_ctx):
    b, tb = pl.program_id(0), pl.program_id(2)
    gt2 = ada_ref[CTX_ROW if is_ctx else b][5:6]
    rc_all = rankc_ref[0]
    cap_pad = y_ref.shape[2]
    half = SLOT // 2
    slot = lax.broadcasted_iota(jnp.int32, (1, SLOT), 1).astype(F32)

    ends, his = {}, {}
    for s in range(per):
        tk = slice(TOK * s, TOK * (s + 1))
        total = jnp.zeros((TOK, tot_scr.shape[1]), F32)
        for e0 in range(0, N_EXPERTS, 2):
            hots, rows = [], []
            for e in (e0, e0 + 1):
                lo = offs_ref[b, e, tb * per + s]
                his[s, e] = offs_ref[b, e, tb * per + s + 1]
                base = pl.multiple_of(jnp.minimum((lo // half) * half, cap_pad - SLOT), half)
                ends[s, e] = base + SLOT
                hots.append(jnp.where(rc_all[tk, e:e + 1] == slot + base.astype(F32), 1.0, 0.0).astype(BF16))
                rows.append(y_ref[0, e, pl.ds(base, SLOT), :])
            total = total + _dot(jnp.concatenate(hots, axis=1), jnp.concatenate(rows, axis=0))
        tot_scr[tk] = total

    for s in range(per):
        tk = slice(TOK * s, TOK * (s + 1))
        for e in range(N_EXPERTS):
            @pl.when(his[s, e] > ends[s, e])
            def _(s=s, e=e, tk=tk):
                rc = rc_all[tk, e:e + 1]

                def body(t, carry):
                    start = ends[s, e] + t * SLOT
                    base = pl.multiple_of(jnp.minimum(start, cap_pad - SLOT), half)
                    hit = (rc == slot + base.astype(F32)) & (rc >= start.astype(F32))
                    tot_scr[tk] += _dot(jnp.where(hit, 1.0, 0.0).astype(BF16), y_ref[0, e, pl.ds(base, SLOT), :])
                    return carry

                lax.fori_loop(0, (his[s, e] - ends[s, e] + SLOT - 1) // SLOT, body, 0)

    o_ref[0] = x_ref[0] + gt2 * tot_scr[...]


def _combine(offs, xu, ys, rank_c, ada_l, *, tb_tok, tb0, n, is_ctx):
    B = xu.shape[0]
    cap_pad = ys.shape[2]
    per = tb_tok // TOK
    return pl.pallas_call(
        functools.partial(_combine_kernel, per=per, is_ctx=is_ctx),
        out_shape=jax.ShapeDtypeStruct(xu.shape, F32),
        grid_spec=pltpu.PrefetchScalarGridSpec(
            num_scalar_prefetch=1,
            grid=(B, D // CCOL, n // tb_tok),
            in_specs=[pl.BlockSpec((1, tb_tok, CCOL), lambda b, c, t, o: (b, tb0 + t, c)),
                      pl.BlockSpec((1, N_EXPERTS, cap_pad, CCOL), lambda b, c, t, o: (b, 0, 0, c),
                                   pipeline_mode=pl.Buffered(1)),
                      pl.BlockSpec((1, tb_tok, N_EXPERTS), lambda b, c, t, o: (b, t, 0)),
                      pl.BlockSpec((ADA_ROWS, ADA_CHUNKS, CCOL), lambda b, c, t, o: (0, 0, c))],
            out_specs=pl.BlockSpec((1, tb_tok, CCOL), lambda b, c, t, o: (b, tb0 + t, c)),
            scratch_shapes=[pltpu.VMEM((tb_tok, CCOL), F32)]),
        input_output_aliases={1: 0},
        compiler_params=_cp(("arbitrary",) * 3, VMEM_LIMIT),
        name="expert_combine",
    )(offs, xu, ys, rank_c, ada_l)


def _moe(xu, hl, pt, ada_l, w1, w3, w2, *, layer, row0, is_ctx):
    B, _, n = pt.shape
    cap = EC_CAPACITY * n // N_EXPERTS
    cap_pad = -(-cap // SLOT) * SLOT
    nb = n // TOK
    rank, offs = _select(pt, cap=cap)
    offs = offs[:, :, :nb + 1]
    gt = min(n, MOE_TOK)
    ct = min(n, COMBINE_TOK)
    xs, gates = _gather(offs, hl, rank, pt, tb_tok=gt, tb0=row0 // gt, n=n, cap_pad=cap_pad, eg=GATHER_EXPERTS)
    mb = 2 if (B % 2 == 0 and cap_pad >= 1024) else (B if cap_pad < 1024 else 1)
    ys = _ffn(xs, gates, w1, w3, w2, layer=layer, mb=mb, tf=FFN_TF)
    rank_c = jnp.swapaxes(rank, 1, 2)
    return _combine(offs, xu, ys, rank_c, ada_l, tb_tok=ct, tb0=row0 // ct, n=n, is_ctx=is_ctx)


def _final_kernel(x_ref, g_ref, o_ref):
    x = x_ref[0]
    r = lax.rsqrt(jnp.mean(x * x, axis=-1, keepdims=True) + EPS)
    o_ref[0] = (x * r) * g_ref[...]


def _final_norm(xu, g, *, n):
    B = xu.shape[0]
    tm = MOE_TOK
    return pl.pallas_call(
        _final_kernel,
        out_shape=jax.ShapeDtypeStruct((B, n, D), F32),
        grid=(B, n // tm),
        in_specs=[pl.BlockSpec((1, tm, D), lambda b, t: (b, t, 0)),
                  pl.BlockSpec((1, D), lambda b, t: (0, 0))],
        out_specs=pl.BlockSpec((1, tm, D), lambda b, t: (b, t, 0)),
        compiler_params=_cp(("arbitrary", "arbitrary")),
        name="final_norm",
    )(xu, g)


def _rope_tables(n, ctx):
    rows = n // GRID_W
    t_row = jnp.repeat(jnp.arange(rows), GRID_W)
    t_col = jnp.tile(jnp.arange(GRID_W), rows)
    nf = DA_DIM // 4
    inv = ROPE_THETA ** (-jnp.arange(nf, dtype=F32) / nf)
    ar = t_row[:, None].astype(F32) * inv
    ac = t_col[:, None].astype(F32) * inv
    ang = jnp.concatenate([ar, ar, ac, ac], axis=-1)
    sign = jnp.where((jnp.arange(DA_DIM) % 16) < 8, -1.0, 1.0).astype(F32)
    cos = jnp.concatenate([jnp.cos(ang), jnp.ones((ctx, DA_DIM), F32)], axis=0)
    sin = jnp.concatenate([jnp.sin(ang) * sign, jnp.zeros((ctx, DA_DIM), F32)], axis=0)
    return jnp.tile(cos, (1, 128 // DA_DIM)), jnp.tile(sin, (1, 128 // DA_DIM))


def _pad_heads_cols(w):
    lead = w.shape[:-1]
    w = w.reshape(lead + (M_HEADS, M_DIM))
    w = jnp.pad(w, [(0, 0)] * len(lead) + [(0, 0), (0, M_PAD - M_DIM)])
    return w.reshape(lead + (MP_WIDTH,))


def _kv_tile(nt):
    for parts in range(1, nt // LANES + 1):
        if nt % parts == 0 and (nt // parts) % LANES == 0 and nt // parts <= KV_TILE_MAX:
            return nt // parts
    raise ValueError(nt)


def kernel(x, c, ctx, c_ctx, ada_w, ada_b, norm1_g, norm2_g, w_in, four_w, m_conv_w, m_conv_b, m_gate_b,
           m_norm_g, d_lam, d_norm_g, w_out, router_w, exp_w1, exp_w3, exp_w2, final_g):
    B, N, _ = x.shape
    CTX = ctx.shape[1]
    depth = w_in.shape[0]
    assert CTX == TOK and N % (FFT_N1 * TOK) == 0 and N % Q_TILE == 0 and B <= CTX_ROW
    NT = N + CTX
    PAD = -NT % MOE_TOK
    n_lat = N // TOK
    n2 = N // FFT_N1

    xu = jnp.concatenate([x, ctx, jnp.zeros((B, PAD, D), F32)], axis=1)
    cvecs = jnp.zeros((ADA_ROWS, D), F32).at[:B].set(c).at[CTX_ROW].set(c_ctx)
    ada = _adaln(cvecs, ada_w, ada_b).reshape(depth, ADA_ROWS, ADA_CHUNKS, D)
    cos_t, sin_t = _rope_tables(N, CTX + PAD)
    tabs = _fourier_tables(N, CTX)
    tk = _kv_tile(NT)
    tq = Q_TILE

    hl = jnp.zeros((B, NT + PAD, D), BF16)
    for layer in range(depth):
        ctx_out = layer < depth - 1
        lam_init = 0.8 - 0.6 * math.exp(-0.3 * layer)
        w = w_in[layer]
        wm = jnp.concatenate([w[:, OFF_F:OFF_DQ], w[:, OFF_DQ:OFF_MO], w[:, OFF_DK:OFF_DV]], axis=1).astype(BF16)
        wvt = jnp.concatenate([w[:, OFF_DV:OFF_MV], _pad_heads_cols(w[:, OFF_MO:OFF_MQ]),
                               _pad_heads_cols(w[:, OFF_MV:OFF_G])], axis=1).T.astype(BF16)
        wg = jnp.pad(w[:, OFF_G:], ((0, 0), (0, LANES - N_GATES)))
        wc = jnp.concatenate([_pad_heads_cols(w[:, OFF_MQ:OFF_MK]), _pad_heads_cols(w[:, OFF_MK:OFF_DK])],
                             axis=1).astype(BF16)
        gb = jnp.pad(m_gate_b[layer], (0, LANES - N_GATES)).reshape(1, LANES)
        cw = jnp.concatenate([_pad_heads_cols(m_conv_w[layer][:, :M_WIDTH]),
                              _pad_heads_cols(m_conv_w[layer][:, M_WIDTH:])], axis=1)
        cb = jnp.concatenate([_pad_heads_cols(m_conv_b[layer][:M_WIDTH]),
                              _pad_heads_cols(m_conv_b[layer][M_WIDTH:])]).reshape(1, 2 * MP_WIDTH)
        ada_l = ada[layer]

        y4, dq, dk, dvT, mo, mq, mk, mv, gl, glT = _inproj(
            xu, ada_l, norm1_g[layer].reshape(1, D), wm, wc, wvt, wg, gb, tabs["cs"], cos_t, sin_t, cw, cb,
            n_lat=n_lat, n2=n2)

        wblk = jnp.zeros((F_WIDTH, F_WIDTH), F32)
        for g in range(F_GROUPS):
            wblk = wblk.at[F_GDIM * g:F_GDIM * (g + 1), F_GDIM * g:F_GDIM * (g + 1)].set(four_w[layer, g])
        f_l, f_c = _fourier(y4, tabs, wblk.astype(BF16), n=N, ctx=CTX, with_ctx=ctx_out)

        dlam = d_lam[layer]
        g2 = d_norm_g[layer].reshape(DA_VDIM, 1)
        da_l = _attention(dq, dk, dvT, dlam, g2, lam_init=lam_init, tq=tq, q0=0, nq=N // tq,
                          tk=tk, k0=0, nk=NT // tk)

        hf, hb = _mlstm(mq, mk, mv, gl, glT, n_lat=n_lat)

        mg = _pad_heads_cols(m_norm_g[layer]).reshape(MP_WIDTH, 1)
        wol = w_out[layer]
        wo = wol[:F_WIDTH].astype(BF16)
        wod = jnp.concatenate([wol[F_WIDTH:F_WIDTH + DA_WIDTH],
                               jnp.pad(wol[F_WIDTH + DA_WIDTH:].reshape(M_HEADS, M_DIM, D),
                                       ((0, 0), (0, M_PAD - M_DIM), (0, 0))).reshape(MP_WIDTH, D)],
                              axis=0).astype(BF16)
        g2n = norm2_g[layer].reshape(1, D)
        wrp = jnp.pad(router_w[layer], ((0, 0), (0, LANES - N_EXPERTS)))
        xu, hl, pt_l = _outproj(xu, f_l, da_l, hf, hb, mo, ada_l, mg, wo, wod, g2n, wrp, hl,
                                t0=0, ntl=n_lat, is_ctx=False)
        if ctx_out:
            da_c = _attention(dq, dk, dvT, dlam, g2, lam_init=lam_init, tq=TOK, q0=n_lat, nq=1,
                              tk=TOK, k0=n_lat, nk=1)
            xu, hl, pt_c = _outproj(xu, f_c, da_c, hf, hb, mo, ada_l, mg, wo, wod, g2n, wrp, hl,
                                    t0=n_lat, ntl=1, is_ctx=True)

        xu = _moe(xu, hl, pt_l, ada_l, exp_w1, exp_w3, exp_w2, layer=layer, row0=0, is_ctx=False)
        if ctx_out:
            xu = _moe(xu, hl, pt_c, ada_l, exp_w1, exp_w3, exp_w2, layer=layer, row0=N, is_ctx=True)

    return _final_norm(xu, final_g.reshape(1, D), n=N)
```

```python
import functools
import math

import numpy as np
import jax
import jax.numpy as jnp
from jax import lax
from jax.experimental import pallas as pl
from jax.experimental.pallas import tpu as pltpu

F32 = jnp.float32
BF16 = jnp.bfloat16
HI = lax.Precision.HIGHEST

D = 1024
EPS = 1e-6
GRID_W = 64
ROPE_THETA = 10000.0
F_GROUPS, F_GDIM = 4, 64
F_WIDTH = F_GROUPS * F_GDIM
DA_HEADS, DA_DIM = 6, 32
DA_VDIM = 2 * DA_DIM
DA_WIDTH = DA_HEADS * DA_VDIM
M_HEADS, M_DIM = 4, 96
M_WIDTH = M_HEADS * M_DIM
M_PAD = 128
MP_WIDTH = M_HEADS * M_PAD
N_GATES = 4 * M_HEADS
N_EXPERTS = 16
EC_CAPACITY = 2
D_FF = 2 * D
ADA_CHUNKS = 6
ADA_ROWS = 8
CTX_ROW = 4

LANES = 128
MXU_DIM = 256
V7X_VMEM_BYTES = 64 * 1024 * 1024

TOK = MXU_DIM
FFT_N1 = 16
SLOT = LANES
Q_TILE = 512
KV_MAX = 8448
MOE_TOK = 512
COMBINE_TOK = 512
FFN_TF = 512
GATHER_EXPERTS = 8
NEG = -1e30

OFF_F = 0
OFF_DQ = OFF_F + F_WIDTH
OFF_MO = OFF_DQ + 2 * DA_HEADS * DA_DIM
OFF_MQ = OFF_MO + M_WIDTH
OFF_MK = OFF_MQ + M_WIDTH
OFF_DK = OFF_MK + M_WIDTH
OFF_DV = OFF_DK + 2 * DA_HEADS * DA_DIM
OFF_MV = OFF_DV + DA_HEADS * DA_VDIM
OFF_G = OFF_MV + M_WIDTH

VMEM_LIMIT = V7X_VMEM_BYTES * 7 // 8


def _cp(sem, vmem=None):
    return pltpu.CompilerParams(dimension_semantics=sem, vmem_limit_bytes=vmem)


def _sigmoid(x):
    return 1.0 / (1.0 + jnp.exp(-x))


def _silu(x):
    return x * _sigmoid(x)


def _dot(a, b, precision=None):
    return jnp.dot(a, b, preferred_element_type=F32, precision=precision)


def _split(a):
    hi = a.astype(BF16)
    return hi, (a - hi.astype(F32)).astype(BF16)


def _dot3(a, b):
    a_hi, a_lo = a if isinstance(a, tuple) else _split(a)
    b_hi, b_lo = b if isinstance(b, tuple) else _split(b)
    return _dot(a_hi, b_hi) + _dot(a_hi, b_lo) + _dot(a_lo, b_hi)


def _split3(a):
    hi = a.astype(BF16)
    r = a - hi.astype(F32)
    mid = r.astype(BF16)
    return hi, mid, (r - mid.astype(F32)).astype(BF16)


def _dot_nt(a, b, precision=None):
    return lax.dot_general(a, b, (((1,), (1,)), ((), ())), preferred_element_type=F32,
                           precision=precision)


def _ada_kernel(c_ref, w_ref, b_ref, o_ref):
    c = c_ref[...]
    o_ref[0] = _dot(_silu(c), w_ref[0], HI) + b_ref[0]


def _adaln(cvecs, ada_w, ada_b):
    depth = ada_w.shape[0]
    tn = 1536
    return pl.pallas_call(
        _ada_kernel,
        out_shape=jax.ShapeDtypeStruct((depth, ADA_ROWS, ADA_CHUNKS * D), F32),
        grid=(depth, ADA_CHUNKS * D // tn),
        in_specs=[pl.BlockSpec((ADA_ROWS, D), lambda l, j: (0, 0)),
                  pl.BlockSpec((1, D, tn), lambda l, j: (l, 0, j)),
                  pl.BlockSpec((1, 1, tn), lambda l, j: (l, 0, j))],
        out_specs=pl.BlockSpec((1, 8, tn), lambda l, j: (l, 0, j)),
        compiler_params=_cp(("arbitrary", "arbitrary")),
        name="adaln",
    )(cvecs, ada_w, ada_b.reshape(depth, 1, ADA_CHUNKS * D))


def _inproj_kernel(x_ref, xp_ref, xn_ref, ada_ref, g_ref, wm_ref, wc_ref, wvt_ref, wg_ref, gb_ref, cs_ref,
                   cos_ref, sin_ref, cw_ref, cb_ref,
                   y_ref, dq_ref, dk_ref, dvt_ref, mo_ref, mq_ref, mk_ref, mv_ref, gl_ref, glt_ref, *, n_lat):
    b = pl.program_id(0)
    t = pl.program_id(1)
    n_tiles = pl.num_programs(1)
    is_ctx = t >= n_lat
    row = jnp.where(is_ctx, CTX_ROW, b)
    mod = ada_ref[row]
    sh, sc = mod[0:1], mod[1:2]

    xa = jnp.concatenate([xp_ref[0], x_ref[0], xn_ref[0]], axis=0)
    r = lax.rsqrt(jnp.mean(xa * xa, axis=-1, keepdims=True) + EPS)
    ha = (xa * r) * g_ref[...] * (1.0 + sc) + sh
    h = ha[8:8 + TOK]
    hb = h.astype(BF16)

    pm = _dot(hb, wm_ref[...])
    o = 0
    pf = pm[:, o:o + F_WIDTH]; o += F_WIDTH
    q = pm[:, o:o + DA_WIDTH]; o += DA_WIDTH
    k = pm[:, o:o + DA_WIDTH]; o += DA_WIDTH
    pt = _dot_nt(wvt_ref[...], hb)
    dvt_ref[0] = pt[:DA_WIDTH].astype(BF16)
    mo_ref[0] = pt[DA_WIDTH:DA_WIDTH + MP_WIDTH].astype(BF16)
    mv_ref[0] = pt[DA_WIDTH + MP_WIDTH:].astype(BF16)

    y_ref[0, 0] = _dot3(pf, cs_ref[...])

    cos = cos_ref[...]
    sin = sin_ref[...]
    lane = lax.broadcasted_iota(jnp.int32, (1, 128), 1)
    low = (lane % 16) < 8

    def rope(z):
        parts = []
        for c in range(DA_WIDTH // 128):
            zc = z[:, 128 * c:128 * (c + 1)]
            rot = jnp.where(low, pltpu.roll(zc, 120, 1), pltpu.roll(zc, 8, 1))
            parts.append(zc * cos + rot * sin)
        return jnp.concatenate(parts, axis=1)

    dq_ref[0] = (rope(q) * (DA_DIM ** -0.5 * math.log2(math.e))).astype(BF16)
    dk_ref[0] = rope(k).astype(BF16)

    gpre = _dot3(h, wg_ref[...]) + gb_ref[...]
    is_forget = (lax.broadcasted_iota(jnp.int32, (1, LANES), 1) % 8) >= 4
    logsig = jnp.minimum(gpre, 0.0) - jnp.log(1.0 + jnp.exp(-jnp.abs(gpre)))
    gl = jnp.where(is_forget, logsig, gpre)
    gl_ref[0] = gl
    glt_ref[0] = gl.T[:N_GATES]

    pc = _dot(ha.astype(BF16), wc_ref[...])
    first = (t == 0) | (t == n_lat)
    last = (t == n_lat - 1) | (t == n_tiles - 1)
    ridx = lax.broadcasted_iota(jnp.int32, (TOK + 16, 1), 0)
    pc = jnp.where(((ridx < 8) & first) | ((ridx >= TOK + 8) & last), 0.0, pc)
    cw = cw_ref[...]
    conv = cb_ref[...] + pc[7:7 + TOK] * cw[0:1] + pc[8:8 + TOK] * cw[1:2] + pc[9:9 + TOK] * cw[2:3]
    act = _silu(conv)
    mq_ref[0] = act[:, :MP_WIDTH].astype(BF16)
    mk_ref[0] = (act[:, MP_WIDTH:] * (M_DIM ** -0.5)).astype(BF16)


def _inproj(xu, ada_l, g1, wm, wc, wvt, wg, gb, cs, cos_t, sin_t, cw, cb, *, n_lat, n2):
    B, NT, _ = xu.shape
    nt = n_lat + 1
    rper = n2 // TOK
    tok3 = lambda w: pl.BlockSpec((1, TOK, w), lambda b, t: (b, t, 0))
    full = lambda a: pl.BlockSpec(a.shape, lambda b, t: (0,) * a.ndim)
    nb8 = NT // 8
    outs = [jax.ShapeDtypeStruct((B, 2 * FFT_N1, n2, 2 * F_WIDTH), F32)]
    nq = -(-NT // Q_TILE) * Q_TILE
    outs += [jax.ShapeDtypeStruct((B, nq, DA_WIDTH), BF16), jax.ShapeDtypeStruct((B, nt * TOK, DA_WIDTH), BF16)]
    outs += [jax.ShapeDtypeStruct((B, DA_WIDTH, nt * TOK), BF16)]
    trs = lambda r: pl.BlockSpec((1, r, TOK), lambda b, t: (b, 0, t))
    trp = jax.ShapeDtypeStruct((B, MP_WIDTH, NT), BF16)
    outs += [trp, jax.ShapeDtypeStruct((B, NT, MP_WIDTH), BF16), jax.ShapeDtypeStruct((B, NT, MP_WIDTH), BF16), trp]
    outs += [jax.ShapeDtypeStruct((B, NT, 128), F32), jax.ShapeDtypeStruct((B, N_GATES, nt * TOK), F32)]
    out_specs = [pl.BlockSpec((1, 1, TOK, 2 * F_WIDTH), lambda b, t: (b, t // rper, t % rper, 0))]
    out_specs += [tok3(DA_WIDTH)] * 2 + [trs(DA_WIDTH)]
    out_specs += [trs(MP_WIDTH), tok3(MP_WIDTH), tok3(MP_WIDTH), trs(MP_WIDTH)]
    out_specs += [tok3(128), pl.BlockSpec((1, N_GATES, TOK), lambda b, t: (b, 0, t))]
    return pl.pallas_call(
        functools.partial(_inproj_kernel, n_lat=n_lat),
        out_shape=outs,
        grid=(B, nt),
        in_specs=[tok3(D),
                  pl.BlockSpec((1, 8, D), lambda b, t: (b, jnp.maximum(t * (TOK // 8) - 1, 0), 0)),
                  pl.BlockSpec((1, 8, D), lambda b, t: (b, jnp.minimum((t + 1) * (TOK // 8), nb8 - 1), 0)),
                  full(ada_l), full(g1), full(wm), full(wc), full(wvt), full(wg), full(gb), full(cs),
                  pl.BlockSpec((TOK, LANES), lambda b, t: (t, 0)),
                  pl.BlockSpec((TOK, LANES), lambda b, t: (t, 0)),
                  full(cw), full(cb)],
        out_specs=out_specs,
        compiler_params=_cp(("arbitrary", "arbitrary"), VMEM_LIMIT),
        name="norm1_inproj",
    )(xu, xu, xu, ada_l, g1, wm, wc, wvt, wg, gb, cs, cos_t, sin_t, cw, cb)


def _fft1_kernel(y_ref, kc_ref, ks_ref, tc_ref, ts_ref, o_ref, *, groups):
    kc = _split(kc_ref[...])
    ks = _split(ks_ref[...])
    for g in range(groups):
        blk = _split(y_ref[0, :, 8 * g:8 * (g + 1), :].reshape(FFT_N1 * 8, 2 * F_WIDTH))
        p = _dot3(kc, blk)
        q = _dot3(ks, blk)
        ar = p[:, :F_WIDTH] - q[:, F_WIDTH:]
        ai = -p[:, F_WIDTH:] - q[:, :F_WIDTH]
        tc = tc_ref[128 * g:128 * (g + 1), :]
        ts = ts_ref[128 * g:128 * (g + 1), :]
        tc = jnp.concatenate([tc, tc], axis=1)
        ts = jnp.concatenate([ts, ts], axis=1)
        br = ar * tc + ai * ts
        bi = ai * tc - ar * ts
        o_ref[0, :, 8 * g:8 * (g + 1), :] = jnp.concatenate([br, bi], axis=1).reshape(FFT_N1, 8, 2 * F_WIDTH)


def _fft2_kernel(b_ref, c2_ref, s2_ref, wb_ref, perm_ref, o_ref, r_scr, *, n2):
    c2 = _split(c2_ref[...])
    s2 = _split(s2_ref[...])
    for i in range(8):
        blk = b_ref[0, i]
        xr = _dot3(c2, blk[:, :F_WIDTH]) + _dot3(s2, blk[:, F_WIDTH:])
        r_scr[i] = _dot(xr.astype(BF16), wb_ref[...]).astype(BF16)
    for t in range(n2 // 32):
        rows = jnp.concatenate([r_scr[i, 32 * t:32 * (t + 1), :] for i in range(8)], axis=0)
        o_ref[0, 32 * t:32 * (t + 1), :, :] = _dot(perm_ref[...], rows).reshape(32, 8, F_WIDTH)


def _fftc_kernel(y_ref, c_ref, s_ref, wb_ref, o_ref):
    y = y_ref[0, 0]
    z = _dot3(c_ref[...], y[:, :F_WIDTH]) - _dot3(s_ref[...], y[:, F_WIDTH:])
    o_ref[0] = _dot(z.astype(BF16), wb_ref[...])


def _fourier_tables(n, ctx):
    n1, n2 = FFT_N1, n // FFT_N1
    a = np.arange(n1)
    ang1 = 2 * np.pi * np.outer(a, a) / n1
    eye8 = np.eye(8)
    kc = np.kron(np.cos(ang1), eye8)
    ks = np.kron(np.sin(ang1), eye8)
    n2i = np.arange(n2).reshape(n2 // 8, 1, 8)
    k1 = np.arange(n1).reshape(1, n1, 1)
    angt = (2 * np.pi * n2i * k1 / n).reshape(-1, 1)
    tc = np.broadcast_to(np.cos(angt), (n2 // 8 * 128, 128))
    ts = np.broadcast_to(np.sin(angt), (n2 // 8 * 128, 128))
    b = np.arange(n2)
    ang2 = 2 * np.pi * np.outer(b, b) / n2
    c2 = np.cos(ang2) / math.sqrt(n)
    s2 = np.sin(ang2) / math.sqrt(n)
    perm = np.zeros((256, 256))
    for kk in range(8):
        for j in range(32):
            perm[j * 8 + kk, kk * 32 + j] = 1.0
    cc = np.arange(ctx)
    angc = 2 * np.pi * np.outer(cc, cc) / ctx
    cctx = np.cos(angc) / math.sqrt(ctx)
    sctx = np.sin(angc) / math.sqrt(ctx)
    ch = np.arange(F_GDIM)
    angch = 2 * np.pi * np.outer(ch, ch) / F_GDIM
    cs = np.concatenate([np.kron(np.eye(F_GROUPS), np.cos(angch)),
                         np.kron(np.eye(F_GROUPS), np.sin(angch))], axis=1) / math.sqrt(F_GDIM)
    f = lambda z: jnp.asarray(np.ascontiguousarray(z), dtype=F32)
    return dict(kc=f(kc), ks=f(ks), tc=f(tc), ts=f(ts), c2=f(c2), s2=f(s2), perm=f(perm).astype(BF16),
                cctx=f(cctx), sctx=f(sctx), cs=f(cs))


def _fourier(y4, tabs, wblk, *, n, ctx, with_ctx):
    B = y4.shape[0]
    n2 = n // FFT_N1
    groups = 4
    full = lambda a, nd: pl.BlockSpec(a.shape, lambda *i: (0,) * a.ndim)
    b4 = pl.pallas_call(
        functools.partial(_fft1_kernel, groups=groups),
        out_shape=jax.ShapeDtypeStruct((B, FFT_N1, n2, 2 * F_WIDTH), F32),
        grid=(B, n2 // (8 * groups)),
        in_specs=[pl.BlockSpec((1, FFT_N1, 8 * groups, 2 * F_WIDTH), lambda b, j: (b, 0, j, 0)),
                  full(tabs["kc"], 2), full(tabs["ks"], 2),
                  pl.BlockSpec((128 * groups, 128), lambda b, j: (j, 0)),
                  pl.BlockSpec((128 * groups, 128), lambda b, j: (j, 0))],
        out_specs=pl.BlockSpec((1, FFT_N1, 8 * groups, 2 * F_WIDTH), lambda b, j: (b, 0, j, 0)),
        compiler_params=_cp(("arbitrary", "arbitrary")),
        name="fourier_stage1",
    )(y4, tabs["kc"], tabs["ks"], tabs["tc"], tabs["ts"])
    f4 = pl.pallas_call(
        functools.partial(_fft2_kernel, n2=n2),
        out_shape=jax.ShapeDtypeStruct((B, n2, 16, F_WIDTH), F32),
        grid=(B, FFT_N1 // 8),
        in_specs=[pl.BlockSpec((1, 8, n2, 2 * F_WIDTH), lambda b, j: (b, j, 0, 0)),
                  full(tabs["c2"], 2), full(tabs["s2"], 2), full(wblk, 2), full(tabs["perm"], 2)],
        out_specs=pl.BlockSpec((1, n2, 8, F_WIDTH), lambda b, j: (b, 0, j, 0)),
        scratch_shapes=[pltpu.VMEM((8, n2, F_WIDTH), BF16)],
        compiler_params=_cp(("arbitrary", "arbitrary"), VMEM_LIMIT),
        name="fourier_stage2",
    )(b4, tabs["c2"], tabs["s2"], wblk, tabs["perm"])
    f_ctx = None
    if with_ctx:
        f_ctx = pl.pallas_call(
            _fftc_kernel,
            out_shape=jax.ShapeDtypeStruct((B, ctx, F_WIDTH), F32),
            grid=(B,),
            in_specs=[pl.BlockSpec((1, 1, TOK, 2 * F_WIDTH), lambda b: (b, FFT_N1, 0, 0)),
                      full(tabs["cctx"], 1), full(tabs["sctx"], 1), full(wblk, 1)],
            out_specs=pl.BlockSpec((1, ctx, F_WIDTH), lambda b: (b, 0, 0)),
            compiler_params=_cp(("arbitrary",)),
            name="fourier_ctx",
        )(y4, tabs["cctx"], tabs["sctx"], wblk)
    return f4.reshape(B, n, F_WIDTH), f_ctx


VROWS = DA_VDIM + 16


def _attn_kernel(q_ref, k_ref, vt_ref, dl_ref, g_ref, o_ref, *, lam_init):
    q = q_ref[0]
    k = k_ref[0]
    vt = vt_ref[0]
    ones = jnp.ones((16, vt.shape[1]), BF16)
    lhs = [jnp.concatenate([vt[DA_VDIM * h:DA_VDIM * (h + 1)], ones], axis=0) for h in range(2)]
    lane = lax.broadcasted_iota(jnp.int32, (1, LANES), 1)
    zero = jnp.zeros((), BF16)

    def scores(j):
        return _dot_nt(k, jnp.where((lane // DA_DIM) == j, q, zero))

    accs = []
    st_next = scores(0)
    for j in range(4):
        st = st_next
        if j < 3:
            st_next = scores(j + 1)
        pt = jnp.exp2(st - jnp.max(st, axis=0, keepdims=True)).astype(BF16)
        accs.append(_dot(lhs[j // 2], pt))

    dl = dl_ref[...]
    lam = (jnp.exp(jnp.sum(dl[0:1] * dl[1:2], keepdims=True))
           - jnp.exp(jnp.sum(dl[2:3] * dl[3:4], keepdims=True)) + lam_init)
    outs = []
    for h in range(2):
        a0, a1 = accs[2 * h], accs[2 * h + 1]
        o = (a0[:DA_VDIM] / a0[DA_VDIM:DA_VDIM + 1]
             - lam * (a1[:DA_VDIM] / a1[DA_VDIM:DA_VDIM + 1]))
        r = lax.rsqrt(jnp.mean(o * o, axis=0, keepdims=True) + EPS)
        outs.append(((o * r) * g_ref[...]) * (1.0 - lam_init))
    o_ref[0] = jnp.concatenate(outs, axis=0).astype(BF16)


def _attention(dq, dk, dvT, dlam, gcol, *, lam_init, tq, q0, nq, tk, k0):
    B = dq.shape[0]
    return pl.pallas_call(
        functools.partial(_attn_kernel, lam_init=lam_init),
        out_shape=jax.ShapeDtypeStruct((B, DA_WIDTH, nq * tq), BF16),
        grid=(B, DA_WIDTH // LANES, nq),
        in_specs=[pl.BlockSpec((1, tq, LANES), lambda b, p, i: (b, q0 + i, p)),
                  pl.BlockSpec((1, tk, LANES), lambda b, p, i: (b, k0, p)),
                  pl.BlockSpec((1, LANES, tk), lambda b, p, i: (b, p, k0)),
                  pl.BlockSpec(dlam.shape, lambda b, p, i: (0, 0)),
                  pl.BlockSpec(gcol.shape, lambda b, p, i: (0, 0))],
        out_specs=pl.BlockSpec((1, LANES, tq), lambda b, p, i: (b, p, i)),
        compiler_params=_cp(("arbitrary",) * 3, VMEM_LIMIT),
        name="diff_attention",
    )(dq, dk, dvT, dlam, gcol)


def _mlstm_kernel(qf_ref, kf_ref, vf_ref, gcf_ref, grf_ref, qb_ref, kb_ref, vb_ref, gcb_ref, grb_ref,
                  hf_ref, hb_ref, c_scr, m_scr):
    t = pl.program_id(1)

    @pl.when(t == 0)
    def _():
        c_scr[...] = jnp.zeros(c_scr.shape, F32)
        m_scr[...] = jnp.zeros(m_scr.shape, F32)

    L = TOK
    si = lax.broadcasted_iota(jnp.int32, (L, L), 0)
    li = lax.broadcasted_iota(jnp.int32, (L, L), 1)
    dirs = ((qf_ref, kf_ref, vf_ref, gcf_ref, grf_ref, hf_ref, si <= li, li <= si, L - 1),
            (qb_ref, kb_ref, vb_ref, gcb_ref, grb_ref, hb_ref, si >= li, li >= si, 0))
    ones = jnp.ones((16, L), F32)
    for d, (q_ref, k_ref, vt_ref, gc_ref, gr_ref, h_ref, seen, seen_t, last) in enumerate(dirs):
        gc = gc_ref[0]
        gr = gr_ref[0]
        seen_b = jnp.where(seen, 1.0, 0.0).astype(BF16)
        seen_tb = jnp.where(seen_t, 1.0, 0.0).astype(BF16)
        bcols = sum(_dot(seen_tb, piece) for piece in _split3(gc))
        brows = sum(_dot(piece, seen_b) for piece in _split3(gr))
        for hd in range(M_HEADS):
            idx = d * M_HEADS + hd
            ji = d * 8 + hd
            jf = d * 8 + 4 + hd
            sl = slice(M_PAD * hd, M_PAD * (hd + 1))
            q = q_ref[0, :, sl]
            k = k_ref[0, :, sl]
            vt = vt_ref[0, sl, :]
            b_row = brows[jf:jf + 1, :]
            cs = gc[:, ji:ji + 1] - bcols[:, jf:jf + 1]
            li_row = gr[ji:ji + 1, :]
            m_old = m_scr[idx][0:1, 0:1]
            c_old = c_scr[idx]

            dlog = jnp.where(seen, b_row + cs, NEG)
            inter = b_row + m_old
            m_t = jnp.maximum(inter, jnp.max(dlog, axis=0, keepdims=True))
            w_inter = jnp.exp(inter - m_t)
            st = _dot_nt(k, q) * jnp.exp(dlog - m_t)
            cq = _dot_nt(c_old.astype(BF16), q)
            num = w_inter * cq[:M_PAD] + _dot(vt, st.astype(BF16))
            den = w_inter * cq[M_PAD:M_PAD + 1] + jnp.sum(st, axis=0, keepdims=True)
            h_ref[0, sl, :] = num / jnp.maximum(jnp.abs(den), jnp.exp(-m_t))

            total = b_row[:, last:last + 1]
            wlog = total - b_row + li_row
            m_new = jnp.maximum(total + m_old, jnp.max(wlog, axis=1, keepdims=True))
            decay = jnp.exp(total + m_old - m_new)
            w = jnp.exp(wlog - m_new)
            vw = jnp.concatenate([vt.astype(F32) * w, ones * w], axis=0).astype(BF16)
            c_scr[idx] = decay * c_old + _dot(vw, k)
            m_scr[idx] = jnp.broadcast_to(m_new, (8, 128))


def _mlstm(mq, mk, mvT, gl, glT, *, n_lat):
    B, NT, _ = mq.shape
    nt = n_lat + 1
    fwd = lambda t: jnp.where(t == 0, n_lat, t - 1)
    bwd = lambda t: jnp.where(t == 0, n_lat, n_lat - t)
    tok = lambda w, f: pl.BlockSpec((1, TOK, w), lambda b, t: (b, f(t), 0))
    lanes = lambda r, f: pl.BlockSpec((1, r, TOK), lambda b, t: (b, 0, f(t)))
    ins, specs = [], []
    for f in (fwd, bwd):
        ins += [mq, mk, mvT, gl, glT]
        specs += [tok(MP_WIDTH, f)] * 2 + [lanes(MP_WIDTH, f), tok(128, f), lanes(N_GATES, f)]
    return pl.pallas_call(
        _mlstm_kernel,
        out_shape=[jax.ShapeDtypeStruct((B, MP_WIDTH, NT), F32)] * 2,
        grid=(B, nt),
        in_specs=specs,
        out_specs=[lanes(MP_WIDTH, fwd), lanes(MP_WIDTH, bwd)],
        scratch_shapes=[pltpu.VMEM((2 * M_HEADS, M_PAD + 16, M_PAD), F32),
                        pltpu.VMEM((2 * M_HEADS, 8, 128), F32)],
        compiler_params=_cp(("arbitrary", "arbitrary"), VMEM_LIMIT),
        name="mlstm",
    )(*ins)


def _outproj_kernel(x_ref, f_ref, dat_ref, hf_ref, hb_ref, mo_ref, ada_ref, mg_ref, wo_ref, wod_ref, g2_ref, wr_ref,
                    xo_ref, hl_ref, pt_ref, *, is_ctx):
    b = pl.program_id(0)
    mod = ada_ref[CTX_ROW if is_ctx else b]
    gt1, sh2, sc2 = mod[2:3], mod[3:4], mod[4:5]
    mg = mg_ref[...]
    for s in range(x_ref.shape[1] // TOK):
        tk = slice(TOK * s, TOK * (s + 1))
        hs = hf_ref[0, :, tk] + hb_ref[0, :, tk]
        og = mo_ref[0, :, tk].astype(F32)
        parts = [dat_ref[0, :, tk]]
        for hd in range(M_HEADS):
            sl = slice(M_PAD * hd, M_PAD * (hd + 1))
            hh = hs[sl]
            r = lax.rsqrt(jnp.sum(hh * hh, axis=0, keepdims=True) * (1.0 / M_DIM) + EPS)
            parts.append((((hh * r) * mg[sl]) * _sigmoid(og[sl])).astype(BF16))
        mix_t = jnp.concatenate(parts, axis=0)
        upd = _dot(f_ref[0, tk].astype(BF16), wo_ref[...]) + lax.dot_general(
            mix_t, wod_ref[...], (((0,), (0,)), ((), ())), preferred_element_type=F32)
        xn = x_ref[0, tk] + gt1 * upd
        xo_ref[0, tk] = xn
        r = lax.rsqrt(jnp.mean(xn * xn, axis=-1, keepdims=True) + EPS)
        h2 = (xn * r) * g2_ref[...] * (1.0 + sc2) + sh2
        hl_ref[0, tk] = h2.astype(BF16)
        lt = _dot3(h2, wr_ref[...]).T[:N_EXPERTS]
        ex = jnp.exp(lt - jnp.max(lt, axis=0, keepdims=True))
        pt_ref[0, :, tk] = ex / jnp.sum(ex, axis=0, keepdims=True)


def _outproj_kernel_aliased(x_ref, f_ref, dat_ref, hf_ref, hb_ref, mo_ref, ada_ref, mg_ref, wo_ref, wod_ref,
                            g2_ref, wr_ref, hlp_ref, xo_ref, hl_ref, pt_ref, *, is_ctx):
    del hlp_ref
    _outproj_kernel(x_ref, f_ref, dat_ref, hf_ref, hb_ref, mo_ref, ada_ref, mg_ref, wo_ref, wod_ref, g2_ref,
                    wr_ref, xo_ref, hl_ref, pt_ref, is_ctx=is_ctx)


def _outproj(xu, f, daT, hf, hb, mo, ada_l, mg, wo, wod, g2, wrp, hl_prev, *, t0, ntl, is_ctx):
    B, NT, _ = xu.shape
    n = ntl * TOK
    tile = 2 * TOK if n % (2 * TOK) == 0 else TOK
    o = t0 * TOK // tile
    tok = lambda w: pl.BlockSpec((1, tile, w), lambda b, t: (b, o + t, 0))
    trs = lambda r: pl.BlockSpec((1, r, tile), lambda b, t: (b, 0, o + t))
    loc = lambda w: pl.BlockSpec((1, tile, w), lambda b, t: (b, t, 0))
    full = lambda a: pl.BlockSpec(a.shape, lambda b, t: (0,) * a.ndim)
    return pl.pallas_call(
        functools.partial(_outproj_kernel_aliased, is_ctx=is_ctx),
        out_shape=[jax.ShapeDtypeStruct(xu.shape, F32), jax.ShapeDtypeStruct((B, NT, D), BF16),
                   jax.ShapeDtypeStruct((B, N_EXPERTS, n), F32)],
        grid=(B, n // tile),
        in_specs=[tok(D), loc(F_WIDTH), pl.BlockSpec((1, DA_WIDTH, tile), lambda b, t: (b, 0, t)),
                  trs(MP_WIDTH), trs(MP_WIDTH), trs(MP_WIDTH),
                  full(ada_l), full(mg), full(wo), full(wod), full(g2), full(wrp),
                  pl.BlockSpec(memory_space=pl.ANY)],
        out_specs=[tok(D), tok(D), pl.BlockSpec((1, N_EXPERTS, tile), lambda b, t: (b, 0, t))],
        input_output_aliases={0: 0, 12: 1},
        compiler_params=_cp(("arbitrary", "arbitrary"), VMEM_LIMIT),
        name="outproj_norm2_router",
    )(xu, f, daT, hf, hb, mo, ada_l, mg, wo, wod, g2, wrp, hl_prev)


def _select_kernel(p_ref, rank_ref, offs_ref, *, n, cap):
    p = p_ref[0]
    xi = pltpu.bitcast(p, jnp.int32)

    def body(i, lo):
        cand = lo | jnp.left_shift(jnp.int32(1), 30 - i)
        cnt = jnp.sum(jnp.where(xi >= cand, 1.0, 0.0), axis=1, keepdims=True)
        return jnp.where(cnt >= cap, cand, lo)

    thr = lax.fori_loop(0, 31, body, jnp.zeros((N_EXPERTS, 1), jnp.int32))
    nb = n // TOK
    rows = lax.broadcasted_iota(jnp.int32, (n, 128), 0)
    cols = lax.broadcasted_iota(jnp.int32, (n, 128), 1)
    blk_ind = jnp.where((rows // TOK) == cols, 1.0, 0.0).astype(BF16)
    u128 = jnp.where(lax.broadcasted_iota(jnp.int32, (128, 128), 0)
                     < lax.broadcasted_iota(jnp.int32, (128, 128), 1), 1.0, 0.0).astype(BF16)
    utok = jnp.where(lax.broadcasted_iota(jnp.int32, (TOK, TOK), 0)
                     < lax.broadcasted_iota(jnp.int32, (TOK, TOK), 1), 1.0, 0.0).astype(BF16)

    def prefix(mf):
        mb = mf.astype(BF16)
        counts = _dot(mb, blk_ind)
        offs = _dot(counts.astype(BF16), u128)
        pieces = [_dot(mb[:, TOK * j:TOK * (j + 1)], utok) + offs[:, j:j + 1] for j in range(nb)]
        return (jnp.concatenate(pieces, axis=1) if nb > 1 else pieces[0]), offs

    gt = xi > thr
    eq = xi == thr
    need = cap - jnp.sum(jnp.where(gt, 1.0, 0.0), axis=1, keepdims=True)
    rank_eq, _ = prefix(jnp.where(eq, 1.0, 0.0))
    sel = gt | (eq & (rank_eq < need))
    rank, offs = prefix(jnp.where(sel, 1.0, 0.0))
    rank_ref[0] = jnp.where(sel, rank, -1.0)
    offs_ref[0] = offs.astype(jnp.int32)


def _select(pt, *, cap):
    B, _, n = pt.shape
    return pl.pallas_call(
        functools.partial(_select_kernel, n=n, cap=cap),
        out_shape=[jax.ShapeDtypeStruct((B, N_EXPERTS, n), F32),
                   jax.ShapeDtypeStruct((B, N_EXPERTS, 128), jnp.int32)],
        grid=(B,),
        in_specs=[pl.BlockSpec((1, N_EXPERTS, n), lambda b: (b, 0, 0))],
        out_specs=[pl.BlockSpec((1, N_EXPERTS, n), lambda b: (b, 0, 0)),
                   pl.BlockSpec((1, N_EXPERTS, 128), lambda b: (b, 0, 0))],
        compiler_params=_cp(("arbitrary",), VMEM_LIMIT),
        name="expert_choice_select",
    )(pt)


def _gather_kernel(offs_ref, h_ref, rank_ref, prob_ref, o_ref, gate_ref, *, eg, per):
    b, g, tb = pl.program_id(0), pl.program_id(1), pl.program_id(2)

    @pl.when(tb == 0)
    def _():
        o_ref[...] = jnp.zeros(o_ref.shape, BF16)
        gate_ref[...] = jnp.zeros(gate_ref.shape, F32)

    cap_pad = o_ref.shape[2]
    half = SLOT // 2
    slot = lax.broadcasted_iota(jnp.int32, (SLOT, TOK), 0).astype(F32)

    def add_rows(i, r, p, h, base, start=None):
        hit = r == slot + base.astype(F32)
        if start is not None:
            hit = hit & (r >= start.astype(F32))
        rows = _dot(jnp.where(hit, 1.0, 0.0).astype(BF16), h).astype(BF16)
        o_ref[0, i, pl.ds(base, SLOT), :] = o_ref[0, i, pl.ds(base, SLOT), :] + rows
        gate_ref[0, i, pl.ds(base, SLOT), :] = (gate_ref[0, i, pl.ds(base, SLOT), :]
                                                + jnp.sum(jnp.where(hit, p, 0.0), axis=1, keepdims=True))

    def operands(s, i):
        tk = slice(TOK * s, TOK * (s + 1))
        e = g * eg + i
        return rank_ref[0, pl.ds(e, 1), tk], prob_ref[0, pl.ds(e, 1), tk], h_ref[0, tk, :]

    ends, his = {}, {}
    for s in range(per):
        for i in range(eg):
            e = g * eg + i
            lo = offs_ref[b, e, tb * per + s]
            his[s, i] = offs_ref[b, e, tb * per + s + 1]
            base = pl.multiple_of(jnp.minimum((lo // half) * half, cap_pad - SLOT), half)
            add_rows(i, *operands(s, i), base)
            ends[s, i] = base + SLOT

    for s in range(per):
        for i in range(eg):
            @pl.when(his[s, i] > ends[s, i])
            def _(s=s, i=i):
                def body(t, carry):
                    start = ends[s, i] + t * SLOT
                    base = pl.multiple_of(jnp.minimum(start, cap_pad - SLOT), half)
                    add_rows(i, *operands(s, i), base, start)
                    return carry

                lax.fori_loop(0, (his[s, i] - ends[s, i] + SLOT - 1) // SLOT, body, 0)


def _gather(offs, hl, rank, pt, *, tb_tok, tb0, n, cap_pad, eg):
    B = hl.shape[0]
    per = tb_tok // TOK
    return pl.pallas_call(
        functools.partial(_gather_kernel, eg=eg, per=per),
        out_shape=[jax.ShapeDtypeStruct((B, N_EXPERTS, cap_pad, D), BF16),
                   jax.ShapeDtypeStruct((B, N_EXPERTS, cap_pad, 1), F32)],
        grid_spec=pltpu.PrefetchScalarGridSpec(
            num_scalar_prefetch=1,
            grid=(B, N_EXPERTS // eg, n // tb_tok),
            in_specs=[pl.BlockSpec((1, tb_tok, D), lambda b, g, t, o: (b, tb0 + t, 0)),
                      pl.BlockSpec((1, N_EXPERTS, tb_tok), lambda b, g, t, o: (b, 0, t)),
                      pl.BlockSpec((1, N_EXPERTS, tb_tok), lambda b, g, t, o: (b, 0, t))],
            out_specs=[pl.BlockSpec((1, eg, cap_pad, D), lambda b, g, t, o: (b, g, 0, 0)),
                       pl.BlockSpec((1, eg, cap_pad, 1), lambda b, g, t, o: (b, g, 0, 0))]),
        compiler_params=_cp(("arbitrary",) * 3, VMEM_LIMIT),
        name="expert_gather",
    )(offs, hl, rank, pt)


FFN_ROWS = 1024


def _ffn_kernel(x_ref, gate_ref, w1_ref, w3_ref, w2_ref, y_ref, acc_ref):
    f = pl.program_id(2)

    @pl.when(f == 0)
    def _():
        acc_ref[...] = jnp.zeros(acc_ref.shape, F32)

    w1 = w1_ref[0, 0].astype(BF16)
    w3 = w3_ref[0, 0].astype(BF16)
    w2 = w2_ref[0, 0].astype(BF16)
    mb, _, cap_pad, _ = x_ref.shape
    rows = min(FFN_ROWS, cap_pad)
    for i in range(mb):
        for r in range(0, cap_pad, rows):
            x = x_ref[i, 0, r:r + rows, :]
            hid = (_silu(_dot(x, w1)) * _dot(x, w3)).astype(BF16)
            acc_ref[i * cap_pad + r:i * cap_pad + r + rows, :] += _dot(hid, w2)

    @pl.when(f == pl.num_programs(2) - 1)
    def _():
        gate = gate_ref[...].reshape(-1, 1)
        y_ref[...] = (acc_ref[...] * gate).astype(BF16).reshape(y_ref.shape)


def _ffn(xs, gates, w1, w3, w2, *, layer, mb, tf):
    B, E, cap_pad, _ = xs.shape
    return pl.pallas_call(
        _ffn_kernel,
        out_shape=jax.ShapeDtypeStruct(xs.shape, BF16),
        grid=(E, B // mb, D_FF // tf),
        in_specs=[pl.BlockSpec((mb, 1, cap_pad, D), lambda e, m, f: (m, e, 0, 0)),
                  pl.BlockSpec((mb, 1, cap_pad, 1), lambda e, m, f: (m, e, 0, 0)),
                  pl.BlockSpec((1, 1, D, tf), lambda e, m, f: (layer, e, 0, f)),
                  pl.BlockSpec((1, 1, D, tf), lambda e, m, f: (layer, e, 0, f)),
                  pl.BlockSpec((1, 1, tf, D), lambda e, m, f: (layer, e, f, 0))],
        out_specs=pl.BlockSpec((mb, 1, cap_pad, D), lambda e, m, f: (m, e, 0, 0)),
        scratch_shapes=[pltpu.VMEM((mb * cap_pad, D), F32)],
        compiler_params=_cp(("arbitrary",) * 3, VMEM_LIMIT),
        name="expert_ffn",
    )(xs, gates, w1, w3, w2)


CCOL = 512


def _combine_kernel(offs_ref, x_ref, y_ref, rankc_ref, ada_ref, o_ref, tot_scr, *, per, is_ctx):
    b, tb = pl.program_id(0), pl.program_id(2)
    gt2 = ada_ref[CTX_ROW if is_ctx else b][5:6]
    rc_all = rankc_ref[0]
    cap_pad = y_ref.shape[2]
    half = SLOT // 2
    slot = lax.broadcasted_iota(jnp.int32, (1, SLOT), 1).astype(F32)

    ends, his = {}, {}
    for s in range(per):
        tk = slice(TOK * s, TOK * (s + 1))
        total = jnp.zeros((TOK, tot_scr.shape[1]), F32)
        for e0 in range(0, N_EXPERTS, 2):
            hots, rows = [], []
            for e in (e0, e0 + 1):
                lo = offs_ref[b, e, tb * per + s]
                his[s, e] = offs_ref[b, e, tb * per + s + 1]
                base = pl.multiple_of(jnp.minimum((lo // half) * half, cap_pad - SLOT), half)
                ends[s, e] = base + SLOT
                hots.append(jnp.where(rc_all[tk, e:e + 1] == slot + base.astype(F32), 1.0, 0.0).astype(BF16))
                rows.append(y_ref[0, e, pl.ds(base, SLOT), :])
            total = total + _dot(jnp.concatenate(hots, axis=1), jnp.concatenate(rows, axis=0))
        tot_scr[tk] = total

    for s in range(per):
        tk = slice(TOK * s, TOK * (s + 1))
        for e in range(N_EXPERTS):
            @pl.when(his[s, e] > ends[s, e])
            def _(s=s, e=e, tk=tk):
                rc = rc_all[tk, e:e + 1]

                def body(t, carry):
                    start = ends[s, e] + t * SLOT
                    base = pl.multiple_of(jnp.minimum(start, cap_pad - SLOT), half)
                    hit = (rc == slot + base.astype(F32)) & (rc >= start.astype(F32))
                    tot_scr[tk] += _dot(jnp.where(hit, 1.0, 0.0).astype(BF16), y_ref[0, e, pl.ds(base, SLOT), :])
                    return carry

                lax.fori_loop(0, (his[s, e] - ends[s, e] + SLOT - 1) // SLOT, body, 0)

    o_ref[0] = x_ref[0] + gt2 * tot_scr[...]


def _combine(offs, xu, ys, rank_c, ada_l, *, tb_tok, tb0, n, is_ctx):
    B = xu.shape[0]
    cap_pad = ys.shape[2]
    per = tb_tok // TOK
    return pl.pallas_call(
        functools.partial(_combine_kernel, per=per, is_ctx=is_ctx),
        out_shape=jax.ShapeDtypeStruct(xu.shape, F32),
        grid_spec=pltpu.PrefetchScalarGridSpec(
            num_scalar_prefetch=1,
            grid=(B, D // CCOL, n // tb_tok),
            in_specs=[pl.BlockSpec((1, tb_tok, CCOL), lambda b, c, t, o: (b, tb0 + t, c)),
                      pl.BlockSpec((1, N_EXPERTS, cap_pad, CCOL), lambda b, c, t, o: (b, 0, 0, c),
                                   pipeline_mode=pl.Buffered(1)),
                      pl.BlockSpec((1, tb_tok, N_EXPERTS), lambda b, c, t, o: (b, t, 0)),
                      pl.BlockSpec((ADA_ROWS, ADA_CHUNKS, CCOL), lambda b, c, t, o: (0, 0, c))],
            out_specs=pl.BlockSpec((1, tb_tok, CCOL), lambda b, c, t, o: (b, tb0 + t, c)),
            scratch_shapes=[pltpu.VMEM((tb_tok, CCOL), F32)]),
        input_output_aliases={1: 0},
        compiler_params=_cp(("arbitrary",) * 3, VMEM_LIMIT),
        name="expert_combine",
    )(offs, xu, ys, rank_c, ada_l)


def _moe(xu, hl, pt, ada_l, w1, w3, w2, *, layer, row0, is_ctx):
    B, _, n = pt.shape
    cap = EC_CAPACITY * n // N_EXPERTS
    cap_pad = -(-cap // SLOT) * SLOT
    nb = n // TOK
    rank, offs = _select(pt, cap=cap)
    offs = offs[:, :, :nb + 1]
    gt = min(n, MOE_TOK)
    ct = min(n, COMBINE_TOK)
    xs, gates = _gather(offs, hl, rank, pt, tb_tok=gt, tb0=row0 // gt, n=n, cap_pad=cap_pad, eg=GATHER_EXPERTS)
    mb = 2 if (B % 2 == 0 and cap_pad >= 1024) else (B if cap_pad < 1024 else 1)
    ys = _ffn(xs, gates, w1, w3, w2, layer=layer, mb=mb, tf=FFN_TF)
    rank_c = jnp.swapaxes(rank, 1, 2)
    return _combine(offs, xu, ys, rank_c, ada_l, tb_tok=ct, tb0=row0 // ct, n=n, is_ctx=is_ctx)


def _final_kernel(x_ref, g_ref, o_ref):
    x = x_ref[0]
    r = lax.rsqrt(jnp.mean(x * x, axis=-1, keepdims=True) + EPS)
    o_ref[0] = (x * r) * g_ref[...]


def _final_norm(xu, g, *, n):
    B = xu.shape[0]
    tm = MOE_TOK
    return pl.pallas_call(
        _final_kernel,
        out_shape=jax.ShapeDtypeStruct((B, n, D), F32),
        grid=(B, n // tm),
        in_specs=[pl.BlockSpec((1, tm, D), lambda b, t: (b, t, 0)),
                  pl.BlockSpec((1, D), lambda b, t: (0, 0))],
        out_specs=pl.BlockSpec((1, tm, D), lambda b, t: (b, t, 0)),
        compiler_params=_cp(("arbitrary", "arbitrary")),
        name="final_norm",
    )(xu, g)


def _rope_tables(n, ctx):
    rows = n // GRID_W
    t_row = jnp.repeat(jnp.arange(rows), GRID_W)
    t_col = jnp.tile(jnp.arange(GRID_W), rows)
    nf = DA_DIM // 4
    inv = ROPE_THETA ** (-jnp.arange(nf, dtype=F32) / nf)
    ar = t_row[:, None].astype(F32) * inv
    ac = t_col[:, None].astype(F32) * inv
    ang = jnp.concatenate([ar, ar, ac, ac], axis=-1)
    sign = jnp.where((jnp.arange(DA_DIM) % 16) < 8, -1.0, 1.0).astype(F32)
    cos = jnp.concatenate([jnp.cos(ang), jnp.ones((ctx, DA_DIM), F32)], axis=0)
    sin = jnp.concatenate([jnp.sin(ang) * sign, jnp.zeros((ctx, DA_DIM), F32)], axis=0)
    return jnp.tile(cos, (1, 128 // DA_DIM)), jnp.tile(sin, (1, 128 // DA_DIM))


def _pad_heads_cols(w):
    lead = w.shape[:-1]
    w = w.reshape(lead + (M_HEADS, M_DIM))
    w = jnp.pad(w, [(0, 0)] * len(lead) + [(0, 0), (0, M_PAD - M_DIM)])
    return w.reshape(lead + (MP_WIDTH,))


def kernel(x, c, ctx, c_ctx, ada_w, ada_b, norm1_g, norm2_g, w_in, four_w, m_conv_w, m_conv_b, m_gate_b,
           m_norm_g, d_lam, d_norm_g, w_out, router_w, exp_w1, exp_w3, exp_w2, final_g):
    B, N, _ = x.shape
    CTX = ctx.shape[1]
    depth = w_in.shape[0]
    assert CTX == TOK and N % (FFT_N1 * TOK) == 0 and N % Q_TILE == 0 and B <= CTX_ROW and N + CTX <= KV_MAX
    NT = N + CTX
    PAD = -NT % MOE_TOK
    n_lat = N // TOK
    n2 = N // FFT_N1

    xu = jnp.concatenate([x, ctx, jnp.zeros((B, PAD, D), F32)], axis=1)
    cvecs = jnp.zeros((ADA_ROWS, D), F32).at[:B].set(c).at[CTX_ROW].set(c_ctx)
    ada = _adaln(cvecs, ada_w, ada_b).reshape(depth, ADA_ROWS, ADA_CHUNKS, D)
    cos_t, sin_t = _rope_tables(N, CTX + PAD)
    tabs = _fourier_tables(N, CTX)
    tq = Q_TILE

    hl = jnp.zeros((B, NT + PAD, D), BF16)
    for layer in range(depth):
        ctx_out = layer < depth - 1
        lam_init = 0.8 - 0.6 * math.exp(-0.3 * layer)
        w = w_in[layer]
        wm = jnp.concatenate([w[:, OFF_F:OFF_DQ], w[:, OFF_DQ:OFF_MO], w[:, OFF_DK:OFF_DV]], axis=1).astype(BF16)
        wvt = jnp.concatenate([w[:, OFF_DV:OFF_MV], _pad_heads_cols(w[:, OFF_MO:OFF_MQ]),
                               _pad_heads_cols(w[:, OFF_MV:OFF_G])], axis=1).T.astype(BF16)
        wg = jnp.pad(w[:, OFF_G:], ((0, 0), (0, LANES - N_GATES)))
        wc = jnp.concatenate([_pad_heads_cols(w[:, OFF_MQ:OFF_MK]), _pad_heads_cols(w[:, OFF_MK:OFF_DK])],
                             axis=1).astype(BF16)
        gb = jnp.pad(m_gate_b[layer], (0, LANES - N_GATES)).reshape(1, LANES)
        cw = jnp.concatenate([_pad_heads_cols(m_conv_w[layer][:, :M_WIDTH]),
                              _pad_heads_cols(m_conv_w[layer][:, M_WIDTH:])], axis=1)
        cb = jnp.concatenate([_pad_heads_cols(m_conv_b[layer][:M_WIDTH]),
                              _pad_heads_cols(m_conv_b[layer][M_WIDTH:])]).reshape(1, 2 * MP_WIDTH)
        ada_l = ada[layer]

        y4, dq, dk, dvT, mo, mq, mk, mv, gl, glT = _inproj(
            xu, ada_l, norm1_g[layer].reshape(1, D), wm, wc, wvt, wg, gb, tabs["cs"], cos_t, sin_t, cw, cb,
            n_lat=n_lat, n2=n2)

        wblk = jnp.zeros((F_WIDTH, F_WIDTH), F32)
        for g in range(F_GROUPS):
            wblk = wblk.at[F_GDIM * g:F_GDIM * (g + 1), F_GDIM * g:F_GDIM * (g + 1)].set(four_w[layer, g])
        f_l, f_c = _fourier(y4, tabs, wblk.astype(BF16), n=N, ctx=CTX, with_ctx=ctx_out)

        dlam = d_lam[layer]
        g2 = d_norm_g[layer].reshape(DA_VDIM, 1)
        da_l = _attention(dq, dk, dvT, dlam, g2, lam_init=lam_init, tq=tq, q0=0, nq=N // tq, tk=NT, k0=0)

        hf, hb = _mlstm(mq, mk, mv, gl, glT, n_lat=n_lat)

        mg = _pad_heads_cols(m_norm_g[layer]).reshape(MP_WIDTH, 1)
        wol = w_out[layer]
        wo = wol[:F_WIDTH].astype(BF16)
        wod = jnp.concatenate([wol[F_WIDTH:F_WIDTH + DA_WIDTH],
                               jnp.pad(wol[F_WIDTH + DA_WIDTH:].reshape(M_HEADS, M_DIM, D),
                                       ((0, 0), (0, M_PAD - M_DIM), (0, 0))).reshape(MP_WIDTH, D)],
                              axis=0).astype(BF16)
        g2n = norm2_g[layer].reshape(1, D)
        wrp = jnp.pad(router_w[layer], ((0, 0), (0, LANES - N_EXPERTS)))
        xu, hl, pt_l = _outproj(xu, f_l, da_l, hf, hb, mo, ada_l, mg, wo, wod, g2n, wrp, hl,
                                t0=0, ntl=n_lat, is_ctx=False)
        if ctx_out:
            da_c = _attention(dq, dk, dvT, dlam, g2, lam_init=lam_init, tq=TOK, q0=n_lat, nq=1,
                              tk=TOK, k0=n_lat)
            xu, hl, pt_c = _outproj(xu, f_c, da_c, hf, hb, mo, ada_l, mg, wo, wod, g2n, wrp, hl,
                                    t0=n_lat, ntl=1, is_ctx=True)

        xu = _moe(xu, hl, pt_l, ada_l, exp_w1, exp_w3, exp_w2, layer=layer, row0=0, is_ctx=False)
        if ctx_out:
            xu = _moe(xu, hl, pt_c, ada_l, exp_w1, exp_w3, exp_w2, layer=layer, row0=N, is_ctx=True)

    return _final_norm(xu, final_g.reshape(1, D), n=N)
```

```python
import functools
import math

import numpy as np
import jax
import jax.numpy as jnp
from jax import lax
from jax.experimental import pallas as pl
from jax.experimental.pallas import tpu as pltpu

F32 = jnp.float32
BF16 = jnp.bfloat16
HI = lax.Precision.HIGHEST

D = 1024
EPS = 1e-6
GRID_W = 64
ROPE_THETA = 10000.0
F_GROUPS, F_GDIM = 4, 64
F_WIDTH = F_GROUPS * F_GDIM
DA_HEADS, DA_DIM = 6, 32
DA_VDIM = 2 * DA_DIM
DA_WIDTH = DA_HEADS * DA_VDIM
M_HEADS, M_DIM = 4, 96
M_WIDTH = M_HEADS * M_DIM
M_PAD = 128
MP_WIDTH = M_HEADS * M_PAD
N_GATES = 4 * M_HEADS
N_EXPERTS = 16
EC_CAPACITY = 2
D_FF = 2 * D
ADA_CHUNKS = 6
ADA_ROWS = 8
CTX_ROW = 4

LANES = 128
MXU_DIM = 256
V7X_VMEM_BYTES = 64 * 1024 * 1024

TOK = MXU_DIM
FFT_N1 = 16
SLOT = LANES
Q_TILE = 512
KV_TILE_MAX = 8448
MOE_TOK = 512
COMBINE_TOK = 512
FFN_TF = 512
GATHER_EXPERTS = 8
NEG = -1e30

OFF_F = 0
OFF_DQ = OFF_F + F_WIDTH
OFF_MO = OFF_DQ + 2 * DA_HEADS * DA_DIM
OFF_MQ = OFF_MO + M_WIDTH
OFF_MK = OFF_MQ + M_WIDTH
OFF_DK = OFF_MK + M_WIDTH
OFF_DV = OFF_DK + 2 * DA_HEADS * DA_DIM
OFF_MV = OFF_DV + DA_HEADS * DA_VDIM
OFF_G = OFF_MV + M_WIDTH

VMEM_LIMIT = V7X_VMEM_BYTES * 7 // 8


def _cp(sem, vmem=None):
    return pltpu.CompilerParams(dimension_semantics=sem, vmem_limit_bytes=vmem)


def _sigmoid(x):
    return 1.0 / (1.0 + jnp.exp(-x))


def _silu(x):
    return x * _sigmoid(x)


def _dot(a, b, precision=None):
    return jnp.dot(a, b, preferred_element_type=F32, precision=precision)


def _split(a):
    hi = a.astype(BF16)
    return hi, (a - hi.astype(F32)).astype(BF16)


def _dot3(a, b):
    a_hi, a_lo = a if isinstance(a, tuple) else _split(a)
    b_hi, b_lo = b if isinstance(b, tuple) else _split(b)
    return _dot(a_hi, b_hi) + _dot(a_hi, b_lo) + _dot(a_lo, b_hi)


def _split3(a):
    hi = a.astype(BF16)
    r = a - hi.astype(F32)
    mid = r.astype(BF16)
    return hi, mid, (r - mid.astype(F32)).astype(BF16)


def _dot_nt(a, b, precision=None):
    return lax.dot_general(a, b, (((1,), (1,)), ((), ())), preferred_element_type=F32,
                           precision=precision)


def _ada_kernel(c_ref, w_ref, b_ref, o_ref):
    c = c_ref[...]
    o_ref[0] = _dot(_silu(c), w_ref[0], HI) + b_ref[0]


def _adaln(cvecs, ada_w, ada_b):
    depth = ada_w.shape[0]
    tn = 1536
    return pl.pallas_call(
        _ada_kernel,
        out_shape=jax.ShapeDtypeStruct((depth, ADA_ROWS, ADA_CHUNKS * D), F32),
        grid=(depth, ADA_CHUNKS * D // tn),
        in_specs=[pl.BlockSpec((ADA_ROWS, D), lambda l, j: (0, 0)),
                  pl.BlockSpec((1, D, tn), lambda l, j: (l, 0, j)),
                  pl.BlockSpec((1, 1, tn), lambda l, j: (l, 0, j))],
        out_specs=pl.BlockSpec((1, 8, tn), lambda l, j: (l, 0, j)),
        compiler_params=_cp(("arbitrary", "arbitrary")),
        name="adaln",
    )(cvecs, ada_w, ada_b.reshape(depth, 1, ADA_CHUNKS * D))


def _inproj_kernel(x_ref, xp_ref, xn_ref, ada_ref, g_ref, wm_ref, wc_ref, wvt_ref, wg_ref, gb_ref, cs_ref,
                   cos_ref, sin_ref, cw_ref, cb_ref,
                   y_ref, dq_ref, dk_ref, dvt_ref, mo_ref, mq_ref, mk_ref, mv_ref, gl_ref, glt_ref, *, n_lat):
    b = pl.program_id(0)
    t = pl.program_id(1)
    n_tiles = pl.num_programs(1)
    is_ctx = t >= n_lat
    row = jnp.where(is_ctx, CTX_ROW, b)
    mod = ada_ref[row]
    sh, sc = mod[0:1], mod[1:2]

    xa = jnp.concatenate([xp_ref[0], x_ref[0], xn_ref[0]], axis=0)
    r = lax.rsqrt(jnp.mean(xa * xa, axis=-1, keepdims=True) + EPS)
    ha = (xa * r) * g_ref[...] * (1.0 + sc) + sh
    h = ha[8:8 + TOK]
    hb = h.astype(BF16)

    pm = _dot(hb, wm_ref[...])
    o = 0
    pf = pm[:, o:o + F_WIDTH]; o += F_WIDTH
    q = pm[:, o:o + DA_WIDTH]; o += DA_WIDTH
    k = pm[:, o:o + DA_WIDTH]; o += DA_WIDTH
    pt = _dot_nt(wvt_ref[...], hb)
    dvt_ref[0] = pt[:DA_WIDTH].astype(BF16)
    mo_ref[0] = pt[DA_WIDTH:DA_WIDTH + MP_WIDTH].astype(BF16)
    mv_ref[0] = pt[DA_WIDTH + MP_WIDTH:].astype(BF16)

    y_ref[0, 0] = _dot3(pf, cs_ref[...])

    cos = cos_ref[...]
    sin = sin_ref[...]
    lane = lax.broadcasted_iota(jnp.int32, (1, 128), 1)
    low = (lane % 16) < 8

    def rope(z):
        parts = []
        for c in range(DA_WIDTH // 128):
            zc = z[:, 128 * c:128 * (c + 1)]
            rot = jnp.where(low, pltpu.roll(zc, 120, 1), pltpu.roll(zc, 8, 1))
            parts.append(zc * cos + rot * sin)
        return jnp.concatenate(parts, axis=1)

    dq_ref[0] = (rope(q) * (DA_DIM ** -0.5 * math.log2(math.e))).astype(BF16)
    dk_ref[0] = rope(k).astype(BF16)

    gpre = _dot3(h, wg_ref[...]) + gb_ref[...]
    is_forget = (lax.broadcasted_iota(jnp.int32, (1, LANES), 1) % 8) >= 4
    logsig = jnp.minimum(gpre, 0.0) - jnp.log(1.0 + jnp.exp(-jnp.abs(gpre)))
    gl = jnp.where(is_forget, logsig, gpre)
    gl_ref[0] = gl
    glt_ref[0] = gl.T[:N_GATES]

    pc = _dot(ha.astype(BF16), wc_ref[...])
    first = (t == 0) | (t == n_lat)
    last = (t == n_lat - 1) | (t == n_tiles - 1)
    ridx = lax.broadcasted_iota(jnp.int32, (TOK + 16, 1), 0)
    pc = jnp.where(((ridx < 8) & first) | ((ridx >= TOK + 8) & last), 0.0, pc)
    cw = cw_ref[...]
    conv = cb_ref[...] + pc[7:7 + TOK] * cw[0:1] + pc[8:8 + TOK] * cw[1:2] + pc[9:9 + TOK] * cw[2:3]
    act = _silu(conv)
    mq_ref[0] = act[:, :MP_WIDTH].astype(BF16)
    mk_ref[0] = (act[:, MP_WIDTH:] * (M_DIM ** -0.5)).astype(BF16)


def _inproj(xu, ada_l, g1, wm, wc, wvt, wg, gb, cs, cos_t, sin_t, cw, cb, *, n_lat, n2):
    B, NT, _ = xu.shape
    nt = n_lat + 1
    rper = n2 // TOK
    tok3 = lambda w: pl.BlockSpec((1, TOK, w), lambda b, t: (b, t, 0))
    full = lambda a: pl.BlockSpec(a.shape, lambda b, t: (0,) * a.ndim)
    nb8 = NT // 8
    outs = [jax.ShapeDtypeStruct((B, 2 * FFT_N1, n2, 2 * F_WIDTH), F32)]
    nq = -(-NT // Q_TILE) * Q_TILE
    outs += [jax.ShapeDtypeStruct((B, nq, DA_WIDTH), BF16), jax.ShapeDtypeStruct((B, nt * TOK, DA_WIDTH), BF16)]
    outs += [jax.ShapeDtypeStruct((B, DA_WIDTH, nt * TOK), BF16)]
    trs = lambda r: pl.BlockSpec((1, r, TOK), lambda b, t: (b, 0, t))
    trp = jax.ShapeDtypeStruct((B, MP_WIDTH, NT), BF16)
    outs += [trp, jax.ShapeDtypeStruct((B, NT, MP_WIDTH), BF16), jax.ShapeDtypeStruct((B, NT, MP_WIDTH), BF16), trp]
    outs += [jax.ShapeDtypeStruct((B, NT, 128), F32), jax.ShapeDtypeStruct((B, N_GATES, nt * TOK), F32)]
    out_specs = [pl.BlockSpec((1, 1, TOK, 2 * F_WIDTH), lambda b, t: (b, t // rper, t % rper, 0))]
    out_specs += [tok3(DA_WIDTH)] * 2 + [trs(DA_WIDTH)]
    out_specs += [trs(MP_WIDTH), tok3(MP_WIDTH), tok3(MP_WIDTH), trs(MP_WIDTH)]
    out_specs += [tok3(128), pl.BlockSpec((1, N_GATES, TOK), lambda b, t: (b, 0, t))]
    return pl.pallas_call(
        functools.partial(_inproj_kernel, n_lat=n_lat),
        out_shape=outs,
        grid=(B, nt),
        in_specs=[tok3(D),
                  pl.BlockSpec((1, 8, D), lambda b, t: (b, jnp.maximum(t * (TOK // 8) - 1, 0), 0)),
                  pl.BlockSpec((1, 8, D), lambda b, t: (b, jnp.minimum((t + 1) * (TOK // 8), nb8 - 1), 0)),
                  full(ada_l), full(g1), full(wm), full(wc), full(wvt), full(wg), full(gb), full(cs),
                  pl.BlockSpec((TOK, LANES), lambda b, t: (t, 0)),
                  pl.BlockSpec((TOK, LANES), lambda b, t: (t, 0)),
                  full(cw), full(cb)],
        out_specs=out_specs,
        compiler_params=_cp(("arbitrary", "arbitrary"), VMEM_LIMIT),
        name="norm1_inproj",
    )(xu, xu, xu, ada_l, g1, wm, wc, wvt, wg, gb, cs, cos_t, sin_t, cw, cb)


def _fft1_kernel(y_ref, kc_ref, ks_ref, tc_ref, ts_ref, o_ref, *, groups):
    kc = _split(kc_ref[...])
    ks = _split(ks_ref[...])
    for g in range(groups):
        blk = _split(y_ref[0, :, 8 * g:8 * (g + 1), :].reshape(FFT_N1 * 8, 2 * F_WIDTH))
        p = _dot3(kc, blk)
        q = _dot3(ks, blk)
        ar = p[:, :F_WIDTH] - q[:, F_WIDTH:]
        ai = -p[:, F_WIDTH:] - q[:, :F_WIDTH]
        tc = tc_ref[128 * g:128 * (g + 1), :]
        ts = ts_ref[128 * g:128 * (g + 1), :]
        tc = jnp.concatenate([tc, tc], axis=1)
        ts = jnp.concatenate([ts, ts], axis=1)
        br = ar * tc + ai * ts
        bi = ai * tc - ar * ts
        o_ref[0, :, 8 * g:8 * (g + 1), :] = jnp.concatenate([br, bi], axis=1).reshape(FFT_N1, 8, 2 * F_WIDTH)


def _fft2_kernel(b_ref, c2_ref, s2_ref, wb_ref, perm_ref, o_ref, r_scr, *, n2):
    c2 = _split(c2_ref[...])
    s2 = _split(s2_ref[...])
    for i in range(8):
        blk = b_ref[0, i]
        xr = _dot3(c2, blk[:, :F_WIDTH]) + _dot3(s2, blk[:, F_WIDTH:])
        r_scr[i] = _dot(xr.astype(BF16), wb_ref[...]).astype(BF16)
    for t in range(n2 // 32):
        rows = jnp.concatenate([r_scr[i, 32 * t:32 * (t + 1), :] for i in range(8)], axis=0)
        o_ref[0, 32 * t:32 * (t + 1), :, :] = _dot(perm_ref[...], rows).reshape(32, 8, F_WIDTH)


def _fftc_kernel(y_ref, c_ref, s_ref, wb_ref, o_ref):
    y = y_ref[0, 0]
    z = _dot3(c_ref[...], y[:, :F_WIDTH]) - _dot3(s_ref[...], y[:, F_WIDTH:])
    o_ref[0] = _dot(z.astype(BF16), wb_ref[...])


def _fourier_tables(n, ctx):
    n1, n2 = FFT_N1, n // FFT_N1
    a = np.arange(n1)
    ang1 = 2 * np.pi * np.outer(a, a) / n1
    eye8 = np.eye(8)
    kc = np.kron(np.cos(ang1), eye8)
    ks = np.kron(np.sin(ang1), eye8)
    n2i = np.arange(n2).reshape(n2 // 8, 1, 8)
    k1 = np.arange(n1).reshape(1, n1, 1)
    angt = (2 * np.pi * n2i * k1 / n).reshape(-1, 1)
    tc = np.broadcast_to(np.cos(angt), (n2 // 8 * 128, 128))
    ts = np.broadcast_to(np.sin(angt), (n2 // 8 * 128, 128))
    b = np.arange(n2)
    ang2 = 2 * np.pi * np.outer(b, b) / n2
    c2 = np.cos(ang2) / math.sqrt(n)
    s2 = np.sin(ang2) / math.sqrt(n)
    perm = np.zeros((256, 256))
    for kk in range(8):
        for j in range(32):
            perm[j * 8 + kk, kk * 32 + j] = 1.0
    cc = np.arange(ctx)
    angc = 2 * np.pi * np.outer(cc, cc) / ctx
    cctx = np.cos(angc) / math.sqrt(ctx)
    sctx = np.sin(angc) / math.sqrt(ctx)
    ch = np.arange(F_GDIM)
    angch = 2 * np.pi * np.outer(ch, ch) / F_GDIM
    cs = np.concatenate([np.kron(np.eye(F_GROUPS), np.cos(angch)),
                         np.kron(np.eye(F_GROUPS), np.sin(angch))], axis=1) / math.sqrt(F_GDIM)
    f = lambda z: jnp.asarray(np.ascontiguousarray(z), dtype=F32)
    return dict(kc=f(kc), ks=f(ks), tc=f(tc), ts=f(ts), c2=f(c2), s2=f(s2), perm=f(perm).astype(BF16),
                cctx=f(cctx), sctx=f(sctx), cs=f(cs))


def _fourier(y4, tabs, wblk, *, n, ctx, with_ctx):
    B = y4.shape[0]
    n2 = n // FFT_N1
    groups = 4
    full = lambda a, nd: pl.BlockSpec(a.shape, lambda *i: (0,) * a.ndim)
    b4 = pl.pallas_call(
        functools.partial(_fft1_kernel, groups=groups),
        out_shape=jax.ShapeDtypeStruct((B, FFT_N1, n2, 2 * F_WIDTH), F32),
        grid=(B, n2 // (8 * groups)),
        in_specs=[pl.BlockSpec((1, FFT_N1, 8 * groups, 2 * F_WIDTH), lambda b, j: (b, 0, j, 0)),
                  full(tabs["kc"], 2), full(tabs["ks"], 2),
                  pl.BlockSpec((128 * groups, 128), lambda b, j: (j, 0)),
                  pl.BlockSpec((128 * groups, 128), lambda b, j: (j, 0))],
        out_specs=pl.BlockSpec((1, FFT_N1, 8 * groups, 2 * F_WIDTH), lambda b, j: (b, 0, j, 0)),
        compiler_params=_cp(("arbitrary", "arbitrary")),
        name="fourier_stage1",
    )(y4, tabs["kc"], tabs["ks"], tabs["tc"], tabs["ts"])
    f4 = pl.pallas_call(
        functools.partial(_fft2_kernel, n2=n2),
        out_shape=jax.ShapeDtypeStruct((B, n2, 16, F_WIDTH), F32),
        grid=(B, FFT_N1 // 8),
        in_specs=[pl.BlockSpec((1, 8, n2, 2 * F_WIDTH), lambda b, j: (b, j, 0, 0)),
                  full(tabs["c2"], 2), full(tabs["s2"], 2), full(wblk, 2), full(tabs["perm"], 2)],
        out_specs=pl.BlockSpec((1, n2, 8, F_WIDTH), lambda b, j: (b, 0, j, 0)),
        scratch_shapes=[pltpu.VMEM((8, n2, F_WIDTH), BF16)],
        compiler_params=_cp(("arbitrary", "arbitrary"), VMEM_LIMIT),
        name="fourier_stage2",
    )(b4, tabs["c2"], tabs["s2"], wblk, tabs["perm"])
    f_ctx = None
    if with_ctx:
        f_ctx = pl.pallas_call(
            _fftc_kernel,
            out_shape=jax.ShapeDtypeStruct((B, ctx, F_WIDTH), F32),
            grid=(B,),
            in_specs=[pl.BlockSpec((1, 1, TOK, 2 * F_WIDTH), lambda b: (b, FFT_N1, 0, 0)),
                      full(tabs["cctx"], 1), full(tabs["sctx"], 1), full(wblk, 1)],
            out_specs=pl.BlockSpec((1, ctx, F_WIDTH), lambda b: (b, 0, 0)),
            compiler_params=_cp(("arbitrary",)),
            name="fourier_ctx",
        )(y4, tabs["cctx"], tabs["sctx"], wblk)
    return f4.reshape(B, n, F_WIDTH), f_ctx


VROWS = DA_VDIM + 16


def _attn_kernel(q_ref, k_ref, vt_ref, dl_ref, g_ref, o_ref, m_scr, acc_scr, *, lam_init):
    kt = pl.program_id(3)
    nk = pl.num_programs(3)

    @pl.when(kt == 0)
    def _():
        m_scr[...] = jnp.full(m_scr.shape, NEG, F32)
        acc_scr[...] = jnp.zeros(acc_scr.shape, F32)

    q = q_ref[0]
    k = k_ref[0]
    vt = vt_ref[0]
    ones = jnp.ones((16, vt.shape[1]), BF16)
    lhs = [jnp.concatenate([vt[DA_VDIM * h:DA_VDIM * (h + 1)], ones], axis=0) for h in range(2)]
    lane = lax.broadcasted_iota(jnp.int32, (1, LANES), 1)
    zero = jnp.zeros((), BF16)

    def scores(j):
        return _dot_nt(k, jnp.where((lane // DA_DIM) == j, q, zero))

    st_next = scores(0)
    for j in range(4):
        st = st_next
        if j < 3:
            st_next = scores(j + 1)
        m_old = m_scr[j]
        m_new = jnp.maximum(m_old, jnp.max(st, axis=0, keepdims=True))
        alpha = jnp.exp2(m_old - m_new)
        pt = jnp.exp2(st - m_new).astype(BF16)
        acc_scr[j] = alpha * acc_scr[j] + _dot(lhs[j // 2], pt)
        m_scr[j] = m_new

    @pl.when(kt == nk - 1)
    def _():
        dl = dl_ref[...]
        lam = (jnp.exp(jnp.sum(dl[0:1] * dl[1:2], keepdims=True))
               - jnp.exp(jnp.sum(dl[2:3] * dl[3:4], keepdims=True)) + lam_init)
        outs = []
        for h in range(2):
            a0 = acc_scr[2 * h]
            a1 = acc_scr[2 * h + 1]
            o = (a0[:DA_VDIM] / a0[DA_VDIM:DA_VDIM + 1]
                 - lam * (a1[:DA_VDIM] / a1[DA_VDIM:DA_VDIM + 1]))
            r = lax.rsqrt(jnp.mean(o * o, axis=0, keepdims=True) + EPS)
            outs.append(((o * r) * g_ref[...]) * (1.0 - lam_init))
        o_ref[0] = jnp.concatenate(outs, axis=0).astype(BF16)


def _attention(dq, dk, dvT, dlam, gcol, *, lam_init, tq, q0, nq, tk, k0, nk):
    B = dq.shape[0]
    return pl.pallas_call(
        functools.partial(_attn_kernel, lam_init=lam_init),
        out_shape=jax.ShapeDtypeStruct((B, DA_WIDTH, nq * tq), BF16),
        grid=(B, DA_WIDTH // 128, nq, nk),
        in_specs=[pl.BlockSpec((1, tq, 128), lambda b, p, i, j: (b, q0 + i, p)),
                  pl.BlockSpec((1, tk, 128), lambda b, p, i, j: (b, k0 + j, p)),
                  pl.BlockSpec((1, 128, tk), lambda b, p, i, j: (b, p, k0 + j)),
                  pl.BlockSpec(dlam.shape, lambda b, p, i, j: (0, 0)),
                  pl.BlockSpec(gcol.shape, lambda b, p, i, j: (0, 0))],
        out_specs=pl.BlockSpec((1, 128, tq), lambda b, p, i, j: (b, p, i)),
        scratch_shapes=[pltpu.VMEM((4, 1, tq), F32), pltpu.VMEM((4, VROWS, tq), F32)],
        compiler_params=_cp(("arbitrary",) * 4, VMEM_LIMIT),
        name="diff_attention",
    )(dq, dk, dvT, dlam, gcol)


def _mlstm_kernel(qf_ref, kf_ref, vf_ref, gcf_ref, grf_ref, qb_ref, kb_ref, vb_ref, gcb_ref, grb_ref,
                  hf_ref, hb_ref, c_scr, m_scr):
    t = pl.program_id(1)

    @pl.when(t == 0)
    def _():
        c_scr[...] = jnp.zeros(c_scr.shape, F32)
        m_scr[...] = jnp.zeros(m_scr.shape, F32)

    L = TOK
    si = lax.broadcasted_iota(jnp.int32, (L, L), 0)
    li = lax.broadcasted_iota(jnp.int32, (L, L), 1)
    dirs = ((qf_ref, kf_ref, vf_ref, gcf_ref, grf_ref, hf_ref, si <= li, li <= si, L - 1),
            (qb_ref, kb_ref, vb_ref, gcb_ref, grb_ref, hb_ref, si >= li, li >= si, 0))
    ones = jnp.ones((16, L), F32)
    for d, (q_ref, k_ref, vt_ref, gc_ref, gr_ref, h_ref, seen, seen_t, last) in enumerate(dirs):
        gc = gc_ref[0]
        gr = gr_ref[0]
        seen_b = jnp.where(seen, 1.0, 0.0).astype(BF16)
        seen_tb = jnp.where(seen_t, 1.0, 0.0).astype(BF16)
        bcols = sum(_dot(seen_tb, piece) for piece in _split3(gc))
        brows = sum(_dot(piece, seen_b) for piece in _split3(gr))
        for hd in range(M_HEADS):
            idx = d * M_HEADS + hd
            ji = d * 8 + hd
            jf = d * 8 + 4 + hd
            sl = slice(M_PAD * hd, M_PAD * (hd + 1))
            q = q_ref[0, :, sl]
            k = k_ref[0, :, sl]
            vt = vt_ref[0, sl, :]
            b_row = brows[jf:jf + 1, :]
            cs = gc[:, ji:ji + 1] - bcols[:, jf:jf + 1]
            li_row = gr[ji:ji + 1, :]
            m_old = m_scr[idx][0:1, 0:1]
            c_old = c_scr[idx]

            dlog = jnp.where(seen, b_row + cs, NEG)
            inter = b_row + m_old
            m_t = jnp.maximum(inter, jnp.max(dlog, axis=0, keepdims=True))
            w_inter = jnp.exp(inter - m_t)
            st = _dot_nt(k, q) * jnp.exp(dlog - m_t)
            cq = _dot_nt(c_old.astype(BF16), q)
            num = w_inter * cq[:M_PAD] + _dot(vt, st.astype(BF16))
            den = w_inter * cq[M_PAD:M_PAD + 1] + jnp.sum(st, axis=0, keepdims=True)
            h_ref[0, sl, :] = num / jnp.maximum(jnp.abs(den), jnp.exp(-m_t))

            total = b_row[:, last:last + 1]
            wlog = total - b_row + li_row
            m_new = jnp.maximum(total + m_old, jnp.max(wlog, axis=1, keepdims=True))
            decay = jnp.exp(total + m_old - m_new)
            w = jnp.exp(wlog - m_new)
            vw = jnp.concatenate([vt.astype(F32) * w, ones * w], axis=0).astype(BF16)
            c_scr[idx] = decay * c_old + _dot(vw, k)
            m_scr[idx] = jnp.broadcast_to(m_new, (8, 128))


def _mlstm(mq, mk, mvT, gl, glT, *, n_lat):
    B, NT, _ = mq.shape
    nt = n_lat + 1
    fwd = lambda t: jnp.where(t == 0, n_lat, t - 1)
    bwd = lambda t: jnp.where(t == 0, n_lat, n_lat - t)
    tok = lambda w, f: pl.BlockSpec((1, TOK, w), lambda b, t: (b, f(t), 0))
    lanes = lambda r, f: pl.BlockSpec((1, r, TOK), lambda b, t: (b, 0, f(t)))
    ins, specs = [], []
    for f in (fwd, bwd):
        ins += [mq, mk, mvT, gl, glT]
        specs += [tok(MP_WIDTH, f)] * 2 + [lanes(MP_WIDTH, f), tok(128, f), lanes(N_GATES, f)]
    return pl.pallas_call(
        _mlstm_kernel,
        out_shape=[jax.ShapeDtypeStruct((B, MP_WIDTH, NT), F32)] * 2,
        grid=(B, nt),
        in_specs=specs,
        out_specs=[lanes(MP_WIDTH, fwd), lanes(MP_WIDTH, bwd)],
        scratch_shapes=[pltpu.VMEM((2 * M_HEADS, M_PAD + 16, M_PAD), F32),
                        pltpu.VMEM((2 * M_HEADS, 8, 128), F32)],
        compiler_params=_cp(("arbitrary", "arbitrary"), VMEM_LIMIT),
        name="mlstm",
    )(*ins)


def _outproj_kernel(x_ref, f_ref, dat_ref, hf_ref, hb_ref, mo_ref, ada_ref, mg_ref, wo_ref, wod_ref, g2_ref, wr_ref,
                    xo_ref, hl_ref, pt_ref, *, is_ctx):
    b = pl.program_id(0)
    mod = ada_ref[CTX_ROW if is_ctx else b]
    gt1, sh2, sc2 = mod[2:3], mod[3:4], mod[4:5]
    mg = mg_ref[...]
    for s in range(x_ref.shape[1] // TOK):
        tk = slice(TOK * s, TOK * (s + 1))
        hs = hf_ref[0, :, tk] + hb_ref[0, :, tk]
        og = mo_ref[0, :, tk].astype(F32)
        parts = [dat_ref[0, :, tk]]
        for hd in range(M_HEADS):
            sl = slice(M_PAD * hd, M_PAD * (hd + 1))
            hh = hs[sl]
            r = lax.rsqrt(jnp.sum(hh * hh, axis=0, keepdims=True) * (1.0 / M_DIM) + EPS)
            parts.append((((hh * r) * mg[sl]) * _sigmoid(og[sl])).astype(BF16))
        mix_t = jnp.concatenate(parts, axis=0)
        upd = _dot(f_ref[0, tk].astype(BF16), wo_ref[...]) + lax.dot_general(
            mix_t, wod_ref[...], (((0,), (0,)), ((), ())), preferred_element_type=F32)
        xn = x_ref[0, tk] + gt1 * upd
        xo_ref[0, tk] = xn
        r = lax.rsqrt(jnp.mean(xn * xn, axis=-1, keepdims=True) + EPS)
        h2 = (xn * r) * g2_ref[...] * (1.0 + sc2) + sh2
        hl_ref[0, tk] = h2.astype(BF16)
        lt = _dot3(h2, wr_ref[...]).T[:N_EXPERTS]
        ex = jnp.exp(lt - jnp.max(lt, axis=0, keepdims=True))
        pt_ref[0, :, tk] = ex / jnp.sum(ex, axis=0, keepdims=True)


def _outproj_kernel_aliased(x_ref, f_ref, dat_ref, hf_ref, hb_ref, mo_ref, ada_ref, mg_ref, wo_ref, wod_ref,
                            g2_ref, wr_ref, hlp_ref, xo_ref, hl_ref, pt_ref, *, is_ctx):
    del hlp_ref
    _outproj_kernel(x_ref, f_ref, dat_ref, hf_ref, hb_ref, mo_ref, ada_ref, mg_ref, wo_ref, wod_ref, g2_ref,
                    wr_ref, xo_ref, hl_ref, pt_ref, is_ctx=is_ctx)


def _outproj(xu, f, daT, hf, hb, mo, ada_l, mg, wo, wod, g2, wrp, hl_prev, *, t0, ntl, is_ctx):
    B, NT, _ = xu.shape
    n = ntl * TOK
    tile = 2 * TOK if n % (2 * TOK) == 0 else TOK
    o = t0 * TOK // tile
    tok = lambda w: pl.BlockSpec((1, tile, w), lambda b, t: (b, o + t, 0))
    trs = lambda r: pl.BlockSpec((1, r, tile), lambda b, t: (b, 0, o + t))
    loc = lambda w: pl.BlockSpec((1, tile, w), lambda b, t: (b, t, 0))
    full = lambda a: pl.BlockSpec(a.shape, lambda b, t: (0,) * a.ndim)
    return pl.pallas_call(
        functools.partial(_outproj_kernel_aliased, is_ctx=is_ctx),
        out_shape=[jax.ShapeDtypeStruct(xu.shape, F32), jax.ShapeDtypeStruct((B, NT, D), BF16),
                   jax.ShapeDtypeStruct((B, N_EXPERTS, n), F32)],
        grid=(B, n // tile),
        in_specs=[tok(D), loc(F_WIDTH), pl.BlockSpec((1, DA_WIDTH, tile), lambda b, t: (b, 0, t)),
                  trs(MP_WIDTH), trs(MP_WIDTH), trs(MP_WIDTH),
                  full(ada_l), full(mg), full(wo), full(wod), full(g2), full(wrp),
                  pl.BlockSpec(memory_space=pl.ANY)],
        out_specs=[tok(D), tok(D), pl.BlockSpec((1, N_EXPERTS, tile), lambda b, t: (b, 0, t))],
        input_output_aliases={0: 0, 12: 1},
        compiler_params=_cp(("arbitrary", "arbitrary"), VMEM_LIMIT),
        name="outproj_norm2_router",
    )(xu, f, daT, hf, hb, mo, ada_l, mg, wo, wod, g2, wrp, hl_prev)


def _select_kernel(p_ref, rank_ref, offs_ref, *, n, cap):
    p = p_ref[0]
    xi = pltpu.bitcast(p, jnp.int32)

    def body(i, lo):
        cand = lo | jnp.left_shift(jnp.int32(1), 30 - i)
        cnt = jnp.sum(jnp.where(xi >= cand, 1.0, 0.0), axis=1, keepdims=True)
        return jnp.where(cnt >= cap, cand, lo)

    thr = lax.fori_loop(0, 31, body, jnp.zeros((N_EXPERTS, 1), jnp.int32))
    nb = n // TOK
    rows = lax.broadcasted_iota(jnp.int32, (n, 128), 0)
    cols = lax.broadcasted_iota(jnp.int32, (n, 128), 1)
    blk_ind = jnp.where((rows // TOK) == cols, 1.0, 0.0).astype(BF16)
    u128 = jnp.where(lax.broadcasted_iota(jnp.int32, (128, 128), 0)
                     < lax.broadcasted_iota(jnp.int32, (128, 128), 1), 1.0, 0.0).astype(BF16)
    utok = jnp.where(lax.broadcasted_iota(jnp.int32, (TOK, TOK), 0)
                     < lax.broadcasted_iota(jnp.int32, (TOK, TOK), 1), 1.0, 0.0).astype(BF16)

    def prefix(mf):
        mb = mf.astype(BF16)
        counts = _dot(mb, blk_ind)
        offs = _dot(counts.astype(BF16), u128)
        pieces = [_dot(mb[:, TOK * j:TOK * (j + 1)], utok) + offs[:, j:j + 1] for j in range(nb)]
        return (jnp.concatenate(pieces, axis=1) if nb > 1 else pieces[0]), offs

    gt = xi > thr
    eq = xi == thr
    need = cap - jnp.sum(jnp.where(gt, 1.0, 0.0), axis=1, keepdims=True)
    rank_eq, _ = prefix(jnp.where(eq, 1.0, 0.0))
    sel = gt | (eq & (rank_eq < need))
    rank, offs = prefix(jnp.where(sel, 1.0, 0.0))
    rank_ref[0] = jnp.where(sel, rank, -1.0)
    offs_ref[0] = offs.astype(jnp.int32)


def _select(pt, *, cap):
    B, _, n = pt.shape
    return pl.pallas_call(
        functools.partial(_select_kernel, n=n, cap=cap),
        out_shape=[jax.ShapeDtypeStruct((B, N_EXPERTS, n), F32),
                   jax.ShapeDtypeStruct((B, N_EXPERTS, 128), jnp.int32)],
        grid=(B,),
        in_specs=[pl.BlockSpec((1, N_EXPERTS, n), lambda b: (b, 0, 0))],
        out_specs=[pl.BlockSpec((1, N_EXPERTS, n), lambda b: (b, 0, 0)),
                   pl.BlockSpec((1, N_EXPERTS, 128), lambda b: (b, 0, 0))],
        compiler_params=_cp(("arbitrary",), VMEM_LIMIT),
        name="expert_choice_select",
    )(pt)


def _gather_kernel(offs_ref, h_ref, rank_ref, prob_ref, o_ref, gate_ref, *, eg, per):
    b, g, tb = pl.program_id(0), pl.program_id(1), pl.program_id(2)

    @pl.when(tb == 0)
    def _():
        o_ref[...] = jnp.zeros(o_ref.shape, BF16)
        gate_ref[...] = jnp.zeros(gate_ref.shape, F32)

    cap_pad = o_ref.shape[2]
    half = SLOT // 2
    slot = lax.broadcasted_iota(jnp.int32, (SLOT, TOK), 0).astype(F32)

    def add_rows(i, r, p, h, base, start=None):
        hit = r == slot + base.astype(F32)
        if start is not None:
            hit = hit & (r >= start.astype(F32))
        rows = _dot(jnp.where(hit, 1.0, 0.0).astype(BF16), h).astype(BF16)
        o_ref[0, i, pl.ds(base, SLOT), :] = o_ref[0, i, pl.ds(base, SLOT), :] + rows
        gate_ref[0, i, pl.ds(base, SLOT), :] = (gate_ref[0, i, pl.ds(base, SLOT), :]
                                                + jnp.sum(jnp.where(hit, p, 0.0), axis=1, keepdims=True))

    def operands(s, i):
        tk = slice(TOK * s, TOK * (s + 1))
        e = g * eg + i
        return rank_ref[0, pl.ds(e, 1), tk], prob_ref[0, pl.ds(e, 1), tk], h_ref[0, tk, :]

    ends, his = {}, {}
    for s in range(per):
        for i in range(eg):
            e = g * eg + i
            lo = offs_ref[b, e, tb * per + s]
            his[s, i] = offs_ref[b, e, tb * per + s + 1]
            base = pl.multiple_of(jnp.minimum((lo // half) * half, cap_pad - SLOT), half)
            add_rows(i, *operands(s, i), base)
            ends[s, i] = base + SLOT

    for s in range(per):
        for i in range(eg):
            @pl.when(his[s, i] > ends[s, i])
            def _(s=s, i=i):
                def body(t, carry):
                    start = ends[s, i] + t * SLOT
                    base = pl.multiple_of(jnp.minimum(start, cap_pad - SLOT), half)
                    add_rows(i, *operands(s, i), base, start)
                    return carry

                lax.fori_loop(0, (his[s, i] - ends[s, i] + SLOT - 1) // SLOT, body, 0)


def _gather(offs, hl, rank, pt, *, tb_tok, tb0, n, cap_pad, eg):
    B = hl.shape[0]
    per = tb_tok // TOK
    return pl.pallas_call(
        functools.partial(_gather_kernel, eg=eg, per=per),
        out_shape=[jax.ShapeDtypeStruct((B, N_EXPERTS, cap_pad, D), BF16),
                   jax.ShapeDtypeStruct((B, N_EXPERTS, cap_pad, 1), F32)],
        grid_spec=pltpu.PrefetchScalarGridSpec(
            num_scalar_prefetch=1,
            grid=(B, N_EXPERTS // eg, n // tb_tok),
            in_specs=[pl.BlockSpec((1, tb_tok, D), lambda b, g, t, o: (b, tb0 + t, 0)),
                      pl.BlockSpec((1, N_EXPERTS, tb_tok), lambda b, g, t, o: (b, 0, t)),
                      pl.BlockSpec((1, N_EXPERTS, tb_tok), lambda b, g, t, o: (b, 0, t))],
            out_specs=[pl.BlockSpec((1, eg, cap_pad, D), lambda b, g, t, o: (b, g, 0, 0)),
                       pl.BlockSpec((1, eg, cap_pad, 1), lambda b, g, t, o: (b, g, 0, 0))]),
        compiler_params=_cp(("arbitrary",) * 3, VMEM_LIMIT),
        name="expert_gather",
    )(offs, hl, rank, pt)


FFN_ROWS = 1024


def _ffn_kernel(x_ref, gate_ref, w1_ref, w3_ref, w2_ref, y_ref, acc_ref):
    f = pl.program_id(2)

    @pl.when(f == 0)
    def _():
        acc_ref[...] = jnp.zeros(acc_ref.shape, F32)

    w1 = w1_ref[0, 0].astype(BF16)
    w3 = w3_ref[0, 0].astype(BF16)
    w2 = w2_ref[0, 0].astype(BF16)
    mb, _, cap_pad, _ = x_ref.shape
    rows = min(FFN_ROWS, cap_pad)
    for i in range(mb):
        for r in range(0, cap_pad, rows):
            x = x_ref[i, 0, r:r + rows, :]
            hid = (_silu(_dot(x, w1)) * _dot(x, w3)).astype(BF16)
            acc_ref[i * cap_pad + r:i * cap_pad + r + rows, :] += _dot(hid, w2)

    @pl.when(f == pl.num_programs(2) - 1)
    def _():
        gate = gate_ref[...].reshape(-1, 1)
        y_ref[...] = (acc_ref[...] * gate).astype(BF16).reshape(y_ref.shape)


def _ffn(xs, gates, w1, w3, w2, *, layer, mb, tf):
    B, E, cap_pad, _ = xs.shape
    return pl.pallas_call(
        _ffn_kernel,
        out_shape=jax.ShapeDtypeStruct(xs.shape, BF16),
        grid=(E, B // mb, D_FF // tf),
        in_specs=[pl.BlockSpec((mb, 1, cap_pad, D), lambda e, m, f: (m, e, 0, 0), pipeline_mode=pl.Buffered(1)),
                  pl.BlockSpec((mb, 1, cap_pad, 1), lambda e, m, f: (m, e, 0, 0), pipeline_mode=pl.Buffered(1)),
                  pl.BlockSpec((1, 1, D, tf), lambda e, m, f: (layer, e, 0, f)),
                  pl.BlockSpec((1, 1, D, tf), lambda e, m, f: (layer, e, 0, f)),
                  pl.BlockSpec((1, 1, tf, D), lambda e, m, f: (layer, e, f, 0))],
        out_specs=pl.BlockSpec((mb, 1, cap_pad, D), lambda e, m, f: (m, e, 0, 0), pipeline_mode=pl.Buffered(1)),
        scratch_shapes=[pltpu.VMEM((mb * cap_pad, D), F32)],
        compiler_params=_cp(("arbitrary",) * 3, VMEM_LIMIT),
        name="expert_ffn",
    )(xs, gates, w1, w3, w2)


CCOL = 512


def _combine_kernel(offs_ref, x_ref, y_ref, rankc_ref, ada_ref, o_ref, tot_scr, *, per, is_ctx):
    b, tb = pl.program_id(0), pl.program_id(2)
    gt2 = ada_ref[CTX_ROW if is_ctx else b][5:6]
    rc_all = rankc_ref[0]
    cap_pad = y_ref.shape[2]
    half = SLOT // 2
    slot = lax.broadcasted_iota(jnp.int32, (1, SLOT), 1).astype(F32)

    ends, his = {}, {}
    for s in range(per):
        tk = slice(TOK * s, TOK * (s + 1))
        total = jnp.zeros((TOK, tot_scr.shape[1]), F32)
        for e0 in range(0, N_EXPERTS, 2):
            hots, rows = [], []
            for e in (e0, e0 + 1):
                lo = offs_ref[b, e, tb * per + s]
                his[s, e] = offs_ref[b, e, tb * per + s + 1]
                base = pl.multiple_of(jnp.minimum((lo // half) * half, cap_pad - SLOT), half)
                ends[s, e] = base + SLOT
                hots.append(jnp.where(rc_all[tk, e:e + 1] == slot + base.astype(F32), 1.0, 0.0).astype(BF16))
                rows.append(y_ref[0, e, pl.ds(base, SLOT), :])
            total = total + _dot(jnp.concatenate(hots, axis=1), jnp.concatenate(rows, axis=0))
        tot_scr[tk] = total

    for s in range(per):
        tk = slice(TOK * s, TOK * (s + 1))
        for e in range(N_EXPERTS):
            @pl.when(his[s, e] > ends[s, e])
            def _(s=s, e=e, tk=tk):
                rc = rc_all[tk, e:e + 1]

                def body(t, carry):
                    start = ends[s, e] + t * SLOT
                    base = pl.multiple_of(jnp.minimum(start, cap_pad - SLOT), half)
                    hit = (rc == slot + base.astype(F32)) & (rc >= start.astype(F32))
                    tot_scr[tk] += _dot(jnp.where(hit, 1.0, 0.0).astype(BF16), y_ref[0, e, pl.ds(base, SLOT), :])
                    return carry

                lax.fori_loop(0, (his[s, e] - ends[s, e] + SLOT - 1) // SLOT, body, 0)

    o_ref[0] = x_ref[0] + gt2 * tot_scr[...]


def _combine(offs, xu, ys, rank_c, ada_l, *, tb_tok, tb0, n, is_ctx):
    B = xu.shape[0]
    cap_pad = ys.shape[2]
    per = tb_tok // TOK
    return pl.pallas_call(
        functools.partial(_combine_kernel, per=per, is_ctx=is_ctx),
        out_shape=jax.ShapeDtypeStruct(xu.shape, F32),
        grid_spec=pltpu.PrefetchScalarGridSpec(
            num_scalar_prefetch=1,
            grid=(B, D // CCOL, n // tb_tok),
            in_specs=[pl.BlockSpec((1, tb_tok, CCOL), lambda b, c, t, o: (b, tb0 + t, c)),
                      pl.BlockSpec((1, N_EXPERTS, cap_pad, CCOL), lambda b, c, t, o: (b, 0, 0, c),
                                   pipeline_mode=pl.Buffered(1)),
                      pl.BlockSpec((1, tb_tok, N_EXPERTS), lambda b, c, t, o: (b, t, 0)),
                      pl.BlockSpec((ADA_ROWS, ADA_CHUNKS, CCOL), lambda b, c, t, o: (0, 0, c))],
            out_specs=pl.BlockSpec((1, tb_tok, CCOL), lambda b, c, t, o: (b, tb0 + t, c)),
            scratch_shapes=[pltpu.VMEM((tb_tok, CCOL), F32)]),
        input_output_aliases={1: 0},
        compiler_params=_cp(("arbitrary",) * 3, VMEM_LIMIT),
        name="expert_combine",
    )(offs, xu, ys, rank_c, ada_l)


def _moe(xu, hl, pt, ada_l, w1, w3, w2, *, layer, row0, is_ctx):
    B, _, n = pt.shape
    cap = EC_CAPACITY * n // N_EXPERTS
    cap_pad = -(-cap // SLOT) * SLOT
    nb = n // TOK
    rank, offs = _select(pt, cap=cap)
    offs = offs[:, :, :nb + 1]
    gt = min(n, MOE_TOK)
    ct = min(n, COMBINE_TOK)
    xs, gates = _gather(offs, hl, rank, pt, tb_tok=gt, tb0=row0 // gt, n=n, cap_pad=cap_pad, eg=GATHER_EXPERTS)
    mb = B
    ys = _ffn(xs, gates, w1, w3, w2, layer=layer, mb=mb, tf=FFN_TF)
    rank_c = jnp.swapaxes(rank, 1, 2)
    return _combine(offs, xu, ys, rank_c, ada_l, tb_tok=ct, tb0=row0 // ct, n=n, is_ctx=is_ctx)


def _final_kernel(x_ref, g_ref, o_ref):
    x = x_ref[0]
    r = lax.rsqrt(jnp.mean(x * x, axis=-1, keepdims=True) + EPS)
    o_ref[0] = (x * r) * g_ref[...]


def _final_norm(xu, g, *, n):
    B = xu.shape[0]
    tm = MOE_TOK
    return pl.pallas_call(
        _final_kernel,
        out_shape=jax.ShapeDtypeStruct((B, n, D), F32),
        grid=(B, n // tm),
        in_specs=[pl.BlockSpec((1, tm, D), lambda b, t: (b, t, 0)),
                  pl.BlockSpec((1, D), lambda b, t: (0, 0))],
        out_specs=pl.BlockSpec((1, tm, D), lambda b, t: (b, t, 0)),
        compiler_params=_cp(("arbitrary", "arbitrary")),
        name="final_norm",
    )(xu, g)


def _rope_tables(n, ctx):
    rows = n // GRID_W
    t_row = jnp.repeat(jnp.arange(rows), GRID_W)
    t_col = jnp.tile(jnp.arange(GRID_W), rows)
    nf = DA_DIM // 4
    inv = ROPE_THETA ** (-jnp.arange(nf, dtype=F32) / nf)
    ar = t_row[:, None].astype(F32) * inv
    ac = t_col[:, None].astype(F32) * inv
    ang = jnp.concatenate([ar, ar, ac, ac], axis=-1)
    sign = jnp.where((jnp.arange(DA_DIM) % 16) < 8, -1.0, 1.0).astype(F32)
    cos = jnp.concatenate([jnp.cos(ang), jnp.ones((ctx, DA_DIM), F32)], axis=0)
    sin = jnp.concatenate([jnp.sin(ang) * sign, jnp.zeros((ctx, DA_DIM), F32)], axis=0)
    return jnp.tile(cos, (1, 128 // DA_DIM)), jnp.tile(sin, (1, 128 // DA_DIM))


def _pad_heads_cols(w):
    lead = w.shape[:-1]
    w = w.reshape(lead + (M_HEADS, M_DIM))
    w = jnp.pad(w, [(0, 0)] * len(lead) + [(0, 0), (0, M_PAD - M_DIM)])
    return w.reshape(lead + (MP_WIDTH,))


def _kv_tile(nt):
    for parts in range(1, nt // LANES + 1):
        if nt % parts == 0 and (nt // parts) % LANES == 0 and nt // parts <= KV_TILE_MAX:
            return nt // parts
    raise ValueError(nt)


def kernel(x, c, ctx, c_ctx, ada_w, ada_b, norm1_g, norm2_g, w_in, four_w, m_conv_w, m_conv_b, m_gate_b,
           m_norm_g, d_lam, d_norm_g, w_out, router_w, exp_w1, exp_w3, exp_w2, final_g):
    B, N, _ = x.shape
    CTX = ctx.shape[1]
    depth = w_in.shape[0]
    assert CTX == TOK and N % (FFT_N1 * TOK) == 0 and N % Q_TILE == 0 and B <= CTX_ROW
    NT = N + CTX
    PAD = -NT % MOE_TOK
    n_lat = N // TOK
    n2 = N // FFT_N1

    xu = jnp.concatenate([x, ctx, jnp.zeros((B, PAD, D), F32)], axis=1)
    cvecs = jnp.zeros((ADA_ROWS, D), F32).at[:B].set(c).at[CTX_ROW].set(c_ctx)
    ada = _adaln(cvecs, ada_w, ada_b).reshape(depth, ADA_ROWS, ADA_CHUNKS, D)
    cos_t, sin_t = _rope_tables(N, CTX + PAD)
    tabs = _fourier_tables(N, CTX)
    tk = _kv_tile(NT)
    tq = Q_TILE

    hl = jnp.zeros((B, NT + PAD, D), BF16)
    for layer in range(depth):
        ctx_out = layer < depth - 1
        lam_init = 0.8 - 0.6 * math.exp(-0.3 * layer)
        w = w_in[layer]
        wm = jnp.concatenate([w[:, OFF_F:OFF_DQ], w[:, OFF_DQ:OFF_MO], w[:, OFF_DK:OFF_DV]], axis=1).astype(BF16)
        wvt = jnp.concatenate([w[:, OFF_DV:OFF_MV], _pad_heads_cols(w[:, OFF_MO:OFF_MQ]),
                               _pad_heads_cols(w[:, OFF_MV:OFF_G])], axis=1).T.astype(BF16)
        wg = jnp.pad(w[:, OFF_G:], ((0, 0), (0, LANES - N_GATES)))
        wc = jnp.concatenate([_pad_heads_cols(w[:, OFF_MQ:OFF_MK]), _pad_heads_cols(w[:, OFF_MK:OFF_DK])],
                             axis=1).astype(BF16)
        gb = jnp.pad(m_gate_b[layer], (0, LANES - N_GATES)).reshape(1, LANES)
        cw = jnp.concatenate([_pad_heads_cols(m_conv_w[layer][:, :M_WIDTH]),
                              _pad_heads_cols(m_conv_w[layer][:, M_WIDTH:])], axis=1)
        cb = jnp.concatenate([_pad_heads_cols(m_conv_b[layer][:M_WIDTH]),
                              _pad_heads_cols(m_conv_b[layer][M_WIDTH:])]).reshape(1, 2 * MP_WIDTH)
        ada_l = ada[layer]

        y4, dq, dk, dvT, mo, mq, mk, mv, gl, glT = _inproj(
            xu, ada_l, norm1_g[layer].reshape(1, D), wm, wc, wvt, wg, gb, tabs["cs"], cos_t, sin_t, cw, cb,
            n_lat=n_lat, n2=n2)

        wblk = jnp.zeros((F_WIDTH, F_WIDTH), F32)
        for g in range(F_GROUPS):
            wblk = wblk.at[F_GDIM * g:F_GDIM * (g + 1), F_GDIM * g:F_GDIM * (g + 1)].set(four_w[layer, g])
        f_l, f_c = _fourier(y4, tabs, wblk.astype(BF16), n=N, ctx=CTX, with_ctx=ctx_out)

        dlam = d_lam[layer]
        g2 = d_norm_g[layer].reshape(DA_VDIM, 1)
        da_l = _attention(dq, dk, dvT, dlam, g2, lam_init=lam_init, tq=tq, q0=0, nq=N // tq,
                          tk=tk, k0=0, nk=NT // tk)

        hf, hb = _mlstm(mq, mk, mv, gl, glT, n_lat=n_lat)

        mg = _pad_heads_cols(m_norm_g[layer]).reshape(MP_WIDTH, 1)
        wol = w_out[layer]
        wo = wol[:F_WIDTH].astype(BF16)
        wod = jnp.concatenate([wol[F_WIDTH:F_WIDTH + DA_WIDTH],
                               jnp.pad(wol[F_WIDTH + DA_WIDTH:].reshape(M_HEADS, M_DIM, D),
                                       ((0, 0), (0, M_PAD - M_DIM), (0, 0))).reshape(MP_WIDTH, D)],
                              axis=0).astype(BF16)
        g2n = norm2_g[layer].reshape(1, D)
        wrp = jnp.pad(router_w[layer], ((0, 0), (0, LANES - N_EXPERTS)))
        xu, hl, pt_l = _outproj(xu, f_l, da_l, hf, hb, mo, ada_l, mg, wo, wod, g2n, wrp, hl,
                                t0=0, ntl=n_lat, is_ctx=False)
        if ctx_out:
            da_c = _attention(dq, dk, dvT, dlam, g2, lam_init=lam_init, tq=TOK, q0=n_lat, nq=1,
                              tk=TOK, k0=n_lat, nk=1)
            xu, hl, pt_c = _outproj(xu, f_c, da_c, hf, hb, mo, ada_l, mg, wo, wod, g2n, wrp, hl,
                                    t0=n_lat, ntl=1, is_ctx=True)

        xu = _moe(xu, hl, pt_l, ada_l, exp_w1, exp_w3, exp_w2, layer=layer, row0=0, is_ctx=False)
        if ctx_out:
            xu = _moe(xu, hl, pt_c, ada_l, exp_w1, exp_w3, exp_w2, layer=layer, row0=N, is_ctx=True)

    return _final_norm(xu, final_g.reshape(1, D), n=N)
```

```python
import functools
import math

import numpy as np
import jax
import jax.numpy as jnp
from jax import lax
from jax.experimental import pallas as pl
from jax.experimental.pallas import tpu as pltpu

F32 = jnp.float32
BF16 = jnp.bfloat16
HI = lax.Precision.HIGHEST

D = 1024
EPS = 1e-6
GRID_W = 64
ROPE_THETA = 10000.0
F_GROUPS, F_GDIM = 4, 64
F_WIDTH = F_GROUPS * F_GDIM
DA_HEADS, DA_DIM = 6, 32
DA_VDIM = 2 * DA_DIM
DA_WIDTH = DA_HEADS * DA_VDIM
M_HEADS, M_DIM = 4, 96
M_WIDTH = M_HEADS * M_DIM
M_PAD = 128
MP_WIDTH = M_HEADS * M_PAD
N_GATES = 4 * M_HEADS
N_EXPERTS = 16
EC_CAPACITY = 2
D_FF = 2 * D
ADA_CHUNKS = 6
ADA_ROWS = 8
CTX_ROW = 4

LANES = 128
MXU_DIM = 256
V7X_VMEM_BYTES = 64 * 1024 * 1024

TOK = MXU_DIM
FFT_N1 = 16
SLOT = LANES
Q_TILE = 512
KV_TILE_MAX = 8448
MOE_TOK = 512
COMBINE_TOK = 512
FFN_TF = 512
GATHER_EXPERTS = 8
NEG = -1e30

OFF_F = 0
OFF_DQ = OFF_F + F_WIDTH
OFF_MO = OFF_DQ + 2 * DA_HEADS * DA_DIM
OFF_MQ = OFF_MO + M_WIDTH
OFF_MK = OFF_MQ + M_WIDTH
OFF_DK = OFF_MK + M_WIDTH
OFF_DV = OFF_DK + 2 * DA_HEADS * DA_DIM
OFF_MV = OFF_DV + DA_HEADS * DA_VDIM
OFF_G = OFF_MV + M_WIDTH

VMEM_LIMIT = V7X_VMEM_BYTES * 7 // 8


def _cp(sem, vmem=None):
    return pltpu.CompilerParams(dimension_semantics=sem, vmem_limit_bytes=vmem)


def _sigmoid(x):
    return 1.0 / (1.0 + jnp.exp(-x))


def _silu(x):
    return x * _sigmoid(x)


def _dot(a, b, precision=None):
    return jnp.dot(a, b, preferred_element_type=F32, precision=precision)


def _split(a):
    hi = a.astype(BF16)
    return hi, (a - hi.astype(F32)).astype(BF16)


def _dot3(a, b):
    a_hi, a_lo = a if isinstance(a, tuple) else _split(a)
    b_hi, b_lo = b if isinstance(b, tuple) else _split(b)
    return _dot(a_hi, b_hi) + _dot(a_hi, b_lo) + _dot(a_lo, b_hi)


def _split3(a):
    hi = a.astype(BF16)
    r = a - hi.astype(F32)
    mid = r.astype(BF16)
    return hi, mid, (r - mid.astype(F32)).astype(BF16)


def _dot_nt(a, b, precision=None):
    return lax.dot_general(a, b, (((1,), (1,)), ((), ())), preferred_element_type=F32,
                           precision=precision)


def _ada_kernel(c_ref, w_ref, b_ref, o_ref):
    c = c_ref[...]
    o_ref[0] = _dot(_silu(c), w_ref[0], HI) + b_ref[0]


def _adaln(cvecs, ada_w, ada_b):
    depth = ada_w.shape[0]
    tn = 1536
    return pl.pallas_call(
        _ada_kernel,
        out_shape=jax.ShapeDtypeStruct((depth, ADA_ROWS, ADA_CHUNKS * D), F32),
        grid=(depth, ADA_CHUNKS * D // tn),
        in_specs=[pl.BlockSpec((ADA_ROWS, D), lambda l, j: (0, 0)),
                  pl.BlockSpec((1, D, tn), lambda l, j: (l, 0, j)),
                  pl.BlockSpec((1, 1, tn), lambda l, j: (l, 0, j))],
        out_specs=pl.BlockSpec((1, 8, tn), lambda l, j: (l, 0, j)),
        compiler_params=_cp(("arbitrary", "arbitrary")),
        name="adaln",
    )(cvecs, ada_w, ada_b.reshape(depth, 1, ADA_CHUNKS * D))


def _inproj_kernel(x_ref, xp_ref, xn_ref, ada_ref, g_ref, wm_ref, wc_ref, wvt_ref, wg_ref, gb_ref, cs_ref,
                   cos_ref, sin_ref, cw_ref, cb_ref,
                   y_ref, dq_ref, dk_ref, dvt_ref, mo_ref, mq_ref, mk_ref, mv_ref, gl_ref, glt_ref, *, n_lat):
    b = pl.program_id(0)
    t = pl.program_id(1)
    n_tiles = pl.num_programs(1)
    is_ctx = t >= n_lat
    row = jnp.where(is_ctx, CTX_ROW, b)
    mod = ada_ref[row]
    sh, sc = mod[0:1], mod[1:2]

    xa = jnp.concatenate([xp_ref[0], x_ref[0], xn_ref[0]], axis=0)
    r = lax.rsqrt(jnp.mean(xa * xa, axis=-1, keepdims=True) + EPS)
    ha = (xa * r) * g_ref[...] * (1.0 + sc) + sh
    h = ha[8:8 + TOK]
    hb = h.astype(BF16)

    pm = _dot(hb, wm_ref[...])
    o = 0
    pf = pm[:, o:o + F_WIDTH]; o += F_WIDTH
    q = pm[:, o:o + DA_WIDTH]; o += DA_WIDTH
    k = pm[:, o:o + DA_WIDTH]; o += DA_WIDTH
    pt = _dot_nt(wvt_ref[...], hb)
    dvt_ref[0] = pt[:DA_WIDTH].astype(BF16)
    mo_ref[0] = pt[DA_WIDTH:DA_WIDTH + MP_WIDTH].astype(BF16)
    mv_ref[0] = pt[DA_WIDTH + MP_WIDTH:].astype(BF16)

    y_ref[0, 0] = _dot3(pf, cs_ref[...])

    cos = cos_ref[...]
    sin = sin_ref[...]
    lane = lax.broadcasted_iota(jnp.int32, (1, 128), 1)
    low = (lane % 16) < 8

    def rope(z):
        parts = []
        for c in range(DA_WIDTH // 128):
            zc = z[:, 128 * c:128 * (c + 1)]
            rot = jnp.where(low, pltpu.roll(zc, 120, 1), pltpu.roll(zc, 8, 1))
            parts.append(zc * cos + rot * sin)
        return jnp.concatenate(parts, axis=1)

    dq_ref[0] = (rope(q) * (DA_DIM ** -0.5 * math.log2(math.e))).astype(BF16)
    dk_ref[0] = rope(k).astype(BF16)

    gpre = _dot3(h, wg_ref[...]) + gb_ref[...]
    is_forget = (lax.broadcasted_iota(jnp.int32, (1, LANES), 1) % 8) >= 4
    logsig = jnp.minimum(gpre, 0.0) - jnp.log(1.0 + jnp.exp(-jnp.abs(gpre)))
    gl = jnp.where(is_forget, logsig, gpre)
    gl_ref[0] = gl
    glt_ref[0] = gl.T[:N_GATES]

    pc = _dot(ha.astype(BF16), wc_ref[...])
    first = (t == 0) | (t == n_lat)
    last = (t == n_lat - 1) | (t == n_tiles - 1)
    ridx = lax.broadcasted_iota(jnp.int32, (TOK + 16, 1), 0)
    pc = jnp.where(((ridx < 8) & first) | ((ridx >= TOK + 8) & last), 0.0, pc)
    cw = cw_ref[...]
    conv = cb_ref[...] + pc[7:7 + TOK] * cw[0:1] + pc[8:8 + TOK] * cw[1:2] + pc[9:9 + TOK] * cw[2:3]
    act = _silu(conv)
    mq_ref[0] = act[:, :MP_WIDTH].astype(BF16)
    mk_ref[0] = (act[:, MP_WIDTH:] * (M_DIM ** -0.5)).astype(BF16)


def _inproj(xu, ada_l, g1, wm, wc, wvt, wg, gb, cs, cos_t, sin_t, cw, cb, *, n_lat, n2):
    B, NT, _ = xu.shape
    nt = n_lat + 1
    rper = n2 // TOK
    tok3 = lambda w: pl.BlockSpec((1, TOK, w), lambda b, t: (b, t, 0))
    full = lambda a: pl.BlockSpec(a.shape, lambda b, t: (0,) * a.ndim)
    nb8 = NT // 8
    outs = [jax.ShapeDtypeStruct((B, 2 * FFT_N1, n2, 2 * F_WIDTH), F32)]
    nq = -(-NT // Q_TILE) * Q_TILE
    outs += [jax.ShapeDtypeStruct((B, nq, DA_WIDTH), BF16), jax.ShapeDtypeStruct((B, nt * TOK, DA_WIDTH), BF16)]
    outs += [jax.ShapeDtypeStruct((B, DA_WIDTH, nt * TOK), BF16)]
    trs = lambda r: pl.BlockSpec((1, r, TOK), lambda b, t: (b, 0, t))
    trp = jax.ShapeDtypeStruct((B, MP_WIDTH, NT), BF16)
    outs += [trp, jax.ShapeDtypeStruct((B, NT, MP_WIDTH), BF16), jax.ShapeDtypeStruct((B, NT, MP_WIDTH), BF16), trp]
    outs += [jax.ShapeDtypeStruct((B, NT, 128), F32), jax.ShapeDtypeStruct((B, N_GATES, nt * TOK), F32)]
    out_specs = [pl.BlockSpec((1, 1, TOK, 2 * F_WIDTH), lambda b, t: (b, t // rper, t % rper, 0))]
    out_specs += [tok3(DA_WIDTH)] * 2 + [trs(DA_WIDTH)]
    out_specs += [trs(MP_WIDTH), tok3(MP_WIDTH), tok3(MP_WIDTH), trs(MP_WIDTH)]
    out_specs += [tok3(128), pl.BlockSpec((1, N_GATES, TOK), lambda b, t: (b, 0, t))]
    return pl.pallas_call(
        functools.partial(_inproj_kernel, n_lat=n_lat),
        out_shape=outs,
        grid=(B, nt),
        in_specs=[tok3(D),
                  pl.BlockSpec((1, 8, D), lambda b, t: (b, jnp.maximum(t * (TOK // 8) - 1, 0), 0)),
                  pl.BlockSpec((1, 8, D), lambda b, t: (b, jnp.minimum((t + 1) * (TOK // 8), nb8 - 1), 0)),
                  full(ada_l), full(g1), full(wm), full(wc), full(wvt), full(wg), full(gb), full(cs),
                  pl.BlockSpec((TOK, LANES), lambda b, t: (t, 0)),
                  pl.BlockSpec((TOK, LANES), lambda b, t: (t, 0)),
                  full(cw), full(cb)],
        out_specs=out_specs,
        compiler_params=_cp(("arbitrary", "arbitrary"), VMEM_LIMIT),
        name="norm1_inproj",
    )(xu, xu, xu, ada_l, g1, wm, wc, wvt, wg, gb, cs, cos_t, sin_t, cw, cb)


def _fft1_kernel(y_ref, kc_ref, ks_ref, tc_ref, ts_ref, o_ref, *, groups):
    kc = _split(kc_ref[...])
    ks = _split(ks_ref[...])
    for g in range(groups):
        blk = _split(y_ref[0, :, 8 * g:8 * (g + 1), :].reshape(FFT_N1 * 8, 2 * F_WIDTH))
        p = _dot3(kc, blk)
        q = _dot3(ks, blk)
        ar = p[:, :F_WIDTH] - q[:, F_WIDTH:]
        ai = -p[:, F_WIDTH:] - q[:, :F_WIDTH]
        tc = tc_ref[128 * g:128 * (g + 1), :]
        ts = ts_ref[128 * g:128 * (g + 1), :]
        tc = jnp.concatenate([tc, tc], axis=1)
        ts = jnp.concatenate([ts, ts], axis=1)
        br = ar * tc + ai * ts
        bi = ai * tc - ar * ts
        o_ref[0, :, 8 * g:8 * (g + 1), :] = jnp.concatenate([br, bi], axis=1).reshape(FFT_N1, 8, 2 * F_WIDTH)


def _fft2_kernel(b_ref, c2_ref, s2_ref, wb_ref, perm_ref, o_ref, r_scr, *, n2):
    c2 = _split(c2_ref[...])
    s2 = _split(s2_ref[...])
    for i in range(8):
        blk = b_ref[0, i]
        xr = _dot3(c2, blk[:, :F_WIDTH]) + _dot3(s2, blk[:, F_WIDTH:])
        r_scr[i] = _dot(xr.astype(BF16), wb_ref[...]).astype(BF16)
    for t in range(n2 // 32):
        rows = jnp.concatenate([r_scr[i, 32 * t:32 * (t + 1), :] for i in range(8)], axis=0)
        o_ref[0, 32 * t:32 * (t + 1), :, :] = _dot(perm_ref[...], rows).reshape(32, 8, F_WIDTH)


def _fftc_kernel(y_ref, c_ref, s_ref, wb_ref, o_ref):
    y = y_ref[0, 0]
    z = _dot3(c_ref[...], y[:, :F_WIDTH]) - _dot3(s_ref[...], y[:, F_WIDTH:])
    o_ref[0] = _dot(z.astype(BF16), wb_ref[...])


def _fourier_tables(n, ctx):
    n1, n2 = FFT_N1, n // FFT_N1
    a = np.arange(n1)
    ang1 = 2 * np.pi * np.outer(a, a) / n1
    eye8 = np.eye(8)
    kc = np.kron(np.cos(ang1), eye8)
    ks = np.kron(np.sin(ang1), eye8)
    n2i = np.arange(n2).reshape(n2 // 8, 1, 8)
    k1 = np.arange(n1).reshape(1, n1, 1)
    angt = (2 * np.pi * n2i * k1 / n).reshape(-1, 1)
    tc = np.broadcast_to(np.cos(angt), (n2 // 8 * 128, 128))
    ts = np.broadcast_to(np.sin(angt), (n2 // 8 * 128, 128))
    b = np.arange(n2)
    ang2 = 2 * np.pi * np.outer(b, b) / n2
    c2 = np.cos(ang2) / math.sqrt(n)
    s2 = np.sin(ang2) / math.sqrt(n)
    perm = np.zeros((256, 256))
    for kk in range(8):
        for j in range(32):
            perm[j * 8 + kk, kk * 32 + j] = 1.0
    cc = np.arange(ctx)
    angc = 2 * np.pi * np.outer(cc, cc) / ctx
    cctx = np.cos(angc) / math.sqrt(ctx)
    sctx = np.sin(angc) / math.sqrt(ctx)
    ch = np.arange(F_GDIM)
    angch = 2 * np.pi * np.outer(ch, ch) / F_GDIM
    cs = np.concatenate([np.kron(np.eye(F_GROUPS), np.cos(angch)),
                         np.kron(np.eye(F_GROUPS), np.sin(angch))], axis=1) / math.sqrt(F_GDIM)
    f = lambda z: jnp.asarray(np.ascontiguousarray(z), dtype=F32)
    return dict(kc=f(kc), ks=f(ks), tc=f(tc), ts=f(ts), c2=f(c2), s2=f(s2), perm=f(perm).astype(BF16),
                cctx=f(cctx), sctx=f(sctx), cs=f(cs))


def _fourier(y4, tabs, wblk, *, n, ctx, with_ctx):
    B = y4.shape[0]
    n2 = n // FFT_N1
    groups = 4
    full = lambda a, nd: pl.BlockSpec(a.shape, lambda *i: (0,) * a.ndim)
    b4 = pl.pallas_call(
        functools.partial(_fft1_kernel, groups=groups),
        out_shape=jax.ShapeDtypeStruct((B, FFT_N1, n2, 2 * F_WIDTH), F32),
        grid=(B, n2 // (8 * groups)),
        in_specs=[pl.BlockSpec((1, FFT_N1, 8 * groups, 2 * F_WIDTH), lambda b, j: (b, 0, j, 0)),
                  full(tabs["kc"], 2), full(tabs["ks"], 2),
                  pl.BlockSpec((128 * groups, 128), lambda b, j: (j, 0)),
                  pl.BlockSpec((128 * groups, 128), lambda b, j: (j, 0))],
        out_specs=pl.BlockSpec((1, FFT_N1, 8 * groups, 2 * F_WIDTH), lambda b, j: (b, 0, j, 0)),
        compiler_params=_cp(("arbitrary", "arbitrary")),
        name="fourier_stage1",
    )(y4, tabs["kc"], tabs["ks"], tabs["tc"], tabs["ts"])
    f4 = pl.pallas_call(
        functools.partial(_fft2_kernel, n2=n2),
        out_shape=jax.ShapeDtypeStruct((B, n2, 16, F_WIDTH), F32),
        grid=(B, FFT_N1 // 8),
        in_specs=[pl.BlockSpec((1, 8, n2, 2 * F_WIDTH), lambda b, j: (b, j, 0, 0)),
                  full(tabs["c2"], 2), full(tabs["s2"], 2), full(wblk, 2), full(tabs["perm"], 2)],
        out_specs=pl.BlockSpec((1, n2, 8, F_WIDTH), lambda b, j: (b, 0, j, 0)),
        scratch_shapes=[pltpu.VMEM((8, n2, F_WIDTH), BF16)],
        compiler_params=_cp(("arbitrary", "arbitrary"), VMEM_LIMIT),
        name="fourier_stage2",
    )(b4, tabs["c2"], tabs["s2"], wblk, tabs["perm"])
    f_ctx = None
    if with_ctx:
        f_ctx = pl.pallas_call(
            _fftc_kernel,
            out_shape=jax.ShapeDtypeStruct((B, ctx, F_WIDTH), F32),
            grid=(B,),
            in_specs=[pl.BlockSpec((1, 1, TOK, 2 * F_WIDTH), lambda b: (b, FFT_N1, 0, 0)),
                      full(tabs["cctx"], 1), full(tabs["sctx"], 1), full(wblk, 1)],
            out_specs=pl.BlockSpec((1, ctx, F_WIDTH), lambda b: (b, 0, 0)),
            compiler_params=_cp(("arbitrary",)),
            name="fourier_ctx",
        )(y4, tabs["cctx"], tabs["sctx"], wblk)
    return f4.reshape(B, n, F_WIDTH), f_ctx


VROWS = DA_VDIM + 16


def _attn_kernel(q_ref, k_ref, vt_ref, dl_ref, g_ref, o_ref, m_scr, acc_scr, *, lam_init):
    kt = pl.program_id(3)
    nk = pl.num_programs(3)

    @pl.when(kt == 0)
    def _():
        m_scr[...] = jnp.full(m_scr.shape, NEG, F32)
        acc_scr[...] = jnp.zeros(acc_scr.shape, F32)

    q = q_ref[0]
    k = k_ref[0]
    vt = vt_ref[0]
    ones = jnp.ones((16, vt.shape[1]), BF16)
    lhs = [jnp.concatenate([vt[DA_VDIM * h:DA_VDIM * (h + 1)], ones], axis=0) for h in range(2)]
    lane = lax.broadcasted_iota(jnp.int32, (1, LANES), 1)
    zero = jnp.zeros((), BF16)

    def scores(j):
        return _dot_nt(k, jnp.where((lane // DA_DIM) == j, q, zero))

    st_next = scores(0)
    for j in range(4):
        st = st_next
        if j < 3:
            st_next = scores(j + 1)
        m_old = m_scr[j]
        m_new = jnp.maximum(m_old, jnp.max(st, axis=0, keepdims=True))
        alpha = jnp.exp2(m_old - m_new)
        pt = jnp.exp2(st - m_new).astype(BF16)
        acc_scr[j] = alpha * acc_scr[j] + _dot(lhs[j // 2], pt)
        m_scr[j] = m_new

    @pl.when(kt == nk - 1)
    def _():
        dl = dl_ref[...]
        lam = (jnp.exp(jnp.sum(dl[0:1] * dl[1:2], keepdims=True))
               - jnp.exp(jnp.sum(dl[2:3] * dl[3:4], keepdims=True)) + lam_init)
        outs = []
        for h in range(2):
            a0 = acc_scr[2 * h]
            a1 = acc_scr[2 * h + 1]
            o = (a0[:DA_VDIM] / a0[DA_VDIM:DA_VDIM + 1]
                 - lam * (a1[:DA_VDIM] / a1[DA_VDIM:DA_VDIM + 1]))
            r = lax.rsqrt(jnp.mean(o * o, axis=0, keepdims=True) + EPS)
            outs.append(((o * r) * g_ref[...]) * (1.0 - lam_init))
        o_ref[0] = jnp.concatenate(outs, axis=0).astype(BF16)


def _attention(dq, dk, dvT, dlam, gcol, *, lam_init, tq, q0, nq, tk, k0, nk):
    B = dq.shape[0]
    return pl.pallas_call(
        functools.partial(_attn_kernel, lam_init=lam_init),
        out_shape=jax.ShapeDtypeStruct((B, DA_WIDTH, nq * tq), BF16),
        grid=(B, DA_WIDTH // 128, nq, nk),
        in_specs=[pl.BlockSpec((1, tq, 128), lambda b, p, i, j: (b, q0 + i, p)),
                  pl.BlockSpec((1, tk, 128), lambda b, p, i, j: (b, k0 + j, p)),
                  pl.BlockSpec((1, 128, tk), lambda b, p, i, j: (b, p, k0 + j)),
                  pl.BlockSpec(dlam.shape, lambda b, p, i, j: (0, 0)),
                  pl.BlockSpec(gcol.shape, lambda b, p, i, j: (0, 0))],
        out_specs=pl.BlockSpec((1, 128, tq), lambda b, p, i, j: (b, p, i)),
        scratch_shapes=[pltpu.VMEM((4, 1, tq), F32), pltpu.VMEM((4, VROWS, tq), F32)],
        compiler_params=_cp(("arbitrary",) * 4, VMEM_LIMIT),
        name="diff_attention",
    )(dq, dk, dvT, dlam, gcol)


def _mlstm_kernel(qf_ref, kf_ref, vf_ref, gcf_ref, grf_ref, qb_ref, kb_ref, vb_ref, gcb_ref, grb_ref,
                  hf_ref, hb_ref, c_scr, m_scr):
    t = pl.program_id(1)

    @pl.when(t == 0)
    def _():
        c_scr[...] = jnp.zeros(c_scr.shape, F32)
        m_scr[...] = jnp.zeros(m_scr.shape, F32)

    L = TOK
    si = lax.broadcasted_iota(jnp.int32, (L, L), 0)
    li = lax.broadcasted_iota(jnp.int32, (L, L), 1)
    dirs = ((qf_ref, kf_ref, vf_ref, gcf_ref, grf_ref, hf_ref, si <= li, li <= si, L - 1),
            (qb_ref, kb_ref, vb_ref, gcb_ref, grb_ref, hb_ref, si >= li, li >= si, 0))
    ones = jnp.ones((16, L), F32)
    for d, (q_ref, k_ref, vt_ref, gc_ref, gr_ref, h_ref, seen, seen_t, last) in enumerate(dirs):
        gc = gc_ref[0]
        gr = gr_ref[0]
        seen_b = jnp.where(seen, 1.0, 0.0).astype(BF16)
        seen_tb = jnp.where(seen_t, 1.0, 0.0).astype(BF16)
        bcols = sum(_dot(seen_tb, piece) for piece in _split3(gc))
        brows = sum(_dot(piece, seen_b) for piece in _split3(gr))
        for hd in range(M_HEADS):
            idx = d * M_HEADS + hd
            ji = d * 8 + hd
            jf = d * 8 + 4 + hd
            sl = slice(M_PAD * hd, M_PAD * (hd + 1))
            q = q_ref[0, :, sl]
            k = k_ref[0, :, sl]
            vt = vt_ref[0, sl, :]
            b_row = brows[jf:jf + 1, :]
            cs = gc[:, ji:ji + 1] - bcols[:, jf:jf + 1]
            li_row = gr[ji:ji + 1, :]
            m_old = m_scr[idx][0:1, 0:1]
            c_old = c_scr[idx]

            dlog = jnp.where(seen, b_row + cs, NEG)
            inter = b_row + m_old
            m_t = jnp.maximum(inter, jnp.max(dlog, axis=0, keepdims=True))
            w_inter = jnp.exp(inter - m_t)
            st = _dot_nt(k, q) * jnp.exp(dlog - m_t)
            cq = _dot_nt(c_old.astype(BF16), q)
            num = w_inter * cq[:M_PAD] + _dot(vt, st.astype(BF16))
            den = w_inter * cq[M_PAD:M_PAD + 1] + jnp.sum(st, axis=0, keepdims=True)
            h_ref[0, sl, :] = num / jnp.maximum(jnp.abs(den), jnp.exp(-m_t))

            total = b_row[:, last:last + 1]
            wlog = total - b_row + li_row
            m_new = jnp.maximum(total + m_old, jnp.max(wlog, axis=1, keepdims=True))
            decay = jnp.exp(total + m_old - m_new)
            w = jnp.exp(wlog - m_new)
            vw = jnp.concatenate([vt.astype(F32) * w, ones * w], axis=0).astype(BF16)
            c_scr[idx] = decay * c_old + _dot(vw, k)
            m_scr[idx] = jnp.broadcast_to(m_new, (8, 128))


def _mlstm(mq, mk, mvT, gl, glT, *, n_lat):
    B, NT, _ = mq.shape
    nt = n_lat + 1
    fwd = lambda t: jnp.where(t == 0, n_lat, t - 1)
    bwd = lambda t: jnp.where(t == 0, n_lat, n_lat - t)
    tok = lambda w, f: pl.BlockSpec((1, TOK, w), lambda b, t: (b, f(t), 0))
    lanes = lambda r, f: pl.BlockSpec((1, r, TOK), lambda b, t: (b, 0, f(t)))
    ins, specs = [], []
    for f in (fwd, bwd):
        ins += [mq, mk, mvT, gl, glT]
        specs += [tok(MP_WIDTH, f)] * 2 + [lanes(MP_WIDTH, f), tok(128, f), lanes(N_GATES, f)]
    return pl.pallas_call(
        _mlstm_kernel,
        out_shape=[jax.ShapeDtypeStruct((B, MP_WIDTH, NT), F32)] * 2,
        grid=(B, nt),
        in_specs=specs,
        out_specs=[lanes(MP_WIDTH, fwd), lanes(MP_WIDTH, bwd)],
        scratch_shapes=[pltpu.VMEM((2 * M_HEADS, M_PAD + 16, M_PAD), F32),
                        pltpu.VMEM((2 * M_HEADS, 8, 128), F32)],
        compiler_params=_cp(("arbitrary", "arbitrary"), VMEM_LIMIT),
        name="mlstm",
    )(*ins)


def _outproj_kernel(x_ref, f_ref, dat_ref, hf_ref, hb_ref, mo_ref, ada_ref, mg_ref, wo_ref, wod_ref, g2_ref, wr_ref,
                    xo_ref, hl_ref, pt_ref, *, is_ctx):
    b = pl.program_id(0)
    mod = ada_ref[CTX_ROW if is_ctx else b]
    gt1, sh2, sc2 = mod[2:3], mod[3:4], mod[4:5]
    mg = mg_ref[...]
    for s in range(x_ref.shape[1] // TOK):
        tk = slice(TOK * s, TOK * (s + 1))
        hs = hf_ref[0, :, tk] + hb_ref[0, :, tk]
        og = mo_ref[0, :, tk].astype(F32)
        parts = [dat_ref[0, :, tk]]
        for hd in range(M_HEADS):
            sl = slice(M_PAD * hd, M_PAD * (hd + 1))
            hh = hs[sl]
            r = lax.rsqrt(jnp.sum(hh * hh, axis=0, keepdims=True) * (1.0 / M_DIM) + EPS)
            parts.append((((hh * r) * mg[sl]) * _sigmoid(og[sl])).astype(BF16))
        mix_t = jnp.concatenate(parts, axis=0)
        upd = _dot(f_ref[0, tk].astype(BF16), wo_ref[...]) + lax.dot_general(
            mix_t, wod_ref[...], (((0,), (0,)), ((), ())), preferred_element_type=F32)
        xn = x_ref[0, tk] + gt1 * upd
        xo_ref[0, tk] = xn
        r = lax.rsqrt(jnp.mean(xn * xn, axis=-1, keepdims=True) + EPS)
        h2 = (xn * r) * g2_ref[...] * (1.0 + sc2) + sh2
        hl_ref[0, tk] = h2.astype(BF16)
        lt = _dot3(h2, wr_ref[...]).T[:N_EXPERTS]
        ex = jnp.exp(lt - jnp.max(lt, axis=0, keepdims=True))
        pt_ref[0, :, tk] = ex / jnp.sum(ex, axis=0, keepdims=True)


def _outproj_kernel_aliased(x_ref, f_ref, dat_ref, hf_ref, hb_ref, mo_ref, ada_ref, mg_ref, wo_ref, wod_ref,
                            g2_ref, wr_ref, hlp_ref, xo_ref, hl_ref, pt_ref, *, is_ctx):
    del hlp_ref
    _outproj_kernel(x_ref, f_ref, dat_ref, hf_ref, hb_ref, mo_ref, ada_ref, mg_ref, wo_ref, wod_ref, g2_ref,
                    wr_ref, xo_ref, hl_ref, pt_ref, is_ctx=is_ctx)


def _outproj(xu, f, daT, hf, hb, mo, ada_l, mg, wo, wod, g2, wrp, hl_prev, *, t0, ntl, is_ctx):
    B, NT, _ = xu.shape
    n = ntl * TOK
    tile = 2 * TOK if n % (2 * TOK) == 0 else TOK
    o = t0 * TOK // tile
    tok = lambda w: pl.BlockSpec((1, tile, w), lambda b, t: (b, o + t, 0))
    trs = lambda r: pl.BlockSpec((1, r, tile), lambda b, t: (b, 0, o + t))
    loc = lambda w: pl.BlockSpec((1, tile, w), lambda b, t: (b, t, 0))
    full = lambda a: pl.BlockSpec(a.shape, lambda b, t: (0,) * a.ndim)
    return pl.pallas_call(
        functools.partial(_outproj_kernel_aliased, is_ctx=is_ctx),
        out_shape=[jax.ShapeDtypeStruct(xu.shape, F32), jax.ShapeDtypeStruct((B, NT, D), BF16),
                   jax.ShapeDtypeStruct((B, N_EXPERTS, n), F32)],
        grid=(B, n // tile),
        in_specs=[tok(D), loc(F_WIDTH), pl.BlockSpec((1, DA_WIDTH, tile), lambda b, t: (b, 0, t)),
                  trs(MP_WIDTH), trs(MP_WIDTH), trs(MP_WIDTH),
                  full(ada_l), full(mg), full(wo), full(wod), full(g2), full(wrp),
                  pl.BlockSpec(memory_space=pl.ANY)],
        out_specs=[tok(D), tok(D), pl.BlockSpec((1, N_EXPERTS, tile), lambda b, t: (b, 0, t))],
        input_output_aliases={0: 0, 12: 1},
        compiler_params=_cp(("arbitrary", "arbitrary"), VMEM_LIMIT),
        name="outproj_norm2_router",
    )(xu, f, daT, hf, hb, mo, ada_l, mg, wo, wod, g2, wrp, hl_prev)


def _select_kernel(p_ref, rank_ref, offs_ref, *, n, cap):
    p = p_ref[0]
    xi = pltpu.bitcast(p, jnp.int32)

    def body(i, lo):
        cand = lo | jnp.left_shift(jnp.int32(1), 30 - i)
        cnt = jnp.sum(jnp.where(xi >= cand, 1.0, 0.0), axis=1, keepdims=True)
        return jnp.where(cnt >= cap, cand, lo)

    thr = lax.fori_loop(0, 31, body, jnp.zeros((N_EXPERTS, 1), jnp.int32))
    nb = n // TOK
    rows = lax.broadcasted_iota(jnp.int32, (n, 128), 0)
    cols = lax.broadcasted_iota(jnp.int32, (n, 128), 1)
    blk_ind = jnp.where((rows // TOK) == cols, 1.0, 0.0).astype(BF16)
    u128 = jnp.where(lax.broadcasted_iota(jnp.int32, (128, 128), 0)
                     < lax.broadcasted_iota(jnp.int32, (128, 128), 1), 1.0, 0.0).astype(BF16)
    utok = jnp.where(lax.broadcasted_iota(jnp.int32, (TOK, TOK), 0)
                     < lax.broadcasted_iota(jnp.int32, (TOK, TOK), 1), 1.0, 0.0).astype(BF16)

    def prefix(mf):
        mb = mf.astype(BF16)
        counts = _dot(mb, blk_ind)
        offs = _dot(counts.astype(BF16), u128)
        pieces = [_dot(mb[:, TOK * j:TOK * (j + 1)], utok) + offs[:, j:j + 1] for j in range(nb)]
        return (jnp.concatenate(pieces, axis=1) if nb > 1 else pieces[0]), offs

    gt = xi > thr
    eq = xi == thr
    need = cap - jnp.sum(jnp.where(gt, 1.0, 0.0), axis=1, keepdims=True)
    rank_eq, _ = prefix(jnp.where(eq, 1.0, 0.0))
    sel = gt | (eq & (rank_eq < need))
    rank, offs = prefix(jnp.where(sel, 1.0, 0.0))
    rank_ref[0] = jnp.where(sel, rank, -1.0)
    offs_ref[0] = offs.astype(jnp.int32)


def _select(pt, *, cap):
    B, _, n = pt.shape
    return pl.pallas_call(
        functools.partial(_select_kernel, n=n, cap=cap),
        out_shape=[jax.ShapeDtypeStruct((B, N_EXPERTS, n), F32),
                   jax.ShapeDtypeStruct((B, N_EXPERTS, 128), jnp.int32)],
        grid=(B,),
        in_specs=[pl.BlockSpec((1, N_EXPERTS, n), lambda b: (b, 0, 0))],
        out_specs=[pl.BlockSpec((1, N_EXPERTS, n), lambda b: (b, 0, 0)),
                   pl.BlockSpec((1, N_EXPERTS, 128), lambda b: (b, 0, 0))],
        compiler_params=_cp(("arbitrary",), VMEM_LIMIT),
        name="expert_choice_select",
    )(pt)


def _gather_kernel(offs_ref, h_ref, rank_ref, prob_ref, o_ref, gate_ref, *, eg, per):
    b, g, tb = pl.program_id(0), pl.program_id(1), pl.program_id(2)

    @pl.when(tb == 0)
    def _():
        o_ref[...] = jnp.zeros(o_ref.shape, BF16)
        gate_ref[...] = jnp.zeros(gate_ref.shape, F32)

    cap_pad = o_ref.shape[2]
    half = SLOT // 2
    slot = lax.broadcasted_iota(jnp.int32, (SLOT, TOK), 0).astype(F32)

    def add_rows(i, r, p, h, base, start=None):
        hit = r == slot + base.astype(F32)
        if start is not None:
            hit = hit & (r >= start.astype(F32))
        rows = _dot(jnp.where(hit, 1.0, 0.0).astype(BF16), h).astype(BF16)
        o_ref[0, i, pl.ds(base, SLOT), :] = o_ref[0, i, pl.ds(base, SLOT), :] + rows
        gate_ref[0, i, pl.ds(base, SLOT), :] = (gate_ref[0, i, pl.ds(base, SLOT), :]
                                                + jnp.sum(jnp.where(hit, p, 0.0), axis=1, keepdims=True))

    def operands(s, i):
        tk = slice(TOK * s, TOK * (s + 1))
        e = g * eg + i
        return rank_ref[0, pl.ds(e, 1), tk], prob_ref[0, pl.ds(e, 1), tk], h_ref[0, tk, :]

    ends, his = {}, {}
    for s in range(per):
        for i in range(eg):
            e = g * eg + i
            lo = offs_ref[b, e, tb * per + s]
            his[s, i] = offs_ref[b, e, tb * per + s + 1]
            base = pl.multiple_of(jnp.minimum((lo // half) * half, cap_pad - SLOT), half)
            add_rows(i, *operands(s, i), base)
            ends[s, i] = base + SLOT

    for s in range(per):
        for i in range(eg):
            @pl.when(his[s, i] > ends[s, i])
            def _(s=s, i=i):
                def body(t, carry):
                    start = ends[s, i] + t * SLOT
                    base = pl.multiple_of(jnp.minimum(start, cap_pad - SLOT), half)
                    add_rows(i, *operands(s, i), base, start)
                    return carry

                lax.fori_loop(0, (his[s, i] - ends[s, i] + SLOT - 1) // SLOT, body, 0)


def _gather(offs, hl, rank, pt, *, tb_tok, tb0, n, cap_pad, eg):
    B = hl.shape[0]
    per = tb_tok // TOK
    return pl.pallas_call(
        functools.partial(_gather_kernel, eg=eg, per=per),
        out_shape=[jax.ShapeDtypeStruct((B, N_EXPERTS, cap_pad, D), BF16),
                   jax.ShapeDtypeStruct((B, N_EXPERTS, cap_pad, 1), F32)],
        grid_spec=pltpu.PrefetchScalarGridSpec(
            num_scalar_prefetch=1,
            grid=(B, N_EXPERTS // eg, n // tb_tok),
            in_specs=[pl.BlockSpec((1, tb_tok, D), lambda b, g, t, o: (b, tb0 + t, 0)),
                      pl.BlockSpec((1, N_EXPERTS, tb_tok), lambda b, g, t, o: (b, 0, t)),
                      pl.BlockSpec((1, N_EXPERTS, tb_tok), lambda b, g, t, o: (b, 0, t))],
            out_specs=[pl.BlockSpec((1, eg, cap_pad, D), lambda b, g, t, o: (b, g, 0, 0)),
                       pl.BlockSpec((1, eg, cap_pad, 1), lambda b, g, t, o: (b, g, 0, 0))]),
        compiler_params=_cp(("arbitrary",) * 3, VMEM_LIMIT),
        name="expert_gather",
    )(offs, hl, rank, pt)


FFN_ROWS = 1024


def _ffn_kernel(x_ref, gate_ref, w1_ref, w3_ref, w2_ref, y_ref, acc_ref):
    f = pl.program_id(2)

    @pl.when(f == 0)
    def _():
        acc_ref[...] = jnp.zeros(acc_ref.shape, F32)

    w1 = w1_ref[0, 0].astype(BF16)
    w3 = w3_ref[0, 0].astype(BF16)
    w2 = w2_ref[0, 0].astype(BF16)
    mb, _, cap_pad, _ = x_ref.shape
    rows = min(FFN_ROWS, cap_pad)
    for i in range(mb):
        for r in range(0, cap_pad, rows):
            x = x_ref[i, 0, r:r + rows, :]
            hid = (_silu(_dot(x, w1)) * _dot(x, w3)).astype(BF16)
            acc_ref[i * cap_pad + r:i * cap_pad + r + rows, :] += _dot(hid, w2)

    @pl.when(f == pl.num_programs(2) - 1)
    def _():
        gate = gate_ref[...].reshape(-1, 1)
        y_ref[...] = (acc_ref[...] * gate).astype(BF16).reshape(y_ref.shape)


def _ffn(xs, gates, w1, w3, w2, *, layer, mb, tf):
    B, E, cap_pad, _ = xs.shape
    return pl.pallas_call(
        _ffn_kernel,
        out_shape=jax.ShapeDtypeStruct(xs.shape, BF16),
        grid=(E, B // mb, D_FF // tf),
        in_specs=[pl.BlockSpec((mb, 1, cap_pad, D), lambda e, m, f: (m, e, 0, 0)),
                  pl.BlockSpec((mb, 1, cap_pad, 1), lambda e, m, f: (m, e, 0, 0)),
                  pl.BlockSpec((1, 1, D, tf), lambda e, m, f: (layer, e, 0, f)),
                  pl.BlockSpec((1, 1, D, tf), lambda e, m, f: (layer, e, 0, f)),
                  pl.BlockSpec((1, 1, tf, D), lambda e, m, f: (layer, e, f, 0))],
        out_specs=pl.BlockSpec((mb, 1, cap_pad, D), lambda e, m, f: (m, e, 0, 0)),
        scratch_shapes=[pltpu.VMEM((mb * cap_pad, D), F32)],
        compiler_params=_cp(("arbitrary",) * 3, VMEM_LIMIT),
        name="expert_ffn",
    )(xs, gates, w1, w3, w2)


CCOL = 512


def _combine_kernel(offs_ref, x_ref, y_ref, rankc_ref, ada_ref, o_ref, tot_scr, *, per, is_ctx):
    b, tb = pl.program_id(0), pl.program_id(2)
    gt2 = ada_ref[CTX_ROW if is_ctx else b][5:6]
    rc_all = rankc_ref[0]
    cap_pad = y_ref.shape[2]
    half = SLOT // 2
    slot = lax.broadcasted_iota(jnp.int32, (1, SLOT), 1).astype(F32)

    ends, his = {}, {}
    for s in range(per):
        tk = slice(TOK * s, TOK * (s + 1))
        total = jnp.zeros((TOK, tot_scr.shape[1]), F32)
        for e0 in range(0, N_EXPERTS, 2):
            hots, rows = [], []
            for e in (e0, e0 + 1):
                lo = offs_ref[b, e, tb * per + s]
                his[s, e] = offs_ref[b, e, tb * per + s + 1]
                base = pl.multiple_of(jnp.minimum((lo // half) * half, cap_pad - SLOT), half)
                ends[s, e] = base + SLOT
                hots.append(jnp.where(rc_all[tk, e:e + 1] == slot + base.astype(F32), 1.0, 0.0).astype(BF16))
                rows.append(y_ref[0, e, pl.ds(base, SLOT), :])
            total = total + _dot(jnp.concatenate(hots, axis=1), jnp.concatenate(rows, axis=0))
        tot_scr[tk] = total

    for s in range(per):
        tk = slice(TOK * s, TOK * (s + 1))
        for e in range(N_EXPERTS):
            @pl.when(his[s, e] > ends[s, e])
            def _(s=s, e=e, tk=tk):
                rc = rc_all[tk, e:e + 1]

                def body(t, carry):
                    start = ends[s, e] + t * SLOT
                    base = pl.multiple_of(jnp.minimum(start, cap_pad - SLOT), half)
                    hit = (rc == slot + base.astype(F32)) & (rc >= start.astype(F32))
                    tot_scr[tk] += _dot(jnp.where(hit, 1.0, 0.0).astype(BF16), y_ref[0, e, pl.ds(base, SLOT), :])
                    return carry

                lax.fori_loop(0, (his[s, e] - ends[s, e] + SLOT - 1) // SLOT, body, 0)

    o_ref[0] = x_ref[0] + gt2 * tot_scr[...]


def _combine(offs, xu, ys, rank_c, ada_l, *, tb_tok, tb0, n, is_ctx):
    B = xu.shape[0]
    cap_pad = ys.shape[2]
    per = tb_tok // TOK
    return pl.pallas_call(
        functools.partial(_combine_kernel, per=per, is_ctx=is_ctx),
        out_shape=jax.ShapeDtypeStruct(xu.shape, F32),
        grid_spec=pltpu.PrefetchScalarGridSpec(
            num_scalar_prefetch=1,
            grid=(B, D // CCOL, n // tb_tok),
            in_specs=[pl.BlockSpec((1, tb_tok, CCOL), lambda b, c, t, o: (b, tb0 + t, c)),
                      pl.BlockSpec((1, N_EXPERTS, cap_pad, CCOL), lambda b, c, t, o: (b, 0, 0, c)),
                      pl.BlockSpec((1, tb_tok, N_EXPERTS), lambda b, c, t, o: (b, t, 0)),
                      pl.BlockSpec((ADA_ROWS, ADA_CHUNKS, CCOL), lambda b, c, t, o: (0, 0, c))],
            out_specs=pl.BlockSpec((1, tb_tok, CCOL), lambda b, c, t, o: (b, tb0 + t, c)),
            scratch_shapes=[pltpu.VMEM((tb_tok, CCOL), F32)]),
        input_output_aliases={1: 0},
        compiler_params=_cp(("arbitrary",) * 3, VMEM_LIMIT),
        name="expert_combine",
    )(offs, xu, ys, rank_c, ada_l)


def _moe(xu, hl, pt, ada_l, w1, w3, w2, *, layer, row0, is_ctx):
    B, _, n = pt.shape
    cap = EC_CAPACITY * n // N_EXPERTS
    cap_pad = -(-cap // SLOT) * SLOT
    nb = n // TOK
    rank, offs = _select(pt, cap=cap)
    offs = offs[:, :, :nb + 1]
    gt = min(n, MOE_TOK)
    ct = min(n, COMBINE_TOK)
    xs, gates = _gather(offs, hl, rank, pt, tb_tok=gt, tb0=row0 // gt, n=n, cap_pad=cap_pad, eg=GATHER_EXPERTS)
    mb = 2 if (B % 2 == 0 and cap_pad >= 1024) else (B if cap_pad < 1024 else 1)
    ys = _ffn(xs, gates, w1, w3, w2, layer=layer, mb=mb, tf=FFN_TF)
    rank_c = jnp.swapaxes(rank, 1, 2)
    return _combine(offs, xu, ys, rank_c, ada_l, tb_tok=ct, tb0=row0 // ct, n=n, is_ctx=is_ctx)


def _final_kernel(x_ref, g_ref, o_ref):
    x = x_ref[0]
    r = lax.rsqrt(jnp.mean(x * x, axis=-1, keepdims=True) + EPS)
    o_ref[0] = (x * r) * g_ref[...]


def _final_norm(xu, g, *, n):
    B = xu.shape[0]
    tm = MOE_TOK
    return pl.pallas_call(
        _final_kernel,
        out_shape=jax.ShapeDtypeStruct((B, n, D), F32),
        grid=(B, n // tm),
        in_specs=[pl.BlockSpec((1, tm, D), lambda b, t: (b, t, 0)),
                  pl.BlockSpec((1, D), lambda b, t: (0, 0))],
        out_specs=pl.BlockSpec((1, tm, D), lambda b, t: (b, t, 0)),
        compiler_params=_cp(("arbitrary", "arbitrary")),
        name="final_norm",
    )(xu, g)


def _rope_tables(n, ctx):
    rows = n // GRID_W
    t_row = jnp.repeat(jnp.arange(rows), GRID_W)
    t_col = jnp.tile(jnp.arange(GRID_W), rows)
    nf = DA_DIM // 4
    inv = ROPE_THETA ** (-jnp.arange(nf, dtype=F32) / nf)
    ar = t_row[:, None].astype(F32) * inv
    ac = t_col[:, None].astype(F32) * inv
    ang = jnp.concatenate([ar, ar, ac, ac], axis=-1)
    sign = jnp.where((jnp.arange(DA_DIM) % 16) < 8, -1.0, 1.0).astype(F32)
    cos = jnp.concatenate([jnp.cos(ang), jnp.ones((ctx, DA_DIM), F32)], axis=0)
    sin = jnp.concatenate([jnp.sin(ang) * sign, jnp.zeros((ctx, DA_DIM), F32)], axis=0)
    return jnp.tile(cos, (1, 128 // DA_DIM)), jnp.tile(sin, (1, 128 // DA_DIM))


def _pad_heads_cols(w):
    lead = w.shape[:-1]
    w = w.reshape(lead + (M_HEADS, M_DIM))
    w = jnp.pad(w, [(0, 0)] * len(lead) + [(0, 0), (0, M_PAD - M_DIM)])
    return w.reshape(lead + (MP_WIDTH,))


def _kv_tile(nt):
    for parts in range(1, nt // LANES + 1):
        if nt % parts == 0 and (nt // parts) % LANES == 0 and nt // parts <= KV_TILE_MAX:
            return nt // parts
    raise ValueError(nt)


def kernel(x, c, ctx, c_ctx, ada_w, ada_b, norm1_g, norm2_g, w_in, four_w, m_conv_w, m_conv_b, m_gate_b,
           m_norm_g, d_lam, d_norm_g, w_out, router_w, exp_w1, exp_w3, exp_w2, final_g):
    B, N, _ = x.shape
    CTX = ctx.shape[1]
    depth = w_in.shape[0]
    assert CTX == TOK and N % (FFT_N1 * TOK) == 0 and N % Q_TILE == 0 and B <= CTX_ROW
    NT = N + CTX
    PAD = -NT % MOE_TOK
    n_lat = N // TOK
    n2 = N // FFT_N1

    xu = jnp.concatenate([x, ctx, jnp.zeros((B, PAD, D), F32)], axis=1)
    cvecs = jnp.zeros((ADA_ROWS, D), F32).at[:B].set(c).at[CTX_ROW].set(c_ctx)
    ada = _adaln(cvecs, ada_w, ada_b).reshape(depth, ADA_ROWS, ADA_CHUNKS, D)
    cos_t, sin_t = _rope_tables(N, CTX + PAD)
    tabs = _fourier_tables(N, CTX)
    tk = _kv_tile(NT)
    tq = Q_TILE

    hl = jnp.zeros((B, NT + PAD, D), BF16)
    for layer in range(depth):
        ctx_out = layer < depth - 1
        lam_init = 0.8 - 0.6 * math.exp(-0.3 * layer)
        w = w_in[layer]
        wm = jnp.concatenate([w[:, OFF_F:OFF_DQ], w[:, OFF_DQ:OFF_MO], w[:, OFF_DK:OFF_DV]], axis=1).astype(BF16)
        wvt = jnp.concatenate([w[:, OFF_DV:OFF_MV], _pad_heads_cols(w[:, OFF_MO:OFF_MQ]),
                               _pad_heads_cols(w[:, OFF_MV:OFF_G])], axis=1).T.astype(BF16)
        wg = jnp.pad(w[:, OFF_G:], ((0, 0), (0, LANES - N_GATES)))
        wc = jnp.concatenate([_pad_heads_cols(w[:, OFF_MQ:OFF_MK]), _pad_heads_cols(w[:, OFF_MK:OFF_DK])],
                             axis=1).astype(BF16)
        gb = jnp.pad(m_gate_b[layer], (0, LANES - N_GATES)).reshape(1, LANES)
        cw = jnp.concatenate([_pad_heads_cols(m_conv_w[layer][:, :M_WIDTH]),
                              _pad_heads_cols(m_conv_w[layer][:, M_WIDTH:])], axis=1)
        cb = jnp.concatenate([_pad_heads_cols(m_conv_b[layer][:M_WIDTH]),
                              _pad_heads_cols(m_conv_b[layer][M_WIDTH:])]).reshape(1, 2 * MP_WIDTH)
        ada_l = ada[layer]

        y4, dq, dk, dvT, mo, mq, mk, mv, gl, glT = _inproj(
            xu, ada_l, norm1_g[layer].reshape(1, D), wm, wc, wvt, wg, gb, tabs["cs"], cos_t, sin_t, cw, cb,
            n_lat=n_lat, n2=n2)

        wblk = jnp.zeros((F_WIDTH, F_WIDTH), F32)
        for g in range(F_GROUPS):
            wblk = wblk.at[F_GDIM * g:F_GDIM * (g + 1), F_GDIM * g:F_GDIM * (g + 1)].set(four_w[layer, g])
        f_l, f_c = _fourier(y4, tabs, wblk.astype(BF16), n=N, ctx=CTX, with_ctx=ctx_out)

        dlam = d_lam[layer]
        g2 = d_norm_g[layer].reshape(DA_VDIM, 1)
        da_l = _attention(dq, dk, dvT, dlam, g2, lam_init=lam_init, tq=tq, q0=0, nq=N // tq,
                          tk=tk, k0=0, nk=NT // tk)

        hf, hb = _mlstm(mq, mk, mv, gl, glT, n_lat=n_lat)

        mg = _pad_heads_cols(m_norm_g[layer]).reshape(MP_WIDTH, 1)
        wol = w_out[layer]
        wo = wol[:F_WIDTH].astype(BF16)
        wod = jnp.concatenate([wol[F_WIDTH:F_WIDTH + DA_WIDTH],
                               jnp.pad(wol[F_WIDTH + DA_WIDTH:].reshape(M_HEADS, M_DIM, D),
                                       ((0, 0), (0, M_PAD - M_DIM), (0, 0))).reshape(MP_WIDTH, D)],
                              axis=0).astype(BF16)
        g2n = norm2_g[layer].reshape(1, D)
        wrp = jnp.pad(router_w[layer], ((0, 0), (0, LANES - N_EXPERTS)))
        xu, hl, pt_l = _outproj(xu, f_l, da_l, hf, hb, mo, ada_l, mg, wo, wod, g2n, wrp, hl,
                                t0=0, ntl=n_lat, is_ctx=False)
        if ctx_out:
            da_c = _attention(dq, dk, dvT, dlam, g2, lam_init=lam_init, tq=TOK, q0=n_lat, nq=1,
                              tk=TOK, k0=n_lat, nk=1)
            xu, hl, pt_c = _outproj(xu, f_c, da_c, hf, hb, mo, ada_l, mg, wo, wod, g2n, wrp, hl,
                                    t0=n_lat, ntl=1, is_ctx=True)

        xu = _moe(xu, hl, pt_l, ada_l, exp_w1, exp_w3, exp_w2, layer=layer, row0=0, is_ctx=False)
        if ctx_out:
            xu = _moe(xu, hl, pt_c, ada_l, exp_w1, exp_w3, exp_w2, layer=layer, row0=N, is_ctx=True)

    return _final_norm(xu, final_g.reshape(1, D), n=N)
```

```python
import functools
import itertools
import math

import numpy as np
import jax
import jax.numpy as jnp
from jax import lax
from jax.experimental import pallas as pl
from jax.experimental.pallas import tpu as pltpu

F32 = jnp.float32
BF16 = jnp.bfloat16
HI = lax.Precision.HIGHEST

D = 1024
EPS = 1e-6
GRID_W = 64
ROPE_THETA = 10000.0
F_GROUPS, F_GDIM = 4, 64
F_WIDTH = F_GROUPS * F_GDIM
DA_HEADS, DA_DIM = 6, 32
DA_VDIM = 2 * DA_DIM
DA_WIDTH = DA_HEADS * DA_VDIM
M_HEADS, M_DIM = 4, 96
M_WIDTH = M_HEADS * M_DIM
M_PAD = 128
MP_WIDTH = M_HEADS * M_PAD
N_GATES = 4 * M_HEADS
N_EXPERTS = 16
EC_CAPACITY = 2
D_FF = 2 * D
ADA_CHUNKS = 6
ADA_ROWS = 8
CTX_ROW = 4

LANES = 128
MXU_DIM = 256
V7X_VMEM_BYTES = 64 * 1024 * 1024

TOK = MXU_DIM
FFT_N1 = 16
SLOT = LANES
Q_TILE = 512
KV_TILE_MAX = 8448
MOE_TOK = 512
COMBINE_TOK = 512
FFN_TF = 512
GATHER_EXPERTS = 8
NEG = -1e30

OFF_F = 0
OFF_DQ = OFF_F + F_WIDTH
OFF_MO = OFF_DQ + 2 * DA_HEADS * DA_DIM
OFF_MQ = OFF_MO + M_WIDTH
OFF_MK = OFF_MQ + M_WIDTH
OFF_DK = OFF_MK + M_WIDTH
OFF_DV = OFF_DK + 2 * DA_HEADS * DA_DIM
OFF_MV = OFF_DV + DA_HEADS * DA_VDIM
OFF_G = OFF_MV + M_WIDTH

VMEM_LIMIT = V7X_VMEM_BYTES * 7 // 8


def _cp(sem, vmem=None):
    return pltpu.CompilerParams(dimension_semantics=sem, vmem_limit_bytes=vmem)


def _sigmoid(x):
    return 1.0 / (1.0 + jnp.exp(-x))


def _silu(x):
    return x * _sigmoid(x)


def _dot(a, b, precision=None):
    return jnp.dot(a, b, preferred_element_type=F32, precision=precision)


def _split(a):
    hi = a.astype(BF16)
    return hi, (a - hi.astype(F32)).astype(BF16)


def _dot3(a, b):
    a_hi, a_lo = a if isinstance(a, tuple) else _split(a)
    b_hi, b_lo = b if isinstance(b, tuple) else _split(b)
    return _dot(a_hi, b_hi) + _dot(a_hi, b_lo) + _dot(a_lo, b_hi)


def _split3(a):
    hi = a.astype(BF16)
    r = a - hi.astype(F32)
    mid = r.astype(BF16)
    return hi, mid, (r - mid.astype(F32)).astype(BF16)


def _dot_nt(a, b, precision=None):
    return lax.dot_general(a, b, (((1,), (1,)), ((), ())), preferred_element_type=F32,
                           precision=precision)


def _ada_kernel(c_ref, w_ref, b_ref, o_ref):
    c = c_ref[...]
    o_ref[0] = _dot(_silu(c), w_ref[0], HI) + b_ref[0]


def _adaln(cvecs, ada_w, ada_b):
    depth = ada_w.shape[0]
    tn = 1536
    return pl.pallas_call(
        _ada_kernel,
        out_shape=jax.ShapeDtypeStruct((depth, ADA_ROWS, ADA_CHUNKS * D), F32),
        grid=(depth, ADA_CHUNKS * D // tn),
        in_specs=[pl.BlockSpec((ADA_ROWS, D), lambda l, j: (0, 0)),
                  pl.BlockSpec((1, D, tn), lambda l, j: (l, 0, j)),
                  pl.BlockSpec((1, 1, tn), lambda l, j: (l, 0, j))],
        out_specs=pl.BlockSpec((1, 8, tn), lambda l, j: (l, 0, j)),
        compiler_params=_cp(("arbitrary", "arbitrary")),
        name="adaln",
    )(cvecs, ada_w, ada_b.reshape(depth, 1, ADA_CHUNKS * D))


def _inproj_kernel(x_ref, xp_ref, xn_ref, ada_ref, g_ref, wm_ref, wc_ref, wvt_ref, wg_ref, gb_ref, cs_ref,
                   cos_ref, sin_ref, cw_ref, cb_ref,
                   y_ref, dq_ref, dk_ref, dvt_ref, mo_ref, mq_ref, mk_ref, mv_ref, gl_ref, glt_ref, *, n_lat):
    b = pl.program_id(0)
    t = pl.program_id(1)
    n_tiles = pl.num_programs(1)
    is_ctx = t >= n_lat
    row = jnp.where(is_ctx, CTX_ROW, b)
    mod = ada_ref[row]
    sh, sc = mod[0:1], mod[1:2]

    xa = jnp.concatenate([xp_ref[0], x_ref[0], xn_ref[0]], axis=0)
    r = lax.rsqrt(jnp.mean(xa * xa, axis=-1, keepdims=True) + EPS)
    ha = (xa * r) * g_ref[...] * (1.0 + sc) + sh
    h = ha[8:8 + TOK]
    hb = h.astype(BF16)

    pm = _dot(hb, wm_ref[...])
    o = 0
    pf = pm[:, o:o + F_WIDTH]; o += F_WIDTH
    q = pm[:, o:o + DA_WIDTH]; o += DA_WIDTH
    k = pm[:, o:o + DA_WIDTH]; o += DA_WIDTH
    pt = _dot_nt(wvt_ref[...], hb)
    dvt_ref[0] = pt[:DA_WIDTH].astype(BF16)
    mo_ref[0] = pt[DA_WIDTH:DA_WIDTH + MP_WIDTH].astype(BF16)
    mv_ref[0] = pt[DA_WIDTH + MP_WIDTH:].astype(BF16)

    y_ref[0, 0] = _dot3(pf, cs_ref[...])

    cos = cos_ref[...]
    sin = sin_ref[...]
    lane = lax.broadcasted_iota(jnp.int32, (1, 128), 1)
    low = (lane % 16) < 8

    def rope(z):
        parts = []
        for c in range(DA_WIDTH // 128):
            zc = z[:, 128 * c:128 * (c + 1)]
            rot = jnp.where(low, pltpu.roll(zc, 120, 1), pltpu.roll(zc, 8, 1))
            parts.append(zc * cos + rot * sin)
        return jnp.concatenate(parts, axis=1)

    dq_ref[0] = (rope(q) * (DA_DIM ** -0.5 * math.log2(math.e))).astype(BF16)
    dk_ref[0] = rope(k).astype(BF16)

    gpre = _dot3(h, wg_ref[...]) + gb_ref[...]
    is_forget = (lax.broadcasted_iota(jnp.int32, (1, LANES), 1) % 8) >= 4
    logsig = jnp.minimum(gpre, 0.0) - jnp.log(1.0 + jnp.exp(-jnp.abs(gpre)))
    gl = jnp.where(is_forget, logsig, gpre)
    gl_ref[0] = gl
    glt_ref[0] = gl.T[:N_GATES]

    pc = _dot(ha.astype(BF16), wc_ref[...])
    first = (t == 0) | (t == n_lat)
    last = (t == n_lat - 1) | (t == n_tiles - 1)
    ridx = lax.broadcasted_iota(jnp.int32, (TOK + 16, 1), 0)
    pc = jnp.where(((ridx < 8) & first) | ((ridx >= TOK + 8) & last), 0.0, pc)
    cw = cw_ref[...]
    conv = cb_ref[...] + pc[7:7 + TOK] * cw[0:1] + pc[8:8 + TOK] * cw[1:2] + pc[9:9 + TOK] * cw[2:3]
    act = _silu(conv)
    mq_ref[0] = act[:, :MP_WIDTH].astype(BF16)
    mk_ref[0] = (act[:, MP_WIDTH:] * (M_DIM ** -0.5)).astype(BF16)


def _inproj(xu, ada_l, g1, wm, wc, wvt, wg, gb, cs, cos_t, sin_t, cw, cb, *, n_lat, n2):
    B, NT, _ = xu.shape
    nt = n_lat + 1
    rper = n2 // TOK
    tok3 = lambda w: pl.BlockSpec((1, TOK, w), lambda b, t: (b, t, 0))
    full = lambda a: pl.BlockSpec(a.shape, lambda b, t: (0,) * a.ndim)
    nb8 = NT // 8
    outs = [jax.ShapeDtypeStruct((B, 2 * FFT_N1, n2, 2 * F_WIDTH), F32)]
    nq = -(-NT // Q_TILE) * Q_TILE
    outs += [jax.ShapeDtypeStruct((B, nq, DA_WIDTH), BF16), jax.ShapeDtypeStruct((B, nt * TOK, DA_WIDTH), BF16)]
    outs += [jax.ShapeDtypeStruct((B, DA_WIDTH, nt * TOK), BF16)]
    trs = lambda r: pl.BlockSpec((1, r, TOK), lambda b, t: (b, 0, t))
    trp = jax.ShapeDtypeStruct((B, MP_WIDTH, NT), BF16)
    outs += [trp, jax.ShapeDtypeStruct((B, NT, MP_WIDTH), BF16), jax.ShapeDtypeStruct((B, NT, MP_WIDTH), BF16), trp]
    outs += [jax.ShapeDtypeStruct((B, NT, 128), F32), jax.ShapeDtypeStruct((B, N_GATES, nt * TOK), F32)]
    out_specs = [pl.BlockSpec((1, 1, TOK, 2 * F_WIDTH), lambda b, t: (b, t // rper, t % rper, 0))]
    out_specs += [tok3(DA_WIDTH)] * 2 + [trs(DA_WIDTH)]
    out_specs += [trs(MP_WIDTH), tok3(MP_WIDTH), tok3(MP_WIDTH), trs(MP_WIDTH)]
    out_specs += [tok3(128), pl.BlockSpec((1, N_GATES, TOK), lambda b, t: (b, 0, t))]
    return pl.pallas_call(
        functools.partial(_inproj_kernel, n_lat=n_lat),
        out_shape=outs,
        grid=(B, nt),
        in_specs=[tok3(D),
                  pl.BlockSpec((1, 8, D), lambda b, t: (b, jnp.maximum(t * (TOK // 8) - 1, 0), 0)),
                  pl.BlockSpec((1, 8, D), lambda b, t: (b, jnp.minimum((t + 1) * (TOK // 8), nb8 - 1), 0)),
                  full(ada_l), full(g1), full(wm), full(wc), full(wvt), full(wg), full(gb), full(cs),
                  pl.BlockSpec((TOK, LANES), lambda b, t: (t, 0)),
                  pl.BlockSpec((TOK, LANES), lambda b, t: (t, 0)),
                  full(cw), full(cb)],
        out_specs=out_specs,
        compiler_params=_cp(("arbitrary", "arbitrary"), VMEM_LIMIT),
        name="norm1_inproj",
    )(xu, xu, xu, ada_l, g1, wm, wc, wvt, wg, gb, cs, cos_t, sin_t, cw, cb)


def _fft1_kernel(y_ref, kc_ref, ks_ref, tc_ref, ts_ref, o_ref, *, groups):
    kc = _split(kc_ref[...])
    ks = _split(ks_ref[...])
    for g in range(groups):
        blk = _split(y_ref[0, :, 8 * g:8 * (g + 1), :].reshape(FFT_N1 * 8, 2 * F_WIDTH))
        p = _dot3(kc, blk)
        q = _dot3(ks, blk)
        ar = p[:, :F_WIDTH] - q[:, F_WIDTH:]
        ai = -p[:, F_WIDTH:] - q[:, :F_WIDTH]
        tc = tc_ref[128 * g:128 * (g + 1), :]
        ts = ts_ref[128 * g:128 * (g + 1), :]
        tc = jnp.concatenate([tc, tc], axis=1)
        ts = jnp.concatenate([ts, ts], axis=1)
        br = ar * tc + ai * ts
        bi = ai * tc - ar * ts
        o_ref[0, :, 8 * g:8 * (g + 1), :] = jnp.concatenate([br, bi], axis=1).reshape(FFT_N1, 8, 2 * F_WIDTH)


def _fft2_kernel(b_ref, c2_ref, s2_ref, wb_ref, perm_ref, o_ref, r_scr, *, n2):
    c2 = _split(c2_ref[...])
    s2 = _split(s2_ref[...])
    for i in range(8):
        blk = b_ref[0, i]
        xr = _dot3(c2, blk[:, :F_WIDTH]) + _dot3(s2, blk[:, F_WIDTH:])
        r_scr[i] = _dot(xr.astype(BF16), wb_ref[...]).astype(BF16)
    for t in range(n2 // 32):
        rows = jnp.concatenate([r_scr[i, 32 * t:32 * (t + 1), :] for i in range(8)], axis=0)
        o_ref[0, 32 * t:32 * (t + 1), :, :] = _dot(perm_ref[...], rows).reshape(32, 8, F_WIDTH)


def _fftc_kernel(y_ref, c_ref, s_ref, wb_ref, o_ref):
    y = y_ref[0, 0]
    z = _dot3(c_ref[...], y[:, :F_WIDTH]) - _dot3(s_ref[...], y[:, F_WIDTH:])
    o_ref[0] = _dot(z.astype(BF16), wb_ref[...])


def _fourier_tables(n, ctx):
    n1, n2 = FFT_N1, n // FFT_N1
    a = np.arange(n1)
    ang1 = 2 * np.pi * np.outer(a, a) / n1
    eye8 = np.eye(8)
    kc = np.kron(np.cos(ang1), eye8)
    ks = np.kron(np.sin(ang1), eye8)
    n2i = np.arange(n2).reshape(n2 // 8, 1, 8)
    k1 = np.arange(n1).reshape(1, n1, 1)
    angt = (2 * np.pi * n2i * k1 / n).reshape(-1, 1)
    tc = np.broadcast_to(np.cos(angt), (n2 // 8 * 128, 128))
    ts = np.broadcast_to(np.sin(angt), (n2 // 8 * 128, 128))
    b = np.arange(n2)
    ang2 = 2 * np.pi * np.outer(b, b) / n2
    c2 = np.cos(ang2) / math.sqrt(n)
    s2 = np.sin(ang2) / math.sqrt(n)
    perm = np.zeros((256, 256))
    for kk in range(8):
        for j in range(32):
            perm[j * 8 + kk, kk * 32 + j] = 1.0
    cc = np.arange(ctx)
    angc = 2 * np.pi * np.outer(cc, cc) / ctx
    cctx = np.cos(angc) / math.sqrt(ctx)
    sctx = np.sin(angc) / math.sqrt(ctx)
    ch = np.arange(F_GDIM)
    angch = 2 * np.pi * np.outer(ch, ch) / F_GDIM
    cs = np.concatenate([np.kron(np.eye(F_GROUPS), np.cos(angch)),
                         np.kron(np.eye(F_GROUPS), np.sin(angch))], axis=1) / math.sqrt(F_GDIM)
    f = lambda z: jnp.asarray(np.ascontiguousarray(z), dtype=F32)
    return dict(kc=f(kc), ks=f(ks), tc=f(tc), ts=f(ts), c2=f(c2), s2=f(s2), perm=f(perm).astype(BF16),
                cctx=f(cctx), sctx=f(sctx), cs=f(cs))


def _fourier(y4, tabs, wblk, *, n, ctx, with_ctx):
    B = y4.shape[0]
    n2 = n // FFT_N1
    groups = 4
    full = lambda a, nd: pl.BlockSpec(a.shape, lambda *i: (0,) * a.ndim)
    b4 = pl.pallas_call(
        functools.partial(_fft1_kernel, groups=groups),
        out_shape=jax.ShapeDtypeStruct((B, FFT_N1, n2, 2 * F_WIDTH), F32),
        grid=(B, n2 // (8 * groups)),
        in_specs=[pl.BlockSpec((1, FFT_N1, 8 * groups, 2 * F_WIDTH), lambda b, j: (b, 0, j, 0)),
                  full(tabs["kc"], 2), full(tabs["ks"], 2),
                  pl.BlockSpec((128 * groups, 128), lambda b, j: (j, 0)),
                  pl.BlockSpec((128 * groups, 128), lambda b, j: (j, 0))],
        out_specs=pl.BlockSpec((1, FFT_N1, 8 * groups, 2 * F_WIDTH), lambda b, j: (b, 0, j, 0)),
        compiler_params=_cp(("arbitrary", "arbitrary")),
        name="fourier_stage1",
    )(y4, tabs["kc"], tabs["ks"], tabs["tc"], tabs["ts"])
    f4 = pl.pallas_call(
        functools.partial(_fft2_kernel, n2=n2),
        out_shape=jax.ShapeDtypeStruct((B, n2, 16, F_WIDTH), F32),
        grid=(B, FFT_N1 // 8),
        in_specs=[pl.BlockSpec((1, 8, n2, 2 * F_WIDTH), lambda b, j: (b, j, 0, 0)),
                  full(tabs["c2"], 2), full(tabs["s2"], 2), full(wblk, 2), full(tabs["perm"], 2)],
        out_specs=pl.BlockSpec((1, n2, 8, F_WIDTH), lambda b, j: (b, 0, j, 0)),
        scratch_shapes=[pltpu.VMEM((8, n2, F_WIDTH), BF16)],
        compiler_params=_cp(("arbitrary", "arbitrary"), VMEM_LIMIT),
        name="fourier_stage2",
    )(b4, tabs["c2"], tabs["s2"], wblk, tabs["perm"])
    f_ctx = None
    if with_ctx:
        f_ctx = pl.pallas_call(
            _fftc_kernel,
            out_shape=jax.ShapeDtypeStruct((B, ctx, F_WIDTH), F32),
            grid=(B,),
            in_specs=[pl.BlockSpec((1, 1, TOK, 2 * F_WIDTH), lambda b: (b, FFT_N1, 0, 0)),
                      full(tabs["cctx"], 1), full(tabs["sctx"], 1), full(wblk, 1)],
            out_specs=pl.BlockSpec((1, ctx, F_WIDTH), lambda b: (b, 0, 0)),
            compiler_params=_cp(("arbitrary",)),
            name="fourier_ctx",
        )(y4, tabs["cctx"], tabs["sctx"], wblk)
    return f4.reshape(B, n, F_WIDTH), f_ctx


VROWS = DA_VDIM + 16


def _attn_kernel(q_ref, k_ref, vt_ref, dl_ref, g_ref, o_ref, m_scr, acc_scr, *, lam_init):
    kt = pl.program_id(3)
    nk = pl.num_programs(3)

    @pl.when(kt == 0)
    def _():
        m_scr[...] = jnp.full(m_scr.shape, NEG, F32)
        acc_scr[...] = jnp.zeros(acc_scr.shape, F32)

    q = q_ref[0]
    k = k_ref[0]
    vt = vt_ref[0]
    ones = jnp.ones((16, vt.shape[1]), BF16)
    lhs = [jnp.concatenate([vt[DA_VDIM * h:DA_VDIM * (h + 1)], ones], axis=0) for h in range(2)]
    lane = lax.broadcasted_iota(jnp.int32, (1, LANES), 1)
    zero = jnp.zeros((), BF16)

    def scores(j):
        return _dot_nt(k, jnp.where((lane // DA_DIM) == j, q, zero))

    st_next = scores(0)
    for j in range(4):
        st = st_next
        if j < 3:
            st_next = scores(j + 1)
        m_old = m_scr[j]
        m_new = jnp.maximum(m_old, jnp.max(st, axis=0, keepdims=True))
        alpha = jnp.exp2(m_old - m_new)
        pt = jnp.exp2(st - m_new).astype(BF16)
        acc_scr[j] = alpha * acc_scr[j] + _dot(lhs[j // 2], pt)
        m_scr[j] = m_new

    @pl.when(kt == nk - 1)
    def _():
        dl = dl_ref[...]
        lam = (jnp.exp(jnp.sum(dl[0:1] * dl[1:2], keepdims=True))
               - jnp.exp(jnp.sum(dl[2:3] * dl[3:4], keepdims=True)) + lam_init)
        outs = []
        for h in range(2):
            a0 = acc_scr[2 * h]
            a1 = acc_scr[2 * h + 1]
            o = (a0[:DA_VDIM] / a0[DA_VDIM:DA_VDIM + 1]
                 - lam * (a1[:DA_VDIM] / a1[DA_VDIM:DA_VDIM + 1]))
            r = lax.rsqrt(jnp.mean(o * o, axis=0, keepdims=True) + EPS)
            outs.append(((o * r) * g_ref[...]) * (1.0 - lam_init))
        o_ref[0] = jnp.concatenate(outs, axis=0).astype(BF16)


def _attention(dq, dk, dvT, dlam, gcol, *, lam_init, tq, q0, nq, tk, k0, nk):
    B = dq.shape[0]
    return pl.pallas_call(
        functools.partial(_attn_kernel, lam_init=lam_init),
        out_shape=jax.ShapeDtypeStruct((B, DA_WIDTH, nq * tq), BF16),
        grid=(B, DA_WIDTH // 128, nq, nk),
        in_specs=[pl.BlockSpec((1, tq, 128), lambda b, p, i, j: (b, q0 + i, p)),
                  pl.BlockSpec((1, tk, 128), lambda b, p, i, j: (b, k0 + j, p)),
                  pl.BlockSpec((1, 128, tk), lambda b, p, i, j: (b, p, k0 + j)),
                  pl.BlockSpec(dlam.shape, lambda b, p, i, j: (0, 0)),
                  pl.BlockSpec(gcol.shape, lambda b, p, i, j: (0, 0))],
        out_specs=pl.BlockSpec((1, 128, tq), lambda b, p, i, j: (b, p, i)),
        scratch_shapes=[pltpu.VMEM((4, 1, tq), F32), pltpu.VMEM((4, VROWS, tq), F32)],
        compiler_params=_cp(("arbitrary",) * 4, VMEM_LIMIT),
        name="diff_attention",
    )(dq, dk, dvT, dlam, gcol)


def _mlstm_kernel(qf_ref, kf_ref, vf_ref, gcf_ref, grf_ref, qb_ref, kb_ref, vb_ref, gcb_ref, grb_ref,
                  hf_ref, hb_ref, c_scr, m_scr):
    t = pl.program_id(1)

    @pl.when(t == 0)
    def _():
        c_scr[...] = jnp.zeros(c_scr.shape, F32)
        m_scr[...] = jnp.zeros(m_scr.shape, F32)

    L = TOK
    si = lax.broadcasted_iota(jnp.int32, (L, L), 0)
    li = lax.broadcasted_iota(jnp.int32, (L, L), 1)
    dirs = ((qf_ref, kf_ref, vf_ref, gcf_ref, grf_ref, hf_ref, si <= li, li <= si, L - 1),
            (qb_ref, kb_ref, vb_ref, gcb_ref, grb_ref, hb_ref, si >= li, li >= si, 0))
    ones = jnp.ones((16, L), F32)
    for smp, (d, (q_ref, k_ref, vt_ref, gc_ref, gr_ref, h_ref, seen, seen_t, last)) in itertools.product(
            range(qf_ref.shape[0]), list(enumerate(dirs))):
        gc = gc_ref[smp]
        gr = gr_ref[smp]
        seen_b = jnp.where(seen, 1.0, 0.0).astype(BF16)
        seen_tb = jnp.where(seen_t, 1.0, 0.0).astype(BF16)
        bcols = sum(_dot(seen_tb, piece) for piece in _split3(gc))
        brows = sum(_dot(piece, seen_b) for piece in _split3(gr))
        for hd in range(M_HEADS):
            idx = (smp * 2 + d) * M_HEADS + hd
            ji = d * 8 + hd
            jf = d * 8 + 4 + hd
            sl = slice(M_PAD * hd, M_PAD * (hd + 1))
            q = q_ref[smp, :, sl]
            k = k_ref[smp, :, sl]
            vt = vt_ref[smp, sl, :]
            b_row = brows[jf:jf + 1, :]
            cs = gc[:, ji:ji + 1] - bcols[:, jf:jf + 1]
            li_row = gr[ji:ji + 1, :]
            m_old = m_scr[idx][0:1, 0:1]
            c_old = c_scr[idx]

            dlog = jnp.where(seen, b_row + cs, NEG)
            inter = b_row + m_old
            m_t = jnp.maximum(inter, jnp.max(dlog, axis=0, keepdims=True))
            w_inter = jnp.exp(inter - m_t)
            st = _dot_nt(k, q) * jnp.exp(dlog - m_t)
            cq = _dot_nt(c_old.astype(BF16), q)
            num = w_inter * cq[:M_PAD] + _dot(vt, st.astype(BF16))
            den = w_inter * cq[M_PAD:M_PAD + 1] + jnp.sum(st, axis=0, keepdims=True)
            h_ref[smp, sl, :] = num / jnp.maximum(jnp.abs(den), jnp.exp(-m_t))

            total = b_row[:, last:last + 1]
            wlog = total - b_row + li_row
            m_new = jnp.maximum(total + m_old, jnp.max(wlog, axis=1, keepdims=True))
            decay = jnp.exp(total + m_old - m_new)
            w = jnp.exp(wlog - m_new)
            vw = jnp.concatenate([vt.astype(F32) * w, ones * w], axis=0).astype(BF16)
            c_scr[idx] = decay * c_old + _dot(vw, k)
            m_scr[idx] = jnp.broadcast_to(m_new, (8, 128))


def _mlstm(mq, mk, mvT, gl, glT, *, n_lat):
    B, NT, _ = mq.shape
    nt = n_lat + 1
    fwd = lambda t: jnp.where(t == 0, n_lat, t - 1)
    bwd = lambda t: jnp.where(t == 0, n_lat, n_lat - t)
    ns = 2 if B % 2 == 0 else 1
    tok = lambda w, f: pl.BlockSpec((ns, TOK, w), lambda b, t: (b, f(t), 0))
    lanes = lambda r, f: pl.BlockSpec((ns, r, TOK), lambda b, t: (b, 0, f(t)))
    ins, specs = [], []
    for f in (fwd, bwd):
        ins += [mq, mk, mvT, gl, glT]
        specs += [tok(MP_WIDTH, f)] * 2 + [lanes(MP_WIDTH, f), tok(128, f), lanes(N_GATES, f)]
    return pl.pallas_call(
        _mlstm_kernel,
        out_shape=[jax.ShapeDtypeStruct((B, MP_WIDTH, NT), F32)] * 2,
        grid=(B // ns, nt),
        in_specs=specs,
        out_specs=[lanes(MP_WIDTH, fwd), lanes(MP_WIDTH, bwd)],
        scratch_shapes=[pltpu.VMEM((ns * 2 * M_HEADS, M_PAD + 16, M_PAD), F32),
                        pltpu.VMEM((ns * 2 * M_HEADS, 8, 128), F32)],
        compiler_params=_cp(("arbitrary", "arbitrary"), VMEM_LIMIT),
        name="mlstm",
    )(*ins)


def _outproj_kernel(x_ref, f_ref, dat_ref, hf_ref, hb_ref, mo_ref, ada_ref, mg_ref, wo_ref, wod_ref, g2_ref, wr_ref,
                    xo_ref, hl_ref, pt_ref, *, is_ctx):
    b = pl.program_id(0)
    mod = ada_ref[CTX_ROW if is_ctx else b]
    gt1, sh2, sc2 = mod[2:3], mod[3:4], mod[4:5]
    mg = mg_ref[...]
    for s in range(x_ref.shape[1] // TOK):
        tk = slice(TOK * s, TOK * (s + 1))
        hs = hf_ref[0, :, tk] + hb_ref[0, :, tk]
        og = mo_ref[0, :, tk].astype(F32)
        parts = [dat_ref[0, :, tk]]
        for hd in range(M_HEADS):
            sl = slice(M_PAD * hd, M_PAD * (hd + 1))
            hh = hs[sl]
            r = lax.rsqrt(jnp.sum(hh * hh, axis=0, keepdims=True) * (1.0 / M_DIM) + EPS)
            parts.append((((hh * r) * mg[sl]) * _sigmoid(og[sl])).astype(BF16))
        mix_t = jnp.concatenate(parts, axis=0)
        upd = _dot(f_ref[0, tk].astype(BF16), wo_ref[...]) + lax.dot_general(
            mix_t, wod_ref[...], (((0,), (0,)), ((), ())), preferred_element_type=F32)
        xn = x_ref[0, tk] + gt1 * upd
        xo_ref[0, tk] = xn
        r = lax.rsqrt(jnp.mean(xn * xn, axis=-1, keepdims=True) + EPS)
        h2 = (xn * r) * g2_ref[...] * (1.0 + sc2) + sh2
        hl_ref[0, tk] = h2.astype(BF16)
        lt = _dot3(h2, wr_ref[...]).T[:N_EXPERTS]
        ex = jnp.exp(lt - jnp.max(lt, axis=0, keepdims=True))
        pt_ref[0, :, tk] = ex / jnp.sum(ex, axis=0, keepdims=True)


def _outproj_kernel_aliased(x_ref, f_ref, dat_ref, hf_ref, hb_ref, mo_ref, ada_ref, mg_ref, wo_ref, wod_ref,
                            g2_ref, wr_ref, hlp_ref, xo_ref, hl_ref, pt_ref, *, is_ctx):
    del hlp_ref
    _outproj_kernel(x_ref, f_ref, dat_ref, hf_ref, hb_ref, mo_ref, ada_ref, mg_ref, wo_ref, wod_ref, g2_ref,
                    wr_ref, xo_ref, hl_ref, pt_ref, is_ctx=is_ctx)


def _outproj(xu, f, daT, hf, hb, mo, ada_l, mg, wo, wod, g2, wrp, hl_prev, *, t0, ntl, is_ctx):
    B, NT, _ = xu.shape
    n = ntl * TOK
    tile = 2 * TOK if n % (2 * TOK) == 0 else TOK
    o = t0 * TOK // tile
    tok = lambda w: pl.BlockSpec((1, tile, w), lambda b, t: (b, o + t, 0))
    trs = lambda r: pl.BlockSpec((1, r, tile), lambda b, t: (b, 0, o + t))
    loc = lambda w: pl.BlockSpec((1, tile, w), lambda b, t: (b, t, 0))
    full = lambda a: pl.BlockSpec(a.shape, lambda b, t: (0,) * a.ndim)
    return pl.pallas_call(
        functools.partial(_outproj_kernel_aliased, is_ctx=is_ctx),
        out_shape=[jax.ShapeDtypeStruct(xu.shape, F32), jax.ShapeDtypeStruct((B, NT, D), BF16),
                   jax.ShapeDtypeStruct((B, N_EXPERTS, n), F32)],
        grid=(B, n // tile),
        in_specs=[tok(D), loc(F_WIDTH), pl.BlockSpec((1, DA_WIDTH, tile), lambda b, t: (b, 0, t)),
                  trs(MP_WIDTH), trs(MP_WIDTH), trs(MP_WIDTH),
                  full(ada_l), full(mg), full(wo), full(wod), full(g2), full(wrp),
                  pl.BlockSpec(memory_space=pl.ANY)],
        out_specs=[tok(D), tok(D), pl.BlockSpec((1, N_EXPERTS, tile), lambda b, t: (b, 0, t))],
        input_output_aliases={0: 0, 12: 1},
        compiler_params=_cp(("arbitrary", "arbitrary"), VMEM_LIMIT),
        name="outproj_norm2_router",
    )(xu, f, daT, hf, hb, mo, ada_l, mg, wo, wod, g2, wrp, hl_prev)


def _select_kernel(p_ref, rank_ref, offs_ref, *, n, cap):
    p = p_ref[0]
    xi = pltpu.bitcast(p, jnp.int32)

    def body(i, lo):
        cand = lo | jnp.left_shift(jnp.int32(1), 30 - i)
        cnt = jnp.sum(jnp.where(xi >= cand, 1.0, 0.0), axis=1, keepdims=True)
        return jnp.where(cnt >= cap, cand, lo)

    thr = lax.fori_loop(0, 31, body, jnp.zeros((N_EXPERTS, 1), jnp.int32))
    nb = n // TOK
    rows = lax.broadcasted_iota(jnp.int32, (n, 128), 0)
    cols = lax.broadcasted_iota(jnp.int32, (n, 128), 1)
    blk_ind = jnp.where((rows // TOK) == cols, 1.0, 0.0).astype(BF16)
    u128 = jnp.where(lax.broadcasted_iota(jnp.int32, (128, 128), 0)
                     < lax.broadcasted_iota(jnp.int32, (128, 128), 1), 1.0, 0.0).astype(BF16)
    utok = jnp.where(lax.broadcasted_iota(jnp.int32, (TOK, TOK), 0)
                     < lax.broadcasted_iota(jnp.int32, (TOK, TOK), 1), 1.0, 0.0).astype(BF16)

    def prefix(mf):
        mb = mf.astype(BF16)
        counts = _dot(mb, blk_ind)
        offs = _dot(counts.astype(BF16), u128)
        pieces = [_dot(mb[:, TOK * j:TOK * (j + 1)], utok) + offs[:, j:j + 1] for j in range(nb)]
        return (jnp.concatenate(pieces, axis=1) if nb > 1 else pieces[0]), offs

    gt = xi > thr
    eq = xi == thr
    need = cap - jnp.sum(jnp.where(gt, 1.0, 0.0), axis=1, keepdims=True)
    rank_eq, _ = prefix(jnp.where(eq, 1.0, 0.0))
    sel = gt | (eq & (rank_eq < need))
    rank, offs = prefix(jnp.where(sel, 1.0, 0.0))
    rank_ref[0] = jnp.where(sel, rank, -1.0)
    offs_ref[0] = offs.astype(jnp.int32)


def _select(pt, *, cap):
    B, _, n = pt.shape
    return pl.pallas_call(
        functools.partial(_select_kernel, n=n, cap=cap),
        out_shape=[jax.ShapeDtypeStruct((B, N_EXPERTS, n), F32),
                   jax.ShapeDtypeStruct((B, N_EXPERTS, 128), jnp.int32)],
        grid=(B,),
        in_specs=[pl.BlockSpec((1, N_EXPERTS, n), lambda b: (b, 0, 0))],
        out_specs=[pl.BlockSpec((1, N_EXPERTS, n), lambda b: (b, 0, 0)),
                   pl.BlockSpec((1, N_EXPERTS, 128), lambda b: (b, 0, 0))],
        compiler_params=_cp(("arbitrary",), VMEM_LIMIT),
        name="expert_choice_select",
    )(pt)


def _gather_kernel(offs_ref, h_ref, rank_ref, prob_ref, o_ref, gate_ref, *, eg, per):
    b, g, tb = pl.program_id(0), pl.program_id(1), pl.program_id(2)

    @pl.when(tb == 0)
    def _():
        o_ref[...] = jnp.zeros(o_ref.shape, BF16)
        gate_ref[...] = jnp.zeros(gate_ref.shape, F32)

    cap_pad = o_ref.shape[2]
    half = SLOT // 2
    slot = lax.broadcasted_iota(jnp.int32, (SLOT, TOK), 0).astype(F32)

    def add_rows(i, r, p, h, base, start=None):
        hit = r == slot + base.astype(F32)
        if start is not None:
            hit = hit & (r >= start.astype(F32))
        rows = _dot(jnp.where(hit, 1.0, 0.0).astype(BF16), h).astype(BF16)
        o_ref[0, i, pl.ds(base, SLOT), :] = o_ref[0, i, pl.ds(base, SLOT), :] + rows
        gate_ref[0, i, pl.ds(base, SLOT), :] = (gate_ref[0, i, pl.ds(base, SLOT), :]
                                                + jnp.sum(jnp.where(hit, p, 0.0), axis=1, keepdims=True))

    def operands(s, i):
        tk = slice(TOK * s, TOK * (s + 1))
        e = g * eg + i
        return rank_ref[0, pl.ds(e, 1), tk], prob_ref[0, pl.ds(e, 1), tk], h_ref[0, tk, :]

    ends, his = {}, {}
    for s in range(per):
        for i in range(eg):
            e = g * eg + i
            lo = offs_ref[b, e, tb * per + s]
            his[s, i] = offs_ref[b, e, tb * per + s + 1]
            base = pl.multiple_of(jnp.minimum((lo // half) * half, cap_pad - SLOT), half)
            add_rows(i, *operands(s, i), base)
            ends[s, i] = base + SLOT

    for s in range(per):
        for i in range(eg):
            @pl.when(his[s, i] > ends[s, i])
            def _(s=s, i=i):
                def body(t, carry):
                    start = ends[s, i] + t * SLOT
                    base = pl.multiple_of(jnp.minimum(start, cap_pad - SLOT), half)
                    add_rows(i, *operands(s, i), base, start)
                    return carry

                lax.fori_loop(0, (his[s, i] - ends[s, i] + SLOT - 1) // SLOT, body, 0)


def _gather(offs, hl, rank, pt, *, tb_tok, tb0, n, cap_pad, eg):
    B = hl.shape[0]
    per = tb_tok // TOK
    return pl.pallas_call(
        functools.partial(_gather_kernel, eg=eg, per=per),
        out_shape=[jax.ShapeDtypeStruct((B, N_EXPERTS, cap_pad, D), BF16),
                   jax.ShapeDtypeStruct((B, N_EXPERTS, cap_pad, 1), F32)],
        grid_spec=pltpu.PrefetchScalarGridSpec(
            num_scalar_prefetch=1,
            grid=(B, N_EXPERTS // eg, n // tb_tok),
            in_specs=[pl.BlockSpec((1, tb_tok, D), lambda b, g, t, o: (b, tb0 + t, 0)),
                      pl.BlockSpec((1, N_EXPERTS, tb_tok), lambda b, g, t, o: (b, 0, t)),
                      pl.BlockSpec((1, N_EXPERTS, tb_tok), lambda b, g, t, o: (b, 0, t))],
            out_specs=[pl.BlockSpec((1, eg, cap_pad, D), lambda b, g, t, o: (b, g, 0, 0)),
                       pl.BlockSpec((1, eg, cap_pad, 1), lambda b, g, t, o: (b, g, 0, 0))]),
        compiler_params=_cp(("arbitrary",) * 3, VMEM_LIMIT),
        name="expert_gather",
    )(offs, hl, rank, pt)


FFN_ROWS = 1024


def _ffn_kernel(x_ref, gate_ref, w1_ref, w3_ref, w2_ref, y_ref, acc_ref):
    f = pl.program_id(2)

    @pl.when(f == 0)
    def _():
        acc_ref[...] = jnp.zeros(acc_ref.shape, F32)

    w1 = w1_ref[0, 0].astype(BF16)
    w3 = w3_ref[0, 0].astype(BF16)
    w2 = w2_ref[0, 0].astype(BF16)
    mb, _, cap_pad, _ = x_ref.shape
    rows = min(FFN_ROWS, cap_pad)
    for i in range(mb):
        for r in range(0, cap_pad, rows):
            x = x_ref[i, 0, r:r + rows, :]
            hid = (_silu(_dot(x, w1)) * _dot(x, w3)).astype(BF16)
            acc_ref[i * cap_pad + r:i * cap_pad + r + rows, :] += _dot(hid, w2)

    @pl.when(f == pl.num_programs(2) - 1)
    def _():
        gate = gate_ref[...].reshape(-1, 1)
        y_ref[...] = (acc_ref[...] * gate).astype(BF16).reshape(y_ref.shape)


def _ffn(xs, gates, w1, w3, w2, *, layer, mb, tf):
    B, E, cap_pad, _ = xs.shape
    return pl.pallas_call(
        _ffn_kernel,
        out_shape=jax.ShapeDtypeStruct(xs.shape, BF16),
        grid=(E, B // mb, D_FF // tf),
        in_specs=[pl.BlockSpec((mb, 1, cap_pad, D), lambda e, m, f: (m, e, 0, 0)),
                  pl.BlockSpec((mb, 1, cap_pad, 1), lambda e, m, f: (m, e, 0, 0)),
                  pl.BlockSpec((1, 1, D, tf), lambda e, m, f: (layer, e, 0, f)),
                  pl.BlockSpec((1, 1, D, tf), lambda e, m, f: (layer, e, 0, f)),
                  pl.BlockSpec((1, 1, tf, D), lambda e, m, f: (layer, e, f, 0))],
        out_specs=pl.BlockSpec((mb, 1, cap_pad, D), lambda e, m, f: (m, e, 0, 0)),
        scratch_shapes=[pltpu.VMEM((mb * cap_pad, D), F32)],
        compiler_params=_cp(("arbitrary",) * 3, VMEM_LIMIT),
        name="expert_ffn",
    )(xs, gates, w1, w3, w2)


CCOL = 512


def _combine_kernel(offs_ref, x_ref, y_ref, rankc_ref, ada_ref, o_ref, tot_scr, *, per, is_ctx):
    b, tb = pl.program_id(0), pl.program_id(2)
    gt2 = ada_ref[CTX_ROW if is_ctx else b][5:6]
    rc_all = rankc_ref[0]
    cap_pad = y_ref.shape[2]
    half = SLOT // 2
    slot = lax.broadcasted_iota(jnp.int32, (1, SLOT), 1).astype(F32)

    ends, his = {}, {}
    for s in range(per):
        tk = slice(TOK * s, TOK * (s + 1))
        total = jnp.zeros((TOK, tot_scr.shape[1]), F32)
        for e0 in range(0, N_EXPERTS, 2):
            hots, rows = [], []
            for e in (e0, e0 + 1):
                lo = offs_ref[b, e, tb * per + s]
                his[s, e] = offs_ref[b, e, tb * per + s + 1]
                base = pl.multiple_of(jnp.minimum((lo // half) * half, cap_pad - SLOT), half)
                ends[s, e] = base + SLOT
                hots.append(jnp.where(rc_all[tk, e:e + 1] == slot + base.astype(F32), 1.0, 0.0).astype(BF16))
                rows.append(y_ref[0, e, pl.ds(base, SLOT), :])
            total = total + _dot(jnp.concatenate(hots, axis=1), jnp.concatenate(rows, axis=0))
        tot_scr[tk] = total

    for s in range(per):
        tk = slice(TOK * s, TOK * (s + 1))
        for e in range(N_EXPERTS):
            @pl.when(his[s, e] > ends[s, e])
            def _(s=s, e=e, tk=tk):
                rc = rc_all[tk, e:e + 1]

                def body(t, carry):
                    start = ends[s, e] + t * SLOT
                    base = pl.multiple_of(jnp.minimum(start, cap_pad - SLOT), half)
                    hit = (rc == slot + base.astype(F32)) & (rc >= start.astype(F32))
                    tot_scr[tk] += _dot(jnp.where(hit, 1.0, 0.0).astype(BF16), y_ref[0, e, pl.ds(base, SLOT), :])
                    return carry

                lax.fori_loop(0, (his[s, e] - ends[s, e] + SLOT - 1) // SLOT, body, 0)

    o_ref[0] = x_ref[0] + gt2 * tot_scr[...]


def _combine(offs, xu, ys, rank_c, ada_l, *, tb_tok, tb0, n, is_ctx):
    B = xu.shape[0]
    cap_pad = ys.shape[2]
    per = tb_tok // TOK
    return pl.pallas_call(
        functools.partial(_combine_kernel, per=per, is_ctx=is_ctx),
        out_shape=jax.ShapeDtypeStruct(xu.shape, F32),
        grid_spec=pltpu.PrefetchScalarGridSpec(
            num_scalar_prefetch=1,
            grid=(B, D // CCOL, n // tb_tok),
            in_specs=[pl.BlockSpec((1, tb_tok, CCOL), lambda b, c, t, o: (b, tb0 + t, c)),
                      pl.BlockSpec((1, N_EXPERTS, cap_pad, CCOL), lambda b, c, t, o: (b, 0, 0, c),
                                   pipeline_mode=pl.Buffered(1)),
                      pl.BlockSpec((1, tb_tok, N_EXPERTS), lambda b, c, t, o: (b, t, 0)),
                      pl.BlockSpec((ADA_ROWS, ADA_CHUNKS, CCOL), lambda b, c, t, o: (0, 0, c))],
            out_specs=pl.BlockSpec((1, tb_tok, CCOL), lambda b, c, t, o: (b, tb0 + t, c)),
            scratch_shapes=[pltpu.VMEM((tb_tok, CCOL), F32)]),
        input_output_aliases={1: 0},
        compiler_params=_cp(("arbitrary",) * 3, VMEM_LIMIT),
        name="expert_combine",
    )(offs, xu, ys, rank_c, ada_l)


def _moe(xu, hl, pt, ada_l, w1, w3, w2, *, layer, row0, is_ctx):
    B, _, n = pt.shape
    cap = EC_CAPACITY * n // N_EXPERTS
    cap_pad = -(-cap // SLOT) * SLOT
    nb = n // TOK
    rank, offs = _select(pt, cap=cap)
    offs = offs[:, :, :nb + 1]
    gt = min(n, MOE_TOK)
    ct = min(n, COMBINE_TOK)
    xs, gates = _gather(offs, hl, rank, pt, tb_tok=gt, tb0=row0 // gt, n=n, cap_pad=cap_pad, eg=GATHER_EXPERTS)
    mb = 2 if (B % 2 == 0 and cap_pad >= 1024) else (B if cap_pad < 1024 else 1)
    ys = _ffn(xs, gates, w1, w3, w2, layer=layer, mb=mb, tf=FFN_TF)
    rank_c = jnp.swapaxes(rank, 1, 2)
    return _combine(offs, xu, ys, rank_c, ada_l, tb_tok=ct, tb0=row0 // ct, n=n, is_ctx=is_ctx)


def _final_kernel(x_ref, g_ref, o_ref):
    x = x_ref[0]
    r = lax.rsqrt(jnp.mean(x * x, axis=-1, keepdims=True) + EPS)
    o_ref[0] = (x * r) * g_ref[...]


def _final_norm(xu, g, *, n):
    B = xu.shape[0]
    tm = MOE_TOK
    return pl.pallas_call(
        _final_kernel,
        out_shape=jax.ShapeDtypeStruct((B, n, D), F32),
        grid=(B, n // tm),
        in_specs=[pl.BlockSpec((1, tm, D), lambda b, t: (b, t, 0)),
                  pl.BlockSpec((1, D), lambda b, t: (0, 0))],
        out_specs=pl.BlockSpec((1, tm, D), lambda b, t: (b, t, 0)),
        compiler_params=_cp(("arbitrary", "arbitrary")),
        name="final_norm",
    )(xu, g)


def _rope_tables(n, ctx):
    rows = n // GRID_W
    t_row = jnp.repeat(jnp.arange(rows), GRID_W)
    t_col = jnp.tile(jnp.arange(GRID_W), rows)
    nf = DA_DIM // 4
    inv = ROPE_THETA ** (-jnp.arange(nf, dtype=F32) / nf)
    ar = t_row[:, None].astype(F32) * inv
    ac = t_col[:, None].astype(F32) * inv
    ang = jnp.concatenate([ar, ar, ac, ac], axis=-1)
    sign = jnp.where((jnp.arange(DA_DIM) % 16) < 8, -1.0, 1.0).astype(F32)
    cos = jnp.concatenate([jnp.cos(ang), jnp.ones((ctx, DA_DIM), F32)], axis=0)
    sin = jnp.concatenate([jnp.sin(ang) * sign, jnp.zeros((ctx, DA_DIM), F32)], axis=0)
    return jnp.tile(cos, (1, 128 // DA_DIM)), jnp.tile(sin, (1, 128 // DA_DIM))


def _pad_heads_cols(w):
    lead = w.shape[:-1]
    w = w.reshape(lead + (M_HEADS, M_DIM))
    w = jnp.pad(w, [(0, 0)] * len(lead) + [(0, 0), (0, M_PAD - M_DIM)])
    return w.reshape(lead + (MP_WIDTH,))


def _kv_tile(nt):
    for parts in range(1, nt // LANES + 1):
        if nt % parts == 0 and (nt // parts) % LANES == 0 and nt // parts <= KV_TILE_MAX:
            return nt // parts
    raise ValueError(nt)


def kernel(x, c, ctx, c_ctx, ada_w, ada_b, norm1_g, norm2_g, w_in, four_w, m_conv_w, m_conv_b, m_gate_b,
           m_norm_g, d_lam, d_norm_g, w_out, router_w, exp_w1, exp_w3, exp_w2, final_g):
    B, N, _ = x.shape
    CTX = ctx.shape[1]
    depth = w_in.shape[0]
    assert CTX == TOK and N % (FFT_N1 * TOK) == 0 and N % Q_TILE == 0 and B <= CTX_ROW
    NT = N + CTX
    PAD = -NT % MOE_TOK
    n_lat = N // TOK
    n2 = N // FFT_N1

    xu = jnp.concatenate([x, ctx, jnp.zeros((B, PAD, D), F32)], axis=1)
    cvecs = jnp.zeros((ADA_ROWS, D), F32).at[:B].set(c).at[CTX_ROW].set(c_ctx)
    ada = _adaln(cvecs, ada_w, ada_b).reshape(depth, ADA_ROWS, ADA_CHUNKS, D)
    cos_t, sin_t = _rope_tables(N, CTX + PAD)
    tabs = _fourier_tables(N, CTX)
    tk = _kv_tile(NT)
    tq = Q_TILE

    hl = jnp.zeros((B, NT + PAD, D), BF16)
    for layer in range(depth):
        ctx_out = layer < depth - 1
        lam_init = 0.8 - 0.6 * math.exp(-0.3 * layer)
        w = w_in[layer]
        wm = jnp.concatenate([w[:, OFF_F:OFF_DQ], w[:, OFF_DQ:OFF_MO], w[:, OFF_DK:OFF_DV]], axis=1).astype(BF16)
        wvt = jnp.concatenate([w[:, OFF_DV:OFF_MV], _pad_heads_cols(w[:, OFF_MO:OFF_MQ]),
                               _pad_heads_cols(w[:, OFF_MV:OFF_G])], axis=1).T.astype(BF16)
        wg = jnp.pad(w[:, OFF_G:], ((0, 0), (0, LANES - N_GATES)))
        wc = jnp.concatenate([_pad_heads_cols(w[:, OFF_MQ:OFF_MK]), _pad_heads_cols(w[:, OFF_MK:OFF_DK])],
                             axis=1).astype(BF16)
        gb = jnp.pad(m_gate_b[layer], (0, LANES - N_GATES)).reshape(1, LANES)
        cw = jnp.concatenate([_pad_heads_cols(m_conv_w[layer][:, :M_WIDTH]),
                              _pad_heads_cols(m_conv_w[layer][:, M_WIDTH:])], axis=1)
        cb = jnp.concatenate([_pad_heads_cols(m_conv_b[layer][:M_WIDTH]),
                              _pad_heads_cols(m_conv_b[layer][M_WIDTH:])]).reshape(1, 2 * MP_WIDTH)
        ada_l = ada[layer]

        y4, dq, dk, dvT, mo, mq, mk, mv, gl, glT = _inproj(
            xu, ada_l, norm1_g[layer].reshape(1, D), wm, wc, wvt, wg, gb, tabs["cs"], cos_t, sin_t, cw, cb,
            n_lat=n_lat, n2=n2)

        wblk = jnp.zeros((F_WIDTH, F_WIDTH), F32)
        for g in range(F_GROUPS):
            wblk = wblk.at[F_GDIM * g:F_GDIM * (g + 1), F_GDIM * g:F_GDIM * (g + 1)].set(four_w[layer, g])
        f_l, f_c = _fourier(y4, tabs, wblk.astype(BF16), n=N, ctx=CTX, with_ctx=ctx_out)

        dlam = d_lam[layer]
        g2 = d_norm_g[layer].reshape(DA_VDIM, 1)
        da_l = _attention(dq, dk, dvT, dlam, g2, lam_init=lam_init, tq=tq, q0=0, nq=N // tq,
                          tk=tk, k0=0, nk=NT // tk)

        hf, hb = _mlstm(mq, mk, mv, gl, glT, n_lat=n_lat)

        mg = _pad_heads_cols(m_norm_g[layer]).reshape(MP_WIDTH, 1)
        wol = w_out[layer]
        wo = wol[:F_WIDTH].astype(BF16)
        wod = jnp.concatenate([wol[F_WIDTH:F_WIDTH + DA_WIDTH],
                               jnp.pad(wol[F_WIDTH + DA_WIDTH:].reshape(M_HEADS, M_DIM, D),
                                       ((0, 0), (0, M_PAD - M_DIM), (0, 0))).reshape(MP_WIDTH, D)],
                              axis=0).astype(BF16)
        g2n = norm2_g[layer].reshape(1, D)
        wrp = jnp.pad(router_w[layer], ((0, 0), (0, LANES - N_EXPERTS)))
        xu, hl, pt_l = _outproj(xu, f_l, da_l, hf, hb, mo, ada_l, mg, wo, wod, g2n, wrp, hl,
                                t0=0, ntl=n_lat, is_ctx=False)
        if ctx_out:
            da_c = _attention(dq, dk, dvT, dlam, g2, lam_init=lam_init, tq=TOK, q0=n_lat, nq=1,
                              tk=TOK, k0=n_lat, nk=1)
            xu, hl, pt_c = _outproj(xu, f_c, da_c, hf, hb, mo, ada_l, mg, wo, wod, g2n, wrp, hl,
                                    t0=n_lat, ntl=1, is_ctx=True)

        xu = _moe(xu, hl, pt_l, ada_l, exp_w1, exp_w3, exp_w2, layer=layer, row0=0, is_ctx=False)
        if ctx_out:
            xu = _moe(xu, hl, pt_c, ada_l, exp_w1, exp_w3, exp_w2, layer=layer, row0=N, is_ctx=True)

    return _final_norm(xu, final_g.reshape(1, D), n=N)
```

```python
import functools
import math

import numpy as np
import jax
import jax.numpy as jnp
from jax import lax
from jax.experimental import pallas as pl
from jax.experimental.pallas import tpu as pltpu

F32 = jnp.float32
BF16 = jnp.bfloat16
HI = lax.Precision.HIGHEST

D = 1024
EPS = 1e-6
GRID_W = 64
ROPE_THETA = 10000.0
F_GROUPS, F_GDIM = 4, 64
F_WIDTH = F_GROUPS * F_GDIM
DA_HEADS, DA_DIM = 6, 32
DA_VDIM = 2 * DA_DIM
DA_WIDTH = DA_HEADS * DA_VDIM
M_HEADS, M_DIM = 4, 96
M_WIDTH = M_HEADS * M_DIM
M_PAD = 128
MP_WIDTH = M_HEADS * M_PAD
N_GATES = 4 * M_HEADS
N_EXPERTS = 16
EC_CAPACITY = 2
D_FF = 2 * D
ADA_CHUNKS = 6
ADA_ROWS = 8
CTX_ROW = 4

LANES = 128
MXU_DIM = 256
V7X_VMEM_BYTES = 64 * 1024 * 1024

TOK = MXU_DIM
FFT_N1 = 16
SLOT = LANES
Q_TILE = 512
KV_TILE_MAX = 8448
MOE_TOK = 512
COMBINE_TOK = 512
FFN_TF = 512
GATHER_EXPERTS = 8
NEG = -1e30

OFF_F = 0
OFF_DQ = OFF_F + F_WIDTH
OFF_MO = OFF_DQ + 2 * DA_HEADS * DA_DIM
OFF_MQ = OFF_MO + M_WIDTH
OFF_MK = OFF_MQ + M_WIDTH
OFF_DK = OFF_MK + M_WIDTH
OFF_DV = OFF_DK + 2 * DA_HEADS * DA_DIM
OFF_MV = OFF_DV + DA_HEADS * DA_VDIM
OFF_G = OFF_MV + M_WIDTH

VMEM_LIMIT = V7X_VMEM_BYTES * 7 // 8


def _cp(sem, vmem=None):
    return pltpu.CompilerParams(dimension_semantics=sem, vmem_limit_bytes=vmem)


def _sigmoid(x):
    return 1.0 / (1.0 + jnp.exp(-x))


def _silu(x):
    return x * _sigmoid(x)


def _dot(a, b, precision=None):
    return jnp.dot(a, b, preferred_element_type=F32, precision=precision)


def _split(a):
    hi = a.astype(BF16)
    return hi, (a - hi.astype(F32)).astype(BF16)


def _dot3(a, b):
    a_hi, a_lo = a if isinstance(a, tuple) else _split(a)
    b_hi, b_lo = b if isinstance(b, tuple) else _split(b)
    return _dot(a_hi, b_hi) + _dot(a_hi, b_lo) + _dot(a_lo, b_hi)


def _split3(a):
    hi = a.astype(BF16)
    r = a - hi.astype(F32)
    mid = r.astype(BF16)
    return hi, mid, (r - mid.astype(F32)).astype(BF16)


def _dot_nt(a, b, precision=None):
    return lax.dot_general(a, b, (((1,), (1,)), ((), ())), preferred_element_type=F32,
                           precision=precision)


def _ada_kernel(c_ref, w_ref, b_ref, o_ref):
    c = c_ref[...]
    o_ref[0] = _dot(_silu(c), w_ref[0], HI) + b_ref[0]


def _adaln(cvecs, ada_w, ada_b):
    depth = ada_w.shape[0]
    tn = 1536
    return pl.pallas_call(
        _ada_kernel,
        out_shape=jax.ShapeDtypeStruct((depth, ADA_ROWS, ADA_CHUNKS * D), F32),
        grid=(depth, ADA_CHUNKS * D // tn),
        in_specs=[pl.BlockSpec((ADA_ROWS, D), lambda l, j: (0, 0)),
                  pl.BlockSpec((1, D, tn), lambda l, j: (l, 0, j)),
                  pl.BlockSpec((1, 1, tn), lambda l, j: (l, 0, j))],
        out_specs=pl.BlockSpec((1, 8, tn), lambda l, j: (l, 0, j)),
        compiler_params=_cp(("arbitrary", "arbitrary")),
        name="adaln",
    )(cvecs, ada_w, ada_b.reshape(depth, 1, ADA_CHUNKS * D))


def _inproj_kernel(x_ref, xp_ref, xn_ref, ada_ref, g_ref, wm_ref, wc_ref, wvt_ref, wg_ref, gb_ref, cs_ref,
                   cos_ref, sin_ref, cw_ref, cb_ref,
                   y_ref, dq_ref, dk_ref, dvt_ref, mo_ref, mq_ref, mk_ref, mv_ref, gl_ref, glt_ref, *, n_lat):
    b = pl.program_id(0)
    t = pl.program_id(1)
    n_tiles = pl.num_programs(1)
    is_ctx = t >= n_lat
    row = jnp.where(is_ctx, CTX_ROW, b)
    mod = ada_ref[row]
    sh, sc = mod[0:1], mod[1:2]

    xa = jnp.concatenate([xp_ref[0], x_ref[0], xn_ref[0]], axis=0)
    r = lax.rsqrt(jnp.mean(xa * xa, axis=-1, keepdims=True) + EPS)
    ha = (xa * r) * g_ref[...] * (1.0 + sc) + sh
    h = ha[8:8 + TOK]
    hb = h.astype(BF16)

    pm = _dot(hb, wm_ref[...])
    o = 0
    pf = pm[:, o:o + F_WIDTH]; o += F_WIDTH
    q = pm[:, o:o + DA_WIDTH]; o += DA_WIDTH
    k = pm[:, o:o + DA_WIDTH]; o += DA_WIDTH
    pt = _dot_nt(wvt_ref[...], hb)
    dvt_ref[0] = pt[:DA_WIDTH].astype(BF16)
    mo_ref[0] = pt[DA_WIDTH:DA_WIDTH + MP_WIDTH].astype(BF16)
    mv_ref[0] = pt[DA_WIDTH + MP_WIDTH:].astype(BF16)

    y_ref[0, 0] = _dot3(pf, cs_ref[...])

    cos = cos_ref[...]
    sin = sin_ref[...]
    lane = lax.broadcasted_iota(jnp.int32, (1, 128), 1)
    low = (lane % 16) < 8

    def rope(z):
        parts = []
        for c in range(DA_WIDTH // 128):
            zc = z[:, 128 * c:128 * (c + 1)]
            rot = jnp.where(low, pltpu.roll(zc, 120, 1), pltpu.roll(zc, 8, 1))
            parts.append(zc * cos + rot * sin)
        return jnp.concatenate(parts, axis=1)

    dq_ref[0] = (rope(q) * (DA_DIM ** -0.5 * math.log2(math.e))).astype(BF16)
    dk_ref[0] = rope(k).astype(BF16)

    gpre = _dot3(h, wg_ref[...]) + gb_ref[...]
    is_forget = (lax.broadcasted_iota(jnp.int32, (1, LANES), 1) % 8) >= 4
    logsig = jnp.minimum(gpre, 0.0) - jnp.log(1.0 + jnp.exp(-jnp.abs(gpre)))
    gl = jnp.where(is_forget, logsig, gpre)
    gl_ref[0] = gl
    glt_ref[0] = gl.T[:N_GATES]

    pc = _dot(ha.astype(BF16), wc_ref[...])
    first = (t == 0) | (t == n_lat)
    last = (t == n_lat - 1) | (t == n_tiles - 1)
    ridx = lax.broadcasted_iota(jnp.int32, (TOK + 16, 1), 0)
    pc = jnp.where(((ridx < 8) & first) | ((ridx >= TOK + 8) & last), 0.0, pc)
    cw = cw_ref[...]
    conv = cb_ref[...] + pc[7:7 + TOK] * cw[0:1] + pc[8:8 + TOK] * cw[1:2] + pc[9:9 + TOK] * cw[2:3]
    act = _silu(conv)
    mq_ref[0] = act[:, :MP_WIDTH].astype(BF16)
    mk_ref[0] = (act[:, MP_WIDTH:] * (M_DIM ** -0.5)).astype(BF16)


def _inproj(xu, ada_l, g1, wm, wc, wvt, wg, gb, cs, cos_t, sin_t, cw, cb, *, n_lat, n2):
    B, NT, _ = xu.shape
    nt = n_lat + 1
    rper = n2 // TOK
    tok3 = lambda w: pl.BlockSpec((1, TOK, w), lambda b, t: (b, t, 0))
    full = lambda a: pl.BlockSpec(a.shape, lambda b, t: (0,) * a.ndim)
    nb8 = NT // 8
    outs = [jax.ShapeDtypeStruct((B, 2 * FFT_N1, n2, 2 * F_WIDTH), F32)]
    nq = -(-NT // Q_TILE) * Q_TILE
    outs += [jax.ShapeDtypeStruct((B, nq, DA_WIDTH), BF16), jax.ShapeDtypeStruct((B, nt * TOK, DA_WIDTH), BF16)]
    outs += [jax.ShapeDtypeStruct((B, DA_WIDTH, nt * TOK), BF16)]
    trs = lambda r: pl.BlockSpec((1, r, TOK), lambda b, t: (b, 0, t))
    trp = jax.ShapeDtypeStruct((B, MP_WIDTH, NT), BF16)
    outs += [trp, jax.ShapeDtypeStruct((B, NT, MP_WIDTH), BF16), jax.ShapeDtypeStruct((B, NT, MP_WIDTH), BF16), trp]
    outs += [jax.ShapeDtypeStruct((B, NT, 128), F32), jax.ShapeDtypeStruct((B, N_GATES, nt * TOK), F32)]
    out_specs = [pl.BlockSpec((1, 1, TOK, 2 * F_WIDTH), lambda b, t: (b, t // rper, t % rper, 0))]
    out_specs += [tok3(DA_WIDTH)] * 2 + [trs(DA_WIDTH)]
    out_specs += [trs(MP_WIDTH), tok3(MP_WIDTH), tok3(MP_WIDTH), trs(MP_WIDTH)]
    out_specs += [tok3(128), pl.BlockSpec((1, N_GATES, TOK), lambda b, t: (b, 0, t))]
    return pl.pallas_call(
        functools.partial(_inproj_kernel, n_lat=n_lat),
        out_shape=outs,
        grid=(B, nt),
        in_specs=[tok3(D),
                  pl.BlockSpec((1, 8, D), lambda b, t: (b, jnp.maximum(t * (TOK // 8) - 1, 0), 0)),
                  pl.BlockSpec((1, 8, D), lambda b, t: (b, jnp.minimum((t + 1) * (TOK // 8), nb8 - 1), 0)),
                  full(ada_l), full(g1), full(wm), full(wc), full(wvt), full(wg), full(gb), full(cs),
                  pl.BlockSpec((TOK, LANES), lambda b, t: (t, 0)),
                  pl.BlockSpec((TOK, LANES), lambda b, t: (t, 0)),
                  full(cw), full(cb)],
        out_specs=out_specs,
        compiler_params=_cp(("arbitrary", "arbitrary"), VMEM_LIMIT),
        name="norm1_inproj",
    )(xu, xu, xu, ada_l, g1, wm, wc, wvt, wg, gb, cs, cos_t, sin_t, cw, cb)


def _fft1_kernel(y_ref, kc_ref, ks_ref, tc_ref, ts_ref, o_ref, *, groups):
    kc = _split(kc_ref[...])
    ks = _split(ks_ref[...])
    for g in range(groups):
        blk = _split(y_ref[0, :, 8 * g:8 * (g + 1), :].reshape(FFT_N1 * 8, 2 * F_WIDTH))
        p = _dot3(kc, blk)
        q = _dot3(ks, blk)
        ar = p[:, :F_WIDTH] - q[:, F_WIDTH:]
        ai = -p[:, F_WIDTH:] - q[:, :F_WIDTH]
        tc = tc_ref[128 * g:128 * (g + 1), :]
        ts = ts_ref[128 * g:128 * (g + 1), :]
        tc = jnp.concatenate([tc, tc], axis=1)
        ts = jnp.concatenate([ts, ts], axis=1)
        br = ar * tc + ai * ts
        bi = ai * tc - ar * ts
        o_ref[0, :, 8 * g:8 * (g + 1), :] = jnp.concatenate([br, bi], axis=1).reshape(FFT_N1, 8, 2 * F_WIDTH)


def _fft2_kernel(b_ref, c2_ref, s2_ref, wb_ref, perm_ref, o_ref, r_scr, *, n2):
    c2 = _split(c2_ref[...])
    s2 = _split(s2_ref[...])
    for i in range(8):
        blk = b_ref[0, i]
        xr = _dot3(c2, blk[:, :F_WIDTH]) + _dot3(s2, blk[:, F_WIDTH:])
        r_scr[i] = _dot(xr.astype(BF16), wb_ref[...]).astype(BF16)
    for t in range(n2 // 32):
        rows = jnp.concatenate([r_scr[i, 32 * t:32 * (t + 1), :] for i in range(8)], axis=0)
        o_ref[0, 32 * t:32 * (t + 1), :, :] = _dot(perm_ref[...], rows).reshape(32, 8, F_WIDTH)


def _fftc_kernel(y_ref, c_ref, s_ref, wb_ref, o_ref):
    y = y_ref[0, 0]
    z = _dot3(c_ref[...], y[:, :F_WIDTH]) - _dot3(s_ref[...], y[:, F_WIDTH:])
    o_ref[0] = _dot(z.astype(BF16), wb_ref[...])


def _fourier_tables(n, ctx):
    n1, n2 = FFT_N1, n // FFT_N1
    a = np.arange(n1)
    ang1 = 2 * np.pi * np.outer(a, a) / n1
    eye8 = np.eye(8)
    kc = np.kron(np.cos(ang1), eye8)
    ks = np.kron(np.sin(ang1), eye8)
    n2i = np.arange(n2).reshape(n2 // 8, 1, 8)
    k1 = np.arange(n1).reshape(1, n1, 1)
    angt = (2 * np.pi * n2i * k1 / n).reshape(-1, 1)
    tc = np.broadcast_to(np.cos(angt), (n2 // 8 * 128, 128))
    ts = np.broadcast_to(np.sin(angt), (n2 // 8 * 128, 128))
    b = np.arange(n2)
    ang2 = 2 * np.pi * np.outer(b, b) / n2
    c2 = np.cos(ang2) / math.sqrt(n)
    s2 = np.sin(ang2) / math.sqrt(n)
    perm = np.zeros((256, 256))
    for kk in range(8):
        for j in range(32):
            perm[j * 8 + kk, kk * 32 + j] = 1.0
    cc = np.arange(ctx)
    angc = 2 * np.pi * np.outer(cc, cc) / ctx
    cctx = np.cos(angc) / math.sqrt(ctx)
    sctx = np.sin(angc) / math.sqrt(ctx)
    ch = np.arange(F_GDIM)
    angch = 2 * np.pi * np.outer(ch, ch) / F_GDIM
    cs = np.concatenate([np.kron(np.eye(F_GROUPS), np.cos(angch)),
                         np.kron(np.eye(F_GROUPS), np.sin(angch))], axis=1) / math.sqrt(F_GDIM)
    f = lambda z: jnp.asarray(np.ascontiguousarray(z), dtype=F32)
    return dict(kc=f(kc), ks=f(ks), tc=f(tc), ts=f(ts), c2=f(c2), s2=f(s2), perm=f(perm).astype(BF16),
                cctx=f(cctx), sctx=f(sctx), cs=f(cs))


def _fourier(y4, tabs, wblk, *, n, ctx, with_ctx):
    B = y4.shape[0]
    n2 = n // FFT_N1
    groups = 4
    full = lambda a, nd: pl.BlockSpec(a.shape, lambda *i: (0,) * a.ndim)
    b4 = pl.pallas_call(
        functools.partial(_fft1_kernel, groups=groups),
        out_shape=jax.ShapeDtypeStruct((B, FFT_N1, n2, 2 * F_WIDTH), F32),
        grid=(B, n2 // (8 * groups)),
        in_specs=[pl.BlockSpec((1, FFT_N1, 8 * groups, 2 * F_WIDTH), lambda b, j: (b, 0, j, 0)),
                  full(tabs["kc"], 2), full(tabs["ks"], 2),
                  pl.BlockSpec((128 * groups, 128), lambda b, j: (j, 0)),
                  pl.BlockSpec((128 * groups, 128), lambda b, j: (j, 0))],
        out_specs=pl.BlockSpec((1, FFT_N1, 8 * groups, 2 * F_WIDTH), lambda b, j: (b, 0, j, 0)),
        compiler_params=_cp(("arbitrary", "arbitrary")),
        name="fourier_stage1",
    )(y4, tabs["kc"], tabs["ks"], tabs["tc"], tabs["ts"])
    f4 = pl.pallas_call(
        functools.partial(_fft2_kernel, n2=n2),
        out_shape=jax.ShapeDtypeStruct((B, n2, 16, F_WIDTH), F32),
        grid=(B, FFT_N1 // 8),
        in_specs=[pl.BlockSpec((1, 8, n2, 2 * F_WIDTH), lambda b, j: (b, j, 0, 0)),
                  full(tabs["c2"], 2), full(tabs["s2"], 2), full(wblk, 2), full(tabs["perm"], 2)],
        out_specs=pl.BlockSpec((1, n2, 8, F_WIDTH), lambda b, j: (b, 0, j, 0)),
        scratch_shapes=[pltpu.VMEM((8, n2, F_WIDTH), BF16)],
        compiler_params=_cp(("arbitrary", "arbitrary"), VMEM_LIMIT),
        name="fourier_stage2",
    )(b4, tabs["c2"], tabs["s2"], wblk, tabs["perm"])
    f_ctx = None
    if with_ctx:
        f_ctx = pl.pallas_call(
            _fftc_kernel,
            out_shape=jax.ShapeDtypeStruct((B, ctx, F_WIDTH), F32),
            grid=(B,),
            in_specs=[pl.BlockSpec((1, 1, TOK, 2 * F_WIDTH), lambda b: (b, FFT_N1, 0, 0)),
                      full(tabs["cctx"], 1), full(tabs["sctx"], 1), full(wblk, 1)],
            out_specs=pl.BlockSpec((1, ctx, F_WIDTH), lambda b: (b, 0, 0)),
            compiler_params=_cp(("arbitrary",)),
            name="fourier_ctx",
        )(y4, tabs["cctx"], tabs["sctx"], wblk)
    return f4.reshape(B, n, F_WIDTH), f_ctx


VROWS = DA_VDIM + 16


def _attn_kernel(q_ref, k_ref, vt_ref, dl_ref, g_ref, o_ref, m_scr, acc_scr, *, lam_init):
    kt = pl.program_id(3)
    nk = pl.num_programs(3)

    @pl.when(kt == 0)
    def _():
        m_scr[...] = jnp.full(m_scr.shape, NEG, F32)
        acc_scr[...] = jnp.zeros(acc_scr.shape, F32)

    q = q_ref[0]
    k = k_ref[0]
    vt = vt_ref[0]
    ones = jnp.ones((16, vt.shape[1]), BF16)
    lhs = [jnp.concatenate([vt[DA_VDIM * h:DA_VDIM * (h + 1)], ones], axis=0) for h in range(2)]
    lane = lax.broadcasted_iota(jnp.int32, (1, LANES), 1)
    zero = jnp.zeros((), BF16)

    def scores(j):
        return _dot_nt(k, jnp.where((lane // DA_DIM) == j, q, zero))

    st_next = scores(0)
    for j in range(4):
        st = st_next
        if j < 3:
            st_next = scores(j + 1)
        m_old = m_scr[j]
        m_new = jnp.maximum(m_old, jnp.max(st, axis=0, keepdims=True))
        alpha = jnp.exp2(m_old - m_new)
        pt = jnp.exp2(st - m_new).astype(BF16)
        acc_scr[j] = alpha * acc_scr[j] + _dot(lhs[j // 2], pt)
        m_scr[j] = m_new

    @pl.when(kt == nk - 1)
    def _():
        dl = dl_ref[...]
        lam = (jnp.exp(jnp.sum(dl[0:1] * dl[1:2], keepdims=True))
               - jnp.exp(jnp.sum(dl[2:3] * dl[3:4], keepdims=True)) + lam_init)
        outs = []
        for h in range(2):
            a0 = acc_scr[2 * h]
            a1 = acc_scr[2 * h + 1]
            o = (a0[:DA_VDIM] / a0[DA_VDIM:DA_VDIM + 1]
                 - lam * (a1[:DA_VDIM] / a1[DA_VDIM:DA_VDIM + 1]))
            r = lax.rsqrt(jnp.mean(o * o, axis=0, keepdims=True) + EPS)
            outs.append(((o * r) * g_ref[...]) * (1.0 - lam_init))
        o_ref[0] = jnp.concatenate(outs, axis=0).astype(BF16)


def _attention(dq, dk, dvT, dlam, gcol, *, lam_init, tq, q0, nq, tk, k0, nk):
    B = dq.shape[0]
    return pl.pallas_call(
        functools.partial(_attn_kernel, lam_init=lam_init),
        out_shape=jax.ShapeDtypeStruct((B, DA_WIDTH, nq * tq), BF16),
        grid=(B, DA_WIDTH // 128, nq, nk),
        in_specs=[pl.BlockSpec((1, tq, 128), lambda b, p, i, j: (b, q0 + i, p)),
                  pl.BlockSpec((1, tk, 128), lambda b, p, i, j: (b, k0 + j, p)),
                  pl.BlockSpec((1, 128, tk), lambda b, p, i, j: (b, p, k0 + j)),
                  pl.BlockSpec(dlam.shape, lambda b, p, i, j: (0, 0)),
                  pl.BlockSpec(gcol.shape, lambda b, p, i, j: (0, 0))],
        out_specs=pl.BlockSpec((1, 128, tq), lambda b, p, i, j: (b, p, i)),
        scratch_shapes=[pltpu.VMEM((4, 1, tq), F32), pltpu.VMEM((4, VROWS, tq), F32)],
        compiler_params=_cp(("arbitrary",) * 4, VMEM_LIMIT),
        name="diff_attention",
    )(dq, dk, dvT, dlam, gcol)


def _mlstm_kernel(qf_ref, kf_ref, vf_ref, gcf_ref, grf_ref, qb_ref, kb_ref, vb_ref, gcb_ref, grb_ref,
                  hf_ref, hb_ref, c_scr, m_scr):
    t = pl.program_id(1)

    @pl.when(t == 0)
    def _():
        c_scr[...] = jnp.zeros(c_scr.shape, F32)
        m_scr[...] = jnp.zeros(m_scr.shape, F32)

    L = TOK
    si = lax.broadcasted_iota(jnp.int32, (L, L), 0)
    li = lax.broadcasted_iota(jnp.int32, (L, L), 1)
    dirs = ((qf_ref, kf_ref, vf_ref, gcf_ref, grf_ref, hf_ref, si <= li, li <= si, L - 1),
            (qb_ref, kb_ref, vb_ref, gcb_ref, grb_ref, hb_ref, si >= li, li >= si, 0))
    ones = jnp.ones((16, L), F32)
    for d, (q_ref, k_ref, vt_ref, gc_ref, gr_ref, h_ref, seen, seen_t, last) in enumerate(dirs):
        gc = gc_ref[0]
        gr = gr_ref[0]
        seen_b = jnp.where(seen, 1.0, 0.0).astype(BF16)
        seen_tb = jnp.where(seen_t, 1.0, 0.0).astype(BF16)
        bcols = sum(_dot(seen_tb, piece) for piece in _split3(gc))
        brows = sum(_dot(piece, seen_b) for piece in _split3(gr))
        for hd in range(M_HEADS):
            idx = d * M_HEADS + hd
            ji = d * 8 + hd
            jf = d * 8 + 4 + hd
            sl = slice(M_PAD * hd, M_PAD * (hd + 1))
            q = q_ref[0, :, sl]
            k = k_ref[0, :, sl]
            vt = vt_ref[0, sl, :]
            b_row = brows[jf:jf + 1, :]
            cs = gc[:, ji:ji + 1] - bcols[:, jf:jf + 1]
            li_row = gr[ji:ji + 1, :]
            m_old = m_scr[idx][0:1, 0:1]
            c_old = c_scr[idx]

            dlog = jnp.where(seen, b_row + cs, NEG)
            inter = b_row + m_old
            m_t = jnp.maximum(inter, jnp.max(dlog, axis=0, keepdims=True))
            w_inter = jnp.exp(inter - m_t)
            st = _dot_nt(k, q) * jnp.exp(dlog - m_t)
            cq = _dot_nt(c_old.astype(BF16), q)
            num = w_inter * cq[:M_PAD] + _dot(vt, st.astype(BF16))
            den = w_inter * cq[M_PAD:M_PAD + 1] + jnp.sum(st, axis=0, keepdims=True)
            h_ref[0, sl, :] = num / jnp.maximum(jnp.abs(den), jnp.exp(-m_t))

            total = b_row[:, last:last + 1]
            wlog = total - b_row + li_row
            m_new = jnp.maximum(total + m_old, jnp.max(wlog, axis=1, keepdims=True))
            decay = jnp.exp(total + m_old - m_new)
            w = jnp.exp(wlog - m_new)
            vw = jnp.concatenate([vt.astype(F32) * w, ones * w], axis=0).astype(BF16)
            c_scr[idx] = decay * c_old + _dot(vw, k)
            m_scr[idx] = jnp.broadcast_to(m_new, (8, 128))


def _mlstm(mq, mk, mvT, gl, glT, *, n_lat):
    B, NT, _ = mq.shape
    nt = n_lat + 1
    fwd = lambda t: jnp.where(t == 0, n_lat, t - 1)
    bwd = lambda t: jnp.where(t == 0, n_lat, n_lat - t)
    tok = lambda w, f: pl.BlockSpec((1, TOK, w), lambda b, t: (b, f(t), 0))
    lanes = lambda r, f: pl.BlockSpec((1, r, TOK), lambda b, t: (b, 0, f(t)))
    ins, specs = [], []
    for f in (fwd, bwd):
        ins += [mq, mk, mvT, gl, glT]
        specs += [tok(MP_WIDTH, f)] * 2 + [lanes(MP_WIDTH, f), tok(128, f), lanes(N_GATES, f)]
    return pl.pallas_call(
        _mlstm_kernel,
        out_shape=[jax.ShapeDtypeStruct((B, MP_WIDTH, NT), F32)] * 2,
        grid=(B, nt),
        in_specs=specs,
        out_specs=[lanes(MP_WIDTH, fwd), lanes(MP_WIDTH, bwd)],
        scratch_shapes=[pltpu.VMEM((2 * M_HEADS, M_PAD + 16, M_PAD), F32),
                        pltpu.VMEM((2 * M_HEADS, 8, 128), F32)],
        compiler_params=_cp(("arbitrary", "arbitrary"), VMEM_LIMIT),
        name="mlstm",
    )(*ins)


def _outproj_kernel(x_ref, f_ref, dat_ref, hf_ref, hb_ref, mo_ref, ada_ref, mg_ref, wo_ref, wod_ref, g2_ref, wr_ref,
                    xo_ref, hl_ref, pt_ref, *, is_ctx):
    b = pl.program_id(0)
    mod = ada_ref[CTX_ROW if is_ctx else b]
    gt1, sh2, sc2 = mod[2:3], mod[3:4], mod[4:5]
    mg = mg_ref[...]
    for s in range(x_ref.shape[1] // TOK):
        tk = slice(TOK * s, TOK * (s + 1))
        hs = hf_ref[0, :, tk] + hb_ref[0, :, tk]
        og = mo_ref[0, :, tk].astype(F32)
        parts = [dat_ref[0, :, tk]]
        for hd in range(M_HEADS):
            sl = slice(M_PAD * hd, M_PAD * (hd + 1))
            hh = hs[sl]
            r = lax.rsqrt(jnp.sum(hh * hh, axis=0, keepdims=True) * (1.0 / M_DIM) + EPS)
            parts.append((((hh * r) * mg[sl]) * _sigmoid(og[sl])).astype(BF16))
        mix_t = jnp.concatenate(parts, axis=0)
        upd = _dot(f_ref[0, tk].astype(BF16), wo_ref[...]) + lax.dot_general(
            mix_t, wod_ref[...], (((0,), (0,)), ((), ())), preferred_element_type=F32)
        xn = x_ref[0, tk] + gt1 * upd
        xo_ref[0, tk] = xn
        r = lax.rsqrt(jnp.mean(xn * xn, axis=-1, keepdims=True) + EPS)
        h2 = (xn * r) * g2_ref[...] * (1.0 + sc2) + sh2
        hl_ref[0, tk] = h2.astype(BF16)
        lt = _dot3(h2, wr_ref[...]).T[:N_EXPERTS]
        ex = jnp.exp(lt - jnp.max(lt, axis=0, keepdims=True))
        pt_ref[0, :, tk] = ex / jnp.sum(ex, axis=0, keepdims=True)


def _outproj_kernel_aliased(x_ref, f_ref, dat_ref, hf_ref, hb_ref, mo_ref, ada_ref, mg_ref, wo_ref, wod_ref,
                            g2_ref, wr_ref, hlp_ref, xo_ref, hl_ref, pt_ref, *, is_ctx):
    del hlp_ref
    _outproj_kernel(x_ref, f_ref, dat_ref, hf_ref, hb_ref, mo_ref, ada_ref, mg_ref, wo_ref, wod_ref, g2_ref,
                    wr_ref, xo_ref, hl_ref, pt_ref, is_ctx=is_ctx)


def _outproj(xu, f, daT, hf, hb, mo, ada_l, mg, wo, wod, g2, wrp, hl_prev, *, t0, ntl, is_ctx):
    B, NT, _ = xu.shape
    n = ntl * TOK
    tile = 2 * TOK if n % (2 * TOK) == 0 else TOK
    o = t0 * TOK // tile
    tok = lambda w: pl.BlockSpec((1, tile, w), lambda b, t: (b, o + t, 0))
    trs = lambda r: pl.BlockSpec((1, r, tile), lambda b, t: (b, 0, o + t))
    loc = lambda w: pl.BlockSpec((1, tile, w), lambda b, t: (b, t, 0))
    full = lambda a: pl.BlockSpec(a.shape, lambda b, t: (0,) * a.ndim)
    return pl.pallas_call(
        functools.partial(_outproj_kernel_aliased, is_ctx=is_ctx),
        out_shape=[jax.ShapeDtypeStruct(xu.shape, F32), jax.ShapeDtypeStruct((B, NT, D), BF16),
                   jax.ShapeDtypeStruct((B, N_EXPERTS, n), F32)],
        grid=(B, n // tile),
        in_specs=[tok(D), loc(F_WIDTH), pl.BlockSpec((1, DA_WIDTH, tile), lambda b, t: (b, 0, t)),
                  trs(MP_WIDTH), trs(MP_WIDTH), trs(MP_WIDTH),
                  full(ada_l), full(mg), full(wo), full(wod), full(g2), full(wrp),
                  pl.BlockSpec(memory_space=pl.ANY)],
        out_specs=[tok(D), tok(D), pl.BlockSpec((1, N_EXPERTS, tile), lambda b, t: (b, 0, t))],
        input_output_aliases={0: 0, 12: 1},
        compiler_params=_cp(("arbitrary", "arbitrary"), VMEM_LIMIT),
        name="outproj_norm2_router",
    )(xu, f, daT, hf, hb, mo, ada_l, mg, wo, wod, g2, wrp, hl_prev)


def _select_kernel(p_ref, rank_ref, offs_ref, *, n, cap):
    p = p_ref[0]
    xi = pltpu.bitcast(p, jnp.int32)

    def body(i, lo):
        cand = lo | jnp.left_shift(jnp.int32(1), 30 - i)
        cnt = jnp.sum(jnp.where(xi >= cand, 1.0, 0.0), axis=1, keepdims=True)
        return jnp.where(cnt >= cap, cand, lo)

    thr = lax.fori_loop(0, 31, body, jnp.zeros((N_EXPERTS, 1), jnp.int32))
    nb = n // TOK
    rows = lax.broadcasted_iota(jnp.int32, (n, 128), 0)
    cols = lax.broadcasted_iota(jnp.int32, (n, 128), 1)
    blk_ind = jnp.where((rows // TOK) == cols, 1.0, 0.0).astype(BF16)
    u128 = jnp.where(lax.broadcasted_iota(jnp.int32, (128, 128), 0)
                     < lax.broadcasted_iota(jnp.int32, (128, 128), 1), 1.0, 0.0).astype(BF16)
    utok = jnp.where(lax.broadcasted_iota(jnp.int32, (TOK, TOK), 0)
                     < lax.broadcasted_iota(jnp.int32, (TOK, TOK), 1), 1.0, 0.0).astype(BF16)

    def prefix(mf):
        mb = mf.astype(BF16)
        counts = _dot(mb, blk_ind)
        offs = _dot(counts.astype(BF16), u128)
        pieces = [_dot(mb[:, TOK * j:TOK * (j + 1)], utok) + offs[:, j:j + 1] for j in range(nb)]
        return (jnp.concatenate(pieces, axis=1) if nb > 1 else pieces[0]), offs

    gt = xi > thr
    eq = xi == thr
    need = cap - jnp.sum(jnp.where(gt, 1.0, 0.0), axis=1, keepdims=True)
    rank_eq, _ = prefix(jnp.where(eq, 1.0, 0.0))
    sel = gt | (eq & (rank_eq < need))
    rank, offs = prefix(jnp.where(sel, 1.0, 0.0))
    rank_ref[0] = jnp.where(sel, rank, -1.0)
    offs_ref[0] = offs.astype(jnp.int32)


def _select(pt, *, cap):
    B, _, n = pt.shape
    return pl.pallas_call(
        functools.partial(_select_kernel, n=n, cap=cap),
        out_shape=[jax.ShapeDtypeStruct((B, N_EXPERTS, n), F32),
                   jax.ShapeDtypeStruct((B, N_EXPERTS, 128), jnp.int32)],
        grid=(B,),
        in_specs=[pl.BlockSpec((1, N_EXPERTS, n), lambda b: (b, 0, 0))],
        out_specs=[pl.BlockSpec((1, N_EXPERTS, n), lambda b: (b, 0, 0)),
                   pl.BlockSpec((1, N_EXPERTS, 128), lambda b: (b, 0, 0))],
        compiler_params=_cp(("arbitrary",), VMEM_LIMIT),
        name="expert_choice_select",
    )(pt)


def _gather_kernel(offs_ref, h_ref, rank_ref, prob_ref, o_ref, gate_ref, *, eg, per):
    b, g, tb = pl.program_id(0), pl.program_id(1), pl.program_id(2)

    @pl.when(tb == 0)
    def _():
        o_ref[...] = jnp.zeros(o_ref.shape, BF16)
        gate_ref[...] = jnp.zeros(gate_ref.shape, F32)

    cap_pad = o_ref.shape[2]
    half = SLOT // 2
    slot = lax.broadcasted_iota(jnp.int32, (SLOT, TOK), 0).astype(F32)

    def add_rows(i, r, p, h, base, start=None):
        hit = r == slot + base.astype(F32)
        if start is not None:
            hit = hit & (r >= start.astype(F32))
        rows = _dot(jnp.where(hit, 1.0, 0.0).astype(BF16), h).astype(BF16)
        o_ref[0, i, pl.ds(base, SLOT), :] = o_ref[0, i, pl.ds(base, SLOT), :] + rows
        gate_ref[0, i, pl.ds(base, SLOT), :] = (gate_ref[0, i, pl.ds(base, SLOT), :]
                                                + jnp.sum(jnp.where(hit, p, 0.0), axis=1, keepdims=True))

    def operands(s, i):
        tk = slice(TOK * s, TOK * (s + 1))
        e = g * eg + i
        return rank_ref[0, pl.ds(e, 1), tk], prob_ref[0, pl.ds(e, 1), tk], h_ref[0, tk, :]

    ends, his = {}, {}
    for s in range(per):
        for i in range(eg):
            e = g * eg + i
            lo = offs_ref[b, e, tb * per + s]
            his[s, i] = offs_ref[b, e, tb * per + s + 1]
            base = pl.multiple_of(jnp.minimum((lo // half) * half, cap_pad - SLOT), half)
            add_rows(i, *operands(s, i), base)
            ends[s, i] = base + SLOT

    for s in range(per):
        for i in range(eg):
            @pl.when(his[s, i] > ends[s, i])
            def _(s=s, i=i):
                def body(t, carry):
                    start = ends[s, i] + t * SLOT
                    base = pl.multiple_of(jnp.minimum(start, cap_pad - SLOT), half)
                    add_rows(i, *operands(s, i), base, start)
                    return carry

                lax.fori_loop(0, (his[s, i] - ends[s, i] + SLOT - 1) // SLOT, body, 0)


def _gather(offs, hl, rank, pt, *, tb_tok, tb0, n, cap_pad, eg):
    B = hl.shape[0]
    per = tb_tok // TOK
    return pl.pallas_call(
        functools.partial(_gather_kernel, eg=eg, per=per),
        out_shape=[jax.ShapeDtypeStruct((B, N_EXPERTS, cap_pad, D), BF16),
                   jax.ShapeDtypeStruct((B, N_EXPERTS, cap_pad, 1), F32)],
        grid_spec=pltpu.PrefetchScalarGridSpec(
            num_scalar_prefetch=1,
            grid=(B, N_EXPERTS // eg, n // tb_tok),
            in_specs=[pl.BlockSpec((1, tb_tok, D), lambda b, g, t, o: (b, tb0 + t, 0)),
                      pl.BlockSpec((1, N_EXPERTS, tb_tok), lambda b, g, t, o: (b, 0, t)),
                      pl.BlockSpec((1, N_EXPERTS, tb_tok), lambda b, g, t, o: (b, 0, t))],
            out_specs=[pl.BlockSpec((1, eg, cap_pad, D), lambda b, g, t, o: (b, g, 0, 0)),
                       pl.BlockSpec((1, eg, cap_pad, 1), lambda b, g, t, o: (b, g, 0, 0))]),
        compiler_params=_cp(("arbitrary",) * 3, VMEM_LIMIT),
        name="expert_gather",
    )(offs, hl, rank, pt)


FFN_ROWS = 1024


def _ffn_kernel(x_ref, gate_ref, w1_ref, w3_ref, w2_ref, y_ref, acc_ref):
    f = pl.program_id(2)

    @pl.when(f == 0)
    def _():
        acc_ref[...] = jnp.zeros(acc_ref.shape, F32)

    w1 = w1_ref[0, 0].astype(BF16)
    w3 = w3_ref[0, 0].astype(BF16)
    w2 = w2_ref[0, 0].astype(BF16)
    mb, _, cap_pad, _ = x_ref.shape
    rows = min(FFN_ROWS, cap_pad)
    for i in range(mb):
        for r in range(0, cap_pad, rows):
            x = x_ref[i, 0, r:r + rows, :]
            hid = (_silu(_dot(x, w1)) * _dot(x, w3)).astype(BF16)
            acc_ref[i * cap_pad + r:i * cap_pad + r + rows, :] += _dot(hid, w2)

    @pl.when(f == pl.num_programs(2) - 1)
    def _():
        gate = gate_ref[...].reshape(-1, 1)
        y_ref[...] = (acc_ref[...] * gate).astype(BF16).reshape(y_ref.shape)


def _ffn(xs, gates, w1, w3, w2, *, layer, mb, tf):
    B, E, cap_pad, _ = xs.shape
    return pl.pallas_call(
        _ffn_kernel,
        out_shape=jax.ShapeDtypeStruct(xs.shape, BF16),
        grid=(E, B // mb, D_FF // tf),
        in_specs=[pl.BlockSpec((mb, 1, cap_pad, D), lambda e, m, f: (m, e, 0, 0)),
                  pl.BlockSpec((mb, 1, cap_pad, 1), lambda e, m, f: (m, e, 0, 0)),
                  pl.BlockSpec((1, 1, D, tf), lambda e, m, f: (layer, e, 0, f)),
                  pl.BlockSpec((1, 1, D, tf), lambda e, m, f: (layer, e, 0, f)),
                  pl.BlockSpec((1, 1, tf, D), lambda e, m, f: (layer, e, f, 0))],
        out_specs=pl.BlockSpec((mb, 1, cap_pad, D), lambda e, m, f: (m, e, 0, 0)),
        scratch_shapes=[pltpu.VMEM((mb * cap_pad, D), F32)],
        compiler_params=_cp(("arbitrary",) * 3, VMEM_LIMIT),
        name="expert_ffn",
    )(xs, gates, w1, w3, w2)


CCOL = 512


def _combine_kernel(offs_ref, x_ref, y_ref, rankc_ref, ada_ref, o_ref, tot_scr, *, per, is_ctx):
    b, tb = pl.program_id(0), pl.program_id(2)
    gt2 = ada_ref[CTX_ROW if is_ctx else b][5:6]
    rc_all = rankc_ref[0]
    cap_pad = y_ref.shape[2]
    half = SLOT // 2
    slot = lax.broadcasted_iota(jnp.int32, (1, SLOT), 1).astype(F32)

    ends, his = {}, {}
    for s in range(per):
        tk = slice(TOK * s, TOK * (s + 1))
        total = jnp.zeros((TOK, tot_scr.shape[1]), F32)
        for e0 in range(0, N_EXPERTS, 2):
            hots, rows = [], []
            for e in (e0, e0 + 1):
                lo = offs_ref[b, e, tb * per + s]
                his[s, e] = offs_ref[b, e, tb * per + s + 1]
                base = pl.multiple_of(jnp.minimum((lo // half) * half, cap_pad - SLOT), half)
                ends[s, e] = base + SLOT
                hots.append(jnp.where(rc_all[tk, e:e + 1] == slot + base.astype(F32), 1.0, 0.0).astype(BF16))
                rows.append(y_ref[0, e, pl.ds(base, SLOT), :])
            total = total + _dot(jnp.concatenate(hots, axis=1), jnp.concatenate(rows, axis=0))
        tot_scr[tk] = total

    for s in range(per):
        tk = slice(TOK * s, TOK * (s + 1))
        for e in range(N_EXPERTS):
            @pl.when(his[s, e] > ends[s, e])
            def _(s=s, e=e, tk=tk):
                rc = rc_all[tk, e:e + 1]

                def body(t, carry):
                    start = ends[s, e] + t * SLOT
                    base = pl.multiple_of(jnp.minimum(start, cap_pad - SLOT), half)
                    hit = (rc == slot + base.astype(F32)) & (rc >= start.astype(F32))
                    tot_scr[tk] += _dot(jnp.where(hit, 1.0, 0.0).astype(BF16), y_ref[0, e, pl.ds(base, SLOT), :])
                    return carry

                lax.fori_loop(0, (his[s, e] - ends[s, e] + SLOT - 1) // SLOT, body, 0)

    o_ref[0] = x_ref[0] + gt2 * tot_scr[...]


def _combine(offs, xu, ys, rank_c, ada_l, *, tb_tok, tb0, n, is_ctx):
    B = xu.shape[0]
    cap_pad = ys.shape[2]
    per = tb_tok // TOK
    return pl.pallas_call(
        functools.partial(_combine_kernel, per=per, is_ctx=is_ctx),
        out_shape=jax.ShapeDtypeStruct(xu.shape, F32),
        grid_spec=pltpu.PrefetchScalarGridSpec(
            num_scalar_prefetch=1,
            grid=(B, D // CCOL, n // tb_tok),
            in_specs=[pl.BlockSpec((1, tb_tok, CCOL), lambda b, c, t, o: (b, tb0 + t, c)),
                      pl.BlockSpec((1, N_EXPERTS, cap_pad, CCOL), lambda b, c, t, o: (b, 0, 0, c),
                                   pipeline_mode=pl.Buffered(1)),
                      pl.BlockSpec((1, tb_tok, N_EXPERTS), lambda b, c, t, o: (b, t, 0)),
                      pl.BlockSpec((ADA_ROWS, ADA_CHUNKS, CCOL), lambda b, c, t, o: (0, 0, c))],
            out_specs=pl.BlockSpec((1, tb_tok, CCOL), lambda b, c, t, o: (b, tb0 + t, c)),
            scratch_shapes=[pltpu.VMEM((tb_tok, CCOL), F32)]),
        input_output_aliases={1: 0},
        compiler_params=_cp(("arbitrary",) * 3, VMEM_LIMIT),
        name="expert_combine",
    )(offs, xu, ys, rank_c, ada_l)


def _moe(xu, hl, pt, ada_l, w1, w3, w2, *, layer, row0, is_ctx):
    B, _, n = pt.shape
    cap = EC_CAPACITY * n // N_EXPERTS
    cap_pad = -(-cap // SLOT) * SLOT
    nb = n // TOK
    rank, offs = _select(pt, cap=cap)
    offs = offs[:, :, :nb + 1]
    gt = min(n, MOE_TOK)
    ct = min(n, COMBINE_TOK)
    xs, gates = _gather(offs, hl, rank, pt, tb_tok=gt, tb0=row0 // gt, n=n, cap_pad=cap_pad, eg=GATHER_EXPERTS)
    mb = 2 if (B % 2 == 0 and cap_pad >= 1024) else (B if cap_pad < 1024 else 1)
    ys = _ffn(xs, gates, w1, w3, w2, layer=layer, mb=mb, tf=FFN_TF if cap_pad >= 1024 else 2 * FFN_TF)
    rank_c = jnp.swapaxes(rank, 1, 2)
    return _combine(offs, xu, ys, rank_c, ada_l, tb_tok=ct, tb0=row0 // ct, n=n, is_ctx=is_ctx)


def _final_kernel(x_ref, g_ref, o_ref):
    x = x_ref[0]
    r = lax.rsqrt(jnp.mean(x * x, axis=-1, keepdims=True) + EPS)
    o_ref[0] = (x * r) * g_ref[...]


def _final_norm(xu, g, *, n):
    B = xu.shape[0]
    tm = MOE_TOK
    return pl.pallas_call(
        _final_kernel,
        out_shape=jax.ShapeDtypeStruct((B, n, D), F32),
        grid=(B, n // tm),
        in_specs=[pl.BlockSpec((1, tm, D), lambda b, t: (b, t, 0)),
                  pl.BlockSpec((1, D), lambda b, t: (0, 0))],
        out_specs=pl.BlockSpec((1, tm, D), lambda b, t: (b, t, 0)),
        compiler_params=_cp(("arbitrary", "arbitrary")),
        name="final_norm",
    )(xu, g)


def _rope_tables(n, ctx):
    rows = n // GRID_W
    t_row = jnp.repeat(jnp.arange(rows), GRID_W)
    t_col = jnp.tile(jnp.arange(GRID_W), rows)
    nf = DA_DIM // 4
    inv = ROPE_THETA ** (-jnp.arange(nf, dtype=F32) / nf)
    ar = t_row[:, None].astype(F32) * inv
    ac = t_col[:, None].astype(F32) * inv
    ang = jnp.concatenate([ar, ar, ac, ac], axis=-1)
    sign = jnp.where((jnp.arange(DA_DIM) % 16) < 8, -1.0, 1.0).astype(F32)
    cos = jnp.concatenate([jnp.cos(ang), jnp.ones((ctx, DA_DIM), F32)], axis=0)
    sin = jnp.concatenate([jnp.sin(ang) * sign, jnp.zeros((ctx, DA_DIM), F32)], axis=0)
    return jnp.tile(cos, (1, 128 // DA_DIM)), jnp.tile(sin, (1, 128 // DA_DIM))


def _pad_heads_cols(w):
    lead = w.shape[:-1]
    w = w.reshape(lead + (M_HEADS, M_DIM))
    w = jnp.pad(w, [(0, 0)] * len(lead) + [(0, 0), (0, M_PAD - M_DIM)])
    return w.reshape(lead + (MP_WIDTH,))


def _kv_tile(nt):
    for parts in range(1, nt // LANES + 1):
        if nt % parts == 0 and (nt // parts) % LANES == 0 and nt // parts <= KV_TILE_MAX:
            return nt // parts
    raise ValueError(nt)


def kernel(x, c, ctx, c_ctx, ada_w, ada_b, norm1_g, norm2_g, w_in, four_w, m_conv_w, m_conv_b, m_gate_b,
           m_norm_g, d_lam, d_norm_g, w_out, router_w, exp_w1, exp_w3, exp_w2, final_g):
    B, N, _ = x.shape
    CTX = ctx.shape[1]
    depth = w_in.shape[0]
    assert CTX == TOK and N % (FFT_N1 * TOK) == 0 and N % Q_TILE == 0 and B <= CTX_ROW
    NT = N + CTX
    PAD = -NT % MOE_TOK
    n_lat = N // TOK
    n2 = N // FFT_N1

    xu = jnp.concatenate([x, ctx, jnp.zeros((B, PAD, D), F32)], axis=1)
    cvecs = jnp.zeros((ADA_ROWS, D), F32).at[:B].set(c).at[CTX_ROW].set(c_ctx)
    ada = _adaln(cvecs, ada_w, ada_b).reshape(depth, ADA_ROWS, ADA_CHUNKS, D)
    cos_t, sin_t = _rope_tables(N, CTX + PAD)
    tabs = _fourier_tables(N, CTX)
    tk = _kv_tile(NT)
    tq = Q_TILE

    hl = jnp.zeros((B, NT + PAD, D), BF16)
    for layer in range(depth):
        ctx_out = layer < depth - 1
        lam_init = 0.8 - 0.6 * math.exp(-0.3 * layer)
        w = w_in[layer]
        wm = jnp.concatenate([w[:, OFF_F:OFF_DQ], w[:, OFF_DQ:OFF_MO], w[:, OFF_DK:OFF_DV]], axis=1).astype(BF16)
        wvt = jnp.concatenate([w[:, OFF_DV:OFF_MV], _pad_heads_cols(w[:, OFF_MO:OFF_MQ]),
                               _pad_heads_cols(w[:, OFF_MV:OFF_G])], axis=1).T.astype(BF16)
        wg = jnp.pad(w[:, OFF_G:], ((0, 0), (0, LANES - N_GATES)))
        wc = jnp.concatenate([_pad_heads_cols(w[:, OFF_MQ:OFF_MK]), _pad_heads_cols(w[:, OFF_MK:OFF_DK])],
                             axis=1).astype(BF16)
        gb = jnp.pad(m_gate_b[layer], (0, LANES - N_GATES)).reshape(1, LANES)
        cw = jnp.concatenate([_pad_heads_cols(m_conv_w[layer][:, :M_WIDTH]),
                              _pad_heads_cols(m_conv_w[layer][:, M_WIDTH:])], axis=1)
        cb = jnp.concatenate([_pad_heads_cols(m_conv_b[layer][:M_WIDTH]),
                              _pad_heads_cols(m_conv_b[layer][M_WIDTH:])]).reshape(1, 2 * MP_WIDTH)
        ada_l = ada[layer]

        y4, dq, dk, dvT, mo, mq, mk, mv, gl, glT = _inproj(
            xu, ada_l, norm1_g[layer].reshape(1, D), wm, wc, wvt, wg, gb, tabs["cs"], cos_t, sin_t, cw, cb,
            n_lat=n_lat, n2=n2)

        wblk = jnp.zeros((F_WIDTH, F_WIDTH), F32)
        for g in range(F_GROUPS):
            wblk = wblk.at[F_GDIM * g:F_GDIM * (g + 1), F_GDIM * g:F_GDIM * (g + 1)].set(four_w[layer, g])
        f_l, f_c = _fourier(y4, tabs, wblk.astype(BF16), n=N, ctx=CTX, with_ctx=ctx_out)

        dlam = d_lam[layer]
        g2 = d_norm_g[layer].reshape(DA_VDIM, 1)
        da_l = _attention(dq, dk, dvT, dlam, g2, lam_init=lam_init, tq=tq, q0=0, nq=N // tq,
                          tk=tk, k0=0, nk=NT // tk)

        hf, hb = _mlstm(mq, mk, mv, gl, glT, n_lat=n_lat)

        mg = _pad_heads_cols(m_norm_g[layer]).reshape(MP_WIDTH, 1)
        wol = w_out[layer]
        wo = wol[:F_WIDTH].astype(BF16)
        wod = jnp.concatenate([wol[F_WIDTH:F_WIDTH + DA_WIDTH],
                               jnp.pad(wol[F_WIDTH + DA_WIDTH:].reshape(M_HEADS, M_DIM, D),
                                       ((0, 0), (0, M_PAD - M_DIM), (0, 0))).reshape(MP_WIDTH, D)],
                              axis=0).astype(BF16)
        g2n = norm2_g[layer].reshape(1, D)
        wrp = jnp.pad(router_w[layer], ((0, 0), (0, LANES - N_EXPERTS)))
        xu, hl, pt_l = _outproj(xu, f_l, da_l, hf, hb, mo, ada_l, mg, wo, wod, g2n, wrp, hl,
                                t0=0, ntl=n_lat, is_ctx=False)
        if ctx_out:
            da_c = _attention(dq, dk, dvT, dlam, g2, lam_init=lam_init, tq=TOK, q0=n_lat, nq=1,
                              tk=TOK, k0=n_lat, nk=1)
            xu, hl, pt_c = _outproj(xu, f_c, da_c, hf, hb, mo, ada_l, mg, wo, wod, g2n, wrp, hl,
                                    t0=n_lat, ntl=1, is_ctx=True)

        xu = _moe(xu, hl, pt_l, ada_l, exp_w1, exp_w3, exp_w2, layer=layer, row0=0, is_ctx=False)
        if ctx_out:
            xu = _moe(xu, hl, pt_c, ada_l, exp_w1, exp_w3, exp_w2, layer=layer, row0=N, is_ctx=True)

    return _final_norm(xu, final_g.reshape(1, D), n=N)
```
